```python
import math
import jax, jax.numpy as jnp
from jax import lax

D_MODEL = 1024
BATCH = 2
SEQ = 8192
DEPTH = 2
DEC_BATCH = 128
DEC_SEQ = 1
PAST_LEN = 8192
PAGE_SIZE = 128

F32 = jnp.float32
EPS = 1e-6
N_META = 16
CHUNK = 64
A_HEADS = 8
A_KV_HEADS = 2
A_HEAD_DIM = 64
A_GROUP = A_HEADS // A_KV_HEADS
A_WIDTH = A_HEADS * A_HEAD_DIM
A_KV_WIDTH = A_KV_HEADS * A_HEAD_DIM
WINDOW = 128
A_BLOCK = 128
B_HEADS = 4
B_DK = 128
B_DV = 128
B_KEY = B_HEADS * B_DK
B_VAL = B_HEADS * B_DV
CONV_W = 4
B_CONV_CH = 2 * B_KEY + B_VAL
C_HEADS = 4
C_KEY = D_MODEL // 2
C_VAL = D_MODEL
C_DK = C_KEY // C_HEADS
C_DV = C_VAL // C_HEADS
C_GATE_RANK = 16
C_GATE_NORM = 16.0
N_AB_LAYERS = (DEPTH + 1) // 2
N_C_LAYERS = DEPTH // 2
WIN_BUF = min(WINDOW, PAST_LEN)
LEAD_PROMPT = (-N_META) % CHUNK
AB_SIZES = (A_WIDTH, A_KV_WIDTH, A_KV_WIDTH, A_WIDTH, B_CONV_CH, B_VAL, B_HEADS, B_HEADS)
AB_IN = A_WIDTH + 2 * A_KV_WIDTH + A_WIDTH + B_CONV_CH + B_VAL + 2 * B_HEADS
AB_MIX = A_WIDTH + B_VAL
C_SIZES = (C_KEY, C_KEY, C_VAL, C_VAL, C_GATE_RANK)
C_IN = 2 * C_KEY + 2 * C_VAL + C_GATE_RANK

kernel_name = 'hybrid_swa_sink_gdn_gla_meta_step'


def _rmsnorm(x, g):
    xf = x.astype(F32)
    y = xf * lax.rsqrt(jnp.mean(xf * xf, axis=-1, keepdims=True) + EPS)
    return (y * g.astype(F32)).astype(x.dtype)


def _l2norm(x):
    xf = x.astype(F32)
    return xf * lax.rsqrt(jnp.sum(xf * xf, axis=-1, keepdims=True) + EPS)


def _split(h, sizes):
    out, s = [], 0
    for n in sizes:
        out.append(h[..., s:s + n])
        s += n
    return out


def _sink_attend(q, k, v, mask, sink):
    s = jnp.einsum('...qhgd,...khd->...hgqk', q.astype(F32), k.astype(F32)) * (A_HEAD_DIM ** -0.5)
    s = jnp.where(mask[..., None, None, :, :], s, -jnp.inf)
    sk = sink.astype(F32)[:, :, None, None]
    m = jnp.maximum(jnp.max(s, axis=-1, keepdims=True), sk)
    p = jnp.exp(s - m)
    p = p / (jnp.sum(p, axis=-1, keepdims=True) + jnp.exp(sk - m))
    o = jnp.einsum('...hgqk,...khd->...qhgd', p, v.astype(F32))
    return o.astype(q.dtype)


def _swa_prompt(q, k, v, sink):
    Bn, L = q.shape[:2]
    nb = -(-L // A_BLOCK)
    Lp = nb * A_BLOCK
    qb = jnp.pad(q, ((0, 0), (0, Lp - L), (0, 0), (0, 0))).reshape(Bn, nb, A_BLOCK, A_KV_HEADS, A_GROUP, A_HEAD_DIM)
    kvpad = ((0, 0), (A_BLOCK, Lp - L), (0, 0), (0, 0))
    kb = jnp.pad(k, kvpad).reshape(Bn, nb + 1, A_BLOCK, A_KV_HEADS, A_HEAD_DIM)
    vb = jnp.pad(v, kvpad).reshape(Bn, nb + 1, A_BLOCK, A_KV_HEADS, A_HEAD_DIM)
    kband = jnp.concatenate([kb[:, :-1], kb[:, 1:]], axis=2)
    vband = jnp.concatenate([vb[:, :-1], vb[:, 1:]], axis=2)
    blk = jnp.arange(nb)[:, None] * A_BLOCK
    qpos = blk + jnp.arange(A_BLOCK)[None, :]
    kpos = blk - A_BLOCK + jnp.arange(2 * A_BLOCK)[None, :]
    diff = qpos[:, :, None] - kpos[:, None, :]
    mask = (diff >= 0) & (diff < WINDOW) & (kpos[:, None, :] >= 0)
    o = _sink_attend(qb, kband, vband, mask, sink)
    return o.reshape(Bn, Lp, A_HEADS, A_HEAD_DIM)[:, :L]


def _swa_sample(q, k_new, v_new, k_buf, v_buf, sink):
    Bn, T = q.shape[:2]
    nbuf = k_buf.shape[1]
    k = jnp.concatenate([k_buf.astype(k_new.dtype), k_new], axis=1)
    v = jnp.concatenate([v_buf.astype(v_new.dtype), v_new], axis=1)
    qpos = PAST_LEN + jnp.arange(T)
    kpos = PAST_LEN - nbuf + jnp.arange(nbuf + T)
    diff = qpos[:, None] - kpos[None, :]
    mask = (diff >= 0) & (diff < WINDOW)
    o = _sink_attend(q.reshape(Bn, T, A_KV_HEADS, A_GROUP, A_HEAD_DIM), k, v, mask, sink)
    return o.reshape(Bn, T, A_HEADS, A_HEAD_DIM), k[:, T:], v[:, T:]


def _causal_conv_silu(xc, w):
    T = xc.shape[1] - (CONV_W - 1)
    y = xc[:, 0:T] * w[0]
    for i in range(1, CONV_W):
        y = y + xc[:, i:i + T] * w[i]
    return jax.nn.silu(y)


def _to_chunks(a, lead, tail):
    a = jnp.pad(a.astype(F32), ((0, 0), (lead, tail)) + ((0, 0),) * (a.ndim - 2))
    Bn, Tp = a.shape[:2]
    a = a.reshape((Bn, Tp // CHUNK, CHUNK) + a.shape[2:])
    return jnp.moveaxis(a, 2, 3)


def _from_chunks(o, lead, T):
    o = jnp.moveaxis(jnp.moveaxis(o, 0, 1), 2, 3)
    Bn, N, C, H, d = o.shape
    return o.reshape(Bn, N * C, H, d)[:, lead:lead + T]


def _gated_delta_chunked(q, k, v, beta, g, s0, lead):
    T = q.shape[1]
    tail = (-(T + lead)) % CHUNK
    qc, kc, vc = (_to_chunks(a, lead, tail) for a in (q, k, v))
    bc, gcs = (_to_chunks(a, lead, tail) for a in (beta, g))
    gc = jnp.cumsum(gcs, axis=-1)
    idx = jnp.arange(CHUNK)
    lower = idx[:, None] >= idx[None, :]
    strict = idx[:, None] > idx[None, :]
    decay = jnp.exp(jnp.where(lower, gc[..., :, None] - gc[..., None, :], -jnp.inf))
    kbeta = kc * bc[..., None]
    a_mat = jnp.where(strict, jnp.einsum('bnhik,bnhjk->bnhij', kbeta, kc) * decay, 0.0) + jnp.eye(CHUNK, dtype=F32)
    rhs = jnp.concatenate([vc * bc[..., None], kbeta * jnp.exp(gc)[..., None]], axis=-1)
    sol = lax.linalg.triangular_solve(a_mat, rhs, left_side=True, lower=True, unit_diagonal=True)
    u, w = sol[..., :B_DV], sol[..., B_DV:]
    qk = jnp.where(lower, jnp.einsum('bnhik,bnhjk->bnhij', qc, kc) * decay, 0.0)
    q_dec = qc * jnp.exp(gc)[..., None]
    k_dec = kc * jnp.exp(gc[..., -1:] - gc)[..., None]
    g_last = jnp.exp(gc[..., -1])

    def step(S, xs):
        u_c, w_c, qk_c, qd_c, kd_c, gl_c = xs
        v_new = u_c - jnp.einsum('bhck,bhkv->bhcv', w_c, S)
        o = jnp.einsum('bhck,bhkv->bhcv', qd_c, S) + jnp.einsum('bhij,bhjv->bhiv', qk_c, v_new)
        S = S * gl_c[..., None, None] + jnp.einsum('bhck,bhcv->bhkv', kd_c, v_new)
        return S, o

    xs = tuple(jnp.moveaxis(a, 1, 0) for a in (u, w, qk, q_dec, k_dec, g_last))
    S, o = lax.scan(step, s0.astype(F32), xs)
    return _from_chunks(o, lead, T), S


def _gla_chunked(q, k, v, log_a, s0, lead):
    T = q.shape[1]
    tail = (-(T + lead)) % CHUNK
    qc, kc, vc, lc = (_to_chunks(a, lead, tail) for a in (q, k, v, log_a))
    bc = jnp.cumsum(lc, axis=-2)
    idx = jnp.arange(CHUNK)
    lower = idx[:, None] >= idx[None, :]
    q_dec = qc * jnp.exp(bc)
    qk = jnp.einsum('bnhik,bnhjk->bnhij', q_dec, kc * jnp.exp(-bc))
    o_intra = jnp.einsum('bnhij,bnhjv->bnhiv', jnp.where(lower, qk, 0.0), vc)
    k_dec = kc * jnp.exp(bc[..., -1:, :] - bc)
    g_last = jnp.exp(bc[..., -1, :])

    def step(S, xs):
        qd_c, kd_c, v_c, gl_c, oi_c = xs
        o = oi_c + jnp.einsum('bhck,bhkv->bhcv', qd_c, S)
        S = S * gl_c[..., None] + jnp.einsum('bhck,bhcv->bhkv', kd_c, v_c)
        return S, o

    xs = tuple(jnp.moveaxis(a, 1, 0) for a in (q_dec, k_dec, vc, g_last, o_intra))
    S, o = lax.scan(step, s0.astype(F32), xs)
    return _from_chunks(o, lead, T), S


def _ab_layer(x, norm_g, w_in, sink, conv_w, a_log, dt_bias, onorm, w_out, k_buf, v_buf, conv_buf, s0, lead):
    Bn, T, _ = x.shape
    h = _rmsnorm(x, norm_g)
    proj = jnp.einsum('btd,de->bte', h, w_in)
    qa, ka, va, za, xb, zb, bb, ab = _split(proj, AB_SIZES)
    qa = qa.reshape(Bn, T, A_HEADS, A_HEAD_DIM)
    ka = ka.reshape(Bn, T, A_KV_HEADS, A_HEAD_DIM)
    va = va.reshape(Bn, T, A_KV_HEADS, A_HEAD_DIM)
    sink = sink.reshape(A_KV_HEADS, A_GROUP)
    if k_buf is None:
        oa = _swa_prompt(qa, ka, va, sink)
        n_keep = min(WINDOW, T)
        new_k, new_v = ka[:, T - n_keep:], va[:, T - n_keep:]
        conv_buf = jnp.zeros((Bn, CONV_W - 1, B_CONV_CH), xb.dtype)
        s0 = jnp.zeros((Bn, B_HEADS, B_DK, B_DV), F32)
    else:
        oa, new_k, new_v = _swa_sample(qa, ka, va, k_buf, v_buf, sink)
    xc = jnp.concatenate([conv_buf.astype(xb.dtype), xb], axis=1)
    new_conv = xc[:, xc.shape[1] - (CONV_W - 1):]
    c = _causal_conv_silu(xc, conv_w.astype(xb.dtype))
    qb, kb, vb = _split(c, (B_KEY, B_KEY, B_VAL))
    qb = _l2norm(qb.reshape(Bn, T, B_HEADS, B_DK)) * (B_DK ** -0.5)
    kb = _l2norm(kb.reshape(Bn, T, B_HEADS, B_DK))
    vb = vb.reshape(Bn, T, B_HEADS, B_DV)
    beta = jax.nn.sigmoid(bb.astype(F32))
    g = -jnp.exp(a_log.astype(F32)) * jax.nn.softplus(ab.astype(F32) + dt_bias.astype(F32))
    ob, s_new = _gated_delta_chunked(qb, kb, vb, beta, g, s0, lead)
    ob = _rmsnorm(ob.astype(x.dtype), onorm) * jax.nn.silu(zb.reshape(Bn, T, B_HEADS, B_DV))
    oa = oa.reshape(Bn, T, A_WIDTH) * jax.nn.silu(za)
    mix = jnp.concatenate([oa, ob.reshape(Bn, T, B_VAL)], axis=-1)
    y = x + jnp.einsum('bte,ed->btd', mix, w_out)
    return y, (new_k, new_v, new_conv, s_new.astype(x.dtype))


def _c_layer(x, norm_g, w_in, w_gk_up, b_gk, onorm, w_out, s0, lead):
    Bn, T, _ = x.shape
    h = _rmsnorm(x, norm_g)
    proj = jnp.einsum('btd,de->bte', h, w_in)
    qc, kc, vc, zc, gk_low = _split(proj, C_SIZES)
    log_a = jax.nn.log_sigmoid(jnp.einsum('btr,rk->btk', gk_low.astype(F32), w_gk_up.astype(F32)) + b_gk.astype(F32)) / C_GATE_NORM
    q = qc.reshape(Bn, T, C_HEADS, C_DK) * (C_DK ** -0.5)
    k = kc.reshape(Bn, T, C_HEADS, C_DK)
    v = vc.reshape(Bn, T, C_HEADS, C_DV)
    if s0 is None:
        s0 = jnp.zeros((Bn, C_HEADS, C_DK, C_DV), F32)
    o, s_new = _gla_chunked(q, k, v, log_a.reshape(Bn, T, C_HEADS, C_DK), s0, lead)
    o = _rmsnorm(o.astype(x.dtype), onorm) * jax.nn.silu(zc.reshape(Bn, T, C_HEADS, C_DV))
    y = x + jnp.einsum('bte,ed->btd', o.reshape(Bn, T, C_VAL), w_out)
    return y, s_new.astype(x.dtype)


def setup_inputs(seed: int = 0) -> dict:
    key = jax.random.key(seed)
    ks = jax.random.split(key, 24)

    def nrm(k, shape, scale):
        return jax.random.normal(k, shape, F32) * scale

    dt = jnp.exp(jax.random.uniform(ks[13], (N_AB_LAYERS, B_HEADS), F32, math.log(1e-3), math.log(1e-1)))
    return {
        'x_prompt': nrm(ks[0], (BATCH, SEQ, D_MODEL), 1.0),
        'x_sample': nrm(ks[1], (DEC_BATCH, DEC_SEQ, D_MODEL), 1.0),
        'cache_swa_k': nrm(ks[2], (N_AB_LAYERS, DEC_BATCH, WIN_BUF, A_KV_HEADS, A_HEAD_DIM), 1.0),
        'cache_swa_v': nrm(ks[3], (N_AB_LAYERS, DEC_BATCH, WIN_BUF, A_KV_HEADS, A_HEAD_DIM), 1.0),
        'state_dn_conv': nrm(ks[4], (N_AB_LAYERS, DEC_BATCH, CONV_W - 1, B_CONV_CH), 1.0),
        'state_dn': nrm(ks[5], (N_AB_LAYERS, DEC_BATCH, B_HEADS, B_DK, B_DV), 0.1),
        'state_gla': nrm(ks[6], (N_C_LAYERS, DEC_BATCH, C_HEADS, C_DK, C_DV), 1.0),
        'meta_tokens': nrm(ks[7], (N_META, D_MODEL), 1.0),
        'norm_ab': 1.0 + nrm(ks[8], (N_AB_LAYERS, D_MODEL), 0.05),
        'w_in_ab': nrm(ks[9], (N_AB_LAYERS, D_MODEL, AB_IN), D_MODEL ** -0.5),
        'sink_a': nrm(ks[10], (N_AB_LAYERS, A_HEADS), 0.5),
        'conv_b': nrm(ks[11], (N_AB_LAYERS, CONV_W, B_CONV_CH), CONV_W ** -0.5),
        'a_log_b': jnp.log(jax.random.uniform(ks[12], (N_AB_LAYERS, B_HEADS), F32, 1.0, 16.0)),
        'dt_bias_b': dt + jnp.log(-jnp.expm1(-dt)),
        'onorm_b': 1.0 + nrm(ks[14], (N_AB_LAYERS, B_DV), 0.05),
        'w_out_ab': nrm(ks[15], (N_AB_LAYERS, AB_MIX, D_MODEL), AB_MIX ** -0.5),
        'norm_c': 1.0 + nrm(ks[16], (N_C_LAYERS, D_MODEL), 0.05),
        'w_in_c': nrm(ks[17], (N_C_LAYERS, D_MODEL, C_IN), D_MODEL ** -0.5),
        'w_gk_up': nrm(ks[18], (N_C_LAYERS, C_GATE_RANK, C_KEY), C_GATE_RANK ** -0.5),
        'b_gk': nrm(ks[19], (N_C_LAYERS, C_KEY), 0.1),
        'onorm_c': 1.0 + nrm(ks[20], (N_C_LAYERS, C_DV), 0.05),
        'w_out_c': nrm(ks[21], (N_C_LAYERS, C_VAL, D_MODEL), C_VAL ** -0.5),
        'final_norm': 1.0 + nrm(ks[22], (D_MODEL,), 0.05),
    }


def reference(x_prompt, x_sample, cache_swa_k, cache_swa_v, state_dn_conv, state_dn, state_gla,
              meta_tokens, norm_ab, w_in_ab, sink_a, conv_b, a_log_b, dt_bias_b, onorm_b, w_out_ab,
              norm_c, w_in_c, w_gk_up, b_gk, onorm_c, w_out_c, final_norm):
    Bp = x_prompt.shape[0]
    meta = jnp.broadcast_to(meta_tokens.astype(x_prompt.dtype)[None], (Bp, N_META, D_MODEL))
    hp = jnp.concatenate([meta, x_prompt], axis=1)
    hs = x_sample
    pk, pv, pconv, pdn, pgla = [], [], [], [], []
    sk, sv, sconv, sdn, sgla = [], [], [], [], []
    for layer in range(DEPTH):
        i = layer // 2
        if layer % 2 == 0:
            hp, (k1, v1, c1, d1) = _ab_layer(hp, norm_ab[i], w_in_ab[i], sink_a[i], conv_b[i], a_log_b[i], dt_bias_b[i],
                                            onorm_b[i], w_out_ab[i], None, None, None, None, LEAD_PROMPT)
            hs, (k2, v2, c2, d2) = _ab_layer(hs, norm_ab[i], w_in_ab[i], sink_a[i], conv_b[i], a_log_b[i], dt_bias_b[i],
                                            onorm_b[i], w_out_ab[i], cache_swa_k[i], cache_swa_v[i],
                                            state_dn_conv[i], state_dn[i], 0)
            pk.append(k1); pv.append(v1); pconv.append(c1); pdn.append(d1)
            sk.append(k2); sv.append(v2); sconv.append(c2); sdn.append(d2)
        else:
            hp, g1 = _c_layer(hp, norm_c[i], w_in_c[i], w_gk_up[i], b_gk[i], onorm_c[i], w_out_c[i], None, LEAD_PROMPT)
            hs, g2 = _c_layer(hs, norm_c[i], w_in_c[i], w_gk_up[i], b_gk[i], onorm_c[i], w_out_c[i], state_gla[i], 0)
            pgla.append(g1)
            sgla.append(g2)
    y_prompt = _rmsnorm(hp, final_norm)[:, N_META:]
    y_sample = _rmsnorm(hs, final_norm)
    return (y_prompt, y_sample,
            jnp.stack(pk), jnp.stack(pv), jnp.stack(pconv), jnp.stack(pdn), jnp.stack(pgla),
            jnp.stack(sk), jnp.stack(sv), jnp.stack(sconv), jnp.stack(sdn), jnp.stack(sgla))
```

```python
import functools

import jax
import jax.numpy as jnp
from jax import lax
from jax.experimental import pallas as pl
from jax.experimental.pallas import tpu as pltpu

F32 = jnp.float32
BF16 = jnp.bfloat16
EPS = 1e-6

D_MODEL = 1024
N_META = 16
CHUNK = 64
LEAD = (-N_META) % CHUNK
A_HEADS, A_KV_HEADS, A_HD = 8, 2, 64
A_GROUP = A_HEADS // A_KV_HEADS
A_WIDTH = A_HEADS * A_HD
A_KV_WIDTH = A_KV_HEADS * A_HD
WINDOW = 128
B_HEADS, B_DK, B_DV = 4, 128, 128
B_KEY = B_HEADS * B_DK
B_VAL = B_HEADS * B_DV
CONV_W = 4
B_CONV_CH = 2 * B_KEY + B_VAL
C_HEADS, C_DK, C_DV = 4, 128, 256
C_KEY = C_HEADS * C_DK
C_VAL = C_HEADS * C_DV
C_RANK = 16
C_GATE_NORM = 16.0
AB_MAIN = 2 * A_WIDTH + 2 * A_KV_WIDTH + B_CONV_CH + B_VAL
C_MAIN = 2 * C_KEY + 2 * C_VAL
LANES = 128
NEG = -1e30

VMEM_LIMIT = 48 * 1024 * 1024


def _cparams(sem):
    return pltpu.CompilerParams(dimension_semantics=sem, vmem_limit_bytes=VMEM_LIMIT)


def _mm(a, b):
    return jnp.dot(a.astype(BF16), b.astype(BF16), preferred_element_type=F32)


def _mm_nt(a, b):
    return lax.dot_general(a.astype(BF16), b.astype(BF16), (((1,), (1,)), ((), ())),
                           preferred_element_type=F32)


def _mm_tn(a, b):
    return lax.dot_general(a.astype(BF16), b.astype(BF16), (((0,), (0,)), ((), ())),
                           preferred_element_type=F32)


def _split3(a):
    hi = a.astype(BF16)
    r = a - hi.astype(F32)
    mid = r.astype(BF16)
    lo = (r - mid.astype(F32)).astype(BF16)
    return hi, mid, lo


def _sigmoid(x):
    return 1.0 / (1.0 + jnp.exp(-x))


def _silu(x):
    return x * _sigmoid(x)


def _softplus(x):
    return jnp.maximum(x, 0.0) + jnp.log1p(jnp.exp(-jnp.abs(x)))


def _log_sigmoid(x):
    return jnp.minimum(x, 0.0) - jnp.log1p(jnp.exp(-jnp.abs(x)))


def _rms(x, g):
    return x * lax.rsqrt(jnp.mean(x * x, axis=-1, keepdims=True) + EPS) * g


def _ab_proj_kernel(x_ref, g_ref, w_ref, wg_ref, wgt_ref,
                    qa_ref, kv_ref, za_ref, xb_ref, zb_ref, gates_ref, gatest_ref):
    hb = _rms(x_ref[...], g_ref[...]).astype(BF16)

    def proj(a, b):
        return jnp.dot(hb, w_ref[:, a:b], preferred_element_type=F32)

    o = 0
    for ref, n in ((qa_ref, A_WIDTH), (kv_ref, 2 * A_KV_WIDTH), (za_ref, A_WIDTH),
                   (xb_ref, B_CONV_CH), (zb_ref, B_VAL)):
        ref[...] = proj(o, o + n)
        o += n
    gates_ref[...] = jnp.dot(hb, wg_ref[...], preferred_element_type=F32)
    gatest_ref[...] = lax.dot_general(wgt_ref[...], hb, (((1,), (1,)), ((), ())),
                                      preferred_element_type=F32)


def _ab_proj(x2d, norm_g, w_main, w_gate, w_gate_t, tm):
    T = x2d.shape[0]
    row = lambda n: pl.BlockSpec((tm, n), lambda i: (i, 0))
    full = lambda a: pl.BlockSpec(a.shape, lambda i: (0,) * a.ndim)
    widths = (A_WIDTH, 2 * A_KV_WIDTH, A_WIDTH, B_CONV_CH, B_VAL, LANES)
    return pl.pallas_call(
        _ab_proj_kernel,
        grid=(T // tm,),
        in_specs=[row(D_MODEL), full(norm_g), full(w_main), full(w_gate), full(w_gate_t)],
        out_specs=[row(n) for n in widths] + [pl.BlockSpec((16, tm), lambda i: (0, i))],
        out_shape=[jax.ShapeDtypeStruct((T, n), F32) for n in widths]
        + [jax.ShapeDtypeStruct((16, T), F32)],
        compiler_params=_cparams(("arbitrary",)),
        name="ab_in_proj",
    )(x2d, norm_g, w_main, w_gate, w_gate_t)


def _swa_kernel(q_ref, kv_ref, kvp_ref, hist_ref, za_ref, sink_ref, o_ref, *, tq, sb, pos_shift):
    i = pl.program_id(1)
    prev = jnp.where(i == 0, hist_ref[...], kvp_ref[0])
    kvfull = jnp.concatenate([prev, kv_ref[0]], axis=0)
    r = lax.broadcasted_iota(jnp.int32, (sb, WINDOW + sb), 0)
    c = lax.broadcasted_iota(jnp.int32, (sb, WINDOW + sb), 1)
    diff = r - c + WINDOW
    in_window = (diff >= 0) & (diff < WINDOW)
    for s in range(tq // sb):
        band = kvfull[s * sb:s * sb + WINDOW + sb]
        k_local = i * tq + s * sb + c - WINDOW
        mask = in_window & (k_local + pos_shift >= 0)
        q = q_ref[0, s * sb:(s + 1) * sb, :]
        outs = []
        for h in range(A_HEADS):
            j = h // A_GROUP
            kj = band[:, j * A_HD:(j + 1) * A_HD]
            vj = band[:, A_KV_WIDTH + j * A_HD:A_KV_WIDTH + (j + 1) * A_HD]
            sc = _mm_nt(q[:, h * A_HD:(h + 1) * A_HD], kj) * (A_HD ** -0.5)
            sc = jnp.where(mask, sc, NEG)
            sk = sink_ref[0:1, h:h + 1]
            m = jnp.maximum(jnp.max(sc, axis=-1, keepdims=True), sk)
            p = jnp.exp(sc - m)
            den = jnp.sum(p, axis=-1, keepdims=True) + jnp.exp(sk - m)
            outs.append(_mm(p, vj) / den)
        o = jnp.concatenate(outs, axis=-1)
        o_ref[0, s * sb:(s + 1) * sb, :] = o * _silu(za_ref[0, s * sb:(s + 1) * sb, :])


def _swa_prompt(qa, kv, kv_prev_src, hist, za, sink, tq, pos_shift):
    B, L, _ = qa.shape
    sb = min(WINDOW, tq)
    nprev = tq // WINDOW if tq >= WINDOW else 1
    kern = functools.partial(_swa_kernel, tq=tq, sb=sb, pos_shift=pos_shift)
    return pl.pallas_call(
        kern,
        grid=(B, L // tq),
        in_specs=[
            pl.BlockSpec((1, tq, A_WIDTH), lambda b, i: (b, i, 0)),
            pl.BlockSpec((1, tq, 2 * A_KV_WIDTH), lambda b, i: (b, i, 0)),
            pl.BlockSpec((1, WINDOW, 2 * A_KV_WIDTH),
                         lambda b, i: (b, jnp.maximum(i * nprev - 1, 0), 0)),
            pl.BlockSpec(hist.shape, lambda b, i: (0, 0)),
            pl.BlockSpec((1, tq, A_WIDTH), lambda b, i: (b, i, 0)),
            pl.BlockSpec(sink.shape, lambda b, i: (0, 0)),
        ],
        out_specs=pl.BlockSpec((1, tq, A_WIDTH), lambda b, i: (b, i, 0)),
        out_shape=jax.ShapeDtypeStruct((B, L, A_WIDTH), F32),
        compiler_params=_cparams(("arbitrary", "arbitrary")),
        name="swa_prompt",
    )(qa, kv, kv_prev_src, hist, za, sink)


def _inv_unit_lower_off(a, ii, jj):
    doff = -jnp.where((ii >> 1) == (jj >> 1), a, 0.0)
    for lg in range(1, 6):
        m = ((ii >> (lg + 1)) == (jj >> (lg + 1))) & ((ii >> lg) != (jj >> lg))
        b = jnp.where(m, a, 0.0)
        y = b + _mm(doff, b)
        x = y + _mm(y, doff)
        doff = doff - x
    return doff


def _gdn_kernel(xb_ref, gates_ref, gatest_ref, zb_ref, convw_ref, gprm_ref, gprmt_ref, onorm_ref,
                convin_ref, sin_ref, ob_ref, sout_ref, s_scr, xc_scr, *, ct, n_lead):
    i = pl.program_id(1)
    nsteps = pl.num_programs(1)

    @pl.when(i == 0)
    def _():
        s_scr[...] = sin_ref[0]
        xc_scr[0:8, :] = convin_ref[0]

    xc_scr[8:8 + ct, :] = xb_ref[0]
    w = convw_ref[...]
    y = xc_scr[pl.ds(8 - (CONV_W - 1), ct), :] * w[0:1, :]
    for t in range(1, CONV_W):
        y = y + xc_scr[pl.ds(8 - (CONV_W - 1) + t, ct), :] * w[t:t + 1, :]
    cv = _silu(y)
    xc_scr[0:8, :] = xc_scr[ct:ct + 8, :]

    gt = gates_ref[0]
    beta = _sigmoid(gt)
    rows = i * ct + lax.broadcasted_iota(jnp.int32, gt.shape, 0)
    gcol = jnp.where(rows >= n_lead, -gprm_ref[0:1, :] * _softplus(gt + gprm_ref[1:2, :]), 0.0)
    gtt = gatest_ref[...]
    lanes = i * ct + lax.broadcasted_iota(jnp.int32, gtt.shape, 1)
    grow = jnp.where(lanes >= n_lead, -gprmt_ref[:, 0:1] * _softplus(gtt + gprmt_ref[:, 1:2]), 0.0)

    ii = lax.broadcasted_iota(jnp.int32, (CHUNK, CHUNK), 0)
    jj = lax.broadcasted_iota(jnp.int32, (CHUNK, CHUNK), 1)
    lower = ii >= jj
    strict = ii > jj
    ltri = lower.astype(BF16)
    utri = (ii <= jj).astype(BF16)

    for c in range(ct // CHUNK):
        r0 = c * CHUNK
        gcum_c = sum(jnp.dot(ltri, p, preferred_element_type=F32)
                     for p in _split3(gcol[r0:r0 + CHUNK]))
        gcum_r = sum(jnp.dot(p, utri, preferred_element_type=F32)
                     for p in _split3(grow[:, r0:r0 + CHUNK]))
        for h in range(B_HEADS):
            q = cv[r0:r0 + CHUNK, h * B_DK:(h + 1) * B_DK]
            k = cv[r0:r0 + CHUNK, B_KEY + h * B_DK:B_KEY + (h + 1) * B_DK]
            v = cv[r0:r0 + CHUNK, 2 * B_KEY + h * B_DV:2 * B_KEY + (h + 1) * B_DV]
            q = q * lax.rsqrt(jnp.sum(q * q, axis=-1, keepdims=True) + EPS) * (B_DK ** -0.5)
            k = k * lax.rsqrt(jnp.sum(k * k, axis=-1, keepdims=True) + EPS)
            bh = beta[r0:r0 + CHUNK, h:h + 1]
            gc = gcum_c[:, B_HEADS + h:B_HEADS + h + 1]
            gr = gcum_r[B_HEADS + h:B_HEADS + h + 1, :]
            decay = jnp.exp(jnp.where(lower, gc - gr, NEG))
            kb = k * bh
            a = jnp.where(strict, _mm_nt(kb, k) * decay, 0.0)
            qk = jnp.where(lower, _mm_nt(q, k) * decay, 0.0)
            toff = _inv_unit_lower_off(a, ii, jj)
            egc = jnp.exp(gc)
            rhs = jnp.concatenate([v * bh, kb * egc], axis=-1)
            sol = rhs + _mm(toff, rhs)
            u = sol[:, :B_DV]
            wm = sol[:, B_DV:]
            s = s_scr[h]
            v_new = u - _mm(wm, s)
            o = _mm(q * egc, s) + _mm(qk, v_new)
            g_last = gc[CHUNK - 1:CHUNK, :]
            kd = k * jnp.exp(g_last - gc)
            s_scr[h] = s * jnp.exp(g_last) + _mm_tn(kd, v_new)
            z = zb_ref[0, r0:r0 + CHUNK, h * B_DV:(h + 1) * B_DV]
            ob_ref[0, r0:r0 + CHUNK, h * B_DV:(h + 1) * B_DV] = _rms(o, onorm_ref[...]) * _silu(z)

    @pl.when(i == nsteps - 1)
    def _():
        sout_ref[0] = s_scr[...]


def _gdn_prompt(xb, gates, gates_t, zb, conv_w8, gprm, gprm_t, onorm, conv_in, s_in, ct, n_lead):
    B, L, _ = xb.shape
    nt = L // ct
    kern = functools.partial(_gdn_kernel, ct=ct, n_lead=n_lead)
    full = lambda a: pl.BlockSpec(a.shape, lambda b, i: (0,) * a.ndim)
    return pl.pallas_call(
        kern,
        grid=(B, nt),
        in_specs=[
            pl.BlockSpec((1, ct, B_CONV_CH), lambda b, i: (b, i, 0)),
            pl.BlockSpec((1, ct, LANES), lambda b, i: (b, i, 0)),
            pl.BlockSpec((16, ct), lambda b, i: (0, b * nt + i)),
            pl.BlockSpec((1, ct, B_VAL), lambda b, i: (b, i, 0)),
            full(conv_w8), full(gprm), full(gprm_t), full(onorm),
            pl.BlockSpec((1, 8, B_CONV_CH), lambda b, i: (b, 0, 0)),
            pl.BlockSpec((1, B_HEADS, B_DK, B_DV), lambda b, i: (b, 0, 0, 0)),
        ],
        out_specs=[
            pl.BlockSpec((1, ct, B_VAL), lambda b, i: (b, i, 0)),
            pl.BlockSpec((1, B_HEADS, B_DK, B_DV), lambda b, i: (b, 0, 0, 0)),
        ],
        out_shape=[jax.ShapeDtypeStruct((B, L, B_VAL), F32),
                   jax.ShapeDtypeStruct((B, B_HEADS, B_DK, B_DV), F32)],
        scratch_shapes=[pltpu.VMEM((B_HEADS, B_DK, B_DV), F32),
                        pltpu.VMEM((ct + 8, B_CONV_CH), F32)],
        compiler_params=_cparams(("arbitrary", "arbitrary")),
        name="gdn_prompt",
    )(xb, gates, gates_t, zb, conv_w8, gprm, gprm_t, onorm, conv_in, s_in)


def _ab_out_kernel(x_ref, oa_ref, ob_ref, w_ref, y_ref):
    y_ref[...] = (x_ref[...]
                  + jnp.dot(oa_ref[...].astype(BF16), w_ref[0:A_WIDTH, :], preferred_element_type=F32)
                  + jnp.dot(ob_ref[...].astype(BF16), w_ref[A_WIDTH:, :], preferred_element_type=F32))


def _ab_out(x2d, oa, ob, w_out, tm):
    T = x2d.shape[0]
    row = lambda n: pl.BlockSpec((tm, n), lambda i: (i, 0))
    return pl.pallas_call(
        _ab_out_kernel,
        grid=(T // tm,),
        in_specs=[row(D_MODEL), row(A_WIDTH), row(B_VAL),
                  pl.BlockSpec(w_out.shape, lambda i: (0, 0))],
        out_specs=row(D_MODEL),
        out_shape=jax.ShapeDtypeStruct((T, D_MODEL), F32),
        compiler_params=_cparams(("arbitrary",)),
        name="ab_out_proj",
    )(x2d, oa, ob, w_out)


def _c_out_kernel(x_ref, o_ref, w_ref, g_ref, y_ref):
    y = x_ref[...] + jnp.dot(o_ref[...].astype(BF16), w_ref[...], preferred_element_type=F32)
    y_ref[...] = _rms(y, g_ref[...])


def _c_out(x2d, oc, w_out, final_g, tm):
    T = x2d.shape[0]
    row = lambda n: pl.BlockSpec((tm, n), lambda i: (i, 0))
    return pl.pallas_call(
        _c_out_kernel,
        grid=(T // tm,),
        in_specs=[row(D_MODEL), row(C_VAL), pl.BlockSpec(w_out.shape, lambda i: (0, 0)),
                  pl.BlockSpec(final_g.shape, lambda i: (0, 0))],
        out_specs=row(D_MODEL),
        out_shape=jax.ShapeDtypeStruct((T, D_MODEL), F32),
        compiler_params=_cparams(("arbitrary",)),
        name="c_out_proj_norm",
    )(x2d, oc, w_out, final_g)


def _c_proj_kernel(x_ref, g_ref, w_ref, wgk_ref, wup_ref, bgk_ref,
                   q_ref, k_ref, v_ref, z_ref, la_ref):
    hb = _rms(x_ref[...], g_ref[...]).astype(BF16)
    o = 0
    for ref, n in ((q_ref, C_KEY), (k_ref, C_KEY), (v_ref, C_VAL), (z_ref, C_VAL)):
        ref[...] = jnp.dot(hb, w_ref[:, o:o + n], preferred_element_type=F32)
        o += n
    low = jnp.dot(hb, wgk_ref[...], preferred_element_type=F32)
    lh, lm, _ = _split3(low)
    wh, wm, _ = _split3(wup_ref[...])
    up = (jnp.dot(lh, wh, preferred_element_type=F32) + jnp.dot(lh, wm, preferred_element_type=F32)
          + jnp.dot(lm, wh, preferred_element_type=F32))
    la_ref[...] = _log_sigmoid(up + bgk_ref[...]) * (1.0 / C_GATE_NORM)


def _c_proj(x2d, norm_g, w_main, w_gk, w_up, b_gk, tm):
    T = x2d.shape[0]
    row = lambda n: pl.BlockSpec((tm, n), lambda i: (i, 0))
    full = lambda a: pl.BlockSpec(a.shape, lambda i: (0,) * a.ndim)
    widths = (C_KEY, C_KEY, C_VAL, C_VAL, C_KEY)
    return pl.pallas_call(
        _c_proj_kernel,
        grid=(T // tm,),
        in_specs=[row(D_MODEL), full(norm_g), full(w_main), full(w_gk), full(w_up), full(b_gk)],
        out_specs=[row(n) for n in widths],
        out_shape=[jax.ShapeDtypeStruct((T, n), F32) for n in widths],
        compiler_params=_cparams(("arbitrary",)),
        name="c_in_proj",
    )(x2d, norm_g, w_main, w_gk, w_up, b_gk)


def _gla_kernel(q_ref, k_ref, v_ref, z_ref, la_ref, onorm_ref, sin_ref, o_ref, sout_ref, s_scr,
                *, ct, n_lead):
    i = pl.program_id(1)
    nsteps = pl.num_programs(1)

    @pl.when(i == 0)
    def _():
        s_scr[...] = sin_ref[0]

    ii = lax.broadcasted_iota(jnp.int32, (CHUNK, CHUNK), 0)
    jj = lax.broadcasted_iota(jnp.int32, (CHUNK, CHUNK), 1)
    lower = ii >= jj
    ltri = lower.astype(BF16)
    ek = lax.broadcasted_iota(jnp.int32, (C_DK, C_DK), 0) == lax.broadcasted_iota(jnp.int32, (C_DK, C_DK), 1)
    ones_v = jnp.ones((C_DV, C_DK), BF16)

    la_all = la_ref[0]
    rows = i * ct + lax.broadcasted_iota(jnp.int32, la_all.shape, 0)
    la_all = jnp.where(rows >= n_lead, la_all, 0.0)

    for c in range(ct // CHUNK):
        r0 = c * CHUNK
        bc_all = sum(jnp.dot(ltri, p, preferred_element_type=F32)
                     for p in _split3(la_all[r0:r0 + CHUNK]))
        for h in range(C_HEADS):
            q = q_ref[0, r0:r0 + CHUNK, h * C_DK:(h + 1) * C_DK] * (C_DK ** -0.5)
            k = k_ref[0, r0:r0 + CHUNK, h * C_DK:(h + 1) * C_DK]
            v = v_ref[0, r0:r0 + CHUNK, h * C_DV:(h + 1) * C_DV]
            bc = bc_all[:, h * C_DK:(h + 1) * C_DK]
            qd = q * jnp.exp(bc)
            qk = jnp.where(lower, _mm_nt(qd, k * jnp.exp(-bc)), 0.0)
            s = s_scr[h]
            o = _mm(qk, v) + _mm(qd, s)
            bl = bc[CHUNK - 1:CHUNK, :]
            kd = k * jnp.exp(bl - bc)
            gl = sum(lax.dot_general(p, ones_v, (((1,), (1,)), ((), ())), preferred_element_type=F32)
                     for p in _split3(jnp.where(ek, jnp.exp(bl), 0.0)))
            s_scr[h] = s * gl + _mm_tn(kd, v)
            z = z_ref[0, r0:r0 + CHUNK, h * C_DV:(h + 1) * C_DV]
            o_ref[0, r0:r0 + CHUNK, h * C_DV:(h + 1) * C_DV] = _rms(o, onorm_ref[...]) * _silu(z)

    @pl.when(i == nsteps - 1)
    def _():
        sout_ref[0] = s_scr[...]


def _gla_prompt(q, k, v, z, la, onorm, s_in, ct, n_lead):
    B, L, _ = q.shape
    kern = functools.partial(_gla_kernel, ct=ct, n_lead=n_lead)
    blk = lambda n: pl.BlockSpec((1, ct, n), lambda b, i: (b, i, 0))
    st = pl.BlockSpec((1, C_HEADS, C_DK, C_DV), lambda b, i: (b, 0, 0, 0))
    return pl.pallas_call(
        kern,
        grid=(B, L // ct),
        in_specs=[blk(C_KEY), blk(C_KEY), blk(C_VAL), blk(C_VAL), blk(C_KEY),
                  pl.BlockSpec(onorm.shape, lambda b, i: (0, 0)), st],
        out_specs=[blk(C_VAL), st],
        out_shape=[jax.ShapeDtypeStruct((B, L, C_VAL), F32),
                   jax.ShapeDtypeStruct((B, C_HEADS, C_DK, C_DV), F32)],
        scratch_shapes=[pltpu.VMEM((C_HEADS, C_DK, C_DV), F32)],
        compiler_params=_cparams(("arbitrary", "arbitrary")),
        name="gla_prompt",
    )(q, k, v, z, la, onorm, s_in)


def _swa_sample_kernel(q_ref, kvn_ref, za_ref, ck_ref, cv_ref, sink_ref, o_ref, nk_ref, nv_ref, *, bt):
    row = lax.broadcasted_iota(jnp.int32, (WINDOW, A_KV_WIDTH), 0)
    hrow = lax.broadcasted_iota(jnp.int32, (A_HEADS, A_HD), 0)
    sk = sink_ref[...]
    for b in range(bt):
        kn = kvn_ref[b:b + 1, 0:A_KV_WIDTH]
        vn = kvn_ref[b:b + 1, A_KV_WIDTH:]
        nk = jnp.where(row == WINDOW - 1, kn, pltpu.roll(ck_ref[b], WINDOW - 1, 0))
        nv = jnp.where(row == WINDOW - 1, vn, pltpu.roll(cv_ref[b], WINDOW - 1, 0))
        nk_ref[b] = nk
        nv_ref[b] = nv
        q8 = q_ref[b]
        o8 = None
        for j in range(A_KV_HEADS):
            sc = _mm_nt(q8, nk[:, j * A_HD:(j + 1) * A_HD]) * (A_HD ** -0.5)
            m = jnp.maximum(jnp.max(sc, axis=-1, keepdims=True), sk)
            p = jnp.exp(sc - m)
            den = jnp.sum(p, axis=-1, keepdims=True) + jnp.exp(sk - m)
            oj = _mm(p, nv[:, j * A_HD:(j + 1) * A_HD]) / den
            o8 = oj if o8 is None else jnp.where(hrow >= j * A_GROUP, oj, o8)
        o_ref[b] = o8 * _silu(za_ref[b])


def _swa_sample(q3, kv_new, za3, cache_k, cache_v, sink_col, bt):
    nb = q3.shape[0]
    hd = pl.BlockSpec((bt, A_HEADS, A_HD), lambda i: (i, 0, 0))
    cache = pl.BlockSpec((bt, WINDOW, A_KV_WIDTH), lambda i: (i, 0, 0))
    return pl.pallas_call(
        functools.partial(_swa_sample_kernel, bt=bt),
        grid=(nb // bt,),
        in_specs=[hd, pl.BlockSpec((bt, 2 * A_KV_WIDTH), lambda i: (i, 0)), hd, cache, cache,
                  pl.BlockSpec(sink_col.shape, lambda i: (0, 0))],
        out_specs=[hd, cache, cache],
        out_shape=[jax.ShapeDtypeStruct(q3.shape, F32),
                   jax.ShapeDtypeStruct(cache_k.shape, F32),
                   jax.ShapeDtypeStruct(cache_v.shape, F32)],
        compiler_params=_cparams(("arbitrary",)),
        name="swa_sample",
    )(q3, kv_new, za3, cache_k, cache_v, sink_col)


def _gdn_sample_prep_kernel(xb_ref, h0_ref, h1_ref, h2_ref, gates_ref, convw_ref, gprm_ref,
                            q_ref, k_ref, v_ref, bg_ref):
    w = convw_ref[...]
    y = (h0_ref[...] * w[0:1, :] + h1_ref[...] * w[1:2, :] + h2_ref[...] * w[2:3, :]
         + xb_ref[...] * w[3:4, :])
    cv = _silu(y)
    for h in range(B_HEADS):
        q = cv[:, h * B_DK:(h + 1) * B_DK]
        k = cv[:, B_KEY + h * B_DK:B_KEY + (h + 1) * B_DK]
        q_ref[:, h * B_DK:(h + 1) * B_DK] = (
            q * lax.rsqrt(jnp.sum(q * q, axis=-1, keepdims=True) + EPS) * (B_DK ** -0.5))
        k_ref[:, h * B_DK:(h + 1) * B_DK] = k * lax.rsqrt(jnp.sum(k * k, axis=-1, keepdims=True) + EPS)
    v_ref[...] = cv[:, 2 * B_KEY:]
    gt = gates_ref[...]
    col = lax.broadcasted_iota(jnp.int32, gt.shape, 1)
    g = -gprm_ref[0:1, :] * _softplus(gt + gprm_ref[1:2, :])
    bg_ref[...] = jnp.where(col < B_HEADS, _sigmoid(gt), jnp.exp(g))


def _gdn_sample_prep(xb, h0, h1, h2, gates, conv_w8, gprm):
    n = xb.shape[0]
    full = lambda a: pl.BlockSpec(a.shape, lambda: (0,) * a.ndim)
    args = (xb, h0, h1, h2, gates, conv_w8, gprm)
    return pl.pallas_call(
        _gdn_sample_prep_kernel,
        in_specs=[full(a) for a in args],
        out_specs=[pl.BlockSpec((n, B_KEY), lambda: (0, 0))] * 3 + [pl.BlockSpec((n, LANES), lambda: (0, 0))],
        out_shape=[jax.ShapeDtypeStruct((n, B_KEY), F32)] * 3 + [jax.ShapeDtypeStruct((n, LANES), F32)],
        compiler_params=pltpu.CompilerParams(vmem_limit_bytes=VMEM_LIMIT),
        name="gdn_sample_prep",
    )(*args)


def _gdn_sample_kernel(s_ref, qt_ref, kt_ref, v_ref, bg_ref, zb_ref, onorm_ref, o_ref, sout_ref, *, bt):
    for b in range(bt):
        for h in range(B_HEADS):
            s = s_ref[b, h]
            kc = kt_ref[0, h, :, b:b + 1]
            qc = qt_ref[0, h, :, b:b + 1]
            v = v_ref[b:b + 1, h * B_DV:(h + 1) * B_DV]
            beta = bg_ref[b:b + 1, h:h + 1]
            eg = bg_ref[b:b + 1, B_HEADS + h:B_HEADS + h + 1]
            ks = jnp.sum(kc * s, axis=0, keepdims=True)
            v_new = beta * (v - eg * ks)
            s_new = eg * s + kc * v_new
            sout_ref[b, h] = s_new
            o = jnp.sum(qc * s_new, axis=0, keepdims=True)
            z = zb_ref[b:b + 1, h * B_DV:(h + 1) * B_DV]
            o_ref[b:b + 1, h * B_DV:(h + 1) * B_DV] = _rms(o, onorm_ref[...]) * _silu(z)


def _gdn_sample(state, qt, kt, v, bg, zb, onorm, bt):
    nb = state.shape[0]
    st = pl.BlockSpec((bt, B_HEADS, B_DK, B_DV), lambda i: (i, 0, 0, 0))
    col = pl.BlockSpec((1, B_HEADS, B_DK, bt), lambda i: (i, 0, 0, 0))
    row = lambda n: pl.BlockSpec((bt, n), lambda i: (i, 0))
    return pl.pallas_call(
        functools.partial(_gdn_sample_kernel, bt=bt),
        grid=(nb // bt,),
        in_specs=[st, col, col, row(B_VAL), row(LANES), row(B_VAL),
                  pl.BlockSpec(onorm.shape, lambda i: (0, 0))],
        out_specs=[row(B_VAL), st],
        out_shape=[jax.ShapeDtypeStruct((nb, B_VAL), F32), jax.ShapeDtypeStruct(state.shape, F32)],
        compiler_params=_cparams(("arbitrary",)),
        name="gdn_sample",
    )(state, qt, kt, v, bg, zb, onorm)


def _gla_sample_kernel(s_ref, qt_ref, kt_ref, lat_ref, v_ref, z_ref, onorm_ref, o_ref, sout_ref, *, bt):
    for b in range(bt):
        for h in range(C_HEADS):
            s = s_ref[b, h]
            kc = kt_ref[0, h, :, b:b + 1]
            qc = qt_ref[0, h, :, b:b + 1] * (C_DK ** -0.5)
            ac = jnp.exp(lat_ref[0, h, :, b:b + 1])
            v = v_ref[b:b + 1, h * C_DV:(h + 1) * C_DV]
            s_new = ac * s + kc * v
            sout_ref[b, h] = s_new
            o = jnp.sum(qc * s_new, axis=0, keepdims=True)
            z = z_ref[b:b + 1, h * C_DV:(h + 1) * C_DV]
            o_ref[b:b + 1, h * C_DV:(h + 1) * C_DV] = _rms(o, onorm_ref[...]) * _silu(z)


def _gla_sample(state, qt, kt, lat, v, z, onorm, bt):
    nb = state.shape[0]
    st = pl.BlockSpec((bt, C_HEADS, C_DK, C_DV), lambda i: (i, 0, 0, 0))
    col = pl.BlockSpec((1, C_HEADS, C_DK, bt), lambda i: (i, 0, 0, 0))
    row = lambda n: pl.BlockSpec((bt, n), lambda i: (i, 0))
    return pl.pallas_call(
        functools.partial(_gla_sample_kernel, bt=bt),
        grid=(nb // bt,),
        in_specs=[st, col, col, col, row(C_VAL), row(C_VAL),
                  pl.BlockSpec(onorm.shape, lambda i: (0, 0))],
        out_specs=[row(C_VAL), st],
        out_shape=[jax.ShapeDtypeStruct((nb, C_VAL), F32), jax.ShapeDtypeStruct(state.shape, F32)],
        compiler_params=_cparams(("arbitrary",)),
        name="gla_sample",
    )(state, qt, kt, lat, v, z, onorm)


def _to_cols(a, heads, dk, bt):
    n = a.shape[0]
    return a.reshape(n // bt, bt, heads, dk).transpose(0, 2, 3, 1)


def kernel(x_prompt, x_sample, cache_swa_k, cache_swa_v, state_dn_conv, state_dn, state_gla,
           meta_tokens, norm_ab, w_in_ab, sink_a, conv_b, a_log_b, dt_bias_b, onorm_b, w_out_ab,
           norm_c, w_in_c, w_gk_up, b_gk, onorm_c, w_out_c, final_norm):
    Bp, L, _ = x_prompt.shape
    Bs = x_sample.shape[0]
    TM, TQ, CT, BT = 256, 512, 256, 8

    w_ab = w_in_ab[0]
    w_ab_main = w_ab[:, :AB_MAIN].astype(BF16)
    w_ab_gate = jnp.pad(w_ab[:, AB_MAIN:], ((0, 0), (0, LANES - 2 * B_HEADS))).astype(BF16)
    w_ab_gate_t = jnp.pad(w_ab[:, AB_MAIN:].T, ((0, 16 - 2 * B_HEADS), (0, 0))).astype(BF16)
    g_ab = norm_ab[0][None, :]
    sink_row = sink_a[0][None, :]
    sink_col = sink_a[0][:, None]
    conv_w8 = jnp.pad(conv_b[0], ((0, 8 - CONV_W), (0, 0)))
    ea = jnp.exp(a_log_b[0])
    gprm = jnp.zeros((8, LANES), F32).at[0, B_HEADS:2 * B_HEADS].set(ea).at[1, B_HEADS:2 * B_HEADS].set(dt_bias_b[0])
    gprm_t = jnp.zeros((16, LANES), F32).at[B_HEADS:2 * B_HEADS, 0].set(ea).at[B_HEADS:2 * B_HEADS, 1].set(dt_bias_b[0])
    onb = onorm_b[0][None, :]
    w_ab_out = w_out_ab[0].astype(BF16)
    w_c = w_in_c[0]
    w_c_main = w_c[:, :C_MAIN].astype(BF16)
    w_c_gk = jnp.pad(w_c[:, C_MAIN:], ((0, 0), (0, LANES - C_RANK))).astype(BF16)
    w_up = jnp.pad(w_gk_up[0], ((0, LANES - C_RANK), (0, 0)))
    bgk = b_gk[0][None, :]
    g_c = norm_c[0][None, :]
    onc = onorm_c[0][None, :]
    w_c_out = w_out_c[0].astype(BF16)
    g_fin = final_norm[None, :]

    def ab_layer_prompt(x3, tm, tq, ct, hist_kv, pos_shift, conv_in, s_in, n_lead):
        B, Ls, _ = x3.shape
        x2 = x3.reshape(B * Ls, D_MODEL)
        qa, kv, za, xb, zb, gates, gates_t = _ab_proj(x2, g_ab, w_ab_main, w_ab_gate, w_ab_gate_t, tm)
        r3 = lambda a: a.reshape(B, Ls, a.shape[-1])
        kv3 = r3(kv)
        kv_prev_src = kv3 if Ls >= WINDOW else hist_kv[None]
        oa = _swa_prompt(r3(qa), kv3, kv_prev_src, hist_kv, r3(za), sink_row, tq, pos_shift)
        ob, s_out = _gdn_prompt(r3(xb), r3(gates), gates_t, r3(zb), conv_w8, gprm, gprm_t, onb,
                                conv_in, s_in, ct, n_lead)
        y = _ab_out(x2, oa.reshape(B * Ls, A_WIDTH), ob.reshape(B * Ls, B_VAL), w_ab_out, tm)
        return y.reshape(B, Ls, D_MODEL), kv3, r3(xb), s_out

    def c_layer_prompt(x3, tm, ct, s_in, n_lead):
        B, Ls, _ = x3.shape
        x2 = x3.reshape(B * Ls, D_MODEL)
        q, k, v, z, la = _c_proj(x2, g_c, w_c_main, w_c_gk, w_up, bgk, tm)
        r3 = lambda a: a.reshape(B, Ls, a.shape[-1])
        oc, s_out = _gla_prompt(r3(q), r3(k), r3(v), r3(z), r3(la), onc, s_in, ct, n_lead)
        return x2, oc.reshape(B * Ls, C_VAL), s_out

    xpre = jnp.concatenate([jnp.zeros((LEAD, D_MODEL), F32), meta_tokens], axis=0)[None]
    zero_hist = jnp.zeros((WINDOW, 2 * A_KV_WIDTH), F32)
    y_pre, kv_pre, xb_pre, sdn_pre = ab_layer_prompt(
        xpre, CHUNK, CHUNK, CHUNK, zero_hist, -LEAD,
        jnp.zeros((1, 8, B_CONV_CH), F32), jnp.zeros((1, B_HEADS, B_DK, B_DV), F32), LEAD)
    _, _, sgla_pre = c_layer_prompt(y_pre, CHUNK, CHUNK, jnp.zeros((1, C_HEADS, C_DK, C_DV), F32), LEAD)

    hist_kv = jnp.concatenate([jnp.zeros((WINDOW - CHUNK, 2 * A_KV_WIDTH), F32), kv_pre[0]], axis=0)
    conv_in = jnp.broadcast_to(xb_pre[:, CHUNK - 8:, :], (Bp, 8, B_CONV_CH))
    y1, kv_main, xb_main, sdn_p = ab_layer_prompt(
        x_prompt, TM, TQ, CT, hist_kv, N_META, conv_in,
        jnp.broadcast_to(sdn_pre, (Bp, B_HEADS, B_DK, B_DV)), 0)
    y1_2d, oc, sgla_p = c_layer_prompt(y1, TM, CT, jnp.broadcast_to(sgla_pre, (Bp, C_HEADS, C_DK, C_DV)), 0)
    y_prompt = _c_out(y1_2d, oc, w_c_out, g_fin, TM).reshape(Bp, L, D_MODEL)

    swa_k_p = kv_main[:, L - WINDOW:, :A_KV_WIDTH].reshape(1, Bp, WINDOW, A_KV_HEADS, A_HD)
    swa_v_p = kv_main[:, L - WINDOW:, A_KV_WIDTH:].reshape(1, Bp, WINDOW, A_KV_HEADS, A_HD)
    dn_conv_p = xb_main[:, L - (CONV_W - 1):, :][None]

    xs = x_sample.reshape(Bs, D_MODEL)
    qa, kvn, za, xb, zb, gates, _ = _ab_proj(xs, g_ab, w_ab_main, w_ab_gate, w_ab_gate_t, Bs)
    oa3, nk, nv = _swa_sample(qa.reshape(Bs, A_HEADS, A_HD), kvn, za.reshape(Bs, A_HEADS, A_HD),
                              cache_swa_k[0].reshape(Bs, WINDOW, A_KV_WIDTH),
                              cache_swa_v[0].reshape(Bs, WINDOW, A_KV_WIDTH), sink_col, BT)
    hist = state_dn_conv[0]
    qn, kn, vn, bg = _gdn_sample_prep(xb, hist[:, 0], hist[:, 1], hist[:, 2], gates, conv_w8, gprm)
    ob, sdn_s = _gdn_sample(state_dn[0], _to_cols(qn, B_HEADS, B_DK, BT), _to_cols(kn, B_HEADS, B_DK, BT),
                            vn, bg, zb, onb, BT)
    ys = _ab_out(xs, oa3.reshape(Bs, A_WIDTH), ob, w_ab_out, Bs)
    qc, kc, vc, zc, la = _c_proj(ys, g_c, w_c_main, w_c_gk, w_up, bgk, Bs)
    ocs, sgla_s = _gla_sample(state_gla[0], _to_cols(qc, C_HEADS, C_DK, BT), _to_cols(kc, C_HEADS, C_DK, BT),
                              _to_cols(la, C_HEADS, C_DK, BT), vc, zc, onc, BT)
    y_sample = _c_out(ys, ocs, w_c_out, g_fin, Bs).reshape(Bs, 1, D_MODEL)
    dn_conv_s = jnp.concatenate([hist[:, 1:], xb[:, None, :]], axis=1)[None]

    return (y_prompt, y_sample, swa_k_p, swa_v_p, dn_conv_p, sdn_p[None], sgla_p[None],
            nk.reshape(1, Bs, WINDOW, A_KV_HEADS, A_HD), nv.reshape(1, Bs, WINDOW, A_KV_HEADS, A_HD),
            dn_conv_s, sdn_s[None], sgla_s[None])
```

```python
import functools

import jax
import jax.numpy as jnp
from jax import lax
from jax.experimental import pallas as pl
from jax.experimental.pallas import tpu as pltpu

F32 = jnp.float32
BF16 = jnp.bfloat16
EPS = 1e-6

D_MODEL = 1024
N_META = 16
CHUNK = 64
LEAD = (-N_META) % CHUNK
A_HEADS, A_KV_HEADS, A_HD = 8, 2, 64
A_GROUP = A_HEADS // A_KV_HEADS
A_WIDTH = A_HEADS * A_HD
A_KV_WIDTH = A_KV_HEADS * A_HD
WINDOW = 128
B_HEADS, B_DK, B_DV = 4, 128, 128
B_KEY = B_HEADS * B_DK
B_VAL = B_HEADS * B_DV
CONV_W = 4
B_CONV_CH = 2 * B_KEY + B_VAL
C_HEADS, C_DK, C_DV = 4, 128, 256
C_KEY = C_HEADS * C_DK
C_VAL = C_HEADS * C_DV
C_RANK = 16
C_GATE_NORM = 16.0
AB_MAIN = 2 * A_WIDTH + 2 * A_KV_WIDTH + B_CONV_CH + B_VAL
C_MAIN = 2 * C_KEY + 2 * C_VAL
LANES = 128
NEG = -1e30

VMEM_LIMIT = 48 * 1024 * 1024


def _cparams(sem):
    return pltpu.CompilerParams(dimension_semantics=sem, vmem_limit_bytes=VMEM_LIMIT)


def _mm(a, b):
    return jnp.dot(a.astype(BF16), b.astype(BF16), preferred_element_type=F32)


def _mm_nt(a, b):
    return lax.dot_general(a.astype(BF16), b.astype(BF16), (((1,), (1,)), ((), ())),
                           preferred_element_type=F32)


def _mm_tn(a, b):
    return lax.dot_general(a.astype(BF16), b.astype(BF16), (((0,), (0,)), ((), ())),
                           preferred_element_type=F32)


def _split3(a):
    hi = a.astype(BF16)
    r = a - hi.astype(F32)
    mid = r.astype(BF16)
    lo = (r - mid.astype(F32)).astype(BF16)
    return hi, mid, lo


def _sigmoid(x):
    return 1.0 / (1.0 + jnp.exp(-x))


def _silu(x):
    return x * _sigmoid(x)


def _softplus(x):
    return jnp.maximum(x, 0.0) + jnp.log1p(jnp.exp(-jnp.abs(x)))


def _log_sigmoid(x):
    return jnp.minimum(x, 0.0) - jnp.log1p(jnp.exp(-jnp.abs(x)))


def _rms(x, g):
    return x * lax.rsqrt(jnp.mean(x * x, axis=-1, keepdims=True) + EPS) * g


def _ab_proj_kernel(x_ref, g_ref, w_ref, wg_ref, wgt_ref,
                    qa_ref, kv_ref, za_ref, xb_ref, zb_ref, gates_ref, gatest_ref):
    hb = _rms(x_ref[...], g_ref[...]).astype(BF16)

    def proj(a, b):
        return jnp.dot(hb, w_ref[:, a:b], preferred_element_type=F32)

    o = 0
    for ref, n in ((qa_ref, A_WIDTH), (kv_ref, 2 * A_KV_WIDTH), (za_ref, A_WIDTH),
                   (xb_ref, B_CONV_CH), (zb_ref, B_VAL)):
        ref[...] = proj(o, o + n)
        o += n
    gates_ref[...] = jnp.dot(hb, wg_ref[...], preferred_element_type=F32)
    gatest_ref[...] = lax.dot_general(wgt_ref[...], hb, (((1,), (1,)), ((), ())),
                                      preferred_element_type=F32)


def _ab_proj(x2d, norm_g, w_main, w_gate, w_gate_t, tm):
    T = x2d.shape[0]
    row = lambda n: pl.BlockSpec((tm, n), lambda i: (i, 0))
    full = lambda a: pl.BlockSpec(a.shape, lambda i: (0,) * a.ndim)
    widths = (A_WIDTH, 2 * A_KV_WIDTH, A_WIDTH, B_CONV_CH, B_VAL, LANES)
    return pl.pallas_call(
        _ab_proj_kernel,
        grid=(T // tm,),
        in_specs=[row(D_MODEL), full(norm_g), full(w_main), full(w_gate), full(w_gate_t)],
        out_specs=[row(n) for n in widths] + [pl.BlockSpec((16, tm), lambda i: (0, i))],
        out_shape=[jax.ShapeDtypeStruct((T, n), F32) for n in widths]
        + [jax.ShapeDtypeStruct((16, T), F32)],
        compiler_params=_cparams(("arbitrary",)),
        name="ab_in_proj",
    )(x2d, norm_g, w_main, w_gate, w_gate_t)


def _swa_kernel(q_ref, kv_ref, kvp_ref, hist_ref, za_ref, sink_ref, o_ref, *, tq, sb, pos_shift):
    i = pl.program_id(1)
    prev = jnp.where(i == 0, hist_ref[...], kvp_ref[0])
    kvfull = jnp.concatenate([prev, kv_ref[0]], axis=0)
    r = lax.broadcasted_iota(jnp.int32, (sb, WINDOW + sb), 0)
    c = lax.broadcasted_iota(jnp.int32, (sb, WINDOW + sb), 1)
    diff = r - c + WINDOW
    in_window = (diff >= 0) & (diff < WINDOW)
    kvb = kvfull.astype(BF16)
    items = [(s, j) for s in range(tq // sb) for j in range(A_KV_HEADS)]
    masks = [jnp.concatenate([in_window & (i * tq + s * sb + c - WINDOW + pos_shift >= 0)] * A_GROUP, axis=0)
             for s in range(tq // sb)]
    sinks = [jnp.concatenate([jnp.broadcast_to(sink_ref[0:1, j * A_GROUP + g:j * A_GROUP + g + 1], (sb, 1))
                              for g in range(A_GROUP)], axis=0) for j in range(A_KV_HEADS)]
    qs = [jnp.concatenate([q_ref[0, s * sb:(s + 1) * sb, (j * A_GROUP + g) * A_HD:(j * A_GROUP + g + 1) * A_HD]
                           for g in range(A_GROUP)], axis=0).astype(BF16) for s, j in items]
    sc = [lax.dot_general(qs[n], kvb[s * sb:s * sb + WINDOW + sb, j * A_HD:(j + 1) * A_HD],
                          (((1,), (1,)), ((), ())), preferred_element_type=F32) * (A_HD ** -0.5)
          for n, (s, j) in enumerate(items)]
    sc = [jnp.where(masks[s], sc[n], NEG) for n, (s, j) in enumerate(items)]
    mx = [jnp.maximum(jnp.max(sc[n], axis=-1, keepdims=True), sinks[j]) for n, (s, j) in enumerate(items)]
    p = [jnp.exp(sc[n] - mx[n]) for n in range(len(items))]
    den = [jnp.sum(p[n], axis=-1, keepdims=True) + jnp.exp(sinks[j] - mx[n]) for n, (s, j) in enumerate(items)]
    pv = [jnp.dot(p[n].astype(BF16),
                  kvb[s * sb:s * sb + WINDOW + sb, A_KV_WIDTH + j * A_HD:A_KV_WIDTH + (j + 1) * A_HD],
                  preferred_element_type=F32) / den[n] for n, (s, j) in enumerate(items)]
    for s in range(tq // sb):
        o = jnp.concatenate([pv[s * A_KV_HEADS + j][g * sb:(g + 1) * sb]
                             for j in range(A_KV_HEADS) for g in range(A_GROUP)], axis=-1)
        o_ref[0, s * sb:(s + 1) * sb, :] = o * _silu(za_ref[0, s * sb:(s + 1) * sb, :])


def _swa_prompt(qa, kv, kv_prev_src, hist, za, sink, tq, pos_shift):
    B, L, _ = qa.shape
    sb = min(WINDOW, tq)
    nprev = tq // WINDOW if tq >= WINDOW else 1
    kern = functools.partial(_swa_kernel, tq=tq, sb=sb, pos_shift=pos_shift)
    return pl.pallas_call(
        kern,
        grid=(B, L // tq),
        in_specs=[
            pl.BlockSpec((1, tq, A_WIDTH), lambda b, i: (b, i, 0)),
            pl.BlockSpec((1, tq, 2 * A_KV_WIDTH), lambda b, i: (b, i, 0)),
            pl.BlockSpec((1, WINDOW, 2 * A_KV_WIDTH),
                         lambda b, i: (b, jnp.maximum(i * nprev - 1, 0), 0)),
            pl.BlockSpec(hist.shape, lambda b, i: (0, 0)),
            pl.BlockSpec((1, tq, A_WIDTH), lambda b, i: (b, i, 0)),
            pl.BlockSpec(sink.shape, lambda b, i: (0, 0)),
        ],
        out_specs=pl.BlockSpec((1, tq, A_WIDTH), lambda b, i: (b, i, 0)),
        out_shape=jax.ShapeDtypeStruct((B, L, A_WIDTH), F32),
        compiler_params=_cparams(("arbitrary", "arbitrary")),
        name="swa_prompt",
    )(qa, kv, kv_prev_src, hist, za, sink)


def _gdn_kernel(xb_ref, gates_ref, gatest_ref, zb_ref, convw_ref, gprm_ref, gprmt_ref, onorm_ref,
                convin_ref, sin_ref, ob_ref, sout_ref, s_scr, xc_scr, *, ct, n_lead):
    i = pl.program_id(1)
    nsteps = pl.num_programs(1)

    @pl.when(i == 0)
    def _():
        s_scr[...] = sin_ref[0]
        xc_scr[0:8, :] = convin_ref[0]

    xc_scr[8:8 + ct, :] = xb_ref[0]
    w = convw_ref[...]
    y = xc_scr[pl.ds(8 - (CONV_W - 1), ct), :] * w[0:1, :]
    for t in range(1, CONV_W):
        y = y + xc_scr[pl.ds(8 - (CONV_W - 1) + t, ct), :] * w[t:t + 1, :]
    cv = _silu(y)
    xc_scr[0:8, :] = xc_scr[ct:ct + 8, :]

    gt = gates_ref[0]
    beta = _sigmoid(gt)
    rows = i * ct + lax.broadcasted_iota(jnp.int32, gt.shape, 0)
    gcol = jnp.where(rows >= n_lead, -gprm_ref[0:1, :] * _softplus(gt + gprm_ref[1:2, :]), 0.0)
    gtt = gatest_ref[...]
    lanes = i * ct + lax.broadcasted_iota(jnp.int32, gtt.shape, 1)
    grow = jnp.where(lanes >= n_lead, -gprmt_ref[:, 0:1] * _softplus(gtt + gprmt_ref[:, 1:2]), 0.0)

    ii = lax.broadcasted_iota(jnp.int32, (CHUNK, CHUNK), 0)
    jj = lax.broadcasted_iota(jnp.int32, (CHUNK, CHUNK), 1)
    lower = ii >= jj
    strict = ii > jj
    ltri = lower.astype(BF16)
    utri = (ii <= jj).astype(BF16)

    nc = ct // CHUNK
    items = [(c, h) for c in range(nc) for h in range(B_HEADS)]
    n_items = len(items)
    qn, kn = [], []
    for h in range(B_HEADS):
        q = cv[:, h * B_DK:(h + 1) * B_DK]
        k = cv[:, B_KEY + h * B_DK:B_KEY + (h + 1) * B_DK]
        qn.append(q * lax.rsqrt(jnp.sum(q * q, axis=-1, keepdims=True) + EPS) * (B_DK ** -0.5))
        kn.append(k * lax.rsqrt(jnp.sum(k * k, axis=-1, keepdims=True) + EPS))
    gcum_c = [sum(jnp.dot(ltri, p, preferred_element_type=F32)
                  for p in _split3(gcol[c * CHUNK:(c + 1) * CHUNK])) for c in range(nc)]
    gcum_r = [sum(jnp.dot(p, utri, preferred_element_type=F32)
                  for p in _split3(grow[:, c * CHUNK:(c + 1) * CHUNK])) for c in range(nc)]
    rs = lambda c: slice(c * CHUNK, (c + 1) * CHUNK)
    q_ = [qn[h][rs(c)] for c, h in items]
    k_ = [kn[h][rs(c)] for c, h in items]
    v_ = [cv[rs(c), 2 * B_KEY + h * B_DV:2 * B_KEY + (h + 1) * B_DV] for c, h in items]
    bh = [beta[rs(c), h:h + 1] for c, h in items]
    gc = [gcum_c[c][:, B_HEADS + h:B_HEADS + h + 1] for c, h in items]
    gr = [gcum_r[c][B_HEADS + h:B_HEADS + h + 1, :] for c, h in items]
    decay = [jnp.exp(jnp.where(lower, gc[n] - gr[n], NEG)) for n in range(n_items)]
    kb = [k_[n] * bh[n] for n in range(n_items)]
    kbf = [k_[n].astype(BF16) for n in range(n_items)]
    kq = [lax.dot_general(jnp.concatenate([kb[n], q_[n]], axis=0).astype(BF16), kbf[n],
                          (((1,), (1,)), ((), ())), preferred_element_type=F32) for n in range(n_items)]
    a = [jnp.where(strict, kq[n][:CHUNK] * decay[n], 0.0) for n in range(n_items)]
    qk = [jnp.where(lower, kq[n][CHUNK:] * decay[n], 0.0).astype(BF16) for n in range(n_items)]
    doff = [-jnp.where((ii >> 1) == (jj >> 1), a[n], 0.0) for n in range(n_items)]
    for lg in range(1, 6):
        m = ((ii >> (lg + 1)) == (jj >> (lg + 1))) & ((ii >> lg) != (jj >> lg))
        bm = [jnp.where(m, a[n], 0.0) for n in range(n_items)]
        db = [doff[n].astype(BF16) for n in range(n_items)]
        y = [bm[n] + jnp.dot(db[n], bm[n].astype(BF16), preferred_element_type=F32) for n in range(n_items)]
        x = [y[n] + jnp.dot(y[n].astype(BF16), db[n], preferred_element_type=F32) for n in range(n_items)]
        doff = [doff[n] - x[n] for n in range(n_items)]
    egc = [jnp.exp(gc[n]) for n in range(n_items)]
    rhs = [jnp.concatenate([v_[n] * bh[n], kb[n] * egc[n]], axis=-1) for n in range(n_items)]
    sol = [rhs[n] + jnp.dot(doff[n].astype(BF16), rhs[n].astype(BF16), preferred_element_type=F32)
           for n in range(n_items)]
    g_last = [gc[n][CHUNK - 1:CHUNK, :] for n in range(n_items)]
    kd = [(k_[n] * jnp.exp(g_last[n] - gc[n])).astype(BF16) for n in range(n_items)]
    wq = [jnp.concatenate([sol[n][:, B_DV:], q_[n] * egc[n]], axis=0).astype(BF16) for n in range(n_items)]
    eg_last = [jnp.exp(g_last[n]) for n in range(n_items)]

    s_cur = [s_scr[h] for h in range(B_HEADS)]
    for c in range(nc):
        idx = [c * B_HEADS + h for h in range(B_HEADS)]
        ws = [jnp.dot(wq[n], s_cur[h].astype(BF16), preferred_element_type=F32) for h, n in enumerate(idx)]
        v_new = [sol[n][:, :B_DV] - ws[h][:CHUNK] for h, n in enumerate(idx)]
        vb = [v_new[h].astype(BF16) for h in range(B_HEADS)]
        o = [ws[h][CHUNK:] + jnp.dot(qk[n], vb[h], preferred_element_type=F32) for h, n in enumerate(idx)]
        s_cur = [s_cur[h] * eg_last[n]
                 + lax.dot_general(kd[n], vb[h], (((0,), (0,)), ((), ())), preferred_element_type=F32)
                 for h, n in enumerate(idx)]
        for h in range(B_HEADS):
            z = zb_ref[0, rs(c), h * B_DV:(h + 1) * B_DV]
            ob_ref[0, rs(c), h * B_DV:(h + 1) * B_DV] = _rms(o[h], onorm_ref[...]) * _silu(z)
    for h in range(B_HEADS):
        s_scr[h] = s_cur[h]

    @pl.when(i == nsteps - 1)
    def _():
        sout_ref[0] = s_scr[...]


def _gdn_prompt(xb, gates, gates_t, zb, conv_w8, gprm, gprm_t, onorm, conv_in, s_in, ct, n_lead):
    B, L, _ = xb.shape
    nt = L // ct
    kern = functools.partial(_gdn_kernel, ct=ct, n_lead=n_lead)
    full = lambda a: pl.BlockSpec(a.shape, lambda b, i: (0,) * a.ndim)
    return pl.pallas_call(
        kern,
        grid=(B, nt),
        in_specs=[
            pl.BlockSpec((1, ct, B_CONV_CH), lambda b, i: (b, i, 0)),
            pl.BlockSpec((1, ct, LANES), lambda b, i: (b, i, 0)),
            pl.BlockSpec((16, ct), lambda b, i: (0, b * nt + i)),
            pl.BlockSpec((1, ct, B_VAL), lambda b, i: (b, i, 0)),
            full(conv_w8), full(gprm), full(gprm_t), full(onorm),
            pl.BlockSpec((1, 8, B_CONV_CH), lambda b, i: (b, 0, 0)),
            pl.BlockSpec((1, B_HEADS, B_DK, B_DV), lambda b, i: (b, 0, 0, 0)),
        ],
        out_specs=[
            pl.BlockSpec((1, ct, B_VAL), lambda b, i: (b, i, 0)),
            pl.BlockSpec((1, B_HEADS, B_DK, B_DV), lambda b, i: (b, 0, 0, 0)),
        ],
        out_shape=[jax.ShapeDtypeStruct((B, L, B_VAL), F32),
                   jax.ShapeDtypeStruct((B, B_HEADS, B_DK, B_DV), F32)],
        scratch_shapes=[pltpu.VMEM((B_HEADS, B_DK, B_DV), F32),
                        pltpu.VMEM((ct + 8, B_CONV_CH), F32)],
        compiler_params=_cparams(("arbitrary", "arbitrary")),
        name="gdn_prompt",
    )(xb, gates, gates_t, zb, conv_w8, gprm, gprm_t, onorm, conv_in, s_in)


def _ab_out_kernel(x_ref, oa_ref, ob_ref, w_ref, y_ref):
    y_ref[...] = (x_ref[...]
                  + jnp.dot(oa_ref[...].astype(BF16), w_ref[0:A_WIDTH, :], preferred_element_type=F32)
                  + jnp.dot(ob_ref[...].astype(BF16), w_ref[A_WIDTH:, :], preferred_element_type=F32))


def _ab_out(x2d, oa, ob, w_out, tm):
    T = x2d.shape[0]
    row = lambda n: pl.BlockSpec((tm, n), lambda i: (i, 0))
    return pl.pallas_call(
        _ab_out_kernel,
        grid=(T // tm,),
        in_specs=[row(D_MODEL), row(A_WIDTH), row(B_VAL),
                  pl.BlockSpec(w_out.shape, lambda i: (0, 0))],
        out_specs=row(D_MODEL),
        out_shape=jax.ShapeDtypeStruct((T, D_MODEL), F32),
        compiler_params=_cparams(("arbitrary",)),
        name="ab_out_proj",
    )(x2d, oa, ob, w_out)


def _c_out_kernel(x_ref, o_ref, w_ref, g_ref, y_ref):
    y = x_ref[...] + jnp.dot(o_ref[...].astype(BF16), w_ref[...], preferred_element_type=F32)
    y_ref[...] = _rms(y, g_ref[...])


def _c_out(x2d, oc, w_out, final_g, tm):
    T = x2d.shape[0]
    row = lambda n: pl.BlockSpec((tm, n), lambda i: (i, 0))
    return pl.pallas_call(
        _c_out_kernel,
        grid=(T // tm,),
        in_specs=[row(D_MODEL), row(C_VAL), pl.BlockSpec(w_out.shape, lambda i: (0, 0)),
                  pl.BlockSpec(final_g.shape, lambda i: (0, 0))],
        out_specs=row(D_MODEL),
        out_shape=jax.ShapeDtypeStruct((T, D_MODEL), F32),
        compiler_params=_cparams(("arbitrary",)),
        name="c_out_proj_norm",
    )(x2d, oc, w_out, final_g)


def _c_proj_kernel(x_ref, g_ref, w_ref, wgk_ref, wup_ref, bgk_ref,
                   q_ref, k_ref, v_ref, z_ref, la_ref):
    hb = _rms(x_ref[...], g_ref[...]).astype(BF16)
    o = 0
    for ref, n in ((q_ref, C_KEY), (k_ref, C_KEY), (v_ref, C_VAL), (z_ref, C_VAL)):
        ref[...] = jnp.dot(hb, w_ref[:, o:o + n], preferred_element_type=F32)
        o += n
    low = jnp.dot(hb, wgk_ref[...], preferred_element_type=F32)
    lh, lm, _ = _split3(low)
    wh, wm, _ = _split3(wup_ref[...])
    up = (jnp.dot(lh, wh, preferred_element_type=F32) + jnp.dot(lh, wm, preferred_element_type=F32)
          + jnp.dot(lm, wh, preferred_element_type=F32))
    la_ref[...] = _log_sigmoid(up + bgk_ref[...]) * (1.0 / C_GATE_NORM)


def _c_proj(x2d, norm_g, w_main, w_gk, w_up, b_gk, tm):
    T = x2d.shape[0]
    row = lambda n: pl.BlockSpec((tm, n), lambda i: (i, 0))
    full = lambda a: pl.BlockSpec(a.shape, lambda i: (0,) * a.ndim)
    widths = (C_KEY, C_KEY, C_VAL, C_VAL, C_KEY)
    return pl.pallas_call(
        _c_proj_kernel,
        grid=(T // tm,),
        in_specs=[row(D_MODEL), full(norm_g), full(w_main), full(w_gk), full(w_up), full(b_gk)],
        out_specs=[row(n) for n in widths],
        out_shape=[jax.ShapeDtypeStruct((T, n), F32) for n in widths],
        compiler_params=_cparams(("arbitrary",)),
        name="c_in_proj",
    )(x2d, norm_g, w_main, w_gk, w_up, b_gk)


def _gla_kernel(q_ref, k_ref, v_ref, z_ref, la_ref, onorm_ref, sin_ref, o_ref, sout_ref, s_scr,
                *, ct, n_lead):
    i = pl.program_id(1)
    nsteps = pl.num_programs(1)

    @pl.when(i == 0)
    def _():
        s_scr[...] = sin_ref[0]

    ii = lax.broadcasted_iota(jnp.int32, (CHUNK, CHUNK), 0)
    jj = lax.broadcasted_iota(jnp.int32, (CHUNK, CHUNK), 1)
    lower = ii >= jj
    ltri = lower.astype(BF16)
    ek = lax.broadcasted_iota(jnp.int32, (C_DK, C_DK), 0) == lax.broadcasted_iota(jnp.int32, (C_DK, C_DK), 1)
    ones_v = jnp.ones((C_DV, C_DK), BF16)

    la_all = la_ref[0]
    rows = i * ct + lax.broadcasted_iota(jnp.int32, la_all.shape, 0)
    la_all = jnp.where(rows >= n_lead, la_all, 0.0)

    for c in range(ct // CHUNK):
        r0 = c * CHUNK
        bc_all = sum(jnp.dot(ltri, p, preferred_element_type=F32)
                     for p in _split3(la_all[r0:r0 + CHUNK]))
        for h in range(C_HEADS):
            q = q_ref[0, r0:r0 + CHUNK, h * C_DK:(h + 1) * C_DK] * (C_DK ** -0.5)
            k = k_ref[0, r0:r0 + CHUNK, h * C_DK:(h + 1) * C_DK]
            v = v_ref[0, r0:r0 + CHUNK, h * C_DV:(h + 1) * C_DV]
            bc = bc_all[:, h * C_DK:(h + 1) * C_DK]
            qd = q * jnp.exp(bc)
            qk = jnp.where(lower, _mm_nt(qd, k * jnp.exp(-bc)), 0.0)
            s = s_scr[h]
            o = _mm(qk, v) + _mm(qd, s)
            bl = bc[CHUNK - 1:CHUNK, :]
            kd = k * jnp.exp(bl - bc)
            gl = sum(lax.dot_general(p, ones_v, (((1,), (1,)), ((), ())), preferred_element_type=F32)
                     for p in _split3(jnp.where(ek, jnp.exp(bl), 0.0)))
            s_scr[h] = s * gl + _mm_tn(kd, v)
            z = z_ref[0, r0:r0 + CHUNK, h * C_DV:(h + 1) * C_DV]
            o_ref[0, r0:r0 + CHUNK, h * C_DV:(h + 1) * C_DV] = _rms(o, onorm_ref[...]) * _silu(z)

    @pl.when(i == nsteps - 1)
    def _():
        sout_ref[0] = s_scr[...]


def _gla_prompt(q, k, v, z, la, onorm, s_in, ct, n_lead):
    B, L, _ = q.shape
    kern = functools.partial(_gla_kernel, ct=ct, n_lead=n_lead)
    blk = lambda n: pl.BlockSpec((1, ct, n), lambda b, i: (b, i, 0))
    st = pl.BlockSpec((1, C_HEADS, C_DK, C_DV), lambda b, i: (b, 0, 0, 0))
    return pl.pallas_call(
        kern,
        grid=(B, L // ct),
        in_specs=[blk(C_KEY), blk(C_KEY), blk(C_VAL), blk(C_VAL), blk(C_KEY),
                  pl.BlockSpec(onorm.shape, lambda b, i: (0, 0)), st],
        out_specs=[blk(C_VAL), st],
        out_shape=[jax.ShapeDtypeStruct((B, L, C_VAL), F32),
                   jax.ShapeDtypeStruct((B, C_HEADS, C_DK, C_DV), F32)],
        scratch_shapes=[pltpu.VMEM((C_HEADS, C_DK, C_DV), F32)],
        compiler_params=_cparams(("arbitrary", "arbitrary")),
        name="gla_prompt",
    )(q, k, v, z, la, onorm, s_in)


def _swa_sample_kernel(q_ref, kvn_ref, za_ref, ck_ref, cv_ref, sink_ref, o_ref, nk_ref, nv_ref, *, bt):
    row = lax.broadcasted_iota(jnp.int32, (WINDOW, A_KV_WIDTH), 0)
    hrow = lax.broadcasted_iota(jnp.int32, (A_HEADS, A_HD), 0)
    sk = sink_ref[...]
    for b in range(bt):
        kn = kvn_ref[b:b + 1, 0:A_KV_WIDTH]
        vn = kvn_ref[b:b + 1, A_KV_WIDTH:]
        nk = jnp.where(row == WINDOW - 1, kn, pltpu.roll(ck_ref[b], WINDOW - 1, 0))
        nv = jnp.where(row == WINDOW - 1, vn, pltpu.roll(cv_ref[b], WINDOW - 1, 0))
        nk_ref[b] = nk
        nv_ref[b] = nv
        q8 = q_ref[b]
        o8 = None
        for j in range(A_KV_HEADS):
            sc = _mm_nt(q8, nk[:, j * A_HD:(j + 1) * A_HD]) * (A_HD ** -0.5)
            m = jnp.maximum(jnp.max(sc, axis=-1, keepdims=True), sk)
            p = jnp.exp(sc - m)
            den = jnp.sum(p, axis=-1, keepdims=True) + jnp.exp(sk - m)
            oj = _mm(p, nv[:, j * A_HD:(j + 1) * A_HD]) / den
            o8 = oj if o8 is None else jnp.where(hrow >= j * A_GROUP, oj, o8)
        o_ref[b] = o8 * _silu(za_ref[b])


def _swa_sample(q3, kv_new, za3, cache_k, cache_v, sink_col, bt):
    nb = q3.shape[0]
    hd = pl.BlockSpec((bt, A_HEADS, A_HD), lambda i: (i, 0, 0))
    cache = pl.BlockSpec((bt, WINDOW, A_KV_WIDTH), lambda i: (i, 0, 0))
    return pl.pallas_call(
        functools.partial(_swa_sample_kernel, bt=bt),
        grid=(nb // bt,),
        in_specs=[hd, pl.BlockSpec((bt, 2 * A_KV_WIDTH), lambda i: (i, 0)), hd, cache, cache,
                  pl.BlockSpec(sink_col.shape, lambda i: (0, 0))],
        out_specs=[hd, cache, cache],
        out_shape=[jax.ShapeDtypeStruct(q3.shape, F32),
                   jax.ShapeDtypeStruct(cache_k.shape, F32),
                   jax.ShapeDtypeStruct(cache_v.shape, F32)],
        compiler_params=_cparams(("arbitrary",)),
        name="swa_sample",
    )(q3, kv_new, za3, cache_k, cache_v, sink_col)


def _gdn_sample_prep_kernel(xb_ref, h0_ref, h1_ref, h2_ref, gates_ref, convw_ref, gprm_ref,
                            q_ref, k_ref, v_ref, bg_ref):
    w = convw_ref[...]
    y = (h0_ref[...] * w[0:1, :] + h1_ref[...] * w[1:2, :] + h2_ref[...] * w[2:3, :]
         + xb_ref[...] * w[3:4, :])
    cv = _silu(y)
    for h in range(B_HEADS):
        q = cv[:, h * B_DK:(h + 1) * B_DK]
        k = cv[:, B_KEY + h * B_DK:B_KEY + (h + 1) * B_DK]
        q_ref[:, h * B_DK:(h + 1) * B_DK] = (
            q * lax.rsqrt(jnp.sum(q * q, axis=-1, keepdims=True) + EPS) * (B_DK ** -0.5))
        k_ref[:, h * B_DK:(h + 1) * B_DK] = k * lax.rsqrt(jnp.sum(k * k, axis=-1, keepdims=True) + EPS)
    v_ref[...] = cv[:, 2 * B_KEY:]
    gt = gates_ref[...]
    col = lax.broadcasted_iota(jnp.int32, gt.shape, 1)
    g = -gprm_ref[0:1, :] * _softplus(gt + gprm_ref[1:2, :])
    bg_ref[...] = jnp.where(col < B_HEADS, _sigmoid(gt), jnp.exp(g))


def _gdn_sample_prep(xb, h0, h1, h2, gates, conv_w8, gprm):
    n = xb.shape[0]
    full = lambda a: pl.BlockSpec(a.shape, lambda: (0,) * a.ndim)
    args = (xb, h0, h1, h2, gates, conv_w8, gprm)
    return pl.pallas_call(
        _gdn_sample_prep_kernel,
        in_specs=[full(a) for a in args],
        out_specs=[pl.BlockSpec((n, B_KEY), lambda: (0, 0))] * 3 + [pl.BlockSpec((n, LANES), lambda: (0, 0))],
        out_shape=[jax.ShapeDtypeStruct((n, B_KEY), F32)] * 3 + [jax.ShapeDtypeStruct((n, LANES), F32)],
        compiler_params=pltpu.CompilerParams(vmem_limit_bytes=VMEM_LIMIT),
        name="gdn_sample_prep",
    )(*args)


def _gdn_sample_kernel(s_ref, qt_ref, kt_ref, v_ref, bg_ref, zb_ref, onorm_ref, o_ref, sout_ref, *, bt):
    for b in range(bt):
        for h in range(B_HEADS):
            s = s_ref[b, h]
            kc = kt_ref[0, h, :, b:b + 1]
            qc = qt_ref[0, h, :, b:b + 1]
            v = v_ref[b:b + 1, h * B_DV:(h + 1) * B_DV]
            beta = bg_ref[b:b + 1, h:h + 1]
            eg = bg_ref[b:b + 1, B_HEADS + h:B_HEADS + h + 1]
            ks = jnp.sum(kc * s, axis=0, keepdims=True)
            v_new = beta * (v - eg * ks)
            s_new = eg * s + kc * v_new
            sout_ref[b, h] = s_new
            o = jnp.sum(qc * s_new, axis=0, keepdims=True)
            z = zb_ref[b:b + 1, h * B_DV:(h + 1) * B_DV]
            o_ref[b:b + 1, h * B_DV:(h + 1) * B_DV] = _rms(o, onorm_ref[...]) * _silu(z)


def _gdn_sample(state, qt, kt, v, bg, zb, onorm, bt):
    nb = state.shape[0]
    st = pl.BlockSpec((bt, B_HEADS, B_DK, B_DV), lambda i: (i, 0, 0, 0))
    col = pl.BlockSpec((1, B_HEADS, B_DK, bt), lambda i: (i, 0, 0, 0))
    row = lambda n: pl.BlockSpec((bt, n), lambda i: (i, 0))
    return pl.pallas_call(
        functools.partial(_gdn_sample_kernel, bt=bt),
        grid=(nb // bt,),
        in_specs=[st, col, col, row(B_VAL), row(LANES), row(B_VAL),
                  pl.BlockSpec(onorm.shape, lambda i: (0, 0))],
        out_specs=[row(B_VAL), st],
        out_shape=[jax.ShapeDtypeStruct((nb, B_VAL), F32), jax.ShapeDtypeStruct(state.shape, F32)],
        compiler_params=_cparams(("arbitrary",)),
        name="gdn_sample",
    )(state, qt, kt, v, bg, zb, onorm)


def _gla_sample_kernel(s_ref, qt_ref, kt_ref, lat_ref, v_ref, z_ref, onorm_ref, o_ref, sout_ref, *, bt):
    for b in range(bt):
        for h in range(C_HEADS):
            s = s_ref[b, h]
            kc = kt_ref[0, h, :, b:b + 1]
            qc = qt_ref[0, h, :, b:b + 1] * (C_DK ** -0.5)
            ac = jnp.exp(lat_ref[0, h, :, b:b + 1])
            v = v_ref[b:b + 1, h * C_DV:(h + 1) * C_DV]
            s_new = ac * s + kc * v
            sout_ref[b, h] = s_new
            o = jnp.sum(qc * s_new, axis=0, keepdims=True)
            z = z_ref[b:b + 1, h * C_DV:(h + 1) * C_DV]
            o_ref[b:b + 1, h * C_DV:(h + 1) * C_DV] = _rms(o, onorm_ref[...]) * _silu(z)


def _gla_sample(state, qt, kt, lat, v, z, onorm, bt):
    nb = state.shape[0]
    st = pl.BlockSpec((bt, C_HEADS, C_DK, C_DV), lambda i: (i, 0, 0, 0))
    col = pl.BlockSpec((1, C_HEADS, C_DK, bt), lambda i: (i, 0, 0, 0))
    row = lambda n: pl.BlockSpec((bt, n), lambda i: (i, 0))
    return pl.pallas_call(
        functools.partial(_gla_sample_kernel, bt=bt),
        grid=(nb // bt,),
        in_specs=[st, col, col, col, row(C_VAL), row(C_VAL),
                  pl.BlockSpec(onorm.shape, lambda i: (0, 0))],
        out_specs=[row(C_VAL), st],
        out_shape=[jax.ShapeDtypeStruct((nb, C_VAL), F32), jax.ShapeDtypeStruct(state.shape, F32)],
        compiler_params=_cparams(("arbitrary",)),
        name="gla_sample",
    )(state, qt, kt, lat, v, z, onorm)


def _to_cols(a, heads, dk, bt):
    n = a.shape[0]
    return a.reshape(n // bt, bt, heads, dk).transpose(0, 2, 3, 1)


def kernel(x_prompt, x_sample, cache_swa_k, cache_swa_v, state_dn_conv, state_dn, state_gla,
           meta_tokens, norm_ab, w_in_ab, sink_a, conv_b, a_log_b, dt_bias_b, onorm_b, w_out_ab,
           norm_c, w_in_c, w_gk_up, b_gk, onorm_c, w_out_c, final_norm):
    Bp, L, _ = x_prompt.shape
    Bs = x_sample.shape[0]
    TM, TQ, CT, BT = 256, 512, 256, 8

    w_ab = w_in_ab[0]
    w_ab_main = w_ab[:, :AB_MAIN].astype(BF16)
    w_ab_gate = jnp.pad(w_ab[:, AB_MAIN:], ((0, 0), (0, LANES - 2 * B_HEADS))).astype(BF16)
    w_ab_gate_t = jnp.pad(w_ab[:, AB_MAIN:].T, ((0, 16 - 2 * B_HEADS), (0, 0))).astype(BF16)
    g_ab = norm_ab[0][None, :]
    sink_row = sink_a[0][None, :]
    sink_col = sink_a[0][:, None]
    conv_w8 = jnp.pad(conv_b[0], ((0, 8 - CONV_W), (0, 0)))
    ea = jnp.exp(a_log_b[0])
    gprm = jnp.zeros((8, LANES), F32).at[0, B_HEADS:2 * B_HEADS].set(ea).at[1, B_HEADS:2 * B_HEADS].set(dt_bias_b[0])
    gprm_t = jnp.zeros((16, LANES), F32).at[B_HEADS:2 * B_HEADS, 0].set(ea).at[B_HEADS:2 * B_HEADS, 1].set(dt_bias_b[0])
    onb = onorm_b[0][None, :]
    w_ab_out = w_out_ab[0].astype(BF16)
    w_c = w_in_c[0]
    w_c_main = w_c[:, :C_MAIN].astype(BF16)
    w_c_gk = jnp.pad(w_c[:, C_MAIN:], ((0, 0), (0, LANES - C_RANK))).astype(BF16)
    w_up = jnp.pad(w_gk_up[0], ((0, LANES - C_RANK), (0, 0)))
    bgk = b_gk[0][None, :]
    g_c = norm_c[0][None, :]
    onc = onorm_c[0][None, :]
    w_c_out = w_out_c[0].astype(BF16)
    g_fin = final_norm[None, :]

    def ab_layer_prompt(x3, tm, tq, ct, hist_kv, pos_shift, conv_in, s_in, n_lead):
        B, Ls, _ = x3.shape
        x2 = x3.reshape(B * Ls, D_MODEL)
        qa, kv, za, xb, zb, gates, gates_t = _ab_proj(x2, g_ab, w_ab_main, w_ab_gate, w_ab_gate_t, tm)
        r3 = lambda a: a.reshape(B, Ls, a.shape[-1])
        kv3 = r3(kv)
        kv_prev_src = kv3 if Ls >= WINDOW else hist_kv[None]
        oa = _swa_prompt(r3(qa), kv3, kv_prev_src, hist_kv, r3(za), sink_row, tq, pos_shift)
        ob, s_out = _gdn_prompt(r3(xb), r3(gates), gates_t, r3(zb), conv_w8, gprm, gprm_t, onb,
                                conv_in, s_in, ct, n_lead)
        y = _ab_out(x2, oa.reshape(B * Ls, A_WIDTH), ob.reshape(B * Ls, B_VAL), w_ab_out, tm)
        return y.reshape(B, Ls, D_MODEL), kv3, r3(xb), s_out

    def c_layer_prompt(x3, tm, ct, s_in, n_lead):
        B, Ls, _ = x3.shape
        x2 = x3.reshape(B * Ls, D_MODEL)
        q, k, v, z, la = _c_proj(x2, g_c, w_c_main, w_c_gk, w_up, bgk, tm)
        r3 = lambda a: a.reshape(B, Ls, a.shape[-1])
        oc, s_out = _gla_prompt(r3(q), r3(k), r3(v), r3(z), r3(la), onc, s_in, ct, n_lead)
        return x2, oc.reshape(B * Ls, C_VAL), s_out

    xpre = jnp.concatenate([jnp.zeros((LEAD, D_MODEL), F32), meta_tokens], axis=0)[None]
    zero_hist = jnp.zeros((WINDOW, 2 * A_KV_WIDTH), F32)
    y_pre, kv_pre, xb_pre, sdn_pre = ab_layer_prompt(
        xpre, CHUNK, CHUNK, CHUNK, zero_hist, -LEAD,
        jnp.zeros((1, 8, B_CONV_CH), F32), jnp.zeros((1, B_HEADS, B_DK, B_DV), F32), LEAD)
    _, _, sgla_pre = c_layer_prompt(y_pre, CHUNK, CHUNK, jnp.zeros((1, C_HEADS, C_DK, C_DV), F32), LEAD)

    hist_kv = jnp.concatenate([jnp.zeros((WINDOW - CHUNK, 2 * A_KV_WIDTH), F32), kv_pre[0]], axis=0)
    conv_in = jnp.broadcast_to(xb_pre[:, CHUNK - 8:, :], (Bp, 8, B_CONV_CH))
    y1, kv_main, xb_main, sdn_p = ab_layer_prompt(
        x_prompt, TM, TQ, CT, hist_kv, N_META, conv_in,
        jnp.broadcast_to(sdn_pre, (Bp, B_HEADS, B_DK, B_DV)), 0)
    y1_2d, oc, sgla_p = c_layer_prompt(y1, TM, CT, jnp.broadcast_to(sgla_pre, (Bp, C_HEADS, C_DK, C_DV)), 0)
    y_prompt = _c_out(y1_2d, oc, w_c_out, g_fin, TM).reshape(Bp, L, D_MODEL)

    swa_k_p = kv_main[:, L - WINDOW:, :A_KV_WIDTH].reshape(1, Bp, WINDOW, A_KV_HEADS, A_HD)
    swa_v_p = kv_main[:, L - WINDOW:, A_KV_WIDTH:].reshape(1, Bp, WINDOW, A_KV_HEADS, A_HD)
    dn_conv_p = xb_main[:, L - (CONV_W - 1):, :][None]

    xs = x_sample.reshape(Bs, D_MODEL)
    qa, kvn, za, xb, zb, gates, _ = _ab_proj(xs, g_ab, w_ab_main, w_ab_gate, w_ab_gate_t, Bs)
    oa3, nk, nv = _swa_sample(qa.reshape(Bs, A_HEADS, A_HD), kvn, za.reshape(Bs, A_HEADS, A_HD),
                              cache_swa_k[0].reshape(Bs, WINDOW, A_KV_WIDTH),
                              cache_swa_v[0].reshape(Bs, WINDOW, A_KV_WIDTH), sink_col, BT)
    hist = state_dn_conv[0]
    qn, kn, vn, bg = _gdn_sample_prep(xb, hist[:, 0], hist[:, 1], hist[:, 2], gates, conv_w8, gprm)
    ob, sdn_s = _gdn_sample(state_dn[0], _to_cols(qn, B_HEADS, B_DK, BT), _to_cols(kn, B_HEADS, B_DK, BT),
                            vn, bg, zb, onb, BT)
    ys = _ab_out(xs, oa3.reshape(Bs, A_WIDTH), ob, w_ab_out, Bs)
    qc, kc, vc, zc, la = _c_proj(ys, g_c, w_c_main, w_c_gk, w_up, bgk, Bs)
    ocs, sgla_s = _gla_sample(state_gla[0], _to_cols(qc, C_HEADS, C_DK, BT), _to_cols(kc, C_HEADS, C_DK, BT),
                              _to_cols(la, C_HEADS, C_DK, BT), vc, zc, onc, BT)
    y_sample = _c_out(ys, ocs, w_c_out, g_fin, Bs).reshape(Bs, 1, D_MODEL)
    dn_conv_s = jnp.concatenate([hist[:, 1:], xb[:, None, :]], axis=1)[None]

    return (y_prompt, y_sample, swa_k_p, swa_v_p, dn_conv_p, sdn_p[None], sgla_p[None],
            nk.reshape(1, Bs, WINDOW, A_KV_HEADS, A_HD), nv.reshape(1, Bs, WINDOW, A_KV_HEADS, A_HD),
            dn_conv_s, sdn_s[None], sgla_s[None])
```

```python
import functools

import jax
import jax.numpy as jnp
from jax import lax
from jax.experimental import pallas as pl
from jax.experimental.pallas import tpu as pltpu

F32 = jnp.float32
BF16 = jnp.bfloat16
EPS = 1e-6

D_MODEL = 1024
N_META = 16
CHUNK = 64
LEAD = (-N_META) % CHUNK
A_HEADS, A_KV_HEADS, A_HD = 8, 2, 64
A_GROUP = A_HEADS // A_KV_HEADS
A_WIDTH = A_HEADS * A_HD
A_KV_WIDTH = A_KV_HEADS * A_HD
WINDOW = 128
B_HEADS, B_DK, B_DV = 4, 128, 128
B_KEY = B_HEADS * B_DK
B_VAL = B_HEADS * B_DV
CONV_W = 4
B_CONV_CH = 2 * B_KEY + B_VAL
C_HEADS, C_DK, C_DV = 4, 128, 256
C_KEY = C_HEADS * C_DK
C_VAL = C_HEADS * C_DV
C_RANK = 16
C_GATE_NORM = 16.0
AB_MAIN = 2 * A_WIDTH + 2 * A_KV_WIDTH + B_CONV_CH + B_VAL
AB_MIX = A_WIDTH + B_VAL
C_MAIN = 2 * C_KEY + 2 * C_VAL
O_QA, O_KV, O_ZA, O_XB, O_ZB = 0, 512, 768, 1280, 2816
LANES = 128
NEG = -1e30

VMEM_LIMIT = 56 * 1024 * 1024

NT_DIMS = (((1,), (1,)), ((), ()))
TN_DIMS = (((0,), (0,)), ((), ()))


def _cparams(sem):
    return pltpu.CompilerParams(dimension_semantics=sem, vmem_limit_bytes=VMEM_LIMIT)


def _dot(a, b):
    return jnp.dot(a, b, preferred_element_type=F32)


def _dotg(a, b, dims):
    return lax.dot_general(a, b, dims, preferred_element_type=F32)


def _split3(a):
    hi = a.astype(BF16)
    r = a - hi.astype(F32)
    mid = r.astype(BF16)
    lo = (r - mid.astype(F32)).astype(BF16)
    return hi, mid, lo


def _sigmoid(x):
    return 1.0 / (1.0 + jnp.exp(-x))


def _silu(x):
    return x * _sigmoid(x)


def _softplus(x):
    return jnp.maximum(x, 0.0) + jnp.log1p(jnp.exp(-jnp.abs(x)))


def _log_sigmoid(x):
    return jnp.minimum(x, 0.0) - jnp.log1p(jnp.exp(-jnp.abs(x)))


def _rms(x, g):
    return x * lax.rsqrt(jnp.mean(x * x, axis=-1, keepdims=True) + EPS) * g


def _gla_log_decay(hb, wgk_ref, wup_ref, bgk_ref):
    low = _dot(hb, wgk_ref[...])
    lh, lm, _ = _split3(low)
    wh, wm, _ = _split3(wup_ref[...])
    up = _dot(lh, wh) + _dot(lh, wm) + _dot(lm, wh)
    return _log_sigmoid(up + bgk_ref[...]) * (1.0 / C_GATE_NORM)


def _swa_tile(i, qa, kvb, za, sink_ref, mix_scr, *, ct, sb, pos_shift):
    r = lax.broadcasted_iota(jnp.int32, (sb, WINDOW + sb), 0)
    c = lax.broadcasted_iota(jnp.int32, (sb, WINDOW + sb), 1)
    diff = r - c + WINDOW
    in_window = (diff >= 0) & (diff < WINDOW)
    nsb = ct // sb
    items = [(s, j) for s in range(nsb) for j in range(A_KV_HEADS)]
    masks = [jnp.concatenate([in_window & (i * ct + s * sb + c - WINDOW + pos_shift >= 0)] * A_GROUP, axis=0)
             for s in range(nsb)]
    sinks = [jnp.concatenate([jnp.broadcast_to(sink_ref[0:1, j * A_GROUP + g:j * A_GROUP + g + 1], (sb, 1))
                              for g in range(A_GROUP)], axis=0) for j in range(A_KV_HEADS)]
    qs = [jnp.concatenate([qa[s * sb:(s + 1) * sb, (j * A_GROUP + g) * A_HD:(j * A_GROUP + g + 1) * A_HD]
                           for g in range(A_GROUP)], axis=0).astype(BF16) for s, j in items]
    sc = [_dotg(qs[n], kvb[s * sb:s * sb + WINDOW + sb, j * A_HD:(j + 1) * A_HD], NT_DIMS) * (A_HD ** -0.5)
          for n, (s, j) in enumerate(items)]
    sc = [jnp.where(masks[s], sc[n], NEG) for n, (s, j) in enumerate(items)]
    mx = [jnp.maximum(jnp.max(sc[n], axis=-1, keepdims=True), sinks[j]) for n, (s, j) in enumerate(items)]
    p = [jnp.exp(sc[n] - mx[n]) for n in range(len(items))]
    den = [jnp.sum(p[n], axis=-1, keepdims=True) + jnp.exp(sinks[j] - mx[n]) for n, (s, j) in enumerate(items)]
    pv = [_dot(p[n].astype(BF16),
               kvb[s * sb:s * sb + WINDOW + sb, A_KV_WIDTH + j * A_HD:A_KV_WIDTH + (j + 1) * A_HD]) / den[n]
          for n, (s, j) in enumerate(items)]
    for s in range(nsb):
        o = jnp.concatenate([pv[s * A_KV_HEADS + j][g * sb:(g + 1) * sb]
                             for j in range(A_KV_HEADS) for g in range(A_GROUP)], axis=-1)
        mix_scr[s * sb:(s + 1) * sb, 0:A_WIDTH] = (o * _silu(za[s * sb:(s + 1) * sb])).astype(BF16)


def _gdn_tile(i, cv, gt, gtt, zb, gprm_ref, gprmt_ref, onorm_ref, s_scr, mix_scr, *, ct, n_lead):
    beta = _sigmoid(gt)
    rows = i * ct + lax.broadcasted_iota(jnp.int32, gt.shape, 0)
    gcol = jnp.where(rows >= n_lead, -gprm_ref[0:1, :] * _softplus(gt + gprm_ref[1:2, :]), 0.0)
    lanes = i * ct + lax.broadcasted_iota(jnp.int32, gtt.shape, 1)
    grow = jnp.where(lanes >= n_lead, -gprmt_ref[:, 0:1] * _softplus(gtt + gprmt_ref[:, 1:2]), 0.0)

    ii = lax.broadcasted_iota(jnp.int32, (CHUNK, CHUNK), 0)
    jj = lax.broadcasted_iota(jnp.int32, (CHUNK, CHUNK), 1)
    lower = ii >= jj
    strict = ii > jj
    ltri = lower.astype(BF16)
    utri = (ii <= jj).astype(BF16)

    nc = ct // CHUNK
    items = [(c, h) for c in range(nc) for h in range(B_HEADS)]
    n_items = len(items)
    rs = lambda c: slice(c * CHUNK, (c + 1) * CHUNK)
    qn, kn = [], []
    for h in range(B_HEADS):
        q = cv[:, h * B_DK:(h + 1) * B_DK]
        k = cv[:, B_KEY + h * B_DK:B_KEY + (h + 1) * B_DK]
        qn.append(q * lax.rsqrt(jnp.sum(q * q, axis=-1, keepdims=True) + EPS) * (B_DK ** -0.5))
        kn.append(k * lax.rsqrt(jnp.sum(k * k, axis=-1, keepdims=True) + EPS))
    gcum_c = [sum(_dot(ltri, p) for p in _split3(gcol[rs(c)])) for c in range(nc)]
    gcum_r = [sum(_dot(p, utri) for p in _split3(grow[:, rs(c)])) for c in range(nc)]
    q_ = [qn[h][rs(c)] for c, h in items]
    k_ = [kn[h][rs(c)] for c, h in items]
    v_ = [cv[rs(c), 2 * B_KEY + h * B_DV:2 * B_KEY + (h + 1) * B_DV] for c, h in items]
    bh = [beta[rs(c), h:h + 1] for c, h in items]
    gc = [gcum_c[c][:, B_HEADS + h:B_HEADS + h + 1] for c, h in items]
    gr = [gcum_r[c][B_HEADS + h:B_HEADS + h + 1, :] for c, h in items]
    decay = [jnp.exp(jnp.where(lower, gc[n] - gr[n], NEG)) for n in range(n_items)]
    kb = [k_[n] * bh[n] for n in range(n_items)]
    kbf = [k_[n].astype(BF16) for n in range(n_items)]
    kq = [_dotg(jnp.concatenate([kb[n], q_[n]], axis=0).astype(BF16), kbf[n], NT_DIMS) for n in range(n_items)]
    a = [jnp.where(strict, kq[n][:CHUNK] * decay[n], 0.0) for n in range(n_items)]
    qk = [jnp.where(lower, kq[n][CHUNK:] * decay[n], 0.0).astype(BF16) for n in range(n_items)]
    doff = [-jnp.where((ii >> 1) == (jj >> 1), a[n], 0.0) for n in range(n_items)]
    for lg in range(1, 6):
        m = ((ii >> (lg + 1)) == (jj >> (lg + 1))) & ((ii >> lg) != (jj >> lg))
        bm = [jnp.where(m, a[n], 0.0) for n in range(n_items)]
        db = [doff[n].astype(BF16) for n in range(n_items)]
        y = [bm[n] + _dot(db[n], bm[n].astype(BF16)) for n in range(n_items)]
        x = [y[n] + _dot(y[n].astype(BF16), db[n]) for n in range(n_items)]
        doff = [doff[n] - x[n] for n in range(n_items)]
    egc = [jnp.exp(gc[n]) for n in range(n_items)]
    rhs = [jnp.concatenate([v_[n] * bh[n], kb[n] * egc[n]], axis=-1) for n in range(n_items)]
    sol = [rhs[n] + _dot(doff[n].astype(BF16), rhs[n].astype(BF16)) for n in range(n_items)]
    g_last = [gc[n][CHUNK - 1:CHUNK, :] for n in range(n_items)]
    kd = [(k_[n] * jnp.exp(g_last[n] - gc[n])).astype(BF16) for n in range(n_items)]
    wq = [jnp.concatenate([sol[n][:, B_DV:], q_[n] * egc[n]], axis=0).astype(BF16) for n in range(n_items)]
    eg_last = [jnp.exp(g_last[n]) for n in range(n_items)]

    s_cur = [s_scr[h] for h in range(B_HEADS)]
    for c in range(nc):
        idx = [c * B_HEADS + h for h in range(B_HEADS)]
        ws = [_dot(wq[n], s_cur[h].astype(BF16)) for h, n in enumerate(idx)]
        v_new = [sol[n][:, :B_DV] - ws[h][:CHUNK] for h, n in enumerate(idx)]
        vb = [v_new[h].astype(BF16) for h in range(B_HEADS)]
        o = [ws[h][CHUNK:] + _dot(qk[n], vb[h]) for h, n in enumerate(idx)]
        s_cur = [s_cur[h] * eg_last[n] + _dotg(kd[n], vb[h], TN_DIMS) for h, n in enumerate(idx)]
        for h in range(B_HEADS):
            z = zb[rs(c), h * B_DV:(h + 1) * B_DV]
            mix_scr[rs(c), A_WIDTH + h * B_DV:A_WIDTH + (h + 1) * B_DV] = (
                _rms(o[h], onorm_ref[...]) * _silu(z)).astype(BF16)
    for h in range(B_HEADS):
        s_scr[h] = s_cur[h]


def _ab_layer_kernel(x_ref, g_ref, w_ref, wg_ref, wgt_ref, sink_ref, hist_ref, convw_ref, gprm_ref,
                     gprmt_ref, onorm_ref, wout_ref, convin_ref, sin_ref,
                     y_ref, kvtail_ref, xbtail_ref, sout_ref,
                     s_scr, xc_scr, kvprev_scr, mix_scr, *, ct, n_lead, pos_shift):
    i = pl.program_id(1)
    nsteps = pl.num_programs(1)
    ntail = min(WINDOW, ct)

    @pl.when(i == 0)
    def _():
        s_scr[...] = sin_ref[0]
        xc_scr[0:8, :] = convin_ref[0]
        kvprev_scr[1] = hist_ref[...]

    x = x_ref[0]
    hb = _rms(x, g_ref[...]).astype(BF16)
    qa = _dot(hb, w_ref[:, O_QA:O_KV])
    kv = _dot(hb, w_ref[:, O_KV:O_ZA])
    za = _dot(hb, w_ref[:, O_ZA:O_XB])
    xb = _dot(hb, w_ref[:, O_XB:O_ZB])
    zb = _dot(hb, w_ref[:, O_ZB:AB_MAIN])
    gt = _dot(hb, wg_ref[...])
    gtt = _dotg(wgt_ref[...], hb, NT_DIMS)
    kvtail_ref[0] = kv[ct - ntail:]
    xbtail_ref[0] = xb[ct - 8:]

    slot = lax.rem(i, 2)
    kvb = jnp.concatenate([kvprev_scr[1 - slot], kv], axis=0).astype(BF16)
    if ct >= WINDOW:
        kvprev_scr[slot] = kv[ct - WINDOW:]
    _swa_tile(i, qa, kvb, za, sink_ref, mix_scr, ct=ct, sb=ntail, pos_shift=pos_shift)

    xc_scr[8:8 + ct, :] = xb
    w = convw_ref[...]
    yc = xc_scr[pl.ds(8 - (CONV_W - 1), ct), :] * w[0:1, :]
    for t in range(1, CONV_W):
        yc = yc + xc_scr[pl.ds(8 - (CONV_W - 1) + t, ct), :] * w[t:t + 1, :]
    cv = _silu(yc)
    xc_scr[0:8, :] = xc_scr[ct:ct + 8, :]
    _gdn_tile(i, cv, gt, gtt, zb, gprm_ref, gprmt_ref, onorm_ref, s_scr, mix_scr, ct=ct, n_lead=n_lead)

    y_ref[0] = x + _dot(mix_scr[...], wout_ref[...])

    @pl.when(i == nsteps - 1)
    def _():
        sout_ref[0] = s_scr[...]


def _ab_layer_prompt(x3, norm_g, w_main, w_gate, w_gate_t, sink, hist_kv, conv_w8, gprm, gprm_t, onorm,
                     w_out, conv_in, s_in, ct, n_lead, pos_shift):
    B, L, _ = x3.shape
    ntail = min(WINDOW, ct)
    kern = functools.partial(_ab_layer_kernel, ct=ct, n_lead=n_lead, pos_shift=pos_shift)
    full = lambda a: pl.BlockSpec(a.shape, lambda b, i: (0,) * a.ndim)
    blk = pl.BlockSpec((1, ct, D_MODEL), lambda b, i: (b, i, 0))
    per_b = lambda *s: pl.BlockSpec((1,) + s, lambda b, i: (b,) + (0,) * len(s))
    return pl.pallas_call(
        kern,
        grid=(B, L // ct),
        in_specs=[blk, full(norm_g), full(w_main), full(w_gate), full(w_gate_t), full(sink), full(hist_kv),
                  full(conv_w8), full(gprm), full(gprm_t), full(onorm), full(w_out),
                  per_b(8, B_CONV_CH), per_b(B_HEADS, B_DK, B_DV)],
        out_specs=[blk, per_b(ntail, 2 * A_KV_WIDTH), per_b(8, B_CONV_CH), per_b(B_HEADS, B_DK, B_DV)],
        out_shape=[jax.ShapeDtypeStruct((B, L, D_MODEL), F32),
                   jax.ShapeDtypeStruct((B, ntail, 2 * A_KV_WIDTH), F32),
                   jax.ShapeDtypeStruct((B, 8, B_CONV_CH), F32),
                   jax.ShapeDtypeStruct((B, B_HEADS, B_DK, B_DV), F32)],
        scratch_shapes=[pltpu.VMEM((B_HEADS, B_DK, B_DV), F32),
                        pltpu.VMEM((ct + 8, B_CONV_CH), F32),
                        pltpu.VMEM((2, WINDOW, 2 * A_KV_WIDTH), F32),
                        pltpu.VMEM((ct, AB_MIX), BF16)],
        compiler_params=_cparams(("arbitrary", "arbitrary")),
        name="ab_layer_prompt",
    )(x3, norm_g, w_main, w_gate, w_gate_t, sink, hist_kv, conv_w8, gprm, gprm_t, onorm, w_out, conv_in, s_in)


def _c_layer_kernel(x_ref, g_ref, w_ref, wgk_ref, wup_ref, bgk_ref, onorm_ref, wout_ref, gfin_ref,
                    sin_ref, y_ref, sout_ref, st_scr, og_scr, *, ct, n_lead):
    i = pl.program_id(1)
    nsteps = pl.num_programs(1)

    @pl.when(i == 0)
    def _():
        st_scr[...] = sin_ref[0]

    x = x_ref[0]
    hb = _rms(x, g_ref[...]).astype(BF16)
    q_all = _dot(hb, w_ref[:, 0:C_KEY])
    k_all = _dot(hb, w_ref[:, C_KEY:2 * C_KEY])
    v_all = _dot(hb, w_ref[:, 2 * C_KEY:2 * C_KEY + C_VAL]).astype(BF16)
    z_all = _dot(hb, w_ref[:, 2 * C_KEY + C_VAL:])
    la_all = _gla_log_decay(hb, wgk_ref, wup_ref, bgk_ref)
    rows = i * ct + lax.broadcasted_iota(jnp.int32, la_all.shape, 0)
    la_all = jnp.where(rows >= n_lead, la_all, 0.0)

    ii = lax.broadcasted_iota(jnp.int32, (CHUNK, CHUNK), 0)
    jj = lax.broadcasted_iota(jnp.int32, (CHUNK, CHUNK), 1)
    lower = ii >= jj
    ltri = lower.astype(BF16)
    nc = ct // CHUNK
    rs = lambda c: slice(c * CHUNK, (c + 1) * CHUNK)
    ks = lambda h: slice(h * C_DK, (h + 1) * C_DK)
    vs = lambda h: slice(h * C_DV, (h + 1) * C_DV)
    items = [(c, h) for c in range(nc) for h in range(C_HEADS)]
    n_items = len(items)
    bc_all = [sum(_dot(ltri, p) for p in _split3(la_all[rs(c)])) for c in range(nc)]
    bc = [bc_all[c][:, ks(h)] for c, h in items]
    k_ = [k_all[rs(c), ks(h)] for c, h in items]
    v_ = [v_all[rs(c), vs(h)] for c, h in items]
    qd = [(q_all[rs(c), ks(h)] * (C_DK ** -0.5) * jnp.exp(bc[n])).astype(BF16) for n, (c, h) in enumerate(items)]
    kinv = [(k_[n] * jnp.exp(-bc[n])).astype(BF16) for n in range(n_items)]
    bl = [bc[n][CHUNK - 1:CHUNK, :] for n in range(n_items)]
    kd = [(k_[n] * jnp.exp(bl[n] - bc[n])).astype(BF16) for n in range(n_items)]
    gl = [jnp.exp(bl[n]) for n in range(n_items)]
    qk = [jnp.where(lower, _dotg(qd[n], kinv[n], NT_DIMS), 0.0).astype(BF16) for n in range(n_items)]
    o_intra = [_dot(qk[n], v_[n]) for n in range(n_items)]

    st = [st_scr[h] for h in range(C_HEADS)]
    for c in range(nc):
        idx = [c * C_HEADS + h for h in range(C_HEADS)]
        o = [o_intra[n] + _dotg(qd[n], st[h].astype(BF16), NT_DIMS) for h, n in enumerate(idx)]
        st = [st[h] * gl[n] + _dotg(v_[n], kd[n], TN_DIMS) for h, n in enumerate(idx)]
        for h in range(C_HEADS):
            og_scr[rs(c), vs(h)] = (_rms(o[h], onorm_ref[...]) * _silu(z_all[rs(c), vs(h)])).astype(BF16)
    for h in range(C_HEADS):
        st_scr[h] = st[h]

    y_ref[0] = _rms(x + _dot(og_scr[...], wout_ref[...]), gfin_ref[...])

    @pl.when(i == nsteps - 1)
    def _():
        sout_ref[0] = st_scr[...]


def _c_layer_prompt(x3, norm_g, w_main, w_gk, w_up, b_gk, onorm, w_out, final_g, st_in, ct, n_lead):
    B, L, _ = x3.shape
    kern = functools.partial(_c_layer_kernel, ct=ct, n_lead=n_lead)
    full = lambda a: pl.BlockSpec(a.shape, lambda b, i: (0,) * a.ndim)
    blk = pl.BlockSpec((1, ct, D_MODEL), lambda b, i: (b, i, 0))
    st = pl.BlockSpec((1, C_HEADS, C_DV, C_DK), lambda b, i: (b, 0, 0, 0))
    return pl.pallas_call(
        kern,
        grid=(B, L // ct),
        in_specs=[blk, full(norm_g), full(w_main), full(w_gk), full(w_up), full(b_gk), full(onorm),
                  full(w_out), full(final_g), st],
        out_specs=[blk, st],
        out_shape=[jax.ShapeDtypeStruct((B, L, D_MODEL), F32),
                   jax.ShapeDtypeStruct((B, C_HEADS, C_DV, C_DK), F32)],
        scratch_shapes=[pltpu.VMEM((C_HEADS, C_DV, C_DK), F32), pltpu.VMEM((ct, C_VAL), BF16)],
        compiler_params=_cparams(("arbitrary", "arbitrary")),
        name="c_layer_prompt",
    )(x3, norm_g, w_main, w_gk, w_up, b_gk, onorm, w_out, final_g, st_in)


def _ab_proj_kernel(x_ref, g_ref, w_ref, wg_ref, qa_ref, kv_ref, za_ref, xb_ref, zb_ref, gates_ref):
    hb = _rms(x_ref[...], g_ref[...]).astype(BF16)
    qa_ref[...] = _dot(hb, w_ref[:, O_QA:O_KV])
    kv_ref[...] = _dot(hb, w_ref[:, O_KV:O_ZA])
    za_ref[...] = _dot(hb, w_ref[:, O_ZA:O_XB])
    xb_ref[...] = _dot(hb, w_ref[:, O_XB:O_ZB])
    zb_ref[...] = _dot(hb, w_ref[:, O_ZB:AB_MAIN])
    gates_ref[...] = _dot(hb, wg_ref[...])


def _ab_proj(x2d, norm_g, w_main, w_gate):
    n = x2d.shape[0]
    full = lambda a: pl.BlockSpec(a.shape, lambda: (0,) * a.ndim)
    widths = (A_WIDTH, 2 * A_KV_WIDTH, A_WIDTH, B_CONV_CH, B_VAL, LANES)
    args = (x2d, norm_g, w_main, w_gate)
    return pl.pallas_call(
        _ab_proj_kernel,
        in_specs=[full(a) for a in args],
        out_specs=[pl.BlockSpec((n, w), lambda: (0, 0)) for w in widths],
        out_shape=[jax.ShapeDtypeStruct((n, w), F32) for w in widths],
        compiler_params=pltpu.CompilerParams(vmem_limit_bytes=VMEM_LIMIT),
        name="ab_in_proj_sample",
    )(*args)


def _swa_sample_kernel(q_ref, kvn_ref, za_ref, ck_ref, cv_ref, sink_ref, o_ref, nk_ref, nv_ref, *, bt):
    row = lax.broadcasted_iota(jnp.int32, (WINDOW, A_KV_WIDTH), 0)
    hrow = lax.broadcasted_iota(jnp.int32, (A_HEADS, A_HD), 0)
    sk = sink_ref[...]
    for b in range(bt):
        kn = kvn_ref[b:b + 1, 0:A_KV_WIDTH]
        vn = kvn_ref[b:b + 1, A_KV_WIDTH:]
        nk = jnp.where(row == WINDOW - 1, kn, pltpu.roll(ck_ref[b], WINDOW - 1, 0))
        nv = jnp.where(row == WINDOW - 1, vn, pltpu.roll(cv_ref[b], WINDOW - 1, 0))
        nk_ref[b] = nk
        nv_ref[b] = nv
        q8 = q_ref[b].astype(BF16)
        nkb = nk.astype(BF16)
        nvb = nv.astype(BF16)
        o8 = None
        for j in range(A_KV_HEADS):
            sc = _dotg(q8, nkb[:, j * A_HD:(j + 1) * A_HD], NT_DIMS) * (A_HD ** -0.5)
            m = jnp.maximum(jnp.max(sc, axis=-1, keepdims=True), sk)
            p = jnp.exp(sc - m)
            den = jnp.sum(p, axis=-1, keepdims=True) + jnp.exp(sk - m)
            oj = _dot(p.astype(BF16), nvb[:, j * A_HD:(j + 1) * A_HD]) / den
            o8 = oj if o8 is None else jnp.where(hrow >= j * A_GROUP, oj, o8)
        o_ref[b] = o8 * _silu(za_ref[b])


def _swa_sample(q3, kv_new, za3, cache_k, cache_v, sink_col, bt):
    nb = q3.shape[0]
    hd = pl.BlockSpec((bt, A_HEADS, A_HD), lambda i: (i, 0, 0))
    cache = pl.BlockSpec((bt, WINDOW, A_KV_WIDTH), lambda i: (i, 0, 0))
    return pl.pallas_call(
        functools.partial(_swa_sample_kernel, bt=bt),
        grid=(nb // bt,),
        in_specs=[hd, pl.BlockSpec((bt, 2 * A_KV_WIDTH), lambda i: (i, 0)), hd, cache, cache,
                  pl.BlockSpec(sink_col.shape, lambda i: (0, 0))],
        out_specs=[hd, cache, cache],
        out_shape=[jax.ShapeDtypeStruct(q3.shape, F32),
                   jax.ShapeDtypeStruct(cache_k.shape, F32),
                   jax.ShapeDtypeStruct(cache_v.shape, F32)],
        compiler_params=_cparams(("arbitrary",)),
        name="swa_sample",
    )(q3, kv_new, za3, cache_k, cache_v, sink_col)


def _gdn_sample_prep_kernel(xb_ref, h0_ref, h1_ref, h2_ref, gates_ref, convw_ref, gprm_ref,
                            q_ref, k_ref, v_ref, bg_ref):
    w = convw_ref[...]
    y = (h0_ref[...] * w[0:1, :] + h1_ref[...] * w[1:2, :] + h2_ref[...] * w[2:3, :]
         + xb_ref[...] * w[3:4, :])
    cv = _silu(y)
    for h in range(B_HEADS):
        q = cv[:, h * B_DK:(h + 1) * B_DK]
        k = cv[:, B_KEY + h * B_DK:B_KEY + (h + 1) * B_DK]
        q_ref[:, h * B_DK:(h + 1) * B_DK] = (
            q * lax.rsqrt(jnp.sum(q * q, axis=-1, keepdims=True) + EPS) * (B_DK ** -0.5))
        k_ref[:, h * B_DK:(h + 1) * B_DK] = k * lax.rsqrt(jnp.sum(k * k, axis=-1, keepdims=True) + EPS)
    v_ref[...] = cv[:, 2 * B_KEY:]
    gt = gates_ref[...]
    col = lax.broadcasted_iota(jnp.int32, gt.shape, 1)
    g = -gprm_ref[0:1, :] * _softplus(gt + gprm_ref[1:2, :])
    bg_ref[...] = jnp.where(col < B_HEADS, _sigmoid(gt), jnp.exp(g))


def _gdn_sample_prep(xb, h0, h1, h2, gates, conv_w8, gprm):
    n = xb.shape[0]
    full = lambda a: pl.BlockSpec(a.shape, lambda: (0,) * a.ndim)
    args = (xb, h0, h1, h2, gates, conv_w8, gprm)
    return pl.pallas_call(
        _gdn_sample_prep_kernel,
        in_specs=[full(a) for a in args],
        out_specs=[pl.BlockSpec((n, B_KEY), lambda: (0, 0))] * 3 + [pl.BlockSpec((n, LANES), lambda: (0, 0))],
        out_shape=[jax.ShapeDtypeStruct((n, B_KEY), F32)] * 3 + [jax.ShapeDtypeStruct((n, LANES), F32)],
        compiler_params=pltpu.CompilerParams(vmem_limit_bytes=VMEM_LIMIT),
        name="gdn_sample_prep",
    )(*args)


def _gdn_sample_kernel(s_ref, qt_ref, kt_ref, v_ref, bg_ref, zb_ref, onorm_ref, o_ref, sout_ref, *, bt):
    for b in range(bt):
        for h in range(B_HEADS):
            s = s_ref[b, h]
            kc = kt_ref[0, h, :, b:b + 1]
            qc = qt_ref[0, h, :, b:b + 1]
            v = v_ref[b:b + 1, h * B_DV:(h + 1) * B_DV]
            beta = bg_ref[b:b + 1, h:h + 1]
            eg = bg_ref[b:b + 1, B_HEADS + h:B_HEADS + h + 1]
            ks = jnp.sum(kc * s, axis=0, keepdims=True)
            v_new = beta * (v - eg * ks)
            s_new = eg * s + kc * v_new
            sout_ref[b, h] = s_new
            o = jnp.sum(qc * s_new, axis=0, keepdims=True)
            z = zb_ref[b:b + 1, h * B_DV:(h + 1) * B_DV]
            o_ref[b:b + 1, h * B_DV:(h + 1) * B_DV] = _rms(o, onorm_ref[...]) * _silu(z)


def _gdn_sample(state, qt, kt, v, bg, zb, onorm, bt):
    nb = state.shape[0]
    st = pl.BlockSpec((bt, B_HEADS, B_DK, B_DV), lambda i: (i, 0, 0, 0))
    col = pl.BlockSpec((1, B_HEADS, B_DK, bt), lambda i: (i, 0, 0, 0))
    row = lambda n: pl.BlockSpec((bt, n), lambda i: (i, 0))
    return pl.pallas_call(
        functools.partial(_gdn_sample_kernel, bt=bt),
        grid=(nb // bt,),
        in_specs=[st, col, col, row(B_VAL), row(LANES), row(B_VAL),
                  pl.BlockSpec(onorm.shape, lambda i: (0, 0))],
        out_specs=[row(B_VAL), st],
        out_shape=[jax.ShapeDtypeStruct((nb, B_VAL), F32), jax.ShapeDtypeStruct(state.shape, F32)],
        compiler_params=_cparams(("arbitrary",)),
        name="gdn_sample",
    )(state, qt, kt, v, bg, zb, onorm)


def _ab_out_c_proj_kernel(x_ref, oa_ref, ob_ref, wout_ref, g_ref, w_ref, wgk_ref, wup_ref, bgk_ref,
                          y_ref, q_ref, k_ref, v_ref, z_ref, la_ref):
    y = (x_ref[...] + _dot(oa_ref[...].astype(BF16), wout_ref[0:A_WIDTH, :])
         + _dot(ob_ref[...].astype(BF16), wout_ref[A_WIDTH:, :]))
    y_ref[...] = y
    hb = _rms(y, g_ref[...]).astype(BF16)
    q_ref[...] = _dot(hb, w_ref[:, 0:C_KEY])
    k_ref[...] = _dot(hb, w_ref[:, C_KEY:2 * C_KEY])
    v_ref[...] = _dot(hb, w_ref[:, 2 * C_KEY:2 * C_KEY + C_VAL])
    z_ref[...] = _dot(hb, w_ref[:, 2 * C_KEY + C_VAL:])
    la_ref[...] = _gla_log_decay(hb, wgk_ref, wup_ref, bgk_ref)


def _ab_out_c_proj(x2d, oa, ob, w_out, norm_g, w_main, w_gk, w_up, b_gk):
    n = x2d.shape[0]
    full = lambda a: pl.BlockSpec(a.shape, lambda: (0,) * a.ndim)
    widths = (D_MODEL, C_KEY, C_KEY, C_VAL, C_VAL, C_KEY)
    args = (x2d, oa, ob, w_out, norm_g, w_main, w_gk, w_up, b_gk)
    return pl.pallas_call(
        _ab_out_c_proj_kernel,
        in_specs=[full(a) for a in args],
        out_specs=[pl.BlockSpec((n, w), lambda: (0, 0)) for w in widths],
        out_shape=[jax.ShapeDtypeStruct((n, w), F32) for w in widths],
        compiler_params=pltpu.CompilerParams(vmem_limit_bytes=VMEM_LIMIT),
        name="ab_out_c_in_proj_sample",
    )(*args)


def _gla_sample_kernel(s_ref, qt_ref, kt_ref, lat_ref, v_ref, z_ref, onorm_ref, o_ref, sout_ref, *, bt):
    for b in range(bt):
        for h in range(C_HEADS):
            s = s_ref[b, h]
            kc = kt_ref[0, h, :, b:b + 1]
            qc = qt_ref[0, h, :, b:b + 1] * (C_DK ** -0.5)
            ac = jnp.exp(lat_ref[0, h, :, b:b + 1])
            v = v_ref[b:b + 1, h * C_DV:(h + 1) * C_DV]
            s_new = ac * s + kc * v
            sout_ref[b, h] = s_new
            o = jnp.sum(qc * s_new, axis=0, keepdims=True)
            z = z_ref[b:b + 1, h * C_DV:(h + 1) * C_DV]
            o_ref[b:b + 1, h * C_DV:(h + 1) * C_DV] = _rms(o, onorm_ref[...]) * _silu(z)


def _gla_sample(state, qt, kt, lat, v, z, onorm, bt):
    nb = state.shape[0]
    st = pl.BlockSpec((bt, C_HEADS, C_DK, C_DV), lambda i: (i, 0, 0, 0))
    col = pl.BlockSpec((1, C_HEADS, C_DK, bt), lambda i: (i, 0, 0, 0))
    row = lambda n: pl.BlockSpec((bt, n), lambda i: (i, 0))
    return pl.pallas_call(
        functools.partial(_gla_sample_kernel, bt=bt),
        grid=(nb // bt,),
        in_specs=[st, col, col, col, row(C_VAL), row(C_VAL),
                  pl.BlockSpec(onorm.shape, lambda i: (0, 0))],
        out_specs=[row(C_VAL), st],
        out_shape=[jax.ShapeDtypeStruct((nb, C_VAL), F32), jax.ShapeDtypeStruct(state.shape, F32)],
        compiler_params=_cparams(("arbitrary",)),
        name="gla_sample",
    )(state, qt, kt, lat, v, z, onorm)


def _c_out_kernel(x_ref, o_ref, w_ref, g_ref, y_ref):
    y_ref[...] = _rms(x_ref[...] + _dot(o_ref[...].astype(BF16), w_ref[...]), g_ref[...])


def _c_out(x2d, oc, w_out, final_g):
    n = x2d.shape[0]
    full = lambda a: pl.BlockSpec(a.shape, lambda: (0,) * a.ndim)
    args = (x2d, oc, w_out, final_g)
    return pl.pallas_call(
        _c_out_kernel,
        in_specs=[full(a) for a in args],
        out_specs=pl.BlockSpec((n, D_MODEL), lambda: (0, 0)),
        out_shape=jax.ShapeDtypeStruct((n, D_MODEL), F32),
        compiler_params=pltpu.CompilerParams(vmem_limit_bytes=VMEM_LIMIT),
        name="c_out_proj_norm_sample",
    )(*args)


def _to_cols(a, heads, dk, bt):
    n = a.shape[0]
    return a.reshape(n // bt, bt, heads, dk).transpose(0, 2, 3, 1)


def kernel(x_prompt, x_sample, cache_swa_k, cache_swa_v, state_dn_conv, state_dn, state_gla,
           meta_tokens, norm_ab, w_in_ab, sink_a, conv_b, a_log_b, dt_bias_b, onorm_b, w_out_ab,
           norm_c, w_in_c, w_gk_up, b_gk, onorm_c, w_out_c, final_norm):
    Bp, L, _ = x_prompt.shape
    Bs = x_sample.shape[0]
    CT, BT = 256, 8

    w_ab = w_in_ab[0]
    w_ab_main = w_ab[:, :AB_MAIN].astype(BF16)
    w_ab_gate = jnp.pad(w_ab[:, AB_MAIN:], ((0, 0), (0, LANES - 2 * B_HEADS))).astype(BF16)
    w_ab_gate_t = jnp.pad(w_ab[:, AB_MAIN:].T, ((0, 16 - 2 * B_HEADS), (0, 0))).astype(BF16)
    g_ab = norm_ab[0][None, :]
    sink_row = sink_a[0][None, :]
    sink_col = sink_a[0][:, None]
    conv_w8 = jnp.pad(conv_b[0], ((0, 8 - CONV_W), (0, 0)))
    ea = jnp.exp(a_log_b[0])
    gprm = jnp.zeros((8, LANES), F32).at[0, B_HEADS:2 * B_HEADS].set(ea).at[1, B_HEADS:2 * B_HEADS].set(dt_bias_b[0])
    gprm_t = jnp.zeros((16, LANES), F32).at[B_HEADS:2 * B_HEADS, 0].set(ea).at[B_HEADS:2 * B_HEADS, 1].set(dt_bias_b[0])
    onb = onorm_b[0][None, :]
    w_ab_out = w_out_ab[0].astype(BF16)
    w_c = w_in_c[0]
    w_c_main = w_c[:, :C_MAIN].astype(BF16)
    w_c_gk = jnp.pad(w_c[:, C_MAIN:], ((0, 0), (0, LANES - C_RANK))).astype(BF16)
    w_up = jnp.pad(w_gk_up[0], ((0, LANES - C_RANK), (0, 0)))
    bgk = b_gk[0][None, :]
    g_c = norm_c[0][None, :]
    onc = onorm_c[0][None, :]
    w_c_out = w_out_c[0].astype(BF16)
    g_fin = final_norm[None, :]

    def ab_layer(x3, ct, hist_kv, pos_shift, conv_in, s_in, n_lead):
        return _ab_layer_prompt(x3, g_ab, w_ab_main, w_ab_gate, w_ab_gate_t, sink_row, hist_kv, conv_w8,
                                gprm, gprm_t, onb, w_ab_out, conv_in, s_in, ct, n_lead, pos_shift)

    def c_layer(x3, ct, st_in, n_lead):
        return _c_layer_prompt(x3, g_c, w_c_main, w_c_gk, w_up, bgk, onc, w_c_out, g_fin, st_in, ct, n_lead)

    xpre = jnp.concatenate([jnp.zeros((LEAD, D_MODEL), F32), meta_tokens], axis=0)[None]
    y_pre, kv_pre, xbtail_pre, sdn_pre = ab_layer(
        xpre, CHUNK, jnp.zeros((WINDOW, 2 * A_KV_WIDTH), F32), -LEAD,
        jnp.zeros((1, 8, B_CONV_CH), F32), jnp.zeros((1, B_HEADS, B_DK, B_DV), F32), LEAD)
    _, sgla_pre = c_layer(y_pre, CHUNK, jnp.zeros((1, C_HEADS, C_DV, C_DK), F32), LEAD)

    hist_kv = jnp.concatenate([jnp.zeros((WINDOW - CHUNK, 2 * A_KV_WIDTH), F32), kv_pre[0]], axis=0)
    y1, kv_tail, xb_tail, sdn_p = ab_layer(
        x_prompt, CT, hist_kv, N_META, jnp.broadcast_to(xbtail_pre, (Bp, 8, B_CONV_CH)),
        jnp.broadcast_to(sdn_pre, (Bp, B_HEADS, B_DK, B_DV)), 0)
    y_prompt, sgla_t = c_layer(y1, CT, jnp.broadcast_to(sgla_pre, (Bp, C_HEADS, C_DV, C_DK)), 0)
    sgla_p = jnp.swapaxes(sgla_t, 2, 3)

    swa_k_p = kv_tail[:, :, :A_KV_WIDTH].reshape(1, Bp, WINDOW, A_KV_HEADS, A_HD)
    swa_v_p = kv_tail[:, :, A_KV_WIDTH:].reshape(1, Bp, WINDOW, A_KV_HEADS, A_HD)
    dn_conv_p = xb_tail[:, 8 - (CONV_W - 1):, :][None]

    xs = x_sample.reshape(Bs, D_MODEL)
    qa, kvn, za, xb, zb, gates = _ab_proj(xs, g_ab, w_ab_main, w_ab_gate)
    oa3, nk, nv = _swa_sample(qa.reshape(Bs, A_HEADS, A_HD), kvn, za.reshape(Bs, A_HEADS, A_HD),
                              cache_swa_k[0].reshape(Bs, WINDOW, A_KV_WIDTH),
                              cache_swa_v[0].reshape(Bs, WINDOW, A_KV_WIDTH), sink_col, BT)
    hist = state_dn_conv[0]
    qn, kn, vn, bg = _gdn_sample_prep(xb, hist[:, 0], hist[:, 1], hist[:, 2], gates, conv_w8, gprm)
    ob, sdn_s = _gdn_sample(state_dn[0], _to_cols(qn, B_HEADS, B_DK, BT), _to_cols(kn, B_HEADS, B_DK, BT),
                            vn, bg, zb, onb, BT)
    ys, qc, kc, vc, zc, la = _ab_out_c_proj(xs, oa3.reshape(Bs, A_WIDTH), ob, w_ab_out,
                                            g_c, w_c_main, w_c_gk, w_up, bgk)
    ocs, sgla_s = _gla_sample(state_gla[0], _to_cols(qc, C_HEADS, C_DK, BT), _to_cols(kc, C_HEADS, C_DK, BT),
                              _to_cols(la, C_HEADS, C_DK, BT), vc, zc, onc, BT)
    y_sample = _c_out(ys, ocs, w_c_out, g_fin).reshape(Bs, 1, D_MODEL)
    dn_conv_s = jnp.concatenate([hist[:, 1:], xb[:, None, :]], axis=1)[None]

    return (y_prompt, y_sample, swa_k_p, swa_v_p, dn_conv_p, sdn_p[None], sgla_p[None],
            nk.reshape(1, Bs, WINDOW, A_KV_HEADS, A_HD), nv.reshape(1, Bs, WINDOW, A_KV_HEADS, A_HD),
            dn_conv_s, sdn_s[None], sgla_s[None])
```

```python
import functools

import jax
import jax.numpy as jnp
from jax import lax
from jax.experimental import pallas as pl
from jax.experimental.pallas import tpu as pltpu

F32 = jnp.float32
BF16 = jnp.bfloat16
EPS = 1e-6

D_MODEL = 1024
N_META = 16
CHUNK = 64
LEAD = (-N_META) % CHUNK
A_HEADS, A_KV_HEADS, A_HD = 8, 2, 64
A_GROUP = A_HEADS // A_KV_HEADS
A_WIDTH = A_HEADS * A_HD
A_KV_WIDTH = A_KV_HEADS * A_HD
WINDOW = 128
B_HEADS, B_DK, B_DV = 4, 128, 128
B_KEY = B_HEADS * B_DK
B_VAL = B_HEADS * B_DV
CONV_W = 4
B_CONV_CH = 2 * B_KEY + B_VAL
C_HEADS, C_DK, C_DV = 4, 128, 256
C_KEY = C_HEADS * C_DK
C_VAL = C_HEADS * C_DV
C_RANK = 16
C_GATE_NORM = 16.0
AB_MAIN = 2 * A_WIDTH + 2 * A_KV_WIDTH + B_CONV_CH + B_VAL
AB_MIX = A_WIDTH + B_VAL
C_MAIN = 2 * C_KEY + 2 * C_VAL
O_QA, O_KV, O_ZA, O_XB, O_ZB = 0, 512, 768, 1280, 2816
LANES = 128
NEG = -1e30

VMEM_LIMIT = 56 * 1024 * 1024

NT_DIMS = (((1,), (1,)), ((), ()))
TN_DIMS = (((0,), (0,)), ((), ()))


def _cparams(sem):
    return pltpu.CompilerParams(dimension_semantics=sem, vmem_limit_bytes=VMEM_LIMIT)


def _dot(a, b):
    return jnp.dot(a, b, preferred_element_type=F32)


def _dotg(a, b, dims):
    return lax.dot_general(a, b, dims, preferred_element_type=F32)


def _split3(a):
    hi = a.astype(BF16)
    r = a - hi.astype(F32)
    mid = r.astype(BF16)
    lo = (r - mid.astype(F32)).astype(BF16)
    return hi, mid, lo


def _sigmoid(x):
    return 1.0 / (1.0 + jnp.exp(-x))


def _silu(x):
    return x * _sigmoid(x)


def _softplus(x):
    return jnp.maximum(x, 0.0) + jnp.log1p(jnp.exp(-jnp.abs(x)))


def _log_sigmoid(x):
    return jnp.minimum(x, 0.0) - jnp.log1p(jnp.exp(-jnp.abs(x)))


def _rms(x, g):
    return x * lax.rsqrt(jnp.mean(x * x, axis=-1, keepdims=True) + EPS) * g


def _gla_log_decay(hb, wgk_ref, wup_ref, bgk_ref):
    low = _dot(hb, wgk_ref[...])
    lh, lm, _ = _split3(low)
    wh, wm, _ = _split3(wup_ref[...])
    up = _dot(lh, wh) + _dot(lh, wm) + _dot(lm, wh)
    return _log_sigmoid(up + bgk_ref[...]) * (1.0 / C_GATE_NORM)


def _swa_tile(i, qa, kvb, za, sink_ref, mix_scr, *, ct, sb, pos_shift):
    r = lax.broadcasted_iota(jnp.int32, (sb, WINDOW + sb), 0)
    c = lax.broadcasted_iota(jnp.int32, (sb, WINDOW + sb), 1)
    diff = r - c + WINDOW
    in_window = (diff >= 0) & (diff < WINDOW)
    nsb = ct // sb
    items = [(s, j) for s in range(nsb) for j in range(A_KV_HEADS)]
    masks = [jnp.concatenate([in_window & (i * ct + s * sb + c - WINDOW + pos_shift >= 0)] * A_GROUP, axis=0)
             for s in range(nsb)]
    sinks = [jnp.concatenate([jnp.broadcast_to(sink_ref[0:1, j * A_GROUP + g:j * A_GROUP + g + 1], (sb, 1))
                              for g in range(A_GROUP)], axis=0) for j in range(A_KV_HEADS)]
    qs = [jnp.concatenate([qa[s * sb:(s + 1) * sb, (j * A_GROUP + g) * A_HD:(j * A_GROUP + g + 1) * A_HD]
                           for g in range(A_GROUP)], axis=0).astype(BF16) for s, j in items]
    sc = [_dotg(qs[n], kvb[s * sb:s * sb + WINDOW + sb, j * A_HD:(j + 1) * A_HD], NT_DIMS) * (A_HD ** -0.5)
          for n, (s, j) in enumerate(items)]
    sc = [jnp.where(masks[s], sc[n], NEG) for n, (s, j) in enumerate(items)]
    mx = [jnp.maximum(jnp.max(sc[n], axis=-1, keepdims=True), sinks[j]) for n, (s, j) in enumerate(items)]
    p = [jnp.exp(sc[n] - mx[n]) for n in range(len(items))]
    den = [jnp.sum(p[n], axis=-1, keepdims=True) + jnp.exp(sinks[j] - mx[n]) for n, (s, j) in enumerate(items)]
    pv = [_dot(p[n].astype(BF16),
               kvb[s * sb:s * sb + WINDOW + sb, A_KV_WIDTH + j * A_HD:A_KV_WIDTH + (j + 1) * A_HD]) / den[n]
          for n, (s, j) in enumerate(items)]
    for s in range(nsb):
        o = jnp.concatenate([pv[s * A_KV_HEADS + j][g * sb:(g + 1) * sb]
                             for j in range(A_KV_HEADS) for g in range(A_GROUP)], axis=-1)
        mix_scr[s * sb:(s + 1) * sb, 0:A_WIDTH] = (o * _silu(za[s * sb:(s + 1) * sb])).astype(BF16)


def _gdn_tile(i, cv, gt, gtt, zb, gprm_ref, gprmt_ref, onorm_ref, s_scr, mix_scr, *, ct, n_lead):
    beta = _sigmoid(gt)
    rows = i * ct + lax.broadcasted_iota(jnp.int32, gt.shape, 0)
    gcol = jnp.where(rows >= n_lead, -gprm_ref[0:1, :] * _softplus(gt + gprm_ref[1:2, :]), 0.0)
    lanes = i * ct + lax.broadcasted_iota(jnp.int32, gtt.shape, 1)
    grow = jnp.where(lanes >= n_lead, -gprmt_ref[:, 0:1] * _softplus(gtt + gprmt_ref[:, 1:2]), 0.0)

    ii = lax.broadcasted_iota(jnp.int32, (CHUNK, CHUNK), 0)
    jj = lax.broadcasted_iota(jnp.int32, (CHUNK, CHUNK), 1)
    lower = ii >= jj
    strict = ii > jj
    ltri = lower.astype(BF16)
    utri = (ii <= jj).astype(BF16)

    nc = ct // CHUNK
    items = [(c, h) for c in range(nc) for h in range(B_HEADS)]
    n_items = len(items)
    rs = lambda c: slice(c * CHUNK, (c + 1) * CHUNK)
    qn, kn = [], []
    for h in range(B_HEADS):
        q = cv[:, h * B_DK:(h + 1) * B_DK]
        k = cv[:, B_KEY + h * B_DK:B_KEY + (h + 1) * B_DK]
        qn.append(q * lax.rsqrt(jnp.sum(q * q, axis=-1, keepdims=True) + EPS) * (B_DK ** -0.5))
        kn.append(k * lax.rsqrt(jnp.sum(k * k, axis=-1, keepdims=True) + EPS))
    gcum_c = [sum(_dot(ltri, p) for p in _split3(gcol[rs(c)])) for c in range(nc)]
    gcum_r = [sum(_dot(p, utri) for p in _split3(grow[:, rs(c)])) for c in range(nc)]
    q_ = [qn[h][rs(c)] for c, h in items]
    k_ = [kn[h][rs(c)] for c, h in items]
    v_ = [cv[rs(c), 2 * B_KEY + h * B_DV:2 * B_KEY + (h + 1) * B_DV] for c, h in items]
    bh = [beta[rs(c), h:h + 1] for c, h in items]
    gc = [gcum_c[c][:, B_HEADS + h:B_HEADS + h + 1] for c, h in items]
    gr = [gcum_r[c][B_HEADS + h:B_HEADS + h + 1, :] for c, h in items]
    decay = [jnp.exp(jnp.where(lower, gc[n] - gr[n], NEG)) for n in range(n_items)]
    kb = [k_[n] * bh[n] for n in range(n_items)]
    kbf = [k_[n].astype(BF16) for n in range(n_items)]
    kq = [_dotg(jnp.concatenate([kb[n], q_[n]], axis=0).astype(BF16), kbf[n], NT_DIMS) for n in range(n_items)]
    a = [jnp.where(strict, kq[n][:CHUNK] * decay[n], 0.0) for n in range(n_items)]
    qk = [jnp.where(lower, kq[n][CHUNK:] * decay[n], 0.0).astype(BF16) for n in range(n_items)]
    doff = [-jnp.where((ii >> 1) == (jj >> 1), a[n], 0.0) for n in range(n_items)]
    for lg in range(1, 6):
        m = ((ii >> (lg + 1)) == (jj >> (lg + 1))) & ((ii >> lg) != (jj >> lg))
        bm = [jnp.where(m, a[n], 0.0) for n in range(n_items)]
        db = [doff[n].astype(BF16) for n in range(n_items)]
        y = [bm[n] + _dot(db[n], bm[n].astype(BF16)) for n in range(n_items)]
        x = [y[n] + _dot(y[n].astype(BF16), db[n]) for n in range(n_items)]
        doff = [doff[n] - x[n] for n in range(n_items)]
    egc = [jnp.exp(gc[n]) for n in range(n_items)]
    rhs = [jnp.concatenate([v_[n] * bh[n], kb[n] * egc[n]], axis=-1) for n in range(n_items)]
    sol = [rhs[n] + _dot(doff[n].astype(BF16), rhs[n].astype(BF16)) for n in range(n_items)]
    g_last = [gc[n][CHUNK - 1:CHUNK, :] for n in range(n_items)]
    kd = [(k_[n] * jnp.exp(g_last[n] - gc[n])).astype(BF16) for n in range(n_items)]
    wq = [jnp.concatenate([sol[n][:, B_DV:], q_[n] * egc[n]], axis=0).astype(BF16) for n in range(n_items)]
    eg_last = [jnp.exp(g_last[n]) for n in range(n_items)]

    s_cur = [s_scr[h] for h in range(B_HEADS)]
    for c in range(nc):
        idx = [c * B_HEADS + h for h in range(B_HEADS)]
        ws = [_dot(wq[n], s_cur[h].astype(BF16)) for h, n in enumerate(idx)]
        v_new = [sol[n][:, :B_DV] - ws[h][:CHUNK] for h, n in enumerate(idx)]
        vb = [v_new[h].astype(BF16) for h in range(B_HEADS)]
        o = [ws[h][CHUNK:] + _dot(qk[n], vb[h]) for h, n in enumerate(idx)]
        s_cur = [s_cur[h] * eg_last[n] + _dotg(kd[n], vb[h], TN_DIMS) for h, n in enumerate(idx)]
        for h in range(B_HEADS):
            z = zb[rs(c), h * B_DV:(h + 1) * B_DV]
            mix_scr[rs(c), A_WIDTH + h * B_DV:A_WIDTH + (h + 1) * B_DV] = (
                _rms(o[h], onorm_ref[...]) * _silu(z)).astype(BF16)
    for h in range(B_HEADS):
        s_scr[h] = s_cur[h]


def _ab_layer_kernel(x_ref, g_ref, w_ref, wg_ref, wgt_ref, sink_ref, hist_ref, convw_ref, gprm_ref,
                     gprmt_ref, onorm_ref, wout_ref, convin_ref, sin_ref,
                     y_ref, kvtail_ref, xbtail_ref, sout_ref,
                     s_scr, xc_scr, kvprev_scr, mix_scr, *, ct, n_lead, pos_shift):
    i = pl.program_id(1)
    nsteps = pl.num_programs(1)
    ntail = min(WINDOW, ct)

    @pl.when(i == 0)
    def _():
        s_scr[...] = sin_ref[0]
        xc_scr[0:8, :] = convin_ref[0]
        kvprev_scr[1] = hist_ref[...]

    x = x_ref[0]
    hb = _rms(x, g_ref[...]).astype(BF16)
    qa = _dot(hb, w_ref[:, O_QA:O_KV])
    kv = _dot(hb, w_ref[:, O_KV:O_ZA])
    za = _dot(hb, w_ref[:, O_ZA:O_XB])
    xb = _dot(hb, w_ref[:, O_XB:O_ZB])
    zb = _dot(hb, w_ref[:, O_ZB:AB_MAIN])
    gt = _dot(hb, wg_ref[...])
    gtt = _dotg(wgt_ref[...], hb, NT_DIMS)
    kvtail_ref[0] = kv[ct - ntail:]
    xbtail_ref[0] = xb[ct - 8:]

    slot = lax.rem(i, 2)
    kvb = jnp.concatenate([kvprev_scr[1 - slot], kv], axis=0).astype(BF16)
    if ct >= WINDOW:
        kvprev_scr[slot] = kv[ct - WINDOW:]
    _swa_tile(i, qa, kvb, za, sink_ref, mix_scr, ct=ct, sb=ntail, pos_shift=pos_shift)

    xc_scr[8:8 + ct, :] = xb
    w = convw_ref[...]
    yc = xc_scr[pl.ds(8 - (CONV_W - 1), ct), :] * w[0:1, :]
    for t in range(1, CONV_W):
        yc = yc + xc_scr[pl.ds(8 - (CONV_W - 1) + t, ct), :] * w[t:t + 1, :]
    cv = _silu(yc)
    xc_scr[0:8, :] = xc_scr[ct:ct + 8, :]
    _gdn_tile(i, cv, gt, gtt, zb, gprm_ref, gprmt_ref, onorm_ref, s_scr, mix_scr, ct=ct, n_lead=n_lead)

    y_ref[0] = x + _dot(mix_scr[...], wout_ref[...])

    @pl.when(i == nsteps - 1)
    def _():
        sout_ref[0] = s_scr[...]


def _swa_stages(i, qa, kvb, env, sink_ref, mix_scr, *, ct, sb, pos_shift):
    r = lax.broadcasted_iota(jnp.int32, (sb, WINDOW + sb), 0)
    c = lax.broadcasted_iota(jnp.int32, (sb, WINDOW + sb), 1)
    diff = r - c + WINDOW
    in_window = (diff >= 0) & (diff < WINDOW)
    nsb = ct // sb
    items = [(s, j) for s in range(nsb) for j in range(A_KV_HEADS)]
    masks = [jnp.concatenate([in_window & (i * ct + s * sb + c - WINDOW + pos_shift >= 0)] * A_GROUP, axis=0)
             for s in range(nsb)]
    sinks = [jnp.concatenate([jnp.broadcast_to(sink_ref[0:1, j * A_GROUP + g:j * A_GROUP + g + 1], (sb, 1))
                              for g in range(A_GROUP)], axis=0) for j in range(A_KV_HEADS)]
    sc = []
    for s, j in items:
        qs = jnp.concatenate([qa[s * sb:(s + 1) * sb, (j * A_GROUP + g) * A_HD:(j * A_GROUP + g + 1) * A_HD]
                              for g in range(A_GROUP)], axis=0).astype(BF16)
        sc.append(_dotg(qs, kvb[s * sb:s * sb + WINDOW + sb, j * A_HD:(j + 1) * A_HD], NT_DIMS) * (A_HD ** -0.5))
        yield
    p, esk = [], []
    for n, (s, j) in enumerate(items):
        scm = jnp.where(masks[s], sc[n], NEG)
        mx = jnp.maximum(jnp.max(scm, axis=-1, keepdims=True), sinks[j])
        p.append(jnp.exp(scm - mx).astype(BF16))
        esk.append(jnp.exp(sinks[j] - mx))
        yield
    pv = []
    ones = jnp.ones((WINDOW + sb, A_HD), BF16)
    for n, (s, j) in enumerate(items):
        vx = jnp.concatenate([kvb[s * sb:s * sb + WINDOW + sb,
                                  A_KV_WIDTH + j * A_HD:A_KV_WIDTH + (j + 1) * A_HD], ones], axis=-1)
        pvx = _dot(p[n], vx)
        pv.append(pvx[:, :A_HD] / (pvx[:, A_HD:] + esk[n]))
        yield
    for s in range(nsb):
        o = jnp.concatenate([pv[s * A_KV_HEADS + j][g * sb:(g + 1) * sb]
                             for j in range(A_KV_HEADS) for g in range(A_GROUP)], axis=-1)
        mix_scr[s * sb:(s + 1) * sb, 0:A_WIDTH] = (o * _silu(env['za'][s * sb:(s + 1) * sb])).astype(BF16)
        yield


def _gdn_stages(i, cvb, gt, gtt, env, gprm_ref, gprmt_ref, onorm_ref, s_scr, mix_scr, *, ct, n_lead):
    nc = ct // CHUNK
    items = [(c, h) for c in range(nc) for h in range(B_HEADS)]
    n_items = len(items)
    rs = lambda c: slice(c * CHUNK, (c + 1) * CHUNK)
    qn, kn = [], []
    for h in range(B_HEADS):
        q = cvb[h]
        k = cvb[B_HEADS + h]
        qn.append(q * lax.rsqrt(jnp.sum(q * q, axis=-1, keepdims=True) + EPS) * (B_DK ** -0.5))
        kn.append(k * lax.rsqrt(jnp.sum(k * k, axis=-1, keepdims=True) + EPS))
        yield
    beta = _sigmoid(gt)
    rows = i * ct + lax.broadcasted_iota(jnp.int32, gt.shape, 0)
    gcol = jnp.where(rows >= n_lead, -gprm_ref[0:1, :] * _softplus(gt + gprm_ref[1:2, :]), 0.0)
    lanes = i * ct + lax.broadcasted_iota(jnp.int32, gtt.shape, 1)
    grow = jnp.where(lanes >= n_lead, -gprmt_ref[:, 0:1] * _softplus(gtt + gprmt_ref[:, 1:2]), 0.0)
    ii = lax.broadcasted_iota(jnp.int32, (CHUNK, CHUNK), 0)
    jj = lax.broadcasted_iota(jnp.int32, (CHUNK, CHUNK), 1)
    lower = ii >= jj
    strict = ii > jj
    ltri = lower.astype(BF16)
    utri = (ii <= jj).astype(BF16)
    gcum_c = [sum(_dot(ltri, p) for p in _split3(gcol[rs(c)])) for c in range(nc)]
    gcum_r = [sum(_dot(p, utri) for p in _split3(grow[:, rs(c)])) for c in range(nc)]
    yield
    q_ = [qn[h][rs(c)] for c, h in items]
    k_ = [kn[h][rs(c)] for c, h in items]
    v_ = [cvb[2 * B_HEADS + h][rs(c)] for c, h in items]
    bh = [beta[rs(c), h:h + 1] for c, h in items]
    gc = [gcum_c[c][:, B_HEADS + h:B_HEADS + h + 1] for c, h in items]
    gr = [gcum_r[c][B_HEADS + h:B_HEADS + h + 1, :] for c, h in items]
    kb, a, qk = [], [], []
    for n in range(n_items):
        decay = jnp.exp(jnp.where(lower, gc[n] - gr[n], NEG))
        kb.append(k_[n] * bh[n])
        kq = _dotg(jnp.concatenate([kb[n], q_[n]], axis=0).astype(BF16), k_[n].astype(BF16), NT_DIMS)
        a.append(jnp.where(strict, kq[:CHUNK] * decay, 0.0))
        qk.append(jnp.where(lower, kq[CHUNK:] * decay, 0.0).astype(BF16))
        if n % B_HEADS == B_HEADS - 1:
            yield
    doff = [-jnp.where((ii >> 1) == (jj >> 1), a[n], 0.0) for n in range(n_items)]
    for lg in range(1, 6):
        m = ((ii >> (lg + 1)) == (jj >> (lg + 1))) & ((ii >> lg) != (jj >> lg))
        bm = [jnp.where(m, a[n], 0.0) for n in range(n_items)]
        db = [doff[n].astype(BF16) for n in range(n_items)]
        y = [bm[n] + _dot(db[n], bm[n].astype(BF16)) for n in range(n_items)]
        yield
        x = [y[n] + _dot(y[n].astype(BF16), db[n]) for n in range(n_items)]
        doff = [doff[n] - x[n] for n in range(n_items)]
        yield
    egc = [jnp.exp(gc[n]) for n in range(n_items)]
    g_last = [gc[n][CHUNK - 1:CHUNK, :] for n in range(n_items)]
    eg_last = [jnp.exp(g_last[n]) for n in range(n_items)]
    sol, kd, wq = [], [], []
    for n in range(n_items):
        rhs = jnp.concatenate([v_[n] * bh[n], kb[n] * egc[n]], axis=-1)
        sol.append(rhs + _dot(doff[n].astype(BF16), rhs.astype(BF16)))
        kd.append((k_[n] * jnp.exp(g_last[n] - gc[n])).astype(BF16))
        wq.append(jnp.concatenate([sol[n][:, B_DV:], q_[n] * egc[n]], axis=0).astype(BF16))
        if n % B_HEADS == B_HEADS - 1:
            yield

    def gate_store(c, o):
        for h in range(B_HEADS):
            z = env['zb'][rs(c), h * B_DV:(h + 1) * B_DV]
            mix_scr[rs(c), A_WIDTH + h * B_DV:A_WIDTH + (h + 1) * B_DV] = (
                _rms(o[h], onorm_ref[...]) * _silu(z)).astype(BF16)

    s_cur = [s_scr[h] for h in range(B_HEADS)]
    o_prev = None
    for c in range(nc):
        idx = [c * B_HEADS + h for h in range(B_HEADS)]
        ws = [_dot(wq[n], s_cur[h].astype(BF16)) for h, n in enumerate(idx)]
        if o_prev is not None:
            gate_store(c - 1, o_prev)
        yield
        v_new = [sol[n][:, :B_DV] - ws[h][:CHUNK] for h, n in enumerate(idx)]
        vb = [v_new[h].astype(BF16) for h in range(B_HEADS)]
        o_prev = [ws[h][CHUNK:] + _dot(qk[n], vb[h]) for h, n in enumerate(idx)]
        s_cur = [s_cur[h] * eg_last[n] + _dotg(kd[n], vb[h], TN_DIMS) for h, n in enumerate(idx)]
        yield
    for h in range(B_HEADS):
        s_scr[h] = s_cur[h]
    gate_store(nc - 1, o_prev)
    yield


def _ab_layer_staged_kernel(x_ref, g_ref, w_ref, wg_ref, wgt_ref, sink_ref, hist_ref, convw_ref, gprm_ref,
                            gprmt_ref, onorm_ref, wout_ref, convin_ref, sin_ref,
                            y_ref, kvtail_ref, xbtail_ref, sout_ref,
                            s_scr, xc_scr, kvprev_scr, mix_scr, *, ct, n_lead, pos_shift):
    i = pl.program_id(1)
    nsteps = pl.num_programs(1)
    ntail = min(WINDOW, ct)

    @pl.when(i == 0)
    def _():
        s_scr[...] = sin_ref[0]
        xc_scr[0:8, :] = convin_ref[0]
        kvprev_scr[1] = hist_ref[...]

    x = x_ref[0]
    hb = _rms(x, g_ref[...]).astype(BF16)
    proj = lambda a, b: _dot(hb, w_ref[:, a:b])
    env = {}

    def step(gen, n=1):
        for _ in range(n):
            next(gen, None)

    nblk = B_CONV_CH // LANES
    w = convw_ref[...]
    cvb = []

    def conv_block(j):
        cs = slice(j * LANES, (j + 1) * LANES)
        yc = xc_scr[pl.ds(8 - (CONV_W - 1), ct), cs] * w[0:1, cs]
        for t in range(1, CONV_W):
            yc = yc + xc_scr[pl.ds(8 - (CONV_W - 1) + t, ct), cs] * w[t:t + 1, cs]
        cvb.append(_silu(yc))

    pw = 2 * LANES
    for j in range(nblk // 2):
        xbj = proj(O_XB + j * pw, O_XB + (j + 1) * pw)
        xc_scr[8:8 + ct, j * pw:(j + 1) * pw] = xbj
        xbtail_ref[0, :, j * pw:(j + 1) * pw] = xbj[ct - 8:]
        if j > 0:
            conv_block(2 * j - 2)
            conv_block(2 * j - 1)
    gt = _dot(hb, wg_ref[...])
    gtt = _dotg(wgt_ref[...], hb, NT_DIMS)
    conv_block(nblk - 2)
    conv_block(nblk - 1)
    xc_scr[0:8, :] = xc_scr[ct:ct + 8, :]

    gdn = _gdn_stages(i, cvb, gt, gtt, env, gprm_ref, gprmt_ref, onorm_ref, s_scr, mix_scr, ct=ct, n_lead=n_lead)
    qa_p = []
    for j in range(A_WIDTH // pw):
        qa_p.append(proj(O_QA + j * pw, O_QA + (j + 1) * pw))
        step(gdn, 2)
    qa = jnp.concatenate(qa_p, axis=-1)
    kv = proj(O_KV, O_ZA)
    kvtail_ref[0] = kv[ct - ntail:]
    slot = lax.rem(i, 2)
    kvb = jnp.concatenate([kvprev_scr[1 - slot], kv], axis=0).astype(BF16)
    if ct >= WINDOW:
        kvprev_scr[slot] = kv[ct - WINDOW:]
    swa = _swa_stages(i, qa, kvb, env, sink_ref, mix_scr, ct=ct, sb=ntail, pos_shift=pos_shift)
    n_sw = (ct // ntail) * A_KV_HEADS
    step(gdn)
    for _ in range(ct // CHUNK):
        step(gdn)
        step(swa)
    step(swa, max(n_sw - ct // CHUNK, 0))
    za_p, zb_p = [], []
    fill = ([lambda: step(swa)] * (2 * n_sw)
            + [lambda j=j: za_p.append(proj(O_ZA + j * pw, O_ZA + (j + 1) * pw)) for j in range(A_WIDTH // pw)]
            + [lambda j=j: zb_p.append(proj(O_ZB + j * pw, O_ZB + (j + 1) * pw)) for j in range(B_VAL // pw)])
    per = -(-len(fill) // 10)
    for lv in range(10):
        step(gdn)
        for f in fill[lv * per:(lv + 1) * per]:
            f()
    for f in fill[10 * per:]:
        f()
    env['za'] = jnp.concatenate(za_p, axis=-1)
    env['zb'] = jnp.concatenate(zb_p, axis=-1)
    step(gdn, ct // CHUNK)
    step(swa, ct // ntail)
    ya = x + _dot(mix_scr[:, 0:A_WIDTH], wout_ref[0:A_WIDTH, :])
    for _ in gdn:
        pass
    y_ref[0] = ya + _dot(mix_scr[:, A_WIDTH:], wout_ref[A_WIDTH:, :])

    @pl.when(i == nsteps - 1)
    def _():
        sout_ref[0] = s_scr[...]


def _ab_layer_prompt(x3, norm_g, w_main, w_gate, w_gate_t, sink, hist_kv, conv_w8, gprm, gprm_t, onorm,
                     w_out, conv_in, s_in, ct, n_lead, pos_shift):
    B, L, _ = x3.shape
    ntail = min(WINDOW, ct)
    kern = functools.partial(_ab_layer_staged_kernel, ct=ct, n_lead=n_lead, pos_shift=pos_shift)
    full = lambda a: pl.BlockSpec(a.shape, lambda b, i: (0,) * a.ndim)
    blk = pl.BlockSpec((1, ct, D_MODEL), lambda b, i: (b, i, 0))
    per_b = lambda *s: pl.BlockSpec((1,) + s, lambda b, i: (b,) + (0,) * len(s))
    return pl.pallas_call(
        kern,
        grid=(B, L // ct),
        in_specs=[blk, full(norm_g), full(w_main), full(w_gate), full(w_gate_t), full(sink), full(hist_kv),
                  full(conv_w8), full(gprm), full(gprm_t), full(onorm), full(w_out),
                  per_b(8, B_CONV_CH), per_b(B_HEADS, B_DK, B_DV)],
        out_specs=[blk, per_b(ntail, 2 * A_KV_WIDTH), per_b(8, B_CONV_CH), per_b(B_HEADS, B_DK, B_DV)],
        out_shape=[jax.ShapeDtypeStruct((B, L, D_MODEL), F32),
                   jax.ShapeDtypeStruct((B, ntail, 2 * A_KV_WIDTH), F32),
                   jax.ShapeDtypeStruct((B, 8, B_CONV_CH), F32),
                   jax.ShapeDtypeStruct((B, B_HEADS, B_DK, B_DV), F32)],
        scratch_shapes=[pltpu.VMEM((B_HEADS, B_DK, B_DV), F32),
                        pltpu.VMEM((ct + 8, B_CONV_CH), F32),
                        pltpu.VMEM((2, WINDOW, 2 * A_KV_WIDTH), F32),
                        pltpu.VMEM((ct, AB_MIX), BF16)],
        compiler_params=_cparams(("arbitrary", "arbitrary")),
        name="ab_layer_prompt",
    )(x3, norm_g, w_main, w_gate, w_gate_t, sink, hist_kv, conv_w8, gprm, gprm_t, onorm, w_out, conv_in, s_in)


def _c_layer_kernel(x_ref, g_ref, w_ref, wgk_ref, wup_ref, bgk_ref, onorm_ref, wout_ref, gfin_ref,
                    sin_ref, y_ref, sout_ref, st_scr, og_scr, *, ct, n_lead):
    i = pl.program_id(1)
    nsteps = pl.num_programs(1)

    @pl.when(i == 0)
    def _():
        st_scr[...] = sin_ref[0]

    x = x_ref[0]
    hb = _rms(x, g_ref[...]).astype(BF16)
    q_all = _dot(hb, w_ref[:, 0:C_KEY])
    k_all = _dot(hb, w_ref[:, C_KEY:2 * C_KEY])
    v_all = _dot(hb, w_ref[:, 2 * C_KEY:2 * C_KEY + C_VAL]).astype(BF16)
    z_all = _dot(hb, w_ref[:, 2 * C_KEY + C_VAL:])
    la_all = _gla_log_decay(hb, wgk_ref, wup_ref, bgk_ref)
    rows = i * ct + lax.broadcasted_iota(jnp.int32, la_all.shape, 0)
    la_all = jnp.where(rows >= n_lead, la_all, 0.0)

    ii = lax.broadcasted_iota(jnp.int32, (CHUNK, CHUNK), 0)
    jj = lax.broadcasted_iota(jnp.int32, (CHUNK, CHUNK), 1)
    lower = ii >= jj
    ltri = lower.astype(BF16)
    nc = ct // CHUNK
    rs = lambda c: slice(c * CHUNK, (c + 1) * CHUNK)
    ks = lambda h: slice(h * C_DK, (h + 1) * C_DK)
    vs = lambda h: slice(h * C_DV, (h + 1) * C_DV)
    items = [(c, h) for c in range(nc) for h in range(C_HEADS)]
    n_items = len(items)
    bc_all = [sum(_dot(ltri, p) for p in _split3(la_all[rs(c)])) for c in range(nc)]
    bc = [bc_all[c][:, ks(h)] for c, h in items]
    k_ = [k_all[rs(c), ks(h)] for c, h in items]
    v_ = [v_all[rs(c), vs(h)] for c, h in items]
    qd = [(q_all[rs(c), ks(h)] * (C_DK ** -0.5) * jnp.exp(bc[n])).astype(BF16) for n, (c, h) in enumerate(items)]
    kinv = [(k_[n] * jnp.exp(-bc[n])).astype(BF16) for n in range(n_items)]
    bl = [bc[n][CHUNK - 1:CHUNK, :] for n in range(n_items)]
    kd = [(k_[n] * jnp.exp(bl[n] - bc[n])).astype(BF16) for n in range(n_items)]
    gl = [jnp.exp(bl[n]) for n in range(n_items)]
    qk = [jnp.where(lower, _dotg(qd[n], kinv[n], NT_DIMS), 0.0).astype(BF16) for n in range(n_items)]
    o_intra = [_dot(qk[n], v_[n]) for n in range(n_items)]

    st = [st_scr[h] for h in range(C_HEADS)]
    for c in range(nc):
        idx = [c * C_HEADS + h for h in range(C_HEADS)]
        o = [o_intra[n] + _dotg(qd[n], st[h].astype(BF16), NT_DIMS) for h, n in enumerate(idx)]
        st = [st[h] * gl[n] + _dotg(v_[n], kd[n], TN_DIMS) for h, n in enumerate(idx)]
        for h in range(C_HEADS):
            og_scr[rs(c), vs(h)] = (_rms(o[h], onorm_ref[...]) * _silu(z_all[rs(c), vs(h)])).astype(BF16)
    for h in range(C_HEADS):
        st_scr[h] = st[h]

    y_ref[0] = _rms(x + _dot(og_scr[...], wout_ref[...]), gfin_ref[...])

    @pl.when(i == nsteps - 1)
    def _():
        sout_ref[0] = st_scr[...]


def _c_layer_prompt(x3, norm_g, w_main, w_gk, w_up, b_gk, onorm, w_out, final_g, st_in, ct, n_lead):
    B, L, _ = x3.shape
    kern = functools.partial(_c_layer_kernel, ct=ct, n_lead=n_lead)
    full = lambda a: pl.BlockSpec(a.shape, lambda b, i: (0,) * a.ndim)
    blk = pl.BlockSpec((1, ct, D_MODEL), lambda b, i: (b, i, 0))
    st = pl.BlockSpec((1, C_HEADS, C_DV, C_DK), lambda b, i: (b, 0, 0, 0))
    return pl.pallas_call(
        kern,
        grid=(B, L // ct),
        in_specs=[blk, full(norm_g), full(w_main), full(w_gk), full(w_up), full(b_gk), full(onorm),
                  full(w_out), full(final_g), st],
        out_specs=[blk, st],
        out_shape=[jax.ShapeDtypeStruct((B, L, D_MODEL), F32),
                   jax.ShapeDtypeStruct((B, C_HEADS, C_DV, C_DK), F32)],
        scratch_shapes=[pltpu.VMEM((C_HEADS, C_DV, C_DK), F32), pltpu.VMEM((ct, C_VAL), BF16)],
        compiler_params=_cparams(("arbitrary", "arbitrary")),
        name="c_layer_prompt",
    )(x3, norm_g, w_main, w_gk, w_up, b_gk, onorm, w_out, final_g, st_in)


def _ab_proj_kernel(x_ref, g_ref, w_ref, wg_ref, qa_ref, kv_ref, za_ref, xb_ref, zb_ref, gates_ref):
    hb = _rms(x_ref[...], g_ref[...]).astype(BF16)
    qa_ref[...] = _dot(hb, w_ref[:, O_QA:O_KV])
    kv_ref[...] = _dot(hb, w_ref[:, O_KV:O_ZA])
    za_ref[...] = _dot(hb, w_ref[:, O_ZA:O_XB])
    xb_ref[...] = _dot(hb, w_ref[:, O_XB:O_ZB])
    zb_ref[...] = _dot(hb, w_ref[:, O_ZB:AB_MAIN])
    gates_ref[...] = _dot(hb, wg_ref[...])


def _ab_proj(x2d, norm_g, w_main, w_gate):
    n = x2d.shape[0]
    full = lambda a: pl.BlockSpec(a.shape, lambda: (0,) * a.ndim)
    widths = (A_WIDTH, 2 * A_KV_WIDTH, A_WIDTH, B_CONV_CH, B_VAL, LANES)
    args = (x2d, norm_g, w_main, w_gate)
    return pl.pallas_call(
        _ab_proj_kernel,
        in_specs=[full(a) for a in args],
        out_specs=[pl.BlockSpec((n, w), lambda: (0, 0)) for w in widths],
        out_shape=[jax.ShapeDtypeStruct((n, w), F32) for w in widths],
        compiler_params=pltpu.CompilerParams(vmem_limit_bytes=VMEM_LIMIT),
        name="ab_in_proj_sample",
    )(*args)


def _swa_sample_kernel(q_ref, kvn_ref, za_ref, ck_ref, cv_ref, sink_ref, o_ref, nk_ref, nv_ref, *, bt):
    row = lax.broadcasted_iota(jnp.int32, (WINDOW, A_KV_WIDTH), 0)
    hrow = lax.broadcasted_iota(jnp.int32, (A_HEADS, A_HD), 0)
    sk = sink_ref[...]
    for b in range(bt):
        kn = kvn_ref[b:b + 1, 0:A_KV_WIDTH]
        vn = kvn_ref[b:b + 1, A_KV_WIDTH:]
        nk = jnp.where(row == WINDOW - 1, kn, pltpu.roll(ck_ref[b], WINDOW - 1, 0))
        nv = jnp.where(row == WINDOW - 1, vn, pltpu.roll(cv_ref[b], WINDOW - 1, 0))
        nk_ref[b] = nk
        nv_ref[b] = nv
        q8 = q_ref[b].astype(BF16)
        nkb = nk.astype(BF16)
        nvb = nv.astype(BF16)
        o8 = None
        for j in range(A_KV_HEADS):
            sc = _dotg(q8, nkb[:, j * A_HD:(j + 1) * A_HD], NT_DIMS) * (A_HD ** -0.5)
            m = jnp.maximum(jnp.max(sc, axis=-1, keepdims=True), sk)
            p = jnp.exp(sc - m)
            den = jnp.sum(p, axis=-1, keepdims=True) + jnp.exp(sk - m)
            oj = _dot(p.astype(BF16), nvb[:, j * A_HD:(j + 1) * A_HD]) / den
            o8 = oj if o8 is None else jnp.where(hrow >= j * A_GROUP, oj, o8)
        o_ref[b] = o8 * _silu(za_ref[b])


def _swa_sample(q3, kv_new, za3, cache_k, cache_v, sink_col, bt):
    nb = q3.shape[0]
    hd = pl.BlockSpec((bt, A_HEADS, A_HD), lambda i: (i, 0, 0))
    cache = pl.BlockSpec((bt, WINDOW, A_KV_WIDTH), lambda i: (i, 0, 0))
    return pl.pallas_call(
        functools.partial(_swa_sample_kernel, bt=bt),
        grid=(nb // bt,),
        in_specs=[hd, pl.BlockSpec((bt, 2 * A_KV_WIDTH), lambda i: (i, 0)), hd, cache, cache,
                  pl.BlockSpec(sink_col.shape, lambda i: (0, 0))],
        out_specs=[hd, cache, cache],
        out_shape=[jax.ShapeDtypeStruct(q3.shape, F32),
                   jax.ShapeDtypeStruct(cache_k.shape, F32),
                   jax.ShapeDtypeStruct(cache_v.shape, F32)],
        compiler_params=_cparams(("arbitrary",)),
        name="swa_sample",
    )(q3, kv_new, za3, cache_k, cache_v, sink_col)


def _gdn_sample_prep_kernel(xb_ref, h0_ref, h1_ref, h2_ref, gates_ref, convw_ref, gprm_ref,
                            q_ref, k_ref, v_ref, bg_ref):
    w = convw_ref[...]
    y = (h0_ref[...] * w[0:1, :] + h1_ref[...] * w[1:2, :] + h2_ref[...] * w[2:3, :]
         + xb_ref[...] * w[3:4, :])
    cv = _silu(y)
    for h in range(B_HEADS):
        q = cv[:, h * B_DK:(h + 1) * B_DK]
        k = cv[:, B_KEY + h * B_DK:B_KEY + (h + 1) * B_DK]
        q_ref[:, h * B_DK:(h + 1) * B_DK] = (
            q * lax.rsqrt(jnp.sum(q * q, axis=-1, keepdims=True) + EPS) * (B_DK ** -0.5))
        k_ref[:, h * B_DK:(h + 1) * B_DK] = k * lax.rsqrt(jnp.sum(k * k, axis=-1, keepdims=True) + EPS)
    v_ref[...] = cv[:, 2 * B_KEY:]
    gt = gates_ref[...]
    col = lax.broadcasted_iota(jnp.int32, gt.shape, 1)
    g = -gprm_ref[0:1, :] * _softplus(gt + gprm_ref[1:2, :])
    bg_ref[...] = jnp.where(col < B_HEADS, _sigmoid(gt), jnp.exp(g))


def _gdn_sample_prep(xb, h0, h1, h2, gates, conv_w8, gprm):
    n = xb.shape[0]
    full = lambda a: pl.BlockSpec(a.shape, lambda: (0,) * a.ndim)
    args = (xb, h0, h1, h2, gates, conv_w8, gprm)
    return pl.pallas_call(
        _gdn_sample_prep_kernel,
        in_specs=[full(a) for a in args],
        out_specs=[pl.BlockSpec((n, B_KEY), lambda: (0, 0))] * 3 + [pl.BlockSpec((n, LANES), lambda: (0, 0))],
        out_shape=[jax.ShapeDtypeStruct((n, B_KEY), F32)] * 3 + [jax.ShapeDtypeStruct((n, LANES), F32)],
        compiler_params=pltpu.CompilerParams(vmem_limit_bytes=VMEM_LIMIT),
        name="gdn_sample_prep",
    )(*args)


def _gdn_sample_kernel(s_ref, qt_ref, kt_ref, v_ref, bg_ref, zb_ref, onorm_ref, o_ref, sout_ref, *, bt):
    for b in range(bt):
        for h in range(B_HEADS):
            s = s_ref[b, h]
            kc = kt_ref[0, h, :, b:b + 1]
            qc = qt_ref[0, h, :, b:b + 1]
            v = v_ref[b:b + 1, h * B_DV:(h + 1) * B_DV]
            beta = bg_ref[b:b + 1, h:h + 1]
            eg = bg_ref[b:b + 1, B_HEADS + h:B_HEADS + h + 1]
            ks = jnp.sum(kc * s, axis=0, keepdims=True)
            v_new = beta * (v - eg * ks)
            s_new = eg * s + kc * v_new
            sout_ref[b, h] = s_new
            o = jnp.sum(qc * s_new, axis=0, keepdims=True)
            z = zb_ref[b:b + 1, h * B_DV:(h + 1) * B_DV]
            o_ref[b:b + 1, h * B_DV:(h + 1) * B_DV] = _rms(o, onorm_ref[...]) * _silu(z)


def _gdn_sample(state, qt, kt, v, bg, zb, onorm, bt):
    nb = state.shape[0]
    st = pl.BlockSpec((bt, B_HEADS, B_DK, B_DV), lambda i: (i, 0, 0, 0))
    col = pl.BlockSpec((1, B_HEADS, B_DK, bt), lambda i: (i, 0, 0, 0))
    row = lambda n: pl.BlockSpec((bt, n), lambda i: (i, 0))
    return pl.pallas_call(
        functools.partial(_gdn_sample_kernel, bt=bt),
        grid=(nb // bt,),
        in_specs=[st, col, col, row(B_VAL), row(LANES), row(B_VAL),
                  pl.BlockSpec(onorm.shape, lambda i: (0, 0))],
        out_specs=[row(B_VAL), st],
        out_shape=[jax.ShapeDtypeStruct((nb, B_VAL), F32), jax.ShapeDtypeStruct(state.shape, F32)],
        compiler_params=_cparams(("arbitrary",)),
        name="gdn_sample",
    )(state, qt, kt, v, bg, zb, onorm)


def _ab_out_c_proj_kernel(x_ref, oa_ref, ob_ref, wout_ref, g_ref, w_ref, wgk_ref, wup_ref, bgk_ref,
                          y_ref, q_ref, k_ref, v_ref, z_ref, la_ref):
    y = (x_ref[...] + _dot(oa_ref[...].astype(BF16), wout_ref[0:A_WIDTH, :])
         + _dot(ob_ref[...].astype(BF16), wout_ref[A_WIDTH:, :]))
    y_ref[...] = y
    hb = _rms(y, g_ref[...]).astype(BF16)
    q_ref[...] = _dot(hb, w_ref[:, 0:C_KEY])
    k_ref[...] = _dot(hb, w_ref[:, C_KEY:2 * C_KEY])
    v_ref[...] = _dot(hb, w_ref[:, 2 * C_KEY:2 * C_KEY + C_VAL])
    z_ref[...] = _dot(hb, w_ref[:, 2 * C_KEY + C_VAL:])
    la_ref[...] = _gla_log_decay(hb, wgk_ref, wup_ref, bgk_ref)


def _ab_out_c_proj(x2d, oa, ob, w_out, norm_g, w_main, w_gk, w_up, b_gk):
    n = x2d.shape[0]
    full = lambda a: pl.BlockSpec(a.shape, lambda: (0,) * a.ndim)
    widths = (D_MODEL, C_KEY, C_KEY, C_VAL, C_VAL, C_KEY)
    args = (x2d, oa, ob, w_out, norm_g, w_main, w_gk, w_up, b_gk)
    return pl.pallas_call(
        _ab_out_c_proj_kernel,
        in_specs=[full(a) for a in args],
        out_specs=[pl.BlockSpec((n, w), lambda: (0, 0)) for w in widths],
        out_shape=[jax.ShapeDtypeStruct((n, w), F32) for w in widths],
        compiler_params=pltpu.CompilerParams(vmem_limit_bytes=VMEM_LIMIT),
        name="ab_out_c_in_proj_sample",
    )(*args)


def _gla_sample_kernel(s_ref, qt_ref, kt_ref, lat_ref, v_ref, z_ref, onorm_ref, o_ref, sout_ref, *, bt):
    for b in range(bt):
        for h in range(C_HEADS):
            s = s_ref[b, h]
            kc = kt_ref[0, h, :, b:b + 1]
            qc = qt_ref[0, h, :, b:b + 1] * (C_DK ** -0.5)
            ac = jnp.exp(lat_ref[0, h, :, b:b + 1])
            v = v_ref[b:b + 1, h * C_DV:(h + 1) * C_DV]
            s_new = ac * s + kc * v
            sout_ref[b, h] = s_new
            o = jnp.sum(qc * s_new, axis=0, keepdims=True)
            z = z_ref[b:b + 1, h * C_DV:(h + 1) * C_DV]
            o_ref[b:b + 1, h * C_DV:(h + 1) * C_DV] = _rms(o, onorm_ref[...]) * _silu(z)


def _gla_sample(state, qt, kt, lat, v, z, onorm, bt):
    nb = state.shape[0]
    st = pl.BlockSpec((bt, C_HEADS, C_DK, C_DV), lambda i: (i, 0, 0, 0))
    col = pl.BlockSpec((1, C_HEADS, C_DK, bt), lambda i: (i, 0, 0, 0))
    row = lambda n: pl.BlockSpec((bt, n), lambda i: (i, 0))
    return pl.pallas_call(
        functools.partial(_gla_sample_kernel, bt=bt),
        grid=(nb // bt,),
        in_specs=[st, col, col, col, row(C_VAL), row(C_VAL),
                  pl.BlockSpec(onorm.shape, lambda i: (0, 0))],
        out_specs=[row(C_VAL), st],
        out_shape=[jax.ShapeDtypeStruct((nb, C_VAL), F32), jax.ShapeDtypeStruct(state.shape, F32)],
        compiler_params=_cparams(("arbitrary",)),
        name="gla_sample",
    )(state, qt, kt, lat, v, z, onorm)


def _c_out_kernel(x_ref, o_ref, w_ref, g_ref, y_ref):
    y_ref[...] = _rms(x_ref[...] + _dot(o_ref[...].astype(BF16), w_ref[...]), g_ref[...])


def _c_out(x2d, oc, w_out, final_g):
    n = x2d.shape[0]
    full = lambda a: pl.BlockSpec(a.shape, lambda: (0,) * a.ndim)
    args = (x2d, oc, w_out, final_g)
    return pl.pallas_call(
        _c_out_kernel,
        in_specs=[full(a) for a in args],
        out_specs=pl.BlockSpec((n, D_MODEL), lambda: (0, 0)),
        out_shape=jax.ShapeDtypeStruct((n, D_MODEL), F32),
        compiler_params=pltpu.CompilerParams(vmem_limit_bytes=VMEM_LIMIT),
        name="c_out_proj_norm_sample",
    )(*args)


def _to_cols(a, heads, dk, bt):
    n = a.shape[0]
    return a.reshape(n // bt, bt, heads, dk).transpose(0, 2, 3, 1)


def kernel(x_prompt, x_sample, cache_swa_k, cache_swa_v, state_dn_conv, state_dn, state_gla,
           meta_tokens, norm_ab, w_in_ab, sink_a, conv_b, a_log_b, dt_bias_b, onorm_b, w_out_ab,
           norm_c, w_in_c, w_gk_up, b_gk, onorm_c, w_out_c, final_norm):
    Bp, L, _ = x_prompt.shape
    Bs = x_sample.shape[0]
    CT, BT = 256, 8

    w_ab = w_in_ab[0]
    w_ab_main = w_ab[:, :AB_MAIN].astype(BF16)
    w_ab_gate = jnp.pad(w_ab[:, AB_MAIN:], ((0, 0), (0, LANES - 2 * B_HEADS))).astype(BF16)
    w_ab_gate_t = jnp.pad(w_ab[:, AB_MAIN:].T, ((0, 16 - 2 * B_HEADS), (0, 0))).astype(BF16)
    g_ab = norm_ab[0][None, :]
    sink_row = sink_a[0][None, :]
    sink_col = sink_a[0][:, None]
    conv_w8 = jnp.pad(conv_b[0], ((0, 8 - CONV_W), (0, 0)))
    ea = jnp.exp(a_log_b[0])
    gprm = jnp.zeros((8, LANES), F32).at[0, B_HEADS:2 * B_HEADS].set(ea).at[1, B_HEADS:2 * B_HEADS].set(dt_bias_b[0])
    gprm_t = jnp.zeros((16, LANES), F32).at[B_HEADS:2 * B_HEADS, 0].set(ea).at[B_HEADS:2 * B_HEADS, 1].set(dt_bias_b[0])
    onb = onorm_b[0][None, :]
    w_ab_out = w_out_ab[0].astype(BF16)
    w_c = w_in_c[0]
    w_c_main = w_c[:, :C_MAIN].astype(BF16)
    w_c_gk = jnp.pad(w_c[:, C_MAIN:], ((0, 0), (0, LANES - C_RANK))).astype(BF16)
    w_up = jnp.pad(w_gk_up[0], ((0, LANES - C_RANK), (0, 0)))
    bgk = b_gk[0][None, :]
    g_c = norm_c[0][None, :]
    onc = onorm_c[0][None, :]
    w_c_out = w_out_c[0].astype(BF16)
    g_fin = final_norm[None, :]

    def ab_layer(x3, ct, hist_kv, pos_shift, conv_in, s_in, n_lead):
        return _ab_layer_prompt(x3, g_ab, w_ab_main, w_ab_gate, w_ab_gate_t, sink_row, hist_kv, conv_w8,
                                gprm, gprm_t, onb, w_ab_out, conv_in, s_in, ct, n_lead, pos_shift)

    def c_layer(x3, ct, st_in, n_lead):
        return _c_layer_prompt(x3, g_c, w_c_main, w_c_gk, w_up, bgk, onc, w_c_out, g_fin, st_in, ct, n_lead)

    xpre = jnp.concatenate([jnp.zeros((LEAD, D_MODEL), F32), meta_tokens], axis=0)[None]
    y_pre, kv_pre, xbtail_pre, sdn_pre = ab_layer(
        xpre, CHUNK, jnp.zeros((WINDOW, 2 * A_KV_WIDTH), F32), -LEAD,
        jnp.zeros((1, 8, B_CONV_CH), F32), jnp.zeros((1, B_HEADS, B_DK, B_DV), F32), LEAD)
    _, sgla_pre = c_layer(y_pre, CHUNK, jnp.zeros((1, C_HEADS, C_DV, C_DK), F32), LEAD)

    hist_kv = jnp.concatenate([jnp.zeros((WINDOW - CHUNK, 2 * A_KV_WIDTH), F32), kv_pre[0]], axis=0)
    y1, kv_tail, xb_tail, sdn_p = ab_layer(
        x_prompt, CT, hist_kv, N_META, jnp.broadcast_to(xbtail_pre, (Bp, 8, B_CONV_CH)),
        jnp.broadcast_to(sdn_pre, (Bp, B_HEADS, B_DK, B_DV)), 0)
    y_prompt, sgla_t = c_layer(y1, CT, jnp.broadcast_to(sgla_pre, (Bp, C_HEADS, C_DV, C_DK)), 0)
    sgla_p = jnp.swapaxes(sgla_t, 2, 3)

    swa_k_p = kv_tail[:, :, :A_KV_WIDTH].reshape(1, Bp, WINDOW, A_KV_HEADS, A_HD)
    swa_v_p = kv_tail[:, :, A_KV_WIDTH:].reshape(1, Bp, WINDOW, A_KV_HEADS, A_HD)
    dn_conv_p = xb_tail[:, 8 - (CONV_W - 1):, :][None]

    xs = x_sample.reshape(Bs, D_MODEL)
    qa, kvn, za, xb, zb, gates = _ab_proj(xs, g_ab, w_ab_main, w_ab_gate)
    oa3, nk, nv = _swa_sample(qa.reshape(Bs, A_HEADS, A_HD), kvn, za.reshape(Bs, A_HEADS, A_HD),
                              cache_swa_k[0].reshape(Bs, WINDOW, A_KV_WIDTH),
                              cache_swa_v[0].reshape(Bs, WINDOW, A_KV_WIDTH), sink_col, BT)
    hist = state_dn_conv[0]
    qn, kn, vn, bg = _gdn_sample_prep(xb, hist[:, 0], hist[:, 1], hist[:, 2], gates, conv_w8, gprm)
    ob, sdn_s = _gdn_sample(state_dn[0], _to_cols(qn, B_HEADS, B_DK, BT), _to_cols(kn, B_HEADS, B_DK, BT),
                            vn, bg, zb, onb, BT)
    ys, qc, kc, vc, zc, la = _ab_out_c_proj(xs, oa3.reshape(Bs, A_WIDTH), ob, w_ab_out,
                                            g_c, w_c_main, w_c_gk, w_up, bgk)
    ocs, sgla_s = _gla_sample(state_gla[0], _to_cols(qc, C_HEADS, C_DK, BT), _to_cols(kc, C_HEADS, C_DK, BT),
                              _to_cols(la, C_HEADS, C_DK, BT), vc, zc, onc, BT)
    y_sample = _c_out(ys, ocs, w_c_out, g_fin).reshape(Bs, 1, D_MODEL)
    dn_conv_s = jnp.concatenate([hist[:, 1:], xb[:, None, :]], axis=1)[None]

    return (y_prompt, y_sample, swa_k_p, swa_v_p, dn_conv_p, sdn_p[None], sgla_p[None],
            nk.reshape(1, Bs, WINDOW, A_KV_HEADS, A_HD), nv.reshape(1, Bs, WINDOW, A_KV_HEADS, A_HD),
            dn_conv_s, sdn_s[None], sgla_s[None])
```

```python
import functools

import jax
import jax.numpy as jnp
from jax import lax
from jax.experimental import pallas as pl
from jax.experimental.pallas import tpu as pltpu

F32 = jnp.float32
BF16 = jnp.bfloat16
EPS = 1e-6

D_MODEL = 1024
N_META = 16
CHUNK = 64
LEAD = (-N_META) % CHUNK
A_HEADS, A_KV_HEADS, A_HD = 8, 2, 64
A_GROUP = A_HEADS // A_KV_HEADS
A_WIDTH = A_HEADS * A_HD
A_KV_WIDTH = A_KV_HEADS * A_HD
WINDOW = 128
B_HEADS, B_DK, B_DV = 4, 128, 128
B_KEY = B_HEADS * B_DK
B_VAL = B_HEADS * B_DV
CONV_W = 4
B_CONV_CH = 2 * B_KEY + B_VAL
C_HEADS, C_DK, C_DV = 4, 128, 256
C_KEY = C_HEADS * C_DK
C_VAL = C_HEADS * C_DV
C_RANK = 16
C_GATE_NORM = 16.0
AB_MAIN = 2 * A_WIDTH + 2 * A_KV_WIDTH + B_CONV_CH + B_VAL
AB_MIX = A_WIDTH + B_VAL
C_MAIN = 2 * C_KEY + 2 * C_VAL
O_QA, O_KV, O_ZA, O_XB, O_ZB = 0, 512, 768, 1280, 2816
LANES = 128
NEG = -1e30

VMEM_LIMIT = 56 * 1024 * 1024

NT_DIMS = (((1,), (1,)), ((), ()))
TN_DIMS = (((0,), (0,)), ((), ()))


def _cparams(sem):
    return pltpu.CompilerParams(dimension_semantics=sem, vmem_limit_bytes=VMEM_LIMIT)


def _dot(a, b):
    return jnp.dot(a, b, preferred_element_type=F32)


def _dotg(a, b, dims):
    return lax.dot_general(a, b, dims, preferred_element_type=F32)


def _split3(a):
    hi = a.astype(BF16)
    r = a - hi.astype(F32)
    mid = r.astype(BF16)
    lo = (r - mid.astype(F32)).astype(BF16)
    return hi, mid, lo


def _sigmoid(x):
    return 1.0 / (1.0 + jnp.exp(-x))


def _silu(x):
    return x * _sigmoid(x)


def _softplus(x):
    return jnp.maximum(x, 0.0) + jnp.log1p(jnp.exp(-jnp.abs(x)))


def _log_sigmoid(x):
    return jnp.minimum(x, 0.0) - jnp.log1p(jnp.exp(-jnp.abs(x)))


def _rms(x, g):
    return x * lax.rsqrt(jnp.mean(x * x, axis=-1, keepdims=True) + EPS) * g


def _gla_log_decay(hb, wgk_ref, wup_ref, bgk_ref):
    low = _dot(hb, wgk_ref[...])
    lh, lm, _ = _split3(low)
    wh, wm, _ = _split3(wup_ref[...])
    up = _dot(lh, wh) + _dot(lh, wm) + _dot(lm, wh)
    return _log_sigmoid(up + bgk_ref[...]) * (1.0 / C_GATE_NORM)


def _swa_tile(i, qa, kvb, za, sink_ref, mix_scr, *, ct, sb, pos_shift):
    r = lax.broadcasted_iota(jnp.int32, (sb, WINDOW + sb), 0)
    c = lax.broadcasted_iota(jnp.int32, (sb, WINDOW + sb), 1)
    diff = r - c + WINDOW
    in_window = (diff >= 0) & (diff < WINDOW)
    nsb = ct // sb
    items = [(s, j) for s in range(nsb) for j in range(A_KV_HEADS)]
    masks = [jnp.concatenate([in_window & (i * ct + s * sb + c - WINDOW + pos_shift >= 0)] * A_GROUP, axis=0)
             for s in range(nsb)]
    sinks = [jnp.concatenate([jnp.broadcast_to(sink_ref[0:1, j * A_GROUP + g:j * A_GROUP + g + 1], (sb, 1))
                              for g in range(A_GROUP)], axis=0) for j in range(A_KV_HEADS)]
    qs = [jnp.concatenate([qa[s * sb:(s + 1) * sb, (j * A_GROUP + g) * A_HD:(j * A_GROUP + g + 1) * A_HD]
                           for g in range(A_GROUP)], axis=0).astype(BF16) for s, j in items]
    sc = [_dotg(qs[n], kvb[s * sb:s * sb + WINDOW + sb, j * A_HD:(j + 1) * A_HD], NT_DIMS) * (A_HD ** -0.5)
          for n, (s, j) in enumerate(items)]
    sc = [jnp.where(masks[s], sc[n], NEG) for n, (s, j) in enumerate(items)]
    mx = [jnp.maximum(jnp.max(sc[n], axis=-1, keepdims=True), sinks[j]) for n, (s, j) in enumerate(items)]
    p = [jnp.exp(sc[n] - mx[n]) for n in range(len(items))]
    den = [jnp.sum(p[n], axis=-1, keepdims=True) + jnp.exp(sinks[j] - mx[n]) for n, (s, j) in enumerate(items)]
    pv = [_dot(p[n].astype(BF16),
               kvb[s * sb:s * sb + WINDOW + sb, A_KV_WIDTH + j * A_HD:A_KV_WIDTH + (j + 1) * A_HD]) / den[n]
          for n, (s, j) in enumerate(items)]
    for s in range(nsb):
        o = jnp.concatenate([pv[s * A_KV_HEADS + j][g * sb:(g + 1) * sb]
                             for j in range(A_KV_HEADS) for g in range(A_GROUP)], axis=-1)
        mix_scr[s * sb:(s + 1) * sb, 0:A_WIDTH] = (o * _silu(za[s * sb:(s + 1) * sb])).astype(BF16)


def _gdn_tile(i, cv, gt, gtt, zb, gprm_ref, gprmt_ref, onorm_ref, s_scr, mix_scr, *, ct, n_lead):
    beta = _sigmoid(gt)
    rows = i * ct + lax.broadcasted_iota(jnp.int32, gt.shape, 0)
    gcol = jnp.where(rows >= n_lead, -gprm_ref[0:1, :] * _softplus(gt + gprm_ref[1:2, :]), 0.0)
    lanes = i * ct + lax.broadcasted_iota(jnp.int32, gtt.shape, 1)
    grow = jnp.where(lanes >= n_lead, -gprmt_ref[:, 0:1] * _softplus(gtt + gprmt_ref[:, 1:2]), 0.0)

    ii = lax.broadcasted_iota(jnp.int32, (CHUNK, CHUNK), 0)
    jj = lax.broadcasted_iota(jnp.int32, (CHUNK, CHUNK), 1)
    lower = ii >= jj
    strict = ii > jj
    ltri = lower.astype(BF16)
    utri = (ii <= jj).astype(BF16)

    nc = ct // CHUNK
    items = [(c, h) for c in range(nc) for h in range(B_HEADS)]
    n_items = len(items)
    rs = lambda c: slice(c * CHUNK, (c + 1) * CHUNK)
    qn, kn = [], []
    for h in range(B_HEADS):
        q = cv[:, h * B_DK:(h + 1) * B_DK]
        k = cv[:, B_KEY + h * B_DK:B_KEY + (h + 1) * B_DK]
        qn.append(q * lax.rsqrt(jnp.sum(q * q, axis=-1, keepdims=True) + EPS) * (B_DK ** -0.5))
        kn.append(k * lax.rsqrt(jnp.sum(k * k, axis=-1, keepdims=True) + EPS))
    gcum_c = [sum(_dot(ltri, p) for p in _split3(gcol[rs(c)])) for c in range(nc)]
    gcum_r = [sum(_dot(p, utri) for p in _split3(grow[:, rs(c)])) for c in range(nc)]
    q_ = [qn[h][rs(c)] for c, h in items]
    k_ = [kn[h][rs(c)] for c, h in items]
    v_ = [cv[rs(c), 2 * B_KEY + h * B_DV:2 * B_KEY + (h + 1) * B_DV] for c, h in items]
    bh = [beta[rs(c), h:h + 1] for c, h in items]
    gc = [gcum_c[c][:, B_HEADS + h:B_HEADS + h + 1] for c, h in items]
    gr = [gcum_r[c][B_HEADS + h:B_HEADS + h + 1, :] for c, h in items]
    decay = [jnp.exp(jnp.where(lower, gc[n] - gr[n], NEG)) for n in range(n_items)]
    kb = [k_[n] * bh[n] for n in range(n_items)]
    kbf = [k_[n].astype(BF16) for n in range(n_items)]
    kq = [_dotg(jnp.concatenate([kb[n], q_[n]], axis=0).astype(BF16), kbf[n], NT_DIMS) for n in range(n_items)]
    a = [jnp.where(strict, kq[n][:CHUNK] * decay[n], 0.0) for n in range(n_items)]
    qk = [jnp.where(lower, kq[n][CHUNK:] * decay[n], 0.0).astype(BF16) for n in range(n_items)]
    doff = [-jnp.where((ii >> 1) == (jj >> 1), a[n], 0.0) for n in range(n_items)]
    for lg in range(1, 6):
        m = ((ii >> (lg + 1)) == (jj >> (lg + 1))) & ((ii >> lg) != (jj >> lg))
        bm = [jnp.where(m, a[n], 0.0) for n in range(n_items)]
        db = [doff[n].astype(BF16) for n in range(n_items)]
        y = [bm[n] + _dot(db[n], bm[n].astype(BF16)) for n in range(n_items)]
        x = [y[n] + _dot(y[n].astype(BF16), db[n]) for n in range(n_items)]
        doff = [doff[n] - x[n] for n in range(n_items)]
    egc = [jnp.exp(gc[n]) for n in range(n_items)]
    rhs = [jnp.concatenate([v_[n] * bh[n], kb[n] * egc[n]], axis=-1) for n in range(n_items)]
    sol = [rhs[n] + _dot(doff[n].astype(BF16), rhs[n].astype(BF16)) for n in range(n_items)]
    g_last = [gc[n][CHUNK - 1:CHUNK, :] for n in range(n_items)]
    kd = [(k_[n] * jnp.exp(g_last[n] - gc[n])).astype(BF16) for n in range(n_items)]
    wq = [jnp.concatenate([sol[n][:, B_DV:], q_[n] * egc[n]], axis=0).astype(BF16) for n in range(n_items)]
    eg_last = [jnp.exp(g_last[n]) for n in range(n_items)]

    s_cur = [s_scr[h] for h in range(B_HEADS)]
    for c in range(nc):
        idx = [c * B_HEADS + h for h in range(B_HEADS)]
        ws = [_dot(wq[n], s_cur[h].astype(BF16)) for h, n in enumerate(idx)]
        v_new = [sol[n][:, :B_DV] - ws[h][:CHUNK] for h, n in enumerate(idx)]
        vb = [v_new[h].astype(BF16) for h in range(B_HEADS)]
        o = [ws[h][CHUNK:] + _dot(qk[n], vb[h]) for h, n in enumerate(idx)]
        s_cur = [s_cur[h] * eg_last[n] + _dotg(kd[n], vb[h], TN_DIMS) for h, n in enumerate(idx)]
        for h in range(B_HEADS):
            z = zb[rs(c), h * B_DV:(h + 1) * B_DV]
            mix_scr[rs(c), A_WIDTH + h * B_DV:A_WIDTH + (h + 1) * B_DV] = (
                _rms(o[h], onorm_ref[...]) * _silu(z)).astype(BF16)
    for h in range(B_HEADS):
        s_scr[h] = s_cur[h]


def _ab_layer_kernel(x_ref, g_ref, w_ref, wg_ref, wgt_ref, sink_ref, hist_ref, convw_ref, gprm_ref,
                     gprmt_ref, onorm_ref, wout_ref, convin_ref, sin_ref,
                     y_ref, kvtail_ref, xbtail_ref, sout_ref,
                     s_scr, xc_scr, kvprev_scr, mix_scr, *, ct, n_lead, pos_shift):
    i = pl.program_id(1)
    nsteps = pl.num_programs(1)
    ntail = min(WINDOW, ct)

    @pl.when(i == 0)
    def _():
        s_scr[...] = sin_ref[0]
        xc_scr[0:8, :] = convin_ref[0]
        kvprev_scr[1] = hist_ref[...]

    x = x_ref[0]
    hb = _rms(x, g_ref[...]).astype(BF16)
    qa = _dot(hb, w_ref[:, O_QA:O_KV])
    kv = _dot(hb, w_ref[:, O_KV:O_ZA])
    za = _dot(hb, w_ref[:, O_ZA:O_XB])
    xb = _dot(hb, w_ref[:, O_XB:O_ZB])
    zb = _dot(hb, w_ref[:, O_ZB:AB_MAIN])
    gt = _dot(hb, wg_ref[...])
    gtt = _dotg(wgt_ref[...], hb, NT_DIMS)
    kvtail_ref[0] = kv[ct - ntail:]
    xbtail_ref[0] = xb[ct - 8:]

    slot = lax.rem(i, 2)
    kvb = jnp.concatenate([kvprev_scr[1 - slot], kv], axis=0).astype(BF16)
    if ct >= WINDOW:
        kvprev_scr[slot] = kv[ct - WINDOW:]
    _swa_tile(i, qa, kvb, za, sink_ref, mix_scr, ct=ct, sb=ntail, pos_shift=pos_shift)

    xc_scr[8:8 + ct, :] = xb
    w = convw_ref[...]
    yc = xc_scr[pl.ds(8 - (CONV_W - 1), ct), :] * w[0:1, :]
    for t in range(1, CONV_W):
        yc = yc + xc_scr[pl.ds(8 - (CONV_W - 1) + t, ct), :] * w[t:t + 1, :]
    cv = _silu(yc)
    xc_scr[0:8, :] = xc_scr[ct:ct + 8, :]
    _gdn_tile(i, cv, gt, gtt, zb, gprm_ref, gprmt_ref, onorm_ref, s_scr, mix_scr, ct=ct, n_lead=n_lead)

    y_ref[0] = x + _dot(mix_scr[...], wout_ref[...])

    @pl.when(i == nsteps - 1)
    def _():
        sout_ref[0] = s_scr[...]


def _swa_stages(i, qa, kvb, env, sink_ref, mix_scr, *, ct, sb, pos_shift):
    r = lax.broadcasted_iota(jnp.int32, (sb, WINDOW + sb), 0)
    c = lax.broadcasted_iota(jnp.int32, (sb, WINDOW + sb), 1)
    diff = r - c + WINDOW
    in_window = (diff >= 0) & (diff < WINDOW)
    nsb = ct // sb
    items = [(s, j) for s in range(nsb) for j in range(A_KV_HEADS)]
    masks = [jnp.concatenate([in_window & (i * ct + s * sb + c - WINDOW + pos_shift >= 0)] * A_GROUP, axis=0)
             for s in range(nsb)]
    sinks = [jnp.concatenate([jnp.broadcast_to(sink_ref[0:1, j * A_GROUP + g:j * A_GROUP + g + 1], (sb, 1))
                              for g in range(A_GROUP)], axis=0) for j in range(A_KV_HEADS)]
    sc = []
    for s, j in items:
        qs = jnp.concatenate([qa[s * sb:(s + 1) * sb, (j * A_GROUP + g) * A_HD:(j * A_GROUP + g + 1) * A_HD]
                              for g in range(A_GROUP)], axis=0).astype(BF16)
        sc.append(_dotg(qs, kvb[s * sb:s * sb + WINDOW + sb, j * A_HD:(j + 1) * A_HD], NT_DIMS) * (A_HD ** -0.5))
        yield
    p, esk = [], []
    for n, (s, j) in enumerate(items):
        scm = jnp.where(masks[s], sc[n], NEG)
        mx = jnp.maximum(jnp.max(scm, axis=-1, keepdims=True), sinks[j])
        p.append(jnp.exp(scm - mx).astype(BF16))
        esk.append(jnp.exp(sinks[j] - mx))
        yield
    pv = []
    ones = jnp.ones((WINDOW + sb, A_HD), BF16)
    for n, (s, j) in enumerate(items):
        vx = jnp.concatenate([kvb[s * sb:s * sb + WINDOW + sb,
                                  A_KV_WIDTH + j * A_HD:A_KV_WIDTH + (j + 1) * A_HD], ones], axis=-1)
        pvx = _dot(p[n], vx)
        pv.append(pvx[:, :A_HD] / (pvx[:, A_HD:] + esk[n]))
        yield
    for s in range(nsb):
        o = jnp.concatenate([pv[s * A_KV_HEADS + j][g * sb:(g + 1) * sb]
                             for j in range(A_KV_HEADS) for g in range(A_GROUP)], axis=-1)
        mix_scr[s * sb:(s + 1) * sb, 0:A_WIDTH] = (o * _silu(env['za'][s * sb:(s + 1) * sb])).astype(BF16)
        yield


def _gdn_stages(i, cvb, gt, gtt, env, gprm_ref, gprmt_ref, onorm_ref, s_scr, mix_scr, *, ct, n_lead):
    nc = ct // CHUNK
    items = [(c, h) for c in range(nc) for h in range(B_HEADS)]
    n_items = len(items)
    rs = lambda c: slice(c * CHUNK, (c + 1) * CHUNK)
    qn, kn = [], []
    for h in range(B_HEADS):
        q = cvb[h]
        k = cvb[B_HEADS + h]
        qn.append(q * lax.rsqrt(jnp.sum(q * q, axis=-1, keepdims=True) + EPS) * (B_DK ** -0.5))
        kn.append(k * lax.rsqrt(jnp.sum(k * k, axis=-1, keepdims=True) + EPS))
        yield
    beta = _sigmoid(gt)
    rows = i * ct + lax.broadcasted_iota(jnp.int32, gt.shape, 0)
    gcol = jnp.where(rows >= n_lead, -gprm_ref[0:1, :] * _softplus(gt + gprm_ref[1:2, :]), 0.0)
    lanes = i * ct + lax.broadcasted_iota(jnp.int32, gtt.shape, 1)
    grow = jnp.where(lanes >= n_lead, -gprmt_ref[:, 0:1] * _softplus(gtt + gprmt_ref[:, 1:2]), 0.0)
    ii = lax.broadcasted_iota(jnp.int32, (CHUNK, CHUNK), 0)
    jj = lax.broadcasted_iota(jnp.int32, (CHUNK, CHUNK), 1)
    lower = ii >= jj
    strict = ii > jj
    ltri = lower.astype(BF16)
    utri = (ii <= jj).astype(BF16)
    gcum_c = [sum(_dot(ltri, p) for p in _split3(gcol[rs(c)])) for c in range(nc)]
    gcum_r = [sum(_dot(p, utri) for p in _split3(grow[:, rs(c)])) for c in range(nc)]
    yield
    q_ = [qn[h][rs(c)] for c, h in items]
    k_ = [kn[h][rs(c)] for c, h in items]
    v_ = [cvb[2 * B_HEADS + h][rs(c)] for c, h in items]
    bh = [beta[rs(c), h:h + 1] for c, h in items]
    gc = [gcum_c[c][:, B_HEADS + h:B_HEADS + h + 1] for c, h in items]
    gr = [gcum_r[c][B_HEADS + h:B_HEADS + h + 1, :] for c, h in items]
    kb, a, qk = [], [], []
    for n in range(n_items):
        decay = jnp.exp(jnp.where(lower, gc[n] - gr[n], NEG))
        kb.append(k_[n] * bh[n])
        kq = _dotg(jnp.concatenate([kb[n], q_[n]], axis=0).astype(BF16), k_[n].astype(BF16), NT_DIMS)
        a.append(jnp.where(strict, kq[:CHUNK] * decay, 0.0))
        qk.append(jnp.where(lower, kq[CHUNK:] * decay, 0.0).astype(BF16))
        if n % B_HEADS == B_HEADS - 1:
            yield
    doff = [-jnp.where((ii >> 1) == (jj >> 1), a[n], 0.0) for n in range(n_items)]
    for lg in range(1, 6):
        m = ((ii >> (lg + 1)) == (jj >> (lg + 1))) & ((ii >> lg) != (jj >> lg))
        bm = [jnp.where(m, a[n], 0.0) for n in range(n_items)]
        db = [doff[n].astype(BF16) for n in range(n_items)]
        y = [bm[n] + _dot(db[n], bm[n].astype(BF16)) for n in range(n_items)]
        yield
        x = [y[n] + _dot(y[n].astype(BF16), db[n]) for n in range(n_items)]
        doff = [doff[n] - x[n] for n in range(n_items)]
        yield
    egc = [jnp.exp(gc[n]) for n in range(n_items)]
    g_last = [gc[n][CHUNK - 1:CHUNK, :] for n in range(n_items)]
    eg_last = [jnp.exp(g_last[n]) for n in range(n_items)]
    sol, kd, wq = [], [], []
    for n in range(n_items):
        rhs = jnp.concatenate([v_[n] * bh[n], kb[n] * egc[n]], axis=-1)
        sol.append(rhs + _dot(doff[n].astype(BF16), rhs.astype(BF16)))
        kd.append((k_[n] * jnp.exp(g_last[n] - gc[n])).astype(BF16))
        wq.append(jnp.concatenate([sol[n][:, B_DV:], q_[n] * egc[n]], axis=0).astype(BF16))
        if n % B_HEADS == B_HEADS - 1:
            yield

    def gate_store(c, o):
        for h in range(B_HEADS):
            z = env['zb'][rs(c), h * B_DV:(h + 1) * B_DV]
            mix_scr[rs(c), A_WIDTH + h * B_DV:A_WIDTH + (h + 1) * B_DV] = (
                _rms(o[h], onorm_ref[...]) * _silu(z)).astype(BF16)

    s_cur = [s_scr[h] for h in range(B_HEADS)]
    o_prev = None
    for c in range(nc):
        idx = [c * B_HEADS + h for h in range(B_HEADS)]
        ws = [_dot(wq[n], s_cur[h].astype(BF16)) for h, n in enumerate(idx)]
        if o_prev is not None:
            gate_store(c - 1, o_prev)
        yield
        v_new = [sol[n][:, :B_DV] - ws[h][:CHUNK] for h, n in enumerate(idx)]
        vb = [v_new[h].astype(BF16) for h in range(B_HEADS)]
        o_prev = [ws[h][CHUNK:] + _dot(qk[n], vb[h]) for h, n in enumerate(idx)]
        s_cur = [s_cur[h] * eg_last[n] + _dotg(kd[n], vb[h], TN_DIMS) for h, n in enumerate(idx)]
        yield
    for h in range(B_HEADS):
        s_scr[h] = s_cur[h]
    gate_store(nc - 1, o_prev)
    yield


def _ab_layer_staged_kernel(x_ref, g_ref, w_ref, wg_ref, wgt_ref, sink_ref, hist_ref, convw_ref, gprm_ref,
                            gprmt_ref, onorm_ref, wout_ref, convin_ref, sin_ref,
                            y_ref, kvtail_ref, xbtail_ref, sout_ref,
                            s_scr, xc_scr, kvprev_scr, mix_scr, *, ct, n_lead, pos_shift):
    i = pl.program_id(1)
    nsteps = pl.num_programs(1)
    ntail = min(WINDOW, ct)

    @pl.when(i == 0)
    def _():
        s_scr[...] = sin_ref[0]
        xc_scr[0:8, :] = convin_ref[0]
        kvprev_scr[1] = hist_ref[...]

    x = x_ref[0]
    hb = _rms(x, g_ref[...]).astype(BF16)
    proj = lambda a, b: _dot(hb, w_ref[:, a:b])
    env = {}

    def step(gen, n=1):
        for _ in range(n):
            next(gen, None)

    nblk = B_CONV_CH // LANES
    w = convw_ref[...]
    cvb = []

    def conv_block(j):
        cs = slice(j * LANES, (j + 1) * LANES)
        yc = xc_scr[pl.ds(8 - (CONV_W - 1), ct), cs] * w[0:1, cs]
        for t in range(1, CONV_W):
            yc = yc + xc_scr[pl.ds(8 - (CONV_W - 1) + t, ct), cs] * w[t:t + 1, cs]
        cvb.append(_silu(yc))

    pw = 2 * LANES
    for j in range(nblk // 2):
        xbj = proj(O_XB + j * pw, O_XB + (j + 1) * pw)
        xc_scr[8:8 + ct, j * pw:(j + 1) * pw] = xbj
        xbtail_ref[0, :, j * pw:(j + 1) * pw] = xbj[ct - 8:]
        if j > 0:
            conv_block(2 * j - 2)
            conv_block(2 * j - 1)
    gt = _dot(hb, wg_ref[...])
    gtt = _dotg(wgt_ref[...], hb, NT_DIMS)
    conv_block(nblk - 2)
    conv_block(nblk - 1)
    xc_scr[0:8, :] = xc_scr[ct:ct + 8, :]

    gdn = _gdn_stages(i, cvb, gt, gtt, env, gprm_ref, gprmt_ref, onorm_ref, s_scr, mix_scr, ct=ct, n_lead=n_lead)
    qa_p = []
    for j in range(A_WIDTH // pw):
        qa_p.append(proj(O_QA + j * pw, O_QA + (j + 1) * pw))
        step(gdn, 2)
    qa = jnp.concatenate(qa_p, axis=-1)
    kv = proj(O_KV, O_ZA)
    kvtail_ref[0] = kv[ct - ntail:]
    slot = lax.rem(i, 2)
    kvb = jnp.concatenate([kvprev_scr[1 - slot], kv], axis=0).astype(BF16)
    if ct >= WINDOW:
        kvprev_scr[slot] = kv[ct - WINDOW:]
    swa = _swa_stages(i, qa, kvb, env, sink_ref, mix_scr, ct=ct, sb=ntail, pos_shift=pos_shift)
    n_sw = (ct // ntail) * A_KV_HEADS
    step(gdn)
    for _ in range(ct // CHUNK):
        step(gdn)
        step(swa)
    step(swa, max(n_sw - ct // CHUNK, 0))
    za_p, zb_p = [], []
    fill = ([lambda: step(swa)] * (2 * n_sw)
            + [lambda j=j: za_p.append(proj(O_ZA + j * pw, O_ZA + (j + 1) * pw)) for j in range(A_WIDTH // pw)]
            + [lambda j=j: zb_p.append(proj(O_ZB + j * pw, O_ZB + (j + 1) * pw)) for j in range(B_VAL // pw)])
    per = -(-len(fill) // 10)
    for lv in range(10):
        step(gdn)
        for f in fill[lv * per:(lv + 1) * per]:
            f()
    for f in fill[10 * per:]:
        f()
    env['za'] = jnp.concatenate(za_p, axis=-1)
    env['zb'] = jnp.concatenate(zb_p, axis=-1)
    step(gdn, ct // CHUNK)
    step(swa, ct // ntail)
    ya = x + _dot(mix_scr[:, 0:A_WIDTH], wout_ref[0:A_WIDTH, :])
    for _ in gdn:
        pass
    y_ref[0] = ya + _dot(mix_scr[:, A_WIDTH:], wout_ref[A_WIDTH:, :])

    @pl.when(i == nsteps - 1)
    def _():
        sout_ref[0] = s_scr[...]


def _ab_layer_prompt(x3, norm_g, w_main, w_gate, w_gate_t, sink, hist_kv, conv_w8, gprm, gprm_t, onorm,
                     w_out, conv_in, s_in, ct, n_lead, pos_shift):
    B, L, _ = x3.shape
    ntail = min(WINDOW, ct)
    kern = functools.partial(_ab_layer_staged_kernel, ct=ct, n_lead=n_lead, pos_shift=pos_shift)
    full = lambda a: pl.BlockSpec(a.shape, lambda b, i: (0,) * a.ndim)
    blk = pl.BlockSpec((1, ct, D_MODEL), lambda b, i: (b, i, 0))
    per_b = lambda *s: pl.BlockSpec((1,) + s, lambda b, i: (b,) + (0,) * len(s))
    return pl.pallas_call(
        kern,
        grid=(B, L // ct),
        in_specs=[blk, full(norm_g), full(w_main), full(w_gate), full(w_gate_t), full(sink), full(hist_kv),
                  full(conv_w8), full(gprm), full(gprm_t), full(onorm), full(w_out),
                  per_b(8, B_CONV_CH), per_b(B_HEADS, B_DK, B_DV)],
        out_specs=[blk, per_b(ntail, 2 * A_KV_WIDTH), per_b(8, B_CONV_CH), per_b(B_HEADS, B_DK, B_DV)],
        out_shape=[jax.ShapeDtypeStruct((B, L, D_MODEL), F32),
                   jax.ShapeDtypeStruct((B, ntail, 2 * A_KV_WIDTH), F32),
                   jax.ShapeDtypeStruct((B, 8, B_CONV_CH), F32),
                   jax.ShapeDtypeStruct((B, B_HEADS, B_DK, B_DV), F32)],
        scratch_shapes=[pltpu.VMEM((B_HEADS, B_DK, B_DV), F32),
                        pltpu.VMEM((ct + 8, B_CONV_CH), F32),
                        pltpu.VMEM((2, WINDOW, 2 * A_KV_WIDTH), F32),
                        pltpu.VMEM((ct, AB_MIX), BF16)],
        compiler_params=_cparams(("arbitrary", "arbitrary")),
        name="ab_layer_prompt",
    )(x3, norm_g, w_main, w_gate, w_gate_t, sink, hist_kv, conv_w8, gprm, gprm_t, onorm, w_out, conv_in, s_in)


def _c_layer_kernel(x_ref, g_ref, w_ref, wgk_ref, wup_ref, bgk_ref, onorm_ref, wout_ref, gfin_ref,
                    sin_ref, y_ref, sout_ref, st_scr, og_scr, *, ct, n_lead):
    i = pl.program_id(1)
    nsteps = pl.num_programs(1)

    @pl.when(i == 0)
    def _():
        st_scr[...] = sin_ref[0]

    x = x_ref[0]
    hb = _rms(x, g_ref[...]).astype(BF16)
    q_all = _dot(hb, w_ref[:, 0:C_KEY])
    k_all = _dot(hb, w_ref[:, C_KEY:2 * C_KEY])
    v_all = _dot(hb, w_ref[:, 2 * C_KEY:2 * C_KEY + C_VAL]).astype(BF16)
    z_all = _dot(hb, w_ref[:, 2 * C_KEY + C_VAL:])
    la_all = _gla_log_decay(hb, wgk_ref, wup_ref, bgk_ref)
    rows = i * ct + lax.broadcasted_iota(jnp.int32, la_all.shape, 0)
    la_all = jnp.where(rows >= n_lead, la_all, 0.0)

    ii = lax.broadcasted_iota(jnp.int32, (CHUNK, CHUNK), 0)
    jj = lax.broadcasted_iota(jnp.int32, (CHUNK, CHUNK), 1)
    lower = ii >= jj
    ltri = lower.astype(BF16)
    nc = ct // CHUNK
    rs = lambda c: slice(c * CHUNK, (c + 1) * CHUNK)
    ks = lambda h: slice(h * C_DK, (h + 1) * C_DK)
    vs = lambda h: slice(h * C_DV, (h + 1) * C_DV)
    items = [(c, h) for c in range(nc) for h in range(C_HEADS)]
    n_items = len(items)
    bc_all = [sum(_dot(ltri, p) for p in _split3(la_all[rs(c)])) for c in range(nc)]
    bc = [bc_all[c][:, ks(h)] for c, h in items]
    k_ = [k_all[rs(c), ks(h)] for c, h in items]
    v_ = [v_all[rs(c), vs(h)] for c, h in items]
    qd = [(q_all[rs(c), ks(h)] * (C_DK ** -0.5) * jnp.exp(bc[n])).astype(BF16) for n, (c, h) in enumerate(items)]
    kinv = [(k_[n] * jnp.exp(-bc[n])).astype(BF16) for n in range(n_items)]
    bl = [bc[n][CHUNK - 1:CHUNK, :] for n in range(n_items)]
    kd = [(k_[n] * jnp.exp(bl[n] - bc[n])).astype(BF16) for n in range(n_items)]
    gl = [jnp.exp(bl[n]) for n in range(n_items)]
    qk = [jnp.where(lower, _dotg(qd[n], kinv[n], NT_DIMS), 0.0).astype(BF16) for n in range(n_items)]
    o_intra = [_dot(qk[n], v_[n]) for n in range(n_items)]

    st = [st_scr[h] for h in range(C_HEADS)]
    for c in range(nc):
        idx = [c * C_HEADS + h for h in range(C_HEADS)]
        o = [o_intra[n] + _dotg(qd[n], st[h].astype(BF16), NT_DIMS) for h, n in enumerate(idx)]
        st = [st[h] * gl[n] + _dotg(v_[n], kd[n], TN_DIMS) for h, n in enumerate(idx)]
        for h in range(C_HEADS):
            og_scr[rs(c), vs(h)] = (_rms(o[h], onorm_ref[...]) * _silu(z_all[rs(c), vs(h)])).astype(BF16)
    for h in range(C_HEADS):
        st_scr[h] = st[h]

    y_ref[0] = _rms(x + _dot(og_scr[...], wout_ref[...]), gfin_ref[...])

    @pl.when(i == nsteps - 1)
    def _():
        sout_ref[0] = st_scr[...]


def _c_layer_prompt(x3, norm_g, w_main, w_gk, w_up, b_gk, onorm, w_out, final_g, st_in, ct, n_lead):
    B, L, _ = x3.shape
    kern = functools.partial(_c_layer_kernel, ct=ct, n_lead=n_lead)
    full = lambda a: pl.BlockSpec(a.shape, lambda b, i: (0,) * a.ndim)
    blk = pl.BlockSpec((1, ct, D_MODEL), lambda b, i: (b, i, 0))
    st = pl.BlockSpec((1, C_HEADS, C_DV, C_DK), lambda b, i: (b, 0, 0, 0))
    return pl.pallas_call(
        kern,
        grid=(B, L // ct),
        in_specs=[blk, full(norm_g), full(w_main), full(w_gk), full(w_up), full(b_gk), full(onorm),
                  full(w_out), full(final_g), st],
        out_specs=[blk, st],
        out_shape=[jax.ShapeDtypeStruct((B, L, D_MODEL), F32),
                   jax.ShapeDtypeStruct((B, C_HEADS, C_DV, C_DK), F32)],
        scratch_shapes=[pltpu.VMEM((C_HEADS, C_DV, C_DK), F32), pltpu.VMEM((ct, C_VAL), BF16)],
        compiler_params=_cparams(("arbitrary", "arbitrary")),
        name="c_layer_prompt",
    )(x3, norm_g, w_main, w_gk, w_up, b_gk, onorm, w_out, final_g, st_in)


def _ab_proj_kernel(x_ref, g_ref, w_ref, wg_ref, qa_ref, kv_ref, za_ref, xb_ref, zb_ref, gates_ref):
    hb = _rms(x_ref[...], g_ref[...]).astype(BF16)
    qa_ref[...] = _dot(hb, w_ref[:, O_QA:O_KV])
    kv_ref[...] = _dot(hb, w_ref[:, O_KV:O_ZA])
    za_ref[...] = _dot(hb, w_ref[:, O_ZA:O_XB])
    xb_ref[...] = _dot(hb, w_ref[:, O_XB:O_ZB])
    zb_ref[...] = _dot(hb, w_ref[:, O_ZB:AB_MAIN])
    gates_ref[...] = _dot(hb, wg_ref[...])


def _ab_proj(x2d, norm_g, w_main, w_gate):
    n = x2d.shape[0]
    full = lambda a: pl.BlockSpec(a.shape, lambda: (0,) * a.ndim)
    widths = (A_WIDTH, 2 * A_KV_WIDTH, A_WIDTH, B_CONV_CH, B_VAL, LANES)
    args = (x2d, norm_g, w_main, w_gate)
    return pl.pallas_call(
        _ab_proj_kernel,
        in_specs=[full(a) for a in args],
        out_specs=[pl.BlockSpec((n, w), lambda: (0, 0)) for w in widths],
        out_shape=[jax.ShapeDtypeStruct((n, w), F32) for w in widths],
        compiler_params=pltpu.CompilerParams(vmem_limit_bytes=VMEM_LIMIT),
        name="ab_in_proj_sample",
    )(*args)


def _swa_sample_kernel(q_ref, kvn_ref, za_ref, ck_ref, cv_ref, sink_ref, o_ref, nk_ref, nv_ref, *, bt):
    row = lax.broadcasted_iota(jnp.int32, (WINDOW, A_KV_WIDTH), 0)
    hrow = lax.broadcasted_iota(jnp.int32, (A_HEADS, A_HD), 0)
    sk = sink_ref[...]
    nkb, nvb = [], []
    for b in range(bt):
        nk = jnp.where(row == WINDOW - 1, kvn_ref[b:b + 1, 0:A_KV_WIDTH], pltpu.roll(ck_ref[b], WINDOW - 1, 0))
        nv = jnp.where(row == WINDOW - 1, kvn_ref[b:b + 1, A_KV_WIDTH:], pltpu.roll(cv_ref[b], WINDOW - 1, 0))
        nk_ref[b] = nk
        nv_ref[b] = nv
        nkb.append(nk.astype(BF16))
        nvb.append(nv.astype(BF16))
    items = [(b, j) for b in range(bt) for j in range(A_KV_HEADS)]
    q8 = [q_ref[b].astype(BF16) for b in range(bt)]
    sc = [_dotg(q8[b], nkb[b][:, j * A_HD:(j + 1) * A_HD], NT_DIMS) * (A_HD ** -0.5) for b, j in items]
    m = [jnp.maximum(jnp.max(s, axis=-1, keepdims=True), sk) for s in sc]
    p = [jnp.exp(sc[n] - m[n]) for n in range(len(items))]
    den = [jnp.sum(p[n], axis=-1, keepdims=True) + jnp.exp(sk - m[n]) for n in range(len(items))]
    oj = [_dot(p[n].astype(BF16), nvb[b][:, j * A_HD:(j + 1) * A_HD]) / den[n] for n, (b, j) in enumerate(items)]
    for b in range(bt):
        o8 = jnp.where(hrow >= A_GROUP, oj[b * A_KV_HEADS + 1], oj[b * A_KV_HEADS])
        o_ref[b] = o8 * _silu(za_ref[b])


def _swa_sample(q3, kv_new, za3, cache_k, cache_v, sink_col, bt):
    nb = q3.shape[0]
    hd = pl.BlockSpec((bt, A_HEADS, A_HD), lambda i: (i, 0, 0))
    cache = pl.BlockSpec((bt, WINDOW, A_KV_WIDTH), lambda i: (i, 0, 0))
    return pl.pallas_call(
        functools.partial(_swa_sample_kernel, bt=bt),
        grid=(nb // bt,),
        in_specs=[hd, pl.BlockSpec((bt, 2 * A_KV_WIDTH), lambda i: (i, 0)), hd, cache, cache,
                  pl.BlockSpec(sink_col.shape, lambda i: (0, 0))],
        out_specs=[hd, cache, cache],
        out_shape=[jax.ShapeDtypeStruct(q3.shape, F32),
                   jax.ShapeDtypeStruct(cache_k.shape, F32),
                   jax.ShapeDtypeStruct(cache_v.shape, F32)],
        compiler_params=_cparams(("arbitrary",)),
        name="swa_sample",
    )(q3, kv_new, za3, cache_k, cache_v, sink_col)


def _gdn_sample_prep_kernel(xb_ref, h0_ref, h1_ref, h2_ref, gates_ref, convw_ref, gprm_ref,
                            q_ref, k_ref, v_ref, bg_ref):
    w = convw_ref[...]
    y = (h0_ref[...] * w[0:1, :] + h1_ref[...] * w[1:2, :] + h2_ref[...] * w[2:3, :]
         + xb_ref[...] * w[3:4, :])
    cv = _silu(y)
    for h in range(B_HEADS):
        q = cv[:, h * B_DK:(h + 1) * B_DK]
        k = cv[:, B_KEY + h * B_DK:B_KEY + (h + 1) * B_DK]
        q_ref[:, h * B_DK:(h + 1) * B_DK] = (
            q * lax.rsqrt(jnp.sum(q * q, axis=-1, keepdims=True) + EPS) * (B_DK ** -0.5))
        k_ref[:, h * B_DK:(h + 1) * B_DK] = k * lax.rsqrt(jnp.sum(k * k, axis=-1, keepdims=True) + EPS)
    v_ref[...] = cv[:, 2 * B_KEY:]
    gt = gates_ref[...]
    col = lax.broadcasted_iota(jnp.int32, gt.shape, 1)
    g = -gprm_ref[0:1, :] * _softplus(gt + gprm_ref[1:2, :])
    bg_ref[...] = jnp.where(col < B_HEADS, _sigmoid(gt), jnp.exp(g))


def _gdn_sample_prep(xb, h0, h1, h2, gates, conv_w8, gprm):
    n = xb.shape[0]
    full = lambda a: pl.BlockSpec(a.shape, lambda: (0,) * a.ndim)
    args = (xb, h0, h1, h2, gates, conv_w8, gprm)
    return pl.pallas_call(
        _gdn_sample_prep_kernel,
        in_specs=[full(a) for a in args],
        out_specs=[pl.BlockSpec((n, B_KEY), lambda: (0, 0))] * 3 + [pl.BlockSpec((n, LANES), lambda: (0, 0))],
        out_shape=[jax.ShapeDtypeStruct((n, B_KEY), F32)] * 3 + [jax.ShapeDtypeStruct((n, LANES), F32)],
        compiler_params=pltpu.CompilerParams(vmem_limit_bytes=VMEM_LIMIT),
        name="gdn_sample_prep",
    )(*args)


def _gdn_sample_kernel(s_ref, qt_ref, kt_ref, v_ref, bg_ref, zb_ref, onorm_ref, o_ref, sout_ref, *, bt):
    items = [(b, h) for b in range(bt) for h in range(B_HEADS)]
    hs = lambda h: slice(h * B_DV, (h + 1) * B_DV)
    kc = [kt_ref[0, h, :, b:b + 1] for b, h in items]
    eg = [bg_ref[b:b + 1, B_HEADS + h:B_HEADS + h + 1] for b, h in items]
    ks = [jnp.sum(kc[n] * s_ref[b, h], axis=0, keepdims=True) for n, (b, h) in enumerate(items)]
    v_new = [bg_ref[b:b + 1, h:h + 1] * (v_ref[b:b + 1, hs(h)] - eg[n] * ks[n]) for n, (b, h) in enumerate(items)]
    o = []
    for n, (b, h) in enumerate(items):
        s_new = eg[n] * s_ref[b, h] + kc[n] * v_new[n]
        sout_ref[b, h] = s_new
        o.append(jnp.sum(qt_ref[0, h, :, b:b + 1] * s_new, axis=0, keepdims=True))
    o_blk = jnp.concatenate([jnp.concatenate([o[b * B_HEADS + h] for h in range(B_HEADS)], axis=-1)
                             for b in range(bt)], axis=0)
    for h in range(B_HEADS):
        o_ref[:, hs(h)] = _rms(o_blk[:, hs(h)], onorm_ref[...]) * _silu(zb_ref[:, hs(h)])


def _gdn_sample(state, qt, kt, v, bg, zb, onorm, bt):
    nb = state.shape[0]
    st = pl.BlockSpec((bt, B_HEADS, B_DK, B_DV), lambda i: (i, 0, 0, 0))
    col = pl.BlockSpec((1, B_HEADS, B_DK, bt), lambda i: (i, 0, 0, 0))
    row = lambda n: pl.BlockSpec((bt, n), lambda i: (i, 0))
    return pl.pallas_call(
        functools.partial(_gdn_sample_kernel, bt=bt),
        grid=(nb // bt,),
        in_specs=[st, col, col, row(B_VAL), row(LANES), row(B_VAL),
                  pl.BlockSpec(onorm.shape, lambda i: (0, 0))],
        out_specs=[row(B_VAL), st],
        out_shape=[jax.ShapeDtypeStruct((nb, B_VAL), F32), jax.ShapeDtypeStruct(state.shape, F32)],
        compiler_params=_cparams(("arbitrary",)),
        name="gdn_sample",
    )(state, qt, kt, v, bg, zb, onorm)


def _ab_out_c_proj_kernel(x_ref, oa_ref, ob_ref, wout_ref, g_ref, w_ref, wgk_ref, wup_ref, bgk_ref,
                          y_ref, q_ref, k_ref, v_ref, z_ref, la_ref):
    y = (x_ref[...] + _dot(oa_ref[...].astype(BF16), wout_ref[0:A_WIDTH, :])
         + _dot(ob_ref[...].astype(BF16), wout_ref[A_WIDTH:, :]))
    y_ref[...] = y
    hb = _rms(y, g_ref[...]).astype(BF16)
    q_ref[...] = _dot(hb, w_ref[:, 0:C_KEY])
    k_ref[...] = _dot(hb, w_ref[:, C_KEY:2 * C_KEY])
    v_ref[...] = _dot(hb, w_ref[:, 2 * C_KEY:2 * C_KEY + C_VAL])
    z_ref[...] = _dot(hb, w_ref[:, 2 * C_KEY + C_VAL:])
    la_ref[...] = _gla_log_decay(hb, wgk_ref, wup_ref, bgk_ref)


def _ab_out_c_proj(x2d, oa, ob, w_out, norm_g, w_main, w_gk, w_up, b_gk):
    n = x2d.shape[0]
    full = lambda a: pl.BlockSpec(a.shape, lambda: (0,) * a.ndim)
    widths = (D_MODEL, C_KEY, C_KEY, C_VAL, C_VAL, C_KEY)
    args = (x2d, oa, ob, w_out, norm_g, w_main, w_gk, w_up, b_gk)
    return pl.pallas_call(
        _ab_out_c_proj_kernel,
        in_specs=[full(a) for a in args],
        out_specs=[pl.BlockSpec((n, w), lambda: (0, 0)) for w in widths],
        out_shape=[jax.ShapeDtypeStruct((n, w), F32) for w in widths],
        compiler_params=pltpu.CompilerParams(vmem_limit_bytes=VMEM_LIMIT),
        name="ab_out_c_in_proj_sample",
    )(*args)


def _gla_sample_kernel(s_ref, qt_ref, kt_ref, lat_ref, v_ref, z_ref, onorm_ref, o_ref, sout_ref, *, bt):
    items = [(b, h) for b in range(bt) for h in range(C_HEADS)]
    hs = lambda h: slice(h * C_DV, (h + 1) * C_DV)
    ac = [jnp.exp(lat_ref[0, h, :, b:b + 1]) for b, h in items]
    o = []
    for n, (b, h) in enumerate(items):
        s_new = ac[n] * s_ref[b, h] + kt_ref[0, h, :, b:b + 1] * v_ref[b:b + 1, hs(h)]
        sout_ref[b, h] = s_new
        o.append(jnp.sum((qt_ref[0, h, :, b:b + 1] * (C_DK ** -0.5)) * s_new, axis=0, keepdims=True))
    o_blk = jnp.concatenate([jnp.concatenate([o[b * C_HEADS + h] for h in range(C_HEADS)], axis=-1)
                             for b in range(bt)], axis=0)
    for h in range(C_HEADS):
        o_ref[:, hs(h)] = _rms(o_blk[:, hs(h)], onorm_ref[...]) * _silu(z_ref[:, hs(h)])


def _gla_sample(state, qt, kt, lat, v, z, onorm, bt):
    nb = state.shape[0]
    st = pl.BlockSpec((bt, C_HEADS, C_DK, C_DV), lambda i: (i, 0, 0, 0))
    col = pl.BlockSpec((1, C_HEADS, C_DK, bt), lambda i: (i, 0, 0, 0))
    row = lambda n: pl.BlockSpec((bt, n), lambda i: (i, 0))
    return pl.pallas_call(
        functools.partial(_gla_sample_kernel, bt=bt),
        grid=(nb // bt,),
        in_specs=[st, col, col, col, row(C_VAL), row(C_VAL),
                  pl.BlockSpec(onorm.shape, lambda i: (0, 0))],
        out_specs=[row(C_VAL), st],
        out_shape=[jax.ShapeDtypeStruct((nb, C_VAL), F32), jax.ShapeDtypeStruct(state.shape, F32)],
        compiler_params=_cparams(("arbitrary",)),
        name="gla_sample",
    )(state, qt, kt, lat, v, z, onorm)


def _c_out_kernel(x_ref, o_ref, w_ref, g_ref, y_ref):
    y_ref[...] = _rms(x_ref[...] + _dot(o_ref[...].astype(BF16), w_ref[...]), g_ref[...])


def _c_out(x2d, oc, w_out, final_g):
    n = x2d.shape[0]
    full = lambda a: pl.BlockSpec(a.shape, lambda: (0,) * a.ndim)
    args = (x2d, oc, w_out, final_g)
    return pl.pallas_call(
        _c_out_kernel,
        in_specs=[full(a) for a in args],
        out_specs=pl.BlockSpec((n, D_MODEL), lambda: (0, 0)),
        out_shape=jax.ShapeDtypeStruct((n, D_MODEL), F32),
        compiler_params=pltpu.CompilerParams(vmem_limit_bytes=VMEM_LIMIT),
        name="c_out_proj_norm_sample",
    )(*args)


def _to_cols(a, heads, dk, bt):
    n = a.shape[0]
    return a.reshape(n // bt, bt, heads, dk).transpose(0, 2, 3, 1)


def kernel(x_prompt, x_sample, cache_swa_k, cache_swa_v, state_dn_conv, state_dn, state_gla,
           meta_tokens, norm_ab, w_in_ab, sink_a, conv_b, a_log_b, dt_bias_b, onorm_b, w_out_ab,
           norm_c, w_in_c, w_gk_up, b_gk, onorm_c, w_out_c, final_norm):
    Bp, L, _ = x_prompt.shape
    Bs = x_sample.shape[0]
    CT, BT = 256, 8

    w_ab = w_in_ab[0]
    w_ab_main = w_ab[:, :AB_MAIN].astype(BF16)
    w_ab_gate = jnp.pad(w_ab[:, AB_MAIN:], ((0, 0), (0, LANES - 2 * B_HEADS))).astype(BF16)
    w_ab_gate_t = jnp.pad(w_ab[:, AB_MAIN:].T, ((0, 16 - 2 * B_HEADS), (0, 0))).astype(BF16)
    g_ab = norm_ab[0][None, :]
    sink_row = sink_a[0][None, :]
    sink_col = sink_a[0][:, None]
    conv_w8 = jnp.pad(conv_b[0], ((0, 8 - CONV_W), (0, 0)))
    ea = jnp.exp(a_log_b[0])
    gprm = jnp.zeros((8, LANES), F32).at[0, B_HEADS:2 * B_HEADS].set(ea).at[1, B_HEADS:2 * B_HEADS].set(dt_bias_b[0])
    gprm_t = jnp.zeros((16, LANES), F32).at[B_HEADS:2 * B_HEADS, 0].set(ea).at[B_HEADS:2 * B_HEADS, 1].set(dt_bias_b[0])
    onb = onorm_b[0][None, :]
    w_ab_out = w_out_ab[0].astype(BF16)
    w_c = w_in_c[0]
    w_c_main = w_c[:, :C_MAIN].astype(BF16)
    w_c_gk = jnp.pad(w_c[:, C_MAIN:], ((0, 0), (0, LANES - C_RANK))).astype(BF16)
    w_up = jnp.pad(w_gk_up[0], ((0, LANES - C_RANK), (0, 0)))
    bgk = b_gk[0][None, :]
    g_c = norm_c[0][None, :]
    onc = onorm_c[0][None, :]
    w_c_out = w_out_c[0].astype(BF16)
    g_fin = final_norm[None, :]

    def ab_layer(x3, ct, hist_kv, pos_shift, conv_in, s_in, n_lead):
        return _ab_layer_prompt(x3, g_ab, w_ab_main, w_ab_gate, w_ab_gate_t, sink_row, hist_kv, conv_w8,
                                gprm, gprm_t, onb, w_ab_out, conv_in, s_in, ct, n_lead, pos_shift)

    def c_layer(x3, ct, st_in, n_lead):
        return _c_layer_prompt(x3, g_c, w_c_main, w_c_gk, w_up, bgk, onc, w_c_out, g_fin, st_in, ct, n_lead)

    xpre = jnp.concatenate([jnp.zeros((LEAD, D_MODEL), F32), meta_tokens], axis=0)[None]
    y_pre, kv_pre, xbtail_pre, sdn_pre = ab_layer(
        xpre, CHUNK, jnp.zeros((WINDOW, 2 * A_KV_WIDTH), F32), -LEAD,
        jnp.zeros((1, 8, B_CONV_CH), F32), jnp.zeros((1, B_HEADS, B_DK, B_DV), F32), LEAD)
    _, sgla_pre = c_layer(y_pre, CHUNK, jnp.zeros((1, C_HEADS, C_DV, C_DK), F32), LEAD)

    hist_kv = jnp.concatenate([jnp.zeros((WINDOW - CHUNK, 2 * A_KV_WIDTH), F32), kv_pre[0]], axis=0)
    y1, kv_tail, xb_tail, sdn_p = ab_layer(
        x_prompt, CT, hist_kv, N_META, jnp.broadcast_to(xbtail_pre, (Bp, 8, B_CONV_CH)),
        jnp.broadcast_to(sdn_pre, (Bp, B_HEADS, B_DK, B_DV)), 0)
    y_prompt, sgla_t = c_layer(y1, CT, jnp.broadcast_to(sgla_pre, (Bp, C_HEADS, C_DV, C_DK)), 0)
    sgla_p = jnp.swapaxes(sgla_t, 2, 3)

    swa_k_p = kv_tail[:, :, :A_KV_WIDTH].reshape(1, Bp, WINDOW, A_KV_HEADS, A_HD)
    swa_v_p = kv_tail[:, :, A_KV_WIDTH:].reshape(1, Bp, WINDOW, A_KV_HEADS, A_HD)
    dn_conv_p = xb_tail[:, 8 - (CONV_W - 1):, :][None]

    xs = x_sample.reshape(Bs, D_MODEL)
    qa, kvn, za, xb, zb, gates = _ab_proj(xs, g_ab, w_ab_main, w_ab_gate)
    oa3, nk, nv = _swa_sample(qa.reshape(Bs, A_HEADS, A_HD), kvn, za.reshape(Bs, A_HEADS, A_HD),
                              cache_swa_k[0].reshape(Bs, WINDOW, A_KV_WIDTH),
                              cache_swa_v[0].reshape(Bs, WINDOW, A_KV_WIDTH), sink_col, BT)
    hist = state_dn_conv[0]
    qn, kn, vn, bg = _gdn_sample_prep(xb, hist[:, 0], hist[:, 1], hist[:, 2], gates, conv_w8, gprm)
    ob, sdn_s = _gdn_sample(state_dn[0], _to_cols(qn, B_HEADS, B_DK, BT), _to_cols(kn, B_HEADS, B_DK, BT),
                            vn, bg, zb, onb, BT)
    ys, qc, kc, vc, zc, la = _ab_out_c_proj(xs, oa3.reshape(Bs, A_WIDTH), ob, w_ab_out,
                                            g_c, w_c_main, w_c_gk, w_up, bgk)
    ocs, sgla_s = _gla_sample(state_gla[0], _to_cols(qc, C_HEADS, C_DK, BT), _to_cols(kc, C_HEADS, C_DK, BT),
                              _to_cols(la, C_HEADS, C_DK, BT), vc, zc, onc, BT)
    y_sample = _c_out(ys, ocs, w_c_out, g_fin).reshape(Bs, 1, D_MODEL)
    dn_conv_s = jnp.concatenate([hist[:, 1:], xb[:, None, :]], axis=1)[None]

    return (y_prompt, y_sample, swa_k_p, swa_v_p, dn_conv_p, sdn_p[None], sgla_p[None],
            nk.reshape(1, Bs, WINDOW, A_KV_HEADS, A_HD), nv.reshape(1, Bs, WINDOW, A_KV_HEADS, A_HD),
            dn_conv_s, sdn_s[None], sgla_s[None])
```

```python
import functools

import jax
import jax.numpy as jnp
from jax import lax
from jax.experimental import pallas as pl
from jax.experimental.pallas import tpu as pltpu

F32 = jnp.float32
BF16 = jnp.bfloat16
EPS = 1e-6

D_MODEL = 1024
N_META = 16
CHUNK = 64
LEAD = (-N_META) % CHUNK
A_HEADS, A_KV_HEADS, A_HD = 8, 2, 64
A_GROUP = A_HEADS // A_KV_HEADS
A_WIDTH = A_HEADS * A_HD
A_KV_WIDTH = A_KV_HEADS * A_HD
WINDOW = 128
B_HEADS, B_DK, B_DV = 4, 128, 128
B_KEY = B_HEADS * B_DK
B_VAL = B_HEADS * B_DV
CONV_W = 4
B_CONV_CH = 2 * B_KEY + B_VAL
C_HEADS, C_DK, C_DV = 4, 128, 256
C_KEY = C_HEADS * C_DK
C_VAL = C_HEADS * C_DV
C_RANK = 16
C_GATE_NORM = 16.0
AB_MAIN = 2 * A_WIDTH + 2 * A_KV_WIDTH + B_CONV_CH + B_VAL
AB_MIX = A_WIDTH + B_VAL
C_MAIN = 2 * C_KEY + 2 * C_VAL
O_QA, O_KV, O_ZA, O_XB, O_ZB = 0, 512, 768, 1280, 2816
LANES = 128
NEG = -1e30

VMEM_LIMIT = 56 * 1024 * 1024

NT_DIMS = (((1,), (1,)), ((), ()))
TN_DIMS = (((0,), (0,)), ((), ()))


def _cparams(sem):
    return pltpu.CompilerParams(dimension_semantics=sem, vmem_limit_bytes=VMEM_LIMIT)


def _dot(a, b):
    return jnp.dot(a, b, preferred_element_type=F32)


def _dotg(a, b, dims):
    return lax.dot_general(a, b, dims, preferred_element_type=F32)


def _split3(a):
    hi = a.astype(BF16)
    r = a - hi.astype(F32)
    mid = r.astype(BF16)
    lo = (r - mid.astype(F32)).astype(BF16)
    return hi, mid, lo


def _sigmoid(x):
    return 1.0 / (1.0 + jnp.exp(-x))


def _silu(x):
    return x * _sigmoid(x)


def _softplus(x):
    return jnp.maximum(x, 0.0) + jnp.log1p(jnp.exp(-jnp.abs(x)))


def _log_sigmoid(x):
    return jnp.minimum(x, 0.0) - jnp.log1p(jnp.exp(-jnp.abs(x)))


def _rms(x, g):
    return x * lax.rsqrt(jnp.mean(x * x, axis=-1, keepdims=True) + EPS) * g


def _gla_log_decay(hb, wgk_ref, wup_ref, bgk_ref):
    low = _dot(hb, wgk_ref[...])
    lh, lm, _ = _split3(low)
    wh, wm, _ = _split3(wup_ref[...])
    up = _dot(lh, wh) + _dot(lh, wm) + _dot(lm, wh)
    return _log_sigmoid(up + bgk_ref[...]) * (1.0 / C_GATE_NORM)


def _swa_tile(i, qa, kvb, za, sink_ref, mix_scr, *, ct, sb, pos_shift):
    r = lax.broadcasted_iota(jnp.int32, (sb, WINDOW + sb), 0)
    c = lax.broadcasted_iota(jnp.int32, (sb, WINDOW + sb), 1)
    diff = r - c + WINDOW
    in_window = (diff >= 0) & (diff < WINDOW)
    nsb = ct // sb
    items = [(s, j) for s in range(nsb) for j in range(A_KV_HEADS)]
    masks = [jnp.concatenate([in_window & (i * ct + s * sb + c - WINDOW + pos_shift >= 0)] * A_GROUP, axis=0)
             for s in range(nsb)]
    sinks = [jnp.concatenate([jnp.broadcast_to(sink_ref[0:1, j * A_GROUP + g:j * A_GROUP + g + 1], (sb, 1))
                              for g in range(A_GROUP)], axis=0) for j in range(A_KV_HEADS)]
    qs = [jnp.concatenate([qa[s * sb:(s + 1) * sb, (j * A_GROUP + g) * A_HD:(j * A_GROUP + g + 1) * A_HD]
                           for g in range(A_GROUP)], axis=0).astype(BF16) for s, j in items]
    sc = [_dotg(qs[n], kvb[s * sb:s * sb + WINDOW + sb, j * A_HD:(j + 1) * A_HD], NT_DIMS) * (A_HD ** -0.5)
          for n, (s, j) in enumerate(items)]
    sc = [jnp.where(masks[s], sc[n], NEG) for n, (s, j) in enumerate(items)]
    mx = [jnp.maximum(jnp.max(sc[n], axis=-1, keepdims=True), sinks[j]) for n, (s, j) in enumerate(items)]
    p = [jnp.exp(sc[n] - mx[n]) for n in range(len(items))]
    den = [jnp.sum(p[n], axis=-1, keepdims=True) + jnp.exp(sinks[j] - mx[n]) for n, (s, j) in enumerate(items)]
    pv = [_dot(p[n].astype(BF16),
               kvb[s * sb:s * sb + WINDOW + sb, A_KV_WIDTH + j * A_HD:A_KV_WIDTH + (j + 1) * A_HD]) / den[n]
          for n, (s, j) in enumerate(items)]
    for s in range(nsb):
        o = jnp.concatenate([pv[s * A_KV_HEADS + j][g * sb:(g + 1) * sb]
                             for j in range(A_KV_HEADS) for g in range(A_GROUP)], axis=-1)
        mix_scr[s * sb:(s + 1) * sb, 0:A_WIDTH] = (o * _silu(za[s * sb:(s + 1) * sb])).astype(BF16)


def _gdn_tile(i, cv, gt, gtt, zb, gprm_ref, gprmt_ref, onorm_ref, s_scr, mix_scr, *, ct, n_lead):
    beta = _sigmoid(gt)
    rows = i * ct + lax.broadcasted_iota(jnp.int32, gt.shape, 0)
    gcol = jnp.where(rows >= n_lead, -gprm_ref[0:1, :] * _softplus(gt + gprm_ref[1:2, :]), 0.0)
    lanes = i * ct + lax.broadcasted_iota(jnp.int32, gtt.shape, 1)
    grow = jnp.where(lanes >= n_lead, -gprmt_ref[:, 0:1] * _softplus(gtt + gprmt_ref[:, 1:2]), 0.0)

    ii = lax.broadcasted_iota(jnp.int32, (CHUNK, CHUNK), 0)
    jj = lax.broadcasted_iota(jnp.int32, (CHUNK, CHUNK), 1)
    lower = ii >= jj
    strict = ii > jj
    ltri = lower.astype(BF16)
    utri = (ii <= jj).astype(BF16)

    nc = ct // CHUNK
    items = [(c, h) for c in range(nc) for h in range(B_HEADS)]
    n_items = len(items)
    rs = lambda c: slice(c * CHUNK, (c + 1) * CHUNK)
    qn, kn = [], []
    for h in range(B_HEADS):
        q = cv[:, h * B_DK:(h + 1) * B_DK]
        k = cv[:, B_KEY + h * B_DK:B_KEY + (h + 1) * B_DK]
        qn.append(q * lax.rsqrt(jnp.sum(q * q, axis=-1, keepdims=True) + EPS) * (B_DK ** -0.5))
        kn.append(k * lax.rsqrt(jnp.sum(k * k, axis=-1, keepdims=True) + EPS))
    gcum_c = [sum(_dot(ltri, p) for p in _split3(gcol[rs(c)])) for c in range(nc)]
    gcum_r = [sum(_dot(p, utri) for p in _split3(grow[:, rs(c)])) for c in range(nc)]
    q_ = [qn[h][rs(c)] for c, h in items]
    k_ = [kn[h][rs(c)] for c, h in items]
    v_ = [cv[rs(c), 2 * B_KEY + h * B_DV:2 * B_KEY + (h + 1) * B_DV] for c, h in items]
    bh = [beta[rs(c), h:h + 1] for c, h in items]
    gc = [gcum_c[c][:, B_HEADS + h:B_HEADS + h + 1] for c, h in items]
    gr = [gcum_r[c][B_HEADS + h:B_HEADS + h + 1, :] for c, h in items]
    decay = [jnp.exp(jnp.where(lower, gc[n] - gr[n], NEG)) for n in range(n_items)]
    kb = [k_[n] * bh[n] for n in range(n_items)]
    kbf = [k_[n].astype(BF16) for n in range(n_items)]
    kq = [_dotg(jnp.concatenate([kb[n], q_[n]], axis=0).astype(BF16), kbf[n], NT_DIMS) for n in range(n_items)]
    a = [jnp.where(strict, kq[n][:CHUNK] * decay[n], 0.0) for n in range(n_items)]
    qk = [jnp.where(lower, kq[n][CHUNK:] * decay[n], 0.0).astype(BF16) for n in range(n_items)]
    doff = [-jnp.where((ii >> 1) == (jj >> 1), a[n], 0.0) for n in range(n_items)]
    for lg in range(1, 6):
        m = ((ii >> (lg + 1)) == (jj >> (lg + 1))) & ((ii >> lg) != (jj >> lg))
        bm = [jnp.where(m, a[n], 0.0) for n in range(n_items)]
        db = [doff[n].astype(BF16) for n in range(n_items)]
        y = [bm[n] + _dot(db[n], bm[n].astype(BF16)) for n in range(n_items)]
        x = [y[n] + _dot(y[n].astype(BF16), db[n]) for n in range(n_items)]
        doff = [doff[n] - x[n] for n in range(n_items)]
    egc = [jnp.exp(gc[n]) for n in range(n_items)]
    rhs = [jnp.concatenate([v_[n] * bh[n], kb[n] * egc[n]], axis=-1) for n in range(n_items)]
    sol = [rhs[n] + _dot(doff[n].astype(BF16), rhs[n].astype(BF16)) for n in range(n_items)]
    g_last = [gc[n][CHUNK - 1:CHUNK, :] for n in range(n_items)]
    kd = [(k_[n] * jnp.exp(g_last[n] - gc[n])).astype(BF16) for n in range(n_items)]
    wq = [jnp.concatenate([sol[n][:, B_DV:], q_[n] * egc[n]], axis=0).astype(BF16) for n in range(n_items)]
    eg_last = [jnp.exp(g_last[n]) for n in range(n_items)]

    s_cur = [s_scr[h] for h in range(B_HEADS)]
    for c in range(nc):
        idx = [c * B_HEADS + h for h in range(B_HEADS)]
        ws = [_dot(wq[n], s_cur[h].astype(BF16)) for h, n in enumerate(idx)]
        v_new = [sol[n][:, :B_DV] - ws[h][:CHUNK] for h, n in enumerate(idx)]
        vb = [v_new[h].astype(BF16) for h in range(B_HEADS)]
        o = [ws[h][CHUNK:] + _dot(qk[n], vb[h]) for h, n in enumerate(idx)]
        s_cur = [s_cur[h] * eg_last[n] + _dotg(kd[n], vb[h], TN_DIMS) for h, n in enumerate(idx)]
        for h in range(B_HEADS):
            z = zb[rs(c), h * B_DV:(h + 1) * B_DV]
            mix_scr[rs(c), A_WIDTH + h * B_DV:A_WIDTH + (h + 1) * B_DV] = (
                _rms(o[h], onorm_ref[...]) * _silu(z)).astype(BF16)
    for h in range(B_HEADS):
        s_scr[h] = s_cur[h]


def _ab_layer_kernel(x_ref, g_ref, w_ref, wg_ref, wgt_ref, sink_ref, hist_ref, convw_ref, gprm_ref,
                     gprmt_ref, onorm_ref, wout_ref, convin_ref, sin_ref,
                     y_ref, kvtail_ref, xbtail_ref, sout_ref,
                     s_scr, xc_scr, kvprev_scr, mix_scr, *, ct, n_lead, pos_shift):
    i = pl.program_id(1)
    nsteps = pl.num_programs(1)
    ntail = min(WINDOW, ct)

    @pl.when(i == 0)
    def _():
        s_scr[...] = sin_ref[0]
        xc_scr[0:8, :] = convin_ref[0]
        kvprev_scr[1] = hist_ref[...]

    x = x_ref[0]
    hb = _rms(x, g_ref[...]).astype(BF16)
    qa = _dot(hb, w_ref[:, O_QA:O_KV])
    kv = _dot(hb, w_ref[:, O_KV:O_ZA])
    za = _dot(hb, w_ref[:, O_ZA:O_XB])
    xb = _dot(hb, w_ref[:, O_XB:O_ZB])
    zb = _dot(hb, w_ref[:, O_ZB:AB_MAIN])
    gt = _dot(hb, wg_ref[...])
    gtt = _dotg(wgt_ref[...], hb, NT_DIMS)
    kvtail_ref[0] = kv[ct - ntail:]
    xbtail_ref[0] = xb[ct - 8:]

    slot = lax.rem(i, 2)
    kvb = jnp.concatenate([kvprev_scr[1 - slot], kv], axis=0).astype(BF16)
    if ct >= WINDOW:
        kvprev_scr[slot] = kv[ct - WINDOW:]
    _swa_tile(i, qa, kvb, za, sink_ref, mix_scr, ct=ct, sb=ntail, pos_shift=pos_shift)

    xc_scr[8:8 + ct, :] = xb
    w = convw_ref[...]
    yc = xc_scr[pl.ds(8 - (CONV_W - 1), ct), :] * w[0:1, :]
    for t in range(1, CONV_W):
        yc = yc + xc_scr[pl.ds(8 - (CONV_W - 1) + t, ct), :] * w[t:t + 1, :]
    cv = _silu(yc)
    xc_scr[0:8, :] = xc_scr[ct:ct + 8, :]
    _gdn_tile(i, cv, gt, gtt, zb, gprm_ref, gprmt_ref, onorm_ref, s_scr, mix_scr, ct=ct, n_lead=n_lead)

    y_ref[0] = x + _dot(mix_scr[...], wout_ref[...])

    @pl.when(i == nsteps - 1)
    def _():
        sout_ref[0] = s_scr[...]


def _swa_stages(i, qa, kvb, env, sink_ref, mix_scr, *, ct, sb, pos_shift):
    r = lax.broadcasted_iota(jnp.int32, (sb, WINDOW + sb), 0)
    c = lax.broadcasted_iota(jnp.int32, (sb, WINDOW + sb), 1)
    diff = r - c + WINDOW
    in_window = (diff >= 0) & (diff < WINDOW)
    nsb = ct // sb
    items = [(s, j) for s in range(nsb) for j in range(A_KV_HEADS)]
    masks = [jnp.concatenate([in_window & (i * ct + s * sb + c - WINDOW + pos_shift >= 0)] * A_GROUP, axis=0)
             for s in range(nsb)]
    sinks = [jnp.concatenate([jnp.broadcast_to(sink_ref[0:1, j * A_GROUP + g:j * A_GROUP + g + 1], (sb, 1))
                              for g in range(A_GROUP)], axis=0) for j in range(A_KV_HEADS)]
    sc = []
    for s, j in items:
        qs = jnp.concatenate([qa[s * sb:(s + 1) * sb, (j * A_GROUP + g) * A_HD:(j * A_GROUP + g + 1) * A_HD]
                              for g in range(A_GROUP)], axis=0).astype(BF16)
        sc.append(_dotg(qs, kvb[s * sb:s * sb + WINDOW + sb, j * A_HD:(j + 1) * A_HD], NT_DIMS) * (A_HD ** -0.5))
        yield
    p, esk = [], []
    for n, (s, j) in enumerate(items):
        scm = jnp.where(masks[s], sc[n], NEG)
        mx = jnp.maximum(jnp.max(scm, axis=-1, keepdims=True), sinks[j])
        p.append(jnp.exp(scm - mx).astype(BF16))
        esk.append(jnp.exp(sinks[j] - mx))
        yield
    pv = []
    ones = jnp.ones((WINDOW + sb, A_HD), BF16)
    for n, (s, j) in enumerate(items):
        vx = jnp.concatenate([kvb[s * sb:s * sb + WINDOW + sb,
                                  A_KV_WIDTH + j * A_HD:A_KV_WIDTH + (j + 1) * A_HD], ones], axis=-1)
        pvx = _dot(p[n], vx)
        pv.append(pvx[:, :A_HD] / (pvx[:, A_HD:] + esk[n]))
        yield
    for s in range(nsb):
        o = jnp.concatenate([pv[s * A_KV_HEADS + j][g * sb:(g + 1) * sb]
                             for j in range(A_KV_HEADS) for g in range(A_GROUP)], axis=-1)
        mix_scr[s * sb:(s + 1) * sb, 0:A_WIDTH] = (o * _silu(env['za'][s * sb:(s + 1) * sb])).astype(BF16)
        yield


def _gdn_stages(i, cvb, gt, gtt, env, gprm_ref, gprmt_ref, onorm_ref, s_scr, mix_scr, *, ct, n_lead):
    nc = ct // CHUNK
    items = [(c, h) for c in range(nc) for h in range(B_HEADS)]
    n_items = len(items)
    rs = lambda c: slice(c * CHUNK, (c + 1) * CHUNK)
    qn, kn = [], []
    for h in range(B_HEADS):
        q = cvb[h]
        k = cvb[B_HEADS + h]
        qn.append(q * lax.rsqrt(jnp.sum(q * q, axis=-1, keepdims=True) + EPS) * (B_DK ** -0.5))
        kn.append(k * lax.rsqrt(jnp.sum(k * k, axis=-1, keepdims=True) + EPS))
        yield
    beta = _sigmoid(gt)
    rows = i * ct + lax.broadcasted_iota(jnp.int32, gt.shape, 0)
    gcol = jnp.where(rows >= n_lead, -gprm_ref[0:1, :] * _softplus(gt + gprm_ref[1:2, :]), 0.0)
    lanes = i * ct + lax.broadcasted_iota(jnp.int32, gtt.shape, 1)
    grow = jnp.where(lanes >= n_lead, -gprmt_ref[:, 0:1] * _softplus(gtt + gprmt_ref[:, 1:2]), 0.0)
    ii = lax.broadcasted_iota(jnp.int32, (CHUNK, CHUNK), 0)
    jj = lax.broadcasted_iota(jnp.int32, (CHUNK, CHUNK), 1)
    lower = ii >= jj
    strict = ii > jj
    ltri = lower.astype(BF16)
    utri = (ii <= jj).astype(BF16)
    gcum_c = [sum(_dot(ltri, p) for p in _split3(gcol[rs(c)])) for c in range(nc)]
    gcum_r = [sum(_dot(p, utri) for p in _split3(grow[:, rs(c)])) for c in range(nc)]
    yield
    q_ = [qn[h][rs(c)] for c, h in items]
    k_ = [kn[h][rs(c)] for c, h in items]
    v_ = [cvb[2 * B_HEADS + h][rs(c)] for c, h in items]
    bh = [beta[rs(c), h:h + 1] for c, h in items]
    gc = [gcum_c[c][:, B_HEADS + h:B_HEADS + h + 1] for c, h in items]
    gr = [gcum_r[c][B_HEADS + h:B_HEADS + h + 1, :] for c, h in items]
    kb, a, qk = [], [], []
    for n in range(n_items):
        decay = jnp.exp(jnp.where(lower, gc[n] - gr[n], NEG))
        kb.append(k_[n] * bh[n])
        kq = _dotg(jnp.concatenate([kb[n], q_[n]], axis=0).astype(BF16), k_[n].astype(BF16), NT_DIMS)
        a.append(jnp.where(strict, kq[:CHUNK] * decay, 0.0))
        qk.append(jnp.where(lower, kq[CHUNK:] * decay, 0.0).astype(BF16))
        if n % B_HEADS == B_HEADS - 1:
            yield
    doff = [-jnp.where((ii >> 1) == (jj >> 1), a[n], 0.0) for n in range(n_items)]
    for lg in range(1, 6):
        m = ((ii >> (lg + 1)) == (jj >> (lg + 1))) & ((ii >> lg) != (jj >> lg))
        bm = [jnp.where(m, a[n], 0.0) for n in range(n_items)]
        db = [doff[n].astype(BF16) for n in range(n_items)]
        y = [bm[n] + _dot(db[n], bm[n].astype(BF16)) for n in range(n_items)]
        yield
        x = [y[n] + _dot(y[n].astype(BF16), db[n]) for n in range(n_items)]
        doff = [doff[n] - x[n] for n in range(n_items)]
        yield
    egc = [jnp.exp(gc[n]) for n in range(n_items)]
    g_last = [gc[n][CHUNK - 1:CHUNK, :] for n in range(n_items)]
    eg_last = [jnp.exp(g_last[n]) for n in range(n_items)]
    sol, kd, wq = [], [], []
    for n in range(n_items):
        rhs = jnp.concatenate([v_[n] * bh[n], kb[n] * egc[n]], axis=-1)
        sol.append(rhs + _dot(doff[n].astype(BF16), rhs.astype(BF16)))
        kd.append((k_[n] * jnp.exp(g_last[n] - gc[n])).astype(BF16))
        wq.append(jnp.concatenate([sol[n][:, B_DV:], q_[n] * egc[n]], axis=0).astype(BF16))
        if n % B_HEADS == B_HEADS - 1:
            yield

    def gate_store(c, o):
        for h in range(B_HEADS):
            z = env['zb'][rs(c), h * B_DV:(h + 1) * B_DV]
            mix_scr[rs(c), A_WIDTH + h * B_DV:A_WIDTH + (h + 1) * B_DV] = (
                _rms(o[h], onorm_ref[...]) * _silu(z)).astype(BF16)

    s_cur = [s_scr[h] for h in range(B_HEADS)]
    o_prev = None
    for c in range(nc):
        idx = [c * B_HEADS + h for h in range(B_HEADS)]
        ws = [_dot(wq[n], s_cur[h].astype(BF16)) for h, n in enumerate(idx)]
        if o_prev is not None:
            gate_store(c - 1, o_prev)
        yield
        v_new = [sol[n][:, :B_DV] - ws[h][:CHUNK] for h, n in enumerate(idx)]
        vb = [v_new[h].astype(BF16) for h in range(B_HEADS)]
        o_prev = [ws[h][CHUNK:] + _dot(qk[n], vb[h]) for h, n in enumerate(idx)]
        s_cur = [s_cur[h] * eg_last[n] + _dotg(kd[n], vb[h], TN_DIMS) for h, n in enumerate(idx)]
        yield
    for h in range(B_HEADS):
        s_scr[h] = s_cur[h]
    gate_store(nc - 1, o_prev)
    yield


def _ab_layer_staged_kernel(x_ref, g_ref, w_ref, wg_ref, wgt_ref, sink_ref, hist_ref, convw_ref, gprm_ref,
                            gprmt_ref, onorm_ref, wout_ref, convin_ref, sin_ref,
                            y_ref, kvtail_ref, xbtail_ref, sout_ref,
                            s_scr, xc_scr, kvprev_scr, mix_scr, *, ct, n_lead, pos_shift):
    i = pl.program_id(1)
    nsteps = pl.num_programs(1)
    ntail = min(WINDOW, ct)

    @pl.when(i == 0)
    def _():
        s_scr[...] = sin_ref[0]
        xc_scr[0:8, :] = convin_ref[0]
        kvprev_scr[1] = hist_ref[...]

    x = x_ref[0]
    hb = _rms(x, g_ref[...]).astype(BF16)
    proj = lambda a, b: _dot(hb, w_ref[:, a:b])
    env = {}

    def step(gen, n=1):
        for _ in range(n):
            next(gen, None)

    nblk = B_CONV_CH // LANES
    w = convw_ref[...]
    cvb = []

    def conv_block(j):
        cs = slice(j * LANES, (j + 1) * LANES)
        yc = xc_scr[pl.ds(8 - (CONV_W - 1), ct), cs] * w[0:1, cs]
        for t in range(1, CONV_W):
            yc = yc + xc_scr[pl.ds(8 - (CONV_W - 1) + t, ct), cs] * w[t:t + 1, cs]
        cvb.append(_silu(yc))

    pw = 2 * LANES
    for j in range(nblk // 2):
        xbj = proj(O_XB + j * pw, O_XB + (j + 1) * pw)
        xc_scr[8:8 + ct, j * pw:(j + 1) * pw] = xbj
        xbtail_ref[0, :, j * pw:(j + 1) * pw] = xbj[ct - 8:]
        if j > 0:
            conv_block(2 * j - 2)
            conv_block(2 * j - 1)
    gt = _dot(hb, wg_ref[...])
    gtt = _dotg(wgt_ref[...], hb, NT_DIMS)
    conv_block(nblk - 2)
    conv_block(nblk - 1)
    xc_scr[0:8, :] = xc_scr[ct:ct + 8, :]

    gdn = _gdn_stages(i, cvb, gt, gtt, env, gprm_ref, gprmt_ref, onorm_ref, s_scr, mix_scr, ct=ct, n_lead=n_lead)
    qa_p = []
    for j in range(A_WIDTH // pw):
        qa_p.append(proj(O_QA + j * pw, O_QA + (j + 1) * pw))
        step(gdn, 2)
    qa = jnp.concatenate(qa_p, axis=-1)
    kv = proj(O_KV, O_ZA)
    kvtail_ref[0] = kv[ct - ntail:]
    slot = lax.rem(i, 2)
    kvb = jnp.concatenate([kvprev_scr[1 - slot], kv], axis=0).astype(BF16)
    if ct >= WINDOW:
        kvprev_scr[slot] = kv[ct - WINDOW:]
    swa = _swa_stages(i, qa, kvb, env, sink_ref, mix_scr, ct=ct, sb=ntail, pos_shift=pos_shift)
    n_sw = (ct // ntail) * A_KV_HEADS
    step(gdn)
    for _ in range(ct // CHUNK):
        step(gdn)
        step(swa)
    step(swa, max(n_sw - ct // CHUNK, 0))
    za_p, zb_p = [], []
    fill = ([lambda: step(swa)] * (2 * n_sw)
            + [lambda j=j: za_p.append(proj(O_ZA + j * pw, O_ZA + (j + 1) * pw)) for j in range(A_WIDTH // pw)]
            + [lambda j=j: zb_p.append(proj(O_ZB + j * pw, O_ZB + (j + 1) * pw)) for j in range(B_VAL // pw)])
    per = -(-len(fill) // 10)
    for lv in range(10):
        step(gdn)
        for f in fill[lv * per:(lv + 1) * per]:
            f()
    for f in fill[10 * per:]:
        f()
    env['za'] = jnp.concatenate(za_p, axis=-1)
    env['zb'] = jnp.concatenate(zb_p, axis=-1)
    step(gdn, ct // CHUNK)
    step(swa, ct // ntail)
    ya = x + _dot(mix_scr[:, 0:A_WIDTH], wout_ref[0:A_WIDTH, :])
    for _ in gdn:
        pass
    y_ref[0] = ya + _dot(mix_scr[:, A_WIDTH:], wout_ref[A_WIDTH:, :])

    @pl.when(i == nsteps - 1)
    def _():
        sout_ref[0] = s_scr[...]


def _ab_layer_prompt(x3, norm_g, w_main, w_gate, w_gate_t, sink, hist_kv, conv_w8, gprm, gprm_t, onorm,
                     w_out, conv_in, s_in, ct, n_lead, pos_shift):
    B, L, _ = x3.shape
    ntail = min(WINDOW, ct)
    kern = functools.partial(_ab_layer_staged_kernel, ct=ct, n_lead=n_lead, pos_shift=pos_shift)
    full = lambda a: pl.BlockSpec(a.shape, lambda b, i: (0,) * a.ndim)
    blk = pl.BlockSpec((1, ct, D_MODEL), lambda b, i: (b, i, 0))
    per_b = lambda *s: pl.BlockSpec((1,) + s, lambda b, i: (b,) + (0,) * len(s))
    return pl.pallas_call(
        kern,
        grid=(B, L // ct),
        in_specs=[blk, full(norm_g), full(w_main), full(w_gate), full(w_gate_t), full(sink), full(hist_kv),
                  full(conv_w8), full(gprm), full(gprm_t), full(onorm), full(w_out),
                  per_b(8, B_CONV_CH), per_b(B_HEADS, B_DK, B_DV)],
        out_specs=[blk, per_b(ntail, 2 * A_KV_WIDTH), per_b(8, B_CONV_CH), per_b(B_HEADS, B_DK, B_DV)],
        out_shape=[jax.ShapeDtypeStruct((B, L, D_MODEL), F32),
                   jax.ShapeDtypeStruct((B, ntail, 2 * A_KV_WIDTH), F32),
                   jax.ShapeDtypeStruct((B, 8, B_CONV_CH), F32),
                   jax.ShapeDtypeStruct((B, B_HEADS, B_DK, B_DV), F32)],
        scratch_shapes=[pltpu.VMEM((B_HEADS, B_DK, B_DV), F32),
                        pltpu.VMEM((ct + 8, B_CONV_CH), F32),
                        pltpu.VMEM((2, WINDOW, 2 * A_KV_WIDTH), F32),
                        pltpu.VMEM((ct, AB_MIX), BF16)],
        compiler_params=_cparams(("arbitrary", "arbitrary")),
        name="ab_layer_prompt",
    )(x3, norm_g, w_main, w_gate, w_gate_t, sink, hist_kv, conv_w8, gprm, gprm_t, onorm, w_out, conv_in, s_in)


def _c_layer_kernel(x_ref, g_ref, w_ref, wgk_ref, wup_ref, bgk_ref, onorm_ref, wout_ref, gfin_ref,
                    sin_ref, y_ref, sout_ref, st_scr, og_scr, *, ct, n_lead):
    i = pl.program_id(1)
    nsteps = pl.num_programs(1)

    @pl.when(i == 0)
    def _():
        st_scr[...] = sin_ref[0]

    x = x_ref[0]
    hb = _rms(x, g_ref[...]).astype(BF16)
    q_all = _dot(hb, w_ref[:, 0:C_KEY])
    k_all = _dot(hb, w_ref[:, C_KEY:2 * C_KEY])
    v_all = _dot(hb, w_ref[:, 2 * C_KEY:2 * C_KEY + C_VAL]).astype(BF16)
    z_all = _dot(hb, w_ref[:, 2 * C_KEY + C_VAL:])
    la_all = _gla_log_decay(hb, wgk_ref, wup_ref, bgk_ref)
    rows = i * ct + lax.broadcasted_iota(jnp.int32, la_all.shape, 0)
    la_all = jnp.where(rows >= n_lead, la_all, 0.0)

    ii = lax.broadcasted_iota(jnp.int32, (CHUNK, CHUNK), 0)
    jj = lax.broadcasted_iota(jnp.int32, (CHUNK, CHUNK), 1)
    lower = ii >= jj
    ltri = lower.astype(BF16)
    nc = ct // CHUNK
    rs = lambda c: slice(c * CHUNK, (c + 1) * CHUNK)
    ks = lambda h: slice(h * C_DK, (h + 1) * C_DK)
    vs = lambda h: slice(h * C_DV, (h + 1) * C_DV)
    items = [(c, h) for c in range(nc) for h in range(C_HEADS)]
    n_items = len(items)
    bc_all = [sum(_dot(ltri, p) for p in _split3(la_all[rs(c)])) for c in range(nc)]
    bc = [bc_all[c][:, ks(h)] for c, h in items]
    k_ = [k_all[rs(c), ks(h)] for c, h in items]
    v_ = [v_all[rs(c), vs(h)] for c, h in items]
    qd = [(q_all[rs(c), ks(h)] * (C_DK ** -0.5) * jnp.exp(bc[n])).astype(BF16) for n, (c, h) in enumerate(items)]
    kinv = [(k_[n] * jnp.exp(-bc[n])).astype(BF16) for n in range(n_items)]
    bl = [bc[n][CHUNK - 1:CHUNK, :] for n in range(n_items)]
    kd = [(k_[n] * jnp.exp(bl[n] - bc[n])).astype(BF16) for n in range(n_items)]
    gl = [jnp.exp(bl[n]) for n in range(n_items)]
    qk = [jnp.where(lower, _dotg(qd[n], kinv[n], NT_DIMS), 0.0).astype(BF16) for n in range(n_items)]
    o_intra = [_dot(qk[n], v_[n]) for n in range(n_items)]

    st = [st_scr[h] for h in range(C_HEADS)]
    for c in range(nc):
        idx = [c * C_HEADS + h for h in range(C_HEADS)]
        o = [o_intra[n] + _dotg(qd[n], st[h].astype(BF16), NT_DIMS) for h, n in enumerate(idx)]
        st = [st[h] * gl[n] + _dotg(v_[n], kd[n], TN_DIMS) for h, n in enumerate(idx)]
        for h in range(C_HEADS):
            og_scr[rs(c), vs(h)] = (_rms(o[h], onorm_ref[...]) * _silu(z_all[rs(c), vs(h)])).astype(BF16)
    for h in range(C_HEADS):
        st_scr[h] = st[h]

    y_ref[0] = _rms(x + _dot(og_scr[...], wout_ref[...]), gfin_ref[...])

    @pl.when(i == nsteps - 1)
    def _():
        sout_ref[0] = st_scr[...]


def _c_layer_prompt(x3, norm_g, w_main, w_gk, w_up, b_gk, onorm, w_out, final_g, st_in, ct, n_lead):
    B, L, _ = x3.shape
    kern = functools.partial(_c_layer_kernel, ct=ct, n_lead=n_lead)
    full = lambda a: pl.BlockSpec(a.shape, lambda b, i: (0,) * a.ndim)
    blk = pl.BlockSpec((1, ct, D_MODEL), lambda b, i: (b, i, 0))
    st = pl.BlockSpec((1, C_HEADS, C_DV, C_DK), lambda b, i: (b, 0, 0, 0))
    return pl.pallas_call(
        kern,
        grid=(B, L // ct),
        in_specs=[blk, full(norm_g), full(w_main), full(w_gk), full(w_up), full(b_gk), full(onorm),
                  full(w_out), full(final_g), st],
        out_specs=[blk, st],
        out_shape=[jax.ShapeDtypeStruct((B, L, D_MODEL), F32),
                   jax.ShapeDtypeStruct((B, C_HEADS, C_DV, C_DK), F32)],
        scratch_shapes=[pltpu.VMEM((C_HEADS, C_DV, C_DK), F32), pltpu.VMEM((ct, C_VAL), BF16)],
        compiler_params=_cparams(("arbitrary", "arbitrary")),
        name="c_layer_prompt",
    )(x3, norm_g, w_main, w_gk, w_up, b_gk, onorm, w_out, final_g, st_in)


def _ab_proj_kernel(x_ref, g_ref, w_ref, wg_ref, qa_ref, kv_ref, za_ref, xb_ref, zb_ref, gates_ref):
    hb = _rms(x_ref[...], g_ref[...]).astype(BF16)
    qa_ref[...] = _dot(hb, w_ref[:, O_QA:O_KV])
    kv_ref[...] = _dot(hb, w_ref[:, O_KV:O_ZA])
    za_ref[...] = _dot(hb, w_ref[:, O_ZA:O_XB])
    xb_ref[...] = _dot(hb, w_ref[:, O_XB:O_ZB])
    zb_ref[...] = _dot(hb, w_ref[:, O_ZB:AB_MAIN])
    gates_ref[...] = _dot(hb, wg_ref[...])


def _ab_proj(x2d, norm_g, w_main, w_gate):
    n = x2d.shape[0]
    full = lambda a: pl.BlockSpec(a.shape, lambda: (0,) * a.ndim)
    widths = (A_WIDTH, 2 * A_KV_WIDTH, A_WIDTH, B_CONV_CH, B_VAL, LANES)
    args = (x2d, norm_g, w_main, w_gate)
    return pl.pallas_call(
        _ab_proj_kernel,
        in_specs=[full(a) for a in args],
        out_specs=[pl.BlockSpec((n, w), lambda: (0, 0)) for w in widths],
        out_shape=[jax.ShapeDtypeStruct((n, w), F32) for w in widths],
        compiler_params=pltpu.CompilerParams(vmem_limit_bytes=VMEM_LIMIT),
        name="ab_in_proj_sample",
    )(*args)


def _swa_sample_kernel(q_ref, kvn_ref, za_ref, ck_ref, cv_ref, sink_ref, o_ref, nk_ref, nv_ref, *, bt):
    row = lax.broadcasted_iota(jnp.int32, (WINDOW, A_KV_WIDTH), 0)
    hrow = lax.broadcasted_iota(jnp.int32, (A_HEADS, A_HD), 0)
    sk = sink_ref[...]
    nkb, nvb = [], []
    for b in range(bt):
        nk = jnp.where(row == WINDOW - 1, kvn_ref[b:b + 1, 0:A_KV_WIDTH], pltpu.roll(ck_ref[b], WINDOW - 1, 0))
        nv = jnp.where(row == WINDOW - 1, kvn_ref[b:b + 1, A_KV_WIDTH:], pltpu.roll(cv_ref[b], WINDOW - 1, 0))
        nk_ref[b] = nk
        nv_ref[b] = nv
        nkb.append(nk.astype(BF16))
        nvb.append(nv.astype(BF16))
    items = [(b, j) for b in range(bt) for j in range(A_KV_HEADS)]
    q8 = [q_ref[b].astype(BF16) for b in range(bt)]
    sc = [_dotg(q8[b], nkb[b][:, j * A_HD:(j + 1) * A_HD], NT_DIMS) * (A_HD ** -0.5) for b, j in items]
    m = [jnp.maximum(jnp.max(s, axis=-1, keepdims=True), sk) for s in sc]
    p = [jnp.exp(sc[n] - m[n]) for n in range(len(items))]
    den = [jnp.sum(p[n], axis=-1, keepdims=True) + jnp.exp(sk - m[n]) for n in range(len(items))]
    oj = [_dot(p[n].astype(BF16), nvb[b][:, j * A_HD:(j + 1) * A_HD]) / den[n] for n, (b, j) in enumerate(items)]
    for b in range(bt):
        o8 = jnp.where(hrow >= A_GROUP, oj[b * A_KV_HEADS + 1], oj[b * A_KV_HEADS])
        o_ref[b] = o8 * _silu(za_ref[b])


def _swa_sample(q3, kv_new, za3, cache_k, cache_v, sink_col, bt):
    nb = q3.shape[0]
    hd = pl.BlockSpec((bt, A_HEADS, A_HD), lambda i: (i, 0, 0))
    cache = pl.BlockSpec((bt, WINDOW, A_KV_WIDTH), lambda i: (i, 0, 0))
    return pl.pallas_call(
        functools.partial(_swa_sample_kernel, bt=bt),
        grid=(nb // bt,),
        in_specs=[hd, pl.BlockSpec((bt, 2 * A_KV_WIDTH), lambda i: (i, 0)), hd, cache, cache,
                  pl.BlockSpec(sink_col.shape, lambda i: (0, 0))],
        out_specs=[hd, cache, cache],
        out_shape=[jax.ShapeDtypeStruct(q3.shape, F32),
                   jax.ShapeDtypeStruct(cache_k.shape, F32),
                   jax.ShapeDtypeStruct(cache_v.shape, F32)],
        compiler_params=_cparams(("arbitrary",)),
        name="swa_sample",
    )(q3, kv_new, za3, cache_k, cache_v, sink_col)


def _gdn_sample_prep_kernel(xb_ref, h0_ref, h1_ref, h2_ref, gates_ref, convw_ref, gprm_ref,
                            q_ref, k_ref, v_ref, bg_ref):
    w = convw_ref[...]
    y = (h0_ref[...] * w[0:1, :] + h1_ref[...] * w[1:2, :] + h2_ref[...] * w[2:3, :]
         + xb_ref[...] * w[3:4, :])
    cv = _silu(y)
    for h in range(B_HEADS):
        q = cv[:, h * B_DK:(h + 1) * B_DK]
        k = cv[:, B_KEY + h * B_DK:B_KEY + (h + 1) * B_DK]
        q_ref[:, h * B_DK:(h + 1) * B_DK] = (
            q * lax.rsqrt(jnp.sum(q * q, axis=-1, keepdims=True) + EPS) * (B_DK ** -0.5))
        k_ref[:, h * B_DK:(h + 1) * B_DK] = k * lax.rsqrt(jnp.sum(k * k, axis=-1, keepdims=True) + EPS)
    v_ref[...] = cv[:, 2 * B_KEY:]
    gt = gates_ref[...]
    col = lax.broadcasted_iota(jnp.int32, gt.shape, 1)
    g = -gprm_ref[0:1, :] * _softplus(gt + gprm_ref[1:2, :])
    bg_ref[...] = jnp.where(col < B_HEADS, _sigmoid(gt), jnp.exp(g))


def _gdn_sample_prep(xb, h0, h1, h2, gates, conv_w8, gprm):
    n = xb.shape[0]
    full = lambda a: pl.BlockSpec(a.shape, lambda: (0,) * a.ndim)
    args = (xb, h0, h1, h2, gates, conv_w8, gprm)
    return pl.pallas_call(
        _gdn_sample_prep_kernel,
        in_specs=[full(a) for a in args],
        out_specs=[pl.BlockSpec((n, B_KEY), lambda: (0, 0))] * 3 + [pl.BlockSpec((n, LANES), lambda: (0, 0))],
        out_shape=[jax.ShapeDtypeStruct((n, B_KEY), F32)] * 3 + [jax.ShapeDtypeStruct((n, LANES), F32)],
        compiler_params=pltpu.CompilerParams(vmem_limit_bytes=VMEM_LIMIT),
        name="gdn_sample_prep",
    )(*args)


def _gdn_sample_kernel(s_ref, qt_ref, kt_ref, v_ref, bg_ref, zb_ref, onorm_ref, o_ref, sout_ref, *, bt):
    items = [(b, h) for b in range(bt) for h in range(B_HEADS)]
    hs = lambda h: slice(h * B_DV, (h + 1) * B_DV)
    kc = [kt_ref[0, h, :, b:b + 1] for b, h in items]
    eg = [bg_ref[b:b + 1, B_HEADS + h:B_HEADS + h + 1] for b, h in items]
    ks = [jnp.sum(kc[n] * s_ref[b, h], axis=0, keepdims=True) for n, (b, h) in enumerate(items)]
    v_new = [bg_ref[b:b + 1, h:h + 1] * (v_ref[b:b + 1, hs(h)] - eg[n] * ks[n]) for n, (b, h) in enumerate(items)]
    o = []
    for n, (b, h) in enumerate(items):
        s_new = eg[n] * s_ref[b, h] + kc[n] * v_new[n]
        sout_ref[b, h] = s_new
        o.append(jnp.sum(qt_ref[0, h, :, b:b + 1] * s_new, axis=0, keepdims=True))
    o_blk = jnp.concatenate([jnp.concatenate([o[b * B_HEADS + h] for h in range(B_HEADS)], axis=-1)
                             for b in range(bt)], axis=0)
    for h in range(B_HEADS):
        o_ref[:, hs(h)] = _rms(o_blk[:, hs(h)], onorm_ref[...]) * _silu(zb_ref[:, hs(h)])


def _gdn_sample(state, qt, kt, v, bg, zb, onorm, bt):
    nb = state.shape[0]
    st = pl.BlockSpec((bt, B_HEADS, B_DK, B_DV), lambda i: (i, 0, 0, 0))
    col = pl.BlockSpec((1, B_HEADS, B_DK, bt), lambda i: (i, 0, 0, 0))
    row = lambda n: pl.BlockSpec((bt, n), lambda i: (i, 0))
    return pl.pallas_call(
        functools.partial(_gdn_sample_kernel, bt=bt),
        grid=(nb // bt,),
        in_specs=[st, col, col, row(B_VAL), row(LANES), row(B_VAL),
                  pl.BlockSpec(onorm.shape, lambda i: (0, 0))],
        out_specs=[row(B_VAL), st],
        out_shape=[jax.ShapeDtypeStruct((nb, B_VAL), F32), jax.ShapeDtypeStruct(state.shape, F32)],
        compiler_params=_cparams(("arbitrary",)),
        name="gdn_sample",
    )(state, qt, kt, v, bg, zb, onorm)


def _ab_out_c_proj_kernel(x_ref, oa_ref, ob_ref, wout_ref, g_ref, w_ref, wgk_ref, wup_ref, bgk_ref,
                          y_ref, q_ref, k_ref, v_ref, z_ref, la_ref):
    y = (x_ref[...] + _dot(oa_ref[...].astype(BF16), wout_ref[0:A_WIDTH, :])
         + _dot(ob_ref[...].astype(BF16), wout_ref[A_WIDTH:, :]))
    y_ref[...] = y
    hb = _rms(y, g_ref[...]).astype(BF16)
    q_ref[...] = _dot(hb, w_ref[:, 0:C_KEY])
    k_ref[...] = _dot(hb, w_ref[:, C_KEY:2 * C_KEY])
    v_ref[...] = _dot(hb, w_ref[:, 2 * C_KEY:2 * C_KEY + C_VAL])
    z_ref[...] = _dot(hb, w_ref[:, 2 * C_KEY + C_VAL:])
    la_ref[...] = _gla_log_decay(hb, wgk_ref, wup_ref, bgk_ref)


def _ab_out_c_proj(x2d, oa, ob, w_out, norm_g, w_main, w_gk, w_up, b_gk):
    n = x2d.shape[0]
    full = lambda a: pl.BlockSpec(a.shape, lambda: (0,) * a.ndim)
    widths = (D_MODEL, C_KEY, C_KEY, C_VAL, C_VAL, C_KEY)
    args = (x2d, oa, ob, w_out, norm_g, w_main, w_gk, w_up, b_gk)
    return pl.pallas_call(
        _ab_out_c_proj_kernel,
        in_specs=[full(a) for a in args],
        out_specs=[pl.BlockSpec((n, w), lambda: (0, 0)) for w in widths],
        out_shape=[jax.ShapeDtypeStruct((n, w), F32) for w in widths],
        compiler_params=pltpu.CompilerParams(vmem_limit_bytes=VMEM_LIMIT),
        name="ab_out_c_in_proj_sample",
    )(*args)


def _gla_sample_kernel(s_ref, qt_ref, kt_ref, lat_ref, v_ref, z_ref, onorm_ref, o_ref, sout_ref, *, bt):
    items = [(b, h) for b in range(bt) for h in range(C_HEADS)]
    hs = lambda h: slice(h * C_DV, (h + 1) * C_DV)
    ac = [jnp.exp(lat_ref[0, h, :, b:b + 1]) for b, h in items]
    o = []
    for n, (b, h) in enumerate(items):
        s_new = ac[n] * s_ref[b, h] + kt_ref[0, h, :, b:b + 1] * v_ref[b:b + 1, hs(h)]
        sout_ref[b, h] = s_new
        o.append(jnp.sum((qt_ref[0, h, :, b:b + 1] * (C_DK ** -0.5)) * s_new, axis=0, keepdims=True))
    o_blk = jnp.concatenate([jnp.concatenate([o[b * C_HEADS + h] for h in range(C_HEADS)], axis=-1)
                             for b in range(bt)], axis=0)
    for h in range(C_HEADS):
        o_ref[:, hs(h)] = _rms(o_blk[:, hs(h)], onorm_ref[...]) * _silu(z_ref[:, hs(h)])


def _gla_sample(state, qt, kt, lat, v, z, onorm, bt):
    nb = state.shape[0]
    st = pl.BlockSpec((bt, C_HEADS, C_DK, C_DV), lambda i: (i, 0, 0, 0))
    col = pl.BlockSpec((1, C_HEADS, C_DK, bt), lambda i: (i, 0, 0, 0))
    row = lambda n: pl.BlockSpec((bt, n), lambda i: (i, 0))
    return pl.pallas_call(
        functools.partial(_gla_sample_kernel, bt=bt),
        grid=(nb // bt,),
        in_specs=[st, col, col, col, row(C_VAL), row(C_VAL),
                  pl.BlockSpec(onorm.shape, lambda i: (0, 0))],
        out_specs=[row(C_VAL), st],
        out_shape=[jax.ShapeDtypeStruct((nb, C_VAL), F32), jax.ShapeDtypeStruct(state.shape, F32)],
        compiler_params=_cparams(("arbitrary",)),
        name="gla_sample",
    )(state, qt, kt, lat, v, z, onorm)


def _c_out_kernel(x_ref, o_ref, w_ref, g_ref, y_ref):
    y_ref[...] = _rms(x_ref[...] + _dot(o_ref[...].astype(BF16), w_ref[...]), g_ref[...])


def _c_out(x2d, oc, w_out, final_g):
    n = x2d.shape[0]
    full = lambda a: pl.BlockSpec(a.shape, lambda: (0,) * a.ndim)
    args = (x2d, oc, w_out, final_g)
    return pl.pallas_call(
        _c_out_kernel,
        in_specs=[full(a) for a in args],
        out_specs=pl.BlockSpec((n, D_MODEL), lambda: (0, 0)),
        out_shape=jax.ShapeDtypeStruct((n, D_MODEL), F32),
        compiler_params=pltpu.CompilerParams(vmem_limit_bytes=VMEM_LIMIT),
        name="c_out_proj_norm_sample",
    )(*args)


def _to_cols(a, heads, dk, bt):
    n = a.shape[0]
    return a.reshape(n // bt, bt, heads, dk).transpose(0, 2, 3, 1)


def kernel(x_prompt, x_sample, cache_swa_k, cache_swa_v, state_dn_conv, state_dn, state_gla,
           meta_tokens, norm_ab, w_in_ab, sink_a, conv_b, a_log_b, dt_bias_b, onorm_b, w_out_ab,
           norm_c, w_in_c, w_gk_up, b_gk, onorm_c, w_out_c, final_norm):
    Bp, L, _ = x_prompt.shape
    Bs = x_sample.shape[0]
    CT, BT = 256, 8

    w_ab = w_in_ab[0]
    w_ab_main = w_ab[:, :AB_MAIN].astype(BF16)
    w_ab_gate = jnp.pad(w_ab[:, AB_MAIN:], ((0, 0), (0, LANES - 2 * B_HEADS))).astype(BF16)
    w_ab_gate_t = jnp.pad(w_ab[:, AB_MAIN:].T, ((0, 16 - 2 * B_HEADS), (0, 0))).astype(BF16)
    g_ab = norm_ab[0][None, :]
    sink_row = sink_a[0][None, :]
    sink_col = sink_a[0][:, None]
    conv_w8 = jnp.pad(conv_b[0], ((0, 8 - CONV_W), (0, 0)))
    ea = jnp.exp(a_log_b[0])
    gprm = jnp.zeros((8, LANES), F32).at[0, B_HEADS:2 * B_HEADS].set(ea).at[1, B_HEADS:2 * B_HEADS].set(dt_bias_b[0])
    gprm_t = jnp.zeros((16, LANES), F32).at[B_HEADS:2 * B_HEADS, 0].set(ea).at[B_HEADS:2 * B_HEADS, 1].set(dt_bias_b[0])
    onb = onorm_b[0][None, :]
    w_ab_out = w_out_ab[0].astype(BF16)
    w_c = w_in_c[0]
    w_c_main = w_c[:, :C_MAIN].astype(BF16)
    w_c_gk = jnp.pad(w_c[:, C_MAIN:], ((0, 0), (0, LANES - C_RANK))).astype(BF16)
    w_up = jnp.pad(w_gk_up[0], ((0, LANES - C_RANK), (0, 0)))
    bgk = b_gk[0][None, :]
    g_c = norm_c[0][None, :]
    onc = onorm_c[0][None, :]
    w_c_out = w_out_c[0].astype(BF16)
    g_fin = final_norm[None, :]

    def ab_layer(x3, ct, hist_kv, pos_shift, conv_in, s_in, n_lead):
        return _ab_layer_prompt(x3, g_ab, w_ab_main, w_ab_gate, w_ab_gate_t, sink_row, hist_kv, conv_w8,
                                gprm, gprm_t, onb, w_ab_out, conv_in, s_in, ct, n_lead, pos_shift)

    def c_layer(x3, ct, st_in, n_lead):
        return _c_layer_prompt(x3, g_c, w_c_main, w_c_gk, w_up, bgk, onc, w_c_out, g_fin, st_in, ct, n_lead)

    xpre = jnp.concatenate([jnp.zeros((LEAD, D_MODEL), F32), meta_tokens], axis=0)[None]
    y_pre, kv_pre, xbtail_pre, sdn_pre = ab_layer(
        xpre, CHUNK, jnp.zeros((WINDOW, 2 * A_KV_WIDTH), F32), -LEAD,
        jnp.zeros((1, 8, B_CONV_CH), F32), jnp.zeros((1, B_HEADS, B_DK, B_DV), F32), LEAD)
    _, sgla_pre = c_layer(y_pre, CHUNK, jnp.zeros((1, C_HEADS, C_DV, C_DK), F32), LEAD)

    hist_kv = jnp.concatenate([jnp.zeros((WINDOW - CHUNK, 2 * A_KV_WIDTH), F32), kv_pre[0]], axis=0)
    y1, kv_tail, xb_tail, sdn_p = ab_layer(
        x_prompt, CT, hist_kv, N_META, jnp.broadcast_to(xbtail_pre, (Bp, 8, B_CONV_CH)),
        jnp.broadcast_to(sdn_pre, (Bp, B_HEADS, B_DK, B_DV)), 0)
    y_prompt, sgla_t = c_layer(y1, 2 * CT, jnp.broadcast_to(sgla_pre, (Bp, C_HEADS, C_DV, C_DK)), 0)
    sgla_p = jnp.swapaxes(sgla_t, 2, 3)

    swa_k_p = kv_tail[:, :, :A_KV_WIDTH].reshape(1, Bp, WINDOW, A_KV_HEADS, A_HD)
    swa_v_p = kv_tail[:, :, A_KV_WIDTH:].reshape(1, Bp, WINDOW, A_KV_HEADS, A_HD)
    dn_conv_p = xb_tail[:, 8 - (CONV_W - 1):, :][None]

    xs = x_sample.reshape(Bs, D_MODEL)
    qa, kvn, za, xb, zb, gates = _ab_proj(xs, g_ab, w_ab_main, w_ab_gate)
    oa3, nk, nv = _swa_sample(qa.reshape(Bs, A_HEADS, A_HD), kvn, za.reshape(Bs, A_HEADS, A_HD),
                              cache_swa_k[0].reshape(Bs, WINDOW, A_KV_WIDTH),
                              cache_swa_v[0].reshape(Bs, WINDOW, A_KV_WIDTH), sink_col, BT)
    hist = state_dn_conv[0]
    qn, kn, vn, bg = _gdn_sample_prep(xb, hist[:, 0], hist[:, 1], hist[:, 2], gates, conv_w8, gprm)
    ob, sdn_s = _gdn_sample(state_dn[0], _to_cols(qn, B_HEADS, B_DK, BT), _to_cols(kn, B_HEADS, B_DK, BT),
                            vn, bg, zb, onb, BT)
    ys, qc, kc, vc, zc, la = _ab_out_c_proj(xs, oa3.reshape(Bs, A_WIDTH), ob, w_ab_out,
                                            g_c, w_c_main, w_c_gk, w_up, bgk)
    ocs, sgla_s = _gla_sample(state_gla[0], _to_cols(qc, C_HEADS, C_DK, BT), _to_cols(kc, C_HEADS, C_DK, BT),
                              _to_cols(la, C_HEADS, C_DK, BT), vc, zc, onc, BT)
    y_sample = _c_out(ys, ocs, w_c_out, g_fin).reshape(Bs, 1, D_MODEL)
    dn_conv_s = jnp.concatenate([hist[:, 1:], xb[:, None, :]], axis=1)[None]

    return (y_prompt, y_sample, swa_k_p, swa_v_p, dn_conv_p, sdn_p[None], sgla_p[None],
            nk.reshape(1, Bs, WINDOW, A_KV_HEADS, A_HD), nv.reshape(1, Bs, WINDOW, A_KV_HEADS, A_HD),
            dn_conv_s, sdn_s[None], sgla_s[None])
```

```python
import functools

import jax
import jax.numpy as jnp
from jax import lax
from jax.experimental import pallas as pl
from jax.experimental.pallas import tpu as pltpu

F32 = jnp.float32
BF16 = jnp.bfloat16
EPS = 1e-6

D_MODEL = 1024
N_META = 16
CHUNK = 64
LEAD = (-N_META) % CHUNK
A_HEADS, A_KV_HEADS, A_HD = 8, 2, 64
A_GROUP = A_HEADS // A_KV_HEADS
A_WIDTH = A_HEADS * A_HD
A_KV_WIDTH = A_KV_HEADS * A_HD
WINDOW = 128
B_HEADS, B_DK, B_DV = 4, 128, 128
B_KEY = B_HEADS * B_DK
B_VAL = B_HEADS * B_DV
CONV_W = 4
B_CONV_CH = 2 * B_KEY + B_VAL
C_HEADS, C_DK, C_DV = 4, 128, 256
C_KEY = C_HEADS * C_DK
C_VAL = C_HEADS * C_DV
C_RANK = 16
C_GATE_NORM = 16.0
AB_MAIN = 2 * A_WIDTH + 2 * A_KV_WIDTH + B_CONV_CH + B_VAL
AB_MIX = A_WIDTH + B_VAL
C_MAIN = 2 * C_KEY + 2 * C_VAL
O_QA, O_KV, O_ZA, O_XB, O_ZB = 0, 512, 768, 1280, 2816
LANES = 128
NEG = -1e30

VMEM_LIMIT = 56 * 1024 * 1024

NT_DIMS = (((1,), (1,)), ((), ()))
TN_DIMS = (((0,), (0,)), ((), ()))


def _cparams(sem):
    return pltpu.CompilerParams(dimension_semantics=sem, vmem_limit_bytes=VMEM_LIMIT)


def _dot(a, b):
    return jnp.dot(a, b, preferred_element_type=F32)


def _dotg(a, b, dims):
    return lax.dot_general(a, b, dims, preferred_element_type=F32)


def _split3(a):
    hi = a.astype(BF16)
    r = a - hi.astype(F32)
    mid = r.astype(BF16)
    lo = (r - mid.astype(F32)).astype(BF16)
    return hi, mid, lo


def _sigmoid(x):
    return 1.0 / (1.0 + jnp.exp(-x))


def _silu(x):
    return x * _sigmoid(x)


def _softplus(x):
    return jnp.maximum(x, 0.0) + jnp.log1p(jnp.exp(-jnp.abs(x)))


def _log_sigmoid(x):
    return jnp.minimum(x, 0.0) - jnp.log1p(jnp.exp(-jnp.abs(x)))


def _rms(x, g):
    return x * lax.rsqrt(jnp.mean(x * x, axis=-1, keepdims=True) + EPS) * g


def _gla_log_decay(low, wup_ref, bgk_ref):
    lh, lm, _ = _split3(low)
    wh, wm, _ = _split3(wup_ref[...])
    up = _dot(lh, wh) + _dot(lh, wm) + _dot(lm, wh)
    return _log_sigmoid(up + bgk_ref[...]) * (1.0 / C_GATE_NORM)


def _swa_tile(i, qa, kvb, za, sink_ref, mix_scr, *, ct, sb, pos_shift):
    r = lax.broadcasted_iota(jnp.int32, (sb, WINDOW + sb), 0)
    c = lax.broadcasted_iota(jnp.int32, (sb, WINDOW + sb), 1)
    diff = r - c + WINDOW
    in_window = (diff >= 0) & (diff < WINDOW)
    nsb = ct // sb
    items = [(s, j) for s in range(nsb) for j in range(A_KV_HEADS)]
    masks = [jnp.concatenate([in_window & (i * ct + s * sb + c - WINDOW + pos_shift >= 0)] * A_GROUP, axis=0)
             for s in range(nsb)]
    sinks = [jnp.concatenate([jnp.broadcast_to(sink_ref[0:1, j * A_GROUP + g:j * A_GROUP + g + 1], (sb, 1))
                              for g in range(A_GROUP)], axis=0) for j in range(A_KV_HEADS)]
    qs = [jnp.concatenate([qa[s * sb:(s + 1) * sb, (j * A_GROUP + g) * A_HD:(j * A_GROUP + g + 1) * A_HD]
                           for g in range(A_GROUP)], axis=0).astype(BF16) for s, j in items]
    sc = [_dotg(qs[n], kvb[s * sb:s * sb + WINDOW + sb, j * A_HD:(j + 1) * A_HD], NT_DIMS) * (A_HD ** -0.5)
          for n, (s, j) in enumerate(items)]
    sc = [jnp.where(masks[s], sc[n], NEG) for n, (s, j) in enumerate(items)]
    mx = [jnp.maximum(jnp.max(sc[n], axis=-1, keepdims=True), sinks[j]) for n, (s, j) in enumerate(items)]
    p = [jnp.exp(sc[n] - mx[n]) for n in range(len(items))]
    den = [jnp.sum(p[n], axis=-1, keepdims=True) + jnp.exp(sinks[j] - mx[n]) for n, (s, j) in enumerate(items)]
    pv = [_dot(p[n].astype(BF16),
               kvb[s * sb:s * sb + WINDOW + sb, A_KV_WIDTH + j * A_HD:A_KV_WIDTH + (j + 1) * A_HD]) / den[n]
          for n, (s, j) in enumerate(items)]
    for s in range(nsb):
        o = jnp.concatenate([pv[s * A_KV_HEADS + j][g * sb:(g + 1) * sb]
                             for j in range(A_KV_HEADS) for g in range(A_GROUP)], axis=-1)
        mix_scr[s * sb:(s + 1) * sb, 0:A_WIDTH] = (o * _silu(za[s * sb:(s + 1) * sb])).astype(BF16)


def _gdn_tile(i, cv, gt, gtt, zb, gprm_ref, gprmt_ref, onorm_ref, s_scr, mix_scr, *, ct, n_lead):
    beta = _sigmoid(gt)
    rows = i * ct + lax.broadcasted_iota(jnp.int32, gt.shape, 0)
    gcol = jnp.where(rows >= n_lead, -gprm_ref[0:1, :] * _softplus(gt + gprm_ref[1:2, :]), 0.0)
    lanes = i * ct + lax.broadcasted_iota(jnp.int32, gtt.shape, 1)
    grow = jnp.where(lanes >= n_lead, -gprmt_ref[:, 0:1] * _softplus(gtt + gprmt_ref[:, 1:2]), 0.0)

    ii = lax.broadcasted_iota(jnp.int32, (CHUNK, CHUNK), 0)
    jj = lax.broadcasted_iota(jnp.int32, (CHUNK, CHUNK), 1)
    lower = ii >= jj
    strict = ii > jj
    ltri = lower.astype(BF16)
    utri = (ii <= jj).astype(BF16)

    nc = ct // CHUNK
    items = [(c, h) for c in range(nc) for h in range(B_HEADS)]
    n_items = len(items)
    rs = lambda c: slice(c * CHUNK, (c + 1) * CHUNK)
    qn, kn = [], []
    for h in range(B_HEADS):
        q = cv[:, h * B_DK:(h + 1) * B_DK]
        k = cv[:, B_KEY + h * B_DK:B_KEY + (h + 1) * B_DK]
        qn.append(q * lax.rsqrt(jnp.sum(q * q, axis=-1, keepdims=True) + EPS) * (B_DK ** -0.5))
        kn.append(k * lax.rsqrt(jnp.sum(k * k, axis=-1, keepdims=True) + EPS))
    gcum_c = [sum(_dot(ltri, p) for p in _split3(gcol[rs(c)])) for c in range(nc)]
    gcum_r = [sum(_dot(p, utri) for p in _split3(grow[:, rs(c)])) for c in range(nc)]
    q_ = [qn[h][rs(c)] for c, h in items]
    k_ = [kn[h][rs(c)] for c, h in items]
    v_ = [cv[rs(c), 2 * B_KEY + h * B_DV:2 * B_KEY + (h + 1) * B_DV] for c, h in items]
    bh = [beta[rs(c), h:h + 1] for c, h in items]
    gc = [gcum_c[c][:, B_HEADS + h:B_HEADS + h + 1] for c, h in items]
    gr = [gcum_r[c][B_HEADS + h:B_HEADS + h + 1, :] for c, h in items]
    decay = [jnp.exp(jnp.where(lower, gc[n] - gr[n], NEG)) for n in range(n_items)]
    kb = [k_[n] * bh[n] for n in range(n_items)]
    kbf = [k_[n].astype(BF16) for n in range(n_items)]
    kq = [_dotg(jnp.concatenate([kb[n], q_[n]], axis=0).astype(BF16), kbf[n], NT_DIMS) for n in range(n_items)]
    a = [jnp.where(strict, kq[n][:CHUNK] * decay[n], 0.0) for n in range(n_items)]
    qk = [jnp.where(lower, kq[n][CHUNK:] * decay[n], 0.0).astype(BF16) for n in range(n_items)]
    doff = [-jnp.where((ii >> 1) == (jj >> 1), a[n], 0.0) for n in range(n_items)]
    for lg in range(1, 6):
        m = ((ii >> (lg + 1)) == (jj >> (lg + 1))) & ((ii >> lg) != (jj >> lg))
        bm = [jnp.where(m, a[n], 0.0) for n in range(n_items)]
        db = [doff[n].astype(BF16) for n in range(n_items)]
        y = [bm[n] + _dot(db[n], bm[n].astype(BF16)) for n in range(n_items)]
        x = [y[n] + _dot(y[n].astype(BF16), db[n]) for n in range(n_items)]
        doff = [doff[n] - x[n] for n in range(n_items)]
    egc = [jnp.exp(gc[n]) for n in range(n_items)]
    rhs = [jnp.concatenate([v_[n] * bh[n], kb[n] * egc[n]], axis=-1) for n in range(n_items)]
    sol = [rhs[n] + _dot(doff[n].astype(BF16), rhs[n].astype(BF16)) for n in range(n_items)]
    g_last = [gc[n][CHUNK - 1:CHUNK, :] for n in range(n_items)]
    kd = [(k_[n] * jnp.exp(g_last[n] - gc[n])).astype(BF16) for n in range(n_items)]
    wq = [jnp.concatenate([sol[n][:, B_DV:], q_[n] * egc[n]], axis=0).astype(BF16) for n in range(n_items)]
    eg_last = [jnp.exp(g_last[n]) for n in range(n_items)]

    s_cur = [s_scr[h] for h in range(B_HEADS)]
    for c in range(nc):
        idx = [c * B_HEADS + h for h in range(B_HEADS)]
        ws = [_dot(wq[n], s_cur[h].astype(BF16)) for h, n in enumerate(idx)]
        v_new = [sol[n][:, :B_DV] - ws[h][:CHUNK] for h, n in enumerate(idx)]
        vb = [v_new[h].astype(BF16) for h in range(B_HEADS)]
        o = [ws[h][CHUNK:] + _dot(qk[n], vb[h]) for h, n in enumerate(idx)]
        s_cur = [s_cur[h] * eg_last[n] + _dotg(kd[n], vb[h], TN_DIMS) for h, n in enumerate(idx)]
        for h in range(B_HEADS):
            z = zb[rs(c), h * B_DV:(h + 1) * B_DV]
            mix_scr[rs(c), A_WIDTH + h * B_DV:A_WIDTH + (h + 1) * B_DV] = (
                _rms(o[h], onorm_ref[...]) * _silu(z)).astype(BF16)
    for h in range(B_HEADS):
        s_scr[h] = s_cur[h]


def _ab_layer_kernel(x_ref, g_ref, w_ref, wg_ref, wgt_ref, sink_ref, hist_ref, convw_ref, gprm_ref,
                     gprmt_ref, onorm_ref, wout_ref, convin_ref, sin_ref,
                     y_ref, kvtail_ref, xbtail_ref, sout_ref,
                     s_scr, xc_scr, kvprev_scr, mix_scr, *, ct, n_lead, pos_shift):
    i = pl.program_id(1)
    nsteps = pl.num_programs(1)
    ntail = min(WINDOW, ct)

    @pl.when(i == 0)
    def _():
        s_scr[...] = sin_ref[0]
        xc_scr[0:8, :] = convin_ref[0]
        kvprev_scr[1] = hist_ref[...]

    x = x_ref[0]
    hb = _rms(x, g_ref[...]).astype(BF16)
    qa = _dot(hb, w_ref[:, O_QA:O_KV])
    kv = _dot(hb, w_ref[:, O_KV:O_ZA])
    za = _dot(hb, w_ref[:, O_ZA:O_XB])
    xb = _dot(hb, w_ref[:, O_XB:O_ZB])
    zb = _dot(hb, w_ref[:, O_ZB:AB_MAIN])
    gt = _dot(hb, wg_ref[...])
    gtt = _dotg(wgt_ref[...], hb, NT_DIMS)
    kvtail_ref[0] = kv[ct - ntail:]
    xbtail_ref[0] = xb[ct - 8:]

    slot = lax.rem(i, 2)
    kvb = jnp.concatenate([kvprev_scr[1 - slot], kv], axis=0).astype(BF16)
    if ct >= WINDOW:
        kvprev_scr[slot] = kv[ct - WINDOW:]
    _swa_tile(i, qa, kvb, za, sink_ref, mix_scr, ct=ct, sb=ntail, pos_shift=pos_shift)

    xc_scr[8:8 + ct, :] = xb
    w = convw_ref[...]
    yc = xc_scr[pl.ds(8 - (CONV_W - 1), ct), :] * w[0:1, :]
    for t in range(1, CONV_W):
        yc = yc + xc_scr[pl.ds(8 - (CONV_W - 1) + t, ct), :] * w[t:t + 1, :]
    cv = _silu(yc)
    xc_scr[0:8, :] = xc_scr[ct:ct + 8, :]
    _gdn_tile(i, cv, gt, gtt, zb, gprm_ref, gprmt_ref, onorm_ref, s_scr, mix_scr, ct=ct, n_lead=n_lead)

    y_ref[0] = x + _dot(mix_scr[...], wout_ref[...])

    @pl.when(i == nsteps - 1)
    def _():
        sout_ref[0] = s_scr[...]


def _swa_stages(i, qa, kvb, env, sink_ref, mix_scr, *, ct, sb, pos_shift):
    r = lax.broadcasted_iota(jnp.int32, (sb, WINDOW + sb), 0)
    c = lax.broadcasted_iota(jnp.int32, (sb, WINDOW + sb), 1)
    diff = r - c + WINDOW
    in_window = (diff >= 0) & (diff < WINDOW)
    nsb = ct // sb
    items = [(s, j) for s in range(nsb) for j in range(A_KV_HEADS)]
    masks = [jnp.concatenate([in_window & (i * ct + s * sb + c - WINDOW + pos_shift >= 0)] * A_GROUP, axis=0)
             for s in range(nsb)]
    sinks = [jnp.concatenate([jnp.broadcast_to(sink_ref[0:1, j * A_GROUP + g:j * A_GROUP + g + 1], (sb, 1))
                              for g in range(A_GROUP)], axis=0) for j in range(A_KV_HEADS)]
    sc = []
    for s, j in items:
        qs = jnp.concatenate([qa[s * sb:(s + 1) * sb, (j * A_GROUP + g) * A_HD:(j * A_GROUP + g + 1) * A_HD]
                              for g in range(A_GROUP)], axis=0).astype(BF16)
        sc.append(_dotg(qs, kvb[s * sb:s * sb + WINDOW + sb, j * A_HD:(j + 1) * A_HD], NT_DIMS) * (A_HD ** -0.5))
        yield
    p, esk = [], []
    for n, (s, j) in enumerate(items):
        scm = jnp.where(masks[s], sc[n], NEG)
        mx = jnp.maximum(jnp.max(scm, axis=-1, keepdims=True), sinks[j])
        p.append(jnp.exp(scm - mx).astype(BF16))
        esk.append(jnp.exp(sinks[j] - mx))
        yield
    pv = []
    ones = jnp.ones((WINDOW + sb, A_HD), BF16)
    for n, (s, j) in enumerate(items):
        vx = jnp.concatenate([kvb[s * sb:s * sb + WINDOW + sb,
                                  A_KV_WIDTH + j * A_HD:A_KV_WIDTH + (j + 1) * A_HD], ones], axis=-1)
        pvx = _dot(p[n], vx)
        pv.append(pvx[:, :A_HD] / (pvx[:, A_HD:] + esk[n]))
        yield
    for s in range(nsb):
        o = jnp.concatenate([pv[s * A_KV_HEADS + j][g * sb:(g + 1) * sb]
                             for j in range(A_KV_HEADS) for g in range(A_GROUP)], axis=-1)
        mix_scr[s * sb:(s + 1) * sb, 0:A_WIDTH] = (o * _silu(env['za'][s * sb:(s + 1) * sb])).astype(BF16)
        yield


def _gdn_stages(i, cvb, gt, gtt, env, gprm_ref, gprmt_ref, onorm_ref, s_scr, mix_scr, *, ct, n_lead):
    nc = ct // CHUNK
    items = [(c, h) for c in range(nc) for h in range(B_HEADS)]
    n_items = len(items)
    rs = lambda c: slice(c * CHUNK, (c + 1) * CHUNK)
    qn, kn = [], []
    for h in range(B_HEADS):
        q = cvb[h]
        k = cvb[B_HEADS + h]
        qn.append(q * lax.rsqrt(jnp.sum(q * q, axis=-1, keepdims=True) + EPS) * (B_DK ** -0.5))
        kn.append(k * lax.rsqrt(jnp.sum(k * k, axis=-1, keepdims=True) + EPS))
        yield
    beta = _sigmoid(gt)
    rows = i * ct + lax.broadcasted_iota(jnp.int32, gt.shape, 0)
    gcol = jnp.where(rows >= n_lead, -gprm_ref[0:1, :] * _softplus(gt + gprm_ref[1:2, :]), 0.0)
    lanes = i * ct + lax.broadcasted_iota(jnp.int32, gtt.shape, 1)
    grow = jnp.where(lanes >= n_lead, -gprmt_ref[:, 0:1] * _softplus(gtt + gprmt_ref[:, 1:2]), 0.0)
    ii = lax.broadcasted_iota(jnp.int32, (CHUNK, CHUNK), 0)
    jj = lax.broadcasted_iota(jnp.int32, (CHUNK, CHUNK), 1)
    lower = ii >= jj
    strict = ii > jj
    ltri = lower.astype(BF16)
    utri = (ii <= jj).astype(BF16)
    gcum_c = [sum(_dot(ltri, p) for p in _split3(gcol[rs(c)])) for c in range(nc)]
    gcum_r = [sum(_dot(p, utri) for p in _split3(grow[:, rs(c)])) for c in range(nc)]
    yield
    q_ = [qn[h][rs(c)] for c, h in items]
    k_ = [kn[h][rs(c)] for c, h in items]
    v_ = [cvb[2 * B_HEADS + h][rs(c)] for c, h in items]
    bh = [beta[rs(c), h:h + 1] for c, h in items]
    gc = [gcum_c[c][:, B_HEADS + h:B_HEADS + h + 1] for c, h in items]
    gr = [gcum_r[c][B_HEADS + h:B_HEADS + h + 1, :] for c, h in items]
    kb, a, qk = [], [], []
    for n in range(n_items):
        decay = jnp.exp(jnp.where(lower, gc[n] - gr[n], NEG))
        kb.append(k_[n] * bh[n])
        kq = _dotg(jnp.concatenate([kb[n], q_[n]], axis=0).astype(BF16), k_[n].astype(BF16), NT_DIMS)
        a.append(jnp.where(strict, kq[:CHUNK] * decay, 0.0))
        qk.append(jnp.where(lower, kq[CHUNK:] * decay, 0.0).astype(BF16))
        if n % B_HEADS == B_HEADS - 1:
            yield
    doff = [-jnp.where((ii >> 1) == (jj >> 1), a[n], 0.0) for n in range(n_items)]
    for lg in range(1, 6):
        m = ((ii >> (lg + 1)) == (jj >> (lg + 1))) & ((ii >> lg) != (jj >> lg))
        bm = [jnp.where(m, a[n], 0.0) for n in range(n_items)]
        db = [doff[n].astype(BF16) for n in range(n_items)]
        y = [bm[n] + _dot(db[n], bm[n].astype(BF16)) for n in range(n_items)]
        yield
        x = [y[n] + _dot(y[n].astype(BF16), db[n]) for n in range(n_items)]
        doff = [doff[n] - x[n] for n in range(n_items)]
        yield
    egc = [jnp.exp(gc[n]) for n in range(n_items)]
    g_last = [gc[n][CHUNK - 1:CHUNK, :] for n in range(n_items)]
    eg_last = [jnp.exp(g_last[n]) for n in range(n_items)]
    sol, kd, wq = [], [], []
    for n in range(n_items):
        rhs = jnp.concatenate([v_[n] * bh[n], kb[n] * egc[n]], axis=-1)
        sol.append(rhs + _dot(doff[n].astype(BF16), rhs.astype(BF16)))
        kd.append((k_[n] * jnp.exp(g_last[n] - gc[n])).astype(BF16))
        wq.append(jnp.concatenate([sol[n][:, B_DV:], q_[n] * egc[n]], axis=0).astype(BF16))
        if n % B_HEADS == B_HEADS - 1:
            yield

    def gate_store(c, o):
        for h in range(B_HEADS):
            z = env['zb'][rs(c), h * B_DV:(h + 1) * B_DV]
            mix_scr[rs(c), A_WIDTH + h * B_DV:A_WIDTH + (h + 1) * B_DV] = (
                _rms(o[h], onorm_ref[...]) * _silu(z)).astype(BF16)

    s_cur = [s_scr[h] for h in range(B_HEADS)]
    o_prev = None
    for c in range(nc):
        idx = [c * B_HEADS + h for h in range(B_HEADS)]
        ws = [_dot(wq[n], s_cur[h].astype(BF16)) for h, n in enumerate(idx)]
        if o_prev is not None:
            gate_store(c - 1, o_prev)
        yield
        v_new = [sol[n][:, :B_DV] - ws[h][:CHUNK] for h, n in enumerate(idx)]
        vb = [v_new[h].astype(BF16) for h in range(B_HEADS)]
        o_prev = [ws[h][CHUNK:] + _dot(qk[n], vb[h]) for h, n in enumerate(idx)]
        s_cur = [s_cur[h] * eg_last[n] + _dotg(kd[n], vb[h], TN_DIMS) for h, n in enumerate(idx)]
        yield
    for h in range(B_HEADS):
        s_scr[h] = s_cur[h]
    gate_store(nc - 1, o_prev)
    yield


def _ab_layer_staged_kernel(x_ref, g_ref, w_ref, wg_ref, wgt_ref, sink_ref, hist_ref, convw_ref, gprm_ref,
                            gprmt_ref, onorm_ref, wout_ref, convin_ref, sin_ref,
                            y_ref, kvtail_ref, xbtail_ref, sout_ref,
                            s_scr, xc_scr, kvprev_scr, mix_scr, *, ct, n_lead, pos_shift):
    i = pl.program_id(1)
    nsteps = pl.num_programs(1)
    ntail = min(WINDOW, ct)

    @pl.when(i == 0)
    def _():
        s_scr[...] = sin_ref[0]
        xc_scr[0:8, :] = convin_ref[0]
        kvprev_scr[1] = hist_ref[...]

    x = x_ref[0]
    hb = _rms(x, g_ref[...]).astype(BF16)
    proj = lambda a, b: _dot(hb, w_ref[:, a:b])
    env = {}

    def step(gen, n=1):
        for _ in range(n):
            next(gen, None)

    nblk = B_CONV_CH // LANES
    w = convw_ref[...]
    cvb = []

    def conv_block(j):
        cs = slice(j * LANES, (j + 1) * LANES)
        yc = xc_scr[pl.ds(8 - (CONV_W - 1), ct), cs] * w[0:1, cs]
        for t in range(1, CONV_W):
            yc = yc + xc_scr[pl.ds(8 - (CONV_W - 1) + t, ct), cs] * w[t:t + 1, cs]
        cvb.append(_silu(yc))

    pw = 2 * LANES
    for j in range(nblk // 2):
        xbj = proj(O_XB + j * pw, O_XB + (j + 1) * pw)
        xc_scr[8:8 + ct, j * pw:(j + 1) * pw] = xbj
        xbtail_ref[0, :, j * pw:(j + 1) * pw] = xbj[ct - 8:]
        if j > 0:
            conv_block(2 * j - 2)
            conv_block(2 * j - 1)
    gt = _dot(hb, wg_ref[...])
    gtt = _dotg(wgt_ref[...], hb, NT_DIMS)
    conv_block(nblk - 2)
    conv_block(nblk - 1)
    xc_scr[0:8, :] = xc_scr[ct:ct + 8, :]

    gdn = _gdn_stages(i, cvb, gt, gtt, env, gprm_ref, gprmt_ref, onorm_ref, s_scr, mix_scr, ct=ct, n_lead=n_lead)
    qa_p = []
    for j in range(A_WIDTH // pw):
        qa_p.append(proj(O_QA + j * pw, O_QA + (j + 1) * pw))
        step(gdn, 2)
    qa = jnp.concatenate(qa_p, axis=-1)
    kv = proj(O_KV, O_ZA)
    kvtail_ref[0] = kv[ct - ntail:]
    slot = lax.rem(i, 2)
    kvb = jnp.concatenate([kvprev_scr[1 - slot], kv], axis=0).astype(BF16)
    if ct >= WINDOW:
        kvprev_scr[slot] = kv[ct - WINDOW:]
    swa = _swa_stages(i, qa, kvb, env, sink_ref, mix_scr, ct=ct, sb=ntail, pos_shift=pos_shift)
    n_sw = (ct // ntail) * A_KV_HEADS
    step(gdn)
    for _ in range(ct // CHUNK):
        step(gdn)
        step(swa)
    step(swa, max(n_sw - ct // CHUNK, 0))
    za_p, zb_p = [], []
    fill = ([lambda: step(swa)] * (2 * n_sw)
            + [lambda j=j: za_p.append(proj(O_ZA + j * pw, O_ZA + (j + 1) * pw)) for j in range(A_WIDTH // pw)]
            + [lambda j=j: zb_p.append(proj(O_ZB + j * pw, O_ZB + (j + 1) * pw)) for j in range(B_VAL // pw)])
    per = -(-len(fill) // 10)
    for lv in range(10):
        step(gdn)
        for f in fill[lv * per:(lv + 1) * per]:
            f()
    for f in fill[10 * per:]:
        f()
    env['za'] = jnp.concatenate(za_p, axis=-1)
    env['zb'] = jnp.concatenate(zb_p, axis=-1)
    step(gdn, ct // CHUNK)
    step(swa, ct // ntail)
    ya = x + _dot(mix_scr[:, 0:A_WIDTH], wout_ref[0:A_WIDTH, :])
    for _ in gdn:
        pass
    y_ref[0] = ya + _dot(mix_scr[:, A_WIDTH:], wout_ref[A_WIDTH:, :])

    @pl.when(i == nsteps - 1)
    def _():
        sout_ref[0] = s_scr[...]


def _ab_layer_prompt(x3, norm_g, w_main, w_gate, w_gate_t, sink, hist_kv, conv_w8, gprm, gprm_t, onorm,
                     w_out, conv_in, s_in, ct, n_lead, pos_shift):
    B, L, _ = x3.shape
    ntail = min(WINDOW, ct)
    kern = functools.partial(_ab_layer_staged_kernel, ct=ct, n_lead=n_lead, pos_shift=pos_shift)
    full = lambda a: pl.BlockSpec(a.shape, lambda b, i: (0,) * a.ndim)
    blk = pl.BlockSpec((1, ct, D_MODEL), lambda b, i: (b, i, 0))
    per_b = lambda *s: pl.BlockSpec((1,) + s, lambda b, i: (b,) + (0,) * len(s))
    return pl.pallas_call(
        kern,
        grid=(B, L // ct),
        in_specs=[blk, full(norm_g), full(w_main), full(w_gate), full(w_gate_t), full(sink), full(hist_kv),
                  full(conv_w8), full(gprm), full(gprm_t), full(onorm), full(w_out),
                  per_b(8, B_CONV_CH), per_b(B_HEADS, B_DK, B_DV)],
        out_specs=[blk, per_b(ntail, 2 * A_KV_WIDTH), per_b(8, B_CONV_CH), per_b(B_HEADS, B_DK, B_DV)],
        out_shape=[jax.ShapeDtypeStruct((B, L, D_MODEL), F32),
                   jax.ShapeDtypeStruct((B, ntail, 2 * A_KV_WIDTH), F32),
                   jax.ShapeDtypeStruct((B, 8, B_CONV_CH), F32),
                   jax.ShapeDtypeStruct((B, B_HEADS, B_DK, B_DV), F32)],
        scratch_shapes=[pltpu.VMEM((B_HEADS, B_DK, B_DV), F32),
                        pltpu.VMEM((ct + 8, B_CONV_CH), F32),
                        pltpu.VMEM((2, WINDOW, 2 * A_KV_WIDTH), F32),
                        pltpu.VMEM((ct, AB_MIX), BF16)],
        compiler_params=_cparams(("arbitrary", "arbitrary")),
        name="ab_layer_prompt",
    )(x3, norm_g, w_main, w_gate, w_gate_t, sink, hist_kv, conv_w8, gprm, gprm_t, onorm, w_out, conv_in, s_in)


def _c_layer_kernel(x_ref, g_ref, w_ref, wgk_ref, wup_ref, bgk_ref, onorm_ref, wout_ref, gfin_ref,
                    sin_ref, y_ref, sout_ref, st_scr, og_scr, *, ct, n_lead):
    i = pl.program_id(1)
    nsteps = pl.num_programs(1)

    @pl.when(i == 0)
    def _():
        st_scr[...] = sin_ref[0]

    x = x_ref[0]
    hb = _rms(x, g_ref[...]).astype(BF16)
    low = _dot(hb, wgk_ref[...])
    q_all = _dot(hb, w_ref[:, 0:C_KEY])
    la_all = _gla_log_decay(low, wup_ref, bgk_ref)
    k_all = _dot(hb, w_ref[:, C_KEY:2 * C_KEY])
    v_h, z_h = [], []
    later = ([lambda h=h: v_h.append(_dot(hb, w_ref[:, 2 * C_KEY + h * C_DV:2 * C_KEY + (h + 1) * C_DV]).astype(BF16))
              for h in range(C_HEADS)]
             + [lambda h=h: z_h.append(_dot(hb, w_ref[:, 2 * C_KEY + C_VAL + h * C_DV:
                                                       2 * C_KEY + C_VAL + (h + 1) * C_DV]))
                for h in range(C_HEADS)])

    def run_later(n):
        for _ in range(n):
            if later:
                later.pop(0)()
    rows = i * ct + lax.broadcasted_iota(jnp.int32, la_all.shape, 0)
    la_all = jnp.where(rows >= n_lead, la_all, 0.0)

    ii = lax.broadcasted_iota(jnp.int32, (CHUNK, CHUNK), 0)
    jj = lax.broadcasted_iota(jnp.int32, (CHUNK, CHUNK), 1)
    lower = ii >= jj
    ltri = lower.astype(BF16)
    nc = ct // CHUNK
    rs = lambda c: slice(c * CHUNK, (c + 1) * CHUNK)
    ks = lambda h: slice(h * C_DK, (h + 1) * C_DK)
    vs = lambda h: slice(h * C_DV, (h + 1) * C_DV)
    items = [(c, h) for c in range(nc) for h in range(C_HEADS)]
    n_items = len(items)
    bc_all = [sum(_dot(ltri, p) for p in _split3(la_all[rs(c)])) for c in range(nc)]
    run_later(2)
    every =max(nc // C_HEADS, 1)
    qd, kinv, kd, gl, qk = [], [], [], [], []
    for n, (c, h) in enumerate(items):
        bc = bc_all[c][:, ks(h)]
        k = k_all[rs(c), ks(h)]
        bl = bc[CHUNK - 1:CHUNK, :]
        qd.append((q_all[rs(c), ks(h)] * (C_DK ** -0.5) * jnp.exp(bc)).astype(BF16))
        kinv.append((k * jnp.exp(-bc)).astype(BF16))
        kd.append((k * jnp.exp(bl - bc)).astype(BF16))
        gl.append(jnp.exp(bl))
        if h == C_HEADS - 1:
            qk += [jnp.where(lower, _dotg(qd[m], kinv[m], NT_DIMS), 0.0).astype(BF16)
                   for m in range(n - C_HEADS + 1, n + 1)]
            if c % every == every - 1:
                run_later(1)
    run_later(C_HEADS - len(v_h))
    v_ = [v_h[h][rs(c)] for c, h in items]
    o_intra = []
    for c in range(nc):
        o_intra += [_dot(qk[c * C_HEADS + h], v_[c * C_HEADS + h]) for h in range(C_HEADS)]
        if c % every == every - 1:
            run_later(1)
    run_later(len(later))

    rb = min(ct, 4 * CHUNK)

    def gate_store(c, o):
        for h in range(C_HEADS):
            og_scr[rs(c), vs(h)] = (_rms(o[h], onorm_ref[...]) * _silu(z_h[h][rs(c)])).astype(BF16)
        if ((c + 1) * CHUNK) % rb == 0:
            r = slice((c + 1) * CHUNK - rb, (c + 1) * CHUNK)
            y_ref[0, r, :] = _rms(x[r] + _dot(og_scr[r, :], wout_ref[...]), gfin_ref[...])

    st = [st_scr[h] for h in range(C_HEADS)]
    o_prev = None
    for c in range(nc):
        idx = [c * C_HEADS + h for h in range(C_HEADS)]
        o = [o_intra[n] + _dotg(qd[n], st[h].astype(BF16), NT_DIMS) for h, n in enumerate(idx)]
        st = [st[h] * gl[n] + _dotg(v_[n], kd[n], TN_DIMS) for h, n in enumerate(idx)]
        if o_prev is not None:
            gate_store(c - 1, o_prev)
        o_prev = o
    gate_store(nc - 1, o_prev)
    for h in range(C_HEADS):
        st_scr[h] = st[h]

    @pl.when(i == nsteps - 1)
    def _():
        sout_ref[0] = st_scr[...]


def _c_layer_prompt(x3, norm_g, w_main, w_gk, w_up, b_gk, onorm, w_out, final_g, st_in, ct, n_lead):
    B, L, _ = x3.shape
    kern = functools.partial(_c_layer_kernel, ct=ct, n_lead=n_lead)
    full = lambda a: pl.BlockSpec(a.shape, lambda b, i: (0,) * a.ndim)
    blk = pl.BlockSpec((1, ct, D_MODEL), lambda b, i: (b, i, 0))
    st = pl.BlockSpec((1, C_HEADS, C_DV, C_DK), lambda b, i: (b, 0, 0, 0))
    return pl.pallas_call(
        kern,
        grid=(B, L // ct),
        in_specs=[blk, full(norm_g), full(w_main), full(w_gk), full(w_up), full(b_gk), full(onorm),
                  full(w_out), full(final_g), st],
        out_specs=[blk, st],
        out_shape=[jax.ShapeDtypeStruct((B, L, D_MODEL), F32),
                   jax.ShapeDtypeStruct((B, C_HEADS, C_DV, C_DK), F32)],
        scratch_shapes=[pltpu.VMEM((C_HEADS, C_DV, C_DK), F32), pltpu.VMEM((ct, C_VAL), BF16)],
        compiler_params=_cparams(("arbitrary", "arbitrary")),
        name="c_layer_prompt",
    )(x3, norm_g, w_main, w_gk, w_up, b_gk, onorm, w_out, final_g, st_in)


def _ab_proj_kernel(x_ref, g_ref, w_ref, wg_ref, qa_ref, kv_ref, za_ref, xb_ref, zb_ref, gates_ref):
    hb = _rms(x_ref[...], g_ref[...]).astype(BF16)
    qa_ref[...] = _dot(hb, w_ref[:, O_QA:O_KV])
    kv_ref[...] = _dot(hb, w_ref[:, O_KV:O_ZA])
    za_ref[...] = _dot(hb, w_ref[:, O_ZA:O_XB])
    xb_ref[...] = _dot(hb, w_ref[:, O_XB:O_ZB])
    zb_ref[...] = _dot(hb, w_ref[:, O_ZB:AB_MAIN])
    gates_ref[...] = _dot(hb, wg_ref[...])


def _ab_proj(x2d, norm_g, w_main, w_gate):
    n = x2d.shape[0]
    full = lambda a: pl.BlockSpec(a.shape, lambda: (0,) * a.ndim)
    widths = (A_WIDTH, 2 * A_KV_WIDTH, A_WIDTH, B_CONV_CH, B_VAL, LANES)
    args = (x2d, norm_g, w_main, w_gate)
    return pl.pallas_call(
        _ab_proj_kernel,
        in_specs=[full(a) for a in args],
        out_specs=[pl.BlockSpec((n, w), lambda: (0, 0)) for w in widths],
        out_shape=[jax.ShapeDtypeStruct((n, w), F32) for w in widths],
        compiler_params=pltpu.CompilerParams(vmem_limit_bytes=VMEM_LIMIT),
        name="ab_in_proj_sample",
    )(*args)


def _swa_sample_kernel(q_ref, kvn_ref, za_ref, ck_ref, cv_ref, sink_ref, o_ref, nk_ref, nv_ref, *, bt):
    row = lax.broadcasted_iota(jnp.int32, (WINDOW, A_KV_WIDTH), 0)
    hrow = lax.broadcasted_iota(jnp.int32, (A_HEADS, A_HD), 0)
    sk = sink_ref[...]
    nkb, nvb = [], []
    for b in range(bt):
        nk = jnp.where(row == WINDOW - 1, kvn_ref[b:b + 1, 0:A_KV_WIDTH], pltpu.roll(ck_ref[b], WINDOW - 1, 0))
        nv = jnp.where(row == WINDOW - 1, kvn_ref[b:b + 1, A_KV_WIDTH:], pltpu.roll(cv_ref[b], WINDOW - 1, 0))
        nk_ref[b] = nk
        nv_ref[b] = nv
        nkb.append(nk.astype(BF16))
        nvb.append(nv.astype(BF16))
    items = [(b, j) for b in range(bt) for j in range(A_KV_HEADS)]
    q8 = [q_ref[b].astype(BF16) for b in range(bt)]
    sc = [_dotg(q8[b], nkb[b][:, j * A_HD:(j + 1) * A_HD], NT_DIMS) * (A_HD ** -0.5) for b, j in items]
    m = [jnp.maximum(jnp.max(s, axis=-1, keepdims=True), sk) for s in sc]
    p = [jnp.exp(sc[n] - m[n]) for n in range(len(items))]
    den = [jnp.sum(p[n], axis=-1, keepdims=True) + jnp.exp(sk - m[n]) for n in range(len(items))]
    oj = [_dot(p[n].astype(BF16), nvb[b][:, j * A_HD:(j + 1) * A_HD]) / den[n] for n, (b, j) in enumerate(items)]
    for b in range(bt):
        o8 = jnp.where(hrow >= A_GROUP, oj[b * A_KV_HEADS + 1], oj[b * A_KV_HEADS])
        o_ref[b] = o8 * _silu(za_ref[b])


def _swa_sample(q3, kv_new, za3, cache_k, cache_v, sink_col, bt):
    nb = q3.shape[0]
    hd = pl.BlockSpec((bt, A_HEADS, A_HD), lambda i: (i, 0, 0))
    cache = pl.BlockSpec((bt, WINDOW, A_KV_WIDTH), lambda i: (i, 0, 0))
    return pl.pallas_call(
        functools.partial(_swa_sample_kernel, bt=bt),
        grid=(nb // bt,),
        in_specs=[hd, pl.BlockSpec((bt, 2 * A_KV_WIDTH), lambda i: (i, 0)), hd, cache, cache,
                  pl.BlockSpec(sink_col.shape, lambda i: (0, 0))],
        out_specs=[hd, cache, cache],
        out_shape=[jax.ShapeDtypeStruct(q3.shape, F32),
                   jax.ShapeDtypeStruct(cache_k.shape, F32),
                   jax.ShapeDtypeStruct(cache_v.shape, F32)],
        compiler_params=_cparams(("arbitrary",)),
        name="swa_sample",
    )(q3, kv_new, za3, cache_k, cache_v, sink_col)


def _gdn_sample_prep_kernel(xb_ref, h0_ref, h1_ref, h2_ref, gates_ref, convw_ref, gprm_ref,
                            q_ref, k_ref, v_ref, bg_ref):
    w = convw_ref[...]
    y = (h0_ref[...] * w[0:1, :] + h1_ref[...] * w[1:2, :] + h2_ref[...] * w[2:3, :]
         + xb_ref[...] * w[3:4, :])
    cv = _silu(y)
    for h in range(B_HEADS):
        q = cv[:, h * B_DK:(h + 1) * B_DK]
        k = cv[:, B_KEY + h * B_DK:B_KEY + (h + 1) * B_DK]
        q_ref[:, h * B_DK:(h + 1) * B_DK] = (
            q * lax.rsqrt(jnp.sum(q * q, axis=-1, keepdims=True) + EPS) * (B_DK ** -0.5))
        k_ref[:, h * B_DK:(h + 1) * B_DK] = k * lax.rsqrt(jnp.sum(k * k, axis=-1, keepdims=True) + EPS)
    v_ref[...] = cv[:, 2 * B_KEY:]
    gt = gates_ref[...]
    col = lax.broadcasted_iota(jnp.int32, gt.shape, 1)
    g = -gprm_ref[0:1, :] * _softplus(gt + gprm_ref[1:2, :])
    bg_ref[...] = jnp.where(col < B_HEADS, _sigmoid(gt), jnp.exp(g))


def _gdn_sample_prep(xb, h0, h1, h2, gates, conv_w8, gprm):
    n = xb.shape[0]
    full = lambda a: pl.BlockSpec(a.shape, lambda: (0,) * a.ndim)
    args = (xb, h0, h1, h2, gates, conv_w8, gprm)
    return pl.pallas_call(
        _gdn_sample_prep_kernel,
        in_specs=[full(a) for a in args],
        out_specs=[pl.BlockSpec((n, B_KEY), lambda: (0, 0))] * 3 + [pl.BlockSpec((n, LANES), lambda: (0, 0))],
        out_shape=[jax.ShapeDtypeStruct((n, B_KEY), F32)] * 3 + [jax.ShapeDtypeStruct((n, LANES), F32)],
        compiler_params=pltpu.CompilerParams(vmem_limit_bytes=VMEM_LIMIT),
        name="gdn_sample_prep",
    )(*args)


def _gdn_sample_kernel(s_ref, qt_ref, kt_ref, v_ref, bg_ref, zb_ref, onorm_ref, o_ref, sout_ref, *, bt):
    items = [(b, h) for b in range(bt) for h in range(B_HEADS)]
    hs = lambda h: slice(h * B_DV, (h + 1) * B_DV)
    kc = [kt_ref[0, h, :, b:b + 1] for b, h in items]
    eg = [bg_ref[b:b + 1, B_HEADS + h:B_HEADS + h + 1] for b, h in items]
    ks = [jnp.sum(kc[n] * s_ref[b, h], axis=0, keepdims=True) for n, (b, h) in enumerate(items)]
    v_new = [bg_ref[b:b + 1, h:h + 1] * (v_ref[b:b + 1, hs(h)] - eg[n] * ks[n]) for n, (b, h) in enumerate(items)]
    o = []
    for n, (b, h) in enumerate(items):
        s_new = eg[n] * s_ref[b, h] + kc[n] * v_new[n]
        sout_ref[b, h] = s_new
        o.append(jnp.sum(qt_ref[0, h, :, b:b + 1] * s_new, axis=0, keepdims=True))
    o_blk = jnp.concatenate([jnp.concatenate([o[b * B_HEADS + h] for h in range(B_HEADS)], axis=-1)
                             for b in range(bt)], axis=0)
    for h in range(B_HEADS):
        o_ref[:, hs(h)] = _rms(o_blk[:, hs(h)], onorm_ref[...]) * _silu(zb_ref[:, hs(h)])


def _gdn_sample(state, qt, kt, v, bg, zb, onorm, bt):
    nb = state.shape[0]
    st = pl.BlockSpec((bt, B_HEADS, B_DK, B_DV), lambda i: (i, 0, 0, 0))
    col = pl.BlockSpec((1, B_HEADS, B_DK, bt), lambda i: (i, 0, 0, 0))
    row = lambda n: pl.BlockSpec((bt, n), lambda i: (i, 0))
    return pl.pallas_call(
        functools.partial(_gdn_sample_kernel, bt=bt),
        grid=(nb // bt,),
        in_specs=[st, col, col, row(B_VAL), row(LANES), row(B_VAL),
                  pl.BlockSpec(onorm.shape, lambda i: (0, 0))],
        out_specs=[row(B_VAL), st],
        out_shape=[jax.ShapeDtypeStruct((nb, B_VAL), F32), jax.ShapeDtypeStruct(state.shape, F32)],
        compiler_params=_cparams(("arbitrary",)),
        name="gdn_sample",
    )(state, qt, kt, v, bg, zb, onorm)


def _ab_out_c_proj_kernel(x_ref, oa_ref, ob_ref, wout_ref, g_ref, w_ref, wgk_ref, wup_ref, bgk_ref,
                          y_ref, q_ref, k_ref, v_ref, z_ref, la_ref):
    y = (x_ref[...] + _dot(oa_ref[...].astype(BF16), wout_ref[0:A_WIDTH, :])
         + _dot(ob_ref[...].astype(BF16), wout_ref[A_WIDTH:, :]))
    y_ref[...] = y
    hb = _rms(y, g_ref[...]).astype(BF16)
    q_ref[...] = _dot(hb, w_ref[:, 0:C_KEY])
    k_ref[...] = _dot(hb, w_ref[:, C_KEY:2 * C_KEY])
    v_ref[...] = _dot(hb, w_ref[:, 2 * C_KEY:2 * C_KEY + C_VAL])
    z_ref[...] = _dot(hb, w_ref[:, 2 * C_KEY + C_VAL:C_MAIN])
    la_ref[...] = _gla_log_decay(_dot(hb, wgk_ref[...]), wup_ref, bgk_ref)


def _ab_out_c_proj(x2d, oa, ob, w_out, norm_g, w_main, w_gk, w_up, b_gk):
    n = x2d.shape[0]
    full = lambda a: pl.BlockSpec(a.shape, lambda: (0,) * a.ndim)
    widths = (D_MODEL, C_KEY, C_KEY, C_VAL, C_VAL, C_KEY)
    args = (x2d, oa, ob, w_out, norm_g, w_main, w_gk, w_up, b_gk)
    return pl.pallas_call(
        _ab_out_c_proj_kernel,
        in_specs=[full(a) for a in args],
        out_specs=[pl.BlockSpec((n, w), lambda: (0, 0)) for w in widths],
        out_shape=[jax.ShapeDtypeStruct((n, w), F32) for w in widths],
        compiler_params=pltpu.CompilerParams(vmem_limit_bytes=VMEM_LIMIT),
        name="ab_out_c_in_proj_sample",
    )(*args)


def _gla_sample_kernel(s_ref, qt_ref, kt_ref, lat_ref, v_ref, z_ref, onorm_ref, o_ref, sout_ref, *, bt):
    items = [(b, h) for b in range(bt) for h in range(C_HEADS)]
    hs = lambda h: slice(h * C_DV, (h + 1) * C_DV)
    ac = [jnp.exp(lat_ref[0, h, :, b:b + 1]) for b, h in items]
    o = []
    for n, (b, h) in enumerate(items):
        s_new = ac[n] * s_ref[b, h] + kt_ref[0, h, :, b:b + 1] * v_ref[b:b + 1, hs(h)]
        sout_ref[b, h] = s_new
        o.append(jnp.sum((qt_ref[0, h, :, b:b + 1] * (C_DK ** -0.5)) * s_new, axis=0, keepdims=True))
    o_blk = jnp.concatenate([jnp.concatenate([o[b * C_HEADS + h] for h in range(C_HEADS)], axis=-1)
                             for b in range(bt)], axis=0)
    for h in range(C_HEADS):
        o_ref[:, hs(h)] = _rms(o_blk[:, hs(h)], onorm_ref[...]) * _silu(z_ref[:, hs(h)])


def _gla_sample(state, qt, kt, lat, v, z, onorm, bt):
    nb = state.shape[0]
    st = pl.BlockSpec((bt, C_HEADS, C_DK, C_DV), lambda i: (i, 0, 0, 0))
    col = pl.BlockSpec((1, C_HEADS, C_DK, bt), lambda i: (i, 0, 0, 0))
    row = lambda n: pl.BlockSpec((bt, n), lambda i: (i, 0))
    return pl.pallas_call(
        functools.partial(_gla_sample_kernel, bt=bt),
        grid=(nb // bt,),
        in_specs=[st, col, col, col, row(C_VAL), row(C_VAL),
                  pl.BlockSpec(onorm.shape, lambda i: (0, 0))],
        out_specs=[row(C_VAL), st],
        out_shape=[jax.ShapeDtypeStruct((nb, C_VAL), F32), jax.ShapeDtypeStruct(state.shape, F32)],
        compiler_params=_cparams(("arbitrary",)),
        name="gla_sample",
    )(state, qt, kt, lat, v, z, onorm)


def _c_out_kernel(x_ref, o_ref, w_ref, g_ref, y_ref):
    y_ref[...] = _rms(x_ref[...] + _dot(o_ref[...].astype(BF16), w_ref[...]), g_ref[...])


def _c_out(x2d, oc, w_out, final_g):
    n = x2d.shape[0]
    full = lambda a: pl.BlockSpec(a.shape, lambda: (0,) * a.ndim)
    args = (x2d, oc, w_out, final_g)
    return pl.pallas_call(
        _c_out_kernel,
        in_specs=[full(a) for a in args],
        out_specs=pl.BlockSpec((n, D_MODEL), lambda: (0, 0)),
        out_shape=jax.ShapeDtypeStruct((n, D_MODEL), F32),
        compiler_params=pltpu.CompilerParams(vmem_limit_bytes=VMEM_LIMIT),
        name="c_out_proj_norm_sample",
    )(*args)


def _to_cols(a, heads, dk, bt):
    n = a.shape[0]
    return a.reshape(n // bt, bt, heads, dk).transpose(0, 2, 3, 1)


def kernel(x_prompt, x_sample, cache_swa_k, cache_swa_v, state_dn_conv, state_dn, state_gla,
           meta_tokens, norm_ab, w_in_ab, sink_a, conv_b, a_log_b, dt_bias_b, onorm_b, w_out_ab,
           norm_c, w_in_c, w_gk_up, b_gk, onorm_c, w_out_c, final_norm):
    Bp, L, _ = x_prompt.shape
    Bs = x_sample.shape[0]
    CT, BT = 256, 8

    w_ab = w_in_ab[0]
    w_ab_main = w_ab.astype(BF16)
    w_ab_gate = jnp.pad(w_ab[:, AB_MAIN:], ((0, 0), (0, LANES - 2 * B_HEADS))).astype(BF16)
    w_ab_gate_t = jnp.pad(w_ab[:, AB_MAIN:].T, ((0, 16 - 2 * B_HEADS), (0, 0))).astype(BF16)
    g_ab = norm_ab[0][None, :]
    sink_row = sink_a[0][None, :]
    sink_col = sink_a[0][:, None]
    conv_w8 = jnp.pad(conv_b[0], ((0, 8 - CONV_W), (0, 0)))
    ea = jnp.exp(a_log_b[0])
    gprm = jnp.zeros((8, LANES), F32).at[0, B_HEADS:2 * B_HEADS].set(ea).at[1, B_HEADS:2 * B_HEADS].set(dt_bias_b[0])
    gprm_t = jnp.zeros((16, LANES), F32).at[B_HEADS:2 * B_HEADS, 0].set(ea).at[B_HEADS:2 * B_HEADS, 1].set(dt_bias_b[0])
    onb = onorm_b[0][None, :]
    w_ab_out = w_out_ab[0].astype(BF16)
    w_c = w_in_c[0]
    w_c_main = w_c.astype(BF16)
    w_c_gk = jnp.pad(w_c[:, C_MAIN:], ((0, 0), (0, LANES - C_RANK))).astype(BF16)
    w_up = jnp.pad(w_gk_up[0], ((0, LANES - C_RANK), (0, 0)))
    bgk = b_gk[0][None, :]
    g_c = norm_c[0][None, :]
    onc = onorm_c[0][None, :]
    w_c_out = w_out_c[0].astype(BF16)
    g_fin = final_norm[None, :]

    def ab_layer(x3, ct, hist_kv, pos_shift, conv_in, s_in, n_lead):
        return _ab_layer_prompt(x3, g_ab, w_ab_main, w_ab_gate, w_ab_gate_t, sink_row, hist_kv, conv_w8,
                                gprm, gprm_t, onb, w_ab_out, conv_in, s_in, ct, n_lead, pos_shift)

    def c_layer(x3, ct, st_in, n_lead):
        return _c_layer_prompt(x3, g_c, w_c_main, w_c_gk, w_up, bgk, onc, w_c_out, g_fin, st_in, ct, n_lead)

    xpre = jnp.concatenate([jnp.zeros((LEAD, D_MODEL), F32), meta_tokens], axis=0)[None]
    y_pre, kv_pre, xbtail_pre, sdn_pre = ab_layer(
        xpre, CHUNK, jnp.zeros((WINDOW, 2 * A_KV_WIDTH), F32), -LEAD,
        jnp.zeros((1, 8, B_CONV_CH), F32), jnp.zeros((1, B_HEADS, B_DK, B_DV), F32), LEAD)
    _, sgla_pre = c_layer(y_pre, CHUNK, jnp.zeros((1, C_HEADS, C_DV, C_DK), F32), LEAD)

    hist_kv = jnp.concatenate([jnp.zeros((WINDOW - CHUNK, 2 * A_KV_WIDTH), F32), kv_pre[0]], axis=0)
    y1, kv_tail, xb_tail, sdn_p = ab_layer(
        x_prompt, CT, hist_kv, N_META, jnp.broadcast_to(xbtail_pre, (Bp, 8, B_CONV_CH)),
        jnp.broadcast_to(sdn_pre, (Bp, B_HEADS, B_DK, B_DV)), 0)
    y_prompt, sgla_t = c_layer(y1, 2 * CT, jnp.broadcast_to(sgla_pre, (Bp, C_HEADS, C_DV, C_DK)), 0)
    sgla_p = jnp.swapaxes(sgla_t, 2, 3)

    swa_k_p = kv_tail[:, :, :A_KV_WIDTH].reshape(1, Bp, WINDOW, A_KV_HEADS, A_HD)
    swa_v_p = kv_tail[:, :, A_KV_WIDTH:].reshape(1, Bp, WINDOW, A_KV_HEADS, A_HD)
    dn_conv_p = xb_tail[:, 8 - (CONV_W - 1):, :][None]

    xs = x_sample.reshape(Bs, D_MODEL)
    qa, kvn, za, xb, zb, gates = _ab_proj(xs, g_ab, w_ab_main, w_ab_gate)
    oa3, nk, nv = _swa_sample(qa.reshape(Bs, A_HEADS, A_HD), kvn, za.reshape(Bs, A_HEADS, A_HD),
                              cache_swa_k[0].reshape(Bs, WINDOW, A_KV_WIDTH),
                              cache_swa_v[0].reshape(Bs, WINDOW, A_KV_WIDTH), sink_col, BT)
    hist = state_dn_conv[0]
    qn, kn, vn, bg = _gdn_sample_prep(xb, hist[:, 0], hist[:, 1], hist[:, 2], gates, conv_w8, gprm)
    ob, sdn_s = _gdn_sample(state_dn[0], _to_cols(qn, B_HEADS, B_DK, BT), _to_cols(kn, B_HEADS, B_DK, BT),
                            vn, bg, zb, onb, BT)
    ys, qc, kc, vc, zc, la = _ab_out_c_proj(xs, oa3.reshape(Bs, A_WIDTH), ob, w_ab_out,
                                            g_c, w_c_main, w_c_gk, w_up, bgk)
    ocs, sgla_s = _gla_sample(state_gla[0], _to_cols(qc, C_HEADS, C_DK, BT), _to_cols(kc, C_HEADS, C_DK, BT),
                              _to_cols(la, C_HEADS, C_DK, BT), vc, zc, onc, BT)
    y_sample = _c_out(ys, ocs, w_c_out, g_fin).reshape(Bs, 1, D_MODEL)
    dn_conv_s = jnp.concatenate([hist[:, 1:], xb[:, None, :]], axis=1)[None]

    return (y_prompt, y_sample, swa_k_p, swa_v_p, dn_conv_p, sdn_p[None], sgla_p[None],
            nk.reshape(1, Bs, WINDOW, A_KV_HEADS, A_HD), nv.reshape(1, Bs, WINDOW, A_KV_HEADS, A_HD),
            dn_conv_s, sdn_s[None], sgla_s[None])
```

```python
import functools

import jax
import jax.numpy as jnp
from jax import lax
from jax.experimental import pallas as pl
from jax.experimental.pallas import tpu as pltpu

F32 = jnp.float32
BF16 = jnp.bfloat16
EPS = 1e-6

D_MODEL = 1024
N_META = 16
CHUNK = 64
LEAD = (-N_META) % CHUNK
A_HEADS, A_KV_HEADS, A_HD = 8, 2, 64
A_GROUP = A_HEADS // A_KV_HEADS
A_WIDTH = A_HEADS * A_HD
A_KV_WIDTH = A_KV_HEADS * A_HD
WINDOW = 128
B_HEADS, B_DK, B_DV = 4, 128, 128
B_KEY = B_HEADS * B_DK
B_VAL = B_HEADS * B_DV
CONV_W = 4
B_CONV_CH = 2 * B_KEY + B_VAL
C_HEADS, C_DK, C_DV = 4, 128, 256
C_KEY = C_HEADS * C_DK
C_VAL = C_HEADS * C_DV
C_RANK = 16
C_GATE_NORM = 16.0
AB_MAIN = 2 * A_WIDTH + 2 * A_KV_WIDTH + B_CONV_CH + B_VAL
AB_MIX = A_WIDTH + B_VAL
C_MAIN = 2 * C_KEY + 2 * C_VAL
O_QA, O_KV, O_ZA, O_XB, O_ZB = 0, 512, 768, 1280, 2816
LANES = 128
NEG = -1e30

VMEM_LIMIT = 56 * 1024 * 1024

NT_DIMS = (((1,), (1,)), ((), ()))
TN_DIMS = (((0,), (0,)), ((), ()))


def _cparams(sem, **kw):
    return pltpu.CompilerParams(dimension_semantics=sem, vmem_limit_bytes=VMEM_LIMIT, **kw)


def _dot(a, b):
    return jnp.dot(a, b, preferred_element_type=F32)


def _dotg(a, b, dims):
    return lax.dot_general(a, b, dims, preferred_element_type=F32)


def _split3(a):
    hi = a.astype(BF16)
    r = a - hi.astype(F32)
    mid = r.astype(BF16)
    lo = (r - mid.astype(F32)).astype(BF16)
    return hi, mid, lo


def _sigmoid(x):
    return 1.0 / (1.0 + jnp.exp(-x))


def _silu(x):
    return x * _sigmoid(x)


def _softplus(x):
    return jnp.maximum(x, 0.0) + jnp.log1p(jnp.exp(-jnp.abs(x)))


def _log_sigmoid(x):
    return jnp.minimum(x, 0.0) - jnp.log1p(jnp.exp(-jnp.abs(x)))


def _rms(x, g):
    return x * lax.rsqrt(jnp.mean(x * x, axis=-1, keepdims=True) + EPS) * g


def _gla_log_decay(low, wup_ref, bgk_ref):
    lh, lm, _ = _split3(low)
    wh, wm, _ = _split3(wup_ref[...])
    up = _dot(lh, wh) + _dot(lh, wm) + _dot(lm, wh)
    return _log_sigmoid(up + bgk_ref[...]) * (1.0 / C_GATE_NORM)


def _swa_tile(i, qa, kvb, za, sink_ref, mix_scr, *, ct, sb, pos_shift):
    r = lax.broadcasted_iota(jnp.int32, (sb, WINDOW + sb), 0)
    c = lax.broadcasted_iota(jnp.int32, (sb, WINDOW + sb), 1)
    diff = r - c + WINDOW
    in_window = (diff >= 0) & (diff < WINDOW)
    nsb = ct // sb
    items = [(s, j) for s in range(nsb) for j in range(A_KV_HEADS)]
    masks = [jnp.concatenate([in_window & (i * ct + s * sb + c - WINDOW + pos_shift >= 0)] * A_GROUP, axis=0)
             for s in range(nsb)]
    sinks = [jnp.concatenate([jnp.broadcast_to(sink_ref[0:1, j * A_GROUP + g:j * A_GROUP + g + 1], (sb, 1))
                              for g in range(A_GROUP)], axis=0) for j in range(A_KV_HEADS)]
    qs = [jnp.concatenate([qa[s * sb:(s + 1) * sb, (j * A_GROUP + g) * A_HD:(j * A_GROUP + g + 1) * A_HD]
                           for g in range(A_GROUP)], axis=0).astype(BF16) for s, j in items]
    sc = [_dotg(qs[n], kvb[s * sb:s * sb + WINDOW + sb, j * A_HD:(j + 1) * A_HD], NT_DIMS) * (A_HD ** -0.5)
          for n, (s, j) in enumerate(items)]
    sc = [jnp.where(masks[s], sc[n], NEG) for n, (s, j) in enumerate(items)]
    mx = [jnp.maximum(jnp.max(sc[n], axis=-1, keepdims=True), sinks[j]) for n, (s, j) in enumerate(items)]
    p = [jnp.exp(sc[n] - mx[n]) for n in range(len(items))]
    den = [jnp.sum(p[n], axis=-1, keepdims=True) + jnp.exp(sinks[j] - mx[n]) for n, (s, j) in enumerate(items)]
    pv = [_dot(p[n].astype(BF16),
               kvb[s * sb:s * sb + WINDOW + sb, A_KV_WIDTH + j * A_HD:A_KV_WIDTH + (j + 1) * A_HD]) / den[n]
          for n, (s, j) in enumerate(items)]
    for s in range(nsb):
        o = jnp.concatenate([pv[s * A_KV_HEADS + j][g * sb:(g + 1) * sb]
                             for j in range(A_KV_HEADS) for g in range(A_GROUP)], axis=-1)
        mix_scr[s * sb:(s + 1) * sb, 0:A_WIDTH] = (o * _silu(za[s * sb:(s + 1) * sb])).astype(BF16)


def _gdn_tile(i, cv, gt, gtt, zb, gprm_ref, gprmt_ref, onorm_ref, s_scr, mix_scr, *, ct, n_lead):
    beta = _sigmoid(gt)
    rows = i * ct + lax.broadcasted_iota(jnp.int32, gt.shape, 0)
    gcol = jnp.where(rows >= n_lead, -gprm_ref[0:1, :] * _softplus(gt + gprm_ref[1:2, :]), 0.0)
    lanes = i * ct + lax.broadcasted_iota(jnp.int32, gtt.shape, 1)
    grow = jnp.where(lanes >= n_lead, -gprmt_ref[:, 0:1] * _softplus(gtt + gprmt_ref[:, 1:2]), 0.0)

    ii = lax.broadcasted_iota(jnp.int32, (CHUNK, CHUNK), 0)
    jj = lax.broadcasted_iota(jnp.int32, (CHUNK, CHUNK), 1)
    lower = ii >= jj
    strict = ii > jj
    ltri = lower.astype(BF16)
    utri = (ii <= jj).astype(BF16)

    nc = ct // CHUNK
    items = [(c, h) for c in range(nc) for h in range(B_HEADS)]
    n_items = len(items)
    rs = lambda c: slice(c * CHUNK, (c + 1) * CHUNK)
    qn, kn = [], []
    for h in range(B_HEADS):
        q = cv[:, h * B_DK:(h + 1) * B_DK]
        k = cv[:, B_KEY + h * B_DK:B_KEY + (h + 1) * B_DK]
        qn.append(q * lax.rsqrt(jnp.sum(q * q, axis=-1, keepdims=True) + EPS) * (B_DK ** -0.5))
        kn.append(k * lax.rsqrt(jnp.sum(k * k, axis=-1, keepdims=True) + EPS))
    gcum_c = [sum(_dot(ltri, p) for p in _split3(gcol[rs(c)])) for c in range(nc)]
    gcum_r = [sum(_dot(p, utri) for p in _split3(grow[:, rs(c)])) for c in range(nc)]
    q_ = [qn[h][rs(c)] for c, h in items]
    k_ = [kn[h][rs(c)] for c, h in items]
    v_ = [cv[rs(c), 2 * B_KEY + h * B_DV:2 * B_KEY + (h + 1) * B_DV] for c, h in items]
    bh = [beta[rs(c), h:h + 1] for c, h in items]
    gc = [gcum_c[c][:, B_HEADS + h:B_HEADS + h + 1] for c, h in items]
    gr = [gcum_r[c][B_HEADS + h:B_HEADS + h + 1, :] for c, h in items]
    decay = [jnp.exp(jnp.where(lower, gc[n] - gr[n], NEG)) for n in range(n_items)]
    kb = [k_[n] * bh[n] for n in range(n_items)]
    kbf = [k_[n].astype(BF16) for n in range(n_items)]
    kq = [_dotg(jnp.concatenate([kb[n], q_[n]], axis=0).astype(BF16), kbf[n], NT_DIMS) for n in range(n_items)]
    a = [jnp.where(strict, kq[n][:CHUNK] * decay[n], 0.0) for n in range(n_items)]
    qk = [jnp.where(lower, kq[n][CHUNK:] * decay[n], 0.0).astype(BF16) for n in range(n_items)]
    doff = [-jnp.where((ii >> 1) == (jj >> 1), a[n], 0.0) for n in range(n_items)]
    for lg in range(1, 6):
        m = ((ii >> (lg + 1)) == (jj >> (lg + 1))) & ((ii >> lg) != (jj >> lg))
        bm = [jnp.where(m, a[n], 0.0) for n in range(n_items)]
        db = [doff[n].astype(BF16) for n in range(n_items)]
        y = [bm[n] + _dot(db[n], bm[n].astype(BF16)) for n in range(n_items)]
        x = [y[n] + _dot(y[n].astype(BF16), db[n]) for n in range(n_items)]
        doff = [doff[n] - x[n] for n in range(n_items)]
    egc = [jnp.exp(gc[n]) for n in range(n_items)]
    rhs = [jnp.concatenate([v_[n] * bh[n], kb[n] * egc[n]], axis=-1) for n in range(n_items)]
    sol = [rhs[n] + _dot(doff[n].astype(BF16), rhs[n].astype(BF16)) for n in range(n_items)]
    g_last = [gc[n][CHUNK - 1:CHUNK, :] for n in range(n_items)]
    kd = [(k_[n] * jnp.exp(g_last[n] - gc[n])).astype(BF16) for n in range(n_items)]
    wq = [jnp.concatenate([sol[n][:, B_DV:], q_[n] * egc[n]], axis=0).astype(BF16) for n in range(n_items)]
    eg_last = [jnp.exp(g_last[n]) for n in range(n_items)]

    s_cur = [s_scr[h] for h in range(B_HEADS)]
    for c in range(nc):
        idx = [c * B_HEADS + h for h in range(B_HEADS)]
        ws = [_dot(wq[n], s_cur[h].astype(BF16)) for h, n in enumerate(idx)]
        v_new = [sol[n][:, :B_DV] - ws[h][:CHUNK] for h, n in enumerate(idx)]
        vb = [v_new[h].astype(BF16) for h in range(B_HEADS)]
        o = [ws[h][CHUNK:] + _dot(qk[n], vb[h]) for h, n in enumerate(idx)]
        s_cur = [s_cur[h] * eg_last[n] + _dotg(kd[n], vb[h], TN_DIMS) for h, n in enumerate(idx)]
        for h in range(B_HEADS):
            z = zb[rs(c), h * B_DV:(h + 1) * B_DV]
            mix_scr[rs(c), A_WIDTH + h * B_DV:A_WIDTH + (h + 1) * B_DV] = (
                _rms(o[h], onorm_ref[...]) * _silu(z)).astype(BF16)
    for h in range(B_HEADS):
        s_scr[h] = s_cur[h]


def _ab_layer_kernel(x_ref, g_ref, w_ref, wg_ref, wgt_ref, sink_ref, hist_ref, convw_ref, gprm_ref,
                     gprmt_ref, onorm_ref, wout_ref, convin_ref, sin_ref,
                     y_ref, kvtail_ref, xbtail_ref, sout_ref,
                     s_scr, xc_scr, kvprev_scr, mix_scr, *, ct, n_lead, pos_shift):
    i = pl.program_id(1)
    nsteps = pl.num_programs(1)
    ntail = min(WINDOW, ct)

    @pl.when(i == 0)
    def _():
        s_scr[...] = sin_ref[0]
        xc_scr[0:8, :] = convin_ref[0]
        kvprev_scr[1] = hist_ref[...]

    x = x_ref[0]
    hb = _rms(x, g_ref[...]).astype(BF16)
    qa = _dot(hb, w_ref[:, O_QA:O_KV])
    kv = _dot(hb, w_ref[:, O_KV:O_ZA])
    za = _dot(hb, w_ref[:, O_ZA:O_XB])
    xb = _dot(hb, w_ref[:, O_XB:O_ZB])
    zb = _dot(hb, w_ref[:, O_ZB:AB_MAIN])
    gt = _dot(hb, wg_ref[...])
    gtt = _dotg(wgt_ref[...], hb, NT_DIMS)
    kvtail_ref[0] = kv[ct - ntail:]
    xbtail_ref[0] = xb[ct - 8:]

    slot = lax.rem(i, 2)
    kvb = jnp.concatenate([kvprev_scr[1 - slot], kv], axis=0).astype(BF16)
    if ct >= WINDOW:
        kvprev_scr[slot] = kv[ct - WINDOW:]
    _swa_tile(i, qa, kvb, za, sink_ref, mix_scr, ct=ct, sb=ntail, pos_shift=pos_shift)

    xc_scr[8:8 + ct, :] = xb
    w = convw_ref[...]
    yc = xc_scr[pl.ds(8 - (CONV_W - 1), ct), :] * w[0:1, :]
    for t in range(1, CONV_W):
        yc = yc + xc_scr[pl.ds(8 - (CONV_W - 1) + t, ct), :] * w[t:t + 1, :]
    cv = _silu(yc)
    xc_scr[0:8, :] = xc_scr[ct:ct + 8, :]
    _gdn_tile(i, cv, gt, gtt, zb, gprm_ref, gprmt_ref, onorm_ref, s_scr, mix_scr, ct=ct, n_lead=n_lead)

    y_ref[0] = x + _dot(mix_scr[...], wout_ref[...])

    @pl.when(i == nsteps - 1)
    def _():
        sout_ref[0] = s_scr[...]


def _swa_stages(i, qa, kvb, env, sink_ref, mix_scr, *, ct, sb, pos_shift):
    r = lax.broadcasted_iota(jnp.int32, (sb, WINDOW + sb), 0)
    c = lax.broadcasted_iota(jnp.int32, (sb, WINDOW + sb), 1)
    diff = r - c + WINDOW
    in_window = (diff >= 0) & (diff < WINDOW)
    nsb = ct // sb
    items = [(s, j) for s in range(nsb) for j in range(A_KV_HEADS)]
    masks = [jnp.concatenate([in_window & (i * ct + s * sb + c - WINDOW + pos_shift >= 0)] * A_GROUP, axis=0)
             for s in range(nsb)]
    sinks = [jnp.concatenate([jnp.broadcast_to(sink_ref[0:1, j * A_GROUP + g:j * A_GROUP + g + 1], (sb, 1))
                              for g in range(A_GROUP)], axis=0) for j in range(A_KV_HEADS)]
    sc = []
    for s, j in items:
        qs = jnp.concatenate([qa[s * sb:(s + 1) * sb, (j * A_GROUP + g) * A_HD:(j * A_GROUP + g + 1) * A_HD]
                              for g in range(A_GROUP)], axis=0).astype(BF16)
        sc.append(_dotg(qs, kvb[s * sb:s * sb + WINDOW + sb, j * A_HD:(j + 1) * A_HD], NT_DIMS) * (A_HD ** -0.5))
        yield
    p, esk = [], []
    for n, (s, j) in enumerate(items):
        scm = jnp.where(masks[s], sc[n], NEG)
        mx = jnp.maximum(jnp.max(scm, axis=-1, keepdims=True), sinks[j])
        p.append(jnp.exp(scm - mx).astype(BF16))
        esk.append(jnp.exp(sinks[j] - mx))
        yield
    pv = []
    ones = jnp.ones((WINDOW + sb, A_HD), BF16)
    for n, (s, j) in enumerate(items):
        vx = jnp.concatenate([kvb[s * sb:s * sb + WINDOW + sb,
                                  A_KV_WIDTH + j * A_HD:A_KV_WIDTH + (j + 1) * A_HD], ones], axis=-1)
        pvx = _dot(p[n], vx)
        pv.append(pvx[:, :A_HD] / (pvx[:, A_HD:] + esk[n]))
        yield
    env['pv'] = pv


def _swa_store(env, mix_scr, *, ct, sb):
    pv = env['pv']
    for s in range(ct // sb):
        o = jnp.concatenate([pv[s * A_KV_HEADS + j][g * sb:(g + 1) * sb]
                             for j in range(A_KV_HEADS) for g in range(A_GROUP)], axis=-1)
        mix_scr[s * sb:(s + 1) * sb, 0:A_WIDTH] = (o * _silu(env['za'][s * sb:(s + 1) * sb])).astype(BF16)


def _gdn_stages(i, cvb, gt, gtt, env, gprm_ref, gprmt_ref, onorm_ref, s_scr, mix_scr, *, ct, n_lead):
    nc = ct // CHUNK
    items = [(c, h) for c in range(nc) for h in range(B_HEADS)]
    n_items = len(items)
    rs = lambda c: slice(c * CHUNK, (c + 1) * CHUNK)
    qn, kn = [], []
    for h in range(B_HEADS):
        q = cvb[h]
        k = cvb[B_HEADS + h]
        qn.append(q * lax.rsqrt(jnp.sum(q * q, axis=-1, keepdims=True) + EPS) * (B_DK ** -0.5))
        kn.append(k * lax.rsqrt(jnp.sum(k * k, axis=-1, keepdims=True) + EPS))
        yield
    beta = _sigmoid(gt)
    rows = i * ct + lax.broadcasted_iota(jnp.int32, gt.shape, 0)
    gcol = jnp.where(rows >= n_lead, -gprm_ref[0:1, :] * _softplus(gt + gprm_ref[1:2, :]), 0.0)
    lanes = i * ct + lax.broadcasted_iota(jnp.int32, gtt.shape, 1)
    grow = jnp.where(lanes >= n_lead, -gprmt_ref[:, 0:1] * _softplus(gtt + gprmt_ref[:, 1:2]), 0.0)
    ii = lax.broadcasted_iota(jnp.int32, (CHUNK, CHUNK), 0)
    jj = lax.broadcasted_iota(jnp.int32, (CHUNK, CHUNK), 1)
    lower = ii >= jj
    strict = ii > jj
    ltri = lower.astype(BF16)
    utri = (ii <= jj).astype(BF16)
    gcum_c = [sum(_dot(ltri, p) for p in _split3(gcol[rs(c)])) for c in range(nc)]
    gcum_r = [sum(_dot(p, utri) for p in _split3(grow[:, rs(c)])) for c in range(nc)]
    yield
    q_ = [qn[h][rs(c)] for c, h in items]
    k_ = [kn[h][rs(c)] for c, h in items]
    v_ = [cvb[2 * B_HEADS + h][rs(c)] for c, h in items]
    bh = [beta[rs(c), h:h + 1] for c, h in items]
    gc = [gcum_c[c][:, B_HEADS + h:B_HEADS + h + 1] for c, h in items]
    gr = [gcum_r[c][B_HEADS + h:B_HEADS + h + 1, :] for c, h in items]
    kb, a, qk = [], [], []
    for n in range(n_items):
        decay = jnp.exp(jnp.where(lower, gc[n] - gr[n], NEG))
        kb.append(k_[n] * bh[n])
        kq = _dotg(jnp.concatenate([kb[n], q_[n]], axis=0).astype(BF16), k_[n].astype(BF16), NT_DIMS)
        a.append(jnp.where(strict, kq[:CHUNK] * decay, 0.0))
        qk.append(jnp.where(lower, kq[CHUNK:] * decay, 0.0).astype(BF16))
        if n % B_HEADS == B_HEADS - 1:
            yield
    doff = [-jnp.where((ii >> 1) == (jj >> 1), a[n], 0.0) for n in range(n_items)]
    for lg in range(1, 6):
        m = ((ii >> (lg + 1)) == (jj >> (lg + 1))) & ((ii >> lg) != (jj >> lg))
        bm = [jnp.where(m, a[n], 0.0) for n in range(n_items)]
        db = [doff[n].astype(BF16) for n in range(n_items)]
        y = [bm[n] + _dot(db[n], bm[n].astype(BF16)) for n in range(n_items)]
        yield
        x = [y[n] + _dot(y[n].astype(BF16), db[n]) for n in range(n_items)]
        doff = [doff[n] - x[n] for n in range(n_items)]
        yield
    egc = [jnp.exp(gc[n]) for n in range(n_items)]
    g_last = [gc[n][CHUNK - 1:CHUNK, :] for n in range(n_items)]
    eg_last = [jnp.exp(g_last[n]) for n in range(n_items)]
    sol, kd, wq = [], [], []
    for n in range(n_items):
        rhs = jnp.concatenate([v_[n] * bh[n], kb[n] * egc[n]], axis=-1)
        sol.append(rhs + _dot(doff[n].astype(BF16), rhs.astype(BF16)))
        kd.append((k_[n] * jnp.exp(g_last[n] - gc[n])).astype(BF16))
        wq.append(jnp.concatenate([sol[n][:, B_DV:], q_[n] * egc[n]], axis=0).astype(BF16))
        if n % B_HEADS == B_HEADS - 1:
            yield

    def gate_store(c, o):
        for h in range(B_HEADS):
            z = env['zb'][rs(c), h * B_DV:(h + 1) * B_DV]
            mix_scr[rs(c), A_WIDTH + h * B_DV:A_WIDTH + (h + 1) * B_DV] = (
                _rms(o[h], onorm_ref[...]) * _silu(z)).astype(BF16)

    s_cur = [s_scr[h] for h in range(B_HEADS)]
    o_prev = None
    for c in range(nc):
        idx = [c * B_HEADS + h for h in range(B_HEADS)]
        ws = [_dot(wq[n], s_cur[h].astype(BF16)) for h, n in enumerate(idx)]
        if o_prev is not None:
            gate_store(c - 1, o_prev)
        yield
        v_new = [sol[n][:, :B_DV] - ws[h][:CHUNK] for h, n in enumerate(idx)]
        vb = [v_new[h].astype(BF16) for h in range(B_HEADS)]
        o_prev = [ws[h][CHUNK:] + _dot(qk[n], vb[h]) for h, n in enumerate(idx)]
        s_cur = [s_cur[h] * eg_last[n] + _dotg(kd[n], vb[h], TN_DIMS) for h, n in enumerate(idx)]
        yield
    for h in range(B_HEADS):
        s_scr[h] = s_cur[h]
    gate_store(nc - 1, o_prev)
    yield


def _ab_layer_staged_kernel(x_ref, g_ref, w_ref, wg_ref, wgt_ref, sink_ref, hist_ref, convw_ref, gprm_ref,
                            gprmt_ref, onorm_ref, wout_ref, convin_ref, sin_ref,
                            y_ref, kvtail_ref, xbtail_ref, sout_ref,
                            s_scr, xc_scr, kvprev_scr, mix_scr, *, ct, n_lead, pos_shift):
    i = pl.program_id(1)
    nsteps = pl.num_programs(1)
    ntail = min(WINDOW, ct)

    @pl.when(i == 0)
    def _():
        s_scr[...] = sin_ref[0]
        xc_scr[0:8, :] = convin_ref[0]
        kvprev_scr[1] = hist_ref[...]

    x = x_ref[0]
    hb = _rms(x, g_ref[...]).astype(BF16)
    proj = lambda a, b: _dot(hb, w_ref[:, a:b])
    env = {}

    def step(gen, n=1):
        for _ in range(n):
            next(gen, None)

    nblk = B_CONV_CH // LANES
    w = convw_ref[...]
    cvb = []

    def conv_block(j):
        cs = slice(j * LANES, (j + 1) * LANES)
        yc = xc_scr[pl.ds(8 - (CONV_W - 1), ct), cs] * w[0:1, cs]
        for t in range(1, CONV_W):
            yc = yc + xc_scr[pl.ds(8 - (CONV_W - 1) + t, ct), cs] * w[t:t + 1, cs]
        cvb.append(_silu(yc))

    pw = 2 * LANES
    for j in range(nblk // 2):
        xbj = proj(O_XB + j * pw, O_XB + (j + 1) * pw)
        xc_scr[8:8 + ct, j * pw:(j + 1) * pw] = xbj
        xbtail_ref[0, :, j * pw:(j + 1) * pw] = xbj[ct - 8:]
        if j > 0:
            conv_block(2 * j - 2)
            conv_block(2 * j - 1)
    gt = _dot(hb, wg_ref[...])
    gtt = _dotg(wgt_ref[...], hb, NT_DIMS)
    conv_block(nblk - 2)
    conv_block(nblk - 1)
    xc_scr[0:8, :] = xc_scr[ct:ct + 8, :]

    gdn = _gdn_stages(i, cvb, gt, gtt, env, gprm_ref, gprmt_ref, onorm_ref, s_scr, mix_scr, ct=ct, n_lead=n_lead)
    qa_p = []
    for j in range(A_WIDTH // pw):
        qa_p.append(proj(O_QA + j * pw, O_QA + (j + 1) * pw))
        step(gdn, 2)
    qa = jnp.concatenate(qa_p, axis=-1)
    kv = proj(O_KV, O_ZA)
    kvtail_ref[0] = kv[ct - ntail:]
    slot = lax.rem(i, 2)
    kvb = jnp.concatenate([kvprev_scr[1 - slot], kv], axis=0).astype(BF16)
    if ct >= WINDOW:
        kvprev_scr[slot] = kv[ct - WINDOW:]
    swa = _swa_stages(i, qa, kvb, env, sink_ref, mix_scr, ct=ct, sb=ntail, pos_shift=pos_shift)
    n_sw = (ct // ntail) * A_KV_HEADS
    step(gdn)
    for _ in range(ct // CHUNK):
        step(gdn)
        step(swa)
    step(swa, max(n_sw - ct // CHUNK, 0))
    za_p, zb_p = [], []
    fill = ([lambda: step(swa)] * (2 * n_sw)
            + [lambda j=j: za_p.append(proj(O_ZA + j * pw, O_ZA + (j + 1) * pw)) for j in range(A_WIDTH // pw)]
            + [lambda j=j: zb_p.append(proj(O_ZB + j * pw, O_ZB + (j + 1) * pw)) for j in range(B_VAL // pw)])
    per = -(-len(fill) // 10)
    for lv in range(10):
        step(gdn)
        for f in fill[lv * per:(lv + 1) * per]:
            f()
    for f in fill[10 * per:]:
        f()
    env['za'] = jnp.concatenate(za_p, axis=-1)
    env['zb'] = jnp.concatenate(zb_p, axis=-1)
    step(gdn, ct // CHUNK)
    for _ in swa:
        pass
    _swa_store(env, mix_scr, ct=ct, sb=ntail)
    ya = x + _dot(mix_scr[:, 0:A_WIDTH], wout_ref[0:A_WIDTH, :])
    for _ in gdn:
        pass
    y_ref[0] = ya + _dot(mix_scr[:, A_WIDTH:], wout_ref[A_WIDTH:, :])

    @pl.when(i == nsteps - 1)
    def _():
        sout_ref[0] = s_scr[...]


def _ab_layer_prompt(x3, norm_g, w_main, w_gate, w_gate_t, sink, hist_kv, conv_w8, gprm, gprm_t, onorm,
                     w_out, conv_in, s_in, ct, n_lead, pos_shift):
    B, L, _ = x3.shape
    ntail = min(WINDOW, ct)
    kern = functools.partial(_ab_layer_staged_kernel, ct=ct, n_lead=n_lead, pos_shift=pos_shift)
    full = lambda a: pl.BlockSpec(a.shape, lambda b, i: (0,) * a.ndim)
    blk = pl.BlockSpec((1, ct, D_MODEL), lambda b, i: (b, i, 0))
    per_b = lambda *s: pl.BlockSpec((1,) + s, lambda b, i: (b,) + (0,) * len(s))
    return pl.pallas_call(
        kern,
        grid=(B, L // ct),
        in_specs=[blk, full(norm_g), full(w_main), full(w_gate), full(w_gate_t), full(sink), full(hist_kv),
                  full(conv_w8), full(gprm), full(gprm_t), full(onorm), full(w_out),
                  per_b(8, B_CONV_CH), per_b(B_HEADS, B_DK, B_DV)],
        out_specs=[blk, per_b(ntail, 2 * A_KV_WIDTH), per_b(8, B_CONV_CH), per_b(B_HEADS, B_DK, B_DV)],
        out_shape=[jax.ShapeDtypeStruct((B, L, D_MODEL), F32),
                   jax.ShapeDtypeStruct((B, ntail, 2 * A_KV_WIDTH), F32),
                   jax.ShapeDtypeStruct((B, 8, B_CONV_CH), F32),
                   jax.ShapeDtypeStruct((B, B_HEADS, B_DK, B_DV), F32)],
        scratch_shapes=[pltpu.VMEM((B_HEADS, B_DK, B_DV), F32),
                        pltpu.VMEM((ct + 8, B_CONV_CH), F32),
                        pltpu.VMEM((2, WINDOW, 2 * A_KV_WIDTH), F32),
                        pltpu.VMEM((ct, AB_MIX), BF16)],
        compiler_params=_cparams(("arbitrary", "arbitrary")),
        name="ab_layer_prompt",
    )(x3, norm_g, w_main, w_gate, w_gate_t, sink, hist_kv, conv_w8, gprm, gprm_t, onorm, w_out, conv_in, s_in)


def _c_layer_kernel(x_ref, g_ref, w_ref, wgk_ref, wup_ref, bgk_ref, onorm_ref, wout_ref, gfin_ref,
                    sin_ref, y_ref, sout_ref, st_scr, og_scr, *, ct, n_lead):
    i = pl.program_id(1)
    nsteps = pl.num_programs(1)

    @pl.when(i == 0)
    def _():
        st_scr[...] = sin_ref[0]

    x = x_ref[0]
    hb = _rms(x, g_ref[...]).astype(BF16)
    low = _dot(hb, wgk_ref[...])
    q_all = _dot(hb, w_ref[:, 0:C_KEY])
    la_all = _gla_log_decay(low, wup_ref, bgk_ref)
    k_all = _dot(hb, w_ref[:, C_KEY:2 * C_KEY])
    v_h, z_h = [], []
    later = ([lambda h=h: v_h.append(_dot(hb, w_ref[:, 2 * C_KEY + h * C_DV:2 * C_KEY + (h + 1) * C_DV]).astype(BF16))
              for h in range(C_HEADS)]
             + [lambda h=h: z_h.append(_dot(hb, w_ref[:, 2 * C_KEY + C_VAL + h * C_DV:
                                                       2 * C_KEY + C_VAL + (h + 1) * C_DV]))
                for h in range(C_HEADS)])

    def run_later(n):
        for _ in range(n):
            if later:
                later.pop(0)()
    rows = i * ct + lax.broadcasted_iota(jnp.int32, la_all.shape, 0)
    la_all = jnp.where(rows >= n_lead, la_all, 0.0)

    ii = lax.broadcasted_iota(jnp.int32, (CHUNK, CHUNK), 0)
    jj = lax.broadcasted_iota(jnp.int32, (CHUNK, CHUNK), 1)
    lower = ii >= jj
    ltri = lower.astype(BF16)
    nc = ct // CHUNK
    rs = lambda c: slice(c * CHUNK, (c + 1) * CHUNK)
    ks = lambda h: slice(h * C_DK, (h + 1) * C_DK)
    vs = lambda h: slice(h * C_DV, (h + 1) * C_DV)
    items = [(c, h) for c in range(nc) for h in range(C_HEADS)]
    n_items = len(items)
    bc_all = [sum(_dot(ltri, p) for p in _split3(la_all[rs(c)])) for c in range(nc)]
    run_later(2)
    every =max(nc // C_HEADS, 1)
    qd, kinv, kd, gl, qk = [], [], [], [], []
    for n, (c, h) in enumerate(items):
        bc = bc_all[c][:, ks(h)]
        k = k_all[rs(c), ks(h)]
        bl = bc[CHUNK - 1:CHUNK, :]
        qd.append((q_all[rs(c), ks(h)] * (C_DK ** -0.5) * jnp.exp(bc)).astype(BF16))
        kinv.append((k * jnp.exp(-bc)).astype(BF16))
        kd.append((k * jnp.exp(bl - bc)).astype(BF16))
        gl.append(jnp.exp(bl))
        if h == C_HEADS - 1:
            qk += [jnp.where(lower, _dotg(qd[m], kinv[m], NT_DIMS), 0.0).astype(BF16)
                   for m in range(n - C_HEADS + 1, n + 1)]
            if c % every == every - 1:
                run_later(1)
    run_later(C_HEADS - len(v_h))
    v_ = [v_h[h][rs(c)] for c, h in items]
    o_intra = []
    for c in range(nc):
        o_intra += [_dot(qk[c * C_HEADS + h], v_[c * C_HEADS + h]) for h in range(C_HEADS)]
        if c % every == every - 1:
            run_later(1)
    run_later(len(later))

    rb = min(ct, 4 * CHUNK)

    def gate_store(c, o):
        for h in range(C_HEADS):
            og_scr[rs(c), vs(h)] = (_rms(o[h], onorm_ref[...]) * _silu(z_h[h][rs(c)])).astype(BF16)
        if ((c + 1) * CHUNK) % rb == 0:
            r = slice((c + 1) * CHUNK - rb, (c + 1) * CHUNK)
            y_ref[0, r, :] = _rms(x[r] + _dot(og_scr[r, :], wout_ref[...]), gfin_ref[...])

    st = [st_scr[h] for h in range(C_HEADS)]
    o_prev = None
    for c in range(nc):
        idx = [c * C_HEADS + h for h in range(C_HEADS)]
        o = [o_intra[n] + _dotg(qd[n], st[h].astype(BF16), NT_DIMS) for h, n in enumerate(idx)]
        st = [st[h] * gl[n] + _dotg(v_[n], kd[n], TN_DIMS) for h, n in enumerate(idx)]
        if o_prev is not None:
            gate_store(c - 1, o_prev)
        o_prev = o
    gate_store(nc - 1, o_prev)
    for h in range(C_HEADS):
        st_scr[h] = st[h]

    @pl.when(i == nsteps - 1)
    def _():
        sout_ref[0] = st_scr[...]


def _c_layer_prompt(x3, norm_g, w_main, w_gk, w_up, b_gk, onorm, w_out, final_g, st_in, ct, n_lead):
    B, L, _ = x3.shape
    kern = functools.partial(_c_layer_kernel, ct=ct, n_lead=n_lead)
    full = lambda a: pl.BlockSpec(a.shape, lambda b, i: (0,) * a.ndim)
    blk = pl.BlockSpec((1, ct, D_MODEL), lambda b, i: (b, i, 0))
    st = pl.BlockSpec((1, C_HEADS, C_DV, C_DK), lambda b, i: (b, 0, 0, 0))
    return pl.pallas_call(
        kern,
        grid=(B, L // ct),
        in_specs=[blk, full(norm_g), full(w_main), full(w_gk), full(w_up), full(b_gk), full(onorm),
                  full(w_out), full(final_g), st],
        out_specs=[blk, st],
        out_shape=[jax.ShapeDtypeStruct((B, L, D_MODEL), F32),
                   jax.ShapeDtypeStruct((B, C_HEADS, C_DV, C_DK), F32)],
        scratch_shapes=[pltpu.VMEM((C_HEADS, C_DV, C_DK), F32), pltpu.VMEM((ct, C_VAL), BF16)],
        compiler_params=_cparams(("arbitrary", "arbitrary")),
        name="c_layer_prompt",
    )(x3, norm_g, w_main, w_gk, w_up, b_gk, onorm, w_out, final_g, st_in)


def _ab_proj_kernel(x_ref, g_ref, w_ref, wg_ref, qa_ref, kv_ref, za_ref, xb_ref, zb_ref, gates_ref):
    hb = _rms(x_ref[...], g_ref[...]).astype(BF16)
    qa_ref[...] = _dot(hb, w_ref[:, O_QA:O_KV])
    kv_ref[...] = _dot(hb, w_ref[:, O_KV:O_ZA])
    za_ref[...] = _dot(hb, w_ref[:, O_ZA:O_XB])
    xb_ref[...] = _dot(hb, w_ref[:, O_XB:O_ZB])
    zb_ref[...] = _dot(hb, w_ref[:, O_ZB:AB_MAIN])
    gates_ref[...] = _dot(hb, wg_ref[...])


def _ab_proj(x2d, norm_g, w_main, w_gate):
    n = x2d.shape[0]
    full = lambda a: pl.BlockSpec(a.shape, lambda: (0,) * a.ndim)
    widths = (A_WIDTH, 2 * A_KV_WIDTH, A_WIDTH, B_CONV_CH, B_VAL, LANES)
    args = (x2d, norm_g, w_main, w_gate)
    return pl.pallas_call(
        _ab_proj_kernel,
        in_specs=[full(a) for a in args],
        out_specs=[pl.BlockSpec((n, w), lambda: (0, 0)) for w in widths],
        out_shape=[jax.ShapeDtypeStruct((n, w), F32) for w in widths],
        compiler_params=pltpu.CompilerParams(vmem_limit_bytes=VMEM_LIMIT),
        name="ab_in_proj_sample",
    )(*args)


def _swa_sample_kernel(q_ref, kvn_ref, za_ref, ck_ref, cv_ref, sink_ref, o_ref, nk_ref, nv_ref, *, bt):
    row = lax.broadcasted_iota(jnp.int32, (WINDOW, A_KV_WIDTH), 0)
    hrow = lax.broadcasted_iota(jnp.int32, (A_HEADS, A_HD), 0)
    sk = sink_ref[...]
    nkb, nvb = [], []
    for b in range(bt):
        nk = jnp.where(row == WINDOW - 1, kvn_ref[b:b + 1, 0:A_KV_WIDTH], pltpu.roll(ck_ref[b], WINDOW - 1, 0))
        nv = jnp.where(row == WINDOW - 1, kvn_ref[b:b + 1, A_KV_WIDTH:], pltpu.roll(cv_ref[b], WINDOW - 1, 0))
        nk_ref[b] = nk
        nv_ref[b] = nv
        nkb.append(nk.astype(BF16))
        nvb.append(nv.astype(BF16))
    items = [(b, j) for b in range(bt) for j in range(A_KV_HEADS)]
    q8 = [q_ref[b].astype(BF16) for b in range(bt)]
    sc = [_dotg(q8[b], nkb[b][:, j * A_HD:(j + 1) * A_HD], NT_DIMS) * (A_HD ** -0.5) for b, j in items]
    m = [jnp.maximum(jnp.max(s, axis=-1, keepdims=True), sk) for s in sc]
    p = [jnp.exp(sc[n] - m[n]) for n in range(len(items))]
    den = [jnp.sum(p[n], axis=-1, keepdims=True) + jnp.exp(sk - m[n]) for n in range(len(items))]
    oj = [_dot(p[n].astype(BF16), nvb[b][:, j * A_HD:(j + 1) * A_HD]) / den[n] for n, (b, j) in enumerate(items)]
    for b in range(bt):
        o8 = jnp.where(hrow >= A_GROUP, oj[b * A_KV_HEADS + 1], oj[b * A_KV_HEADS])
        o_ref[b] = o8 * _silu(za_ref[b])


def _swa_sample(q3, kv_new, za3, cache_k, cache_v, sink_col, bt):
    nb = q3.shape[0]
    hd = pl.BlockSpec((bt, A_HEADS, A_HD), lambda i: (i, 0, 0))
    cache = pl.BlockSpec((bt, WINDOW, A_KV_WIDTH), lambda i: (i, 0, 0))
    return pl.pallas_call(
        functools.partial(_swa_sample_kernel, bt=bt),
        grid=(nb // bt,),
        in_specs=[hd, pl.BlockSpec((bt, 2 * A_KV_WIDTH), lambda i: (i, 0)), hd, cache, cache,
                  pl.BlockSpec(sink_col.shape, lambda i: (0, 0))],
        out_specs=[hd, cache, cache],
        out_shape=[jax.ShapeDtypeStruct(q3.shape, F32),
                   jax.ShapeDtypeStruct(cache_k.shape, F32),
                   jax.ShapeDtypeStruct(cache_v.shape, F32)],
        compiler_params=_cparams(("arbitrary",)),
        name="swa_sample",
    )(q3, kv_new, za3, cache_k, cache_v, sink_col)


def _gdn_sample_prep_kernel(xb_ref, h0_ref, h1_ref, h2_ref, gates_ref, convw_ref, gprm_ref,
                            q_ref, k_ref, v_ref, bg_ref):
    w = convw_ref[...]
    y = (h0_ref[...] * w[0:1, :] + h1_ref[...] * w[1:2, :] + h2_ref[...] * w[2:3, :]
         + xb_ref[...] * w[3:4, :])
    cv = _silu(y)
    for h in range(B_HEADS):
        q = cv[:, h * B_DK:(h + 1) * B_DK]
        k = cv[:, B_KEY + h * B_DK:B_KEY + (h + 1) * B_DK]
        q_ref[:, h * B_DK:(h + 1) * B_DK] = (
            q * lax.rsqrt(jnp.sum(q * q, axis=-1, keepdims=True) + EPS) * (B_DK ** -0.5))
        k_ref[:, h * B_DK:(h + 1) * B_DK] = k * lax.rsqrt(jnp.sum(k * k, axis=-1, keepdims=True) + EPS)
    v_ref[...] = cv[:, 2 * B_KEY:]
    gt = gates_ref[...]
    col = lax.broadcasted_iota(jnp.int32, gt.shape, 1)
    g = -gprm_ref[0:1, :] * _softplus(gt + gprm_ref[1:2, :])
    bg_ref[...] = jnp.where(col < B_HEADS, _sigmoid(gt), jnp.exp(g))


def _gdn_sample_prep(xb, h0, h1, h2, gates, conv_w8, gprm):
    n = xb.shape[0]
    full = lambda a: pl.BlockSpec(a.shape, lambda: (0,) * a.ndim)
    args = (xb, h0, h1, h2, gates, conv_w8, gprm)
    return pl.pallas_call(
        _gdn_sample_prep_kernel,
        in_specs=[full(a) for a in args],
        out_specs=[pl.BlockSpec((n, B_KEY), lambda: (0, 0))] * 3 + [pl.BlockSpec((n, LANES), lambda: (0, 0))],
        out_shape=[jax.ShapeDtypeStruct((n, B_KEY), F32)] * 3 + [jax.ShapeDtypeStruct((n, LANES), F32)],
        compiler_params=pltpu.CompilerParams(vmem_limit_bytes=VMEM_LIMIT),
        name="gdn_sample_prep",
    )(*args)


def _gdn_sample_kernel(s_ref, qt_ref, kt_ref, v_ref, bg_ref, zb_ref, onorm_ref, o_ref, sout_ref, *, bt):
    items = [(b, h) for b in range(bt) for h in range(B_HEADS)]
    hs = lambda h: slice(h * B_DV, (h + 1) * B_DV)
    kc = [kt_ref[0, h, :, b:b + 1] for b, h in items]
    eg = [bg_ref[b:b + 1, B_HEADS + h:B_HEADS + h + 1] for b, h in items]
    ks = [jnp.sum(kc[n] * s_ref[b, h], axis=0, keepdims=True) for n, (b, h) in enumerate(items)]
    v_new = [bg_ref[b:b + 1, h:h + 1] * (v_ref[b:b + 1, hs(h)] - eg[n] * ks[n]) for n, (b, h) in enumerate(items)]
    o = []
    for n, (b, h) in enumerate(items):
        s_new = eg[n] * s_ref[b, h] + kc[n] * v_new[n]
        sout_ref[b, h] = s_new
        o.append(jnp.sum(qt_ref[0, h, :, b:b + 1] * s_new, axis=0, keepdims=True))
    o_blk = jnp.concatenate([jnp.concatenate([o[b * B_HEADS + h] for h in range(B_HEADS)], axis=-1)
                             for b in range(bt)], axis=0)
    for h in range(B_HEADS):
        o_ref[:, hs(h)] = _rms(o_blk[:, hs(h)], onorm_ref[...]) * _silu(zb_ref[:, hs(h)])


def _gdn_sample(state, qt, kt, v, bg, zb, onorm, bt):
    nb = state.shape[0]
    st = pl.BlockSpec((bt, B_HEADS, B_DK, B_DV), lambda i: (i, 0, 0, 0))
    col = pl.BlockSpec((1, B_HEADS, B_DK, bt), lambda i: (i, 0, 0, 0))
    row = lambda n: pl.BlockSpec((bt, n), lambda i: (i, 0))
    return pl.pallas_call(
        functools.partial(_gdn_sample_kernel, bt=bt),
        grid=(nb // bt,),
        in_specs=[st, col, col, row(B_VAL), row(LANES), row(B_VAL),
                  pl.BlockSpec(onorm.shape, lambda i: (0, 0))],
        out_specs=[row(B_VAL), st],
        out_shape=[jax.ShapeDtypeStruct((nb, B_VAL), F32), jax.ShapeDtypeStruct(state.shape, F32)],
        compiler_params=_cparams(("arbitrary",)),
        name="gdn_sample",
    )(state, qt, kt, v, bg, zb, onorm)


def _ab_out_c_proj_kernel(x_ref, oa_ref, ob_ref, wout_ref, g_ref, w_ref, wgk_ref, wup_ref, bgk_ref,
                          y_ref, q_ref, k_ref, v_ref, z_ref, la_ref):
    y = (x_ref[...] + _dot(oa_ref[...].astype(BF16), wout_ref[0:A_WIDTH, :])
         + _dot(ob_ref[...].astype(BF16), wout_ref[A_WIDTH:, :]))
    y_ref[...] = y
    hb = _rms(y, g_ref[...]).astype(BF16)
    q_ref[...] = _dot(hb, w_ref[:, 0:C_KEY])
    k_ref[...] = _dot(hb, w_ref[:, C_KEY:2 * C_KEY])
    v_ref[...] = _dot(hb, w_ref[:, 2 * C_KEY:2 * C_KEY + C_VAL])
    z_ref[...] = _dot(hb, w_ref[:, 2 * C_KEY + C_VAL:C_MAIN])
    la_ref[...] = _gla_log_decay(_dot(hb, wgk_ref[...]), wup_ref, bgk_ref)


def _ab_out_c_proj(x2d, oa, ob, w_out, norm_g, w_main, w_gk, w_up, b_gk):
    n = x2d.shape[0]
    full = lambda a: pl.BlockSpec(a.shape, lambda: (0,) * a.ndim)
    widths = (D_MODEL, C_KEY, C_KEY, C_VAL, C_VAL, C_KEY)
    args = (x2d, oa, ob, w_out, norm_g, w_main, w_gk, w_up, b_gk)
    return pl.pallas_call(
        _ab_out_c_proj_kernel,
        in_specs=[full(a) for a in args],
        out_specs=[pl.BlockSpec((n, w), lambda: (0, 0)) for w in widths],
        out_shape=[jax.ShapeDtypeStruct((n, w), F32) for w in widths],
        compiler_params=pltpu.CompilerParams(vmem_limit_bytes=VMEM_LIMIT),
        name="ab_out_c_in_proj_sample",
    )(*args)


def _gla_sample_kernel(s_ref, qt_ref, kt_ref, lat_ref, v_ref, z_ref, onorm_ref, o_ref, sout_ref, *, bt):
    items = [(b, h) for b in range(bt) for h in range(C_HEADS)]
    hs = lambda h: slice(h * C_DV, (h + 1) * C_DV)
    ac = [jnp.exp(lat_ref[0, h, :, b:b + 1]) for b, h in items]
    o = []
    for n, (b, h) in enumerate(items):
        s_new = ac[n] * s_ref[b, h] + kt_ref[0, h, :, b:b + 1] * v_ref[b:b + 1, hs(h)]
        sout_ref[b, h] = s_new
        o.append(jnp.sum((qt_ref[0, h, :, b:b + 1] * (C_DK ** -0.5)) * s_new, axis=0, keepdims=True))
    o_blk = jnp.concatenate([jnp.concatenate([o[b * C_HEADS + h] for h in range(C_HEADS)], axis=-1)
                             for b in range(bt)], axis=0)
    for h in range(C_HEADS):
        o_ref[:, hs(h)] = _rms(o_blk[:, hs(h)], onorm_ref[...]) * _silu(z_ref[:, hs(h)])


def _gla_sample(state, qt, kt, lat, v, z, onorm, bt):
    nb = state.shape[0]
    st = pl.BlockSpec((bt, C_HEADS, C_DK, C_DV), lambda i: (i, 0, 0, 0))
    col = pl.BlockSpec((1, C_HEADS, C_DK, bt), lambda i: (i, 0, 0, 0))
    row = lambda n: pl.BlockSpec((bt, n), lambda i: (i, 0))
    return pl.pallas_call(
        functools.partial(_gla_sample_kernel, bt=bt),
        grid=(nb // bt,),
        in_specs=[st, col, col, col, row(C_VAL), row(C_VAL),
                  pl.BlockSpec(onorm.shape, lambda i: (0, 0))],
        out_specs=[row(C_VAL), st],
        out_shape=[jax.ShapeDtypeStruct((nb, C_VAL), F32), jax.ShapeDtypeStruct(state.shape, F32)],
        compiler_params=_cparams(("arbitrary",)),
        name="gla_sample",
    )(state, qt, kt, lat, v, z, onorm)


def _c_out_kernel(x_ref, o_ref, w_ref, g_ref, y_ref):
    y_ref[...] = _rms(x_ref[...] + _dot(o_ref[...].astype(BF16), w_ref[...]), g_ref[...])


def _c_out(x2d, oc, w_out, final_g):
    n = x2d.shape[0]
    full = lambda a: pl.BlockSpec(a.shape, lambda: (0,) * a.ndim)
    args = (x2d, oc, w_out, final_g)
    return pl.pallas_call(
        _c_out_kernel,
        in_specs=[full(a) for a in args],
        out_specs=pl.BlockSpec((n, D_MODEL), lambda: (0, 0)),
        out_shape=jax.ShapeDtypeStruct((n, D_MODEL), F32),
        compiler_params=pltpu.CompilerParams(vmem_limit_bytes=VMEM_LIMIT),
        name="c_out_proj_norm_sample",
    )(*args)


def _to_cols(a, heads, dk, bt):
    n = a.shape[0]
    return a.reshape(n // bt, bt, heads, dk).transpose(0, 2, 3, 1)


def kernel(x_prompt, x_sample, cache_swa_k, cache_swa_v, state_dn_conv, state_dn, state_gla,
           meta_tokens, norm_ab, w_in_ab, sink_a, conv_b, a_log_b, dt_bias_b, onorm_b, w_out_ab,
           norm_c, w_in_c, w_gk_up, b_gk, onorm_c, w_out_c, final_norm):
    Bp, L, _ = x_prompt.shape
    Bs = x_sample.shape[0]
    CT, BT = 256, 8

    w_ab = w_in_ab[0]
    w_ab_main = w_ab.astype(BF16)
    w_ab_gate = jnp.pad(w_ab[:, AB_MAIN:], ((0, 0), (0, LANES - 2 * B_HEADS))).astype(BF16)
    w_ab_gate_t = jnp.pad(w_ab[:, AB_MAIN:].T, ((0, 16 - 2 * B_HEADS), (0, 0))).astype(BF16)
    g_ab = norm_ab[0][None, :]
    sink_row = sink_a[0][None, :]
    sink_col = sink_a[0][:, None]
    conv_w8 = jnp.pad(conv_b[0], ((0, 8 - CONV_W), (0, 0)))
    ea = jnp.exp(a_log_b[0])
    gprm = jnp.zeros((8, LANES), F32).at[0, B_HEADS:2 * B_HEADS].set(ea).at[1, B_HEADS:2 * B_HEADS].set(dt_bias_b[0])
    gprm_t = jnp.zeros((16, LANES), F32).at[B_HEADS:2 * B_HEADS, 0].set(ea).at[B_HEADS:2 * B_HEADS, 1].set(dt_bias_b[0])
    onb = onorm_b[0][None, :]
    w_ab_out = w_out_ab[0].astype(BF16)
    w_c = w_in_c[0]
    w_c_main = w_c.astype(BF16)
    w_c_gk = jnp.pad(w_c[:, C_MAIN:], ((0, 0), (0, LANES - C_RANK))).astype(BF16)
    w_up = jnp.pad(w_gk_up[0], ((0, LANES - C_RANK), (0, 0)))
    bgk = b_gk[0][None, :]
    g_c = norm_c[0][None, :]
    onc = onorm_c[0][None, :]
    w_c_out = w_out_c[0].astype(BF16)
    g_fin = final_norm[None, :]

    def ab_layer(x3, ct, hist_kv, pos_shift, conv_in, s_in, n_lead):
        return _ab_layer_prompt(x3, g_ab, w_ab_main, w_ab_gate, w_ab_gate_t, sink_row, hist_kv, conv_w8,
                                gprm, gprm_t, onb, w_ab_out, conv_in, s_in, ct, n_lead, pos_shift)

    def c_layer(x3, ct, st_in, n_lead):
        return _c_layer_prompt(x3, g_c, w_c_main, w_c_gk, w_up, bgk, onc, w_c_out, g_fin, st_in, ct, n_lead)

    xpre = jnp.concatenate([jnp.zeros((LEAD, D_MODEL), F32), meta_tokens], axis=0)[None]
    y_pre, kv_pre, xbtail_pre, sdn_pre = ab_layer(
        xpre, CHUNK, jnp.zeros((WINDOW, 2 * A_KV_WIDTH), F32), -LEAD,
        jnp.zeros((1, 8, B_CONV_CH), F32), jnp.zeros((1, B_HEADS, B_DK, B_DV), F32), LEAD)
    _, sgla_pre = c_layer(y_pre, CHUNK, jnp.zeros((1, C_HEADS, C_DV, C_DK), F32), LEAD)

    hist_kv = jnp.concatenate([jnp.zeros((WINDOW - CHUNK, 2 * A_KV_WIDTH), F32), kv_pre[0]], axis=0)
    y1, kv_tail, xb_tail, sdn_p = ab_layer(
        x_prompt, 2 * CT, hist_kv, N_META, jnp.broadcast_to(xbtail_pre, (Bp, 8, B_CONV_CH)),
        jnp.broadcast_to(sdn_pre, (Bp, B_HEADS, B_DK, B_DV)), 0)
    y_prompt, sgla_t = c_layer(y1, 2 * CT, jnp.broadcast_to(sgla_pre, (Bp, C_HEADS, C_DV, C_DK)), 0)
    sgla_p = jnp.swapaxes(sgla_t, 2, 3)

    swa_k_p = kv_tail[:, :, :A_KV_WIDTH].reshape(1, Bp, WINDOW, A_KV_HEADS, A_HD)
    swa_v_p = kv_tail[:, :, A_KV_WIDTH:].reshape(1, Bp, WINDOW, A_KV_HEADS, A_HD)
    dn_conv_p = xb_tail[:, 8 - (CONV_W - 1):, :][None]

    xs = x_sample.reshape(Bs, D_MODEL)
    qa, kvn, za, xb, zb, gates = _ab_proj(xs, g_ab, w_ab_main, w_ab_gate)
    oa3, nk, nv = _swa_sample(qa.reshape(Bs, A_HEADS, A_HD), kvn, za.reshape(Bs, A_HEADS, A_HD),
                              cache_swa_k[0].reshape(Bs, WINDOW, A_KV_WIDTH),
                              cache_swa_v[0].reshape(Bs, WINDOW, A_KV_WIDTH), sink_col, BT)
    hist = state_dn_conv[0]
    qn, kn, vn, bg = _gdn_sample_prep(xb, hist[:, 0], hist[:, 1], hist[:, 2], gates, conv_w8, gprm)
    ob, sdn_s = _gdn_sample(state_dn[0], _to_cols(qn, B_HEADS, B_DK, BT), _to_cols(kn, B_HEADS, B_DK, BT),
                            vn, bg, zb, onb, BT)
    ys, qc, kc, vc, zc, la = _ab_out_c_proj(xs, oa3.reshape(Bs, A_WIDTH), ob, w_ab_out,
                                            g_c, w_c_main, w_c_gk, w_up, bgk)
    ocs, sgla_s = _gla_sample(state_gla[0], _to_cols(qc, C_HEADS, C_DK, BT), _to_cols(kc, C_HEADS, C_DK, BT),
                              _to_cols(la, C_HEADS, C_DK, BT), vc, zc, onc, BT)
    y_sample = _c_out(ys, ocs, w_c_out, g_fin).reshape(Bs, 1, D_MODEL)
    dn_conv_s = jnp.concatenate([hist[:, 1:], xb[:, None, :]], axis=1)[None]

    return (y_prompt, y_sample, swa_k_p, swa_v_p, dn_conv_p, sdn_p[None], sgla_p[None],
            nk.reshape(1, Bs, WINDOW, A_KV_HEADS, A_HD), nv.reshape(1, Bs, WINDOW, A_KV_HEADS, A_HD),
            dn_conv_s, sdn_s[None], sgla_s[None])
```

```python
import functools

import jax
import jax.numpy as jnp
from jax import lax
from jax.experimental import pallas as pl
from jax.experimental.pallas import tpu as pltpu

F32 = jnp.float32
BF16 = jnp.bfloat16
EPS = 1e-6

D_MODEL = 1024
N_META = 16
CHUNK = 64
LEAD = (-N_META) % CHUNK
A_HEADS, A_KV_HEADS, A_HD = 8, 2, 64
A_GROUP = A_HEADS // A_KV_HEADS
A_WIDTH = A_HEADS * A_HD
A_KV_WIDTH = A_KV_HEADS * A_HD
WINDOW = 128
B_HEADS, B_DK, B_DV = 4, 128, 128
B_KEY = B_HEADS * B_DK
B_VAL = B_HEADS * B_DV
CONV_W = 4
B_CONV_CH = 2 * B_KEY + B_VAL
C_HEADS, C_DK, C_DV = 4, 128, 256
C_KEY = C_HEADS * C_DK
C_VAL = C_HEADS * C_DV
C_RANK = 16
C_GATE_NORM = 16.0
AB_MAIN = 2 * A_WIDTH + 2 * A_KV_WIDTH + B_CONV_CH + B_VAL
AB_MIX = A_WIDTH + B_VAL
C_MAIN = 2 * C_KEY + 2 * C_VAL
O_QA, O_KV, O_ZA, O_XB, O_ZB = 0, 512, 768, 1280, 2816
LANES = 128
NEG = -1e30

VMEM_LIMIT = 56 * 1024 * 1024

NT_DIMS = (((1,), (1,)), ((), ()))
TN_DIMS = (((0,), (0,)), ((), ()))


def _cparams(sem, **kw):
    return pltpu.CompilerParams(dimension_semantics=sem, vmem_limit_bytes=VMEM_LIMIT, **kw)


def _dot(a, b):
    return jnp.dot(a, b, preferred_element_type=F32)


def _dotg(a, b, dims):
    return lax.dot_general(a, b, dims, preferred_element_type=F32)


def _split3(a):
    hi = a.astype(BF16)
    r = a - hi.astype(F32)
    mid = r.astype(BF16)
    lo = (r - mid.astype(F32)).astype(BF16)
    return hi, mid, lo


def _sigmoid(x):
    return 1.0 / (1.0 + jnp.exp(-x))


def _silu(x):
    return x * _sigmoid(x)


def _softplus(x):
    return jnp.maximum(x, 0.0) + jnp.log1p(jnp.exp(-jnp.abs(x)))


def _log_sigmoid(x):
    return jnp.minimum(x, 0.0) - jnp.log1p(jnp.exp(-jnp.abs(x)))


def _rms(x, g):
    return x * lax.rsqrt(jnp.mean(x * x, axis=-1, keepdims=True) + EPS) * g


def _gla_log_decay(low, wup_ref, bgk_ref):
    lh, lm, _ = _split3(low)
    wh, wm, _ = _split3(wup_ref[...])
    up = _dot(lh, wh) + _dot(lh, wm) + _dot(lm, wh)
    return _log_sigmoid(up + bgk_ref[...]) * (1.0 / C_GATE_NORM)


def _swa_tile(i, qa, kvb, za, sink_ref, mix_scr, *, ct, sb, pos_shift):
    r = lax.broadcasted_iota(jnp.int32, (sb, WINDOW + sb), 0)
    c = lax.broadcasted_iota(jnp.int32, (sb, WINDOW + sb), 1)
    diff = r - c + WINDOW
    in_window = (diff >= 0) & (diff < WINDOW)
    nsb = ct // sb
    items = [(s, j) for s in range(nsb) for j in range(A_KV_HEADS)]
    masks = [jnp.concatenate([in_window & (i * ct + s * sb + c - WINDOW + pos_shift >= 0)] * A_GROUP, axis=0)
             for s in range(nsb)]
    sinks = [jnp.concatenate([jnp.broadcast_to(sink_ref[0:1, j * A_GROUP + g:j * A_GROUP + g + 1], (sb, 1))
                              for g in range(A_GROUP)], axis=0) for j in range(A_KV_HEADS)]
    qs = [jnp.concatenate([qa[s * sb:(s + 1) * sb, (j * A_GROUP + g) * A_HD:(j * A_GROUP + g + 1) * A_HD]
                           for g in range(A_GROUP)], axis=0).astype(BF16) for s, j in items]
    sc = [_dotg(qs[n], kvb[s * sb:s * sb + WINDOW + sb, j * A_HD:(j + 1) * A_HD], NT_DIMS) * (A_HD ** -0.5)
          for n, (s, j) in enumerate(items)]
    sc = [jnp.where(masks[s], sc[n], NEG) for n, (s, j) in enumerate(items)]
    mx = [jnp.maximum(jnp.max(sc[n], axis=-1, keepdims=True), sinks[j]) for n, (s, j) in enumerate(items)]
    p = [jnp.exp(sc[n] - mx[n]) for n in range(len(items))]
    den = [jnp.sum(p[n], axis=-1, keepdims=True) + jnp.exp(sinks[j] - mx[n]) for n, (s, j) in enumerate(items)]
    pv = [_dot(p[n].astype(BF16),
               kvb[s * sb:s * sb + WINDOW + sb, A_KV_WIDTH + j * A_HD:A_KV_WIDTH + (j + 1) * A_HD]) / den[n]
          for n, (s, j) in enumerate(items)]
    for s in range(nsb):
        o = jnp.concatenate([pv[s * A_KV_HEADS + j][g * sb:(g + 1) * sb]
                             for j in range(A_KV_HEADS) for g in range(A_GROUP)], axis=-1)
        mix_scr[s * sb:(s + 1) * sb, 0:A_WIDTH] = (o * _silu(za[s * sb:(s + 1) * sb])).astype(BF16)


def _gdn_tile(i, cv, gt, gtt, zb, gprm_ref, gprmt_ref, onorm_ref, s_scr, mix_scr, *, ct, n_lead):
    beta = _sigmoid(gt)
    rows = i * ct + lax.broadcasted_iota(jnp.int32, gt.shape, 0)
    gcol = jnp.where(rows >= n_lead, -gprm_ref[0:1, :] * _softplus(gt + gprm_ref[1:2, :]), 0.0)
    lanes = i * ct + lax.broadcasted_iota(jnp.int32, gtt.shape, 1)
    grow = jnp.where(lanes >= n_lead, -gprmt_ref[:, 0:1] * _softplus(gtt + gprmt_ref[:, 1:2]), 0.0)

    ii = lax.broadcasted_iota(jnp.int32, (CHUNK, CHUNK), 0)
    jj = lax.broadcasted_iota(jnp.int32, (CHUNK, CHUNK), 1)
    lower = ii >= jj
    strict = ii > jj
    ltri = lower.astype(BF16)
    utri = (ii <= jj).astype(BF16)

    nc = ct // CHUNK
    items = [(c, h) for c in range(nc) for h in range(B_HEADS)]
    n_items = len(items)
    rs = lambda c: slice(c * CHUNK, (c + 1) * CHUNK)
    qn, kn = [], []
    for h in range(B_HEADS):
        q = cv[:, h * B_DK:(h + 1) * B_DK]
        k = cv[:, B_KEY + h * B_DK:B_KEY + (h + 1) * B_DK]
        qn.append(q * lax.rsqrt(jnp.sum(q * q, axis=-1, keepdims=True) + EPS) * (B_DK ** -0.5))
        kn.append(k * lax.rsqrt(jnp.sum(k * k, axis=-1, keepdims=True) + EPS))
    gcum_c = [sum(_dot(ltri, p) for p in _split3(gcol[rs(c)])) for c in range(nc)]
    gcum_r = [sum(_dot(p, utri) for p in _split3(grow[:, rs(c)])) for c in range(nc)]
    q_ = [qn[h][rs(c)] for c, h in items]
    k_ = [kn[h][rs(c)] for c, h in items]
    v_ = [cv[rs(c), 2 * B_KEY + h * B_DV:2 * B_KEY + (h + 1) * B_DV] for c, h in items]
    bh = [beta[rs(c), h:h + 1] for c, h in items]
    gc = [gcum_c[c][:, B_HEADS + h:B_HEADS + h + 1] for c, h in items]
    gr = [gcum_r[c][B_HEADS + h:B_HEADS + h + 1, :] for c, h in items]
    decay = [jnp.exp(jnp.where(lower, gc[n] - gr[n], NEG)) for n in range(n_items)]
    kb = [k_[n] * bh[n] for n in range(n_items)]
    kbf = [k_[n].astype(BF16) for n in range(n_items)]
    kq = [_dotg(jnp.concatenate([kb[n], q_[n]], axis=0).astype(BF16), kbf[n], NT_DIMS) for n in range(n_items)]
    a = [jnp.where(strict, kq[n][:CHUNK] * decay[n], 0.0) for n in range(n_items)]
    qk = [jnp.where(lower, kq[n][CHUNK:] * decay[n], 0.0).astype(BF16) for n in range(n_items)]
    doff = [-jnp.where((ii >> 1) == (jj >> 1), a[n], 0.0) for n in range(n_items)]
    for lg in range(1, 6):
        m = ((ii >> (lg + 1)) == (jj >> (lg + 1))) & ((ii >> lg) != (jj >> lg))
        bm = [jnp.where(m, a[n], 0.0) for n in range(n_items)]
        db = [doff[n].astype(BF16) for n in range(n_items)]
        y = [bm[n] + _dot(db[n], bm[n].astype(BF16)) for n in range(n_items)]
        x = [y[n] + _dot(y[n].astype(BF16), db[n]) for n in range(n_items)]
        doff = [doff[n] - x[n] for n in range(n_items)]
    egc = [jnp.exp(gc[n]) for n in range(n_items)]
    rhs = [jnp.concatenate([v_[n] * bh[n], kb[n] * egc[n]], axis=-1) for n in range(n_items)]
    sol = [rhs[n] + _dot(doff[n].astype(BF16), rhs[n].astype(BF16)) for n in range(n_items)]
    g_last = [gc[n][CHUNK - 1:CHUNK, :] for n in range(n_items)]
    kd = [(k_[n] * jnp.exp(g_last[n] - gc[n])).astype(BF16) for n in range(n_items)]
    wq = [jnp.concatenate([sol[n][:, B_DV:], q_[n] * egc[n]], axis=0).astype(BF16) for n in range(n_items)]
    eg_last = [jnp.exp(g_last[n]) for n in range(n_items)]

    s_cur = [s_scr[h] for h in range(B_HEADS)]
    for c in range(nc):
        idx = [c * B_HEADS + h for h in range(B_HEADS)]
        ws = [_dot(wq[n], s_cur[h].astype(BF16)) for h, n in enumerate(idx)]
        v_new = [sol[n][:, :B_DV] - ws[h][:CHUNK] for h, n in enumerate(idx)]
        vb = [v_new[h].astype(BF16) for h in range(B_HEADS)]
        o = [ws[h][CHUNK:] + _dot(qk[n], vb[h]) for h, n in enumerate(idx)]
        s_cur = [s_cur[h] * eg_last[n] + _dotg(kd[n], vb[h], TN_DIMS) for h, n in enumerate(idx)]
        for h in range(B_HEADS):
            z = zb[rs(c), h * B_DV:(h + 1) * B_DV]
            mix_scr[rs(c), A_WIDTH + h * B_DV:A_WIDTH + (h + 1) * B_DV] = (
                _rms(o[h], onorm_ref[...]) * _silu(z)).astype(BF16)
    for h in range(B_HEADS):
        s_scr[h] = s_cur[h]


def _ab_layer_kernel(x_ref, g_ref, w_ref, wg_ref, wgt_ref, sink_ref, hist_ref, convw_ref, gprm_ref,
                     gprmt_ref, onorm_ref, wout_ref, convin_ref, sin_ref,
                     y_ref, kvtail_ref, xbtail_ref, sout_ref,
                     s_scr, xc_scr, kvprev_scr, mix_scr, *, ct, n_lead, pos_shift):
    i = pl.program_id(1)
    nsteps = pl.num_programs(1)
    ntail = min(WINDOW, ct)

    @pl.when(i == 0)
    def _():
        s_scr[...] = sin_ref[0]
        xc_scr[0:8, :] = convin_ref[0]
        kvprev_scr[1] = hist_ref[...]

    x = x_ref[0]
    hb = _rms(x, g_ref[...]).astype(BF16)
    qa = _dot(hb, w_ref[:, O_QA:O_KV])
    kv = _dot(hb, w_ref[:, O_KV:O_ZA])
    za = _dot(hb, w_ref[:, O_ZA:O_XB])
    xb = _dot(hb, w_ref[:, O_XB:O_ZB])
    zb = _dot(hb, w_ref[:, O_ZB:AB_MAIN])
    gt = _dot(hb, wg_ref[...])
    gtt = _dotg(wgt_ref[...], hb, NT_DIMS)
    kvtail_ref[0] = kv[ct - ntail:]
    xbtail_ref[0] = xb[ct - 8:]

    slot = lax.rem(i, 2)
    kvb = jnp.concatenate([kvprev_scr[1 - slot], kv], axis=0).astype(BF16)
    if ct >= WINDOW:
        kvprev_scr[slot] = kv[ct - WINDOW:]
    _swa_tile(i, qa, kvb, za, sink_ref, mix_scr, ct=ct, sb=ntail, pos_shift=pos_shift)

    xc_scr[8:8 + ct, :] = xb
    w = convw_ref[...]
    yc = xc_scr[pl.ds(8 - (CONV_W - 1), ct), :] * w[0:1, :]
    for t in range(1, CONV_W):
        yc = yc + xc_scr[pl.ds(8 - (CONV_W - 1) + t, ct), :] * w[t:t + 1, :]
    cv = _silu(yc)
    xc_scr[0:8, :] = xc_scr[ct:ct + 8, :]
    _gdn_tile(i, cv, gt, gtt, zb, gprm_ref, gprmt_ref, onorm_ref, s_scr, mix_scr, ct=ct, n_lead=n_lead)

    y_ref[0] = x + _dot(mix_scr[...], wout_ref[...])

    @pl.when(i == nsteps - 1)
    def _():
        sout_ref[0] = s_scr[...]


def _swa_stages(i, qa, kvb, env, sink_ref, mix_scr, *, ct, sb, pos_shift):
    r = lax.broadcasted_iota(jnp.int32, (sb, WINDOW + sb), 0)
    c = lax.broadcasted_iota(jnp.int32, (sb, WINDOW + sb), 1)
    diff = r - c + WINDOW
    in_window = (diff >= 0) & (diff < WINDOW)
    nsb = ct // sb
    items = [(s, j) for s in range(nsb) for j in range(A_KV_HEADS)]
    masks = [jnp.concatenate([in_window & (i * ct + s * sb + c - WINDOW + pos_shift >= 0)] * A_GROUP, axis=0)
             for s in range(nsb)]
    sinks = [jnp.concatenate([jnp.broadcast_to(sink_ref[0:1, j * A_GROUP + g:j * A_GROUP + g + 1], (sb, 1))
                              for g in range(A_GROUP)], axis=0) for j in range(A_KV_HEADS)]
    sc = []
    for s, j in items:
        qs = jnp.concatenate([qa[s * sb:(s + 1) * sb, (j * A_GROUP + g) * A_HD:(j * A_GROUP + g + 1) * A_HD]
                              for g in range(A_GROUP)], axis=0).astype(BF16)
        sc.append(_dotg(qs, kvb[s * sb:s * sb + WINDOW + sb, j * A_HD:(j + 1) * A_HD], NT_DIMS) * (A_HD ** -0.5))
        yield
    p, esk = [], []
    for n, (s, j) in enumerate(items):
        scm = jnp.where(masks[s], sc[n], NEG)
        mx = jnp.maximum(jnp.max(scm, axis=-1, keepdims=True), sinks[j])
        p.append(jnp.exp(scm - mx).astype(BF16))
        esk.append(jnp.exp(sinks[j] - mx))
        yield
    pv = []
    ones = jnp.ones((WINDOW + sb, A_HD), BF16)
    for n, (s, j) in enumerate(items):
        vx = jnp.concatenate([kvb[s * sb:s * sb + WINDOW + sb,
                                  A_KV_WIDTH + j * A_HD:A_KV_WIDTH + (j + 1) * A_HD], ones], axis=-1)
        pvx = _dot(p[n], vx)
        pv.append(pvx[:, :A_HD] / (pvx[:, A_HD:] + esk[n]))
        yield
    env['pv'] = pv


def _swa_store(env, mix_scr, *, ct, sb):
    pv = env['pv']
    for s in range(ct // sb):
        o = jnp.concatenate([pv[s * A_KV_HEADS + j][g * sb:(g + 1) * sb]
                             for j in range(A_KV_HEADS) for g in range(A_GROUP)], axis=-1)
        mix_scr[s * sb:(s + 1) * sb, 0:A_WIDTH] = (o * _silu(env['za'][s * sb:(s + 1) * sb])).astype(BF16)


def _gdn_stages(i, cvb, gt, gtt, env, gprm_ref, gprmt_ref, onorm_ref, s_scr, mix_scr, *, ct, n_lead):
    nc = ct // CHUNK
    items = [(c, h) for c in range(nc) for h in range(B_HEADS)]
    n_items = len(items)
    rs = lambda c: slice(c * CHUNK, (c + 1) * CHUNK)
    qn, kn = [], []
    for h in range(B_HEADS):
        q = cvb[h]
        k = cvb[B_HEADS + h]
        qn.append(q * lax.rsqrt(jnp.sum(q * q, axis=-1, keepdims=True) + EPS) * (B_DK ** -0.5))
        kn.append(k * lax.rsqrt(jnp.sum(k * k, axis=-1, keepdims=True) + EPS))
        yield
    beta = _sigmoid(gt)
    rows = i * ct + lax.broadcasted_iota(jnp.int32, gt.shape, 0)
    gcol = jnp.where(rows >= n_lead, -gprm_ref[0:1, :] * _softplus(gt + gprm_ref[1:2, :]), 0.0)
    lanes = i * ct + lax.broadcasted_iota(jnp.int32, gtt.shape, 1)
    grow = jnp.where(lanes >= n_lead, -gprmt_ref[:, 0:1] * _softplus(gtt + gprmt_ref[:, 1:2]), 0.0)
    ii = lax.broadcasted_iota(jnp.int32, (CHUNK, CHUNK), 0)
    jj = lax.broadcasted_iota(jnp.int32, (CHUNK, CHUNK), 1)
    lower = ii >= jj
    strict = ii > jj
    ltri = lower.astype(BF16)
    utri = (ii <= jj).astype(BF16)
    gcum_c = [sum(_dot(ltri, p) for p in _split3(gcol[rs(c)])) for c in range(nc)]
    gcum_r = [sum(_dot(p, utri) for p in _split3(grow[:, rs(c)])) for c in range(nc)]
    yield
    q_ = [qn[h][rs(c)] for c, h in items]
    k_ = [kn[h][rs(c)] for c, h in items]
    v_ = [cvb[2 * B_HEADS + h][rs(c)] for c, h in items]
    bh = [beta[rs(c), h:h + 1] for c, h in items]
    gc = [gcum_c[c][:, B_HEADS + h:B_HEADS + h + 1] for c, h in items]
    gr = [gcum_r[c][B_HEADS + h:B_HEADS + h + 1, :] for c, h in items]
    kb, a, qk = [], [], []
    for n in range(n_items):
        decay = jnp.exp(jnp.where(lower, gc[n] - gr[n], NEG))
        kb.append(k_[n] * bh[n])
        kq = _dotg(jnp.concatenate([kb[n], q_[n]], axis=0).astype(BF16), k_[n].astype(BF16), NT_DIMS)
        a.append(jnp.where(strict, kq[:CHUNK] * decay, 0.0))
        qk.append(jnp.where(lower, kq[CHUNK:] * decay, 0.0).astype(BF16))
        if n % B_HEADS == B_HEADS - 1:
            yield
    doff = [-jnp.where((ii >> 1) == (jj >> 1), a[n], 0.0) for n in range(n_items)]
    for lg in range(1, 6):
        m = ((ii >> (lg + 1)) == (jj >> (lg + 1))) & ((ii >> lg) != (jj >> lg))
        bm = [jnp.where(m, a[n], 0.0) for n in range(n_items)]
        db = [doff[n].astype(BF16) for n in range(n_items)]
        y = [bm[n] + _dot(db[n], bm[n].astype(BF16)) for n in range(n_items)]
        yield
        x = [y[n] + _dot(y[n].astype(BF16), db[n]) for n in range(n_items)]
        doff = [doff[n] - x[n] for n in range(n_items)]
        yield
    egc = [jnp.exp(gc[n]) for n in range(n_items)]
    g_last = [gc[n][CHUNK - 1:CHUNK, :] for n in range(n_items)]
    eg_last = [jnp.exp(g_last[n]) for n in range(n_items)]
    sol, kd, wq = [], [], []
    for n in range(n_items):
        rhs = jnp.concatenate([v_[n] * bh[n], kb[n] * egc[n]], axis=-1)
        sol.append(rhs + _dot(doff[n].astype(BF16), rhs.astype(BF16)))
        kd.append((k_[n] * jnp.exp(g_last[n] - gc[n])).astype(BF16))
        wq.append(jnp.concatenate([sol[n][:, B_DV:], q_[n] * egc[n]], axis=0).astype(BF16))
        if n % B_HEADS == B_HEADS - 1:
            yield

    def gate_store(c, o):
        for h in range(B_HEADS):
            z = env['zb'][rs(c), h * B_DV:(h + 1) * B_DV]
            mix_scr[rs(c), A_WIDTH + h * B_DV:A_WIDTH + (h + 1) * B_DV] = (
                _rms(o[h], onorm_ref[...]) * _silu(z)).astype(BF16)

    s_cur = [s_scr[h] for h in range(B_HEADS)]
    o_prev = None
    for c in range(nc):
        idx = [c * B_HEADS + h for h in range(B_HEADS)]
        ws = [_dot(wq[n], s_cur[h].astype(BF16)) for h, n in enumerate(idx)]
        if o_prev is not None:
            gate_store(c - 1, o_prev)
        yield
        v_new = [sol[n][:, :B_DV] - ws[h][:CHUNK] for h, n in enumerate(idx)]
        vb = [v_new[h].astype(BF16) for h in range(B_HEADS)]
        o_prev = [ws[h][CHUNK:] + _dot(qk[n], vb[h]) for h, n in enumerate(idx)]
        s_cur = [s_cur[h] * eg_last[n] + _dotg(kd[n], vb[h], TN_DIMS) for h, n in enumerate(idx)]
        yield
    for h in range(B_HEADS):
        s_scr[h] = s_cur[h]
    gate_store(nc - 1, o_prev)
    yield


def _ab_layer_staged_kernel(x_ref, g_ref, w_ref, wg_ref, wgt_ref, sink_ref, hist_ref, convw_ref, gprm_ref,
                            gprmt_ref, onorm_ref, wout_ref, convin_ref, sin_ref,
                            y_ref, kvtail_ref, xbtail_ref, sout_ref,
                            s_scr, xc_scr, kvprev_scr, mix_scr, *, ct, n_lead, pos_shift):
    i = pl.program_id(1)
    nsteps = pl.num_programs(1)
    ntail = min(WINDOW, ct)

    @pl.when(i == 0)
    def _():
        s_scr[...] = sin_ref[0]
        xc_scr[0:8, :] = convin_ref[0]
        kvprev_scr[1] = hist_ref[...]

    x = x_ref[0]
    hb = _rms(x, g_ref[...]).astype(BF16)
    proj = lambda a, b: _dot(hb, w_ref[:, a:b])
    env = {}

    def step(gen, n=1):
        for _ in range(n):
            next(gen, None)

    nblk = B_CONV_CH // LANES
    w = convw_ref[...]
    cvb = []

    def conv_block(j):
        cs = slice(j * LANES, (j + 1) * LANES)
        yc = xc_scr[pl.ds(8 - (CONV_W - 1), ct), cs] * w[0:1, cs]
        for t in range(1, CONV_W):
            yc = yc + xc_scr[pl.ds(8 - (CONV_W - 1) + t, ct), cs] * w[t:t + 1, cs]
        cvb.append(_silu(yc))

    pw = 2 * LANES
    for j in range(nblk // 2):
        xbj = proj(O_XB + j * pw, O_XB + (j + 1) * pw)
        xc_scr[8:8 + ct, j * pw:(j + 1) * pw] = xbj
        xbtail_ref[0, :, j * pw:(j + 1) * pw] = xbj[ct - 8:]
        if j > 0:
            conv_block(2 * j - 2)
            conv_block(2 * j - 1)
    gt = _dot(hb, wg_ref[...])
    gtt = _dotg(wgt_ref[...], hb, NT_DIMS)
    conv_block(nblk - 2)
    conv_block(nblk - 1)
    xc_scr[0:8, :] = xc_scr[ct:ct + 8, :]

    gdn = _gdn_stages(i, cvb, gt, gtt, env, gprm_ref, gprmt_ref, onorm_ref, s_scr, mix_scr, ct=ct, n_lead=n_lead)
    qa_p = []
    for j in range(A_WIDTH // pw):
        qa_p.append(proj(O_QA + j * pw, O_QA + (j + 1) * pw))
        step(gdn, 2)
    qa = jnp.concatenate(qa_p, axis=-1)
    kv = proj(O_KV, O_ZA)
    kvtail_ref[0] = kv[ct - ntail:]
    slot = lax.rem(i, 2)
    kvb = jnp.concatenate([kvprev_scr[1 - slot], kv], axis=0).astype(BF16)
    if ct >= WINDOW:
        kvprev_scr[slot] = kv[ct - WINDOW:]
    swa = _swa_stages(i, qa, kvb, env, sink_ref, mix_scr, ct=ct, sb=ntail, pos_shift=pos_shift)
    n_sw = (ct // ntail) * A_KV_HEADS
    step(gdn)
    for _ in range(ct // CHUNK):
        step(gdn)
        step(swa)
    step(swa, max(n_sw - ct // CHUNK, 0))
    za_p, zb_p = [], []
    fill = ([lambda: step(swa)] * (2 * n_sw)
            + [lambda j=j: za_p.append(proj(O_ZA + j * pw, O_ZA + (j + 1) * pw)) for j in range(A_WIDTH // pw)]
            + [lambda j=j: zb_p.append(proj(O_ZB + j * pw, O_ZB + (j + 1) * pw)) for j in range(B_VAL // pw)])
    per = -(-len(fill) // 10)
    for lv in range(10):
        step(gdn)
        for f in fill[lv * per:(lv + 1) * per]:
            f()
    for f in fill[10 * per:]:
        f()
    env['za'] = jnp.concatenate(za_p, axis=-1)
    env['zb'] = jnp.concatenate(zb_p, axis=-1)
    step(gdn, ct // CHUNK)
    for _ in swa:
        pass
    _swa_store(env, mix_scr, ct=ct, sb=ntail)
    ya = x + _dot(mix_scr[:, 0:A_WIDTH], wout_ref[0:A_WIDTH, :])
    for _ in gdn:
        pass
    y_ref[0] = ya + _dot(mix_scr[:, A_WIDTH:], wout_ref[A_WIDTH:, :])

    @pl.when(i == nsteps - 1)
    def _():
        sout_ref[0] = s_scr[...]


def _ab_layer_prompt(x3, norm_g, w_main, w_gate, w_gate_t, sink, hist_kv, conv_w8, gprm, gprm_t, onorm,
                     w_out, conv_in, s_in, ct, n_lead, pos_shift):
    B, L, _ = x3.shape
    ntail = min(WINDOW, ct)
    kern = functools.partial(_ab_layer_staged_kernel, ct=ct, n_lead=n_lead, pos_shift=pos_shift)
    full = lambda a: pl.BlockSpec(a.shape, lambda b, i: (0,) * a.ndim)
    blk = pl.BlockSpec((1, ct, D_MODEL), lambda b, i: (b, i, 0))
    per_b = lambda *s: pl.BlockSpec((1,) + s, lambda b, i: (b,) + (0,) * len(s))
    return pl.pallas_call(
        kern,
        grid=(B, L // ct),
        in_specs=[blk, full(norm_g), full(w_main), full(w_gate), full(w_gate_t), full(sink), full(hist_kv),
                  full(conv_w8), full(gprm), full(gprm_t), full(onorm), full(w_out),
                  full(conv_in), full(s_in)],
        out_specs=[blk, per_b(ntail, 2 * A_KV_WIDTH), per_b(8, B_CONV_CH), per_b(B_HEADS, B_DK, B_DV)],
        out_shape=[jax.ShapeDtypeStruct((B, L, D_MODEL), F32),
                   jax.ShapeDtypeStruct((B, ntail, 2 * A_KV_WIDTH), F32),
                   jax.ShapeDtypeStruct((B, 8, B_CONV_CH), F32),
                   jax.ShapeDtypeStruct((B, B_HEADS, B_DK, B_DV), F32)],
        scratch_shapes=[pltpu.VMEM((B_HEADS, B_DK, B_DV), F32),
                        pltpu.VMEM((ct + 8, B_CONV_CH), F32),
                        pltpu.VMEM((2, WINDOW, 2 * A_KV_WIDTH), F32),
                        pltpu.VMEM((ct, AB_MIX), BF16)],
        compiler_params=_cparams(("arbitrary", "arbitrary")),
        name="ab_layer_prompt",
    )(x3, norm_g, w_main, w_gate, w_gate_t, sink, hist_kv, conv_w8, gprm, gprm_t, onorm, w_out, conv_in, s_in)


def _c_layer_kernel(x_ref, g_ref, w_ref, wgk_ref, wup_ref, bgk_ref, onorm_ref, wout_ref, gfin_ref,
                    sin_ref, y_ref, sout_ref, st_scr, og_scr, *, ct, n_lead):
    i = pl.program_id(1)
    nsteps = pl.num_programs(1)

    @pl.when(i == 0)
    def _():
        st_scr[...] = sin_ref[0]

    x = x_ref[0]
    hb = _rms(x, g_ref[...]).astype(BF16)
    low = _dot(hb, wgk_ref[...])
    q_all = _dot(hb, w_ref[:, 0:C_KEY])
    la_all = _gla_log_decay(low, wup_ref, bgk_ref)
    k_all = _dot(hb, w_ref[:, C_KEY:2 * C_KEY])
    v_h, z_h = [], []
    later = ([lambda h=h: v_h.append(_dot(hb, w_ref[:, 2 * C_KEY + h * C_DV:2 * C_KEY + (h + 1) * C_DV]).astype(BF16))
              for h in range(C_HEADS)]
             + [lambda h=h: z_h.append(_dot(hb, w_ref[:, 2 * C_KEY + C_VAL + h * C_DV:
                                                       2 * C_KEY + C_VAL + (h + 1) * C_DV]))
                for h in range(C_HEADS)])

    def run_later(n):
        for _ in range(n):
            if later:
                later.pop(0)()
    rows = i * ct + lax.broadcasted_iota(jnp.int32, la_all.shape, 0)
    la_all = jnp.where(rows >= n_lead, la_all, 0.0)

    ii = lax.broadcasted_iota(jnp.int32, (CHUNK, CHUNK), 0)
    jj = lax.broadcasted_iota(jnp.int32, (CHUNK, CHUNK), 1)
    lower = ii >= jj
    ltri = lower.astype(BF16)
    nc = ct // CHUNK
    rs = lambda c: slice(c * CHUNK, (c + 1) * CHUNK)
    ks = lambda h: slice(h * C_DK, (h + 1) * C_DK)
    vs = lambda h: slice(h * C_DV, (h + 1) * C_DV)
    items = [(c, h) for c in range(nc) for h in range(C_HEADS)]
    n_items = len(items)
    bc_all = [sum(_dot(ltri, p) for p in _split3(la_all[rs(c)])) for c in range(nc)]
    run_later(2)
    every =max(nc // C_HEADS, 1)
    qd, kinv, kd, gl, qk = [], [], [], [], []
    for n, (c, h) in enumerate(items):
        bc = bc_all[c][:, ks(h)]
        k = k_all[rs(c), ks(h)]
        bl = bc[CHUNK - 1:CHUNK, :]
        qd.append((q_all[rs(c), ks(h)] * (C_DK ** -0.5) * jnp.exp(bc)).astype(BF16))
        kinv.append((k * jnp.exp(-bc)).astype(BF16))
        kd.append((k * jnp.exp(bl - bc)).astype(BF16))
        gl.append(jnp.exp(bl))
        if h == C_HEADS - 1:
            qk += [jnp.where(lower, _dotg(qd[m], kinv[m], NT_DIMS), 0.0).astype(BF16)
                   for m in range(n - C_HEADS + 1, n + 1)]
            if c % every == every - 1:
                run_later(1)
    run_later(C_HEADS - len(v_h))
    v_ = [v_h[h][rs(c)] for c, h in items]
    o_intra = []
    for c in range(nc):
        o_intra += [_dot(qk[c * C_HEADS + h], v_[c * C_HEADS + h]) for h in range(C_HEADS)]
        if c % every == every - 1:
            run_later(1)
    run_later(len(later))

    rb = min(ct, 4 * CHUNK)

    def gate_store(c, o):
        for h in range(C_HEADS):
            og_scr[rs(c), vs(h)] = (_rms(o[h], onorm_ref[...]) * _silu(z_h[h][rs(c)])).astype(BF16)
        if ((c + 1) * CHUNK) % rb == 0:
            r = slice((c + 1) * CHUNK - rb, (c + 1) * CHUNK)
            y_ref[0, r, :] = _rms(x[r] + _dot(og_scr[r, :], wout_ref[...]), gfin_ref[...])

    st = [st_scr[h] for h in range(C_HEADS)]
    o_prev = None
    for c in range(nc):
        idx = [c * C_HEADS + h for h in range(C_HEADS)]
        o = [o_intra[n] + _dotg(qd[n], st[h].astype(BF16), NT_DIMS) for h, n in enumerate(idx)]
        st = [st[h] * gl[n] + _dotg(v_[n], kd[n], TN_DIMS) for h, n in enumerate(idx)]
        if o_prev is not None:
            gate_store(c - 1, o_prev)
        o_prev = o
    gate_store(nc - 1, o_prev)
    for h in range(C_HEADS):
        st_scr[h] = st[h]

    @pl.when(i == nsteps - 1)
    def _():
        sout_ref[0] = st_scr[...]


def _c_layer_prompt(x3, norm_g, w_main, w_gk, w_up, b_gk, onorm, w_out, final_g, st_in, ct, n_lead):
    B, L, _ = x3.shape
    kern = functools.partial(_c_layer_kernel, ct=ct, n_lead=n_lead)
    full = lambda a: pl.BlockSpec(a.shape, lambda b, i: (0,) * a.ndim)
    blk = pl.BlockSpec((1, ct, D_MODEL), lambda b, i: (b, i, 0))
    st = pl.BlockSpec((1, C_HEADS, C_DV, C_DK), lambda b, i: (b, 0, 0, 0))
    return pl.pallas_call(
        kern,
        grid=(B, L // ct),
        in_specs=[blk, full(norm_g), full(w_main), full(w_gk), full(w_up), full(b_gk), full(onorm),
                  full(w_out), full(final_g), full(st_in)],
        out_specs=[blk, st],
        out_shape=[jax.ShapeDtypeStruct((B, L, D_MODEL), F32),
                   jax.ShapeDtypeStruct((B, C_HEADS, C_DV, C_DK), F32)],
        scratch_shapes=[pltpu.VMEM((C_HEADS, C_DV, C_DK), F32), pltpu.VMEM((ct, C_VAL), BF16)],
        compiler_params=_cparams(("arbitrary", "arbitrary")),
        name="c_layer_prompt",
    )(x3, norm_g, w_main, w_gk, w_up, b_gk, onorm, w_out, final_g, st_in)


def _ab_proj_kernel(x_ref, g_ref, w_ref, wg_ref, qa_ref, kv_ref, za_ref, xb_ref, zb_ref, gates_ref):
    hb = _rms(x_ref[...], g_ref[...]).astype(BF16)
    qa_ref[...] = _dot(hb, w_ref[:, O_QA:O_KV])
    kv_ref[...] = _dot(hb, w_ref[:, O_KV:O_ZA])
    za_ref[...] = _dot(hb, w_ref[:, O_ZA:O_XB])
    xb_ref[...] = _dot(hb, w_ref[:, O_XB:O_ZB])
    zb_ref[...] = _dot(hb, w_ref[:, O_ZB:AB_MAIN])
    gates_ref[...] = _dot(hb, wg_ref[...])


def _ab_proj(x2d, norm_g, w_main, w_gate):
    n = x2d.shape[0]
    full = lambda a: pl.BlockSpec(a.shape, lambda: (0,) * a.ndim)
    widths = (A_WIDTH, 2 * A_KV_WIDTH, A_WIDTH, B_CONV_CH, B_VAL, LANES)
    args = (x2d, norm_g, w_main, w_gate)
    return pl.pallas_call(
        _ab_proj_kernel,
        in_specs=[full(a) for a in args],
        out_specs=[pl.BlockSpec((n, w), lambda: (0, 0)) for w in widths],
        out_shape=[jax.ShapeDtypeStruct((n, w), F32) for w in widths],
        compiler_params=pltpu.CompilerParams(vmem_limit_bytes=VMEM_LIMIT),
        name="ab_in_proj_sample",
    )(*args)


def _swa_sample_kernel(q_ref, kvn_ref, za_ref, ck_ref, cv_ref, sink_ref, o_ref, nk_ref, nv_ref, *, bt):
    row = lax.broadcasted_iota(jnp.int32, (WINDOW, A_KV_WIDTH), 0)
    hrow = lax.broadcasted_iota(jnp.int32, (A_HEADS, A_HD), 0)
    sk = sink_ref[...]
    nkb, nvb = [], []
    for b in range(bt):
        nk = jnp.where(row == WINDOW - 1, kvn_ref[b:b + 1, 0:A_KV_WIDTH], pltpu.roll(ck_ref[b], WINDOW - 1, 0))
        nv = jnp.where(row == WINDOW - 1, kvn_ref[b:b + 1, A_KV_WIDTH:], pltpu.roll(cv_ref[b], WINDOW - 1, 0))
        nk_ref[b] = nk
        nv_ref[b] = nv
        nkb.append(nk.astype(BF16))
        nvb.append(nv.astype(BF16))
    items = [(b, j) for b in range(bt) for j in range(A_KV_HEADS)]
    q8 = [q_ref[b].astype(BF16) for b in range(bt)]
    sc = [_dotg(q8[b], nkb[b][:, j * A_HD:(j + 1) * A_HD], NT_DIMS) * (A_HD ** -0.5) for b, j in items]
    m = [jnp.maximum(jnp.max(s, axis=-1, keepdims=True), sk) for s in sc]
    p = [jnp.exp(sc[n] - m[n]) for n in range(len(items))]
    den = [jnp.sum(p[n], axis=-1, keepdims=True) + jnp.exp(sk - m[n]) for n in range(len(items))]
    oj = [_dot(p[n].astype(BF16), nvb[b][:, j * A_HD:(j + 1) * A_HD]) / den[n] for n, (b, j) in enumerate(items)]
    for b in range(bt):
        o8 = jnp.where(hrow >= A_GROUP, oj[b * A_KV_HEADS + 1], oj[b * A_KV_HEADS])
        o_ref[b] = o8 * _silu(za_ref[b])


def _swa_sample(q3, kv_new, za3, cache_k, cache_v, sink_col, bt):
    nb = q3.shape[0]
    hd = pl.BlockSpec((bt, A_HEADS, A_HD), lambda i: (i, 0, 0))
    cache = pl.BlockSpec((bt, WINDOW, A_KV_WIDTH), lambda i: (i, 0, 0))
    return pl.pallas_call(
        functools.partial(_swa_sample_kernel, bt=bt),
        grid=(nb // bt,),
        in_specs=[hd, pl.BlockSpec((bt, 2 * A_KV_WIDTH), lambda i: (i, 0)), hd, cache, cache,
                  pl.BlockSpec(sink_col.shape, lambda i: (0, 0))],
        out_specs=[hd, cache, cache],
        out_shape=[jax.ShapeDtypeStruct(q3.shape, F32),
                   jax.ShapeDtypeStruct(cache_k.shape, F32),
                   jax.ShapeDtypeStruct(cache_v.shape, F32)],
        compiler_params=_cparams(("arbitrary",)),
        name="swa_sample",
    )(q3, kv_new, za3, cache_k, cache_v, sink_col)


def _gdn_sample_prep_kernel(xb_ref, h0_ref, h1_ref, h2_ref, gates_ref, convw_ref, gprm_ref,
                            q_ref, k_ref, v_ref, bg_ref):
    w = convw_ref[...]
    y = (h0_ref[...] * w[0:1, :] + h1_ref[...] * w[1:2, :] + h2_ref[...] * w[2:3, :]
         + xb_ref[...] * w[3:4, :])
    cv = _silu(y)
    for h in range(B_HEADS):
        q = cv[:, h * B_DK:(h + 1) * B_DK]
        k = cv[:, B_KEY + h * B_DK:B_KEY + (h + 1) * B_DK]
        q_ref[:, h * B_DK:(h + 1) * B_DK] = (
            q * lax.rsqrt(jnp.sum(q * q, axis=-1, keepdims=True) + EPS) * (B_DK ** -0.5))
        k_ref[:, h * B_DK:(h + 1) * B_DK] = k * lax.rsqrt(jnp.sum(k * k, axis=-1, keepdims=True) + EPS)
    v_ref[...] = cv[:, 2 * B_KEY:]
    gt = gates_ref[...]
    col = lax.broadcasted_iota(jnp.int32, gt.shape, 1)
    g = -gprm_ref[0:1, :] * _softplus(gt + gprm_ref[1:2, :])
    bg_ref[...] = jnp.where(col < B_HEADS, _sigmoid(gt), jnp.exp(g))


def _gdn_sample_prep(xb, h0, h1, h2, gates, conv_w8, gprm):
    n = xb.shape[0]
    full = lambda a: pl.BlockSpec(a.shape, lambda: (0,) * a.ndim)
    args = (xb, h0, h1, h2, gates, conv_w8, gprm)
    return pl.pallas_call(
        _gdn_sample_prep_kernel,
        in_specs=[full(a) for a in args],
        out_specs=[pl.BlockSpec((n, B_KEY), lambda: (0, 0))] * 3 + [pl.BlockSpec((n, LANES), lambda: (0, 0))],
        out_shape=[jax.ShapeDtypeStruct((n, B_KEY), F32)] * 3 + [jax.ShapeDtypeStruct((n, LANES), F32)],
        compiler_params=pltpu.CompilerParams(vmem_limit_bytes=VMEM_LIMIT),
        name="gdn_sample_prep",
    )(*args)


def _gdn_sample_kernel(s_ref, qt_ref, kt_ref, v_ref, bg_ref, zb_ref, onorm_ref, o_ref, sout_ref, *, bt):
    items = [(b, h) for b in range(bt) for h in range(B_HEADS)]
    hs = lambda h: slice(h * B_DV, (h + 1) * B_DV)
    kt = [kt_ref[:, hs(h)].T for h in range(B_HEADS)]
    qt = [qt_ref[:, hs(h)].T for h in range(B_HEADS)]
    kc = [kt[h][:, b:b + 1] for b, h in items]
    eg = [bg_ref[b:b + 1, B_HEADS + h:B_HEADS + h + 1] for b, h in items]
    ks = [jnp.sum(kc[n] * s_ref[b, h], axis=0, keepdims=True) for n, (b, h) in enumerate(items)]
    v_new = [bg_ref[b:b + 1, h:h + 1] * (v_ref[b:b + 1, hs(h)] - eg[n] * ks[n]) for n, (b, h) in enumerate(items)]
    o = []
    for n, (b, h) in enumerate(items):
        s_new = eg[n] * s_ref[b, h] + kc[n] * v_new[n]
        sout_ref[b, h] = s_new
        o.append(jnp.sum(qt[h][:, b:b + 1] * s_new, axis=0, keepdims=True))
    o_blk = jnp.concatenate([jnp.concatenate([o[b * B_HEADS + h] for h in range(B_HEADS)], axis=-1)
                             for b in range(bt)], axis=0)
    for h in range(B_HEADS):
        o_ref[:, hs(h)] = _rms(o_blk[:, hs(h)], onorm_ref[...]) * _silu(zb_ref[:, hs(h)])


def _gdn_sample(state, qt, kt, v, bg, zb, onorm, bt):
    nb = state.shape[0]
    st = pl.BlockSpec((bt, B_HEADS, B_DK, B_DV), lambda i: (i, 0, 0, 0))
    col = pl.BlockSpec((1, B_HEADS, B_DK, bt), lambda i: (i, 0, 0, 0))
    row = lambda n: pl.BlockSpec((bt, n), lambda i: (i, 0))
    return pl.pallas_call(
        functools.partial(_gdn_sample_kernel, bt=bt),
        grid=(nb // bt,),
        in_specs=[st, row(B_KEY), row(B_KEY), row(B_VAL), row(LANES), row(B_VAL),
                  pl.BlockSpec(onorm.shape, lambda i: (0, 0))],
        out_specs=[row(B_VAL), st],
        out_shape=[jax.ShapeDtypeStruct((nb, B_VAL), F32), jax.ShapeDtypeStruct(state.shape, F32)],
        compiler_params=_cparams(("arbitrary",)),
        name="gdn_sample",
    )(state, qt, kt, v, bg, zb, onorm)


def _ab_out_c_proj_kernel(x_ref, oa_ref, ob_ref, wout_ref, g_ref, w_ref, wgk_ref, wup_ref, bgk_ref,
                          y_ref, q_ref, k_ref, v_ref, z_ref, la_ref):
    y = (x_ref[...] + _dot(oa_ref[...].astype(BF16), wout_ref[0:A_WIDTH, :])
         + _dot(ob_ref[...].astype(BF16), wout_ref[A_WIDTH:, :]))
    y_ref[...] = y
    hb = _rms(y, g_ref[...]).astype(BF16)
    q_ref[...] = _dot(hb, w_ref[:, 0:C_KEY])
    k_ref[...] = _dot(hb, w_ref[:, C_KEY:2 * C_KEY])
    v_ref[...] = _dot(hb, w_ref[:, 2 * C_KEY:2 * C_KEY + C_VAL])
    z_ref[...] = _dot(hb, w_ref[:, 2 * C_KEY + C_VAL:C_MAIN])
    la_ref[...] = _gla_log_decay(_dot(hb, wgk_ref[...]), wup_ref, bgk_ref)


def _ab_out_c_proj(x2d, oa, ob, w_out, norm_g, w_main, w_gk, w_up, b_gk):
    n = x2d.shape[0]
    full = lambda a: pl.BlockSpec(a.shape, lambda: (0,) * a.ndim)
    widths = (D_MODEL, C_KEY, C_KEY, C_VAL, C_VAL, C_KEY)
    args = (x2d, oa, ob, w_out, norm_g, w_main, w_gk, w_up, b_gk)
    return pl.pallas_call(
        _ab_out_c_proj_kernel,
        in_specs=[full(a) for a in args],
        out_specs=[pl.BlockSpec((n, w), lambda: (0, 0)) for w in widths],
        out_shape=[jax.ShapeDtypeStruct((n, w), F32) for w in widths],
        compiler_params=pltpu.CompilerParams(vmem_limit_bytes=VMEM_LIMIT),
        name="ab_out_c_in_proj_sample",
    )(*args)


def _gla_sample_kernel(s_ref, qt_ref, kt_ref, lat_ref, v_ref, z_ref, onorm_ref, o_ref, sout_ref, *, bt):
    items = [(b, h) for b in range(bt) for h in range(C_HEADS)]
    hs = lambda h: slice(h * C_DV, (h + 1) * C_DV)
    ks = lambda h: slice(h * C_DK, (h + 1) * C_DK)
    at = [jnp.exp(lat_ref[:, ks(h)]).T for h in range(C_HEADS)]
    kt = [kt_ref[:, ks(h)].T for h in range(C_HEADS)]
    qt = [(qt_ref[:, ks(h)] * (C_DK ** -0.5)).T for h in range(C_HEADS)]
    o = []
    for n, (b, h) in enumerate(items):
        s_new = at[h][:, b:b + 1] * s_ref[b, h] + kt[h][:, b:b + 1] * v_ref[b:b + 1, hs(h)]
        sout_ref[b, h] = s_new
        o.append(jnp.sum(qt[h][:, b:b + 1] * s_new, axis=0, keepdims=True))
    o_blk = jnp.concatenate([jnp.concatenate([o[b * C_HEADS + h] for h in range(C_HEADS)], axis=-1)
                             for b in range(bt)], axis=0)
    for h in range(C_HEADS):
        o_ref[:, hs(h)] = _rms(o_blk[:, hs(h)], onorm_ref[...]) * _silu(z_ref[:, hs(h)])


def _gla_sample(state, qt, kt, lat, v, z, onorm, bt):
    nb = state.shape[0]
    st = pl.BlockSpec((bt, C_HEADS, C_DK, C_DV), lambda i: (i, 0, 0, 0))
    col = pl.BlockSpec((1, C_HEADS, C_DK, bt), lambda i: (i, 0, 0, 0))
    row = lambda n: pl.BlockSpec((bt, n), lambda i: (i, 0))
    return pl.pallas_call(
        functools.partial(_gla_sample_kernel, bt=bt),
        grid=(nb // bt,),
        in_specs=[st, row(C_KEY), row(C_KEY), row(C_KEY), row(C_VAL), row(C_VAL),
                  pl.BlockSpec(onorm.shape, lambda i: (0, 0))],
        out_specs=[row(C_VAL), st],
        out_shape=[jax.ShapeDtypeStruct((nb, C_VAL), F32), jax.ShapeDtypeStruct(state.shape, F32)],
        compiler_params=_cparams(("arbitrary",)),
        name="gla_sample",
    )(state, qt, kt, lat, v, z, onorm)


def _c_out_kernel(x_ref, o_ref, w_ref, g_ref, y_ref):
    y_ref[...] = _rms(x_ref[...] + _dot(o_ref[...].astype(BF16), w_ref[...]), g_ref[...])


def _c_out(x2d, oc, w_out, final_g):
    n = x2d.shape[0]
    full = lambda a: pl.BlockSpec(a.shape, lambda: (0,) * a.ndim)
    args = (x2d, oc, w_out, final_g)
    return pl.pallas_call(
        _c_out_kernel,
        in_specs=[full(a) for a in args],
        out_specs=pl.BlockSpec((n, D_MODEL), lambda: (0, 0)),
        out_shape=jax.ShapeDtypeStruct((n, D_MODEL), F32),
        compiler_params=pltpu.CompilerParams(vmem_limit_bytes=VMEM_LIMIT),
        name="c_out_proj_norm_sample",
    )(*args)


def _to_cols(a, heads, dk, bt):
    n = a.shape[0]
    return a.reshape(n // bt, bt, heads, dk).transpose(0, 2, 3, 1)


def kernel(x_prompt, x_sample, cache_swa_k, cache_swa_v, state_dn_conv, state_dn, state_gla,
           meta_tokens, norm_ab, w_in_ab, sink_a, conv_b, a_log_b, dt_bias_b, onorm_b, w_out_ab,
           norm_c, w_in_c, w_gk_up, b_gk, onorm_c, w_out_c, final_norm):
    Bp, L, _ = x_prompt.shape
    Bs = x_sample.shape[0]
    CT, BT = 256, 8

    w_ab = w_in_ab[0]
    w_ab_main = w_ab.astype(BF16)
    w_ab_gate = jnp.pad(w_ab[:, AB_MAIN:], ((0, 0), (0, LANES - 2 * B_HEADS))).astype(BF16)
    w_ab_gate_t = jnp.pad(w_ab[:, AB_MAIN:].T, ((0, 16 - 2 * B_HEADS), (0, 0))).astype(BF16)
    g_ab = norm_ab[0][None, :]
    sink_row = sink_a[0][None, :]
    sink_col = sink_a[0][:, None]
    conv_w8 = jnp.pad(conv_b[0], ((0, 8 - CONV_W), (0, 0)))
    gp = jnp.stack([jnp.exp(a_log_b[0]), dt_bias_b[0]])
    gprm = jnp.pad(gp, ((0, 6), (B_HEADS, LANES - 2 * B_HEADS)))
    gprm_t = jnp.pad(gp.T, ((B_HEADS, 16 - 2 * B_HEADS), (0, LANES - 2)))
    onb = onorm_b[0][None, :]
    w_ab_out = w_out_ab[0].astype(BF16)
    w_c = w_in_c[0]
    w_c_main = w_c.astype(BF16)
    w_c_gk = jnp.pad(w_c[:, C_MAIN:], ((0, 0), (0, LANES - C_RANK))).astype(BF16)
    w_up = jnp.pad(w_gk_up[0], ((0, LANES - C_RANK), (0, 0)))
    bgk = b_gk[0][None, :]
    g_c = norm_c[0][None, :]
    onc = onorm_c[0][None, :]
    w_c_out = w_out_c[0].astype(BF16)
    g_fin = final_norm[None, :]

    def ab_layer(x3, ct, hist_kv, pos_shift, conv_in, s_in, n_lead):
        return _ab_layer_prompt(x3, g_ab, w_ab_main, w_ab_gate, w_ab_gate_t, sink_row, hist_kv, conv_w8,
                                gprm, gprm_t, onb, w_ab_out, conv_in, s_in, ct, n_lead, pos_shift)

    def c_layer(x3, ct, st_in, n_lead):
        return _c_layer_prompt(x3, g_c, w_c_main, w_c_gk, w_up, bgk, onc, w_c_out, g_fin, st_in, ct, n_lead)

    xpre = jnp.concatenate([jnp.zeros((LEAD, D_MODEL), F32), meta_tokens], axis=0)[None]
    y_pre, kv_pre, xbtail_pre, sdn_pre = ab_layer(
        xpre, CHUNK, jnp.zeros((WINDOW, 2 * A_KV_WIDTH), F32), -LEAD,
        jnp.zeros((1, 8, B_CONV_CH), F32), jnp.zeros((1, B_HEADS, B_DK, B_DV), F32), LEAD)
    _, sgla_pre = c_layer(y_pre, CHUNK, jnp.zeros((1, C_HEADS, C_DV, C_DK), F32), LEAD)

    hist_kv = jnp.concatenate([jnp.zeros((WINDOW - CHUNK, 2 * A_KV_WIDTH), F32), kv_pre[0]], axis=0)
    y1, kv_tail, xb_tail, sdn_p = ab_layer(
        x_prompt, 2 * CT, hist_kv, N_META, xbtail_pre, sdn_pre, 0)
    y_prompt, sgla_t = c_layer(y1, 2 * CT, sgla_pre, 0)
    sgla_p = jnp.swapaxes(sgla_t, 2, 3)

    swa_k_p = kv_tail[:, :, :A_KV_WIDTH].reshape(1, Bp, WINDOW, A_KV_HEADS, A_HD)
    swa_v_p = kv_tail[:, :, A_KV_WIDTH:].reshape(1, Bp, WINDOW, A_KV_HEADS, A_HD)
    dn_conv_p = xb_tail[:, 8 - (CONV_W - 1):, :][None]

    xs = x_sample.reshape(Bs, D_MODEL)
    qa, kvn, za, xb, zb, gates = _ab_proj(xs, g_ab, w_ab_main, w_ab_gate)
    oa3, nk, nv = _swa_sample(qa.reshape(Bs, A_HEADS, A_HD), kvn, za.reshape(Bs, A_HEADS, A_HD),
                              cache_swa_k[0].reshape(Bs, WINDOW, A_KV_WIDTH),
                              cache_swa_v[0].reshape(Bs, WINDOW, A_KV_WIDTH), sink_col, BT)
    hist = state_dn_conv[0]
    qn, kn, vn, bg = _gdn_sample_prep(xb, hist[:, 0], hist[:, 1], hist[:, 2], gates, conv_w8, gprm)
    ob, sdn_s = _gdn_sample(state_dn[0], qn, kn,
                            vn, bg, zb, onb, BT)
    ys, qc, kc, vc, zc, la = _ab_out_c_proj(xs, oa3.reshape(Bs, A_WIDTH), ob, w_ab_out,
                                            g_c, w_c_main, w_c_gk, w_up, bgk)
    ocs, sgla_s = _gla_sample(state_gla[0], qc, kc, la, vc, zc, onc, BT)
    y_sample = _c_out(ys, ocs, w_c_out, g_fin).reshape(Bs, 1, D_MODEL)
    dn_conv_s = jnp.concatenate([hist[:, 1:], xb[:, None, :]], axis=1)[None]

    return (y_prompt, y_sample, swa_k_p, swa_v_p, dn_conv_p, sdn_p[None], sgla_p[None],
            nk.reshape(1, Bs, WINDOW, A_KV_HEADS, A_HD), nv.reshape(1, Bs, WINDOW, A_KV_HEADS, A_HD),
            dn_conv_s, sdn_s[None], sgla_s[None])
```

```python
import functools

import jax
import jax.numpy as jnp
from jax import lax
from jax.experimental import pallas as pl
from jax.experimental.pallas import tpu as pltpu

F32 = jnp.float32
BF16 = jnp.bfloat16
EPS = 1e-6

D_MODEL = 1024
N_META = 16
CHUNK = 64
LEAD = (-N_META) % CHUNK
A_HEADS, A_KV_HEADS, A_HD = 8, 2, 64
A_GROUP = A_HEADS // A_KV_HEADS
A_WIDTH = A_HEADS * A_HD
A_KV_WIDTH = A_KV_HEADS * A_HD
WINDOW = 128
B_HEADS, B_DK, B_DV = 4, 128, 128
B_KEY = B_HEADS * B_DK
B_VAL = B_HEADS * B_DV
CONV_W = 4
B_CONV_CH = 2 * B_KEY + B_VAL
C_HEADS, C_DK, C_DV = 4, 128, 256
C_KEY = C_HEADS * C_DK
C_VAL = C_HEADS * C_DV
C_RANK = 16
C_GATE_NORM = 16.0
AB_MAIN = 2 * A_WIDTH + 2 * A_KV_WIDTH + B_CONV_CH + B_VAL
AB_MIX = A_WIDTH + B_VAL
C_MAIN = 2 * C_KEY + 2 * C_VAL
O_QA, O_KV, O_ZA, O_XB, O_ZB = 0, 512, 768, 1280, 2816
LANES = 128
NEG = -1e30

VMEM_LIMIT = 56 * 1024 * 1024

NT_DIMS = (((1,), (1,)), ((), ()))
TN_DIMS = (((0,), (0,)), ((), ()))


def _cparams(sem, **kw):
    return pltpu.CompilerParams(dimension_semantics=sem, vmem_limit_bytes=VMEM_LIMIT, **kw)


def _dot(a, b):
    return jnp.dot(a, b, preferred_element_type=F32)


def _dotg(a, b, dims):
    return lax.dot_general(a, b, dims, preferred_element_type=F32)


def _split3(a):
    hi = a.astype(BF16)
    r = a - hi.astype(F32)
    mid = r.astype(BF16)
    lo = (r - mid.astype(F32)).astype(BF16)
    return hi, mid, lo


def _sigmoid(x):
    return 1.0 / (1.0 + jnp.exp(-x))


def _silu(x):
    return x * _sigmoid(x)


def _softplus(x):
    return jnp.maximum(x, 0.0) + jnp.log1p(jnp.exp(-jnp.abs(x)))


def _log_sigmoid(x):
    return jnp.minimum(x, 0.0) - jnp.log1p(jnp.exp(-jnp.abs(x)))


def _rms(x, g):
    return x * lax.rsqrt(jnp.mean(x * x, axis=-1, keepdims=True) + EPS) * g


def _gla_log_decay(low, wup_ref, bgk_ref):
    lh, lm, _ = _split3(low)
    wh, wm, _ = _split3(wup_ref[...])
    up = _dot(lh, wh) + _dot(lh, wm) + _dot(lm, wh)
    return _log_sigmoid(up + bgk_ref[...]) * (1.0 / C_GATE_NORM)


def _swa_tile(i, qa, kvb, za, sink_ref, mix_scr, *, ct, sb, pos_shift):
    r = lax.broadcasted_iota(jnp.int32, (sb, WINDOW + sb), 0)
    c = lax.broadcasted_iota(jnp.int32, (sb, WINDOW + sb), 1)
    diff = r - c + WINDOW
    in_window = (diff >= 0) & (diff < WINDOW)
    nsb = ct // sb
    items = [(s, j) for s in range(nsb) for j in range(A_KV_HEADS)]
    masks = [jnp.concatenate([in_window & (i * ct + s * sb + c - WINDOW + pos_shift >= 0)] * A_GROUP, axis=0)
             for s in range(nsb)]
    sinks = [jnp.concatenate([jnp.broadcast_to(sink_ref[0:1, j * A_GROUP + g:j * A_GROUP + g + 1], (sb, 1))
                              for g in range(A_GROUP)], axis=0) for j in range(A_KV_HEADS)]
    qs = [jnp.concatenate([qa[s * sb:(s + 1) * sb, (j * A_GROUP + g) * A_HD:(j * A_GROUP + g + 1) * A_HD]
                           for g in range(A_GROUP)], axis=0).astype(BF16) for s, j in items]
    sc = [_dotg(qs[n], kvb[s * sb:s * sb + WINDOW + sb, j * A_HD:(j + 1) * A_HD], NT_DIMS) * (A_HD ** -0.5)
          for n, (s, j) in enumerate(items)]
    sc = [jnp.where(masks[s], sc[n], NEG) for n, (s, j) in enumerate(items)]
    mx = [jnp.maximum(jnp.max(sc[n], axis=-1, keepdims=True), sinks[j]) for n, (s, j) in enumerate(items)]
    p = [jnp.exp(sc[n] - mx[n]) for n in range(len(items))]
    den = [jnp.sum(p[n], axis=-1, keepdims=True) + jnp.exp(sinks[j] - mx[n]) for n, (s, j) in enumerate(items)]
    pv = [_dot(p[n].astype(BF16),
               kvb[s * sb:s * sb + WINDOW + sb, A_KV_WIDTH + j * A_HD:A_KV_WIDTH + (j + 1) * A_HD]) / den[n]
          for n, (s, j) in enumerate(items)]
    for s in range(nsb):
        o = jnp.concatenate([pv[s * A_KV_HEADS + j][g * sb:(g + 1) * sb]
                             for j in range(A_KV_HEADS) for g in range(A_GROUP)], axis=-1)
        mix_scr[s * sb:(s + 1) * sb, 0:A_WIDTH] = (o * _silu(za[s * sb:(s + 1) * sb])).astype(BF16)


def _gdn_tile(i, cv, gt, gtt, zb, gprm_ref, gprmt_ref, onorm_ref, s_scr, mix_scr, *, ct, n_lead):
    beta = _sigmoid(gt)
    rows = i * ct + lax.broadcasted_iota(jnp.int32, gt.shape, 0)
    gcol = jnp.where(rows >= n_lead, -gprm_ref[0:1, :] * _softplus(gt + gprm_ref[1:2, :]), 0.0)
    lanes = i * ct + lax.broadcasted_iota(jnp.int32, gtt.shape, 1)
    grow = jnp.where(lanes >= n_lead, -gprmt_ref[:, 0:1] * _softplus(gtt + gprmt_ref[:, 1:2]), 0.0)

    ii = lax.broadcasted_iota(jnp.int32, (CHUNK, CHUNK), 0)
    jj = lax.broadcasted_iota(jnp.int32, (CHUNK, CHUNK), 1)
    lower = ii >= jj
    strict = ii > jj
    ltri = lower.astype(BF16)
    utri = (ii <= jj).astype(BF16)

    nc = ct // CHUNK
    items = [(c, h) for c in range(nc) for h in range(B_HEADS)]
    n_items = len(items)
    rs = lambda c: slice(c * CHUNK, (c + 1) * CHUNK)
    qn, kn = [], []
    for h in range(B_HEADS):
        q = cv[:, h * B_DK:(h + 1) * B_DK]
        k = cv[:, B_KEY + h * B_DK:B_KEY + (h + 1) * B_DK]
        qn.append(q * lax.rsqrt(jnp.sum(q * q, axis=-1, keepdims=True) + EPS) * (B_DK ** -0.5))
        kn.append(k * lax.rsqrt(jnp.sum(k * k, axis=-1, keepdims=True) + EPS))
    gcum_c = [sum(_dot(ltri, p) for p in _split3(gcol[rs(c)])) for c in range(nc)]
    gcum_r = [sum(_dot(p, utri) for p in _split3(grow[:, rs(c)])) for c in range(nc)]
    q_ = [qn[h][rs(c)] for c, h in items]
    k_ = [kn[h][rs(c)] for c, h in items]
    v_ = [cv[rs(c), 2 * B_KEY + h * B_DV:2 * B_KEY + (h + 1) * B_DV] for c, h in items]
    bh = [beta[rs(c), h:h + 1] for c, h in items]
    gc = [gcum_c[c][:, B_HEADS + h:B_HEADS + h + 1] for c, h in items]
    gr = [gcum_r[c][B_HEADS + h:B_HEADS + h + 1, :] for c, h in items]
    decay = [jnp.exp(jnp.where(lower, gc[n] - gr[n], NEG)) for n in range(n_items)]
    kb = [k_[n] * bh[n] for n in range(n_items)]
    kbf = [k_[n].astype(BF16) for n in range(n_items)]
    kq = [_dotg(jnp.concatenate([kb[n], q_[n]], axis=0).astype(BF16), kbf[n], NT_DIMS) for n in range(n_items)]
    a = [jnp.where(strict, kq[n][:CHUNK] * decay[n], 0.0) for n in range(n_items)]
    qk = [jnp.where(lower, kq[n][CHUNK:] * decay[n], 0.0).astype(BF16) for n in range(n_items)]
    doff = [-jnp.where((ii >> 1) == (jj >> 1), a[n], 0.0) for n in range(n_items)]
    for lg in range(1, 6):
        m = ((ii >> (lg + 1)) == (jj >> (lg + 1))) & ((ii >> lg) != (jj >> lg))
        bm = [jnp.where(m, a[n], 0.0) for n in range(n_items)]
        db = [doff[n].astype(BF16) for n in range(n_items)]
        y = [bm[n] + _dot(db[n], bm[n].astype(BF16)) for n in range(n_items)]
        x = [y[n] + _dot(y[n].astype(BF16), db[n]) for n in range(n_items)]
        doff = [doff[n] - x[n] for n in range(n_items)]
    egc = [jnp.exp(gc[n]) for n in range(n_items)]
    rhs = [jnp.concatenate([v_[n] * bh[n], kb[n] * egc[n]], axis=-1) for n in range(n_items)]
    sol = [rhs[n] + _dot(doff[n].astype(BF16), rhs[n].astype(BF16)) for n in range(n_items)]
    g_last = [gc[n][CHUNK - 1:CHUNK, :] for n in range(n_items)]
    kd = [(k_[n] * jnp.exp(g_last[n] - gc[n])).astype(BF16) for n in range(n_items)]
    wq = [jnp.concatenate([sol[n][:, B_DV:], q_[n] * egc[n]], axis=0).astype(BF16) for n in range(n_items)]
    eg_last = [jnp.exp(g_last[n]) for n in range(n_items)]

    s_cur = [s_scr[h] for h in range(B_HEADS)]
    for c in range(nc):
        idx = [c * B_HEADS + h for h in range(B_HEADS)]
        ws = [_dot(wq[n], s_cur[h].astype(BF16)) for h, n in enumerate(idx)]
        v_new = [sol[n][:, :B_DV] - ws[h][:CHUNK] for h, n in enumerate(idx)]
        vb = [v_new[h].astype(BF16) for h in range(B_HEADS)]
        o = [ws[h][CHUNK:] + _dot(qk[n], vb[h]) for h, n in enumerate(idx)]
        s_cur = [s_cur[h] * eg_last[n] + _dotg(kd[n], vb[h], TN_DIMS) for h, n in enumerate(idx)]
        for h in range(B_HEADS):
            z = zb[rs(c), h * B_DV:(h + 1) * B_DV]
            mix_scr[rs(c), A_WIDTH + h * B_DV:A_WIDTH + (h + 1) * B_DV] = (
                _rms(o[h], onorm_ref[...]) * _silu(z)).astype(BF16)
    for h in range(B_HEADS):
        s_scr[h] = s_cur[h]


def _ab_layer_kernel(x_ref, g_ref, w_ref, wg_ref, wgt_ref, sink_ref, hist_ref, convw_ref, gprm_ref,
                     gprmt_ref, onorm_ref, wout_ref, convin_ref, sin_ref,
                     y_ref, kvtail_ref, xbtail_ref, sout_ref,
                     s_scr, xc_scr, kvprev_scr, mix_scr, *, ct, n_lead, pos_shift):
    i = pl.program_id(1)
    nsteps = pl.num_programs(1)
    ntail = min(WINDOW, ct)

    @pl.when(i == 0)
    def _():
        s_scr[...] = sin_ref[0]
        xc_scr[0:8, :] = convin_ref[0]
        kvprev_scr[1] = hist_ref[...]

    x = x_ref[0]
    hb = _rms(x, g_ref[...]).astype(BF16)
    qa = _dot(hb, w_ref[:, O_QA:O_KV])
    kv = _dot(hb, w_ref[:, O_KV:O_ZA])
    za = _dot(hb, w_ref[:, O_ZA:O_XB])
    xb = _dot(hb, w_ref[:, O_XB:O_ZB])
    zb = _dot(hb, w_ref[:, O_ZB:AB_MAIN])
    gt = _dot(hb, wg_ref[...])
    gtt = _dotg(wgt_ref[...], hb, NT_DIMS)
    kvtail_ref[0] = kv[ct - ntail:]
    xbtail_ref[0] = xb[ct - 8:]

    slot = lax.rem(i, 2)
    kvb = jnp.concatenate([kvprev_scr[1 - slot], kv], axis=0).astype(BF16)
    if ct >= WINDOW:
        kvprev_scr[slot] = kv[ct - WINDOW:]
    _swa_tile(i, qa, kvb, za, sink_ref, mix_scr, ct=ct, sb=ntail, pos_shift=pos_shift)

    xc_scr[8:8 + ct, :] = xb
    w = convw_ref[...]
    yc = xc_scr[pl.ds(8 - (CONV_W - 1), ct), :] * w[0:1, :]
    for t in range(1, CONV_W):
        yc = yc + xc_scr[pl.ds(8 - (CONV_W - 1) + t, ct), :] * w[t:t + 1, :]
    cv = _silu(yc)
    xc_scr[0:8, :] = xc_scr[ct:ct + 8, :]
    _gdn_tile(i, cv, gt, gtt, zb, gprm_ref, gprmt_ref, onorm_ref, s_scr, mix_scr, ct=ct, n_lead=n_lead)

    y_ref[0] = x + _dot(mix_scr[...], wout_ref[...])

    @pl.when(i == nsteps - 1)
    def _():
        sout_ref[0] = s_scr[...]


def _swa_stages(i, qa, kvb, env, sink_ref, mix_scr, *, ct, sb, pos_shift):
    r = lax.broadcasted_iota(jnp.int32, (sb, WINDOW + sb), 0)
    c = lax.broadcasted_iota(jnp.int32, (sb, WINDOW + sb), 1)
    diff = r - c + WINDOW
    in_window = (diff >= 0) & (diff < WINDOW)
    nsb = ct // sb
    items = [(s, j) for s in range(nsb) for j in range(A_KV_HEADS)]
    masks = [jnp.concatenate([in_window & (i * ct + s * sb + c - WINDOW + pos_shift >= 0)] * A_GROUP, axis=0)
             for s in range(nsb)]
    sinks = [jnp.concatenate([jnp.broadcast_to(sink_ref[0:1, j * A_GROUP + g:j * A_GROUP + g + 1], (sb, 1))
                              for g in range(A_GROUP)], axis=0) for j in range(A_KV_HEADS)]
    sc = []
    for s, j in items:
        qs = (jnp.concatenate([qa[s * sb:(s + 1) * sb, (j * A_GROUP + g) * A_HD:(j * A_GROUP + g + 1) * A_HD]
                               for g in range(A_GROUP)], axis=0) * (A_HD ** -0.5)).astype(BF16)
        sc.append(_dotg(qs, kvb[s * sb:s * sb + WINDOW + sb, j * A_HD:(j + 1) * A_HD], NT_DIMS))
        yield
    p, esk = [], []
    for n, (s, j) in enumerate(items):
        scm = jnp.where(masks[s], sc[n], NEG)
        mx = jnp.maximum(jnp.max(scm, axis=-1, keepdims=True), sinks[j])
        p.append(jnp.exp(scm - mx).astype(BF16))
        esk.append(jnp.exp(sinks[j] - mx))
        yield
    pv = []
    ones = jnp.ones((WINDOW + sb, A_HD), BF16)
    for n, (s, j) in enumerate(items):
        vx = jnp.concatenate([kvb[s * sb:s * sb + WINDOW + sb,
                                  A_KV_WIDTH + j * A_HD:A_KV_WIDTH + (j + 1) * A_HD], ones], axis=-1)
        pvx = _dot(p[n], vx)
        pv.append(pvx[:, :A_HD] / (pvx[:, A_HD:] + esk[n]))
        yield
    env['pv'] = pv


def _swa_store(env, mix_scr, *, ct, sb):
    pv = env['pv']
    for s in range(ct // sb):
        o = jnp.concatenate([pv[s * A_KV_HEADS + j][g * sb:(g + 1) * sb]
                             for j in range(A_KV_HEADS) for g in range(A_GROUP)], axis=-1)
        mix_scr[s * sb:(s + 1) * sb, 0:A_WIDTH] = (o * _silu(env['za'][s * sb:(s + 1) * sb])).astype(BF16)


def _gdn_stages(i, cvb, gt, gtt, env, gprm_ref, gprmt_ref, onorm_ref, s_scr, mix_scr, *, ct, n_lead):
    nc = ct // CHUNK
    items = [(c, h) for c in range(nc) for h in range(B_HEADS)]
    n_items = len(items)
    rs = lambda c: slice(c * CHUNK, (c + 1) * CHUNK)
    qn, kn = [], []
    for h in range(B_HEADS):
        q = cvb[h]
        k = cvb[B_HEADS + h]
        qn.append(q * lax.rsqrt(jnp.sum(q * q, axis=-1, keepdims=True) + EPS) * (B_DK ** -0.5))
        kn.append(k * lax.rsqrt(jnp.sum(k * k, axis=-1, keepdims=True) + EPS))
        yield
    beta = _sigmoid(gt)
    rows = i * ct + lax.broadcasted_iota(jnp.int32, gt.shape, 0)
    gcol = jnp.where(rows >= n_lead, -gprm_ref[0:1, :] * _softplus(gt + gprm_ref[1:2, :]), 0.0)
    lanes = i * ct + lax.broadcasted_iota(jnp.int32, gtt.shape, 1)
    grow = jnp.where(lanes >= n_lead, -gprmt_ref[:, 0:1] * _softplus(gtt + gprmt_ref[:, 1:2]), 0.0)
    ii = lax.broadcasted_iota(jnp.int32, (CHUNK, CHUNK), 0)
    jj = lax.broadcasted_iota(jnp.int32, (CHUNK, CHUNK), 1)
    lower = ii >= jj
    strict = ii > jj
    ltri = lower.astype(BF16)
    utri = (ii <= jj).astype(BF16)
    gcum_c = [sum(_dot(ltri, p) for p in _split3(gcol[rs(c)])) for c in range(nc)]
    gcum_r = [sum(_dot(p, utri) for p in _split3(grow[:, rs(c)])) for c in range(nc)]
    yield
    q_ = [qn[h][rs(c)] for c, h in items]
    k_ = [kn[h][rs(c)] for c, h in items]
    v_ = [cvb[2 * B_HEADS + h][rs(c)] for c, h in items]
    bh = [beta[rs(c), h:h + 1] for c, h in items]
    gc = [gcum_c[c][:, B_HEADS + h:B_HEADS + h + 1] for c, h in items]
    gr = [gcum_r[c][B_HEADS + h:B_HEADS + h + 1, :] for c, h in items]
    kb, a, qk = [], [], []
    for n in range(n_items):
        decay = jnp.exp(jnp.where(lower, gc[n] - gr[n], NEG))
        kb.append(k_[n] * bh[n])
        kq = _dotg(jnp.concatenate([kb[n], q_[n]], axis=0).astype(BF16), k_[n].astype(BF16), NT_DIMS)
        a.append(jnp.where(strict, kq[:CHUNK] * decay, 0.0))
        qk.append(jnp.where(lower, kq[CHUNK:] * decay, 0.0).astype(BF16))
        if n % B_HEADS == B_HEADS - 1:
            yield
    doff = [-jnp.where((ii >> 1) == (jj >> 1), a[n], 0.0) for n in range(n_items)]
    for lg in range(1, 6):
        m = ((ii >> (lg + 1)) == (jj >> (lg + 1))) & ((ii >> lg) != (jj >> lg))
        bm = [jnp.where(m, a[n], 0.0) for n in range(n_items)]
        db = [doff[n].astype(BF16) for n in range(n_items)]
        y = [bm[n] + _dot(db[n], bm[n].astype(BF16)) for n in range(n_items)]
        yield
        x = [y[n] + _dot(y[n].astype(BF16), db[n]) for n in range(n_items)]
        doff = [doff[n] - x[n] for n in range(n_items)]
        yield
    egc = [jnp.exp(gc[n]) for n in range(n_items)]
    g_last = [gc[n][CHUNK - 1:CHUNK, :] for n in range(n_items)]
    eg_last = [jnp.exp(g_last[n]) for n in range(n_items)]
    sol, kd, wq = [], [], []
    for n in range(n_items):
        rhs = jnp.concatenate([v_[n] * bh[n], kb[n] * egc[n]], axis=-1)
        sol.append(rhs + _dot(doff[n].astype(BF16), rhs.astype(BF16)))
        kd.append((k_[n] * jnp.exp(g_last[n] - gc[n])).astype(BF16))
        wq.append(jnp.concatenate([sol[n][:, B_DV:], q_[n] * egc[n]], axis=0).astype(BF16))
        if n % B_HEADS == B_HEADS - 1:
            yield

    def gate_store(c, o):
        for h in range(B_HEADS):
            z = env['zb'][rs(c), h * B_DV:(h + 1) * B_DV]
            mix_scr[rs(c), A_WIDTH + h * B_DV:A_WIDTH + (h + 1) * B_DV] = (
                _rms(o[h], onorm_ref[...]) * _silu(z)).astype(BF16)

    s_cur = [s_scr[h] for h in range(B_HEADS)]
    o_prev = None
    for c in range(nc):
        idx = [c * B_HEADS + h for h in range(B_HEADS)]
        ws = [_dot(wq[n], s_cur[h].astype(BF16)) for h, n in enumerate(idx)]
        if o_prev is not None:
            gate_store(c - 1, o_prev)
        yield
        v_new = [sol[n][:, :B_DV] - ws[h][:CHUNK] for h, n in enumerate(idx)]
        vb = [v_new[h].astype(BF16) for h in range(B_HEADS)]
        o_prev = [ws[h][CHUNK:] + _dot(qk[n], vb[h]) for h, n in enumerate(idx)]
        s_cur = [s_cur[h] * eg_last[n] + _dotg(kd[n], vb[h], TN_DIMS) for h, n in enumerate(idx)]
        yield
    for h in range(B_HEADS):
        s_scr[h] = s_cur[h]
    gate_store(nc - 1, o_prev)
    yield


def _ab_layer_staged_kernel(x_ref, g_ref, w_ref, wg_ref, wgt_ref, sink_ref, hist_ref, convw_ref, gprm_ref,
                            gprmt_ref, onorm_ref, wout_ref, convin_ref, sin_ref,
                            y_ref, kvtail_ref, xbtail_ref, sout_ref,
                            s_scr, xc_scr, kvprev_scr, mix_scr, *, ct, n_lead, pos_shift):
    i = pl.program_id(1)
    nsteps = pl.num_programs(1)
    ntail = min(WINDOW, ct)

    @pl.when(i == 0)
    def _():
        s_scr[...] = sin_ref[0]
        xc_scr[0:8, :] = convin_ref[0]
        kvprev_scr[1] = hist_ref[...]

    x = x_ref[0]
    hb = _rms(x, g_ref[...]).astype(BF16)
    proj = lambda a, b: _dot(hb, w_ref[:, a:b])
    env = {}

    def step(gen, n=1):
        for _ in range(n):
            next(gen, None)

    nblk = B_CONV_CH // LANES
    w = convw_ref[...]
    cvb = []

    xblk = []
    row8 = lax.broadcasted_iota(jnp.int32, (8, LANES), 0)

    def conv_block(j):
        cs = slice(j * LANES, (j + 1) * LANES)
        xj = xblk[j]
        h8 = xc_scr[0:8, cs]
        yc = xj * w[CONV_W - 1:CONV_W, cs]
        for k in range(1, CONV_W):
            sh = pltpu.roll(xj, k, 0)
            top = jnp.where(row8 < k, pltpu.roll(h8, k, 0), sh[0:8])
            yc = yc + jnp.concatenate([top, sh[8:]], axis=0) * w[CONV_W - 1 - k:CONV_W - k, cs]
        cvb.append(_silu(yc))
        xc_scr[0:8, cs] = xj[ct - 8:]

    pw = 2 * LANES
    for j in range(nblk // 2):
        xbj = proj(O_XB + j * pw, O_XB + (j + 1) * pw)
        xblk += [xbj[:, 0:LANES], xbj[:, LANES:]]
        xbtail_ref[0, :, j * pw:(j + 1) * pw] = xbj[ct - 8:]
        if j > 0:
            conv_block(2 * j - 2)
            conv_block(2 * j - 1)
    gt = _dot(hb, wg_ref[...])
    gtt = _dotg(wgt_ref[...], hb, NT_DIMS)
    conv_block(nblk - 2)
    conv_block(nblk - 1)

    gdn = _gdn_stages(i, cvb, gt, gtt, env, gprm_ref, gprmt_ref, onorm_ref, s_scr, mix_scr, ct=ct, n_lead=n_lead)
    qa_p = []
    for j in range(A_WIDTH // pw):
        qa_p.append(proj(O_QA + j * pw, O_QA + (j + 1) * pw))
        step(gdn, 2)
    qa = jnp.concatenate(qa_p, axis=-1)
    kv = proj(O_KV, O_ZA)
    kvtail_ref[0] = kv[ct - ntail:]
    slot = lax.rem(i, 2)
    kvb = jnp.concatenate([kvprev_scr[1 - slot], kv], axis=0).astype(BF16)
    if ct >= WINDOW:
        kvprev_scr[slot] = kv[ct - WINDOW:]
    swa = _swa_stages(i, qa, kvb, env, sink_ref, mix_scr, ct=ct, sb=ntail, pos_shift=pos_shift)
    n_sw = (ct // ntail) * A_KV_HEADS
    step(gdn)
    for _ in range(ct // CHUNK):
        step(gdn)
        step(swa)
    step(swa, max(n_sw - ct // CHUNK, 0))
    za_p, zb_p = [], []
    fill = ([lambda j=j: za_p.append(proj(O_ZA + j * pw, O_ZA + (j + 1) * pw)) for j in range(A_WIDTH // pw)]
            + [lambda j=j: zb_p.append(proj(O_ZB + j * pw, O_ZB + (j + 1) * pw)) for j in range(B_VAL // pw)])
    for lv in range(10):
        step(gdn)
        if lv % 2 == 1 and fill:
            fill.pop(0)()
    for f in fill:
        f()
    env['za'] = jnp.concatenate(za_p, axis=-1)
    env['zb'] = jnp.concatenate(zb_p, axis=-1)
    step(gdn, ct // CHUNK)
    for _ in range(2 * (ct // CHUNK)):
        step(gdn)
        step(swa, -(-2 * n_sw // (2 * (ct // CHUNK))))
    for _ in swa:
        pass
    _swa_store(env, mix_scr, ct=ct, sb=ntail)
    ya = x + _dot(mix_scr[:, 0:A_WIDTH], wout_ref[0:A_WIDTH, :])
    for _ in gdn:
        pass
    y_ref[0] = ya + _dot(mix_scr[:, A_WIDTH:], wout_ref[A_WIDTH:, :])

    @pl.when(i == nsteps - 1)
    def _():
        sout_ref[0] = s_scr[...]


def _ab_layer_prompt(x3, norm_g, w_main, w_gate, w_gate_t, sink, hist_kv, conv_w8, gprm, gprm_t, onorm,
                     w_out, conv_in, s_in, ct, n_lead, pos_shift):
    B, L, _ = x3.shape
    ntail = min(WINDOW, ct)
    kern = functools.partial(_ab_layer_staged_kernel, ct=ct, n_lead=n_lead, pos_shift=pos_shift)
    full = lambda a: pl.BlockSpec(a.shape, lambda b, i: (0,) * a.ndim)
    blk = pl.BlockSpec((1, ct, D_MODEL), lambda b, i: (b, i, 0))
    per_b = lambda *s: pl.BlockSpec((1,) + s, lambda b, i: (b,) + (0,) * len(s))
    return pl.pallas_call(
        kern,
        grid=(B, L // ct),
        in_specs=[blk, full(norm_g), full(w_main), full(w_gate), full(w_gate_t), full(sink), full(hist_kv),
                  full(conv_w8), full(gprm), full(gprm_t), full(onorm), full(w_out),
                  full(conv_in), full(s_in)],
        out_specs=[blk, per_b(ntail, 2 * A_KV_WIDTH), per_b(8, B_CONV_CH), per_b(B_HEADS, B_DK, B_DV)],
        out_shape=[jax.ShapeDtypeStruct((B, L, D_MODEL), F32),
                   jax.ShapeDtypeStruct((B, ntail, 2 * A_KV_WIDTH), F32),
                   jax.ShapeDtypeStruct((B, 8, B_CONV_CH), F32),
                   jax.ShapeDtypeStruct((B, B_HEADS, B_DK, B_DV), F32)],
        scratch_shapes=[pltpu.VMEM((B_HEADS, B_DK, B_DV), F32),
                        pltpu.VMEM((8, B_CONV_CH), F32),
                        pltpu.VMEM((2, WINDOW, 2 * A_KV_WIDTH), F32),
                        pltpu.VMEM((ct, AB_MIX), BF16)],
        compiler_params=_cparams(("arbitrary", "arbitrary")),
        name="ab_layer_prompt",
    )(x3, norm_g, w_main, w_gate, w_gate_t, sink, hist_kv, conv_w8, gprm, gprm_t, onorm, w_out, conv_in, s_in)


def _c_layer_kernel(x_ref, g_ref, w_ref, wgk_ref, wup_ref, bgk_ref, onorm_ref, wout_ref, gfin_ref,
                    sin_ref, y_ref, sout_ref, st_scr, og_scr, *, ct, n_lead):
    i = pl.program_id(1)
    nsteps = pl.num_programs(1)

    @pl.when(i == 0)
    def _():
        st_scr[...] = sin_ref[0]

    x = x_ref[0]
    hb = _rms(x, g_ref[...]).astype(BF16)
    low = _dot(hb, wgk_ref[...])
    q_all = _dot(hb, w_ref[:, 0:C_KEY])
    la_all = _gla_log_decay(low, wup_ref, bgk_ref)
    k_all = _dot(hb, w_ref[:, C_KEY:2 * C_KEY])
    v_h, z_h = [], []
    later = ([lambda h=h: v_h.append(_dot(hb, w_ref[:, 2 * C_KEY + h * C_DV:2 * C_KEY + (h + 1) * C_DV]).astype(BF16))
              for h in range(C_HEADS)]
             + [lambda h=h: z_h.append(_dot(hb, w_ref[:, 2 * C_KEY + C_VAL + h * C_DV:
                                                       2 * C_KEY + C_VAL + (h + 1) * C_DV]))
                for h in range(C_HEADS)])

    def run_later(n):
        for _ in range(n):
            if later:
                later.pop(0)()
    rows = i * ct + lax.broadcasted_iota(jnp.int32, la_all.shape, 0)
    la_all = jnp.where(rows >= n_lead, la_all, 0.0)

    ii = lax.broadcasted_iota(jnp.int32, (CHUNK, CHUNK), 0)
    jj = lax.broadcasted_iota(jnp.int32, (CHUNK, CHUNK), 1)
    lower = ii >= jj
    ltri = lower.astype(BF16)
    nc = ct // CHUNK
    rs = lambda c: slice(c * CHUNK, (c + 1) * CHUNK)
    ks = lambda h: slice(h * C_DK, (h + 1) * C_DK)
    vs = lambda h: slice(h * C_DV, (h + 1) * C_DV)
    items = [(c, h) for c in range(nc) for h in range(C_HEADS)]
    n_items = len(items)
    bc_all = [sum(_dot(ltri, p) for p in _split3(la_all[rs(c)])) for c in range(nc)]
    run_later(2)
    every =max(nc // C_HEADS, 1)
    qd, kinv, kd, gl, qk = [], [], [], [], []
    for n, (c, h) in enumerate(items):
        bc = bc_all[c][:, ks(h)]
        k = k_all[rs(c), ks(h)]
        bl = bc[CHUNK - 1:CHUNK, :]
        qd.append((q_all[rs(c), ks(h)] * (C_DK ** -0.5) * jnp.exp(bc)).astype(BF16))
        kinv.append((k * jnp.exp(-bc)).astype(BF16))
        kd.append((k * jnp.exp(bl - bc)).astype(BF16))
        gl.append(jnp.exp(bl))
        if h == C_HEADS - 1:
            qk += [jnp.where(lower, _dotg(qd[m], kinv[m], NT_DIMS), 0.0).astype(BF16)
                   for m in range(n - C_HEADS + 1, n + 1)]
            if c % every == every - 1:
                run_later(1)
    run_later(C_HEADS - len(v_h))
    v_ = [v_h[h][rs(c)] for c, h in items]
    o_intra = []
    for c in range(nc):
        o_intra += [_dot(qk[c * C_HEADS + h], v_[c * C_HEADS + h]) for h in range(C_HEADS)]
        if c % every == every - 1:
            run_later(1)
    run_later(len(later))

    rb = min(ct, 4 * CHUNK)

    def gate_store(c, o):
        for h in range(C_HEADS):
            og_scr[rs(c), vs(h)] = (_rms(o[h], onorm_ref[...]) * _silu(z_h[h][rs(c)])).astype(BF16)
        if ((c + 1) * CHUNK) % rb == 0:
            r = slice((c + 1) * CHUNK - rb, (c + 1) * CHUNK)
            y_ref[0, r, :] = _rms(x[r] + _dot(og_scr[r, :], wout_ref[...]), gfin_ref[...])

    st = [st_scr[h] for h in range(C_HEADS)]
    o_prev = None
    for c in range(nc):
        idx = [c * C_HEADS + h for h in range(C_HEADS)]
        o = [o_intra[n] + _dotg(qd[n], st[h].astype(BF16), NT_DIMS) for h, n in enumerate(idx)]
        st = [st[h] * gl[n] + _dotg(v_[n], kd[n], TN_DIMS) for h, n in enumerate(idx)]
        if o_prev is not None:
            gate_store(c - 1, o_prev)
        o_prev = o
    gate_store(nc - 1, o_prev)
    for h in range(C_HEADS):
        st_scr[h] = st[h]

    @pl.when(i == nsteps - 1)
    def _():
        sout_ref[0] = st_scr[...]


def _c_layer_prompt(x3, norm_g, w_main, w_gk, w_up, b_gk, onorm, w_out, final_g, st_in, ct, n_lead):
    B, L, _ = x3.shape
    kern = functools.partial(_c_layer_kernel, ct=ct, n_lead=n_lead)
    full = lambda a: pl.BlockSpec(a.shape, lambda b, i: (0,) * a.ndim)
    blk = pl.BlockSpec((1, ct, D_MODEL), lambda b, i: (b, i, 0))
    st = pl.BlockSpec((1, C_HEADS, C_DV, C_DK), lambda b, i: (b, 0, 0, 0))
    return pl.pallas_call(
        kern,
        grid=(B, L // ct),
        in_specs=[blk, full(norm_g), full(w_main), full(w_gk), full(w_up), full(b_gk), full(onorm),
                  full(w_out), full(final_g), full(st_in)],
        out_specs=[blk, st],
        out_shape=[jax.ShapeDtypeStruct((B, L, D_MODEL), F32),
                   jax.ShapeDtypeStruct((B, C_HEADS, C_DV, C_DK), F32)],
        scratch_shapes=[pltpu.VMEM((C_HEADS, C_DV, C_DK), F32), pltpu.VMEM((ct, C_VAL), BF16)],
        compiler_params=_cparams(("arbitrary", "arbitrary")),
        name="c_layer_prompt",
    )(x3, norm_g, w_main, w_gk, w_up, b_gk, onorm, w_out, final_g, st_in)


def _ab_proj_kernel(x_ref, g_ref, w_ref, wg_ref, qa_ref, kv_ref, za_ref, xb_ref, zb_ref, gates_ref):
    hb = _rms(x_ref[...], g_ref[...]).astype(BF16)
    qa_ref[...] = _dot(hb, w_ref[:, O_QA:O_KV])
    kv_ref[...] = _dot(hb, w_ref[:, O_KV:O_ZA])
    za_ref[...] = _dot(hb, w_ref[:, O_ZA:O_XB])
    xb_ref[...] = _dot(hb, w_ref[:, O_XB:O_ZB])
    zb_ref[...] = _dot(hb, w_ref[:, O_ZB:AB_MAIN])
    gates_ref[...] = _dot(hb, wg_ref[...])


def _ab_proj(x2d, norm_g, w_main, w_gate):
    n = x2d.shape[0]
    full = lambda a: pl.BlockSpec(a.shape, lambda: (0,) * a.ndim)
    widths = (A_WIDTH, 2 * A_KV_WIDTH, A_WIDTH, B_CONV_CH, B_VAL, LANES)
    args = (x2d, norm_g, w_main, w_gate)
    return pl.pallas_call(
        _ab_proj_kernel,
        in_specs=[full(a) for a in args],
        out_specs=[pl.BlockSpec((n, w), lambda: (0, 0)) for w in widths],
        out_shape=[jax.ShapeDtypeStruct((n, w), F32) for w in widths],
        compiler_params=pltpu.CompilerParams(vmem_limit_bytes=VMEM_LIMIT),
        name="ab_in_proj_sample",
    )(*args)


def _swa_sample_kernel(q_ref, kvn_ref, za_ref, ck_ref, cv_ref, sink_ref, o_ref, nk_ref, nv_ref, *, bt):
    row = lax.broadcasted_iota(jnp.int32, (WINDOW, A_KV_WIDTH), 0)
    hrow = lax.broadcasted_iota(jnp.int32, (A_HEADS, A_HD), 0)
    sk = sink_ref[...]
    nkb, nvb = [], []
    for b in range(bt):
        nk = jnp.where(row == WINDOW - 1, kvn_ref[b:b + 1, 0:A_KV_WIDTH], pltpu.roll(ck_ref[b], WINDOW - 1, 0))
        nv = jnp.where(row == WINDOW - 1, kvn_ref[b:b + 1, A_KV_WIDTH:], pltpu.roll(cv_ref[b], WINDOW - 1, 0))
        nk_ref[b] = nk
        nv_ref[b] = nv
        nkb.append(nk.astype(BF16))
        nvb.append(nv.astype(BF16))
    items = [(b, j) for b in range(bt) for j in range(A_KV_HEADS)]
    q8 = [q_ref[b].astype(BF16) for b in range(bt)]
    sc = [_dotg(q8[b], nkb[b][:, j * A_HD:(j + 1) * A_HD], NT_DIMS) * (A_HD ** -0.5) for b, j in items]
    m = [jnp.maximum(jnp.max(s, axis=-1, keepdims=True), sk) for s in sc]
    p = [jnp.exp(sc[n] - m[n]) for n in range(len(items))]
    den = [jnp.sum(p[n], axis=-1, keepdims=True) + jnp.exp(sk - m[n]) for n in range(len(items))]
    oj = [_dot(p[n].astype(BF16), nvb[b][:, j * A_HD:(j + 1) * A_HD]) / den[n] for n, (b, j) in enumerate(items)]
    for b in range(bt):
        o8 = jnp.where(hrow >= A_GROUP, oj[b * A_KV_HEADS + 1], oj[b * A_KV_HEADS])
        o_ref[b] = o8 * _silu(za_ref[b])


def _swa_sample(q3, kv_new, za3, cache_k, cache_v, sink_col, bt):
    nb = q3.shape[0]
    hd = pl.BlockSpec((bt, A_HEADS, A_HD), lambda i: (i, 0, 0))
    cache = pl.BlockSpec((bt, WINDOW, A_KV_WIDTH), lambda i: (i, 0, 0))
    return pl.pallas_call(
        functools.partial(_swa_sample_kernel, bt=bt),
        grid=(nb // bt,),
        in_specs=[hd, pl.BlockSpec((bt, 2 * A_KV_WIDTH), lambda i: (i, 0)), hd, cache, cache,
                  pl.BlockSpec(sink_col.shape, lambda i: (0, 0))],
        out_specs=[hd, cache, cache],
        out_shape=[jax.ShapeDtypeStruct(q3.shape, F32),
                   jax.ShapeDtypeStruct(cache_k.shape, F32),
                   jax.ShapeDtypeStruct(cache_v.shape, F32)],
        compiler_params=_cparams(("arbitrary",)),
        name="swa_sample",
    )(q3, kv_new, za3, cache_k, cache_v, sink_col)


def _gdn_sample_prep_kernel(xb_ref, h0_ref, h1_ref, h2_ref, gates_ref, convw_ref, gprm_ref,
                            q_ref, k_ref, v_ref, bg_ref):
    w = convw_ref[...]
    y = (h0_ref[...] * w[0:1, :] + h1_ref[...] * w[1:2, :] + h2_ref[...] * w[2:3, :]
         + xb_ref[...] * w[3:4, :])
    cv = _silu(y)
    for h in range(B_HEADS):
        q = cv[:, h * B_DK:(h + 1) * B_DK]
        k = cv[:, B_KEY + h * B_DK:B_KEY + (h + 1) * B_DK]
        q_ref[:, h * B_DK:(h + 1) * B_DK] = (
            q * lax.rsqrt(jnp.sum(q * q, axis=-1, keepdims=True) + EPS) * (B_DK ** -0.5))
        k_ref[:, h * B_DK:(h + 1) * B_DK] = k * lax.rsqrt(jnp.sum(k * k, axis=-1, keepdims=True) + EPS)
    v_ref[...] = cv[:, 2 * B_KEY:]
    gt = gates_ref[...]
    col = lax.broadcasted_iota(jnp.int32, gt.shape, 1)
    g = -gprm_ref[0:1, :] * _softplus(gt + gprm_ref[1:2, :])
    bg_ref[...] = jnp.where(col < B_HEADS, _sigmoid(gt), jnp.exp(g))


def _gdn_sample_prep(xb, h0, h1, h2, gates, conv_w8, gprm):
    n = xb.shape[0]
    full = lambda a: pl.BlockSpec(a.shape, lambda: (0,) * a.ndim)
    args = (xb, h0, h1, h2, gates, conv_w8, gprm)
    return pl.pallas_call(
        _gdn_sample_prep_kernel,
        in_specs=[full(a) for a in args],
        out_specs=[pl.BlockSpec((n, B_KEY), lambda: (0, 0))] * 3 + [pl.BlockSpec((n, LANES), lambda: (0, 0))],
        out_shape=[jax.ShapeDtypeStruct((n, B_KEY), F32)] * 3 + [jax.ShapeDtypeStruct((n, LANES), F32)],
        compiler_params=pltpu.CompilerParams(vmem_limit_bytes=VMEM_LIMIT),
        name="gdn_sample_prep",
    )(*args)


def _gdn_sample_kernel(s_ref, qt_ref, kt_ref, v_ref, bg_ref, zb_ref, onorm_ref, o_ref, sout_ref, *, bt):
    items = [(b, h) for b in range(bt) for h in range(B_HEADS)]
    hs = lambda h: slice(h * B_DV, (h + 1) * B_DV)
    kt = [kt_ref[:, hs(h)].T for h in range(B_HEADS)]
    qt = [qt_ref[:, hs(h)].T for h in range(B_HEADS)]
    kc = [kt[h][:, b:b + 1] for b, h in items]
    eg = [bg_ref[b:b + 1, B_HEADS + h:B_HEADS + h + 1] for b, h in items]
    ks = [jnp.sum(kc[n] * s_ref[b, h], axis=0, keepdims=True) for n, (b, h) in enumerate(items)]
    v_new = [bg_ref[b:b + 1, h:h + 1] * (v_ref[b:b + 1, hs(h)] - eg[n] * ks[n]) for n, (b, h) in enumerate(items)]
    o = []
    for n, (b, h) in enumerate(items):
        s_new = eg[n] * s_ref[b, h] + kc[n] * v_new[n]
        sout_ref[b, h] = s_new
        o.append(jnp.sum(qt[h][:, b:b + 1] * s_new, axis=0, keepdims=True))
    o_blk = jnp.concatenate([jnp.concatenate([o[b * B_HEADS + h] for h in range(B_HEADS)], axis=-1)
                             for b in range(bt)], axis=0)
    for h in range(B_HEADS):
        o_ref[:, hs(h)] = _rms(o_blk[:, hs(h)], onorm_ref[...]) * _silu(zb_ref[:, hs(h)])


def _gdn_sample(state, qt, kt, v, bg, zb, onorm, bt):
    nb = state.shape[0]
    st = pl.BlockSpec((bt, B_HEADS, B_DK, B_DV), lambda i: (i, 0, 0, 0))
    col = pl.BlockSpec((1, B_HEADS, B_DK, bt), lambda i: (i, 0, 0, 0))
    row = lambda n: pl.BlockSpec((bt, n), lambda i: (i, 0))
    return pl.pallas_call(
        functools.partial(_gdn_sample_kernel, bt=bt),
        grid=(nb // bt,),
        in_specs=[st, row(B_KEY), row(B_KEY), row(B_VAL), row(LANES), row(B_VAL),
                  pl.BlockSpec(onorm.shape, lambda i: (0, 0))],
        out_specs=[row(B_VAL), st],
        out_shape=[jax.ShapeDtypeStruct((nb, B_VAL), F32), jax.ShapeDtypeStruct(state.shape, F32)],
        compiler_params=_cparams(("arbitrary",)),
        name="gdn_sample",
    )(state, qt, kt, v, bg, zb, onorm)


def _ab_out_c_proj_kernel(x_ref, oa_ref, ob_ref, wout_ref, g_ref, w_ref, wgk_ref, wup_ref, bgk_ref,
                          y_ref, q_ref, k_ref, v_ref, z_ref, la_ref):
    y = (x_ref[...] + _dot(oa_ref[...].astype(BF16), wout_ref[0:A_WIDTH, :])
         + _dot(ob_ref[...].astype(BF16), wout_ref[A_WIDTH:, :]))
    y_ref[...] = y
    hb = _rms(y, g_ref[...]).astype(BF16)
    q_ref[...] = _dot(hb, w_ref[:, 0:C_KEY])
    k_ref[...] = _dot(hb, w_ref[:, C_KEY:2 * C_KEY])
    v_ref[...] = _dot(hb, w_ref[:, 2 * C_KEY:2 * C_KEY + C_VAL])
    z_ref[...] = _dot(hb, w_ref[:, 2 * C_KEY + C_VAL:C_MAIN])
    la_ref[...] = _gla_log_decay(_dot(hb, wgk_ref[...]), wup_ref, bgk_ref)


def _ab_out_c_proj(x2d, oa, ob, w_out, norm_g, w_main, w_gk, w_up, b_gk):
    n = x2d.shape[0]
    full = lambda a: pl.BlockSpec(a.shape, lambda: (0,) * a.ndim)
    widths = (D_MODEL, C_KEY, C_KEY, C_VAL, C_VAL, C_KEY)
    args = (x2d, oa, ob, w_out, norm_g, w_main, w_gk, w_up, b_gk)
    return pl.pallas_call(
        _ab_out_c_proj_kernel,
        in_specs=[full(a) for a in args],
        out_specs=[pl.BlockSpec((n, w), lambda: (0, 0)) for w in widths],
        out_shape=[jax.ShapeDtypeStruct((n, w), F32) for w in widths],
        compiler_params=pltpu.CompilerParams(vmem_limit_bytes=VMEM_LIMIT),
        name="ab_out_c_in_proj_sample",
    )(*args)


def _gla_sample_kernel(s_ref, qt_ref, kt_ref, lat_ref, v_ref, z_ref, onorm_ref, o_ref, sout_ref, *, bt):
    items = [(b, h) for b in range(bt) for h in range(C_HEADS)]
    hs = lambda h: slice(h * C_DV, (h + 1) * C_DV)
    ks = lambda h: slice(h * C_DK, (h + 1) * C_DK)
    at = [jnp.exp(lat_ref[:, ks(h)]).T for h in range(C_HEADS)]
    kt = [kt_ref[:, ks(h)].T for h in range(C_HEADS)]
    qt = [(qt_ref[:, ks(h)] * (C_DK ** -0.5)).T for h in range(C_HEADS)]
    o = []
    for n, (b, h) in enumerate(items):
        s_new = at[h][:, b:b + 1] * s_ref[b, h] + kt[h][:, b:b + 1] * v_ref[b:b + 1, hs(h)]
        sout_ref[b, h] = s_new
        o.append(jnp.sum(qt[h][:, b:b + 1] * s_new, axis=0, keepdims=True))
    o_blk = jnp.concatenate([jnp.concatenate([o[b * C_HEADS + h] for h in range(C_HEADS)], axis=-1)
                             for b in range(bt)], axis=0)
    for h in range(C_HEADS):
        o_ref[:, hs(h)] = _rms(o_blk[:, hs(h)], onorm_ref[...]) * _silu(z_ref[:, hs(h)])


def _gla_sample(state, qt, kt, lat, v, z, onorm, bt):
    nb = state.shape[0]
    st = pl.BlockSpec((bt, C_HEADS, C_DK, C_DV), lambda i: (i, 0, 0, 0))
    col = pl.BlockSpec((1, C_HEADS, C_DK, bt), lambda i: (i, 0, 0, 0))
    row = lambda n: pl.BlockSpec((bt, n), lambda i: (i, 0))
    return pl.pallas_call(
        functools.partial(_gla_sample_kernel, bt=bt),
        grid=(nb // bt,),
        in_specs=[st, row(C_KEY), row(C_KEY), row(C_KEY), row(C_VAL), row(C_VAL),
                  pl.BlockSpec(onorm.shape, lambda i: (0, 0))],
        out_specs=[row(C_VAL), st],
        out_shape=[jax.ShapeDtypeStruct((nb, C_VAL), F32), jax.ShapeDtypeStruct(state.shape, F32)],
        compiler_params=_cparams(("arbitrary",)),
        name="gla_sample",
    )(state, qt, kt, lat, v, z, onorm)


def _c_out_kernel(x_ref, o_ref, w_ref, g_ref, y_ref):
    y_ref[...] = _rms(x_ref[...] + _dot(o_ref[...].astype(BF16), w_ref[...]), g_ref[...])


def _c_out(x2d, oc, w_out, final_g):
    n = x2d.shape[0]
    full = lambda a: pl.BlockSpec(a.shape, lambda: (0,) * a.ndim)
    args = (x2d, oc, w_out, final_g)
    return pl.pallas_call(
        _c_out_kernel,
        in_specs=[full(a) for a in args],
        out_specs=pl.BlockSpec((n, D_MODEL), lambda: (0, 0)),
        out_shape=jax.ShapeDtypeStruct((n, D_MODEL), F32),
        compiler_params=pltpu.CompilerParams(vmem_limit_bytes=VMEM_LIMIT),
        name="c_out_proj_norm_sample",
    )(*args)


def _to_cols(a, heads, dk, bt):
    n = a.shape[0]
    return a.reshape(n // bt, bt, heads, dk).transpose(0, 2, 3, 1)


def kernel(x_prompt, x_sample, cache_swa_k, cache_swa_v, state_dn_conv, state_dn, state_gla,
           meta_tokens, norm_ab, w_in_ab, sink_a, conv_b, a_log_b, dt_bias_b, onorm_b, w_out_ab,
           norm_c, w_in_c, w_gk_up, b_gk, onorm_c, w_out_c, final_norm):
    Bp, L, _ = x_prompt.shape
    Bs = x_sample.shape[0]
    CT, BT = 256, 8

    w_ab = w_in_ab[0]
    w_ab_main = w_ab.astype(BF16)
    w_ab_gate = jnp.pad(w_ab[:, AB_MAIN:], ((0, 0), (0, LANES - 2 * B_HEADS))).astype(BF16)
    w_ab_gate_t = jnp.pad(w_ab[:, AB_MAIN:].T, ((0, 16 - 2 * B_HEADS), (0, 0))).astype(BF16)
    g_ab = norm_ab[0][None, :]
    sink_row = sink_a[0][None, :]
    sink_col = sink_a[0][:, None]
    conv_w8 = jnp.pad(conv_b[0], ((0, 8 - CONV_W), (0, 0)))
    gp = jnp.stack([jnp.exp(a_log_b[0]), dt_bias_b[0]])
    gprm = jnp.pad(gp, ((0, 6), (B_HEADS, LANES - 2 * B_HEADS)))
    gprm_t = jnp.pad(gp.T, ((B_HEADS, 16 - 2 * B_HEADS), (0, LANES - 2)))
    onb = onorm_b[0][None, :]
    w_ab_out = w_out_ab[0].astype(BF16)
    w_c = w_in_c[0]
    w_c_main = w_c.astype(BF16)
    w_c_gk = jnp.pad(w_c[:, C_MAIN:], ((0, 0), (0, LANES - C_RANK))).astype(BF16)
    w_up = jnp.pad(w_gk_up[0], ((0, LANES - C_RANK), (0, 0)))
    bgk = b_gk[0][None, :]
    g_c = norm_c[0][None, :]
    onc = onorm_c[0][None, :]
    w_c_out = w_out_c[0].astype(BF16)
    g_fin = final_norm[None, :]

    def ab_layer(x3, ct, hist_kv, pos_shift, conv_in, s_in, n_lead):
        return _ab_layer_prompt(x3, g_ab, w_ab_main, w_ab_gate, w_ab_gate_t, sink_row, hist_kv, conv_w8,
                                gprm, gprm_t, onb, w_ab_out, conv_in, s_in, ct, n_lead, pos_shift)

    def c_layer(x3, ct, st_in, n_lead):
        return _c_layer_prompt(x3, g_c, w_c_main, w_c_gk, w_up, bgk, onc, w_c_out, g_fin, st_in, ct, n_lead)

    xpre = jnp.concatenate([jnp.zeros((LEAD, D_MODEL), F32), meta_tokens], axis=0)[None]
    y_pre, kv_pre, xbtail_pre, sdn_pre = ab_layer(
        xpre, CHUNK, jnp.zeros((WINDOW, 2 * A_KV_WIDTH), F32), -LEAD,
        jnp.zeros((1, 8, B_CONV_CH), F32), jnp.zeros((1, B_HEADS, B_DK, B_DV), F32), LEAD)
    _, sgla_pre = c_layer(y_pre, CHUNK, jnp.zeros((1, C_HEADS, C_DV, C_DK), F32), LEAD)

    hist_kv = jnp.concatenate([jnp.zeros((WINDOW - CHUNK, 2 * A_KV_WIDTH), F32), kv_pre[0]], axis=0)
    y1, kv_tail, xb_tail, sdn_p = ab_layer(
        x_prompt, 2 * CT, hist_kv, N_META, xbtail_pre, sdn_pre, 0)
    y_prompt, sgla_t = c_layer(y1, 2 * CT, sgla_pre, 0)
    sgla_p = jnp.swapaxes(sgla_t, 2, 3)

    swa_k_p = kv_tail[:, :, :A_KV_WIDTH].reshape(1, Bp, WINDOW, A_KV_HEADS, A_HD)
    swa_v_p = kv_tail[:, :, A_KV_WIDTH:].reshape(1, Bp, WINDOW, A_KV_HEADS, A_HD)
    dn_conv_p = xb_tail[:, 8 - (CONV_W - 1):, :][None]

    xs = x_sample.reshape(Bs, D_MODEL)
    qa, kvn, za, xb, zb, gates = _ab_proj(xs, g_ab, w_ab_main, w_ab_gate)
    oa3, nk, nv = _swa_sample(qa.reshape(Bs, A_HEADS, A_HD), kvn, za.reshape(Bs, A_HEADS, A_HD),
                              cache_swa_k[0].reshape(Bs, WINDOW, A_KV_WIDTH),
                              cache_swa_v[0].reshape(Bs, WINDOW, A_KV_WIDTH), sink_col, BT)
    hist = state_dn_conv[0]
    qn, kn, vn, bg = _gdn_sample_prep(xb, hist[:, 0], hist[:, 1], hist[:, 2], gates, conv_w8, gprm)
    ob, sdn_s = _gdn_sample(state_dn[0], qn, kn,
                            vn, bg, zb, onb, BT)
    ys, qc, kc, vc, zc, la = _ab_out_c_proj(xs, oa3.reshape(Bs, A_WIDTH), ob, w_ab_out,
                                            g_c, w_c_main, w_c_gk, w_up, bgk)
    ocs, sgla_s = _gla_sample(state_gla[0], qc, kc, la, vc, zc, onc, BT)
    y_sample = _c_out(ys, ocs, w_c_out, g_fin).reshape(Bs, 1, D_MODEL)
    dn_conv_s = jnp.concatenate([hist[:, 1:], xb[:, None, :]], axis=1)[None]

    return (y_prompt, y_sample, swa_k_p, swa_v_p, dn_conv_p, sdn_p[None], sgla_p[None],
            nk.reshape(1, Bs, WINDOW, A_KV_HEADS, A_HD), nv.reshape(1, Bs, WINDOW, A_KV_HEADS, A_HD),
            dn_conv_s, sdn_s[None], sgla_s[None])
```

```python
import functools

import jax
import jax.numpy as jnp
from jax import lax
from jax.experimental import pallas as pl
from jax.experimental.pallas import tpu as pltpu

F32 = jnp.float32
BF16 = jnp.bfloat16
EPS = 1e-6

D_MODEL = 1024
N_META = 16
CHUNK = 64
LEAD = (-N_META) % CHUNK
A_HEADS, A_KV_HEADS, A_HD = 8, 2, 64
A_GROUP = A_HEADS // A_KV_HEADS
A_WIDTH = A_HEADS * A_HD
A_KV_WIDTH = A_KV_HEADS * A_HD
WINDOW = 128
B_HEADS, B_DK, B_DV = 4, 128, 128
B_KEY = B_HEADS * B_DK
B_VAL = B_HEADS * B_DV
CONV_W = 4
B_CONV_CH = 2 * B_KEY + B_VAL
C_HEADS, C_DK, C_DV = 4, 128, 256
C_KEY = C_HEADS * C_DK
C_VAL = C_HEADS * C_DV
C_RANK = 16
C_GATE_NORM = 16.0
AB_MAIN = 2 * A_WIDTH + 2 * A_KV_WIDTH + B_CONV_CH + B_VAL
AB_MIX = A_WIDTH + B_VAL
C_MAIN = 2 * C_KEY + 2 * C_VAL
O_QA, O_KV, O_ZA, O_XB, O_ZB = 0, 512, 768, 1280, 2816
LANES = 128
NEG = -1e30

VMEM_LIMIT = 56 * 1024 * 1024

NT_DIMS = (((1,), (1,)), ((), ()))
TN_DIMS = (((0,), (0,)), ((), ()))


def _cparams(sem, **kw):
    return pltpu.CompilerParams(dimension_semantics=sem, vmem_limit_bytes=VMEM_LIMIT, **kw)


def _dot(a, b):
    return jnp.dot(a, b, preferred_element_type=F32)


def _dotg(a, b, dims):
    return lax.dot_general(a, b, dims, preferred_element_type=F32)


def _split3(a):
    hi = a.astype(BF16)
    r = a - hi.astype(F32)
    mid = r.astype(BF16)
    lo = (r - mid.astype(F32)).astype(BF16)
    return hi, mid, lo


def _split2(a):
    hi = a.astype(BF16)
    return hi, (a - hi.astype(F32)).astype(BF16)


def _sigmoid(x):
    return 1.0 / (1.0 + jnp.exp(-x))


def _silu(x):
    return x * _sigmoid(x)


def _softplus(x):
    return jnp.maximum(x, 0.0) + jnp.log1p(jnp.exp(-jnp.abs(x)))


def _log_sigmoid(x):
    return jnp.minimum(x, 0.0) - jnp.log1p(jnp.exp(-jnp.abs(x)))


def _rms(x, g):
    return x * lax.rsqrt(jnp.mean(x * x, axis=-1, keepdims=True) + EPS) * g


def _gla_log_decay(low, wup_ref, bgk_ref):
    lh, lm, _ = _split3(low)
    wh, wm, _ = _split3(wup_ref[...])
    up = _dot(lh, wh) + _dot(lh, wm) + _dot(lm, wh)
    return _log_sigmoid(up + bgk_ref[...]) * (1.0 / C_GATE_NORM)


def _swa_tile(i, qa, kvb, za, sink_ref, mix_scr, *, ct, sb, pos_shift):
    r = lax.broadcasted_iota(jnp.int32, (sb, WINDOW + sb), 0)
    c = lax.broadcasted_iota(jnp.int32, (sb, WINDOW + sb), 1)
    diff = r - c + WINDOW
    in_window = (diff >= 0) & (diff < WINDOW)
    nsb = ct // sb
    items = [(s, j) for s in range(nsb) for j in range(A_KV_HEADS)]
    masks = [jnp.concatenate([in_window & (i * ct + s * sb + c - WINDOW + pos_shift >= 0)] * A_GROUP, axis=0)
             for s in range(nsb)]
    sinks = [jnp.concatenate([jnp.broadcast_to(sink_ref[0:1, j * A_GROUP + g:j * A_GROUP + g + 1], (sb, 1))
                              for g in range(A_GROUP)], axis=0) for j in range(A_KV_HEADS)]
    qs = [jnp.concatenate([qa[s * sb:(s + 1) * sb, (j * A_GROUP + g) * A_HD:(j * A_GROUP + g + 1) * A_HD]
                           for g in range(A_GROUP)], axis=0).astype(BF16) for s, j in items]
    sc = [_dotg(qs[n], kvb[s * sb:s * sb + WINDOW + sb, j * A_HD:(j + 1) * A_HD], NT_DIMS) * (A_HD ** -0.5)
          for n, (s, j) in enumerate(items)]
    sc = [jnp.where(masks[s], sc[n], NEG) for n, (s, j) in enumerate(items)]
    mx = [jnp.maximum(jnp.max(sc[n], axis=-1, keepdims=True), sinks[j]) for n, (s, j) in enumerate(items)]
    p = [jnp.exp(sc[n] - mx[n]) for n in range(len(items))]
    den = [jnp.sum(p[n], axis=-1, keepdims=True) + jnp.exp(sinks[j] - mx[n]) for n, (s, j) in enumerate(items)]
    pv = [_dot(p[n].astype(BF16),
               kvb[s * sb:s * sb + WINDOW + sb, A_KV_WIDTH + j * A_HD:A_KV_WIDTH + (j + 1) * A_HD]) / den[n]
          for n, (s, j) in enumerate(items)]
    for s in range(nsb):
        o = jnp.concatenate([pv[s * A_KV_HEADS + j][g * sb:(g + 1) * sb]
                             for j in range(A_KV_HEADS) for g in range(A_GROUP)], axis=-1)
        mix_scr[s * sb:(s + 1) * sb, 0:A_WIDTH] = (o * _silu(za[s * sb:(s + 1) * sb])).astype(BF16)


def _gdn_tile(i, cv, gt, gtt, zb, gprm_ref, gprmt_ref, onorm_ref, s_scr, mix_scr, *, ct, n_lead):
    beta = _sigmoid(gt)
    rows = i * ct + lax.broadcasted_iota(jnp.int32, gt.shape, 0)
    gcol = jnp.where(rows >= n_lead, -gprm_ref[0:1, :] * _softplus(gt + gprm_ref[1:2, :]), 0.0)
    lanes = i * ct + lax.broadcasted_iota(jnp.int32, gtt.shape, 1)
    grow = jnp.where(lanes >= n_lead, -gprmt_ref[:, 0:1] * _softplus(gtt + gprmt_ref[:, 1:2]), 0.0)

    ii = lax.broadcasted_iota(jnp.int32, (CHUNK, CHUNK), 0)
    jj = lax.broadcasted_iota(jnp.int32, (CHUNK, CHUNK), 1)
    lower = ii >= jj
    strict = ii > jj
    ltri = lower.astype(BF16)
    utri = (ii <= jj).astype(BF16)

    nc = ct // CHUNK
    items = [(c, h) for c in range(nc) for h in range(B_HEADS)]
    n_items = len(items)
    rs = lambda c: slice(c * CHUNK, (c + 1) * CHUNK)
    qn, kn = [], []
    for h in range(B_HEADS):
        q = cv[:, h * B_DK:(h + 1) * B_DK]
        k = cv[:, B_KEY + h * B_DK:B_KEY + (h + 1) * B_DK]
        qn.append(q * lax.rsqrt(jnp.sum(q * q, axis=-1, keepdims=True) + EPS) * (B_DK ** -0.5))
        kn.append(k * lax.rsqrt(jnp.sum(k * k, axis=-1, keepdims=True) + EPS))
    gcum_c = [sum(_dot(ltri, p) for p in _split3(gcol[rs(c)])) for c in range(nc)]
    gcum_r = [sum(_dot(p, utri) for p in _split3(grow[:, rs(c)])) for c in range(nc)]
    q_ = [qn[h][rs(c)] for c, h in items]
    k_ = [kn[h][rs(c)] for c, h in items]
    v_ = [cv[rs(c), 2 * B_KEY + h * B_DV:2 * B_KEY + (h + 1) * B_DV] for c, h in items]
    bh = [beta[rs(c), h:h + 1] for c, h in items]
    gc = [gcum_c[c][:, B_HEADS + h:B_HEADS + h + 1] for c, h in items]
    gr = [gcum_r[c][B_HEADS + h:B_HEADS + h + 1, :] for c, h in items]
    decay = [jnp.exp(jnp.where(lower, gc[n] - gr[n], NEG)) for n in range(n_items)]
    kb = [k_[n] * bh[n] for n in range(n_items)]
    kbf = [k_[n].astype(BF16) for n in range(n_items)]
    kq = [_dotg(jnp.concatenate([kb[n], q_[n]], axis=0).astype(BF16), kbf[n], NT_DIMS) for n in range(n_items)]
    a = [jnp.where(strict, kq[n][:CHUNK] * decay[n], 0.0) for n in range(n_items)]
    qk = [jnp.where(lower, kq[n][CHUNK:] * decay[n], 0.0).astype(BF16) for n in range(n_items)]
    doff = [-jnp.where((ii >> 1) == (jj >> 1), a[n], 0.0) for n in range(n_items)]
    for lg in range(1, 6):
        m = ((ii >> (lg + 1)) == (jj >> (lg + 1))) & ((ii >> lg) != (jj >> lg))
        bm = [jnp.where(m, a[n], 0.0) for n in range(n_items)]
        db = [doff[n].astype(BF16) for n in range(n_items)]
        y = [bm[n] + _dot(db[n], bm[n].astype(BF16)) for n in range(n_items)]
        x = [y[n] + _dot(y[n].astype(BF16), db[n]) for n in range(n_items)]
        doff = [doff[n] - x[n] for n in range(n_items)]
    egc = [jnp.exp(gc[n]) for n in range(n_items)]
    rhs = [jnp.concatenate([v_[n] * bh[n], kb[n] * egc[n]], axis=-1) for n in range(n_items)]
    sol = [rhs[n] + _dot(doff[n].astype(BF16), rhs[n].astype(BF16)) for n in range(n_items)]
    g_last = [gc[n][CHUNK - 1:CHUNK, :] for n in range(n_items)]
    kd = [(k_[n] * jnp.exp(g_last[n] - gc[n])).astype(BF16) for n in range(n_items)]
    wq = [jnp.concatenate([sol[n][:, B_DV:], q_[n] * egc[n]], axis=0).astype(BF16) for n in range(n_items)]
    eg_last = [jnp.exp(g_last[n]) for n in range(n_items)]

    s_cur = [s_scr[h] for h in range(B_HEADS)]
    for c in range(nc):
        idx = [c * B_HEADS + h for h in range(B_HEADS)]
        ws = [_dot(wq[n], s_cur[h].astype(BF16)) for h, n in enumerate(idx)]
        v_new = [sol[n][:, :B_DV] - ws[h][:CHUNK] for h, n in enumerate(idx)]
        vb = [v_new[h].astype(BF16) for h in range(B_HEADS)]
        o = [ws[h][CHUNK:] + _dot(qk[n], vb[h]) for h, n in enumerate(idx)]
        s_cur = [s_cur[h] * eg_last[n] + _dotg(kd[n], vb[h], TN_DIMS) for h, n in enumerate(idx)]
        for h in range(B_HEADS):
            z = zb[rs(c), h * B_DV:(h + 1) * B_DV]
            mix_scr[rs(c), A_WIDTH + h * B_DV:A_WIDTH + (h + 1) * B_DV] = (
                _rms(o[h], onorm_ref[...]) * _silu(z)).astype(BF16)
    for h in range(B_HEADS):
        s_scr[h] = s_cur[h]


def _ab_layer_kernel(x_ref, g_ref, w_ref, wg_ref, wgt_ref, sink_ref, hist_ref, convw_ref, gprm_ref,
                     gprmt_ref, onorm_ref, wout_ref, convin_ref, sin_ref,
                     y_ref, kvtail_ref, xbtail_ref, sout_ref,
                     s_scr, xc_scr, kvprev_scr, mix_scr, *, ct, n_lead, pos_shift):
    i = pl.program_id(1)
    nsteps = pl.num_programs(1)
    ntail = min(WINDOW, ct)

    @pl.when(i == 0)
    def _():
        s_scr[...] = sin_ref[0]
        xc_scr[0:8, :] = convin_ref[0]
        kvprev_scr[1] = hist_ref[...]

    x = x_ref[0]
    hb = _rms(x, g_ref[...]).astype(BF16)
    qa = _dot(hb, w_ref[:, O_QA:O_KV])
    kv = _dot(hb, w_ref[:, O_KV:O_ZA])
    za = _dot(hb, w_ref[:, O_ZA:O_XB])
    xb = _dot(hb, w_ref[:, O_XB:O_ZB])
    zb = _dot(hb, w_ref[:, O_ZB:AB_MAIN])
    gt = _dot(hb, wg_ref[...])
    gtt = _dotg(wgt_ref[...], hb, NT_DIMS)
    kvtail_ref[0] = kv[ct - ntail:]
    xbtail_ref[0] = xb[ct - 8:]

    slot = lax.rem(i, 2)
    kvb = jnp.concatenate([kvprev_scr[1 - slot], kv], axis=0).astype(BF16)
    if ct >= WINDOW:
        kvprev_scr[slot] = kv[ct - WINDOW:]
    _swa_tile(i, qa, kvb, za, sink_ref, mix_scr, ct=ct, sb=ntail, pos_shift=pos_shift)

    xc_scr[8:8 + ct, :] = xb
    w = convw_ref[...]
    yc = xc_scr[pl.ds(8 - (CONV_W - 1), ct), :] * w[0:1, :]
    for t in range(1, CONV_W):
        yc = yc + xc_scr[pl.ds(8 - (CONV_W - 1) + t, ct), :] * w[t:t + 1, :]
    cv = _silu(yc)
    xc_scr[0:8, :] = xc_scr[ct:ct + 8, :]
    _gdn_tile(i, cv, gt, gtt, zb, gprm_ref, gprmt_ref, onorm_ref, s_scr, mix_scr, ct=ct, n_lead=n_lead)

    y_ref[0] = x + _dot(mix_scr[...], wout_ref[...])

    @pl.when(i == nsteps - 1)
    def _():
        sout_ref[0] = s_scr[...]


def _swa_stages(i, qa, kvb, env, sink_ref, mix_scr, *, ct, sb, pos_shift):
    r = lax.broadcasted_iota(jnp.int32, (sb, WINDOW + sb), 0)
    c = lax.broadcasted_iota(jnp.int32, (sb, WINDOW + sb), 1)
    diff = r - c + WINDOW
    in_window = (diff >= 0) & (diff < WINDOW)
    nsb = ct // sb
    items = [(s, j) for s in range(nsb) for j in range(A_KV_HEADS)]
    masks = [jnp.concatenate([in_window & (i * ct + s * sb + c - WINDOW + pos_shift >= 0)] * A_GROUP, axis=0)
             for s in range(nsb)]
    sinks = [jnp.concatenate([jnp.broadcast_to(sink_ref[0:1, j * A_GROUP + g:j * A_GROUP + g + 1], (sb, 1))
                              for g in range(A_GROUP)], axis=0) for j in range(A_KV_HEADS)]
    sc = []
    for s, j in items:
        qs = (jnp.concatenate([qa[s * sb:(s + 1) * sb, (j * A_GROUP + g) * A_HD:(j * A_GROUP + g + 1) * A_HD]
                               for g in range(A_GROUP)], axis=0) * (A_HD ** -0.5)).astype(BF16)
        sc.append(_dotg(qs, kvb[s * sb:s * sb + WINDOW + sb, j * A_HD:(j + 1) * A_HD], NT_DIMS))
        yield
    p, esk = [], []
    for n, (s, j) in enumerate(items):
        scm = jnp.where(masks[s], sc[n], NEG)
        mx = jnp.maximum(jnp.max(scm, axis=-1, keepdims=True), sinks[j])
        p.append(jnp.exp(scm - mx).astype(BF16))
        esk.append(jnp.exp(sinks[j] - mx))
        yield
    pv = []
    ones = jnp.ones((WINDOW + sb, A_HD), BF16)
    for n, (s, j) in enumerate(items):
        vx = jnp.concatenate([kvb[s * sb:s * sb + WINDOW + sb,
                                  A_KV_WIDTH + j * A_HD:A_KV_WIDTH + (j + 1) * A_HD], ones], axis=-1)
        pvx = _dot(p[n], vx)
        pv.append(pvx[:, :A_HD] / (pvx[:, A_HD:] + esk[n]))
        yield
    env['pv'] = pv


def _swa_store(env, mix_scr, *, ct, sb):
    pv = env['pv']
    for s in range(ct // sb):
        o = jnp.concatenate([pv[s * A_KV_HEADS + j][g * sb:(g + 1) * sb]
                             for j in range(A_KV_HEADS) for g in range(A_GROUP)], axis=-1)
        mix_scr[s * sb:(s + 1) * sb, 0:A_WIDTH] = (o * _silu(env['za'][s * sb:(s + 1) * sb])).astype(BF16)


def _gdn_stages(i, cvb, gt, gtt, env, gprm_ref, gprmt_ref, onorm_ref, s_scr, mix_scr, *, ct, n_lead):
    nc = ct // CHUNK
    items = [(c, h) for c in range(nc) for h in range(B_HEADS)]
    n_items = len(items)
    rs = lambda c: slice(c * CHUNK, (c + 1) * CHUNK)
    qn, kn = [], []
    for h in range(B_HEADS):
        q = cvb[h]
        k = cvb[B_HEADS + h]
        qn.append(q * lax.rsqrt(jnp.sum(q * q, axis=-1, keepdims=True) + EPS) * (B_DK ** -0.5))
        kn.append(k * lax.rsqrt(jnp.sum(k * k, axis=-1, keepdims=True) + EPS))
        yield
    beta = _sigmoid(gt)
    rows = i * ct + lax.broadcasted_iota(jnp.int32, gt.shape, 0)
    gcol = jnp.where(rows >= n_lead, -gprm_ref[0:1, :] * _softplus(gt + gprm_ref[1:2, :]), 0.0)
    lanes = i * ct + lax.broadcasted_iota(jnp.int32, gtt.shape, 1)
    grow = jnp.where(lanes >= n_lead, -gprmt_ref[:, 0:1] * _softplus(gtt + gprmt_ref[:, 1:2]), 0.0)
    ii = lax.broadcasted_iota(jnp.int32, (CHUNK, CHUNK), 0)
    jj = lax.broadcasted_iota(jnp.int32, (CHUNK, CHUNK), 1)
    lower = ii >= jj
    strict = ii > jj
    ltri = lower.astype(BF16)
    utri = (ii <= jj).astype(BF16)
    gcum_c = [sum(_dot(ltri, p) for p in _split3(gcol[rs(c)])) for c in range(nc)]
    gcum_r = [sum(_dot(p, utri) for p in _split3(grow[:, rs(c)])) for c in range(nc)]
    yield
    q_ = [qn[h][rs(c)] for c, h in items]
    k_ = [kn[h][rs(c)] for c, h in items]
    v_ = [cvb[2 * B_HEADS + h][rs(c)] for c, h in items]
    bh = [beta[rs(c), h:h + 1] for c, h in items]
    gc = [gcum_c[c][:, B_HEADS + h:B_HEADS + h + 1] for c, h in items]
    gr = [gcum_r[c][B_HEADS + h:B_HEADS + h + 1, :] for c, h in items]
    kb, a, qk = [], [], []
    for n in range(n_items):
        decay = jnp.exp(jnp.where(lower, gc[n] - gr[n], NEG))
        kb.append(k_[n] * bh[n])
        kq = _dotg(jnp.concatenate([kb[n], q_[n]], axis=0).astype(BF16), k_[n].astype(BF16), NT_DIMS)
        a.append(jnp.where(strict, kq[:CHUNK] * decay, 0.0))
        qk.append(jnp.where(lower, kq[CHUNK:] * decay, 0.0).astype(BF16))
        if n % B_HEADS == B_HEADS - 1:
            yield
    doff = [-jnp.where((ii >> 1) == (jj >> 1), a[n], 0.0) for n in range(n_items)]
    for lg in range(1, 6):
        m = ((ii >> (lg + 1)) == (jj >> (lg + 1))) & ((ii >> lg) != (jj >> lg))
        bm = [jnp.where(m, a[n], 0.0) for n in range(n_items)]
        db = [doff[n].astype(BF16) for n in range(n_items)]
        y = [bm[n] + _dot(db[n], bm[n].astype(BF16)) for n in range(n_items)]
        yield
        x = [y[n] + _dot(y[n].astype(BF16), db[n]) for n in range(n_items)]
        doff = [doff[n] - x[n] for n in range(n_items)]
        yield
    egc = [jnp.exp(gc[n]) for n in range(n_items)]
    g_last = [gc[n][CHUNK - 1:CHUNK, :] for n in range(n_items)]
    eg_last = [jnp.exp(g_last[n]) for n in range(n_items)]
    sol, kd, wq = [], [], []
    for n in range(n_items):
        rhs = jnp.concatenate([v_[n] * bh[n], kb[n] * egc[n]], axis=-1)
        sol.append(rhs + _dot(doff[n].astype(BF16), rhs.astype(BF16)))
        kd.append((k_[n] * jnp.exp(g_last[n] - gc[n])).astype(BF16))
        wq.append(jnp.concatenate([sol[n][:, B_DV:], q_[n] * egc[n]], axis=0).astype(BF16))
        if n % B_HEADS == B_HEADS - 1:
            yield

    def gate_store(c, o):
        for h in range(B_HEADS):
            z = env['zb'][rs(c), h * B_DV:(h + 1) * B_DV]
            mix_scr[rs(c), A_WIDTH + h * B_DV:A_WIDTH + (h + 1) * B_DV] = (
                _rms(o[h], onorm_ref[...]) * _silu(z)).astype(BF16)

    s_cur = [s_scr[h] for h in range(B_HEADS)]
    o_prev = None
    for c in range(nc):
        idx = [c * B_HEADS + h for h in range(B_HEADS)]
        ws = [_dot(wq[n], s_cur[h].astype(BF16)) for h, n in enumerate(idx)]
        if o_prev is not None:
            gate_store(c - 1, o_prev)
        yield
        v_new = [sol[n][:, :B_DV] - ws[h][:CHUNK] for h, n in enumerate(idx)]
        vb = [v_new[h].astype(BF16) for h in range(B_HEADS)]
        o_prev = [ws[h][CHUNK:] + _dot(qk[n], vb[h]) for h, n in enumerate(idx)]
        s_cur = [s_cur[h] * eg_last[n] + _dotg(kd[n], vb[h], TN_DIMS) for h, n in enumerate(idx)]
        yield
    for h in range(B_HEADS):
        s_scr[h] = s_cur[h]
    gate_store(nc - 1, o_prev)
    yield


def _ab_layer_staged_kernel(x_ref, g_ref, w_ref, wg_ref, wgt_ref, sink_ref, hist_ref, convw_ref, gprm_ref,
                            gprmt_ref, onorm_ref, wout_ref, convin_ref, sin_ref,
                            y_ref, kvtail_ref, xbtail_ref, sout_ref,
                            s_scr, xc_scr, kvprev_scr, mix_scr, *, ct, n_lead, pos_shift):
    i = pl.program_id(1)
    nsteps = pl.num_programs(1)
    ntail = min(WINDOW, ct)

    @pl.when(i == 0)
    def _():
        s_scr[...] = sin_ref[0]
        xc_scr[0:8, :] = convin_ref[0]
        kvprev_scr[1] = hist_ref[...]

    x = x_ref[0]
    hb = _rms(x, g_ref[...]).astype(BF16)
    proj = lambda a, b: _dot(hb, w_ref[:, a:b])
    env = {}

    def step(gen, n=1):
        for _ in range(n):
            next(gen, None)

    nblk = B_CONV_CH // LANES
    w = convw_ref[...]
    cvb = []

    xblk = []
    row8 = lax.broadcasted_iota(jnp.int32, (8, LANES), 0)

    def conv_block(j):
        cs = slice(j * LANES, (j + 1) * LANES)
        xj = xblk[j]
        h8 = xc_scr[0:8, cs]
        yc = xj * w[CONV_W - 1:CONV_W, cs]
        for k in range(1, CONV_W):
            sh = pltpu.roll(xj, k, 0)
            top = jnp.where(row8 < k, pltpu.roll(h8, k, 0), sh[0:8])
            yc = yc + jnp.concatenate([top, sh[8:]], axis=0) * w[CONV_W - 1 - k:CONV_W - k, cs]
        cvb.append(_silu(yc))
        xc_scr[0:8, cs] = xj[ct - 8:]

    pw = 2 * LANES
    for j in range(nblk // 2):
        xbj = proj(O_XB + j * pw, O_XB + (j + 1) * pw)
        xblk += [xbj[:, 0:LANES], xbj[:, LANES:]]
        xbtail_ref[0, :, j * pw:(j + 1) * pw] = xbj[ct - 8:]
        if j > 0:
            conv_block(2 * j - 2)
            conv_block(2 * j - 1)
    gt = _dot(hb, wg_ref[...])
    gtt = _dotg(wgt_ref[...], hb, NT_DIMS)
    conv_block(nblk - 2)
    conv_block(nblk - 1)

    gdn = _gdn_stages(i, cvb, gt, gtt, env, gprm_ref, gprmt_ref, onorm_ref, s_scr, mix_scr, ct=ct, n_lead=n_lead)
    qa_p = []
    for j in range(A_WIDTH // pw):
        qa_p.append(proj(O_QA + j * pw, O_QA + (j + 1) * pw))
        step(gdn, 2)
    qa = jnp.concatenate(qa_p, axis=-1)
    kv = proj(O_KV, O_ZA)
    kvtail_ref[0] = kv[ct - ntail:]
    slot = lax.rem(i, 2)
    kvb = jnp.concatenate([kvprev_scr[1 - slot], kv], axis=0).astype(BF16)
    if ct >= WINDOW:
        kvprev_scr[slot] = kv[ct - WINDOW:]
    swa = _swa_stages(i, qa, kvb, env, sink_ref, mix_scr, ct=ct, sb=ntail, pos_shift=pos_shift)
    n_sw = (ct // ntail) * A_KV_HEADS
    step(gdn)
    for _ in range(ct // CHUNK):
        step(gdn)
        step(swa)
    step(swa, max(n_sw - ct // CHUNK, 0))
    za_p, zb_p = [], []
    fill = ([lambda j=j: za_p.append(proj(O_ZA + j * pw, O_ZA + (j + 1) * pw)) for j in range(A_WIDTH // pw)]
            + [lambda j=j: zb_p.append(proj(O_ZB + j * pw, O_ZB + (j + 1) * pw)) for j in range(B_VAL // pw)])
    for lv in range(10):
        step(gdn)
        if lv % 2 == 1 and fill:
            fill.pop(0)()
    for f in fill:
        f()
    env['za'] = jnp.concatenate(za_p, axis=-1)
    env['zb'] = jnp.concatenate(zb_p, axis=-1)
    step(gdn, ct // CHUNK)
    for _ in range(2 * (ct // CHUNK)):
        step(gdn)
        step(swa, -(-2 * n_sw // (2 * (ct // CHUNK))))
    for _ in swa:
        pass
    _swa_store(env, mix_scr, ct=ct, sb=ntail)
    ya = x + _dot(mix_scr[:, 0:A_WIDTH], wout_ref[0:A_WIDTH, :])
    for _ in gdn:
        pass
    y_ref[0] = ya + _dot(mix_scr[:, A_WIDTH:], wout_ref[A_WIDTH:, :])

    @pl.when(i == nsteps - 1)
    def _():
        sout_ref[0] = s_scr[...]


def _ab_layer_prompt(x3, norm_g, w_main, w_gate, w_gate_t, sink, hist_kv, conv_w8, gprm, gprm_t, onorm,
                     w_out, conv_in, s_in, ct, n_lead, pos_shift):
    B, L, _ = x3.shape
    ntail = min(WINDOW, ct)
    kern = functools.partial(_ab_layer_staged_kernel, ct=ct, n_lead=n_lead, pos_shift=pos_shift)
    full = lambda a: pl.BlockSpec(a.shape, lambda b, i: (0,) * a.ndim)
    blk = pl.BlockSpec((1, ct, D_MODEL), lambda b, i: (b, i, 0))
    per_b = lambda *s: pl.BlockSpec((1,) + s, lambda b, i: (b,) + (0,) * len(s))
    return pl.pallas_call(
        kern,
        grid=(B, L // ct),
        in_specs=[blk, full(norm_g), full(w_main), full(w_gate), full(w_gate_t), full(sink), full(hist_kv),
                  full(conv_w8), full(gprm), full(gprm_t), full(onorm), full(w_out),
                  full(conv_in), full(s_in)],
        out_specs=[blk, per_b(ntail, 2 * A_KV_WIDTH), per_b(8, B_CONV_CH), per_b(B_HEADS, B_DK, B_DV)],
        out_shape=[jax.ShapeDtypeStruct((B, L, D_MODEL), F32),
                   jax.ShapeDtypeStruct((B, ntail, 2 * A_KV_WIDTH), F32),
                   jax.ShapeDtypeStruct((B, 8, B_CONV_CH), F32),
                   jax.ShapeDtypeStruct((B, B_HEADS, B_DK, B_DV), F32)],
        scratch_shapes=[pltpu.VMEM((B_HEADS, B_DK, B_DV), F32),
                        pltpu.VMEM((8, B_CONV_CH), F32),
                        pltpu.VMEM((2, WINDOW, 2 * A_KV_WIDTH), F32),
                        pltpu.VMEM((ct, AB_MIX), BF16)],
        compiler_params=_cparams(("arbitrary", "arbitrary")),
        name="ab_layer_prompt",
    )(x3, norm_g, w_main, w_gate, w_gate_t, sink, hist_kv, conv_w8, gprm, gprm_t, onorm, w_out, conv_in, s_in)


def _c_layer_kernel(x_ref, g_ref, w_ref, wgk_ref, wup_ref, bgk_ref, onorm_ref, wout_ref, gfin_ref,
                    sin_ref, y_ref, sout_ref, st_scr, og_scr, *, ct, n_lead):
    i = pl.program_id(1)
    nsteps = pl.num_programs(1)

    @pl.when(i == 0)
    def _():
        st_scr[...] = sin_ref[0]

    x = x_ref[0]
    hb = _rms(x, g_ref[...]).astype(BF16)
    low = _dot(hb, wgk_ref[...])
    q_all = _dot(hb, w_ref[:, 0:C_KEY])
    la_all = _gla_log_decay(low, wup_ref, bgk_ref)
    k_all = _dot(hb, w_ref[:, C_KEY:2 * C_KEY])
    v_h, z_h = [], []
    later = ([lambda h=h: v_h.append(_dot(hb, w_ref[:, 2 * C_KEY + h * C_DV:2 * C_KEY + (h + 1) * C_DV]).astype(BF16))
              for h in range(C_HEADS)]
             + [lambda h=h: z_h.append(_dot(hb, w_ref[:, 2 * C_KEY + C_VAL + h * C_DV:
                                                       2 * C_KEY + C_VAL + (h + 1) * C_DV]))
                for h in range(C_HEADS)])

    def run_later(n):
        for _ in range(n):
            if later:
                later.pop(0)()
    rows = i * ct + lax.broadcasted_iota(jnp.int32, la_all.shape, 0)
    la_all = jnp.where(rows >= n_lead, la_all, 0.0)

    ii = lax.broadcasted_iota(jnp.int32, (CHUNK, CHUNK), 0)
    jj = lax.broadcasted_iota(jnp.int32, (CHUNK, CHUNK), 1)
    lower = ii >= jj
    ltri = lower.astype(BF16)
    nc = ct // CHUNK
    rs = lambda c: slice(c * CHUNK, (c + 1) * CHUNK)
    ks = lambda h: slice(h * C_DK, (h + 1) * C_DK)
    vs = lambda h: slice(h * C_DV, (h + 1) * C_DV)
    items = [(c, h) for c in range(nc) for h in range(C_HEADS)]
    n_items = len(items)
    bc_all = [sum(_dot(ltri, p) for p in _split2(la_all[rs(c)])) for c in range(nc)]
    run_later(2)
    every =max(nc // C_HEADS, 1)
    qd, kinv, kd, gl, qk = [], [], [], [], []
    for n, (c, h) in enumerate(items):
        bc = bc_all[c][:, ks(h)]
        k = k_all[rs(c), ks(h)]
        bl = bc[CHUNK - 1:CHUNK, :]
        qd.append((q_all[rs(c), ks(h)] * (C_DK ** -0.5) * jnp.exp(bc)).astype(BF16))
        kinv.append((k * jnp.exp(-bc)).astype(BF16))
        kd.append((k * jnp.exp(bl - bc)).astype(BF16))
        gl.append(jnp.exp(bl))
        if h == C_HEADS - 1:
            qk += [jnp.where(lower, _dotg(qd[m], kinv[m], NT_DIMS), 0.0).astype(BF16)
                   for m in range(n - C_HEADS + 1, n + 1)]
            if c % every == every - 1:
                run_later(1)
    run_later(C_HEADS - len(v_h))
    v_ = [v_h[h][rs(c)] for c, h in items]
    o_intra = []
    for c in range(nc):
        o_intra += [_dot(qk[c * C_HEADS + h], v_[c * C_HEADS + h]) for h in range(C_HEADS)]
        if c % every == every - 1:
            run_later(1)
    run_later(len(later))

    rb = min(ct, 4 * CHUNK)

    def gate_store(c, o):
        for h in range(C_HEADS):
            og_scr[rs(c), vs(h)] = (_rms(o[h], onorm_ref[...]) * _silu(z_h[h][rs(c)])).astype(BF16)
        if ((c + 1) * CHUNK) % rb == 0:
            r = slice((c + 1) * CHUNK - rb, (c + 1) * CHUNK)
            y_ref[0, r, :] = _rms(x[r] + _dot(og_scr[r, :], wout_ref[...]), gfin_ref[...])

    st = [st_scr[h] for h in range(C_HEADS)]
    o_prev = None
    pad_rows = jnp.zeros((8 - C_HEADS, C_DK), F32)
    for c in range(nc):
        idx = [c * C_HEADS + h for h in range(C_HEADS)]
        glt = jnp.concatenate([gl[n] for n in idx] + [pad_rows], axis=0).T
        o = [o_intra[n] + _dot(qd[n], st[h].astype(BF16)) for h, n in enumerate(idx)]
        st = [st[h] * glt[:, h:h + 1] + _dotg(kd[n], v_[n], TN_DIMS) for h, n in enumerate(idx)]
        if o_prev is not None:
            gate_store(c - 1, o_prev)
        o_prev = o
    gate_store(nc - 1, o_prev)
    for h in range(C_HEADS):
        st_scr[h] = st[h]

    @pl.when(i == nsteps - 1)
    def _():
        sout_ref[0] = st_scr[...]


def _c_layer_prompt(x3, norm_g, w_main, w_gk, w_up, b_gk, onorm, w_out, final_g, st_in, ct, n_lead):
    B, L, _ = x3.shape
    kern = functools.partial(_c_layer_kernel, ct=ct, n_lead=n_lead)
    full = lambda a: pl.BlockSpec(a.shape, lambda b, i: (0,) * a.ndim)
    blk = pl.BlockSpec((1, ct, D_MODEL), lambda b, i: (b, i, 0))
    st = pl.BlockSpec((1, C_HEADS, C_DK, C_DV), lambda b, i: (b, 0, 0, 0))
    return pl.pallas_call(
        kern,
        grid=(B, L // ct),
        in_specs=[blk, full(norm_g), full(w_main), full(w_gk), full(w_up), full(b_gk), full(onorm),
                  full(w_out), full(final_g), full(st_in)],
        out_specs=[blk, st],
        out_shape=[jax.ShapeDtypeStruct((B, L, D_MODEL), F32),
                   jax.ShapeDtypeStruct((B, C_HEADS, C_DK, C_DV), F32)],
        scratch_shapes=[pltpu.VMEM((C_HEADS, C_DK, C_DV), F32), pltpu.VMEM((ct, C_VAL), BF16)],
        compiler_params=_cparams(("arbitrary", "arbitrary")),
        name="c_layer_prompt",
    )(x3, norm_g, w_main, w_gk, w_up, b_gk, onorm, w_out, final_g, st_in)


def _ab_proj_kernel(x_ref, g_ref, w_ref, wg_ref, qa_ref, kv_ref, za_ref, xb_ref, zb_ref, gates_ref):
    hb = _rms(x_ref[...], g_ref[...]).astype(BF16)
    qa_ref[...] = _dot(hb, w_ref[:, O_QA:O_KV])
    kv_ref[...] = _dot(hb, w_ref[:, O_KV:O_ZA])
    za_ref[...] = _dot(hb, w_ref[:, O_ZA:O_XB])
    xb_ref[...] = _dot(hb, w_ref[:, O_XB:O_ZB])
    zb_ref[...] = _dot(hb, w_ref[:, O_ZB:AB_MAIN])
    gates_ref[...] = _dot(hb, wg_ref[...])


def _ab_proj(x2d, norm_g, w_main, w_gate):
    n = x2d.shape[0]
    full = lambda a: pl.BlockSpec(a.shape, lambda: (0,) * a.ndim)
    widths = (A_WIDTH, 2 * A_KV_WIDTH, A_WIDTH, B_CONV_CH, B_VAL, LANES)
    args = (x2d, norm_g, w_main, w_gate)
    return pl.pallas_call(
        _ab_proj_kernel,
        in_specs=[full(a) for a in args],
        out_specs=[pl.BlockSpec((n, w), lambda: (0, 0)) for w in widths],
        out_shape=[jax.ShapeDtypeStruct((n, w), F32) for w in widths],
        compiler_params=pltpu.CompilerParams(vmem_limit_bytes=VMEM_LIMIT),
        name="ab_in_proj_sample",
    )(*args)


def _swa_sample_kernel(q_ref, kvn_ref, za_ref, ck_ref, cv_ref, sink_ref, o_ref, nk_ref, nv_ref, *, bt):
    row = lax.broadcasted_iota(jnp.int32, (WINDOW, A_KV_WIDTH), 0)
    hrow = lax.broadcasted_iota(jnp.int32, (A_HEADS, A_HD), 0)
    sk = sink_ref[...]
    nkb, nvb = [], []
    for b in range(bt):
        nk = jnp.where(row == WINDOW - 1, kvn_ref[b:b + 1, 0:A_KV_WIDTH], pltpu.roll(ck_ref[b], WINDOW - 1, 0))
        nv = jnp.where(row == WINDOW - 1, kvn_ref[b:b + 1, A_KV_WIDTH:], pltpu.roll(cv_ref[b], WINDOW - 1, 0))
        nk_ref[b] = nk
        nv_ref[b] = nv
        nkb.append(nk.astype(BF16))
        nvb.append(nv.astype(BF16))
    items = [(b, j) for b in range(bt) for j in range(A_KV_HEADS)]
    q8 = [q_ref[b].astype(BF16) for b in range(bt)]
    sc = [_dotg(q8[b], nkb[b][:, j * A_HD:(j + 1) * A_HD], NT_DIMS) * (A_HD ** -0.5) for b, j in items]
    m = [jnp.maximum(jnp.max(s, axis=-1, keepdims=True), sk) for s in sc]
    p = [jnp.exp(sc[n] - m[n]) for n in range(len(items))]
    den = [jnp.sum(p[n], axis=-1, keepdims=True) + jnp.exp(sk - m[n]) for n in range(len(items))]
    oj = [_dot(p[n].astype(BF16), nvb[b][:, j * A_HD:(j + 1) * A_HD]) / den[n] for n, (b, j) in enumerate(items)]
    for b in range(bt):
        o8 = jnp.where(hrow >= A_GROUP, oj[b * A_KV_HEADS + 1], oj[b * A_KV_HEADS])
        o_ref[b] = o8 * _silu(za_ref[b])


def _swa_sample(q3, kv_new, za3, cache_k, cache_v, sink_col, bt):
    nb = q3.shape[0]
    hd = pl.BlockSpec((bt, A_HEADS, A_HD), lambda i: (i, 0, 0))
    cache = pl.BlockSpec((bt, WINDOW, A_KV_WIDTH), lambda i: (i, 0, 0))
    return pl.pallas_call(
        functools.partial(_swa_sample_kernel, bt=bt),
        grid=(nb // bt,),
        in_specs=[hd, pl.BlockSpec((bt, 2 * A_KV_WIDTH), lambda i: (i, 0)), hd, cache, cache,
                  pl.BlockSpec(sink_col.shape, lambda i: (0, 0))],
        out_specs=[hd, cache, cache],
        out_shape=[jax.ShapeDtypeStruct(q3.shape, F32),
                   jax.ShapeDtypeStruct(cache_k.shape, F32),
                   jax.ShapeDtypeStruct(cache_v.shape, F32)],
        compiler_params=_cparams(("arbitrary",)),
        name="swa_sample",
    )(q3, kv_new, za3, cache_k, cache_v, sink_col)


def _gdn_sample_prep_kernel(xb_ref, h0_ref, h1_ref, h2_ref, gates_ref, convw_ref, gprm_ref,
                            q_ref, k_ref, v_ref, bg_ref):
    w = convw_ref[...]
    y = (h0_ref[...] * w[0:1, :] + h1_ref[...] * w[1:2, :] + h2_ref[...] * w[2:3, :]
         + xb_ref[...] * w[3:4, :])
    cv = _silu(y)
    for h in range(B_HEADS):
        q = cv[:, h * B_DK:(h + 1) * B_DK]
        k = cv[:, B_KEY + h * B_DK:B_KEY + (h + 1) * B_DK]
        q_ref[:, h * B_DK:(h + 1) * B_DK] = (
            q * lax.rsqrt(jnp.sum(q * q, axis=-1, keepdims=True) + EPS) * (B_DK ** -0.5))
        k_ref[:, h * B_DK:(h + 1) * B_DK] = k * lax.rsqrt(jnp.sum(k * k, axis=-1, keepdims=True) + EPS)
    v_ref[...] = cv[:, 2 * B_KEY:]
    gt = gates_ref[...]
    col = lax.broadcasted_iota(jnp.int32, gt.shape, 1)
    g = -gprm_ref[0:1, :] * _softplus(gt + gprm_ref[1:2, :])
    bg_ref[...] = jnp.where(col < B_HEADS, _sigmoid(gt), jnp.exp(g))


def _gdn_sample_prep(xb, h0, h1, h2, gates, conv_w8, gprm):
    n = xb.shape[0]
    full = lambda a: pl.BlockSpec(a.shape, lambda: (0,) * a.ndim)
    args = (xb, h0, h1, h2, gates, conv_w8, gprm)
    return pl.pallas_call(
        _gdn_sample_prep_kernel,
        in_specs=[full(a) for a in args],
        out_specs=[pl.BlockSpec((n, B_KEY), lambda: (0, 0))] * 3 + [pl.BlockSpec((n, LANES), lambda: (0, 0))],
        out_shape=[jax.ShapeDtypeStruct((n, B_KEY), F32)] * 3 + [jax.ShapeDtypeStruct((n, LANES), F32)],
        compiler_params=pltpu.CompilerParams(vmem_limit_bytes=VMEM_LIMIT),
        name="gdn_sample_prep",
    )(*args)


def _gdn_sample_kernel(s_ref, qt_ref, kt_ref, v_ref, bg_ref, zb_ref, onorm_ref, o_ref, sout_ref, *, bt):
    items = [(b, h) for b in range(bt) for h in range(B_HEADS)]
    hs = lambda h: slice(h * B_DV, (h + 1) * B_DV)
    kt = [kt_ref[:, hs(h)].T for h in range(B_HEADS)]
    qt = [qt_ref[:, hs(h)].T for h in range(B_HEADS)]
    kc = [kt[h][:, b:b + 1] for b, h in items]
    eg = [bg_ref[b:b + 1, B_HEADS + h:B_HEADS + h + 1] for b, h in items]
    ks = [jnp.sum(kc[n] * s_ref[b, h], axis=0, keepdims=True) for n, (b, h) in enumerate(items)]
    v_new = [bg_ref[b:b + 1, h:h + 1] * (v_ref[b:b + 1, hs(h)] - eg[n] * ks[n]) for n, (b, h) in enumerate(items)]
    o = []
    for n, (b, h) in enumerate(items):
        s_new = eg[n] * s_ref[b, h] + kc[n] * v_new[n]
        sout_ref[b, h] = s_new
        o.append(jnp.sum(qt[h][:, b:b + 1] * s_new, axis=0, keepdims=True))
    o_blk = jnp.concatenate([jnp.concatenate([o[b * B_HEADS + h] for h in range(B_HEADS)], axis=-1)
                             for b in range(bt)], axis=0)
    for h in range(B_HEADS):
        o_ref[:, hs(h)] = _rms(o_blk[:, hs(h)], onorm_ref[...]) * _silu(zb_ref[:, hs(h)])


def _gdn_sample(state, qt, kt, v, bg, zb, onorm, bt):
    nb = state.shape[0]
    st = pl.BlockSpec((bt, B_HEADS, B_DK, B_DV), lambda i: (i, 0, 0, 0))
    col = pl.BlockSpec((1, B_HEADS, B_DK, bt), lambda i: (i, 0, 0, 0))
    row = lambda n: pl.BlockSpec((bt, n), lambda i: (i, 0))
    return pl.pallas_call(
        functools.partial(_gdn_sample_kernel, bt=bt),
        grid=(nb // bt,),
        in_specs=[st, row(B_KEY), row(B_KEY), row(B_VAL), row(LANES), row(B_VAL),
                  pl.BlockSpec(onorm.shape, lambda i: (0, 0))],
        out_specs=[row(B_VAL), st],
        out_shape=[jax.ShapeDtypeStruct((nb, B_VAL), F32), jax.ShapeDtypeStruct(state.shape, F32)],
        compiler_params=_cparams(("arbitrary",)),
        name="gdn_sample",
    )(state, qt, kt, v, bg, zb, onorm)


def _ab_out_c_proj_kernel(x_ref, oa_ref, ob_ref, wout_ref, g_ref, w_ref, wgk_ref, wup_ref, bgk_ref,
                          y_ref, q_ref, k_ref, v_ref, z_ref, la_ref):
    y = (x_ref[...] + _dot(oa_ref[...].astype(BF16), wout_ref[0:A_WIDTH, :])
         + _dot(ob_ref[...].astype(BF16), wout_ref[A_WIDTH:, :]))
    y_ref[...] = y
    hb = _rms(y, g_ref[...]).astype(BF16)
    q_ref[...] = _dot(hb, w_ref[:, 0:C_KEY])
    k_ref[...] = _dot(hb, w_ref[:, C_KEY:2 * C_KEY])
    v_ref[...] = _dot(hb, w_ref[:, 2 * C_KEY:2 * C_KEY + C_VAL])
    z_ref[...] = _dot(hb, w_ref[:, 2 * C_KEY + C_VAL:C_MAIN])
    la_ref[...] = _gla_log_decay(_dot(hb, wgk_ref[...]), wup_ref, bgk_ref)


def _ab_out_c_proj(x2d, oa, ob, w_out, norm_g, w_main, w_gk, w_up, b_gk):
    n = x2d.shape[0]
    full = lambda a: pl.BlockSpec(a.shape, lambda: (0,) * a.ndim)
    widths = (D_MODEL, C_KEY, C_KEY, C_VAL, C_VAL, C_KEY)
    args = (x2d, oa, ob, w_out, norm_g, w_main, w_gk, w_up, b_gk)
    return pl.pallas_call(
        _ab_out_c_proj_kernel,
        in_specs=[full(a) for a in args],
        out_specs=[pl.BlockSpec((n, w), lambda: (0, 0)) for w in widths],
        out_shape=[jax.ShapeDtypeStruct((n, w), F32) for w in widths],
        compiler_params=pltpu.CompilerParams(vmem_limit_bytes=VMEM_LIMIT),
        name="ab_out_c_in_proj_sample",
    )(*args)


def _gla_sample_kernel(s_ref, qt_ref, kt_ref, lat_ref, v_ref, z_ref, onorm_ref, o_ref, sout_ref, *, bt):
    items = [(b, h) for b in range(bt) for h in range(C_HEADS)]
    hs = lambda h: slice(h * C_DV, (h + 1) * C_DV)
    ks = lambda h: slice(h * C_DK, (h + 1) * C_DK)
    at = [jnp.exp(lat_ref[:, ks(h)]).T for h in range(C_HEADS)]
    kt = [kt_ref[:, ks(h)].T for h in range(C_HEADS)]
    qt = [(qt_ref[:, ks(h)] * (C_DK ** -0.5)).T for h in range(C_HEADS)]
    o = []
    for n, (b, h) in enumerate(items):
        s_new = at[h][:, b:b + 1] * s_ref[b, h] + kt[h][:, b:b + 1] * v_ref[b:b + 1, hs(h)]
        sout_ref[b, h] = s_new
        o.append(jnp.sum(qt[h][:, b:b + 1] * s_new, axis=0, keepdims=True))
    o_blk = jnp.concatenate([jnp.concatenate([o[b * C_HEADS + h] for h in range(C_HEADS)], axis=-1)
                             for b in range(bt)], axis=0)
    for h in range(C_HEADS):
        o_ref[:, hs(h)] = _rms(o_blk[:, hs(h)], onorm_ref[...]) * _silu(z_ref[:, hs(h)])


def _gla_sample(state, qt, kt, lat, v, z, onorm, bt):
    nb = state.shape[0]
    st = pl.BlockSpec((bt, C_HEADS, C_DK, C_DV), lambda i: (i, 0, 0, 0))
    col = pl.BlockSpec((1, C_HEADS, C_DK, bt), lambda i: (i, 0, 0, 0))
    row = lambda n: pl.BlockSpec((bt, n), lambda i: (i, 0))
    return pl.pallas_call(
        functools.partial(_gla_sample_kernel, bt=bt),
        grid=(nb // bt,),
        in_specs=[st, row(C_KEY), row(C_KEY), row(C_KEY), row(C_VAL), row(C_VAL),
                  pl.BlockSpec(onorm.shape, lambda i: (0, 0))],
        out_specs=[row(C_VAL), st],
        out_shape=[jax.ShapeDtypeStruct((nb, C_VAL), F32), jax.ShapeDtypeStruct(state.shape, F32)],
        compiler_params=_cparams(("arbitrary",)),
        name="gla_sample",
    )(state, qt, kt, lat, v, z, onorm)


def _c_out_kernel(x_ref, o_ref, w_ref, g_ref, y_ref):
    y_ref[...] = _rms(x_ref[...] + _dot(o_ref[...].astype(BF16), w_ref[...]), g_ref[...])


def _c_out(x2d, oc, w_out, final_g):
    n = x2d.shape[0]
    full = lambda a: pl.BlockSpec(a.shape, lambda: (0,) * a.ndim)
    args = (x2d, oc, w_out, final_g)
    return pl.pallas_call(
        _c_out_kernel,
        in_specs=[full(a) for a in args],
        out_specs=pl.BlockSpec((n, D_MODEL), lambda: (0, 0)),
        out_shape=jax.ShapeDtypeStruct((n, D_MODEL), F32),
        compiler_params=pltpu.CompilerParams(vmem_limit_bytes=VMEM_LIMIT),
        name="c_out_proj_norm_sample",
    )(*args)


def _to_cols(a, heads, dk, bt):
    n = a.shape[0]
    return a.reshape(n // bt, bt, heads, dk).transpose(0, 2, 3, 1)


def kernel(x_prompt, x_sample, cache_swa_k, cache_swa_v, state_dn_conv, state_dn, state_gla,
           meta_tokens, norm_ab, w_in_ab, sink_a, conv_b, a_log_b, dt_bias_b, onorm_b, w_out_ab,
           norm_c, w_in_c, w_gk_up, b_gk, onorm_c, w_out_c, final_norm):
    Bp, L, _ = x_prompt.shape
    Bs = x_sample.shape[0]
    CT, BT = 256, 8

    w_ab = w_in_ab[0]
    w_ab_main = w_ab.astype(BF16)
    w_ab_gate = jnp.pad(w_ab[:, AB_MAIN:], ((0, 0), (0, LANES - 2 * B_HEADS))).astype(BF16)
    w_ab_gate_t = jnp.pad(w_ab[:, AB_MAIN:].T, ((0, 16 - 2 * B_HEADS), (0, 0))).astype(BF16)
    g_ab = norm_ab[0][None, :]
    sink_row = sink_a[0][None, :]
    sink_col = sink_a[0][:, None]
    conv_w8 = jnp.pad(conv_b[0], ((0, 8 - CONV_W), (0, 0)))
    gp = jnp.stack([jnp.exp(a_log_b[0]), dt_bias_b[0]])
    gprm = jnp.pad(gp, ((0, 6), (B_HEADS, LANES - 2 * B_HEADS)))
    gprm_t = jnp.pad(gp.T, ((B_HEADS, 16 - 2 * B_HEADS), (0, LANES - 2)))
    onb = onorm_b[0][None, :]
    w_ab_out = w_out_ab[0].astype(BF16)
    w_c = w_in_c[0]
    w_c_main = w_c.astype(BF16)
    w_c_gk = jnp.pad(w_c[:, C_MAIN:], ((0, 0), (0, LANES - C_RANK))).astype(BF16)
    w_up = jnp.pad(w_gk_up[0], ((0, LANES - C_RANK), (0, 0)))
    bgk = b_gk[0][None, :]
    g_c = norm_c[0][None, :]
    onc = onorm_c[0][None, :]
    w_c_out = w_out_c[0].astype(BF16)
    g_fin = final_norm[None, :]

    def ab_layer(x3, ct, hist_kv, pos_shift, conv_in, s_in, n_lead):
        return _ab_layer_prompt(x3, g_ab, w_ab_main, w_ab_gate, w_ab_gate_t, sink_row, hist_kv, conv_w8,
                                gprm, gprm_t, onb, w_ab_out, conv_in, s_in, ct, n_lead, pos_shift)

    def c_layer(x3, ct, st_in, n_lead):
        return _c_layer_prompt(x3, g_c, w_c_main, w_c_gk, w_up, bgk, onc, w_c_out, g_fin, st_in, ct, n_lead)

    xpre = jnp.concatenate([jnp.zeros((LEAD, D_MODEL), F32), meta_tokens], axis=0)[None]
    y_pre, kv_pre, xbtail_pre, sdn_pre = ab_layer(
        xpre, CHUNK, jnp.zeros((WINDOW, 2 * A_KV_WIDTH), F32), -LEAD,
        jnp.zeros((1, 8, B_CONV_CH), F32), jnp.zeros((1, B_HEADS, B_DK, B_DV), F32), LEAD)
    _, sgla_pre = c_layer(y_pre, CHUNK, jnp.zeros((1, C_HEADS, C_DK, C_DV), F32), LEAD)

    hist_kv = jnp.concatenate([jnp.zeros((WINDOW - CHUNK, 2 * A_KV_WIDTH), F32), kv_pre[0]], axis=0)
    y1, kv_tail, xb_tail, sdn_p = ab_layer(
        x_prompt, 2 * CT, hist_kv, N_META, xbtail_pre, sdn_pre, 0)
    y_prompt, sgla_p = c_layer(y1, 2 * CT, sgla_pre, 0)

    swa_k_p = kv_tail[:, :, :A_KV_WIDTH].reshape(1, Bp, WINDOW, A_KV_HEADS, A_HD)
    swa_v_p = kv_tail[:, :, A_KV_WIDTH:].reshape(1, Bp, WINDOW, A_KV_HEADS, A_HD)
    dn_conv_p = xb_tail[:, 8 - (CONV_W - 1):, :][None]

    xs = x_sample.reshape(Bs, D_MODEL)
    qa, kvn, za, xb, zb, gates = _ab_proj(xs, g_ab, w_ab_main, w_ab_gate)
    oa3, nk, nv = _swa_sample(qa.reshape(Bs, A_HEADS, A_HD), kvn, za.reshape(Bs, A_HEADS, A_HD),
                              cache_swa_k[0].reshape(Bs, WINDOW, A_KV_WIDTH),
                              cache_swa_v[0].reshape(Bs, WINDOW, A_KV_WIDTH), sink_col, BT)
    hist = state_dn_conv[0]
    qn, kn, vn, bg = _gdn_sample_prep(xb, hist[:, 0], hist[:, 1], hist[:, 2], gates, conv_w8, gprm)
    ob, sdn_s = _gdn_sample(state_dn[0], qn, kn,
                            vn, bg, zb, onb, BT)
    ys, qc, kc, vc, zc, la = _ab_out_c_proj(xs, oa3.reshape(Bs, A_WIDTH), ob, w_ab_out,
                                            g_c, w_c_main, w_c_gk, w_up, bgk)
    ocs, sgla_s = _gla_sample(state_gla[0], qc, kc, la, vc, zc, onc, BT)
    y_sample = _c_out(ys, ocs, w_c_out, g_fin).reshape(Bs, 1, D_MODEL)
    dn_conv_s = jnp.concatenate([hist[:, 1:], xb[:, None, :]], axis=1)[None]

    return (y_prompt, y_sample, swa_k_p, swa_v_p, dn_conv_p, sdn_p[None], sgla_p[None],
            nk.reshape(1, Bs, WINDOW, A_KV_HEADS, A_HD), nv.reshape(1, Bs, WINDOW, A_KV_HEADS, A_HD),
            dn_conv_s, sdn_s[None], sgla_s[None])
```

```python
import functools

import jax
import jax.numpy as jnp
from jax import lax
from jax.experimental import pallas as pl
from jax.experimental.pallas import tpu as pltpu

F32 = jnp.float32
BF16 = jnp.bfloat16
EPS = 1e-6

D_MODEL = 1024
N_META = 16
CHUNK = 64
LEAD = (-N_META) % CHUNK
A_HEADS, A_KV_HEADS, A_HD = 8, 2, 64
A_GROUP = A_HEADS // A_KV_HEADS
A_WIDTH = A_HEADS * A_HD
A_KV_WIDTH = A_KV_HEADS * A_HD
WINDOW = 128
B_HEADS, B_DK, B_DV = 4, 128, 128
B_KEY = B_HEADS * B_DK
B_VAL = B_HEADS * B_DV
CONV_W = 4
B_CONV_CH = 2 * B_KEY + B_VAL
C_HEADS, C_DK, C_DV = 4, 128, 256
C_KEY = C_HEADS * C_DK
C_VAL = C_HEADS * C_DV
C_RANK = 16
C_GATE_NORM = 16.0
AB_MAIN = 2 * A_WIDTH + 2 * A_KV_WIDTH + B_CONV_CH + B_VAL
AB_MIX = A_WIDTH + B_VAL
C_MAIN = 2 * C_KEY + 2 * C_VAL
O_QA, O_KV, O_ZA, O_XB, O_ZB = 0, 512, 768, 1280, 2816
LANES = 128
NEG = -1e30

VMEM_LIMIT = 56 * 1024 * 1024

NT_DIMS = (((1,), (1,)), ((), ()))
TN_DIMS = (((0,), (0,)), ((), ()))


def _cparams(sem, **kw):
    return pltpu.CompilerParams(dimension_semantics=sem, vmem_limit_bytes=VMEM_LIMIT, **kw)


def _dot(a, b):
    return jnp.dot(a, b, preferred_element_type=F32)


def _dotg(a, b, dims):
    return lax.dot_general(a, b, dims, preferred_element_type=F32)


def _split3(a):
    hi = a.astype(BF16)
    r = a - hi.astype(F32)
    mid = r.astype(BF16)
    lo = (r - mid.astype(F32)).astype(BF16)
    return hi, mid, lo


def _split2(a):
    hi = a.astype(BF16)
    return hi, (a - hi.astype(F32)).astype(BF16)


def _sigmoid(x):
    return 1.0 / (1.0 + jnp.exp(-x))


def _silu(x):
    return x * _sigmoid(x)


def _softplus(x):
    return jnp.maximum(x, 0.0) + jnp.log1p(jnp.exp(-jnp.abs(x)))


def _log_sigmoid(x):
    return jnp.minimum(x, 0.0) - jnp.log1p(jnp.exp(-jnp.abs(x)))


def _rms(x, g):
    return x * lax.rsqrt(jnp.mean(x * x, axis=-1, keepdims=True) + EPS) * g


def _gla_log_decay(low, wup_ref, bgk_ref):
    lh, lm, _ = _split3(low)
    wh, wm, _ = _split3(wup_ref[...])
    up = _dot(lh, wh) + _dot(lh, wm) + _dot(lm, wh)
    return _log_sigmoid(up + bgk_ref[...]) * (1.0 / C_GATE_NORM)


def _swa_tile(i, qa, kvb, za, sink_ref, mix_scr, *, ct, sb, pos_shift):
    r = lax.broadcasted_iota(jnp.int32, (sb, WINDOW + sb), 0)
    c = lax.broadcasted_iota(jnp.int32, (sb, WINDOW + sb), 1)
    diff = r - c + WINDOW
    in_window = (diff >= 0) & (diff < WINDOW)
    nsb = ct // sb
    items = [(s, j) for s in range(nsb) for j in range(A_KV_HEADS)]
    masks = [jnp.concatenate([in_window & (i * ct + s * sb + c - WINDOW + pos_shift >= 0)] * A_GROUP, axis=0)
             for s in range(nsb)]
    sinks = [jnp.concatenate([jnp.broadcast_to(sink_ref[0:1, j * A_GROUP + g:j * A_GROUP + g + 1], (sb, 1))
                              for g in range(A_GROUP)], axis=0) for j in range(A_KV_HEADS)]
    qs = [jnp.concatenate([qa[s * sb:(s + 1) * sb, (j * A_GROUP + g) * A_HD:(j * A_GROUP + g + 1) * A_HD]
                           for g in range(A_GROUP)], axis=0).astype(BF16) for s, j in items]
    sc = [_dotg(qs[n], kvb[s * sb:s * sb + WINDOW + sb, j * A_HD:(j + 1) * A_HD], NT_DIMS) * (A_HD ** -0.5)
          for n, (s, j) in enumerate(items)]
    sc = [jnp.where(masks[s], sc[n], NEG) for n, (s, j) in enumerate(items)]
    mx = [jnp.maximum(jnp.max(sc[n], axis=-1, keepdims=True), sinks[j]) for n, (s, j) in enumerate(items)]
    p = [jnp.exp(sc[n] - mx[n]) for n in range(len(items))]
    den = [jnp.sum(p[n], axis=-1, keepdims=True) + jnp.exp(sinks[j] - mx[n]) for n, (s, j) in enumerate(items)]
    pv = [_dot(p[n].astype(BF16),
               kvb[s * sb:s * sb + WINDOW + sb, A_KV_WIDTH + j * A_HD:A_KV_WIDTH + (j + 1) * A_HD]) / den[n]
          for n, (s, j) in enumerate(items)]
    for s in range(nsb):
        o = jnp.concatenate([pv[s * A_KV_HEADS + j][g * sb:(g + 1) * sb]
                             for j in range(A_KV_HEADS) for g in range(A_GROUP)], axis=-1)
        mix_scr[s * sb:(s + 1) * sb, 0:A_WIDTH] = (o * _silu(za[s * sb:(s + 1) * sb])).astype(BF16)


def _gdn_tile(i, cv, gt, gtt, zb, gprm_ref, gprmt_ref, onorm_ref, s_scr, mix_scr, *, ct, n_lead):
    beta = _sigmoid(gt)
    rows = i * ct + lax.broadcasted_iota(jnp.int32, gt.shape, 0)
    gcol = jnp.where(rows >= n_lead, -gprm_ref[0:1, :] * _softplus(gt + gprm_ref[1:2, :]), 0.0)
    lanes = i * ct + lax.broadcasted_iota(jnp.int32, gtt.shape, 1)
    grow = jnp.where(lanes >= n_lead, -gprmt_ref[:, 0:1] * _softplus(gtt + gprmt_ref[:, 1:2]), 0.0)

    ii = lax.broadcasted_iota(jnp.int32, (CHUNK, CHUNK), 0)
    jj = lax.broadcasted_iota(jnp.int32, (CHUNK, CHUNK), 1)
    lower = ii >= jj
    strict = ii > jj
    ltri = lower.astype(BF16)
    utri = (ii <= jj).astype(BF16)

    nc = ct // CHUNK
    items = [(c, h) for c in range(nc) for h in range(B_HEADS)]
    n_items = len(items)
    rs = lambda c: slice(c * CHUNK, (c + 1) * CHUNK)
    qn, kn = [], []
    for h in range(B_HEADS):
        q = cv[:, h * B_DK:(h + 1) * B_DK]
        k = cv[:, B_KEY + h * B_DK:B_KEY + (h + 1) * B_DK]
        qn.append(q * lax.rsqrt(jnp.sum(q * q, axis=-1, keepdims=True) + EPS) * (B_DK ** -0.5))
        kn.append(k * lax.rsqrt(jnp.sum(k * k, axis=-1, keepdims=True) + EPS))
    gcum_c = [sum(_dot(ltri, p) for p in _split3(gcol[rs(c)])) for c in range(nc)]
    gcum_r = [sum(_dot(p, utri) for p in _split3(grow[:, rs(c)])) for c in range(nc)]
    q_ = [qn[h][rs(c)] for c, h in items]
    k_ = [kn[h][rs(c)] for c, h in items]
    v_ = [cv[rs(c), 2 * B_KEY + h * B_DV:2 * B_KEY + (h + 1) * B_DV] for c, h in items]
    bh = [beta[rs(c), h:h + 1] for c, h in items]
    gc = [gcum_c[c][:, B_HEADS + h:B_HEADS + h + 1] for c, h in items]
    gr = [gcum_r[c][B_HEADS + h:B_HEADS + h + 1, :] for c, h in items]
    decay = [jnp.exp(jnp.where(lower, gc[n] - gr[n], NEG)) for n in range(n_items)]
    kb = [k_[n] * bh[n] for n in range(n_items)]
    kbf = [k_[n].astype(BF16) for n in range(n_items)]
    kq = [_dotg(jnp.concatenate([kb[n], q_[n]], axis=0).astype(BF16), kbf[n], NT_DIMS) for n in range(n_items)]
    a = [jnp.where(strict, kq[n][:CHUNK] * decay[n], 0.0) for n in range(n_items)]
    qk = [jnp.where(lower, kq[n][CHUNK:] * decay[n], 0.0).astype(BF16) for n in range(n_items)]
    doff = [-jnp.where((ii >> 1) == (jj >> 1), a[n], 0.0) for n in range(n_items)]
    for lg in range(1, 6):
        m = ((ii >> (lg + 1)) == (jj >> (lg + 1))) & ((ii >> lg) != (jj >> lg))
        bm = [jnp.where(m, a[n], 0.0) for n in range(n_items)]
        db = [doff[n].astype(BF16) for n in range(n_items)]
        y = [bm[n] + _dot(db[n], bm[n].astype(BF16)) for n in range(n_items)]
        x = [y[n] + _dot(y[n].astype(BF16), db[n]) for n in range(n_items)]
        doff = [doff[n] - x[n] for n in range(n_items)]
    egc = [jnp.exp(gc[n]) for n in range(n_items)]
    rhs = [jnp.concatenate([v_[n] * bh[n], kb[n] * egc[n]], axis=-1) for n in range(n_items)]
    sol = [rhs[n] + _dot(doff[n].astype(BF16), rhs[n].astype(BF16)) for n in range(n_items)]
    g_last = [gc[n][CHUNK - 1:CHUNK, :] for n in range(n_items)]
    kd = [(k_[n] * jnp.exp(g_last[n] - gc[n])).astype(BF16) for n in range(n_items)]
    wq = [jnp.concatenate([sol[n][:, B_DV:], q_[n] * egc[n]], axis=0).astype(BF16) for n in range(n_items)]
    eg_last = [jnp.exp(g_last[n]) for n in range(n_items)]

    s_cur = [s_scr[h] for h in range(B_HEADS)]
    for c in range(nc):
        idx = [c * B_HEADS + h for h in range(B_HEADS)]
        ws = [_dot(wq[n], s_cur[h].astype(BF16)) for h, n in enumerate(idx)]
        v_new = [sol[n][:, :B_DV] - ws[h][:CHUNK] for h, n in enumerate(idx)]
        vb = [v_new[h].astype(BF16) for h in range(B_HEADS)]
        o = [ws[h][CHUNK:] + _dot(qk[n], vb[h]) for h, n in enumerate(idx)]
        s_cur = [s_cur[h] * eg_last[n] + _dotg(kd[n], vb[h], TN_DIMS) for h, n in enumerate(idx)]
        for h in range(B_HEADS):
            z = zb[rs(c), h * B_DV:(h + 1) * B_DV]
            mix_scr[rs(c), A_WIDTH + h * B_DV:A_WIDTH + (h + 1) * B_DV] = (
                _rms(o[h], onorm_ref[...]) * _silu(z)).astype(BF16)
    for h in range(B_HEADS):
        s_scr[h] = s_cur[h]


def _ab_layer_kernel(x_ref, g_ref, w_ref, wg_ref, wgt_ref, sink_ref, hist_ref, convw_ref, gprm_ref,
                     gprmt_ref, onorm_ref, wout_ref, convin_ref, sin_ref,
                     y_ref, kvtail_ref, xbtail_ref, sout_ref,
                     s_scr, xc_scr, kvprev_scr, mix_scr, *, ct, n_lead, pos_shift):
    i = pl.program_id(1)
    nsteps = pl.num_programs(1)
    ntail = min(WINDOW, ct)

    @pl.when(i == 0)
    def _():
        s_scr[...] = sin_ref[0]
        xc_scr[0:8, :] = convin_ref[0]
        kvprev_scr[1] = hist_ref[...]

    x = x_ref[0]
    hb = _rms(x, g_ref[...]).astype(BF16)
    qa = _dot(hb, w_ref[:, O_QA:O_KV])
    kv = _dot(hb, w_ref[:, O_KV:O_ZA])
    za = _dot(hb, w_ref[:, O_ZA:O_XB])
    xb = _dot(hb, w_ref[:, O_XB:O_ZB])
    zb = _dot(hb, w_ref[:, O_ZB:AB_MAIN])
    gt = _dot(hb, wg_ref[...])
    gtt = _dotg(wgt_ref[...], hb, NT_DIMS)
    kvtail_ref[0] = kv[ct - ntail:]
    xbtail_ref[0] = xb[ct - 8:]

    slot = lax.rem(i, 2)
    kvb = jnp.concatenate([kvprev_scr[1 - slot], kv], axis=0).astype(BF16)
    if ct >= WINDOW:
        kvprev_scr[slot] = kv[ct - WINDOW:]
    _swa_tile(i, qa, kvb, za, sink_ref, mix_scr, ct=ct, sb=ntail, pos_shift=pos_shift)

    xc_scr[8:8 + ct, :] = xb
    w = convw_ref[...]
    yc = xc_scr[pl.ds(8 - (CONV_W - 1), ct), :] * w[0:1, :]
    for t in range(1, CONV_W):
        yc = yc + xc_scr[pl.ds(8 - (CONV_W - 1) + t, ct), :] * w[t:t + 1, :]
    cv = _silu(yc)
    xc_scr[0:8, :] = xc_scr[ct:ct + 8, :]
    _gdn_tile(i, cv, gt, gtt, zb, gprm_ref, gprmt_ref, onorm_ref, s_scr, mix_scr, ct=ct, n_lead=n_lead)

    y_ref[0] = x + _dot(mix_scr[...], wout_ref[...])

    @pl.when(i == nsteps - 1)
    def _():
        sout_ref[0] = s_scr[...]


def _swa_stages(i, qa, kvb, env, sink_ref, mix_scr, *, ct, sb, pos_shift):
    r = lax.broadcasted_iota(jnp.int32, (sb, WINDOW + sb), 0)
    c = lax.broadcasted_iota(jnp.int32, (sb, WINDOW + sb), 1)
    diff = r - c + WINDOW
    in_window = (diff >= 0) & (diff < WINDOW)
    nsb = ct // sb
    items = [(s, j) for s in range(nsb) for j in range(A_KV_HEADS)]
    masks = [jnp.concatenate([in_window & (i * ct + s * sb + c - WINDOW + pos_shift >= 0)] * A_GROUP, axis=0)
             for s in range(nsb)]
    sinks = [jnp.concatenate([jnp.broadcast_to(sink_ref[0:1, j * A_GROUP + g:j * A_GROUP + g + 1], (sb, 1))
                              for g in range(A_GROUP)], axis=0) for j in range(A_KV_HEADS)]
    sc = []
    for s, j in items:
        qs = (jnp.concatenate([qa[s * sb:(s + 1) * sb, (j * A_GROUP + g) * A_HD:(j * A_GROUP + g + 1) * A_HD]
                               for g in range(A_GROUP)], axis=0) * (A_HD ** -0.5)).astype(BF16)
        sc.append(_dotg(qs, kvb[s * sb:s * sb + WINDOW + sb, j * A_HD:(j + 1) * A_HD], NT_DIMS))
        yield
    p, esk = [], []
    for n, (s, j) in enumerate(items):
        scm = jnp.where(masks[s], sc[n], NEG)
        mx = jnp.maximum(jnp.max(scm, axis=-1, keepdims=True), sinks[j])
        p.append(jnp.exp(scm - mx).astype(BF16))
        esk.append(jnp.exp(sinks[j] - mx))
        yield
    pv = []
    ones = jnp.ones((WINDOW + sb, A_HD), BF16)
    for n, (s, j) in enumerate(items):
        vx = jnp.concatenate([kvb[s * sb:s * sb + WINDOW + sb,
                                  A_KV_WIDTH + j * A_HD:A_KV_WIDTH + (j + 1) * A_HD], ones], axis=-1)
        pvx = _dot(p[n], vx)
        pv.append(pvx[:, :A_HD] / (pvx[:, A_HD:] + esk[n]))
        yield
    env['pv'] = pv


def _swa_store(env, mix_scr, *, ct, sb):
    pv = env['pv']
    for s in range(ct // sb):
        o = jnp.concatenate([pv[s * A_KV_HEADS + j][g * sb:(g + 1) * sb]
                             for j in range(A_KV_HEADS) for g in range(A_GROUP)], axis=-1)
        mix_scr[s * sb:(s + 1) * sb, 0:A_WIDTH] = (o * _silu(env['za'][s * sb:(s + 1) * sb])).astype(BF16)


def _gdn_stages(i, cvb, gt, gtt, env, gprm_ref, gprmt_ref, onorm_ref, s_scr, mix_scr, *, ct, n_lead):
    nc = ct // CHUNK
    items = [(c, h) for c in range(nc) for h in range(B_HEADS)]
    n_items = len(items)
    rs = lambda c: slice(c * CHUNK, (c + 1) * CHUNK)
    qn, kn = [], []
    for h in range(B_HEADS):
        q = cvb[h]
        k = cvb[B_HEADS + h]
        qn.append(q * lax.rsqrt(jnp.sum(q * q, axis=-1, keepdims=True) + EPS) * (B_DK ** -0.5))
        kn.append(k * lax.rsqrt(jnp.sum(k * k, axis=-1, keepdims=True) + EPS))
        yield
    beta = _sigmoid(gt)
    rows = i * ct + lax.broadcasted_iota(jnp.int32, gt.shape, 0)
    gcol = jnp.where(rows >= n_lead, -gprm_ref[0:1, :] * _softplus(gt + gprm_ref[1:2, :]), 0.0)
    lanes = i * ct + lax.broadcasted_iota(jnp.int32, gtt.shape, 1)
    grow = jnp.where(lanes >= n_lead, -gprmt_ref[:, 0:1] * _softplus(gtt + gprmt_ref[:, 1:2]), 0.0)
    ii = lax.broadcasted_iota(jnp.int32, (CHUNK, CHUNK), 0)
    jj = lax.broadcasted_iota(jnp.int32, (CHUNK, CHUNK), 1)
    lower = ii >= jj
    strict = ii > jj
    ltri = lower.astype(BF16)
    utri = (ii <= jj).astype(BF16)
    gcum_c = [sum(_dot(ltri, p) for p in _split3(gcol[rs(c)])) for c in range(nc)]
    gcum_r = [sum(_dot(p, utri) for p in _split3(grow[:, rs(c)])) for c in range(nc)]
    yield
    q_ = [qn[h][rs(c)] for c, h in items]
    k_ = [kn[h][rs(c)] for c, h in items]
    v_ = [cvb[2 * B_HEADS + h][rs(c)] for c, h in items]
    bh = [beta[rs(c), h:h + 1] for c, h in items]
    gc = [gcum_c[c][:, B_HEADS + h:B_HEADS + h + 1] for c, h in items]
    gr = [gcum_r[c][B_HEADS + h:B_HEADS + h + 1, :] for c, h in items]
    pairs = [(c, p) for c in range(nc) for p in range(B_HEADS // 2)]
    n_pairs = len(pairs)
    first = lambda c, p: c * B_HEADS + 2 * p
    row = lax.broadcasted_iota(jnp.int32, (CHUNK, 2 * CHUNK), 0)
    lane = lax.broadcasted_iota(jnp.int32, (CHUNK, 2 * CHUNK), 1)
    col = lane & (CHUNK - 1)
    left = lane < CHUNK
    lower2 = row >= col
    strict2 = row > col
    zk = jnp.zeros((CHUNK, B_DK), BF16)

    def bdiag(t):
        return jnp.concatenate([jnp.where(left, t, 0.0), jnp.where(left, 0.0, t)], axis=0)

    def bstack(t0, t1, z):
        return jnp.concatenate([jnp.concatenate([t0, z], axis=1), jnp.concatenate([z, t1], axis=1)], axis=0)

    kb = [k_[n] * bh[n] for n in range(n_items)]
    a, qk = [], []
    for c, p in pairs:
        n0 = first(c, p)
        n1 = n0 + 1
        decay = jnp.exp(jnp.where(lower2, jnp.where(left, gc[n0], gc[n1])
                                  - jnp.concatenate([gr[n0], gr[n1]], axis=1), NEG))
        lhs = jnp.concatenate([jnp.concatenate([kb[n0], kb[n1]], axis=1),
                               jnp.concatenate([q_[n0], q_[n1]], axis=1)], axis=0).astype(BF16)
        kq = _dotg(lhs, bstack(k_[n0].astype(BF16), k_[n1].astype(BF16), zk), NT_DIMS)
        a.append(jnp.where(strict2, kq[:CHUNK] * decay, 0.0))
        qk.append(jnp.where(lower2, kq[CHUNK:] * decay, 0.0).astype(BF16))
        if p == B_HEADS // 2 - 1:
            yield
    doff = [-jnp.where((row >> 1) == (col >> 1), a[n], 0.0) for n in range(n_pairs)]
    for lg in range(1, 6):
        m = ((row >> (lg + 1)) == (col >> (lg + 1))) & ((row >> lg) != (col >> lg))
        bm = [jnp.where(m, a[n], 0.0) for n in range(n_pairs)]
        y = [bm[n] + _dot(doff[n].astype(BF16), bdiag(bm[n]).astype(BF16)) for n in range(n_pairs)]
        yield
        x = [y[n] + _dot(y[n].astype(BF16), bdiag(doff[n]).astype(BF16)) for n in range(n_pairs)]
        doff = [doff[n] - x[n] for n in range(n_pairs)]
        yield
    egc = [jnp.exp(gc[n]) for n in range(n_items)]
    g_last = [gc[n][CHUNK - 1:CHUNK, :] for n in range(n_items)]
    eg_last = [jnp.exp(g_last[n]) for n in range(n_items)]
    sol, kd, wq = [], [], []
    zr = jnp.zeros((CHUNK, 2 * B_DV), BF16)
    for m_, (c, p) in enumerate(pairs):
        n0 = first(c, p)
        rhs = [jnp.concatenate([v_[n] * bh[n], kb[n] * egc[n]], axis=-1) for n in (n0, n0 + 1)]
        t = _dot(doff[m_].astype(BF16), bstack(rhs[0].astype(BF16), rhs[1].astype(BF16), zr))
        for e, n in enumerate((n0, n0 + 1)):
            sol.append(rhs[e] + t[:, e * 2 * B_DV:(e + 1) * 2 * B_DV])
            kd.append((k_[n] * jnp.exp(g_last[n] - gc[n])).astype(BF16))
            wq.append(jnp.concatenate([sol[n][:, B_DV:], q_[n] * egc[n]], axis=0).astype(BF16))
        if p == B_HEADS // 2 - 1:
            yield

    def gate_store(c, o):
        for h in range(B_HEADS):
            z = env['zb'][rs(c), h * B_DV:(h + 1) * B_DV]
            mix_scr[rs(c), A_WIDTH + h * B_DV:A_WIDTH + (h + 1) * B_DV] = (
                _rms(o[h], onorm_ref[...]) * _silu(z)).astype(BF16)

    s_cur = [s_scr[h] for h in range(B_HEADS)]
    o_prev = None
    for c in range(nc):
        idx = [c * B_HEADS + h for h in range(B_HEADS)]
        ws = [_dot(wq[n], s_cur[h].astype(BF16)) for h, n in enumerate(idx)]
        if o_prev is not None:
            gate_store(c - 1, o_prev)
        yield
        v_new = [sol[n][:, :B_DV] - ws[h][:CHUNK] for h, n in enumerate(idx)]
        vb = [v_new[h].astype(BF16) for h in range(B_HEADS)]
        qv = [_dot(qk[c * (B_HEADS // 2) + p], bstack(vb[2 * p], vb[2 * p + 1], zk))
              for p in range(B_HEADS // 2)]
        o_prev = [ws[h][CHUNK:] + qv[h // 2][:, (h % 2) * B_DV:(h % 2 + 1) * B_DV] for h in range(B_HEADS)]
        s_cur = [s_cur[h] * eg_last[n] + _dotg(kd[n], vb[h], TN_DIMS) for h, n in enumerate(idx)]
        yield
    for h in range(B_HEADS):
        s_scr[h] = s_cur[h]
    gate_store(nc - 1, o_prev)
    yield


def _ab_layer_staged_kernel(x_ref, g_ref, w_ref, wg_ref, wgt_ref, sink_ref, hist_ref, convw_ref, gprm_ref,
                            gprmt_ref, onorm_ref, wout_ref, convin_ref, sin_ref,
                            y_ref, kvtail_ref, xbtail_ref, sout_ref,
                            s_scr, xc_scr, kvprev_scr, mix_scr, *, ct, n_lead, pos_shift):
    i = pl.program_id(1)
    nsteps = pl.num_programs(1)
    ntail = min(WINDOW, ct)

    @pl.when(i == 0)
    def _():
        s_scr[...] = sin_ref[0]
        xc_scr[0:8, :] = convin_ref[0]
        kvprev_scr[1] = hist_ref[...]

    x = x_ref[0]
    hb = _rms(x, g_ref[...]).astype(BF16)
    proj = lambda a, b: _dot(hb, w_ref[:, a:b])
    env = {}

    def step(gen, n=1):
        for _ in range(n):
            next(gen, None)

    nblk = B_CONV_CH // LANES
    w = convw_ref[...]
    cvb = []

    xblk = []
    row8 = lax.broadcasted_iota(jnp.int32, (8, LANES), 0)

    def conv_block(j):
        cs = slice(j * LANES, (j + 1) * LANES)
        xj = xblk[j]
        h8 = xc_scr[0:8, cs]
        yc = xj * w[CONV_W - 1:CONV_W, cs]
        for k in range(1, CONV_W):
            sh = pltpu.roll(xj, k, 0)
            top = jnp.where(row8 < k, pltpu.roll(h8, k, 0), sh[0:8])
            yc = yc + jnp.concatenate([top, sh[8:]], axis=0) * w[CONV_W - 1 - k:CONV_W - k, cs]
        cvb.append(_silu(yc))
        xc_scr[0:8, cs] = xj[ct - 8:]

    pw = 2 * LANES
    for j in range(nblk // 2):
        xbj = proj(O_XB + j * pw, O_XB + (j + 1) * pw)
        xblk += [xbj[:, 0:LANES], xbj[:, LANES:]]
        xbtail_ref[0, :, j * pw:(j + 1) * pw] = xbj[ct - 8:]
        if j > 0:
            conv_block(2 * j - 2)
            conv_block(2 * j - 1)
    gt = _dot(hb, wg_ref[...])
    gtt = _dotg(wgt_ref[...], hb, NT_DIMS)
    conv_block(nblk - 2)
    conv_block(nblk - 1)

    gdn = _gdn_stages(i, cvb, gt, gtt, env, gprm_ref, gprmt_ref, onorm_ref, s_scr, mix_scr, ct=ct, n_lead=n_lead)
    qa_p = []
    for j in range(A_WIDTH // pw):
        qa_p.append(proj(O_QA + j * pw, O_QA + (j + 1) * pw))
        step(gdn, 2)
    qa = jnp.concatenate(qa_p, axis=-1)
    kv = proj(O_KV, O_ZA)
    kvtail_ref[0] = kv[ct - ntail:]
    slot = lax.rem(i, 2)
    kvb = jnp.concatenate([kvprev_scr[1 - slot], kv], axis=0).astype(BF16)
    if ct >= WINDOW:
        kvprev_scr[slot] = kv[ct - WINDOW:]
    swa = _swa_stages(i, qa, kvb, env, sink_ref, mix_scr, ct=ct, sb=ntail, pos_shift=pos_shift)
    n_sw = (ct // ntail) * A_KV_HEADS
    step(gdn)
    for _ in range(ct // CHUNK):
        step(gdn)
        step(swa)
    step(swa, max(n_sw - ct // CHUNK, 0))
    za_p, zb_p = [], []
    fill = ([lambda j=j: za_p.append(proj(O_ZA + j * pw, O_ZA + (j + 1) * pw)) for j in range(A_WIDTH // pw)]
            + [lambda j=j: zb_p.append(proj(O_ZB + j * pw, O_ZB + (j + 1) * pw)) for j in range(B_VAL // pw)])
    for lv in range(10):
        step(gdn)
        if lv % 2 == 1 and fill:
            fill.pop(0)()
        elif lv % 2 == 0:
            step(swa)
    for f in fill:
        f()
    env['za'] = jnp.concatenate(za_p, axis=-1)
    env['zb'] = jnp.concatenate(zb_p, axis=-1)
    step(gdn, ct // CHUNK)
    for _ in range(2 * (ct // CHUNK)):
        step(gdn)
        step(swa, -(-2 * n_sw // (2 * (ct // CHUNK))))
    for _ in swa:
        pass
    _swa_store(env, mix_scr, ct=ct, sb=ntail)
    ya = x + _dot(mix_scr[:, 0:A_WIDTH], wout_ref[0:A_WIDTH, :])
    for _ in gdn:
        pass
    y_ref[0] = ya + _dot(mix_scr[:, A_WIDTH:], wout_ref[A_WIDTH:, :])

    @pl.when(i == nsteps - 1)
    def _():
        sout_ref[0] = s_scr[...]


def _ab_layer_prompt(x3, norm_g, w_main, w_gate, w_gate_t, sink, hist_kv, conv_w8, gprm, gprm_t, onorm,
                     w_out, conv_in, s_in, ct, n_lead, pos_shift):
    B, L, _ = x3.shape
    ntail = min(WINDOW, ct)
    kern = functools.partial(_ab_layer_staged_kernel, ct=ct, n_lead=n_lead, pos_shift=pos_shift)
    full = lambda a: pl.BlockSpec(a.shape, lambda b, i: (0,) * a.ndim)
    blk = pl.BlockSpec((1, ct, D_MODEL), lambda b, i: (b, i, 0))
    per_b = lambda *s: pl.BlockSpec((1,) + s, lambda b, i: (b,) + (0,) * len(s))
    return pl.pallas_call(
        kern,
        grid=(B, L // ct),
        in_specs=[blk, full(norm_g), full(w_main), full(w_gate), full(w_gate_t), full(sink), full(hist_kv),
                  full(conv_w8), full(gprm), full(gprm_t), full(onorm), full(w_out),
                  full(conv_in), full(s_in)],
        out_specs=[blk, per_b(ntail, 2 * A_KV_WIDTH), per_b(8, B_CONV_CH), per_b(B_HEADS, B_DK, B_DV)],
        out_shape=[jax.ShapeDtypeStruct((B, L, D_MODEL), F32),
                   jax.ShapeDtypeStruct((B, ntail, 2 * A_KV_WIDTH), F32),
                   jax.ShapeDtypeStruct((B, 8, B_CONV_CH), F32),
                   jax.ShapeDtypeStruct((B, B_HEADS, B_DK, B_DV), F32)],
        scratch_shapes=[pltpu.VMEM((B_HEADS, B_DK, B_DV), F32),
                        pltpu.VMEM((8, B_CONV_CH), F32),
                        pltpu.VMEM((2, WINDOW, 2 * A_KV_WIDTH), F32),
                        pltpu.VMEM((ct, AB_MIX), BF16)],
        compiler_params=_cparams(("arbitrary", "arbitrary")),
        name="ab_layer_prompt",
    )(x3, norm_g, w_main, w_gate, w_gate_t, sink, hist_kv, conv_w8, gprm, gprm_t, onorm, w_out, conv_in, s_in)


def _c_layer_kernel(x_ref, g_ref, w_ref, wgk_ref, wup_ref, bgk_ref, onorm_ref, wout_ref, gfin_ref,
                    sin_ref, y_ref, sout_ref, st_scr, og_scr, *, ct, n_lead):
    i = pl.program_id(1)
    nsteps = pl.num_programs(1)

    @pl.when(i == 0)
    def _():
        st_scr[...] = sin_ref[0]

    x = x_ref[0]
    hb = _rms(x, g_ref[...]).astype(BF16)
    low = _dot(hb, wgk_ref[...])
    q_all = _dot(hb, w_ref[:, 0:C_KEY])
    la_all = _gla_log_decay(low, wup_ref, bgk_ref)
    k_all = _dot(hb, w_ref[:, C_KEY:2 * C_KEY])
    v_h, z_h = [], []
    later = ([lambda h=h: v_h.append(_dot(hb, w_ref[:, 2 * C_KEY + h * C_DV:2 * C_KEY + (h + 1) * C_DV]).astype(BF16))
              for h in range(C_HEADS)]
             + [lambda h=h: z_h.append(_dot(hb, w_ref[:, 2 * C_KEY + C_VAL + h * C_DV:
                                                       2 * C_KEY + C_VAL + (h + 1) * C_DV]))
                for h in range(C_HEADS)])

    def run_later(n):
        for _ in range(n):
            if later:
                later.pop(0)()
    rows = i * ct + lax.broadcasted_iota(jnp.int32, la_all.shape, 0)
    la_all = jnp.where(rows >= n_lead, la_all, 0.0)

    ii = lax.broadcasted_iota(jnp.int32, (CHUNK, CHUNK), 0)
    jj = lax.broadcasted_iota(jnp.int32, (CHUNK, CHUNK), 1)
    lower = ii >= jj
    ltri = lower.astype(BF16)
    nc = ct // CHUNK
    rs = lambda c: slice(c * CHUNK, (c + 1) * CHUNK)
    ks = lambda h: slice(h * C_DK, (h + 1) * C_DK)
    vs = lambda h: slice(h * C_DV, (h + 1) * C_DV)
    items = [(c, h) for c in range(nc) for h in range(C_HEADS)]
    n_items = len(items)
    bc_all = [sum(_dot(ltri, p) for p in _split2(la_all[rs(c)])) for c in range(nc)]
    run_later(2)
    every =max(nc // C_HEADS, 1)
    qd, kinv, kd, gl, qk = [], [], [], [], []
    for n, (c, h) in enumerate(items):
        bc = bc_all[c][:, ks(h)]
        k = k_all[rs(c), ks(h)]
        bl = bc[CHUNK - 1:CHUNK, :]
        qd.append((q_all[rs(c), ks(h)] * (C_DK ** -0.5) * jnp.exp(bc)).astype(BF16))
        kinv.append((k * jnp.exp(-bc)).astype(BF16))
        kd.append((k * jnp.exp(bl - bc)).astype(BF16))
        gl.append(jnp.exp(bl))
        if h == C_HEADS - 1:
            qk += [jnp.where(lower, _dotg(qd[m], kinv[m], NT_DIMS), 0.0).astype(BF16)
                   for m in range(n - C_HEADS + 1, n + 1)]
            if c % every == every - 1:
                run_later(1)
    run_later(C_HEADS - len(v_h))
    v_ = [v_h[h][rs(c)] for c, h in items]
    o_intra = []
    for c in range(nc):
        o_intra += [_dot(qk[c * C_HEADS + h], v_[c * C_HEADS + h]) for h in range(C_HEADS)]
        if c % every == every - 1:
            run_later(1)
    run_later(len(later))

    rb = min(ct, 4 * CHUNK)

    def gate_store(c, o):
        for h in range(C_HEADS):
            og_scr[rs(c), vs(h)] = (_rms(o[h], onorm_ref[...]) * _silu(z_h[h][rs(c)])).astype(BF16)
        if ((c + 1) * CHUNK) % rb == 0:
            r = slice((c + 1) * CHUNK - rb, (c + 1) * CHUNK)
            y_ref[0, r, :] = _rms(x[r] + _dot(og_scr[r, :], wout_ref[...]), gfin_ref[...])

    st = [st_scr[h] for h in range(C_HEADS)]
    o_prev = None
    pad_rows = jnp.zeros((8 - C_HEADS, C_DK), F32)
    for c in range(nc):
        idx = [c * C_HEADS + h for h in range(C_HEADS)]
        glt = jnp.concatenate([gl[n] for n in idx] + [pad_rows], axis=0).T
        o = [o_intra[n] + _dot(qd[n], st[h].astype(BF16)) for h, n in enumerate(idx)]
        st = [st[h] * glt[:, h:h + 1] + _dotg(kd[n], v_[n], TN_DIMS) for h, n in enumerate(idx)]
        if o_prev is not None:
            gate_store(c - 1, o_prev)
        o_prev = o
    gate_store(nc - 1, o_prev)
    for h in range(C_HEADS):
        st_scr[h] = st[h]

    @pl.when(i == nsteps - 1)
    def _():
        sout_ref[0] = st_scr[...]


def _c_layer_prompt(x3, norm_g, w_main, w_gk, w_up, b_gk, onorm, w_out, final_g, st_in, ct, n_lead):
    B, L, _ = x3.shape
    kern = functools.partial(_c_layer_kernel, ct=ct, n_lead=n_lead)
    full = lambda a: pl.BlockSpec(a.shape, lambda b, i: (0,) * a.ndim)
    blk = pl.BlockSpec((1, ct, D_MODEL), lambda b, i: (b, i, 0))
    st = pl.BlockSpec((1, C_HEADS, C_DK, C_DV), lambda b, i: (b, 0, 0, 0))
    return pl.pallas_call(
        kern,
        grid=(B, L // ct),
        in_specs=[blk, full(norm_g), full(w_main), full(w_gk), full(w_up), full(b_gk), full(onorm),
                  full(w_out), full(final_g), full(st_in)],
        out_specs=[blk, st],
        out_shape=[jax.ShapeDtypeStruct((B, L, D_MODEL), F32),
                   jax.ShapeDtypeStruct((B, C_HEADS, C_DK, C_DV), F32)],
        scratch_shapes=[pltpu.VMEM((C_HEADS, C_DK, C_DV), F32), pltpu.VMEM((ct, C_VAL), BF16)],
        compiler_params=_cparams(("arbitrary", "arbitrary")),
        name="c_layer_prompt",
    )(x3, norm_g, w_main, w_gk, w_up, b_gk, onorm, w_out, final_g, st_in)


def _ab_proj_kernel(x_ref, g_ref, w_ref, wg_ref, qa_ref, kv_ref, za_ref, xb_ref, zb_ref, gates_ref):
    hb = _rms(x_ref[...], g_ref[...]).astype(BF16)
    qa_ref[...] = _dot(hb, w_ref[:, O_QA:O_KV])
    kv_ref[...] = _dot(hb, w_ref[:, O_KV:O_ZA])
    za_ref[...] = _dot(hb, w_ref[:, O_ZA:O_XB])
    xb_ref[...] = _dot(hb, w_ref[:, O_XB:O_ZB])
    zb_ref[...] = _dot(hb, w_ref[:, O_ZB:AB_MAIN])
    gates_ref[...] = _dot(hb, wg_ref[...])


def _ab_proj(x2d, norm_g, w_main, w_gate):
    n = x2d.shape[0]
    full = lambda a: pl.BlockSpec(a.shape, lambda: (0,) * a.ndim)
    widths = (A_WIDTH, 2 * A_KV_WIDTH, A_WIDTH, B_CONV_CH, B_VAL, LANES)
    args = (x2d, norm_g, w_main, w_gate)
    return pl.pallas_call(
        _ab_proj_kernel,
        in_specs=[full(a) for a in args],
        out_specs=[pl.BlockSpec((n, w), lambda: (0, 0)) for w in widths],
        out_shape=[jax.ShapeDtypeStruct((n, w), F32) for w in widths],
        compiler_params=pltpu.CompilerParams(vmem_limit_bytes=VMEM_LIMIT),
        name="ab_in_proj_sample",
    )(*args)


def _swa_sample_kernel(q_ref, kvn_ref, za_ref, ck_ref, cv_ref, sink_ref, o_ref, nk_ref, nv_ref, *, bt):
    row = lax.broadcasted_iota(jnp.int32, (WINDOW, A_KV_WIDTH), 0)
    hrow = lax.broadcasted_iota(jnp.int32, (A_HEADS, A_HD), 0)
    sk = sink_ref[...]
    nkb, nvb = [], []
    for b in range(bt):
        nk = jnp.where(row == WINDOW - 1, kvn_ref[b:b + 1, 0:A_KV_WIDTH], pltpu.roll(ck_ref[b], WINDOW - 1, 0))
        nv = jnp.where(row == WINDOW - 1, kvn_ref[b:b + 1, A_KV_WIDTH:], pltpu.roll(cv_ref[b], WINDOW - 1, 0))
        nk_ref[b] = nk
        nv_ref[b] = nv
        nkb.append(nk.astype(BF16))
        nvb.append(nv.astype(BF16))
    items = [(b, j) for b in range(bt) for j in range(A_KV_HEADS)]
    q8 = [q_ref[b].astype(BF16) for b in range(bt)]
    sc = [_dotg(q8[b], nkb[b][:, j * A_HD:(j + 1) * A_HD], NT_DIMS) * (A_HD ** -0.5) for b, j in items]
    m = [jnp.maximum(jnp.max(s, axis=-1, keepdims=True), sk) for s in sc]
    p = [jnp.exp(sc[n] - m[n]) for n in range(len(items))]
    den = [jnp.sum(p[n], axis=-1, keepdims=True) + jnp.exp(sk - m[n]) for n in range(len(items))]
    oj = [_dot(p[n].astype(BF16), nvb[b][:, j * A_HD:(j + 1) * A_HD]) / den[n] for n, (b, j) in enumerate(items)]
    for b in range(bt):
        o8 = jnp.where(hrow >= A_GROUP, oj[b * A_KV_HEADS + 1], oj[b * A_KV_HEADS])
        o_ref[b] = o8 * _silu(za_ref[b])


def _swa_sample(q3, kv_new, za3, cache_k, cache_v, sink_col, bt):
    nb = q3.shape[0]
    hd = pl.BlockSpec((bt, A_HEADS, A_HD), lambda i: (i, 0, 0))
    cache = pl.BlockSpec((bt, WINDOW, A_KV_WIDTH), lambda i: (i, 0, 0))
    return pl.pallas_call(
        functools.partial(_swa_sample_kernel, bt=bt),
        grid=(nb // bt,),
        in_specs=[hd, pl.BlockSpec((bt, 2 * A_KV_WIDTH), lambda i: (i, 0)), hd, cache, cache,
                  pl.BlockSpec(sink_col.shape, lambda i: (0, 0))],
        out_specs=[hd, cache, cache],
        out_shape=[jax.ShapeDtypeStruct(q3.shape, F32),
                   jax.ShapeDtypeStruct(cache_k.shape, F32),
                   jax.ShapeDtypeStruct(cache_v.shape, F32)],
        compiler_params=_cparams(("arbitrary",)),
        name="swa_sample",
    )(q3, kv_new, za3, cache_k, cache_v, sink_col)


def _gdn_sample_prep_kernel(xb_ref, h0_ref, h1_ref, h2_ref, gates_ref, convw_ref, gprm_ref,
                            q_ref, k_ref, v_ref, bg_ref):
    w = convw_ref[...]
    y = (h0_ref[...] * w[0:1, :] + h1_ref[...] * w[1:2, :] + h2_ref[...] * w[2:3, :]
         + xb_ref[...] * w[3:4, :])
    cv = _silu(y)
    for h in range(B_HEADS):
        q = cv[:, h * B_DK:(h + 1) * B_DK]
        k = cv[:, B_KEY + h * B_DK:B_KEY + (h + 1) * B_DK]
        q_ref[:, h * B_DK:(h + 1) * B_DK] = (
            q * lax.rsqrt(jnp.sum(q * q, axis=-1, keepdims=True) + EPS) * (B_DK ** -0.5))
        k_ref[:, h * B_DK:(h + 1) * B_DK] = k * lax.rsqrt(jnp.sum(k * k, axis=-1, keepdims=True) + EPS)
    v_ref[...] = cv[:, 2 * B_KEY:]
    gt = gates_ref[...]
    col = lax.broadcasted_iota(jnp.int32, gt.shape, 1)
    g = -gprm_ref[0:1, :] * _softplus(gt + gprm_ref[1:2, :])
    bg_ref[...] = jnp.where(col < B_HEADS, _sigmoid(gt), jnp.exp(g))


def _gdn_sample_prep(xb, h0, h1, h2, gates, conv_w8, gprm):
    n = xb.shape[0]
    full = lambda a: pl.BlockSpec(a.shape, lambda: (0,) * a.ndim)
    args = (xb, h0, h1, h2, gates, conv_w8, gprm)
    return pl.pallas_call(
        _gdn_sample_prep_kernel,
        in_specs=[full(a) for a in args],
        out_specs=[pl.BlockSpec((n, B_KEY), lambda: (0, 0))] * 3 + [pl.BlockSpec((n, LANES), lambda: (0, 0))],
        out_shape=[jax.ShapeDtypeStruct((n, B_KEY), F32)] * 3 + [jax.ShapeDtypeStruct((n, LANES), F32)],
        compiler_params=pltpu.CompilerParams(vmem_limit_bytes=VMEM_LIMIT),
        name="gdn_sample_prep",
    )(*args)


def _gdn_sample_kernel(s_ref, qt_ref, kt_ref, v_ref, bg_ref, zb_ref, onorm_ref, o_ref, sout_ref, *, bt):
    items = [(b, h) for b in range(bt) for h in range(B_HEADS)]
    hs = lambda h: slice(h * B_DV, (h + 1) * B_DV)
    kt = [kt_ref[:, hs(h)].T for h in range(B_HEADS)]
    qt = [qt_ref[:, hs(h)].T for h in range(B_HEADS)]
    kc = [kt[h][:, b:b + 1] for b, h in items]
    eg = [bg_ref[b:b + 1, B_HEADS + h:B_HEADS + h + 1] for b, h in items]
    ks = [jnp.sum(kc[n] * s_ref[b, h], axis=0, keepdims=True) for n, (b, h) in enumerate(items)]
    v_new = [bg_ref[b:b + 1, h:h + 1] * (v_ref[b:b + 1, hs(h)] - eg[n] * ks[n]) for n, (b, h) in enumerate(items)]
    o = []
    for n, (b, h) in enumerate(items):
        s_new = eg[n] * s_ref[b, h] + kc[n] * v_new[n]
        sout_ref[b, h] = s_new
        o.append(jnp.sum(qt[h][:, b:b + 1] * s_new, axis=0, keepdims=True))
    o_blk = jnp.concatenate([jnp.concatenate([o[b * B_HEADS + h] for h in range(B_HEADS)], axis=-1)
                             for b in range(bt)], axis=0)
    for h in range(B_HEADS):
        o_ref[:, hs(h)] = _rms(o_blk[:, hs(h)], onorm_ref[...]) * _silu(zb_ref[:, hs(h)])


def _gdn_sample(state, qt, kt, v, bg, zb, onorm, bt):
    nb = state.shape[0]
    st = pl.BlockSpec((bt, B_HEADS, B_DK, B_DV), lambda i: (i, 0, 0, 0))
    col = pl.BlockSpec((1, B_HEADS, B_DK, bt), lambda i: (i, 0, 0, 0))
    row = lambda n: pl.BlockSpec((bt, n), lambda i: (i, 0))
    return pl.pallas_call(
        functools.partial(_gdn_sample_kernel, bt=bt),
        grid=(nb // bt,),
        in_specs=[st, row(B_KEY), row(B_KEY), row(B_VAL), row(LANES), row(B_VAL),
                  pl.BlockSpec(onorm.shape, lambda i: (0, 0))],
        out_specs=[row(B_VAL), st],
        out_shape=[jax.ShapeDtypeStruct((nb, B_VAL), F32), jax.ShapeDtypeStruct(state.shape, F32)],
        compiler_params=_cparams(("arbitrary",)),
        name="gdn_sample",
    )(state, qt, kt, v, bg, zb, onorm)


def _ab_out_c_proj_kernel(x_ref, oa_ref, ob_ref, wout_ref, g_ref, w_ref, wgk_ref, wup_ref, bgk_ref,
                          y_ref, q_ref, k_ref, v_ref, z_ref, la_ref):
    y = (x_ref[...] + _dot(oa_ref[...].astype(BF16), wout_ref[0:A_WIDTH, :])
         + _dot(ob_ref[...].astype(BF16), wout_ref[A_WIDTH:, :]))
    y_ref[...] = y
    hb = _rms(y, g_ref[...]).astype(BF16)
    q_ref[...] = _dot(hb, w_ref[:, 0:C_KEY])
    k_ref[...] = _dot(hb, w_ref[:, C_KEY:2 * C_KEY])
    v_ref[...] = _dot(hb, w_ref[:, 2 * C_KEY:2 * C_KEY + C_VAL])
    z_ref[...] = _dot(hb, w_ref[:, 2 * C_KEY + C_VAL:C_MAIN])
    la_ref[...] = _gla_log_decay(_dot(hb, wgk_ref[...]), wup_ref, bgk_ref)


def _ab_out_c_proj(x2d, oa, ob, w_out, norm_g, w_main, w_gk, w_up, b_gk):
    n = x2d.shape[0]
    full = lambda a: pl.BlockSpec(a.shape, lambda: (0,) * a.ndim)
    widths = (D_MODEL, C_KEY, C_KEY, C_VAL, C_VAL, C_KEY)
    args = (x2d, oa, ob, w_out, norm_g, w_main, w_gk, w_up, b_gk)
    return pl.pallas_call(
        _ab_out_c_proj_kernel,
        in_specs=[full(a) for a in args],
        out_specs=[pl.BlockSpec((n, w), lambda: (0, 0)) for w in widths],
        out_shape=[jax.ShapeDtypeStruct((n, w), F32) for w in widths],
        compiler_params=pltpu.CompilerParams(vmem_limit_bytes=VMEM_LIMIT),
        name="ab_out_c_in_proj_sample",
    )(*args)


def _gla_sample_kernel(s_ref, qt_ref, kt_ref, lat_ref, v_ref, z_ref, onorm_ref, o_ref, sout_ref, *, bt):
    items = [(b, h) for b in range(bt) for h in range(C_HEADS)]
    hs = lambda h: slice(h * C_DV, (h + 1) * C_DV)
    ks = lambda h: slice(h * C_DK, (h + 1) * C_DK)
    at = [jnp.exp(lat_ref[:, ks(h)]).T for h in range(C_HEADS)]
    kt = [kt_ref[:, ks(h)].T for h in range(C_HEADS)]
    qt = [(qt_ref[:, ks(h)] * (C_DK ** -0.5)).T for h in range(C_HEADS)]
    o = []
    for n, (b, h) in enumerate(items):
        s_new = at[h][:, b:b + 1] * s_ref[b, h] + kt[h][:, b:b + 1] * v_ref[b:b + 1, hs(h)]
        sout_ref[b, h] = s_new
        o.append(jnp.sum(qt[h][:, b:b + 1] * s_new, axis=0, keepdims=True))
    o_blk = jnp.concatenate([jnp.concatenate([o[b * C_HEADS + h] for h in range(C_HEADS)], axis=-1)
                             for b in range(bt)], axis=0)
    for h in range(C_HEADS):
        o_ref[:, hs(h)] = _rms(o_blk[:, hs(h)], onorm_ref[...]) * _silu(z_ref[:, hs(h)])


def _gla_sample(state, qt, kt, lat, v, z, onorm, bt):
    nb = state.shape[0]
    st = pl.BlockSpec((bt, C_HEADS, C_DK, C_DV), lambda i: (i, 0, 0, 0))
    col = pl.BlockSpec((1, C_HEADS, C_DK, bt), lambda i: (i, 0, 0, 0))
    row = lambda n: pl.BlockSpec((bt, n), lambda i: (i, 0))
    return pl.pallas_call(
        functools.partial(_gla_sample_kernel, bt=bt),
        grid=(nb // bt,),
        in_specs=[st, row(C_KEY), row(C_KEY), row(C_KEY), row(C_VAL), row(C_VAL),
                  pl.BlockSpec(onorm.shape, lambda i: (0, 0))],
        out_specs=[row(C_VAL), st],
        out_shape=[jax.ShapeDtypeStruct((nb, C_VAL), F32), jax.ShapeDtypeStruct(state.shape, F32)],
        compiler_params=_cparams(("arbitrary",)),
        name="gla_sample",
    )(state, qt, kt, lat, v, z, onorm)


def _c_out_kernel(x_ref, o_ref, w_ref, g_ref, y_ref):
    y_ref[...] = _rms(x_ref[...] + _dot(o_ref[...].astype(BF16), w_ref[...]), g_ref[...])


def _c_out(x2d, oc, w_out, final_g):
    n = x2d.shape[0]
    full = lambda a: pl.BlockSpec(a.shape, lambda: (0,) * a.ndim)
    args = (x2d, oc, w_out, final_g)
    return pl.pallas_call(
        _c_out_kernel,
        in_specs=[full(a) for a in args],
        out_specs=pl.BlockSpec((n, D_MODEL), lambda: (0, 0)),
        out_shape=jax.ShapeDtypeStruct((n, D_MODEL), F32),
        compiler_params=pltpu.CompilerParams(vmem_limit_bytes=VMEM_LIMIT),
        name="c_out_proj_norm_sample",
    )(*args)


def _to_cols(a, heads, dk, bt):
    n = a.shape[0]
    return a.reshape(n // bt, bt, heads, dk).transpose(0, 2, 3, 1)


def kernel(x_prompt, x_sample, cache_swa_k, cache_swa_v, state_dn_conv, state_dn, state_gla,
           meta_tokens, norm_ab, w_in_ab, sink_a, conv_b, a_log_b, dt_bias_b, onorm_b, w_out_ab,
           norm_c, w_in_c, w_gk_up, b_gk, onorm_c, w_out_c, final_norm):
    Bp, L, _ = x_prompt.shape
    Bs = x_sample.shape[0]
    CT, BT = 256, 8

    w_ab = w_in_ab[0]
    w_ab_main = w_ab.astype(BF16)
    w_ab_gate = jnp.pad(w_ab[:, AB_MAIN:], ((0, 0), (0, LANES - 2 * B_HEADS))).astype(BF16)
    w_ab_gate_t = jnp.pad(w_ab[:, AB_MAIN:].T, ((0, 16 - 2 * B_HEADS), (0, 0))).astype(BF16)
    g_ab = norm_ab[0][None, :]
    sink_row = sink_a[0][None, :]
    sink_col = sink_a[0][:, None]
    conv_w8 = jnp.pad(conv_b[0], ((0, 8 - CONV_W), (0, 0)))
    gp = jnp.stack([jnp.exp(a_log_b[0]), dt_bias_b[0]])
    gprm = jnp.pad(gp, ((0, 6), (B_HEADS, LANES - 2 * B_HEADS)))
    gprm_t = jnp.pad(gp.T, ((B_HEADS, 16 - 2 * B_HEADS), (0, LANES - 2)))
    onb = onorm_b[0][None, :]
    w_ab_out = w_out_ab[0].astype(BF16)
    w_c = w_in_c[0]
    w_c_main = w_c.astype(BF16)
    w_c_gk = jnp.pad(w_c[:, C_MAIN:], ((0, 0), (0, LANES - C_RANK))).astype(BF16)
    w_up = jnp.pad(w_gk_up[0], ((0, LANES - C_RANK), (0, 0)))
    bgk = b_gk[0][None, :]
    g_c = norm_c[0][None, :]
    onc = onorm_c[0][None, :]
    w_c_out = w_out_c[0].astype(BF16)
    g_fin = final_norm[None, :]

    def ab_layer(x3, ct, hist_kv, pos_shift, conv_in, s_in, n_lead):
        return _ab_layer_prompt(x3, g_ab, w_ab_main, w_ab_gate, w_ab_gate_t, sink_row, hist_kv, conv_w8,
                                gprm, gprm_t, onb, w_ab_out, conv_in, s_in, ct, n_lead, pos_shift)

    def c_layer(x3, ct, st_in, n_lead):
        return _c_layer_prompt(x3, g_c, w_c_main, w_c_gk, w_up, bgk, onc, w_c_out, g_fin, st_in, ct, n_lead)

    xpre = jnp.concatenate([jnp.zeros((LEAD, D_MODEL), F32), meta_tokens], axis=0)[None]
    y_pre, kv_pre, xbtail_pre, sdn_pre = ab_layer(
        xpre, CHUNK, jnp.zeros((WINDOW, 2 * A_KV_WIDTH), F32), -LEAD,
        jnp.zeros((1, 8, B_CONV_CH), F32), jnp.zeros((1, B_HEADS, B_DK, B_DV), F32), LEAD)
    _, sgla_pre = c_layer(y_pre, CHUNK, jnp.zeros((1, C_HEADS, C_DK, C_DV), F32), LEAD)

    hist_kv = jnp.concatenate([jnp.zeros((WINDOW - CHUNK, 2 * A_KV_WIDTH), F32), kv_pre[0]], axis=0)
    y1, kv_tail, xb_tail, sdn_p = ab_layer(
        x_prompt, 2 * CT, hist_kv, N_META, xbtail_pre, sdn_pre, 0)
    y_prompt, sgla_p = c_layer(y1, 2 * CT, sgla_pre, 0)

    swa_k_p = kv_tail[:, :, :A_KV_WIDTH].reshape(1, Bp, WINDOW, A_KV_HEADS, A_HD)
    swa_v_p = kv_tail[:, :, A_KV_WIDTH:].reshape(1, Bp, WINDOW, A_KV_HEADS, A_HD)
    dn_conv_p = xb_tail[:, 8 - (CONV_W - 1):, :][None]

    xs = x_sample.reshape(Bs, D_MODEL)
    qa, kvn, za, xb, zb, gates = _ab_proj(xs, g_ab, w_ab_main, w_ab_gate)
    oa3, nk, nv = _swa_sample(qa.reshape(Bs, A_HEADS, A_HD), kvn, za.reshape(Bs, A_HEADS, A_HD),
                              cache_swa_k[0].reshape(Bs, WINDOW, A_KV_WIDTH),
                              cache_swa_v[0].reshape(Bs, WINDOW, A_KV_WIDTH), sink_col, BT)
    hist = state_dn_conv[0]
    qn, kn, vn, bg = _gdn_sample_prep(xb, hist[:, 0], hist[:, 1], hist[:, 2], gates, conv_w8, gprm)
    ob, sdn_s = _gdn_sample(state_dn[0], qn, kn,
                            vn, bg, zb, onb, BT)
    ys, qc, kc, vc, zc, la = _ab_out_c_proj(xs, oa3.reshape(Bs, A_WIDTH), ob, w_ab_out,
                                            g_c, w_c_main, w_c_gk, w_up, bgk)
    ocs, sgla_s = _gla_sample(state_gla[0], qc, kc, la, vc, zc, onc, BT)
    y_sample = _c_out(ys, ocs, w_c_out, g_fin).reshape(Bs, 1, D_MODEL)
    dn_conv_s = jnp.concatenate([hist[:, 1:], xb[:, None, :]], axis=1)[None]

    return (y_prompt, y_sample, swa_k_p, swa_v_p, dn_conv_p, sdn_p[None], sgla_p[None],
            nk.reshape(1, Bs, WINDOW, A_KV_HEADS, A_HD), nv.reshape(1, Bs, WINDOW, A_KV_HEADS, A_HD),
            dn_conv_s, sdn_s[None], sgla_s[None])
```

```python
import functools

import jax
import jax.numpy as jnp
from jax import lax
from jax.experimental import pallas as pl
from jax.experimental.pallas import tpu as pltpu

F32 = jnp.float32
BF16 = jnp.bfloat16
EPS = 1e-6

D_MODEL = 1024
N_META = 16
CHUNK = 64
LEAD = (-N_META) % CHUNK
A_HEADS, A_KV_HEADS, A_HD = 8, 2, 64
A_GROUP = A_HEADS // A_KV_HEADS
A_WIDTH = A_HEADS * A_HD
A_KV_WIDTH = A_KV_HEADS * A_HD
WINDOW = 128
B_HEADS, B_DK, B_DV = 4, 128, 128
B_KEY = B_HEADS * B_DK
B_VAL = B_HEADS * B_DV
CONV_W = 4
B_CONV_CH = 2 * B_KEY + B_VAL
C_HEADS, C_DK, C_DV = 4, 128, 256
C_KEY = C_HEADS * C_DK
C_VAL = C_HEADS * C_DV
C_RANK = 16
C_GATE_NORM = 16.0
AB_MAIN = 2 * A_WIDTH + 2 * A_KV_WIDTH + B_CONV_CH + B_VAL
AB_MIX = A_WIDTH + B_VAL
C_MAIN = 2 * C_KEY + 2 * C_VAL
O_QA, O_KV, O_ZA, O_XB, O_ZB = 0, 512, 768, 1280, 2816
LANES = 128
NEG = -1e30

VMEM_LIMIT = 56 * 1024 * 1024

NT_DIMS = (((1,), (1,)), ((), ()))
TN_DIMS = (((0,), (0,)), ((), ()))


def _cparams(sem, **kw):
    return pltpu.CompilerParams(dimension_semantics=sem, vmem_limit_bytes=VMEM_LIMIT, **kw)


def _dot(a, b):
    return jnp.dot(a, b, preferred_element_type=F32)


def _dotg(a, b, dims):
    return lax.dot_general(a, b, dims, preferred_element_type=F32)


def _split3(a):
    hi = a.astype(BF16)
    r = a - hi.astype(F32)
    mid = r.astype(BF16)
    lo = (r - mid.astype(F32)).astype(BF16)
    return hi, mid, lo


def _split2(a):
    hi = a.astype(BF16)
    return hi, (a - hi.astype(F32)).astype(BF16)


def _sigmoid(x):
    return 1.0 / (1.0 + jnp.exp(-x))


def _silu(x):
    return x * _sigmoid(x)


def _softplus(x):
    return jnp.maximum(x, 0.0) + jnp.log1p(jnp.exp(-jnp.abs(x)))


def _log_sigmoid(x):
    return jnp.minimum(x, 0.0) - jnp.log1p(jnp.exp(-jnp.abs(x)))


def _rms(x, g):
    return x * lax.rsqrt(jnp.mean(x * x, axis=-1, keepdims=True) + EPS) * g


def _gla_log_decay(low, wup_ref, bgk_ref):
    lh, lm, _ = _split3(low)
    wh, wm, _ = _split3(wup_ref[...])
    up = _dot(lh, wh) + _dot(lh, wm) + _dot(lm, wh)
    return _log_sigmoid(up + bgk_ref[...]) * (1.0 / C_GATE_NORM)


def _swa_tile(i, qa, kvb, za, sink_ref, mix_scr, *, ct, sb, pos_shift):
    r = lax.broadcasted_iota(jnp.int32, (sb, WINDOW + sb), 0)
    c = lax.broadcasted_iota(jnp.int32, (sb, WINDOW + sb), 1)
    diff = r - c + WINDOW
    in_window = (diff >= 0) & (diff < WINDOW)
    nsb = ct // sb
    items = [(s, j) for s in range(nsb) for j in range(A_KV_HEADS)]
    masks = [jnp.concatenate([in_window & (i * ct + s * sb + c - WINDOW + pos_shift >= 0)] * A_GROUP, axis=0)
             for s in range(nsb)]
    sinks = [jnp.concatenate([jnp.broadcast_to(sink_ref[0:1, j * A_GROUP + g:j * A_GROUP + g + 1], (sb, 1))
                              for g in range(A_GROUP)], axis=0) for j in range(A_KV_HEADS)]
    qs = [jnp.concatenate([qa[s * sb:(s + 1) * sb, (j * A_GROUP + g) * A_HD:(j * A_GROUP + g + 1) * A_HD]
                           for g in range(A_GROUP)], axis=0).astype(BF16) for s, j in items]
    sc = [_dotg(qs[n], kvb[s * sb:s * sb + WINDOW + sb, j * A_HD:(j + 1) * A_HD], NT_DIMS) * (A_HD ** -0.5)
          for n, (s, j) in enumerate(items)]
    sc = [jnp.where(masks[s], sc[n], NEG) for n, (s, j) in enumerate(items)]
    mx = [jnp.maximum(jnp.max(sc[n], axis=-1, keepdims=True), sinks[j]) for n, (s, j) in enumerate(items)]
    p = [jnp.exp(sc[n] - mx[n]) for n in range(len(items))]
    den = [jnp.sum(p[n], axis=-1, keepdims=True) + jnp.exp(sinks[j] - mx[n]) for n, (s, j) in enumerate(items)]
    pv = [_dot(p[n].astype(BF16),
               kvb[s * sb:s * sb + WINDOW + sb, A_KV_WIDTH + j * A_HD:A_KV_WIDTH + (j + 1) * A_HD]) / den[n]
          for n, (s, j) in enumerate(items)]
    for s in range(nsb):
        o = jnp.concatenate([pv[s * A_KV_HEADS + j][g * sb:(g + 1) * sb]
                             for j in range(A_KV_HEADS) for g in range(A_GROUP)], axis=-1)
        mix_scr[s * sb:(s + 1) * sb, 0:A_WIDTH] = (o * _silu(za[s * sb:(s + 1) * sb])).astype(BF16)


def _gdn_tile(i, cv, gt, gtt, zb, gprm_ref, gprmt_ref, onorm_ref, s_scr, mix_scr, *, ct, n_lead):
    beta = _sigmoid(gt)
    rows = i * ct + lax.broadcasted_iota(jnp.int32, gt.shape, 0)
    gcol = jnp.where(rows >= n_lead, -gprm_ref[0:1, :] * _softplus(gt + gprm_ref[1:2, :]), 0.0)
    lanes = i * ct + lax.broadcasted_iota(jnp.int32, gtt.shape, 1)
    grow = jnp.where(lanes >= n_lead, -gprmt_ref[:, 0:1] * _softplus(gtt + gprmt_ref[:, 1:2]), 0.0)

    ii = lax.broadcasted_iota(jnp.int32, (CHUNK, CHUNK), 0)
    jj = lax.broadcasted_iota(jnp.int32, (CHUNK, CHUNK), 1)
    lower = ii >= jj
    strict = ii > jj
    ltri = lower.astype(BF16)
    utri = (ii <= jj).astype(BF16)

    nc = ct // CHUNK
    items = [(c, h) for c in range(nc) for h in range(B_HEADS)]
    n_items = len(items)
    rs = lambda c: slice(c * CHUNK, (c + 1) * CHUNK)
    qn, kn = [], []
    for h in range(B_HEADS):
        q = cv[:, h * B_DK:(h + 1) * B_DK]
        k = cv[:, B_KEY + h * B_DK:B_KEY + (h + 1) * B_DK]
        qn.append(q * lax.rsqrt(jnp.sum(q * q, axis=-1, keepdims=True) + EPS) * (B_DK ** -0.5))
        kn.append(k * lax.rsqrt(jnp.sum(k * k, axis=-1, keepdims=True) + EPS))
    gcum_c = [sum(_dot(ltri, p) for p in _split3(gcol[rs(c)])) for c in range(nc)]
    gcum_r = [sum(_dot(p, utri) for p in _split3(grow[:, rs(c)])) for c in range(nc)]
    q_ = [qn[h][rs(c)] for c, h in items]
    k_ = [kn[h][rs(c)] for c, h in items]
    v_ = [cv[rs(c), 2 * B_KEY + h * B_DV:2 * B_KEY + (h + 1) * B_DV] for c, h in items]
    bh = [beta[rs(c), h:h + 1] for c, h in items]
    gc = [gcum_c[c][:, B_HEADS + h:B_HEADS + h + 1] for c, h in items]
    gr = [gcum_r[c][B_HEADS + h:B_HEADS + h + 1, :] for c, h in items]
    decay = [jnp.exp(jnp.where(lower, gc[n] - gr[n], NEG)) for n in range(n_items)]
    kb = [k_[n] * bh[n] for n in range(n_items)]
    kbf = [k_[n].astype(BF16) for n in range(n_items)]
    kq = [_dotg(jnp.concatenate([kb[n], q_[n]], axis=0).astype(BF16), kbf[n], NT_DIMS) for n in range(n_items)]
    a = [jnp.where(strict, kq[n][:CHUNK] * decay[n], 0.0) for n in range(n_items)]
    qk = [jnp.where(lower, kq[n][CHUNK:] * decay[n], 0.0).astype(BF16) for n in range(n_items)]
    doff = [-jnp.where((ii >> 1) == (jj >> 1), a[n], 0.0) for n in range(n_items)]
    for lg in range(1, 6):
        m = ((ii >> (lg + 1)) == (jj >> (lg + 1))) & ((ii >> lg) != (jj >> lg))
        bm = [jnp.where(m, a[n], 0.0) for n in range(n_items)]
        db = [doff[n].astype(BF16) for n in range(n_items)]
        y = [bm[n] + _dot(db[n], bm[n].astype(BF16)) for n in range(n_items)]
        x = [y[n] + _dot(y[n].astype(BF16), db[n]) for n in range(n_items)]
        doff = [doff[n] - x[n] for n in range(n_items)]
    egc = [jnp.exp(gc[n]) for n in range(n_items)]
    rhs = [jnp.concatenate([v_[n] * bh[n], kb[n] * egc[n]], axis=-1) for n in range(n_items)]
    sol = [rhs[n] + _dot(doff[n].astype(BF16), rhs[n].astype(BF16)) for n in range(n_items)]
    g_last = [gc[n][CHUNK - 1:CHUNK, :] for n in range(n_items)]
    kd = [(k_[n] * jnp.exp(g_last[n] - gc[n])).astype(BF16) for n in range(n_items)]
    wq = [jnp.concatenate([sol[n][:, B_DV:], q_[n] * egc[n]], axis=0).astype(BF16) for n in range(n_items)]
    eg_last = [jnp.exp(g_last[n]) for n in range(n_items)]

    s_cur = [s_scr[h] for h in range(B_HEADS)]
    for c in range(nc):
        idx = [c * B_HEADS + h for h in range(B_HEADS)]
        ws = [_dot(wq[n], s_cur[h].astype(BF16)) for h, n in enumerate(idx)]
        v_new = [sol[n][:, :B_DV] - ws[h][:CHUNK] for h, n in enumerate(idx)]
        vb = [v_new[h].astype(BF16) for h in range(B_HEADS)]
        o = [ws[h][CHUNK:] + _dot(qk[n], vb[h]) for h, n in enumerate(idx)]
        s_cur = [s_cur[h] * eg_last[n] + _dotg(kd[n], vb[h], TN_DIMS) for h, n in enumerate(idx)]
        for h in range(B_HEADS):
            z = zb[rs(c), h * B_DV:(h + 1) * B_DV]
            mix_scr[rs(c), A_WIDTH + h * B_DV:A_WIDTH + (h + 1) * B_DV] = (
                _rms(o[h], onorm_ref[...]) * _silu(z)).astype(BF16)
    for h in range(B_HEADS):
        s_scr[h] = s_cur[h]


def _ab_layer_kernel(x_ref, g_ref, w_ref, wg_ref, wgt_ref, sink_ref, hist_ref, convw_ref, gprm_ref,
                     gprmt_ref, onorm_ref, wout_ref, convin_ref, sin_ref,
                     y_ref, kvtail_ref, xbtail_ref, sout_ref,
                     s_scr, xc_scr, kvprev_scr, mix_scr, *, ct, n_lead, pos_shift):
    i = pl.program_id(1)
    nsteps = pl.num_programs(1)
    ntail = min(WINDOW, ct)

    @pl.when(i == 0)
    def _():
        s_scr[...] = sin_ref[0]
        xc_scr[0:8, :] = convin_ref[0]
        kvprev_scr[1] = hist_ref[...]

    x = x_ref[0]
    hb = _rms(x, g_ref[...]).astype(BF16)
    qa = _dot(hb, w_ref[:, O_QA:O_KV])
    kv = _dot(hb, w_ref[:, O_KV:O_ZA])
    za = _dot(hb, w_ref[:, O_ZA:O_XB])
    xb = _dot(hb, w_ref[:, O_XB:O_ZB])
    zb = _dot(hb, w_ref[:, O_ZB:AB_MAIN])
    gt = _dot(hb, wg_ref[...])
    gtt = _dotg(wgt_ref[...], hb, NT_DIMS)
    kvtail_ref[0] = kv[ct - ntail:]
    xbtail_ref[0] = xb[ct - 8:]

    slot = lax.rem(i, 2)
    kvb = jnp.concatenate([kvprev_scr[1 - slot], kv], axis=0).astype(BF16)
    if ct >= WINDOW:
        kvprev_scr[slot] = kv[ct - WINDOW:]
    _swa_tile(i, qa, kvb, za, sink_ref, mix_scr, ct=ct, sb=ntail, pos_shift=pos_shift)

    xc_scr[8:8 + ct, :] = xb
    w = convw_ref[...]
    yc = xc_scr[pl.ds(8 - (CONV_W - 1), ct), :] * w[0:1, :]
    for t in range(1, CONV_W):
        yc = yc + xc_scr[pl.ds(8 - (CONV_W - 1) + t, ct), :] * w[t:t + 1, :]
    cv = _silu(yc)
    xc_scr[0:8, :] = xc_scr[ct:ct + 8, :]
    _gdn_tile(i, cv, gt, gtt, zb, gprm_ref, gprmt_ref, onorm_ref, s_scr, mix_scr, ct=ct, n_lead=n_lead)

    y_ref[0] = x + _dot(mix_scr[...], wout_ref[...])

    @pl.when(i == nsteps - 1)
    def _():
        sout_ref[0] = s_scr[...]


def _swa_stages(i, qa, kvb, env, sink_ref, mix_scr, *, ct, sb, pos_shift):
    r = lax.broadcasted_iota(jnp.int32, (sb, WINDOW + sb), 0)
    c = lax.broadcasted_iota(jnp.int32, (sb, WINDOW + sb), 1)
    diff = r - c + WINDOW
    in_window = (diff >= 0) & (diff < WINDOW)
    nsb = ct // sb
    items = [(s, j) for s in range(nsb) for j in range(A_KV_HEADS)]
    masks = [jnp.concatenate([in_window & (i * ct + s * sb + c - WINDOW + pos_shift >= 0)] * A_GROUP, axis=0)
             for s in range(nsb)]
    sinks = [jnp.concatenate([jnp.broadcast_to(sink_ref[0:1, j * A_GROUP + g:j * A_GROUP + g + 1], (sb, 1))
                              for g in range(A_GROUP)], axis=0) for j in range(A_KV_HEADS)]
    sc = []
    for s, j in items:
        qs = (jnp.concatenate([qa[s * sb:(s + 1) * sb, (j * A_GROUP + g) * A_HD:(j * A_GROUP + g + 1) * A_HD]
                               for g in range(A_GROUP)], axis=0) * (A_HD ** -0.5)).astype(BF16)
        sc.append(_dotg(qs, kvb[s * sb:s * sb + WINDOW + sb, j * A_HD:(j + 1) * A_HD], NT_DIMS))
        yield
    p, esk = [], []
    for n, (s, j) in enumerate(items):
        scm = jnp.where(masks[s], sc[n], NEG)
        mx = jnp.maximum(jnp.max(scm, axis=-1, keepdims=True), sinks[j])
        p.append(jnp.exp(scm - mx).astype(BF16))
        esk.append(jnp.exp(sinks[j] - mx))
        yield
    pv = []
    ones = jnp.ones((WINDOW + sb, A_HD), BF16)
    for n, (s, j) in enumerate(items):
        vx = jnp.concatenate([kvb[s * sb:s * sb + WINDOW + sb,
                                  A_KV_WIDTH + j * A_HD:A_KV_WIDTH + (j + 1) * A_HD], ones], axis=-1)
        pvx = _dot(p[n], vx)
        pv.append(pvx[:, :A_HD] / (pvx[:, A_HD:] + esk[n]))
        yield
    env['pv'] = pv


def _swa_store(env, mix_scr, *, ct, sb):
    pv = env['pv']
    for s in range(ct // sb):
        o = jnp.concatenate([pv[s * A_KV_HEADS + j][g * sb:(g + 1) * sb]
                             for j in range(A_KV_HEADS) for g in range(A_GROUP)], axis=-1)
        mix_scr[s * sb:(s + 1) * sb, 0:A_WIDTH] = (o * _silu(env['za'][s * sb:(s + 1) * sb])).astype(BF16)


def _gdn_stages(i, cvb, gt, gtt, env, gprm_ref, gprmt_ref, onorm_ref, s_scr, mix_scr, *, ct, n_lead):
    nc = ct // CHUNK
    items = [(c, h) for c in range(nc) for h in range(B_HEADS)]
    n_items = len(items)
    rs = lambda c: slice(c * CHUNK, (c + 1) * CHUNK)
    qn, kn = [], []
    for h in range(B_HEADS):
        q = cvb[h]
        k = cvb[B_HEADS + h]
        qn.append(q * lax.rsqrt(jnp.sum(q * q, axis=-1, keepdims=True) + EPS) * (B_DK ** -0.5))
        kn.append(k * lax.rsqrt(jnp.sum(k * k, axis=-1, keepdims=True) + EPS))
        yield
    beta = _sigmoid(gt)
    rows = i * ct + lax.broadcasted_iota(jnp.int32, gt.shape, 0)
    gcol = jnp.where(rows >= n_lead, -gprm_ref[0:1, :] * _softplus(gt + gprm_ref[1:2, :]), 0.0)
    lanes = i * ct + lax.broadcasted_iota(jnp.int32, gtt.shape, 1)
    grow = jnp.where(lanes >= n_lead, -gprmt_ref[:, 0:1] * _softplus(gtt + gprmt_ref[:, 1:2]), 0.0)
    ii = lax.broadcasted_iota(jnp.int32, (CHUNK, CHUNK), 0)
    jj = lax.broadcasted_iota(jnp.int32, (CHUNK, CHUNK), 1)
    lower = ii >= jj
    strict = ii > jj
    ltri = lower.astype(BF16)
    utri = (ii <= jj).astype(BF16)
    gcum_c = [sum(_dot(ltri, p) for p in _split3(gcol[rs(c)])) for c in range(nc)]
    gcum_r = [sum(_dot(p, utri) for p in _split3(grow[:, rs(c)])) for c in range(nc)]
    yield
    q_ = [qn[h][rs(c)] for c, h in items]
    k_ = [kn[h][rs(c)] for c, h in items]
    v_ = [cvb[2 * B_HEADS + h][rs(c)] for c, h in items]
    bh = [beta[rs(c), h:h + 1] for c, h in items]
    gc = [gcum_c[c][:, B_HEADS + h:B_HEADS + h + 1] for c, h in items]
    gr = [gcum_r[c][B_HEADS + h:B_HEADS + h + 1, :] for c, h in items]
    pairs = [(c, p) for c in range(nc) for p in range(B_HEADS // 2)]
    n_pairs = len(pairs)
    first = lambda c, p: c * B_HEADS + 2 * p
    row = lax.broadcasted_iota(jnp.int32, (CHUNK, 2 * CHUNK), 0)
    lane = lax.broadcasted_iota(jnp.int32, (CHUNK, 2 * CHUNK), 1)
    col = lane & (CHUNK - 1)
    left = lane < CHUNK
    lower2 = row >= col
    strict2 = row > col
    zk = jnp.zeros((CHUNK, B_DK), BF16)

    def bdiag(t):
        return jnp.concatenate([jnp.where(left, t, 0.0), jnp.where(left, 0.0, t)], axis=0)

    def bstack(t0, t1, z):
        return jnp.concatenate([jnp.concatenate([t0, z], axis=1), jnp.concatenate([z, t1], axis=1)], axis=0)

    kb = [k_[n] * bh[n] for n in range(n_items)]
    a, qk = [], []
    for c, p in pairs:
        n0 = first(c, p)
        n1 = n0 + 1
        decay = jnp.exp(jnp.where(lower2, jnp.where(left, gc[n0], gc[n1])
                                  - jnp.concatenate([gr[n0], gr[n1]], axis=1), NEG))
        lhs = jnp.concatenate([jnp.concatenate([kb[n0], kb[n1]], axis=1),
                               jnp.concatenate([q_[n0], q_[n1]], axis=1)], axis=0).astype(BF16)
        kq = _dotg(lhs, bstack(k_[n0].astype(BF16), k_[n1].astype(BF16), zk), NT_DIMS)
        a.append(jnp.where(strict2, kq[:CHUNK] * decay, 0.0))
        qk.append(jnp.where(lower2, kq[CHUNK:] * decay, 0.0).astype(BF16))
        if p == B_HEADS // 2 - 1:
            yield
    doff = [-jnp.where((row >> 1) == (col >> 1), a[n], 0.0) for n in range(n_pairs)]
    for lg in range(1, 6):
        m = ((row >> (lg + 1)) == (col >> (lg + 1))) & ((row >> lg) != (col >> lg))
        bm = [jnp.where(m, a[n], 0.0) for n in range(n_pairs)]
        y = [bm[n] + _dot(doff[n].astype(BF16), bdiag(bm[n]).astype(BF16)) for n in range(n_pairs)]
        yield
        x = [y[n] + _dot(y[n].astype(BF16), bdiag(doff[n]).astype(BF16)) for n in range(n_pairs)]
        doff = [doff[n] - x[n] for n in range(n_pairs)]
        yield
    egc = [jnp.exp(gc[n]) for n in range(n_items)]
    g_last = [gc[n][CHUNK - 1:CHUNK, :] for n in range(n_items)]
    eg_last = [jnp.exp(g_last[n]) for n in range(n_items)]
    sol, kd, wq = [], [], []
    zr = jnp.zeros((CHUNK, 2 * B_DV), BF16)
    for m_, (c, p) in enumerate(pairs):
        n0 = first(c, p)
        rhs = [jnp.concatenate([v_[n] * bh[n], kb[n] * egc[n]], axis=-1) for n in (n0, n0 + 1)]
        t = _dot(doff[m_].astype(BF16), bstack(rhs[0].astype(BF16), rhs[1].astype(BF16), zr))
        for e, n in enumerate((n0, n0 + 1)):
            sol.append(rhs[e] + t[:, e * 2 * B_DV:(e + 1) * 2 * B_DV])
            kd.append((k_[n] * jnp.exp(g_last[n] - gc[n])).astype(BF16))
            wq.append(jnp.concatenate([sol[n][:, B_DV:], q_[n] * egc[n]], axis=0).astype(BF16))
        if p == B_HEADS // 2 - 1:
            yield

    def gate_store(c, o):
        for h in range(B_HEADS):
            z = env['zb'][rs(c), h * B_DV:(h + 1) * B_DV]
            mix_scr[rs(c), A_WIDTH + h * B_DV:A_WIDTH + (h + 1) * B_DV] = (
                _rms(o[h], onorm_ref[...]) * _silu(z)).astype(BF16)

    s_cur = [s_scr[h] for h in range(B_HEADS)]
    o_prev = None
    for c in range(nc):
        idx = [c * B_HEADS + h for h in range(B_HEADS)]
        ws = [_dot(wq[n], s_cur[h].astype(BF16)) for h, n in enumerate(idx)]
        if o_prev is not None:
            gate_store(c - 1, o_prev)
        yield
        v_new = [sol[n][:, :B_DV] - ws[h][:CHUNK] for h, n in enumerate(idx)]
        vb = [v_new[h].astype(BF16) for h in range(B_HEADS)]
        qv = [_dot(qk[c * (B_HEADS // 2) + p], bstack(vb[2 * p], vb[2 * p + 1], zk))
              for p in range(B_HEADS // 2)]
        o_prev = [ws[h][CHUNK:] + qv[h // 2][:, (h % 2) * B_DV:(h % 2 + 1) * B_DV] for h in range(B_HEADS)]
        s_cur = [s_cur[h] * eg_last[n] + _dotg(kd[n], vb[h], TN_DIMS) for h, n in enumerate(idx)]
        yield
    for h in range(B_HEADS):
        s_scr[h] = s_cur[h]
    gate_store(nc - 1, o_prev)
    yield


def _ab_layer_staged_kernel(x_ref, g_ref, w_ref, wg_ref, wgt_ref, sink_ref, hist_ref, convw_ref, gprm_ref,
                            gprmt_ref, onorm_ref, wout_ref, convin_ref, sin_ref,
                            y_ref, kvtail_ref, xbtail_ref, sout_ref,
                            s_scr, xc_scr, kvprev_scr, mix_scr, *, ct, n_lead, pos_shift):
    i = pl.program_id(1)
    nsteps = pl.num_programs(1)
    ntail = min(WINDOW, ct)

    @pl.when(i == 0)
    def _():
        s_scr[...] = sin_ref[0]
        xc_scr[0:8, :] = convin_ref[0]
        kvprev_scr[1] = hist_ref[...]

    x = x_ref[0]
    hb = _rms(x, g_ref[...]).astype(BF16)
    proj = lambda a, b: _dot(hb, w_ref[:, a:b])
    env = {}

    def step(gen, n=1):
        for _ in range(n):
            next(gen, None)

    nblk = B_CONV_CH // LANES
    w = convw_ref[...]
    cvb = []

    xblk = []
    row8 = lax.broadcasted_iota(jnp.int32, (8, LANES), 0)

    def conv_block(j):
        cs = slice(j * LANES, (j + 1) * LANES)
        xj = xblk[j]
        h8 = xc_scr[0:8, cs]
        yc = xj * w[CONV_W - 1:CONV_W, cs]
        for k in range(1, CONV_W):
            sh = pltpu.roll(xj, k, 0)
            top = jnp.where(row8 < k, pltpu.roll(h8, k, 0), sh[0:8])
            yc = yc + jnp.concatenate([top, sh[8:]], axis=0) * w[CONV_W - 1 - k:CONV_W - k, cs]
        cvb.append(_silu(yc))
        xc_scr[0:8, cs] = xj[ct - 8:]

    pw = 2 * LANES
    for j in range(nblk // 2):
        xbj = proj(O_XB + j * pw, O_XB + (j + 1) * pw)
        xblk += [xbj[:, 0:LANES], xbj[:, LANES:]]
        xbtail_ref[0, :, j * pw:(j + 1) * pw] = xbj[ct - 8:]
        if j > 0:
            conv_block(2 * j - 2)
            conv_block(2 * j - 1)
    gt = _dot(hb, wg_ref[...])
    gtt = _dotg(wgt_ref[...], hb, NT_DIMS)
    conv_block(nblk - 2)
    conv_block(nblk - 1)

    gdn = _gdn_stages(i, cvb, gt, gtt, env, gprm_ref, gprmt_ref, onorm_ref, s_scr, mix_scr, ct=ct, n_lead=n_lead)
    qa_p = []
    for j in range(A_WIDTH // pw):
        qa_p.append(proj(O_QA + j * pw, O_QA + (j + 1) * pw))
        step(gdn, 2)
    qa = jnp.concatenate(qa_p, axis=-1)
    kv = proj(O_KV, O_ZA)
    kvtail_ref[0] = kv[ct - ntail:]
    slot = lax.rem(i, 2)
    kvb = jnp.concatenate([kvprev_scr[1 - slot], kv], axis=0).astype(BF16)
    if ct >= WINDOW:
        kvprev_scr[slot] = kv[ct - WINDOW:]
    swa = _swa_stages(i, qa, kvb, env, sink_ref, mix_scr, ct=ct, sb=ntail, pos_shift=pos_shift)
    n_sw = (ct // ntail) * A_KV_HEADS
    step(gdn)
    for _ in range(ct // CHUNK):
        step(gdn)
        step(swa)
    step(swa, max(n_sw - ct // CHUNK, 0))
    za_p, zb_p = [], []
    fill = ([lambda j=j: za_p.append(proj(O_ZA + j * pw, O_ZA + (j + 1) * pw)) for j in range(A_WIDTH // pw)]
            + [lambda j=j: zb_p.append(proj(O_ZB + j * pw, O_ZB + (j + 1) * pw)) for j in range(B_VAL // pw)])
    for lv in range(10):
        step(gdn)
        if lv % 2 == 1 and fill:
            fill.pop(0)()
        elif lv % 2 == 0:
            step(swa)
    for f in fill:
        f()
    env['za'] = jnp.concatenate(za_p, axis=-1)
    env['zb'] = jnp.concatenate(zb_p, axis=-1)
    step(gdn, ct // CHUNK)
    for _ in range(2 * (ct // CHUNK)):
        step(gdn)
        step(swa, -(-2 * n_sw // (2 * (ct // CHUNK))))
    for _ in swa:
        pass
    _swa_store(env, mix_scr, ct=ct, sb=ntail)
    ya = x + _dot(mix_scr[:, 0:A_WIDTH], wout_ref[0:A_WIDTH, :])
    for _ in gdn:
        pass
    y_ref[0] = ya + _dot(mix_scr[:, A_WIDTH:], wout_ref[A_WIDTH:, :])

    @pl.when(i == nsteps - 1)
    def _():
        sout_ref[0] = s_scr[...]


def _ab_layer_prompt(x3, norm_g, w_main, w_gate, w_gate_t, sink, hist_kv, conv_w8, gprm, gprm_t, onorm,
                     w_out, conv_in, s_in, ct, n_lead, pos_shift):
    B, L, _ = x3.shape
    ntail = min(WINDOW, ct)
    kern = functools.partial(_ab_layer_staged_kernel, ct=ct, n_lead=n_lead, pos_shift=pos_shift)
    full = lambda a: pl.BlockSpec(a.shape, lambda b, i: (0,) * a.ndim)
    blk = pl.BlockSpec((1, ct, D_MODEL), lambda b, i: (b, i, 0))
    per_b = lambda *s: pl.BlockSpec((1,) + s, lambda b, i: (b,) + (0,) * len(s))
    return pl.pallas_call(
        kern,
        grid=(B, L // ct),
        in_specs=[blk, full(norm_g), full(w_main), full(w_gate), full(w_gate_t), full(sink), full(hist_kv),
                  full(conv_w8), full(gprm), full(gprm_t), full(onorm), full(w_out),
                  full(conv_in), full(s_in)],
        out_specs=[blk, per_b(ntail, 2 * A_KV_WIDTH), per_b(8, B_CONV_CH), per_b(B_HEADS, B_DK, B_DV)],
        out_shape=[jax.ShapeDtypeStruct((B, L, D_MODEL), F32),
                   jax.ShapeDtypeStruct((B, ntail, 2 * A_KV_WIDTH), F32),
                   jax.ShapeDtypeStruct((B, 8, B_CONV_CH), F32),
                   jax.ShapeDtypeStruct((B, B_HEADS, B_DK, B_DV), F32)],
        scratch_shapes=[pltpu.VMEM((B_HEADS, B_DK, B_DV), F32),
                        pltpu.VMEM((8, B_CONV_CH), F32),
                        pltpu.VMEM((2, WINDOW, 2 * A_KV_WIDTH), F32),
                        pltpu.VMEM((ct, AB_MIX), BF16)],
        compiler_params=_cparams(("arbitrary", "arbitrary")),
        name="ab_layer_prompt",
    )(x3, norm_g, w_main, w_gate, w_gate_t, sink, hist_kv, conv_w8, gprm, gprm_t, onorm, w_out, conv_in, s_in)


def _c_layer_kernel(x_ref, g_ref, w_ref, wgk_ref, wup_ref, bgk_ref, onorm_ref, wout_ref, gfin_ref,
                    sin_ref, y_ref, sout_ref, st_scr, og_scr, *, ct, n_lead):
    i = pl.program_id(1)
    nsteps = pl.num_programs(1)

    @pl.when(i == 0)
    def _():
        st_scr[...] = sin_ref[0]

    x = x_ref[0]
    hb = _rms(x, g_ref[...]).astype(BF16)
    low = _dot(hb, wgk_ref[...])
    q_all = _dot(hb, w_ref[:, 0:C_KEY])
    la_all = _gla_log_decay(low, wup_ref, bgk_ref)
    k_all = _dot(hb, w_ref[:, C_KEY:2 * C_KEY])
    v_h, z_h = [], []
    later = ([lambda h=h: v_h.append(_dot(hb, w_ref[:, 2 * C_KEY + h * C_DV:2 * C_KEY + (h + 1) * C_DV]).astype(BF16))
              for h in range(C_HEADS)]
             + [lambda h=h: z_h.append(_dot(hb, w_ref[:, 2 * C_KEY + C_VAL + h * C_DV:
                                                       2 * C_KEY + C_VAL + (h + 1) * C_DV]))
                for h in range(C_HEADS)])

    def run_later(n):
        for _ in range(n):
            if later:
                later.pop(0)()
    rows = i * ct + lax.broadcasted_iota(jnp.int32, la_all.shape, 0)
    la_all = jnp.where(rows >= n_lead, la_all, 0.0)

    ii = lax.broadcasted_iota(jnp.int32, (CHUNK, CHUNK), 0)
    jj = lax.broadcasted_iota(jnp.int32, (CHUNK, CHUNK), 1)
    lower = ii >= jj
    ltri = lower.astype(BF16)
    nc = ct // CHUNK
    rs = lambda c: slice(c * CHUNK, (c + 1) * CHUNK)
    ks = lambda h: slice(h * C_DK, (h + 1) * C_DK)
    vs = lambda h: slice(h * C_DV, (h + 1) * C_DV)
    items = [(c, h) for c in range(nc) for h in range(C_HEADS)]
    n_items = len(items)
    bc_all = [sum(_dot(ltri, p) for p in _split2(la_all[rs(c)])) for c in range(nc)]
    run_later(2)
    every =max(nc // C_HEADS, 1)
    qd, kinv, kd, gl, qk = [], [], [], [], []
    for n, (c, h) in enumerate(items):
        bc = bc_all[c][:, ks(h)]
        k = k_all[rs(c), ks(h)]
        bl = bc[CHUNK - 1:CHUNK, :]
        qd.append((q_all[rs(c), ks(h)] * (C_DK ** -0.5) * jnp.exp(bc)).astype(BF16))
        kinv.append((k * jnp.exp(-bc)).astype(BF16))
        kd.append((k * jnp.exp(bl - bc)).astype(BF16))
        gl.append(jnp.exp(bl))
        if h == C_HEADS - 1:
            qk += [jnp.where(lower, _dotg(qd[m], kinv[m], NT_DIMS), 0.0).astype(BF16)
                   for m in range(n - C_HEADS + 1, n + 1)]
            if c % every == every - 1:
                run_later(1)
    run_later(C_HEADS - len(v_h))
    v_ = [v_h[h][rs(c)] for c, h in items]
    o_intra = []
    for c in range(nc):
        o_intra += [_dot(qk[c * C_HEADS + h], v_[c * C_HEADS + h]) for h in range(C_HEADS)]
        if c % every == every - 1:
            run_later(1)
    run_later(len(later))

    rb = min(ct, 4 * CHUNK)

    def gate_store(c, o):
        for h in range(C_HEADS):
            og_scr[rs(c), vs(h)] = (_rms(o[h], onorm_ref[...]) * _silu(z_h[h][rs(c)])).astype(BF16)
        if ((c + 1) * CHUNK) % rb == 0:
            r = slice((c + 1) * CHUNK - rb, (c + 1) * CHUNK)
            y_ref[0, r, :] = _rms(x[r] + _dot(og_scr[r, :], wout_ref[...]), gfin_ref[...])

    st = [st_scr[h] for h in range(C_HEADS)]
    o_prev = None
    pad_rows = jnp.zeros((8 - C_HEADS, C_DK), F32)
    for c in range(nc):
        idx = [c * C_HEADS + h for h in range(C_HEADS)]
        glt = jnp.concatenate([gl[n] for n in idx] + [pad_rows], axis=0).T
        o = [o_intra[n] + _dot(qd[n], st[h].astype(BF16)) for h, n in enumerate(idx)]
        st = [st[h] * glt[:, h:h + 1] + _dotg(kd[n], v_[n], TN_DIMS) for h, n in enumerate(idx)]
        if o_prev is not None:
            gate_store(c - 1, o_prev)
        o_prev = o
    gate_store(nc - 1, o_prev)
    for h in range(C_HEADS):
        st_scr[h] = st[h]

    @pl.when(i == nsteps - 1)
    def _():
        sout_ref[0] = st_scr[...]


def _c_layer_prompt(x3, norm_g, w_main, w_gk, w_up, b_gk, onorm, w_out, final_g, st_in, ct, n_lead):
    B, L, _ = x3.shape
    kern = functools.partial(_c_layer_kernel, ct=ct, n_lead=n_lead)
    full = lambda a: pl.BlockSpec(a.shape, lambda b, i: (0,) * a.ndim)
    blk = pl.BlockSpec((1, ct, D_MODEL), lambda b, i: (b, i, 0))
    st = pl.BlockSpec((1, C_HEADS, C_DK, C_DV), lambda b, i: (b, 0, 0, 0))
    return pl.pallas_call(
        kern,
        grid=(B, L // ct),
        in_specs=[blk, full(norm_g), full(w_main), full(w_gk), full(w_up), full(b_gk), full(onorm),
                  full(w_out), full(final_g), full(st_in)],
        out_specs=[blk, st],
        out_shape=[jax.ShapeDtypeStruct((B, L, D_MODEL), F32),
                   jax.ShapeDtypeStruct((B, C_HEADS, C_DK, C_DV), F32)],
        scratch_shapes=[pltpu.VMEM((C_HEADS, C_DK, C_DV), F32), pltpu.VMEM((ct, C_VAL), BF16)],
        compiler_params=_cparams(("arbitrary", "arbitrary")),
        name="c_layer_prompt",
    )(x3, norm_g, w_main, w_gk, w_up, b_gk, onorm, w_out, final_g, st_in)


def _ab_proj_kernel(x_ref, g_ref, w_ref, wg_ref, qa_ref, kv_ref, za_ref, xb_ref, zb_ref, gates_ref):
    hb = _rms(x_ref[...], g_ref[...]).astype(BF16)
    qa_ref[...] = _dot(hb, w_ref[:, O_QA:O_KV])
    kv_ref[...] = _dot(hb, w_ref[:, O_KV:O_ZA])
    za_ref[...] = _dot(hb, w_ref[:, O_ZA:O_XB])
    xb_ref[...] = _dot(hb, w_ref[:, O_XB:O_ZB])
    zb_ref[...] = _dot(hb, w_ref[:, O_ZB:AB_MAIN])
    gates_ref[...] = _dot(hb, wg_ref[...])


def _ab_proj(x2d, norm_g, w_main, w_gate):
    n = x2d.shape[0]
    full = lambda a: pl.BlockSpec(a.shape, lambda: (0,) * a.ndim)
    widths = (A_WIDTH, 2 * A_KV_WIDTH, A_WIDTH, B_CONV_CH, B_VAL, LANES)
    args = (x2d, norm_g, w_main, w_gate)
    return pl.pallas_call(
        _ab_proj_kernel,
        in_specs=[full(a) for a in args],
        out_specs=[pl.BlockSpec((n, w), lambda: (0, 0)) for w in widths],
        out_shape=[jax.ShapeDtypeStruct((n, w), F32) for w in widths],
        compiler_params=pltpu.CompilerParams(vmem_limit_bytes=VMEM_LIMIT),
        name="ab_in_proj_sample",
    )(*args)


def _swa_sample_kernel(q_ref, kvn_ref, za_ref, ck_ref, cv_ref, sink_ref, o_ref, nk_ref, nv_ref, *, bt):
    row = lax.broadcasted_iota(jnp.int32, (WINDOW, A_KV_WIDTH), 0)
    hrow = lax.broadcasted_iota(jnp.int32, (A_HEADS, A_HD), 0)
    sk = sink_ref[...]
    nkb, nvb = [], []
    for b in range(bt):
        nk = jnp.where(row == WINDOW - 1, kvn_ref[b:b + 1, 0:A_KV_WIDTH], pltpu.roll(ck_ref[b], WINDOW - 1, 0))
        nv = jnp.where(row == WINDOW - 1, kvn_ref[b:b + 1, A_KV_WIDTH:], pltpu.roll(cv_ref[b], WINDOW - 1, 0))
        nk_ref[b] = nk
        nv_ref[b] = nv
        nkb.append(nk.astype(BF16))
        nvb.append(nv.astype(BF16))
    items = [(b, j) for b in range(bt) for j in range(A_KV_HEADS)]
    q8 = [q_ref[b].astype(BF16) for b in range(bt)]
    sc = [_dotg(q8[b], nkb[b][:, j * A_HD:(j + 1) * A_HD], NT_DIMS) * (A_HD ** -0.5) for b, j in items]
    m = [jnp.maximum(jnp.max(s, axis=-1, keepdims=True), sk) for s in sc]
    p = [jnp.exp(sc[n] - m[n]) for n in range(len(items))]
    den = [jnp.sum(p[n], axis=-1, keepdims=True) + jnp.exp(sk - m[n]) for n in range(len(items))]
    oj = [_dot(p[n].astype(BF16), nvb[b][:, j * A_HD:(j + 1) * A_HD]) / den[n] for n, (b, j) in enumerate(items)]
    for b in range(bt):
        o8 = jnp.where(hrow >= A_GROUP, oj[b * A_KV_HEADS + 1], oj[b * A_KV_HEADS])
        o_ref[b] = o8 * _silu(za_ref[b])


def _swa_sample(q3, kv_new, za3, cache_k, cache_v, sink_col, bt):
    nb = q3.shape[0]
    hd = pl.BlockSpec((bt, A_HEADS, A_HD), lambda i: (i, 0, 0))
    cache = pl.BlockSpec((bt, WINDOW, A_KV_WIDTH), lambda i: (i, 0, 0))
    return pl.pallas_call(
        functools.partial(_swa_sample_kernel, bt=bt),
        grid=(nb // bt,),
        in_specs=[hd, pl.BlockSpec((bt, 2 * A_KV_WIDTH), lambda i: (i, 0)), hd, cache, cache,
                  pl.BlockSpec(sink_col.shape, lambda i: (0, 0))],
        out_specs=[hd, cache, cache],
        out_shape=[jax.ShapeDtypeStruct(q3.shape, F32),
                   jax.ShapeDtypeStruct(cache_k.shape, F32),
                   jax.ShapeDtypeStruct(cache_v.shape, F32)],
        compiler_params=_cparams(("arbitrary",)),
        name="swa_sample",
    )(q3, kv_new, za3, cache_k, cache_v, sink_col)


def _gdn_sample_prep_kernel(xb_ref, h0_ref, h1_ref, h2_ref, gates_ref, convw_ref, gprm_ref,
                            q_ref, k_ref, v_ref, bg_ref):
    w = convw_ref[...]
    y = (h0_ref[...] * w[0:1, :] + h1_ref[...] * w[1:2, :] + h2_ref[...] * w[2:3, :]
         + xb_ref[...] * w[3:4, :])
    cv = _silu(y)
    for h in range(B_HEADS):
        q = cv[:, h * B_DK:(h + 1) * B_DK]
        k = cv[:, B_KEY + h * B_DK:B_KEY + (h + 1) * B_DK]
        q_ref[:, h * B_DK:(h + 1) * B_DK] = (
            q * lax.rsqrt(jnp.sum(q * q, axis=-1, keepdims=True) + EPS) * (B_DK ** -0.5))
        k_ref[:, h * B_DK:(h + 1) * B_DK] = k * lax.rsqrt(jnp.sum(k * k, axis=-1, keepdims=True) + EPS)
    v_ref[...] = cv[:, 2 * B_KEY:]
    gt = gates_ref[...]
    col = lax.broadcasted_iota(jnp.int32, gt.shape, 1)
    g = -gprm_ref[0:1, :] * _softplus(gt + gprm_ref[1:2, :])
    bg_ref[...] = jnp.where(col < B_HEADS, _sigmoid(gt), jnp.exp(g))


def _gdn_sample_prep(xb, h0, h1, h2, gates, conv_w8, gprm):
    n = xb.shape[0]
    full = lambda a: pl.BlockSpec(a.shape, lambda: (0,) * a.ndim)
    args = (xb, h0, h1, h2, gates, conv_w8, gprm)
    return pl.pallas_call(
        _gdn_sample_prep_kernel,
        in_specs=[full(a) for a in args],
        out_specs=[pl.BlockSpec((n, B_KEY), lambda: (0, 0))] * 3 + [pl.BlockSpec((n, LANES), lambda: (0, 0))],
        out_shape=[jax.ShapeDtypeStruct((n, B_KEY), F32)] * 3 + [jax.ShapeDtypeStruct((n, LANES), F32)],
        compiler_params=pltpu.CompilerParams(vmem_limit_bytes=VMEM_LIMIT),
        name="gdn_sample_prep",
    )(*args)


def _gdn_sample_kernel(s_ref, qt_ref, kt_ref, v_ref, bg_ref, zb_ref, onorm_ref, o_ref, sout_ref, *, bt):
    items = [(b, h) for b in range(bt) for h in range(B_HEADS)]
    hs = lambda h: slice(h * B_DV, (h + 1) * B_DV)
    kt = [kt_ref[:, hs(h)].T for h in range(B_HEADS)]
    qt = [qt_ref[:, hs(h)].T for h in range(B_HEADS)]
    kc = [kt[h][:, b:b + 1] for b, h in items]
    eg = [bg_ref[b:b + 1, B_HEADS + h:B_HEADS + h + 1] for b, h in items]
    ks = [jnp.sum(kc[n] * s_ref[b, h], axis=0, keepdims=True) for n, (b, h) in enumerate(items)]
    v_new = [bg_ref[b:b + 1, h:h + 1] * (v_ref[b:b + 1, hs(h)] - eg[n] * ks[n]) for n, (b, h) in enumerate(items)]
    o = []
    for n, (b, h) in enumerate(items):
        s_new = eg[n] * s_ref[b, h] + kc[n] * v_new[n]
        sout_ref[b, h] = s_new
        o.append(jnp.sum(qt[h][:, b:b + 1] * s_new, axis=0, keepdims=True))
    o_blk = jnp.concatenate([jnp.concatenate([o[b * B_HEADS + h] for h in range(B_HEADS)], axis=-1)
                             for b in range(bt)], axis=0)
    for h in range(B_HEADS):
        o_ref[:, hs(h)] = _rms(o_blk[:, hs(h)], onorm_ref[...]) * _silu(zb_ref[:, hs(h)])


def _gdn_sample(state, qt, kt, v, bg, zb, onorm, bt):
    nb = state.shape[0]
    st = pl.BlockSpec((bt, B_HEADS, B_DK, B_DV), lambda i: (i, 0, 0, 0))
    col = pl.BlockSpec((1, B_HEADS, B_DK, bt), lambda i: (i, 0, 0, 0))
    row = lambda n: pl.BlockSpec((bt, n), lambda i: (i, 0))
    return pl.pallas_call(
        functools.partial(_gdn_sample_kernel, bt=bt),
        grid=(nb // bt,),
        in_specs=[st, row(B_KEY), row(B_KEY), row(B_VAL), row(LANES), row(B_VAL),
                  pl.BlockSpec(onorm.shape, lambda i: (0, 0))],
        out_specs=[row(B_VAL), st],
        out_shape=[jax.ShapeDtypeStruct((nb, B_VAL), F32), jax.ShapeDtypeStruct(state.shape, F32)],
        compiler_params=_cparams(("arbitrary",)),
        name="gdn_sample",
    )(state, qt, kt, v, bg, zb, onorm)


def _ab_out_c_proj_kernel(x_ref, oa_ref, ob_ref, wout_ref, g_ref, w_ref, wgk_ref, wup_ref, bgk_ref,
                          y_ref, q_ref, k_ref, v_ref, z_ref, la_ref):
    y = (x_ref[...] + _dot(oa_ref[...].astype(BF16), wout_ref[0:A_WIDTH, :])
         + _dot(ob_ref[...].astype(BF16), wout_ref[A_WIDTH:, :]))
    y_ref[...] = y
    hb = _rms(y, g_ref[...]).astype(BF16)
    q_ref[...] = _dot(hb, w_ref[:, 0:C_KEY])
    k_ref[...] = _dot(hb, w_ref[:, C_KEY:2 * C_KEY])
    v_ref[...] = _dot(hb, w_ref[:, 2 * C_KEY:2 * C_KEY + C_VAL])
    z_ref[...] = _dot(hb, w_ref[:, 2 * C_KEY + C_VAL:C_MAIN])
    la_ref[...] = _gla_log_decay(_dot(hb, wgk_ref[...]), wup_ref, bgk_ref)


def _ab_out_c_proj(x2d, oa, ob, w_out, norm_g, w_main, w_gk, w_up, b_gk):
    n = x2d.shape[0]
    full = lambda a: pl.BlockSpec(a.shape, lambda: (0,) * a.ndim)
    widths = (D_MODEL, C_KEY, C_KEY, C_VAL, C_VAL, C_KEY)
    args = (x2d, oa, ob, w_out, norm_g, w_main, w_gk, w_up, b_gk)
    return pl.pallas_call(
        _ab_out_c_proj_kernel,
        in_specs=[full(a) for a in args],
        out_specs=[pl.BlockSpec((n, w), lambda: (0, 0)) for w in widths],
        out_shape=[jax.ShapeDtypeStruct((n, w), F32) for w in widths],
        compiler_params=pltpu.CompilerParams(vmem_limit_bytes=VMEM_LIMIT),
        name="ab_out_c_in_proj_sample",
    )(*args)


def _gla_sample_kernel(s_ref, qt_ref, kt_ref, lat_ref, v_ref, z_ref, onorm_ref, o_ref, sout_ref, *, bt):
    items = [(b, h) for b in range(bt) for h in range(C_HEADS)]
    hs = lambda h: slice(h * C_DV, (h + 1) * C_DV)
    ks = lambda h: slice(h * C_DK, (h + 1) * C_DK)
    at = [jnp.exp(lat_ref[:, ks(h)]).T for h in range(C_HEADS)]
    kt = [kt_ref[:, ks(h)].T for h in range(C_HEADS)]
    qt = [(qt_ref[:, ks(h)] * (C_DK ** -0.5)).T for h in range(C_HEADS)]
    o = []
    for n, (b, h) in enumerate(items):
        s_new = at[h][:, b:b + 1] * s_ref[b, h] + kt[h][:, b:b + 1] * v_ref[b:b + 1, hs(h)]
        sout_ref[b, h] = s_new
        o.append(jnp.sum(qt[h][:, b:b + 1] * s_new, axis=0, keepdims=True))
    o_blk = jnp.concatenate([jnp.concatenate([o[b * C_HEADS + h] for h in range(C_HEADS)], axis=-1)
                             for b in range(bt)], axis=0)
    for h in range(C_HEADS):
        o_ref[:, hs(h)] = _rms(o_blk[:, hs(h)], onorm_ref[...]) * _silu(z_ref[:, hs(h)])


def _gla_sample(state, qt, kt, lat, v, z, onorm, bt):
    nb = state.shape[0]
    st = pl.BlockSpec((bt, C_HEADS, C_DK, C_DV), lambda i: (i, 0, 0, 0))
    col = pl.BlockSpec((1, C_HEADS, C_DK, bt), lambda i: (i, 0, 0, 0))
    row = lambda n: pl.BlockSpec((bt, n), lambda i: (i, 0))
    return pl.pallas_call(
        functools.partial(_gla_sample_kernel, bt=bt),
        grid=(nb // bt,),
        in_specs=[st, row(C_KEY), row(C_KEY), row(C_KEY), row(C_VAL), row(C_VAL),
                  pl.BlockSpec(onorm.shape, lambda i: (0, 0))],
        out_specs=[row(C_VAL), st],
        out_shape=[jax.ShapeDtypeStruct((nb, C_VAL), F32), jax.ShapeDtypeStruct(state.shape, F32)],
        compiler_params=_cparams(("arbitrary",)),
        name="gla_sample",
    )(state, qt, kt, lat, v, z, onorm)


def _c_out_kernel(x_ref, o_ref, w_ref, g_ref, y_ref):
    y_ref[...] = _rms(x_ref[...] + _dot(o_ref[...].astype(BF16), w_ref[...]), g_ref[...])


def _c_out(x2d, oc, w_out, final_g):
    n = x2d.shape[0]
    full = lambda a: pl.BlockSpec(a.shape, lambda: (0,) * a.ndim)
    args = (x2d, oc, w_out, final_g)
    return pl.pallas_call(
        _c_out_kernel,
        in_specs=[full(a) for a in args],
        out_specs=pl.BlockSpec((n, D_MODEL), lambda: (0, 0)),
        out_shape=jax.ShapeDtypeStruct((n, D_MODEL), F32),
        compiler_params=pltpu.CompilerParams(vmem_limit_bytes=VMEM_LIMIT),
        name="c_out_proj_norm_sample",
    )(*args)


def _cast_kernel(w_ref, o_ref):
    o_ref[...] = w_ref[:, 0:o_ref.shape[1]].astype(BF16)


def _to_bf16(w, n_cols, rows=128):
    k, n = w.shape
    return pl.pallas_call(
        _cast_kernel,
        grid=(k // rows,),
        in_specs=[pl.BlockSpec((rows, n), lambda i: (i, 0))],
        out_specs=pl.BlockSpec((rows, n_cols), lambda i: (i, 0)),
        out_shape=jax.ShapeDtypeStruct((k, n_cols), BF16),
        compiler_params=_cparams(("arbitrary",)),
        name="weight_to_bf16",
    )(w)


def _to_cols(a, heads, dk, bt):
    n = a.shape[0]
    return a.reshape(n // bt, bt, heads, dk).transpose(0, 2, 3, 1)


def kernel(x_prompt, x_sample, cache_swa_k, cache_swa_v, state_dn_conv, state_dn, state_gla,
           meta_tokens, norm_ab, w_in_ab, sink_a, conv_b, a_log_b, dt_bias_b, onorm_b, w_out_ab,
           norm_c, w_in_c, w_gk_up, b_gk, onorm_c, w_out_c, final_norm):
    Bp, L, _ = x_prompt.shape
    Bs = x_sample.shape[0]
    CT, BT = 256, 8

    w_ab = w_in_ab[0]
    w_ab_main = _to_bf16(w_ab, AB_MAIN)
    w_ab_gate = jnp.pad(w_ab[:, AB_MAIN:], ((0, 0), (0, LANES - 2 * B_HEADS))).astype(BF16)
    w_ab_gate_t = jnp.pad(w_ab[:, AB_MAIN:].T, ((0, 16 - 2 * B_HEADS), (0, 0))).astype(BF16)
    g_ab = norm_ab[0][None, :]
    sink_row = sink_a[0][None, :]
    sink_col = sink_a[0][:, None]
    conv_w8 = jnp.pad(conv_b[0], ((0, 8 - CONV_W), (0, 0)))
    gp = jnp.stack([jnp.exp(a_log_b[0]), dt_bias_b[0]])
    gprm = jnp.pad(gp, ((0, 6), (B_HEADS, LANES - 2 * B_HEADS)))
    gprm_t = jnp.pad(gp.T, ((B_HEADS, 16 - 2 * B_HEADS), (0, LANES - 2)))
    onb = onorm_b[0][None, :]
    w_ab_out = w_out_ab[0].astype(BF16)
    w_c = w_in_c[0]
    w_c_main = _to_bf16(w_c, C_MAIN)
    w_c_gk = jnp.pad(w_c[:, C_MAIN:], ((0, 0), (0, LANES - C_RANK))).astype(BF16)
    w_up = jnp.pad(w_gk_up[0], ((0, LANES - C_RANK), (0, 0)))
    bgk = b_gk[0][None, :]
    g_c = norm_c[0][None, :]
    onc = onorm_c[0][None, :]
    w_c_out = w_out_c[0].astype(BF16)
    g_fin = final_norm[None, :]

    def ab_layer(x3, ct, hist_kv, pos_shift, conv_in, s_in, n_lead):
        return _ab_layer_prompt(x3, g_ab, w_ab_main, w_ab_gate, w_ab_gate_t, sink_row, hist_kv, conv_w8,
                                gprm, gprm_t, onb, w_ab_out, conv_in, s_in, ct, n_lead, pos_shift)

    def c_layer(x3, ct, st_in, n_lead):
        return _c_layer_prompt(x3, g_c, w_c_main, w_c_gk, w_up, bgk, onc, w_c_out, g_fin, st_in, ct, n_lead)

    xpre = jnp.concatenate([jnp.zeros((LEAD, D_MODEL), F32), meta_tokens], axis=0)[None]
    y_pre, kv_pre, xbtail_pre, sdn_pre = ab_layer(
        xpre, CHUNK, jnp.zeros((WINDOW, 2 * A_KV_WIDTH), F32), -LEAD,
        jnp.zeros((1, 8, B_CONV_CH), F32), jnp.zeros((1, B_HEADS, B_DK, B_DV), F32), LEAD)
    _, sgla_pre = c_layer(y_pre, CHUNK, jnp.zeros((1, C_HEADS, C_DK, C_DV), F32), LEAD)

    hist_kv = jnp.concatenate([jnp.zeros((WINDOW - CHUNK, 2 * A_KV_WIDTH), F32), kv_pre[0]], axis=0)
    y1, kv_tail, xb_tail, sdn_p = ab_layer(
        x_prompt, 2 * CT, hist_kv, N_META, xbtail_pre, sdn_pre, 0)
    y_prompt, sgla_p = c_layer(y1, 2 * CT, sgla_pre, 0)

    swa_k_p = kv_tail[:, :, :A_KV_WIDTH].reshape(1, Bp, WINDOW, A_KV_HEADS, A_HD)
    swa_v_p = kv_tail[:, :, A_KV_WIDTH:].reshape(1, Bp, WINDOW, A_KV_HEADS, A_HD)
    dn_conv_p = xb_tail[:, 8 - (CONV_W - 1):, :][None]

    xs = x_sample.reshape(Bs, D_MODEL)
    qa, kvn, za, xb, zb, gates = _ab_proj(xs, g_ab, w_ab_main, w_ab_gate)
    oa3, nk, nv = _swa_sample(qa.reshape(Bs, A_HEADS, A_HD), kvn, za.reshape(Bs, A_HEADS, A_HD),
                              cache_swa_k[0].reshape(Bs, WINDOW, A_KV_WIDTH),
                              cache_swa_v[0].reshape(Bs, WINDOW, A_KV_WIDTH), sink_col, BT)
    hist = state_dn_conv[0]
    qn, kn, vn, bg = _gdn_sample_prep(xb, hist[:, 0], hist[:, 1], hist[:, 2], gates, conv_w8, gprm)
    ob, sdn_s = _gdn_sample(state_dn[0], qn, kn,
                            vn, bg, zb, onb, BT)
    ys, qc, kc, vc, zc, la = _ab_out_c_proj(xs, oa3.reshape(Bs, A_WIDTH), ob, w_ab_out,
                                            g_c, w_c_main, w_c_gk, w_up, bgk)
    ocs, sgla_s = _gla_sample(state_gla[0], qc, kc, la, vc, zc, onc, BT)
    y_sample = _c_out(ys, ocs, w_c_out, g_fin).reshape(Bs, 1, D_MODEL)
    dn_conv_s = jnp.concatenate([hist[:, 1:], xb[:, None, :]], axis=1)[None]

    return (y_prompt, y_sample, swa_k_p, swa_v_p, dn_conv_p, sdn_p[None], sgla_p[None],
            nk.reshape(1, Bs, WINDOW, A_KV_HEADS, A_HD), nv.reshape(1, Bs, WINDOW, A_KV_HEADS, A_HD),
            dn_conv_s, sdn_s[None], sgla_s[None])
```

```python
import functools

import jax
import jax.numpy as jnp
from jax import lax
from jax.experimental import pallas as pl
from jax.experimental.pallas import tpu as pltpu

F32 = jnp.float32
BF16 = jnp.bfloat16
EPS = 1e-6

D_MODEL = 1024
N_META = 16
CHUNK = 64
LEAD = (-N_META) % CHUNK
A_HEADS, A_KV_HEADS, A_HD = 8, 2, 64
A_GROUP = A_HEADS // A_KV_HEADS
A_WIDTH = A_HEADS * A_HD
A_KV_WIDTH = A_KV_HEADS * A_HD
WINDOW = 128
B_HEADS, B_DK, B_DV = 4, 128, 128
B_KEY = B_HEADS * B_DK
B_VAL = B_HEADS * B_DV
CONV_W = 4
B_CONV_CH = 2 * B_KEY + B_VAL
C_HEADS, C_DK, C_DV = 4, 128, 256
C_KEY = C_HEADS * C_DK
C_VAL = C_HEADS * C_DV
C_RANK = 16
C_GATE_NORM = 16.0
AB_MAIN = 2 * A_WIDTH + 2 * A_KV_WIDTH + B_CONV_CH + B_VAL
AB_MIX = A_WIDTH + B_VAL
C_MAIN = 2 * C_KEY + 2 * C_VAL
O_QA, O_KV, O_ZA, O_XB, O_ZB = 0, 512, 768, 1280, 2816

LANES = 128
SUBLANES = 8
BF16_ROWS = 16
MXU_WIDTH = 2 * LANES
VMEM_LIMIT = 56 * 1024 * 1024

PROMPT_TILE = 512
SAMPLE_BLOCK = 8
NEG = -1e30

NT_DIMS = (((1,), (1,)), ((), ()))
TN_DIMS = (((0,), (0,)), ((), ()))


def _cparams(sem):
    return pltpu.CompilerParams(dimension_semantics=sem, vmem_limit_bytes=VMEM_LIMIT)


def _dot(a, b):
    return jnp.dot(a, b, preferred_element_type=F32)


def _dotg(a, b, dims):
    return lax.dot_general(a, b, dims, preferred_element_type=F32)


def _split3(a):
    hi = a.astype(BF16)
    r = a - hi.astype(F32)
    mid = r.astype(BF16)
    lo = (r - mid.astype(F32)).astype(BF16)
    return hi, mid, lo


def _split2(a):
    hi = a.astype(BF16)
    return hi, (a - hi.astype(F32)).astype(BF16)


def _sigmoid(x):
    return 1.0 / (1.0 + jnp.exp(-x))


def _silu(x):
    return x * _sigmoid(x)


def _softplus(x):
    return jnp.maximum(x, 0.0) + jnp.log1p(jnp.exp(-jnp.abs(x)))


def _log_sigmoid(x):
    return jnp.minimum(x, 0.0) - jnp.log1p(jnp.exp(-jnp.abs(x)))


def _rms(x, g):
    return x * lax.rsqrt(jnp.mean(x * x, axis=-1, keepdims=True) + EPS) * g


def _gla_log_decay(low, wup_ref, bgk_ref):
    lh, lm, _ = _split3(low)
    wh, wm, _ = _split3(wup_ref[...])
    up = _dot(lh, wh) + _dot(lh, wm) + _dot(lm, wh)
    return _log_sigmoid(up + bgk_ref[...]) * (1.0 / C_GATE_NORM)


def _swa_stages(i, qa, kvb, env, sink_ref, *, ct, sb, pos_shift):
    r = lax.broadcasted_iota(jnp.int32, (sb, WINDOW + sb), 0)
    c = lax.broadcasted_iota(jnp.int32, (sb, WINDOW + sb), 1)
    diff = r - c + WINDOW
    in_window = (diff >= 0) & (diff < WINDOW)
    nsb = ct // sb
    items = [(s, j) for s in range(nsb) for j in range(A_KV_HEADS)]
    masks = [jnp.concatenate([in_window & (i * ct + s * sb + c - WINDOW + pos_shift >= 0)] * A_GROUP, axis=0)
             for s in range(nsb)]
    sinks = [jnp.concatenate([jnp.broadcast_to(sink_ref[0:1, j * A_GROUP + g:j * A_GROUP + g + 1], (sb, 1))
                              for g in range(A_GROUP)], axis=0) for j in range(A_KV_HEADS)]
    sc = []
    for s, j in items:
        qs = (jnp.concatenate([qa[s * sb:(s + 1) * sb, (j * A_GROUP + g) * A_HD:(j * A_GROUP + g + 1) * A_HD]
                               for g in range(A_GROUP)], axis=0) * (A_HD ** -0.5)).astype(BF16)
        sc.append(_dotg(qs, kvb[s * sb:s * sb + WINDOW + sb, j * A_HD:(j + 1) * A_HD], NT_DIMS))
        yield
    p, esk = [], []
    for n, (s, j) in enumerate(items):
        scm = jnp.where(masks[s], sc[n], NEG)
        mx = jnp.maximum(jnp.max(scm, axis=-1, keepdims=True), sinks[j])
        p.append(jnp.exp(scm - mx).astype(BF16))
        esk.append(jnp.exp(sinks[j] - mx))
        yield
    pv = []
    ones = jnp.ones((WINDOW + sb, A_HD), BF16)
    for n, (s, j) in enumerate(items):
        vx = jnp.concatenate([kvb[s * sb:s * sb + WINDOW + sb,
                                  A_KV_WIDTH + j * A_HD:A_KV_WIDTH + (j + 1) * A_HD], ones], axis=-1)
        pvx = _dot(p[n], vx)
        pv.append(pvx[:, :A_HD] / (pvx[:, A_HD:] + esk[n]))
        yield
    env['pv'] = pv


def _swa_store(env, mix_scr, *, ct, sb):
    pv = env['pv']
    for s in range(ct // sb):
        o = jnp.concatenate([pv[s * A_KV_HEADS + j][g * sb:(g + 1) * sb]
                             for j in range(A_KV_HEADS) for g in range(A_GROUP)], axis=-1)
        mix_scr[s * sb:(s + 1) * sb, 0:A_WIDTH] = (o * _silu(env['za'][s * sb:(s + 1) * sb])).astype(BF16)


def _gdn_stages(i, cvb, gt, gtt, env, gprm_ref, gprmt_ref, onorm_ref, s_scr, mix_scr, *, ct, n_lead):
    nc = ct // CHUNK
    items = [(c, h) for c in range(nc) for h in range(B_HEADS)]
    n_items = len(items)
    rs = lambda c: slice(c * CHUNK, (c + 1) * CHUNK)
    qn, kn = [], []
    for h in range(B_HEADS):
        q = cvb[h]
        k = cvb[B_HEADS + h]
        qn.append(q * lax.rsqrt(jnp.sum(q * q, axis=-1, keepdims=True) + EPS) * (B_DK ** -0.5))
        kn.append(k * lax.rsqrt(jnp.sum(k * k, axis=-1, keepdims=True) + EPS))
        yield
    beta = _sigmoid(gt)
    rows = i * ct + lax.broadcasted_iota(jnp.int32, gt.shape, 0)
    gcol = jnp.where(rows >= n_lead, -gprm_ref[0:1, :] * _softplus(gt + gprm_ref[1:2, :]), 0.0)
    lanes = i * ct + lax.broadcasted_iota(jnp.int32, gtt.shape, 1)
    grow = jnp.where(lanes >= n_lead, -gprmt_ref[:, 0:1] * _softplus(gtt + gprmt_ref[:, 1:2]), 0.0)
    ii = lax.broadcasted_iota(jnp.int32, (CHUNK, CHUNK), 0)
    jj = lax.broadcasted_iota(jnp.int32, (CHUNK, CHUNK), 1)
    lower = ii >= jj
    strict = ii > jj
    ltri = lower.astype(BF16)
    utri = (ii <= jj).astype(BF16)
    gcum_c = [sum(_dot(ltri, p) for p in _split3(gcol[rs(c)])) for c in range(nc)]
    gcum_r = [sum(_dot(p, utri) for p in _split3(grow[:, rs(c)])) for c in range(nc)]
    yield
    q_ = [qn[h][rs(c)] for c, h in items]
    k_ = [kn[h][rs(c)] for c, h in items]
    v_ = [cvb[2 * B_HEADS + h][rs(c)] for c, h in items]
    bh = [beta[rs(c), h:h + 1] for c, h in items]
    gc = [gcum_c[c][:, B_HEADS + h:B_HEADS + h + 1] for c, h in items]
    gr = [gcum_r[c][B_HEADS + h:B_HEADS + h + 1, :] for c, h in items]
    kb, a, qk = [], [], []
    for n in range(n_items):
        decay = jnp.exp(jnp.where(lower, gc[n] - gr[n], NEG))
        kb.append(k_[n] * bh[n])
        kq = _dotg(jnp.concatenate([kb[n], q_[n]], axis=0).astype(BF16), k_[n].astype(BF16), NT_DIMS)
        a.append(jnp.where(strict, kq[:CHUNK] * decay, 0.0))
        qk.append(jnp.where(lower, kq[CHUNK:] * decay, 0.0).astype(BF16))
        if n % B_HEADS == B_HEADS - 1:
            yield
    doff = [-jnp.where((ii >> 1) == (jj >> 1), a[n], 0.0) for n in range(n_items)]
    for lg in range(1, 6):
        m = ((ii >> (lg + 1)) == (jj >> (lg + 1))) & ((ii >> lg) != (jj >> lg))
        bm = [jnp.where(m, a[n], 0.0) for n in range(n_items)]
        db = [doff[n].astype(BF16) for n in range(n_items)]
        y = [bm[n] + _dot(db[n], bm[n].astype(BF16)) for n in range(n_items)]
        yield
        x = [y[n] + _dot(y[n].astype(BF16), db[n]) for n in range(n_items)]
        doff = [doff[n] - x[n] for n in range(n_items)]
        yield
    egc = [jnp.exp(gc[n]) for n in range(n_items)]
    g_last = [gc[n][CHUNK - 1:CHUNK, :] for n in range(n_items)]
    eg_last = [jnp.exp(g_last[n]) for n in range(n_items)]
    sol, kd, wq = [], [], []
    for n in range(n_items):
        rhs = jnp.concatenate([v_[n] * bh[n], kb[n] * egc[n]], axis=-1)
        sol.append(rhs + _dot(doff[n].astype(BF16), rhs.astype(BF16)))
        kd.append((k_[n] * jnp.exp(g_last[n] - gc[n])).astype(BF16))
        wq.append(jnp.concatenate([sol[n][:, B_DV:], q_[n] * egc[n]], axis=0).astype(BF16))
        if n % B_HEADS == B_HEADS - 1:
            yield

    def gate_store(c, o):
        for h in range(B_HEADS):
            z = env['zb'][rs(c), h * B_DV:(h + 1) * B_DV]
            mix_scr[rs(c), A_WIDTH + h * B_DV:A_WIDTH + (h + 1) * B_DV] = (
                _rms(o[h], onorm_ref[...]) * _silu(z)).astype(BF16)

    s_cur = [s_scr[h] for h in range(B_HEADS)]
    o_prev = None
    for c in range(nc):
        idx = [c * B_HEADS + h for h in range(B_HEADS)]
        ws = [_dot(wq[n], s_cur[h].astype(BF16)) for h, n in enumerate(idx)]
        if o_prev is not None:
            gate_store(c - 1, o_prev)
        yield
        v_new = [sol[n][:, :B_DV] - ws[h][:CHUNK] for h, n in enumerate(idx)]
        vb = [v_new[h].astype(BF16) for h in range(B_HEADS)]
        o_prev = [ws[h][CHUNK:] + _dot(qk[n], vb[h]) for h, n in enumerate(idx)]
        s_cur = [s_cur[h] * eg_last[n] + _dotg(kd[n], vb[h], TN_DIMS) for h, n in enumerate(idx)]
        yield
    for h in range(B_HEADS):
        s_scr[h] = s_cur[h]
    gate_store(nc - 1, o_prev)
    yield


def _ab_layer_kernel(x_ref, g_ref, w_ref, wg_ref, wgt_ref, sink_ref, hist_ref, convw_ref, gprm_ref,
                     gprmt_ref, onorm_ref, wout_ref, convin_ref, sin_ref,
                     y_ref, kvtail_ref, xbtail_ref, sout_ref,
                     s_scr, xc_scr, kvprev_scr, mix_scr, *, ct, n_lead, pos_shift):
    i = pl.program_id(1)
    nsteps = pl.num_programs(1)
    ntail = min(WINDOW, ct)

    @pl.when(i == 0)
    def _():
        s_scr[...] = sin_ref[0]
        xc_scr[0:SUBLANES, :] = convin_ref[0]
        kvprev_scr[1] = hist_ref[...]

    x = x_ref[0]
    hb = _rms(x, g_ref[...]).astype(BF16)
    proj = lambda a, b: _dot(hb, w_ref[:, a:b])
    env = {}

    def step(gen, n=1):
        for _ in range(n):
            next(gen, None)

    nblk = B_CONV_CH // LANES
    w = convw_ref[...]
    cvb = []

    xblk = []
    row8 = lax.broadcasted_iota(jnp.int32, (SUBLANES, LANES), 0)

    def conv_block(j):
        cs = slice(j * LANES, (j + 1) * LANES)
        xj = xblk[j]
        h8 = xc_scr[0:SUBLANES, cs]
        yc = xj * w[CONV_W - 1:CONV_W, cs]
        for k in range(1, CONV_W):
            sh = pltpu.roll(xj, k, 0)
            top = jnp.where(row8 < k, pltpu.roll(h8, k, 0), sh[0:SUBLANES])
            yc = yc + jnp.concatenate([top, sh[SUBLANES:]], axis=0) * w[CONV_W - 1 - k:CONV_W - k, cs]
        cvb.append(_silu(yc))
        xc_scr[0:SUBLANES, cs] = xj[ct - SUBLANES:]

    pw = MXU_WIDTH
    for j in range(nblk // 2):
        xbj = proj(O_XB + j * pw, O_XB + (j + 1) * pw)
        xblk += [xbj[:, 0:LANES], xbj[:, LANES:]]
        xbtail_ref[0, :, j * pw:(j + 1) * pw] = xbj[ct - SUBLANES:]
        if j > 0:
            conv_block(2 * j - 2)
            conv_block(2 * j - 1)
    gt = _dot(hb, wg_ref[...])
    gtt = _dotg(wgt_ref[...], hb, NT_DIMS)
    conv_block(nblk - 2)
    conv_block(nblk - 1)

    gdn = _gdn_stages(i, cvb, gt, gtt, env, gprm_ref, gprmt_ref, onorm_ref, s_scr, mix_scr, ct=ct, n_lead=n_lead)
    qa_p = []
    for j in range(A_WIDTH // pw):
        qa_p.append(proj(O_QA + j * pw, O_QA + (j + 1) * pw))
        step(gdn, 2)
    qa = jnp.concatenate(qa_p, axis=-1)
    kv = proj(O_KV, O_ZA)
    kvtail_ref[0] = kv[ct - ntail:]
    slot = lax.rem(i, 2)
    kvb = jnp.concatenate([kvprev_scr[1 - slot], kv], axis=0).astype(BF16)
    if ct >= WINDOW:
        kvprev_scr[slot] = kv[ct - WINDOW:]
    swa = _swa_stages(i, qa, kvb, env, sink_ref, ct=ct, sb=ntail, pos_shift=pos_shift)
    n_sw = (ct // ntail) * A_KV_HEADS
    step(gdn)
    for _ in range(ct // CHUNK):
        step(gdn)
        step(swa)
    step(swa, max(n_sw - ct // CHUNK, 0))
    za_p, zb_p = [], []
    fill = ([lambda j=j: za_p.append(proj(O_ZA + j * pw, O_ZA + (j + 1) * pw)) for j in range(A_WIDTH // pw)]
            + [lambda j=j: zb_p.append(proj(O_ZB + j * pw, O_ZB + (j + 1) * pw)) for j in range(B_VAL // pw)])
    for lv in range(10):
        step(gdn)
        if lv % 2 == 1 and fill:
            fill.pop(0)()
    for f in fill:
        f()
    env['za'] = jnp.concatenate(za_p, axis=-1)
    env['zb'] = jnp.concatenate(zb_p, axis=-1)
    step(gdn, ct // CHUNK)
    for _ in range(2 * (ct // CHUNK)):
        step(gdn)
        step(swa, -(-2 * n_sw // (2 * (ct // CHUNK))))
    for _ in swa:
        pass
    _swa_store(env, mix_scr, ct=ct, sb=ntail)
    ya = x + _dot(mix_scr[:, 0:A_WIDTH], wout_ref[0:A_WIDTH, :])
    for _ in gdn:
        pass
    y_ref[0] = ya + _dot(mix_scr[:, A_WIDTH:], wout_ref[A_WIDTH:, :])

    @pl.when(i == nsteps - 1)
    def _():
        sout_ref[0] = s_scr[...]


def _ab_layer_prompt(x3, norm_g, w_main, w_gate, w_gate_t, sink, hist_kv, conv_w8, gprm, gprm_t, onorm,
                     w_out, conv_in, s_in, ct, n_lead, pos_shift):
    B, L, _ = x3.shape
    ntail = min(WINDOW, ct)
    kern = functools.partial(_ab_layer_kernel, ct=ct, n_lead=n_lead, pos_shift=pos_shift)
    full = lambda a: pl.BlockSpec(a.shape, lambda b, i: (0,) * a.ndim)
    blk = pl.BlockSpec((1, ct, D_MODEL), lambda b, i: (b, i, 0))
    per_b = lambda *s: pl.BlockSpec((1,) + s, lambda b, i: (b,) + (0,) * len(s))
    return pl.pallas_call(
        kern,
        grid=(B, L // ct),
        in_specs=[blk, full(norm_g), full(w_main), full(w_gate), full(w_gate_t), full(sink), full(hist_kv),
                  full(conv_w8), full(gprm), full(gprm_t), full(onorm), full(w_out),
                  full(conv_in), full(s_in)],
        out_specs=[blk, per_b(ntail, 2 * A_KV_WIDTH), per_b(SUBLANES, B_CONV_CH), per_b(B_HEADS, B_DK, B_DV)],
        out_shape=[jax.ShapeDtypeStruct((B, L, D_MODEL), F32),
                   jax.ShapeDtypeStruct((B, ntail, 2 * A_KV_WIDTH), F32),
                   jax.ShapeDtypeStruct((B, SUBLANES, B_CONV_CH), F32),
                   jax.ShapeDtypeStruct((B, B_HEADS, B_DK, B_DV), F32)],
        scratch_shapes=[pltpu.VMEM((B_HEADS, B_DK, B_DV), F32),
                        pltpu.VMEM((SUBLANES, B_CONV_CH), F32),
                        pltpu.VMEM((2, WINDOW, 2 * A_KV_WIDTH), F32),
                        pltpu.VMEM((ct, AB_MIX), BF16)],
        compiler_params=_cparams(("arbitrary", "arbitrary")),
        name="ab_layer_prompt",
    )(x3, norm_g, w_main, w_gate, w_gate_t, sink, hist_kv, conv_w8, gprm, gprm_t, onorm, w_out, conv_in, s_in)


def _c_layer_kernel(x_ref, g_ref, w_ref, wgk_ref, wup_ref, bgk_ref, onorm_ref, wout_ref, gfin_ref,
                    sin_ref, y_ref, sout_ref, st_scr, og_scr, *, ct, n_lead):
    i = pl.program_id(1)
    nsteps = pl.num_programs(1)

    @pl.when(i == 0)
    def _():
        st_scr[...] = sin_ref[0]

    x = x_ref[0]
    hb = _rms(x, g_ref[...]).astype(BF16)
    low = _dot(hb, wgk_ref[...])
    q_all = _dot(hb, w_ref[:, 0:C_KEY])
    la_all = _gla_log_decay(low, wup_ref, bgk_ref)
    k_all = _dot(hb, w_ref[:, C_KEY:2 * C_KEY])
    v_h, z_h = [], []
    later = ([lambda h=h: v_h.append(_dot(hb, w_ref[:, 2 * C_KEY + h * C_DV:2 * C_KEY + (h + 1) * C_DV]).astype(BF16))
              for h in range(C_HEADS)]
             + [lambda h=h: z_h.append(_dot(hb, w_ref[:, 2 * C_KEY + C_VAL + h * C_DV:
                                                       2 * C_KEY + C_VAL + (h + 1) * C_DV]))
                for h in range(C_HEADS)])

    def run_later(n):
        for _ in range(n):
            if later:
                later.pop(0)()
    rows = i * ct + lax.broadcasted_iota(jnp.int32, la_all.shape, 0)
    la_all = jnp.where(rows >= n_lead, la_all, 0.0)

    ii = lax.broadcasted_iota(jnp.int32, (CHUNK, CHUNK), 0)
    jj = lax.broadcasted_iota(jnp.int32, (CHUNK, CHUNK), 1)
    lower = ii >= jj
    ltri = lower.astype(BF16)
    nc = ct // CHUNK
    rs = lambda c: slice(c * CHUNK, (c + 1) * CHUNK)
    ks = lambda h: slice(h * C_DK, (h + 1) * C_DK)
    vs = lambda h: slice(h * C_DV, (h + 1) * C_DV)
    items = [(c, h) for c in range(nc) for h in range(C_HEADS)]
    bc_all = [sum(_dot(ltri, p) for p in _split2(la_all[rs(c)])) for c in range(nc)]
    run_later(2)
    every = max(nc // C_HEADS, 1)
    qd, kinv, kd, gl, qk = [], [], [], [], []
    for n, (c, h) in enumerate(items):
        bc = bc_all[c][:, ks(h)]
        k = k_all[rs(c), ks(h)]
        bl = bc[CHUNK - 1:CHUNK, :]
        qd.append((q_all[rs(c), ks(h)] * (C_DK ** -0.5) * jnp.exp(bc)).astype(BF16))
        kinv.append((k * jnp.exp(-bc)).astype(BF16))
        kd.append((k * jnp.exp(bl - bc)).astype(BF16))
        gl.append(jnp.exp(bl))
        if h == C_HEADS - 1:
            qk += [jnp.where(lower, _dotg(qd[m], kinv[m], NT_DIMS), 0.0).astype(BF16)
                   for m in range(n - C_HEADS + 1, n + 1)]
            if c % every == every - 1:
                run_later(1)
    run_later(C_HEADS - len(v_h))
    v_ = [v_h[h][rs(c)] for c, h in items]
    o_intra = []
    for c in range(nc):
        o_intra += [_dot(qk[c * C_HEADS + h], v_[c * C_HEADS + h]) for h in range(C_HEADS)]
        if c % every == every - 1:
            run_later(1)
    run_later(len(later))

    rb = min(ct, 4 * CHUNK)

    def gate_store(c, o):
        for h in range(C_HEADS):
            og_scr[rs(c), vs(h)] = (_rms(o[h], onorm_ref[...]) * _silu(z_h[h][rs(c)])).astype(BF16)
        if ((c + 1) * CHUNK) % rb == 0:
            r = slice((c + 1) * CHUNK - rb, (c + 1) * CHUNK)
            y_ref[0, r, :] = _rms(x[r] + _dot(og_scr[r, :], wout_ref[...]), gfin_ref[...])

    st = [st_scr[h] for h in range(C_HEADS)]
    o_prev = None
    pad_rows = jnp.zeros((SUBLANES - C_HEADS, C_DK), F32)
    for c in range(nc):
        idx = [c * C_HEADS + h for h in range(C_HEADS)]
        glt = jnp.concatenate([gl[n] for n in idx] + [pad_rows], axis=0).T
        o = [o_intra[n] + _dot(qd[n], st[h].astype(BF16)) for h, n in enumerate(idx)]
        st = [st[h] * glt[:, h:h + 1] + _dotg(kd[n], v_[n], TN_DIMS) for h, n in enumerate(idx)]
        if o_prev is not None:
            gate_store(c - 1, o_prev)
        o_prev = o
    gate_store(nc - 1, o_prev)
    for h in range(C_HEADS):
        st_scr[h] = st[h]

    @pl.when(i == nsteps - 1)
    def _():
        sout_ref[0] = st_scr[...]


def _c_layer_prompt(x3, norm_g, w_main, w_gk, w_up, b_gk, onorm, w_out, final_g, st_in, ct, n_lead):
    B, L, _ = x3.shape
    kern = functools.partial(_c_layer_kernel, ct=ct, n_lead=n_lead)
    full = lambda a: pl.BlockSpec(a.shape, lambda b, i: (0,) * a.ndim)
    blk = pl.BlockSpec((1, ct, D_MODEL), lambda b, i: (b, i, 0))
    st = pl.BlockSpec((1, C_HEADS, C_DK, C_DV), lambda b, i: (b, 0, 0, 0))
    return pl.pallas_call(
        kern,
        grid=(B, L // ct),
        in_specs=[blk, full(norm_g), full(w_main), full(w_gk), full(w_up), full(b_gk), full(onorm),
                  full(w_out), full(final_g), full(st_in)],
        out_specs=[blk, st],
        out_shape=[jax.ShapeDtypeStruct((B, L, D_MODEL), F32),
                   jax.ShapeDtypeStruct((B, C_HEADS, C_DK, C_DV), F32)],
        scratch_shapes=[pltpu.VMEM((C_HEADS, C_DK, C_DV), F32), pltpu.VMEM((ct, C_VAL), BF16)],
        compiler_params=_cparams(("arbitrary", "arbitrary")),
        name="c_layer_prompt",
    )(x3, norm_g, w_main, w_gk, w_up, b_gk, onorm, w_out, final_g, st_in)


def _ab_proj_kernel(x_ref, g_ref, w_ref, wg_ref, qa_ref, kv_ref, za_ref, xb_ref, zb_ref, gates_ref):
    hb = _rms(x_ref[...], g_ref[...]).astype(BF16)
    qa_ref[...] = _dot(hb, w_ref[:, O_QA:O_KV])
    kv_ref[...] = _dot(hb, w_ref[:, O_KV:O_ZA])
    za_ref[...] = _dot(hb, w_ref[:, O_ZA:O_XB])
    xb_ref[...] = _dot(hb, w_ref[:, O_XB:O_ZB])
    zb_ref[...] = _dot(hb, w_ref[:, O_ZB:AB_MAIN])
    gates_ref[...] = _dot(hb, wg_ref[...])


def _ab_proj(x2d, norm_g, w_main, w_gate):
    n = x2d.shape[0]
    full = lambda a: pl.BlockSpec(a.shape, lambda: (0,) * a.ndim)
    widths = (A_WIDTH, 2 * A_KV_WIDTH, A_WIDTH, B_CONV_CH, B_VAL, LANES)
    args = (x2d, norm_g, w_main, w_gate)
    return pl.pallas_call(
        _ab_proj_kernel,
        in_specs=[full(a) for a in args],
        out_specs=[pl.BlockSpec((n, w), lambda: (0, 0)) for w in widths],
        out_shape=[jax.ShapeDtypeStruct((n, w), F32) for w in widths],
        compiler_params=pltpu.CompilerParams(vmem_limit_bytes=VMEM_LIMIT),
        name="ab_in_proj_sample",
    )(*args)


def _swa_sample_kernel(q_ref, kvn_ref, za_ref, ck_ref, cv_ref, sink_ref, o_ref, nk_ref, nv_ref, *, bt):
    row = lax.broadcasted_iota(jnp.int32, (WINDOW, A_KV_WIDTH), 0)
    hrow = lax.broadcasted_iota(jnp.int32, (A_HEADS, A_HD), 0)
    sk = sink_ref[...]
    nkb, nvb = [], []
    for b in range(bt):
        nk = jnp.where(row == WINDOW - 1, kvn_ref[b:b + 1, 0:A_KV_WIDTH], pltpu.roll(ck_ref[b], WINDOW - 1, 0))
        nv = jnp.where(row == WINDOW - 1, kvn_ref[b:b + 1, A_KV_WIDTH:], pltpu.roll(cv_ref[b], WINDOW - 1, 0))
        nk_ref[b] = nk
        nv_ref[b] = nv
        nkb.append(nk.astype(BF16))
        nvb.append(nv.astype(BF16))
    items = [(b, j) for b in range(bt) for j in range(A_KV_HEADS)]
    q8 = [q_ref[b].astype(BF16) for b in range(bt)]
    sc = [_dotg(q8[b], nkb[b][:, j * A_HD:(j + 1) * A_HD], NT_DIMS) * (A_HD ** -0.5) for b, j in items]
    m = [jnp.maximum(jnp.max(s, axis=-1, keepdims=True), sk) for s in sc]
    p = [jnp.exp(sc[n] - m[n]) for n in range(len(items))]
    den = [jnp.sum(p[n], axis=-1, keepdims=True) + jnp.exp(sk - m[n]) for n in range(len(items))]
    oj = [_dot(p[n].astype(BF16), nvb[b][:, j * A_HD:(j + 1) * A_HD]) / den[n] for n, (b, j) in enumerate(items)]
    for b in range(bt):
        o8 = jnp.where(hrow >= A_GROUP, oj[b * A_KV_HEADS + 1], oj[b * A_KV_HEADS])
        o_ref[b] = o8 * _silu(za_ref[b])


def _swa_sample(q3, kv_new, za3, cache_k, cache_v, sink_col, bt):
    nb = q3.shape[0]
    hd = pl.BlockSpec((bt, A_HEADS, A_HD), lambda i: (i, 0, 0))
    cache = pl.BlockSpec((bt, WINDOW, A_KV_WIDTH), lambda i: (i, 0, 0))
    return pl.pallas_call(
        functools.partial(_swa_sample_kernel, bt=bt),
        grid=(nb // bt,),
        in_specs=[hd, pl.BlockSpec((bt, 2 * A_KV_WIDTH), lambda i: (i, 0)), hd, cache, cache,
                  pl.BlockSpec(sink_col.shape, lambda i: (0, 0))],
        out_specs=[hd, cache, cache],
        out_shape=[jax.ShapeDtypeStruct(q3.shape, F32),
                   jax.ShapeDtypeStruct(cache_k.shape, F32),
                   jax.ShapeDtypeStruct(cache_v.shape, F32)],
        compiler_params=_cparams(("arbitrary",)),
        name="swa_sample",
    )(q3, kv_new, za3, cache_k, cache_v, sink_col)


def _gdn_sample_prep_kernel(xb_ref, h0_ref, h1_ref, h2_ref, gates_ref, convw_ref, gprm_ref,
                            q_ref, k_ref, v_ref, bg_ref):
    w = convw_ref[...]
    y = (h0_ref[...] * w[0:1, :] + h1_ref[...] * w[1:2, :] + h2_ref[...] * w[2:3, :]
         + xb_ref[...] * w[3:4, :])
    cv = _silu(y)
    for h in range(B_HEADS):
        q = cv[:, h * B_DK:(h + 1) * B_DK]
        k = cv[:, B_KEY + h * B_DK:B_KEY + (h + 1) * B_DK]
        q_ref[:, h * B_DK:(h + 1) * B_DK] = (
            q * lax.rsqrt(jnp.sum(q * q, axis=-1, keepdims=True) + EPS) * (B_DK ** -0.5))
        k_ref[:, h * B_DK:(h + 1) * B_DK] = k * lax.rsqrt(jnp.sum(k * k, axis=-1, keepdims=True) + EPS)
    v_ref[...] = cv[:, 2 * B_KEY:]
    gt = gates_ref[...]
    col = lax.broadcasted_iota(jnp.int32, gt.shape, 1)
    g = -gprm_ref[0:1, :] * _softplus(gt + gprm_ref[1:2, :])
    bg_ref[...] = jnp.where(col < B_HEADS, _sigmoid(gt), jnp.exp(g))


def _gdn_sample_prep(xb, h0, h1, h2, gates, conv_w8, gprm):
    n = xb.shape[0]
    full = lambda a: pl.BlockSpec(a.shape, lambda: (0,) * a.ndim)
    args = (xb, h0, h1, h2, gates, conv_w8, gprm)
    return pl.pallas_call(
        _gdn_sample_prep_kernel,
        in_specs=[full(a) for a in args],
        out_specs=[pl.BlockSpec((n, B_KEY), lambda: (0, 0))] * 3 + [pl.BlockSpec((n, LANES), lambda: (0, 0))],
        out_shape=[jax.ShapeDtypeStruct((n, B_KEY), F32)] * 3 + [jax.ShapeDtypeStruct((n, LANES), F32)],
        compiler_params=pltpu.CompilerParams(vmem_limit_bytes=VMEM_LIMIT),
        name="gdn_sample_prep",
    )(*args)


def _gdn_sample_kernel(s_ref, qt_ref, kt_ref, v_ref, bg_ref, zb_ref, onorm_ref, o_ref, sout_ref, *, bt):
    items = [(b, h) for b in range(bt) for h in range(B_HEADS)]
    hs = lambda h: slice(h * B_DV, (h + 1) * B_DV)
    kt = [kt_ref[:, hs(h)].T for h in range(B_HEADS)]
    qt = [qt_ref[:, hs(h)].T for h in range(B_HEADS)]
    kc = [kt[h][:, b:b + 1] for b, h in items]
    eg = [bg_ref[b:b + 1, B_HEADS + h:B_HEADS + h + 1] for b, h in items]
    ks = [jnp.sum(kc[n] * s_ref[b, h], axis=0, keepdims=True) for n, (b, h) in enumerate(items)]
    v_new = [bg_ref[b:b + 1, h:h + 1] * (v_ref[b:b + 1, hs(h)] - eg[n] * ks[n]) for n, (b, h) in enumerate(items)]
    o = []
    for n, (b, h) in enumerate(items):
        s_new = eg[n] * s_ref[b, h] + kc[n] * v_new[n]
        sout_ref[b, h] = s_new
        o.append(jnp.sum(qt[h][:, b:b + 1] * s_new, axis=0, keepdims=True))
    o_blk = jnp.concatenate([jnp.concatenate([o[b * B_HEADS + h] for h in range(B_HEADS)], axis=-1)
                             for b in range(bt)], axis=0)
    for h in range(B_HEADS):
        o_ref[:, hs(h)] = _rms(o_blk[:, hs(h)], onorm_ref[...]) * _silu(zb_ref[:, hs(h)])


def _gdn_sample(state, qt, kt, v, bg, zb, onorm, bt):
    nb = state.shape[0]
    st = pl.BlockSpec((bt, B_HEADS, B_DK, B_DV), lambda i: (i, 0, 0, 0))
    row = lambda n: pl.BlockSpec((bt, n), lambda i: (i, 0))
    return pl.pallas_call(
        functools.partial(_gdn_sample_kernel, bt=bt),
        grid=(nb // bt,),
        in_specs=[st, row(B_KEY), row(B_KEY), row(B_VAL), row(LANES), row(B_VAL),
                  pl.BlockSpec(onorm.shape, lambda i: (0, 0))],
        out_specs=[row(B_VAL), st],
        out_shape=[jax.ShapeDtypeStruct((nb, B_VAL), F32), jax.ShapeDtypeStruct(state.shape, F32)],
        compiler_params=_cparams(("arbitrary",)),
        name="gdn_sample",
    )(state, qt, kt, v, bg, zb, onorm)


def _ab_out_c_proj_kernel(x_ref, oa_ref, ob_ref, wout_ref, g_ref, w_ref, wgk_ref, wup_ref, bgk_ref,
                          y_ref, q_ref, k_ref, v_ref, z_ref, la_ref):
    y = (x_ref[...] + _dot(oa_ref[...].astype(BF16), wout_ref[0:A_WIDTH, :])
         + _dot(ob_ref[...].astype(BF16), wout_ref[A_WIDTH:, :]))
    y_ref[...] = y
    hb = _rms(y, g_ref[...]).astype(BF16)
    q_ref[...] = _dot(hb, w_ref[:, 0:C_KEY])
    k_ref[...] = _dot(hb, w_ref[:, C_KEY:2 * C_KEY])
    v_ref[...] = _dot(hb, w_ref[:, 2 * C_KEY:2 * C_KEY + C_VAL])
    z_ref[...] = _dot(hb, w_ref[:, 2 * C_KEY + C_VAL:C_MAIN])
    la_ref[...] = _gla_log_decay(_dot(hb, wgk_ref[...]), wup_ref, bgk_ref)


def _ab_out_c_proj(x2d, oa, ob, w_out, norm_g, w_main, w_gk, w_up, b_gk):
    n = x2d.shape[0]
    full = lambda a: pl.BlockSpec(a.shape, lambda: (0,) * a.ndim)
    widths = (D_MODEL, C_KEY, C_KEY, C_VAL, C_VAL, C_KEY)
    args = (x2d, oa, ob, w_out, norm_g, w_main, w_gk, w_up, b_gk)
    return pl.pallas_call(
        _ab_out_c_proj_kernel,
        in_specs=[full(a) for a in args],
        out_specs=[pl.BlockSpec((n, w), lambda: (0, 0)) for w in widths],
        out_shape=[jax.ShapeDtypeStruct((n, w), F32) for w in widths],
        compiler_params=pltpu.CompilerParams(vmem_limit_bytes=VMEM_LIMIT),
        name="ab_out_c_in_proj_sample",
    )(*args)


def _gla_sample_kernel(s_ref, qt_ref, kt_ref, lat_ref, v_ref, z_ref, onorm_ref, o_ref, sout_ref, *, bt):
    items = [(b, h) for b in range(bt) for h in range(C_HEADS)]
    hs = lambda h: slice(h * C_DV, (h + 1) * C_DV)
    ks = lambda h: slice(h * C_DK, (h + 1) * C_DK)
    at = [jnp.exp(lat_ref[:, ks(h)]).T for h in range(C_HEADS)]
    kt = [kt_ref[:, ks(h)].T for h in range(C_HEADS)]
    qt = [(qt_ref[:, ks(h)] * (C_DK ** -0.5)).T for h in range(C_HEADS)]
    o = []
    for n, (b, h) in enumerate(items):
        s_new = at[h][:, b:b + 1] * s_ref[b, h] + kt[h][:, b:b + 1] * v_ref[b:b + 1, hs(h)]
        sout_ref[b, h] = s_new
        o.append(jnp.sum(qt[h][:, b:b + 1] * s_new, axis=0, keepdims=True))
    o_blk = jnp.concatenate([jnp.concatenate([o[b * C_HEADS + h] for h in range(C_HEADS)], axis=-1)
                             for b in range(bt)], axis=0)
    for h in range(C_HEADS):
        o_ref[:, hs(h)] = _rms(o_blk[:, hs(h)], onorm_ref[...]) * _silu(z_ref[:, hs(h)])


def _gla_sample(state, qt, kt, lat, v, z, onorm, bt):
    nb = state.shape[0]
    st = pl.BlockSpec((bt, C_HEADS, C_DK, C_DV), lambda i: (i, 0, 0, 0))
    row = lambda n: pl.BlockSpec((bt, n), lambda i: (i, 0))
    return pl.pallas_call(
        functools.partial(_gla_sample_kernel, bt=bt),
        grid=(nb // bt,),
        in_specs=[st, row(C_KEY), row(C_KEY), row(C_KEY), row(C_VAL), row(C_VAL),
                  pl.BlockSpec(onorm.shape, lambda i: (0, 0))],
        out_specs=[row(C_VAL), st],
        out_shape=[jax.ShapeDtypeStruct((nb, C_VAL), F32), jax.ShapeDtypeStruct(state.shape, F32)],
        compiler_params=_cparams(("arbitrary",)),
        name="gla_sample",
    )(state, qt, kt, lat, v, z, onorm)


def _c_out_kernel(x_ref, o_ref, w_ref, g_ref, y_ref):
    y_ref[...] = _rms(x_ref[...] + _dot(o_ref[...].astype(BF16), w_ref[...]), g_ref[...])


def _c_out(x2d, oc, w_out, final_g):
    n = x2d.shape[0]
    full = lambda a: pl.BlockSpec(a.shape, lambda: (0,) * a.ndim)
    args = (x2d, oc, w_out, final_g)
    return pl.pallas_call(
        _c_out_kernel,
        in_specs=[full(a) for a in args],
        out_specs=pl.BlockSpec((n, D_MODEL), lambda: (0, 0)),
        out_shape=jax.ShapeDtypeStruct((n, D_MODEL), F32),
        compiler_params=pltpu.CompilerParams(vmem_limit_bytes=VMEM_LIMIT),
        name="c_out_proj_norm_sample",
    )(*args)


def kernel(x_prompt, x_sample, cache_swa_k, cache_swa_v, state_dn_conv, state_dn, state_gla,
           meta_tokens, norm_ab, w_in_ab, sink_a, conv_b, a_log_b, dt_bias_b, onorm_b, w_out_ab,
           norm_c, w_in_c, w_gk_up, b_gk, onorm_c, w_out_c, final_norm):
    Bp, L, _ = x_prompt.shape
    Bs = x_sample.shape[0]
    assert L % PROMPT_TILE == 0 and Bs % SAMPLE_BLOCK == 0

    w_ab = w_in_ab[0]
    w_ab_main = w_ab.astype(BF16)
    w_ab_gate = jnp.pad(w_ab[:, AB_MAIN:], ((0, 0), (0, LANES - 2 * B_HEADS))).astype(BF16)
    w_ab_gate_t = jnp.pad(w_ab[:, AB_MAIN:].T, ((0, BF16_ROWS - 2 * B_HEADS), (0, 0))).astype(BF16)
    g_ab = norm_ab[0][None, :]
    sink_row = sink_a[0][None, :]
    sink_col = sink_a[0][:, None]
    conv_w8 = jnp.pad(conv_b[0], ((0, SUBLANES - CONV_W), (0, 0)))
    gp = jnp.stack([jnp.exp(a_log_b[0]), dt_bias_b[0]])
    gprm = jnp.pad(gp, ((0, SUBLANES - 2), (B_HEADS, LANES - 2 * B_HEADS)))
    gprm_t = jnp.pad(gp.T, ((B_HEADS, BF16_ROWS - 2 * B_HEADS), (0, LANES - 2)))
    onb = onorm_b[0][None, :]
    w_ab_out = w_out_ab[0].astype(BF16)
    w_c = w_in_c[0]
    w_c_main = w_c.astype(BF16)
    w_c_gk = jnp.pad(w_c[:, C_MAIN:], ((0, 0), (0, LANES - C_RANK))).astype(BF16)
    w_up = jnp.pad(w_gk_up[0], ((0, LANES - C_RANK), (0, 0)))
    bgk = b_gk[0][None, :]
    g_c = norm_c[0][None, :]
    onc = onorm_c[0][None, :]
    w_c_out = w_out_c[0].astype(BF16)
    g_fin = final_norm[None, :]

    def ab_layer(x3, ct, hist_kv, pos_shift, conv_in, s_in, n_lead):
        return _ab_layer_prompt(x3, g_ab, w_ab_main, w_ab_gate, w_ab_gate_t, sink_row, hist_kv, conv_w8,
                                gprm, gprm_t, onb, w_ab_out, conv_in, s_in, ct, n_lead, pos_shift)

    def c_layer(x3, ct, st_in, n_lead):
        return _c_layer_prompt(x3, g_c, w_c_main, w_c_gk, w_up, bgk, onc, w_c_out, g_fin, st_in, ct, n_lead)

    xpre = jnp.concatenate([jnp.zeros((LEAD, D_MODEL), F32), meta_tokens], axis=0)[None]
    y_pre, kv_pre, xbtail_pre, sdn_pre = ab_layer(
        xpre, CHUNK, jnp.zeros((WINDOW, 2 * A_KV_WIDTH), F32), -LEAD,
        jnp.zeros((1, SUBLANES, B_CONV_CH), F32), jnp.zeros((1, B_HEADS, B_DK, B_DV), F32), LEAD)
    _, sgla_pre = c_layer(y_pre, CHUNK, jnp.zeros((1, C_HEADS, C_DK, C_DV), F32), LEAD)

    hist_kv = jnp.concatenate([jnp.zeros((WINDOW - CHUNK, 2 * A_KV_WIDTH), F32), kv_pre[0]], axis=0)
    y1, kv_tail, xb_tail, sdn_p = ab_layer(
        x_prompt, PROMPT_TILE, hist_kv, N_META, xbtail_pre, sdn_pre, 0)
    y_prompt, sgla_p = c_layer(y1, PROMPT_TILE, sgla_pre, 0)

    swa_k_p = kv_tail[:, :, :A_KV_WIDTH].reshape(1, Bp, WINDOW, A_KV_HEADS, A_HD)
    swa_v_p = kv_tail[:, :, A_KV_WIDTH:].reshape(1, Bp, WINDOW, A_KV_HEADS, A_HD)
    dn_conv_p = xb_tail[:, SUBLANES - (CONV_W - 1):, :][None]

    xs = x_sample.reshape(Bs, D_MODEL)
    qa, kvn, za, xb, zb, gates = _ab_proj(xs, g_ab, w_ab_main, w_ab_gate)
    oa3, nk, nv = _swa_sample(qa.reshape(Bs, A_HEADS, A_HD), kvn, za.reshape(Bs, A_HEADS, A_HD),
                              cache_swa_k[0].reshape(Bs, WINDOW, A_KV_WIDTH),
                              cache_swa_v[0].reshape(Bs, WINDOW, A_KV_WIDTH), sink_col, SAMPLE_BLOCK)
    hist = state_dn_conv[0]
    qn, kn, vn, bg = _gdn_sample_prep(xb, hist[:, 0], hist[:, 1], hist[:, 2], gates, conv_w8, gprm)
    ob, sdn_s = _gdn_sample(state_dn[0], qn, kn, vn, bg, zb, onb, SAMPLE_BLOCK)
    ys, qc, kc, vc, zc, la = _ab_out_c_proj(xs, oa3.reshape(Bs, A_WIDTH), ob, w_ab_out,
                                            g_c, w_c_main, w_c_gk, w_up, bgk)
    ocs, sgla_s = _gla_sample(state_gla[0], qc, kc, la, vc, zc, onc, SAMPLE_BLOCK)
    y_sample = _c_out(ys, ocs, w_c_out, g_fin).reshape(Bs, 1, D_MODEL)
    dn_conv_s = jnp.concatenate([hist[:, 1:], xb[:, None, :]], axis=1)[None]

    return (y_prompt, y_sample, swa_k_p, swa_v_p, dn_conv_p, sdn_p[None], sgla_p[None],
            nk.reshape(1, Bs, WINDOW, A_KV_HEADS, A_HD), nv.reshape(1, Bs, WINDOW, A_KV_HEADS, A_HD),
            dn_conv_s, sdn_s[None], sgla_s[None])
```

```python
import functools

import jax
import jax.numpy as jnp
from jax import lax
from jax.experimental import pallas as pl
from jax.experimental.pallas import tpu as pltpu

F32 = jnp.float32
BF16 = jnp.bfloat16
EPS = 1e-6

D_MODEL = 1024
N_META = 16
CHUNK = 64
LEAD = (-N_META) % CHUNK
A_HEADS, A_KV_HEADS, A_HD = 8, 2, 64
A_GROUP = A_HEADS // A_KV_HEADS
A_WIDTH = A_HEADS * A_HD
A_KV_WIDTH = A_KV_HEADS * A_HD
WINDOW = 128
B_HEADS, B_DK, B_DV = 4, 128, 128
B_KEY = B_HEADS * B_DK
B_VAL = B_HEADS * B_DV
CONV_W = 4
B_CONV_CH = 2 * B_KEY + B_VAL
C_HEADS, C_DK, C_DV = 4, 128, 256
C_KEY = C_HEADS * C_DK
C_VAL = C_HEADS * C_DV
C_RANK = 16
C_GATE_NORM = 16.0
AB_MAIN = 2 * A_WIDTH + 2 * A_KV_WIDTH + B_CONV_CH + B_VAL
AB_MIX = A_WIDTH + B_VAL
C_MAIN = 2 * C_KEY + 2 * C_VAL
O_QA, O_KV, O_ZA, O_XB, O_ZB = 0, 512, 768, 1280, 2816

LANES = 128
SUBLANES = 8
BF16_ROWS = 16
MXU_WIDTH = 2 * LANES
VMEM_LIMIT = 56 * 1024 * 1024

PROMPT_TILE = 512
SAMPLE_BLOCK = 16
NEG = -1e30

NT_DIMS = (((1,), (1,)), ((), ()))
TN_DIMS = (((0,), (0,)), ((), ()))


def _cparams(sem):
    return pltpu.CompilerParams(dimension_semantics=sem, vmem_limit_bytes=VMEM_LIMIT)


def _dot(a, b):
    return jnp.dot(a, b, preferred_element_type=F32)


def _dotg(a, b, dims):
    return lax.dot_general(a, b, dims, preferred_element_type=F32)


def _split3(a):
    hi = a.astype(BF16)
    r = a - hi.astype(F32)
    mid = r.astype(BF16)
    lo = (r - mid.astype(F32)).astype(BF16)
    return hi, mid, lo


def _split2(a):
    hi = a.astype(BF16)
    return hi, (a - hi.astype(F32)).astype(BF16)


def _sigmoid(x):
    return 1.0 / (1.0 + jnp.exp(-x))


def _silu(x):
    return x * _sigmoid(x)


def _softplus(x):
    return jnp.maximum(x, 0.0) + jnp.log1p(jnp.exp(-jnp.abs(x)))


def _log_sigmoid(x):
    return jnp.minimum(x, 0.0) - jnp.log1p(jnp.exp(-jnp.abs(x)))


def _rms(x, g):
    return x * lax.rsqrt(jnp.mean(x * x, axis=-1, keepdims=True) + EPS) * g


def _gla_log_decay(low, wup_ref, bgk_ref):
    lh, lm, _ = _split3(low)
    wh, wm, _ = _split3(wup_ref[...])
    up = _dot(lh, wh) + _dot(lh, wm) + _dot(lm, wh)
    return _log_sigmoid(up + bgk_ref[...]) * (1.0 / C_GATE_NORM)


def _swa_stages(i, qa, kvb, env, sink_ref, *, ct, sb, pos_shift):
    r = lax.broadcasted_iota(jnp.int32, (sb, WINDOW + sb), 0)
    c = lax.broadcasted_iota(jnp.int32, (sb, WINDOW + sb), 1)
    diff = r - c + WINDOW
    in_window = (diff >= 0) & (diff < WINDOW)
    nsb = ct // sb
    items = [(s, j) for s in range(nsb) for j in range(A_KV_HEADS)]
    masks = [jnp.concatenate([in_window & (i * ct + s * sb + c - WINDOW + pos_shift >= 0)] * A_GROUP, axis=0)
             for s in range(nsb)]
    sinks = [jnp.concatenate([jnp.broadcast_to(sink_ref[0:1, j * A_GROUP + g:j * A_GROUP + g + 1], (sb, 1))
                              for g in range(A_GROUP)], axis=0) for j in range(A_KV_HEADS)]
    sc = []
    for s, j in items:
        qs = (jnp.concatenate([qa[s * sb:(s + 1) * sb, (j * A_GROUP + g) * A_HD:(j * A_GROUP + g + 1) * A_HD]
                               for g in range(A_GROUP)], axis=0) * (A_HD ** -0.5)).astype(BF16)
        sc.append(_dotg(qs, kvb[s * sb:s * sb + WINDOW + sb, j * A_HD:(j + 1) * A_HD], NT_DIMS))
        yield
    p, esk = [], []
    for n, (s, j) in enumerate(items):
        scm = jnp.where(masks[s], sc[n], NEG)
        mx = jnp.maximum(jnp.max(scm, axis=-1, keepdims=True), sinks[j])
        p.append(jnp.exp(scm - mx).astype(BF16))
        esk.append(jnp.exp(sinks[j] - mx))
        yield
    pv = []
    ones = jnp.ones((WINDOW + sb, A_HD), BF16)
    for n, (s, j) in enumerate(items):
        vx = jnp.concatenate([kvb[s * sb:s * sb + WINDOW + sb,
                                  A_KV_WIDTH + j * A_HD:A_KV_WIDTH + (j + 1) * A_HD], ones], axis=-1)
        pvx = _dot(p[n], vx)
        pv.append(pvx[:, :A_HD] / (pvx[:, A_HD:] + esk[n]))
        yield
    env['pv'] = pv


def _swa_store(env, mix_scr, *, ct, sb):
    pv = env['pv']
    for s in range(ct // sb):
        o = jnp.concatenate([pv[s * A_KV_HEADS + j][g * sb:(g + 1) * sb]
                             for j in range(A_KV_HEADS) for g in range(A_GROUP)], axis=-1)
        mix_scr[s * sb:(s + 1) * sb, 0:A_WIDTH] = (o * _silu(env['za'][s * sb:(s + 1) * sb])).astype(BF16)


def _gdn_stages(i, cvb, gt, gtt, env, gprm_ref, gprmt_ref, onorm_ref, s_scr, mix_scr, *, ct, n_lead):
    nc = ct // CHUNK
    items = [(c, h) for c in range(nc) for h in range(B_HEADS)]
    n_items = len(items)
    rs = lambda c: slice(c * CHUNK, (c + 1) * CHUNK)
    qn, kn = [], []
    for h in range(B_HEADS):
        q = cvb[h]
        k = cvb[B_HEADS + h]
        qn.append(q * lax.rsqrt(jnp.sum(q * q, axis=-1, keepdims=True) + EPS) * (B_DK ** -0.5))
        kn.append(k * lax.rsqrt(jnp.sum(k * k, axis=-1, keepdims=True) + EPS))
        yield
    beta = _sigmoid(gt)
    rows = i * ct + lax.broadcasted_iota(jnp.int32, gt.shape, 0)
    gcol = jnp.where(rows >= n_lead, -gprm_ref[0:1, :] * _softplus(gt + gprm_ref[1:2, :]), 0.0)
    lanes = i * ct + lax.broadcasted_iota(jnp.int32, gtt.shape, 1)
    grow = jnp.where(lanes >= n_lead, -gprmt_ref[:, 0:1] * _softplus(gtt + gprmt_ref[:, 1:2]), 0.0)
    ii = lax.broadcasted_iota(jnp.int32, (CHUNK, CHUNK), 0)
    jj = lax.broadcasted_iota(jnp.int32, (CHUNK, CHUNK), 1)
    lower = ii >= jj
    strict = ii > jj
    ltri = lower.astype(BF16)
    utri = (ii <= jj).astype(BF16)
    gcum_c = [sum(_dot(ltri, p) for p in _split3(gcol[rs(c)])) for c in range(nc)]
    gcum_r = [sum(_dot(p, utri) for p in _split3(grow[:, rs(c)])) for c in range(nc)]
    yield
    q_ = [qn[h][rs(c)] for c, h in items]
    k_ = [kn[h][rs(c)] for c, h in items]
    v_ = [cvb[2 * B_HEADS + h][rs(c)] for c, h in items]
    bh = [beta[rs(c), h:h + 1] for c, h in items]
    gc = [gcum_c[c][:, B_HEADS + h:B_HEADS + h + 1] for c, h in items]
    gr = [gcum_r[c][B_HEADS + h:B_HEADS + h + 1, :] for c, h in items]
    kb, a, qk = [], [], []
    for n in range(n_items):
        decay = jnp.exp(jnp.where(lower, gc[n] - gr[n], NEG))
        kb.append(k_[n] * bh[n])
        kq = _dotg(jnp.concatenate([kb[n], q_[n]], axis=0).astype(BF16), k_[n].astype(BF16), NT_DIMS)
        a.append(jnp.where(strict, kq[:CHUNK] * decay, 0.0))
        qk.append(jnp.where(lower, kq[CHUNK:] * decay, 0.0).astype(BF16))
        if n % B_HEADS == B_HEADS - 1:
            yield
    doff = [-jnp.where((ii >> 1) == (jj >> 1), a[n], 0.0) for n in range(n_items)]
    for lg in range(1, 6):
        m = ((ii >> (lg + 1)) == (jj >> (lg + 1))) & ((ii >> lg) != (jj >> lg))
        bm = [jnp.where(m, a[n], 0.0) for n in range(n_items)]
        db = [doff[n].astype(BF16) for n in range(n_items)]
        y = [bm[n] + _dot(db[n], bm[n].astype(BF16)) for n in range(n_items)]
        yield
        x = [y[n] + _dot(y[n].astype(BF16), db[n]) for n in range(n_items)]
        doff = [doff[n] - x[n] for n in range(n_items)]
        yield
    egc = [jnp.exp(gc[n]) for n in range(n_items)]
    g_last = [gc[n][CHUNK - 1:CHUNK, :] for n in range(n_items)]
    eg_last = [jnp.exp(g_last[n]) for n in range(n_items)]
    sol, kd, wq = [], [], []
    for n in range(n_items):
        rhs = jnp.concatenate([v_[n] * bh[n], kb[n] * egc[n]], axis=-1)
        sol.append(rhs + _dot(doff[n].astype(BF16), rhs.astype(BF16)))
        kd.append((k_[n] * jnp.exp(g_last[n] - gc[n])).astype(BF16))
        wq.append(jnp.concatenate([sol[n][:, B_DV:], q_[n] * egc[n]], axis=0).astype(BF16))
        if n % B_HEADS == B_HEADS - 1:
            yield

    def gate_store(c, o):
        for h in range(B_HEADS):
            z = env['zb'][rs(c), h * B_DV:(h + 1) * B_DV]
            mix_scr[rs(c), A_WIDTH + h * B_DV:A_WIDTH + (h + 1) * B_DV] = (
                _rms(o[h], onorm_ref[...]) * _silu(z)).astype(BF16)

    s_cur = [s_scr[h] for h in range(B_HEADS)]
    o_prev = None
    for c in range(nc):
        idx = [c * B_HEADS + h for h in range(B_HEADS)]
        ws = [_dot(wq[n], s_cur[h].astype(BF16)) for h, n in enumerate(idx)]
        if o_prev is not None:
            gate_store(c - 1, o_prev)
        yield
        v_new = [sol[n][:, :B_DV] - ws[h][:CHUNK] for h, n in enumerate(idx)]
        vb = [v_new[h].astype(BF16) for h in range(B_HEADS)]
        o_prev = [ws[h][CHUNK:] + _dot(qk[n], vb[h]) for h, n in enumerate(idx)]
        s_cur = [s_cur[h] * eg_last[n] + _dotg(kd[n], vb[h], TN_DIMS) for h, n in enumerate(idx)]
        yield
    for h in range(B_HEADS):
        s_scr[h] = s_cur[h]
    gate_store(nc - 1, o_prev)
    yield


def _ab_layer_kernel(x_ref, g_ref, w_ref, wg_ref, wgt_ref, sink_ref, hist_ref, convw_ref, gprm_ref,
                     gprmt_ref, onorm_ref, wout_ref, convin_ref, sin_ref,
                     y_ref, kvtail_ref, xbtail_ref, sout_ref,
                     s_scr, xc_scr, kvprev_scr, mix_scr, *, ct, n_lead, pos_shift):
    i = pl.program_id(1)
    nsteps = pl.num_programs(1)
    ntail = min(WINDOW, ct)

    @pl.when(i == 0)
    def _():
        s_scr[...] = sin_ref[0]
        xc_scr[0:SUBLANES, :] = convin_ref[0]
        kvprev_scr[1] = hist_ref[...]

    x = x_ref[0]
    hb = _rms(x, g_ref[...]).astype(BF16)
    proj = lambda a, b: _dot(hb, w_ref[:, a:b])
    env = {}

    def step(gen, n=1):
        for _ in range(n):
            next(gen, None)

    nblk = B_CONV_CH // LANES
    w = convw_ref[...]
    cvb = []

    xblk = []
    row8 = lax.broadcasted_iota(jnp.int32, (SUBLANES, LANES), 0)

    def conv_block(j):
        cs = slice(j * LANES, (j + 1) * LANES)
        xj = xblk[j]
        h8 = xc_scr[0:SUBLANES, cs]
        yc = xj * w[CONV_W - 1:CONV_W, cs]
        for k in range(1, CONV_W):
            sh = pltpu.roll(xj, k, 0)
            top = jnp.where(row8 < k, pltpu.roll(h8, k, 0), sh[0:SUBLANES])
            yc = yc + jnp.concatenate([top, sh[SUBLANES:]], axis=0) * w[CONV_W - 1 - k:CONV_W - k, cs]
        cvb.append(_silu(yc))
        xc_scr[0:SUBLANES, cs] = xj[ct - SUBLANES:]

    pw = MXU_WIDTH
    for j in range(nblk // 2):
        xbj = proj(O_XB + j * pw, O_XB + (j + 1) * pw)
        xblk += [xbj[:, 0:LANES], xbj[:, LANES:]]
        xbtail_ref[0, :, j * pw:(j + 1) * pw] = xbj[ct - SUBLANES:]
        if j > 0:
            conv_block(2 * j - 2)
            conv_block(2 * j - 1)
    gt = _dot(hb, wg_ref[...])
    gtt = _dotg(wgt_ref[...], hb, NT_DIMS)
    conv_block(nblk - 2)
    conv_block(nblk - 1)

    gdn = _gdn_stages(i, cvb, gt, gtt, env, gprm_ref, gprmt_ref, onorm_ref, s_scr, mix_scr, ct=ct, n_lead=n_lead)
    qa_p = []
    for j in range(A_WIDTH // pw):
        qa_p.append(proj(O_QA + j * pw, O_QA + (j + 1) * pw))
        step(gdn, 2)
    qa = jnp.concatenate(qa_p, axis=-1)
    kv = proj(O_KV, O_ZA)
    kvtail_ref[0] = kv[ct - ntail:]
    slot = lax.rem(i, 2)
    kvb = jnp.concatenate([kvprev_scr[1 - slot], kv], axis=0).astype(BF16)
    if ct >= WINDOW:
        kvprev_scr[slot] = kv[ct - WINDOW:]
    swa = _swa_stages(i, qa, kvb, env, sink_ref, ct=ct, sb=ntail, pos_shift=pos_shift)
    n_sw = (ct // ntail) * A_KV_HEADS
    step(gdn)
    for _ in range(ct // CHUNK):
        step(gdn)
        step(swa)
    step(swa, max(n_sw - ct // CHUNK, 0))
    za_p, zb_p = [], []
    fill = ([lambda j=j: za_p.append(proj(O_ZA + j * pw, O_ZA + (j + 1) * pw)) for j in range(A_WIDTH // pw)]
            + [lambda j=j: zb_p.append(proj(O_ZB + j * pw, O_ZB + (j + 1) * pw)) for j in range(B_VAL // pw)])
    for lv in range(10):
        step(gdn)
        if lv % 2 == 1 and fill:
            fill.pop(0)()
    for f in fill:
        f()
    env['za'] = jnp.concatenate(za_p, axis=-1)
    env['zb'] = jnp.concatenate(zb_p, axis=-1)
    step(gdn, ct // CHUNK)
    for _ in range(2 * (ct // CHUNK)):
        step(gdn)
        step(swa, -(-2 * n_sw // (2 * (ct // CHUNK))))
    for _ in swa:
        pass
    _swa_store(env, mix_scr, ct=ct, sb=ntail)
    ya = x + _dot(mix_scr[:, 0:A_WIDTH], wout_ref[0:A_WIDTH, :])
    for _ in gdn:
        pass
    y_ref[0] = ya + _dot(mix_scr[:, A_WIDTH:], wout_ref[A_WIDTH:, :])

    @pl.when(i == nsteps - 1)
    def _():
        sout_ref[0] = s_scr[...]


def _ab_layer_prompt(x3, norm_g, w_main, w_gate, w_gate_t, sink, hist_kv, conv_w8, gprm, gprm_t, onorm,
                     w_out, conv_in, s_in, ct, n_lead, pos_shift):
    B, L, _ = x3.shape
    ntail = min(WINDOW, ct)
    kern = functools.partial(_ab_layer_kernel, ct=ct, n_lead=n_lead, pos_shift=pos_shift)
    full = lambda a: pl.BlockSpec(a.shape, lambda b, i: (0,) * a.ndim)
    blk = pl.BlockSpec((1, ct, D_MODEL), lambda b, i: (b, i, 0))
    per_b = lambda *s: pl.BlockSpec((1,) + s, lambda b, i: (b,) + (0,) * len(s))
    return pl.pallas_call(
        kern,
        grid=(B, L // ct),
        in_specs=[blk, full(norm_g), full(w_main), full(w_gate), full(w_gate_t), full(sink), full(hist_kv),
                  full(conv_w8), full(gprm), full(gprm_t), full(onorm), full(w_out),
                  full(conv_in), full(s_in)],
        out_specs=[blk, per_b(ntail, 2 * A_KV_WIDTH), per_b(SUBLANES, B_CONV_CH), per_b(B_HEADS, B_DK, B_DV)],
        out_shape=[jax.ShapeDtypeStruct((B, L, D_MODEL), F32),
                   jax.ShapeDtypeStruct((B, ntail, 2 * A_KV_WIDTH), F32),
                   jax.ShapeDtypeStruct((B, SUBLANES, B_CONV_CH), F32),
                   jax.ShapeDtypeStruct((B, B_HEADS, B_DK, B_DV), F32)],
        scratch_shapes=[pltpu.VMEM((B_HEADS, B_DK, B_DV), F32),
                        pltpu.VMEM((SUBLANES, B_CONV_CH), F32),
                        pltpu.VMEM((2, WINDOW, 2 * A_KV_WIDTH), F32),
                        pltpu.VMEM((ct, AB_MIX), BF16)],
        compiler_params=_cparams(("arbitrary", "arbitrary")),
        name="ab_layer_prompt",
    )(x3, norm_g, w_main, w_gate, w_gate_t, sink, hist_kv, conv_w8, gprm, gprm_t, onorm, w_out, conv_in, s_in)


def _c_layer_kernel(x_ref, g_ref, w_ref, wgk_ref, wup_ref, bgk_ref, onorm_ref, wout_ref, gfin_ref,
                    sin_ref, y_ref, sout_ref, st_scr, og_scr, *, ct, n_lead):
    i = pl.program_id(1)
    nsteps = pl.num_programs(1)

    @pl.when(i == 0)
    def _():
        st_scr[...] = sin_ref[0]

    x = x_ref[0]
    hb = _rms(x, g_ref[...]).astype(BF16)
    low = _dot(hb, wgk_ref[...])
    q_all = _dot(hb, w_ref[:, 0:C_KEY])
    la_all = _gla_log_decay(low, wup_ref, bgk_ref)
    k_all = _dot(hb, w_ref[:, C_KEY:2 * C_KEY])
    v_h, z_h = [], []
    later = ([lambda h=h: v_h.append(_dot(hb, w_ref[:, 2 * C_KEY + h * C_DV:2 * C_KEY + (h + 1) * C_DV]).astype(BF16))
              for h in range(C_HEADS)]
             + [lambda h=h: z_h.append(_dot(hb, w_ref[:, 2 * C_KEY + C_VAL + h * C_DV:
                                                       2 * C_KEY + C_VAL + (h + 1) * C_DV]))
                for h in range(C_HEADS)])

    def run_later(n):
        for _ in range(n):
            if later:
                later.pop(0)()
    rows = i * ct + lax.broadcasted_iota(jnp.int32, la_all.shape, 0)
    la_all = jnp.where(rows >= n_lead, la_all, 0.0)

    ii = lax.broadcasted_iota(jnp.int32, (CHUNK, CHUNK), 0)
    jj = lax.broadcasted_iota(jnp.int32, (CHUNK, CHUNK), 1)
    lower = ii >= jj
    ltri = lower.astype(BF16)
    nc = ct // CHUNK
    rs = lambda c: slice(c * CHUNK, (c + 1) * CHUNK)
    ks = lambda h: slice(h * C_DK, (h + 1) * C_DK)
    vs = lambda h: slice(h * C_DV, (h + 1) * C_DV)
    items = [(c, h) for c in range(nc) for h in range(C_HEADS)]
    bc_all = [sum(_dot(ltri, p) for p in _split2(la_all[rs(c)])) for c in range(nc)]
    run_later(2)
    every = max(nc // C_HEADS, 1)
    qd, kinv, kd, gl, qk = [], [], [], [], []
    for n, (c, h) in enumerate(items):
        bc = bc_all[c][:, ks(h)]
        k = k_all[rs(c), ks(h)]
        bl = bc[CHUNK - 1:CHUNK, :]
        qd.append((q_all[rs(c), ks(h)] * (C_DK ** -0.5) * jnp.exp(bc)).astype(BF16))
        kinv.append((k * jnp.exp(-bc)).astype(BF16))
        kd.append((k * jnp.exp(bl - bc)).astype(BF16))
        gl.append(jnp.exp(bl))
        if h == C_HEADS - 1:
            qk += [jnp.where(lower, _dotg(qd[m], kinv[m], NT_DIMS), 0.0).astype(BF16)
                   for m in range(n - C_HEADS + 1, n + 1)]
            if c % every == every - 1:
                run_later(1)
    run_later(C_HEADS - len(v_h))
    v_ = [v_h[h][rs(c)] for c, h in items]
    o_intra = []
    for c in range(nc):
        o_intra += [_dot(qk[c * C_HEADS + h], v_[c * C_HEADS + h]) for h in range(C_HEADS)]
        if c % every == every - 1:
            run_later(1)
    run_later(len(later))

    rb = min(ct, 4 * CHUNK)

    def gate_store(c, o):
        for h in range(C_HEADS):
            og_scr[rs(c), vs(h)] = (_rms(o[h], onorm_ref[...]) * _silu(z_h[h][rs(c)])).astype(BF16)
        if ((c + 1) * CHUNK) % rb == 0:
            r = slice((c + 1) * CHUNK - rb, (c + 1) * CHUNK)
            y_ref[0, r, :] = _rms(x[r] + _dot(og_scr[r, :], wout_ref[...]), gfin_ref[...])

    st = [st_scr[h] for h in range(C_HEADS)]
    o_prev = None
    pad_rows = jnp.zeros((SUBLANES - C_HEADS, C_DK), F32)
    for c in range(nc):
        idx = [c * C_HEADS + h for h in range(C_HEADS)]
        glt = jnp.concatenate([gl[n] for n in idx] + [pad_rows], axis=0).T
        o = [o_intra[n] + _dot(qd[n], st[h].astype(BF16)) for h, n in enumerate(idx)]
        st = [st[h] * glt[:, h:h + 1] + _dotg(kd[n], v_[n], TN_DIMS) for h, n in enumerate(idx)]
        if o_prev is not None:
            gate_store(c - 1, o_prev)
        o_prev = o
    gate_store(nc - 1, o_prev)
    for h in range(C_HEADS):
        st_scr[h] = st[h]

    @pl.when(i == nsteps - 1)
    def _():
        sout_ref[0] = st_scr[...]


def _c_layer_prompt(x3, norm_g, w_main, w_gk, w_up, b_gk, onorm, w_out, final_g, st_in, ct, n_lead):
    B, L, _ = x3.shape
    kern = functools.partial(_c_layer_kernel, ct=ct, n_lead=n_lead)
    full = lambda a: pl.BlockSpec(a.shape, lambda b, i: (0,) * a.ndim)
    blk = pl.BlockSpec((1, ct, D_MODEL), lambda b, i: (b, i, 0))
    st = pl.BlockSpec((1, C_HEADS, C_DK, C_DV), lambda b, i: (b, 0, 0, 0))
    return pl.pallas_call(
        kern,
        grid=(B, L // ct),
        in_specs=[blk, full(norm_g), full(w_main), full(w_gk), full(w_up), full(b_gk), full(onorm),
                  full(w_out), full(final_g), full(st_in)],
        out_specs=[blk, st],
        out_shape=[jax.ShapeDtypeStruct((B, L, D_MODEL), F32),
                   jax.ShapeDtypeStruct((B, C_HEADS, C_DK, C_DV), F32)],
        scratch_shapes=[pltpu.VMEM((C_HEADS, C_DK, C_DV), F32), pltpu.VMEM((ct, C_VAL), BF16)],
        compiler_params=_cparams(("arbitrary", "arbitrary")),
        name="c_layer_prompt",
    )(x3, norm_g, w_main, w_gk, w_up, b_gk, onorm, w_out, final_g, st_in)


def _ab_proj_kernel(x_ref, g_ref, w_ref, wg_ref, qa_ref, kv_ref, za_ref, xb_ref, zb_ref, gates_ref):
    hb = _rms(x_ref[...], g_ref[...]).astype(BF16)
    qa_ref[...] = _dot(hb, w_ref[:, O_QA:O_KV])
    kv_ref[...] = _dot(hb, w_ref[:, O_KV:O_ZA])
    za_ref[...] = _dot(hb, w_ref[:, O_ZA:O_XB])
    xb_ref[...] = _dot(hb, w_ref[:, O_XB:O_ZB])
    zb_ref[...] = _dot(hb, w_ref[:, O_ZB:AB_MAIN])
    gates_ref[...] = _dot(hb, wg_ref[...])


def _ab_proj(x2d, norm_g, w_main, w_gate):
    n = x2d.shape[0]
    full = lambda a: pl.BlockSpec(a.shape, lambda: (0,) * a.ndim)
    widths = (A_WIDTH, 2 * A_KV_WIDTH, A_WIDTH, B_CONV_CH, B_VAL, LANES)
    args = (x2d, norm_g, w_main, w_gate)
    return pl.pallas_call(
        _ab_proj_kernel,
        in_specs=[full(a) for a in args],
        out_specs=[pl.BlockSpec((n, w), lambda: (0, 0)) for w in widths],
        out_shape=[jax.ShapeDtypeStruct((n, w), F32) for w in widths],
        compiler_params=pltpu.CompilerParams(vmem_limit_bytes=VMEM_LIMIT),
        name="ab_in_proj_sample",
    )(*args)


def _swa_sample_kernel(q_ref, kvn_ref, za_ref, ck_ref, cv_ref, sink_ref, o_ref, nk_ref, nv_ref, *, bt):
    row = lax.broadcasted_iota(jnp.int32, (WINDOW, A_KV_WIDTH), 0)
    hrow = lax.broadcasted_iota(jnp.int32, (A_HEADS, A_HD), 0)
    sk = sink_ref[...]
    nkb, nvb = [], []
    for b in range(bt):
        nk = jnp.where(row == WINDOW - 1, kvn_ref[b:b + 1, 0:A_KV_WIDTH], pltpu.roll(ck_ref[b], WINDOW - 1, 0))
        nv = jnp.where(row == WINDOW - 1, kvn_ref[b:b + 1, A_KV_WIDTH:], pltpu.roll(cv_ref[b], WINDOW - 1, 0))
        nk_ref[b] = nk
        nv_ref[b] = nv
        nkb.append(nk.astype(BF16))
        nvb.append(nv.astype(BF16))
    items = [(b, j) for b in range(bt) for j in range(A_KV_HEADS)]
    q8 = [q_ref[b].astype(BF16) for b in range(bt)]
    sc = [_dotg(q8[b], nkb[b][:, j * A_HD:(j + 1) * A_HD], NT_DIMS) * (A_HD ** -0.5) for b, j in items]
    m = [jnp.maximum(jnp.max(s, axis=-1, keepdims=True), sk) for s in sc]
    p = [jnp.exp(sc[n] - m[n]) for n in range(len(items))]
    den = [jnp.sum(p[n], axis=-1, keepdims=True) + jnp.exp(sk - m[n]) for n in range(len(items))]
    oj = [_dot(p[n].astype(BF16), nvb[b][:, j * A_HD:(j + 1) * A_HD]) / den[n] for n, (b, j) in enumerate(items)]
    for b in range(bt):
        o8 = jnp.where(hrow >= A_GROUP, oj[b * A_KV_HEADS + 1], oj[b * A_KV_HEADS])
        o_ref[b] = o8 * _silu(za_ref[b])


def _swa_sample(q3, kv_new, za3, cache_k, cache_v, sink_col, bt):
    nb = q3.shape[0]
    hd = pl.BlockSpec((bt, A_HEADS, A_HD), lambda i: (i, 0, 0))
    cache = pl.BlockSpec((bt, WINDOW, A_KV_WIDTH), lambda i: (i, 0, 0))
    return pl.pallas_call(
        functools.partial(_swa_sample_kernel, bt=bt),
        grid=(nb // bt,),
        in_specs=[hd, pl.BlockSpec((bt, 2 * A_KV_WIDTH), lambda i: (i, 0)), hd, cache, cache,
                  pl.BlockSpec(sink_col.shape, lambda i: (0, 0))],
        out_specs=[hd, cache, cache],
        out_shape=[jax.ShapeDtypeStruct(q3.shape, F32),
                   jax.ShapeDtypeStruct(cache_k.shape, F32),
                   jax.ShapeDtypeStruct(cache_v.shape, F32)],
        compiler_params=_cparams(("arbitrary",)),
        name="swa_sample",
    )(q3, kv_new, za3, cache_k, cache_v, sink_col)


def _gdn_sample_prep_kernel(xb_ref, h0_ref, h1_ref, h2_ref, gates_ref, convw_ref, gprm_ref,
                            q_ref, k_ref, v_ref, bg_ref):
    w = convw_ref[...]
    y = (h0_ref[...] * w[0:1, :] + h1_ref[...] * w[1:2, :] + h2_ref[...] * w[2:3, :]
         + xb_ref[...] * w[3:4, :])
    cv = _silu(y)
    for h in range(B_HEADS):
        q = cv[:, h * B_DK:(h + 1) * B_DK]
        k = cv[:, B_KEY + h * B_DK:B_KEY + (h + 1) * B_DK]
        q_ref[:, h * B_DK:(h + 1) * B_DK] = (
            q * lax.rsqrt(jnp.sum(q * q, axis=-1, keepdims=True) + EPS) * (B_DK ** -0.5))
        k_ref[:, h * B_DK:(h + 1) * B_DK] = k * lax.rsqrt(jnp.sum(k * k, axis=-1, keepdims=True) + EPS)
    v_ref[...] = cv[:, 2 * B_KEY:]
    gt = gates_ref[...]
    col = lax.broadcasted_iota(jnp.int32, gt.shape, 1)
    g = -gprm_ref[0:1, :] * _softplus(gt + gprm_ref[1:2, :])
    bg_ref[...] = jnp.where(col < B_HEADS, _sigmoid(gt), jnp.exp(g))


def _gdn_sample_prep(xb, h0, h1, h2, gates, conv_w8, gprm):
    n = xb.shape[0]
    full = lambda a: pl.BlockSpec(a.shape, lambda: (0,) * a.ndim)
    args = (xb, h0, h1, h2, gates, conv_w8, gprm)
    return pl.pallas_call(
        _gdn_sample_prep_kernel,
        in_specs=[full(a) for a in args],
        out_specs=[pl.BlockSpec((n, B_KEY), lambda: (0, 0))] * 3 + [pl.BlockSpec((n, LANES), lambda: (0, 0))],
        out_shape=[jax.ShapeDtypeStruct((n, B_KEY), F32)] * 3 + [jax.ShapeDtypeStruct((n, LANES), F32)],
        compiler_params=pltpu.CompilerParams(vmem_limit_bytes=VMEM_LIMIT),
        name="gdn_sample_prep",
    )(*args)


def _gdn_sample_kernel(s_ref, qt_ref, kt_ref, v_ref, bg_ref, zb_ref, onorm_ref, o_ref, sout_ref, *, bt):
    items = [(b, h) for b in range(bt) for h in range(B_HEADS)]
    hs = lambda h: slice(h * B_DV, (h + 1) * B_DV)
    kt = [kt_ref[:, hs(h)].T for h in range(B_HEADS)]
    qt = [qt_ref[:, hs(h)].T for h in range(B_HEADS)]
    kc = [kt[h][:, b:b + 1] for b, h in items]
    eg = [bg_ref[b:b + 1, B_HEADS + h:B_HEADS + h + 1] for b, h in items]
    ks = [jnp.sum(kc[n] * s_ref[b, h], axis=0, keepdims=True) for n, (b, h) in enumerate(items)]
    v_new = [bg_ref[b:b + 1, h:h + 1] * (v_ref[b:b + 1, hs(h)] - eg[n] * ks[n]) for n, (b, h) in enumerate(items)]
    o = []
    for n, (b, h) in enumerate(items):
        s_new = eg[n] * s_ref[b, h] + kc[n] * v_new[n]
        sout_ref[b, h] = s_new
        o.append(jnp.sum(qt[h][:, b:b + 1] * s_new, axis=0, keepdims=True))
    o_blk = jnp.concatenate([jnp.concatenate([o[b * B_HEADS + h] for h in range(B_HEADS)], axis=-1)
                             for b in range(bt)], axis=0)
    for h in range(B_HEADS):
        o_ref[:, hs(h)] = _rms(o_blk[:, hs(h)], onorm_ref[...]) * _silu(zb_ref[:, hs(h)])


def _gdn_sample(state, qt, kt, v, bg, zb, onorm, bt):
    nb = state.shape[0]
    st = pl.BlockSpec((bt, B_HEADS, B_DK, B_DV), lambda i: (i, 0, 0, 0))
    row = lambda n: pl.BlockSpec((bt, n), lambda i: (i, 0))
    return pl.pallas_call(
        functools.partial(_gdn_sample_kernel, bt=bt),
        grid=(nb // bt,),
        in_specs=[st, row(B_KEY), row(B_KEY), row(B_VAL), row(LANES), row(B_VAL),
                  pl.BlockSpec(onorm.shape, lambda i: (0, 0))],
        out_specs=[row(B_VAL), st],
        out_shape=[jax.ShapeDtypeStruct((nb, B_VAL), F32), jax.ShapeDtypeStruct(state.shape, F32)],
        compiler_params=_cparams(("arbitrary",)),
        name="gdn_sample",
    )(state, qt, kt, v, bg, zb, onorm)


def _ab_out_c_proj_kernel(x_ref, oa_ref, ob_ref, wout_ref, g_ref, w_ref, wgk_ref, wup_ref, bgk_ref,
                          y_ref, q_ref, k_ref, v_ref, z_ref, la_ref):
    y = (x_ref[...] + _dot(oa_ref[...].astype(BF16), wout_ref[0:A_WIDTH, :])
         + _dot(ob_ref[...].astype(BF16), wout_ref[A_WIDTH:, :]))
    y_ref[...] = y
    hb = _rms(y, g_ref[...]).astype(BF16)
    q_ref[...] = _dot(hb, w_ref[:, 0:C_KEY])
    k_ref[...] = _dot(hb, w_ref[:, C_KEY:2 * C_KEY])
    v_ref[...] = _dot(hb, w_ref[:, 2 * C_KEY:2 * C_KEY + C_VAL])
    z_ref[...] = _dot(hb, w_ref[:, 2 * C_KEY + C_VAL:C_MAIN])
    la_ref[...] = _gla_log_decay(_dot(hb, wgk_ref[...]), wup_ref, bgk_ref)


def _ab_out_c_proj(x2d, oa, ob, w_out, norm_g, w_main, w_gk, w_up, b_gk):
    n = x2d.shape[0]
    full = lambda a: pl.BlockSpec(a.shape, lambda: (0,) * a.ndim)
    widths = (D_MODEL, C_KEY, C_KEY, C_VAL, C_VAL, C_KEY)
    args = (x2d, oa, ob, w_out, norm_g, w_main, w_gk, w_up, b_gk)
    return pl.pallas_call(
        _ab_out_c_proj_kernel,
        in_specs=[full(a) for a in args],
        out_specs=[pl.BlockSpec((n, w), lambda: (0, 0)) for w in widths],
        out_shape=[jax.ShapeDtypeStruct((n, w), F32) for w in widths],
        compiler_params=pltpu.CompilerParams(vmem_limit_bytes=VMEM_LIMIT),
        name="ab_out_c_in_proj_sample",
    )(*args)


def _gla_sample_kernel(s_ref, qt_ref, kt_ref, lat_ref, v_ref, z_ref, onorm_ref, o_ref, sout_ref, *, bt):
    items = [(b, h) for b in range(bt) for h in range(C_HEADS)]
    hs = lambda h: slice(h * C_DV, (h + 1) * C_DV)
    ks = lambda h: slice(h * C_DK, (h + 1) * C_DK)
    at = [jnp.exp(lat_ref[:, ks(h)]).T for h in range(C_HEADS)]
    kt = [kt_ref[:, ks(h)].T for h in range(C_HEADS)]
    qt = [(qt_ref[:, ks(h)] * (C_DK ** -0.5)).T for h in range(C_HEADS)]
    o = []
    for n, (b, h) in enumerate(items):
        s_new = at[h][:, b:b + 1] * s_ref[b, h] + kt[h][:, b:b + 1] * v_ref[b:b + 1, hs(h)]
        sout_ref[b, h] = s_new
        o.append(jnp.sum(qt[h][:, b:b + 1] * s_new, axis=0, keepdims=True))
    o_blk = jnp.concatenate([jnp.concatenate([o[b * C_HEADS + h] for h in range(C_HEADS)], axis=-1)
                             for b in range(bt)], axis=0)
    for h in range(C_HEADS):
        o_ref[:, hs(h)] = _rms(o_blk[:, hs(h)], onorm_ref[...]) * _silu(z_ref[:, hs(h)])


def _gla_sample(state, qt, kt, lat, v, z, onorm, bt):
    nb = state.shape[0]
    st = pl.BlockSpec((bt, C_HEADS, C_DK, C_DV), lambda i: (i, 0, 0, 0))
    row = lambda n: pl.BlockSpec((bt, n), lambda i: (i, 0))
    return pl.pallas_call(
        functools.partial(_gla_sample_kernel, bt=bt),
        grid=(nb // bt,),
        in_specs=[st, row(C_KEY), row(C_KEY), row(C_KEY), row(C_VAL), row(C_VAL),
                  pl.BlockSpec(onorm.shape, lambda i: (0, 0))],
        out_specs=[row(C_VAL), st],
        out_shape=[jax.ShapeDtypeStruct((nb, C_VAL), F32), jax.ShapeDtypeStruct(state.shape, F32)],
        compiler_params=_cparams(("arbitrary",)),
        name="gla_sample",
    )(state, qt, kt, lat, v, z, onorm)


def _c_out_kernel(x_ref, o_ref, w_ref, g_ref, y_ref):
    y_ref[...] = _rms(x_ref[...] + _dot(o_ref[...].astype(BF16), w_ref[...]), g_ref[...])


def _c_out(x2d, oc, w_out, final_g):
    n = x2d.shape[0]
    full = lambda a: pl.BlockSpec(a.shape, lambda: (0,) * a.ndim)
    args = (x2d, oc, w_out, final_g)
    return pl.pallas_call(
        _c_out_kernel,
        in_specs=[full(a) for a in args],
        out_specs=pl.BlockSpec((n, D_MODEL), lambda: (0, 0)),
        out_shape=jax.ShapeDtypeStruct((n, D_MODEL), F32),
        compiler_params=pltpu.CompilerParams(vmem_limit_bytes=VMEM_LIMIT),
        name="c_out_proj_norm_sample",
    )(*args)


def kernel(x_prompt, x_sample, cache_swa_k, cache_swa_v, state_dn_conv, state_dn, state_gla,
           meta_tokens, norm_ab, w_in_ab, sink_a, conv_b, a_log_b, dt_bias_b, onorm_b, w_out_ab,
           norm_c, w_in_c, w_gk_up, b_gk, onorm_c, w_out_c, final_norm):
    Bp, L, _ = x_prompt.shape
    Bs = x_sample.shape[0]
    assert L % PROMPT_TILE == 0 and Bs % SAMPLE_BLOCK == 0

    w_ab = w_in_ab[0]
    w_ab_main = w_ab.astype(BF16)
    w_ab_gate = jnp.pad(w_ab[:, AB_MAIN:], ((0, 0), (0, LANES - 2 * B_HEADS))).astype(BF16)
    w_ab_gate_t = jnp.pad(w_ab[:, AB_MAIN:].T, ((0, BF16_ROWS - 2 * B_HEADS), (0, 0))).astype(BF16)
    g_ab = norm_ab[0][None, :]
    sink_row = sink_a[0][None, :]
    sink_col = sink_a[0][:, None]
    conv_w8 = jnp.pad(conv_b[0], ((0, SUBLANES - CONV_W), (0, 0)))
    gp = jnp.stack([jnp.exp(a_log_b[0]), dt_bias_b[0]])
    gprm = jnp.pad(gp, ((0, SUBLANES - 2), (B_HEADS, LANES - 2 * B_HEADS)))
    gprm_t = jnp.pad(gp.T, ((B_HEADS, BF16_ROWS - 2 * B_HEADS), (0, LANES - 2)))
    onb = onorm_b[0][None, :]
    w_ab_out = w_out_ab[0].astype(BF16)
    w_c = w_in_c[0]
    w_c_main = w_c.astype(BF16)
    w_c_gk = jnp.pad(w_c[:, C_MAIN:], ((0, 0), (0, LANES - C_RANK))).astype(BF16)
    w_up = jnp.pad(w_gk_up[0], ((0, LANES - C_RANK), (0, 0)))
    bgk = b_gk[0][None, :]
    g_c = norm_c[0][None, :]
    onc = onorm_c[0][None, :]
    w_c_out = w_out_c[0].astype(BF16)
    g_fin = final_norm[None, :]

    def ab_layer(x3, ct, hist_kv, pos_shift, conv_in, s_in, n_lead):
        return _ab_layer_prompt(x3, g_ab, w_ab_main, w_ab_gate, w_ab_gate_t, sink_row, hist_kv, conv_w8,
                                gprm, gprm_t, onb, w_ab_out, conv_in, s_in, ct, n_lead, pos_shift)

    def c_layer(x3, ct, st_in, n_lead):
        return _c_layer_prompt(x3, g_c, w_c_main, w_c_gk, w_up, bgk, onc, w_c_out, g_fin, st_in, ct, n_lead)

    xpre = jnp.concatenate([jnp.zeros((LEAD, D_MODEL), F32), meta_tokens], axis=0)[None]
    y_pre, kv_pre, xbtail_pre, sdn_pre = ab_layer(
        xpre, CHUNK, jnp.zeros((WINDOW, 2 * A_KV_WIDTH), F32), -LEAD,
        jnp.zeros((1, SUBLANES, B_CONV_CH), F32), jnp.zeros((1, B_HEADS, B_DK, B_DV), F32), LEAD)
    _, sgla_pre = c_layer(y_pre, CHUNK, jnp.zeros((1, C_HEADS, C_DK, C_DV), F32), LEAD)

    hist_kv = jnp.concatenate([jnp.zeros((WINDOW - CHUNK, 2 * A_KV_WIDTH), F32), kv_pre[0]], axis=0)
    y1, kv_tail, xb_tail, sdn_p = ab_layer(
        x_prompt, PROMPT_TILE, hist_kv, N_META, xbtail_pre, sdn_pre, 0)
    y_prompt, sgla_p = c_layer(y1, PROMPT_TILE, sgla_pre, 0)

    swa_k_p = kv_tail[:, :, :A_KV_WIDTH].reshape(1, Bp, WINDOW, A_KV_HEADS, A_HD)
    swa_v_p = kv_tail[:, :, A_KV_WIDTH:].reshape(1, Bp, WINDOW, A_KV_HEADS, A_HD)
    dn_conv_p = xb_tail[:, SUBLANES - (CONV_W - 1):, :][None]

    xs = x_sample.reshape(Bs, D_MODEL)
    qa, kvn, za, xb, zb, gates = _ab_proj(xs, g_ab, w_ab_main, w_ab_gate)
    oa3, nk, nv = _swa_sample(qa.reshape(Bs, A_HEADS, A_HD), kvn, za.reshape(Bs, A_HEADS, A_HD),
                              cache_swa_k[0].reshape(Bs, WINDOW, A_KV_WIDTH),
                              cache_swa_v[0].reshape(Bs, WINDOW, A_KV_WIDTH), sink_col, SAMPLE_BLOCK)
    hist = state_dn_conv[0]
    qn, kn, vn, bg = _gdn_sample_prep(xb, hist[:, 0], hist[:, 1], hist[:, 2], gates, conv_w8, gprm)
    ob, sdn_s = _gdn_sample(state_dn[0], qn, kn, vn, bg, zb, onb, SAMPLE_BLOCK)
    ys, qc, kc, vc, zc, la = _ab_out_c_proj(xs, oa3.reshape(Bs, A_WIDTH), ob, w_ab_out,
                                            g_c, w_c_main, w_c_gk, w_up, bgk)
    ocs, sgla_s = _gla_sample(state_gla[0], qc, kc, la, vc, zc, onc, SAMPLE_BLOCK)
    y_sample = _c_out(ys, ocs, w_c_out, g_fin).reshape(Bs, 1, D_MODEL)
    dn_conv_s = jnp.concatenate([hist[:, 1:], xb[:, None, :]], axis=1)[None]

    return (y_prompt, y_sample, swa_k_p, swa_v_p, dn_conv_p, sdn_p[None], sgla_p[None],
            nk.reshape(1, Bs, WINDOW, A_KV_HEADS, A_HD), nv.reshape(1, Bs, WINDOW, A_KV_HEADS, A_HD),
            dn_conv_s, sdn_s[None], sgla_s[None])
```

```python
import functools

import jax
import jax.numpy as jnp
from jax import lax
from jax.experimental import pallas as pl
from jax.experimental.pallas import tpu as pltpu

F32 = jnp.float32
BF16 = jnp.bfloat16
EPS = 1e-6

D_MODEL = 1024
N_META = 16
CHUNK = 64
LEAD = (-N_META) % CHUNK
A_HEADS, A_KV_HEADS, A_HD = 8, 2, 64
A_GROUP = A_HEADS // A_KV_HEADS
A_WIDTH = A_HEADS * A_HD
A_KV_WIDTH = A_KV_HEADS * A_HD
WINDOW = 128
B_HEADS, B_DK, B_DV = 4, 128, 128
B_KEY = B_HEADS * B_DK
B_VAL = B_HEADS * B_DV
CONV_W = 4
B_CONV_CH = 2 * B_KEY + B_VAL
C_HEADS, C_DK, C_DV = 4, 128, 256
C_KEY = C_HEADS * C_DK
C_VAL = C_HEADS * C_DV
C_RANK = 16
C_GATE_NORM = 16.0
AB_MAIN = 2 * A_WIDTH + 2 * A_KV_WIDTH + B_CONV_CH + B_VAL
AB_MIX = A_WIDTH + B_VAL
C_MAIN = 2 * C_KEY + 2 * C_VAL
O_QA, O_KV, O_ZA, O_XB, O_ZB = 0, 512, 768, 1280, 2816

LANES = 128
SUBLANES = 8
BF16_ROWS = 16
MXU_WIDTH = 2 * LANES
VMEM_LIMIT = 56 * 1024 * 1024

PROMPT_TILE = 512
SAMPLE_BLOCK = 16
NEG = -1e30

NT_DIMS = (((1,), (1,)), ((), ()))
TN_DIMS = (((0,), (0,)), ((), ()))


def _cparams(sem):
    return pltpu.CompilerParams(dimension_semantics=sem, vmem_limit_bytes=VMEM_LIMIT)


def _dot(a, b):
    return jnp.dot(a, b, preferred_element_type=F32)


def _dotg(a, b, dims):
    return lax.dot_general(a, b, dims, preferred_element_type=F32)


def _split3(a):
    hi = a.astype(BF16)
    r = a - hi.astype(F32)
    mid = r.astype(BF16)
    lo = (r - mid.astype(F32)).astype(BF16)
    return hi, mid, lo


def _split2(a):
    hi = a.astype(BF16)
    return hi, (a - hi.astype(F32)).astype(BF16)


def _sigmoid(x):
    return 1.0 / (1.0 + jnp.exp(-x))


def _silu(x):
    return x * _sigmoid(x)


def _softplus(x):
    return jnp.maximum(x, 0.0) + jnp.log1p(jnp.exp(-jnp.abs(x)))


def _log_sigmoid(x):
    return jnp.minimum(x, 0.0) - jnp.log1p(jnp.exp(-jnp.abs(x)))


def _rms(x, g):
    return x * lax.rsqrt(jnp.mean(x * x, axis=-1, keepdims=True) + EPS) * g


def _gla_log_decay(low, wup_ref, bgk_ref):
    lh, lm, _ = _split3(low)
    wh, wm, _ = _split3(wup_ref[...])
    up = _dot(lh, wh) + _dot(lh, wm) + _dot(lm, wh)
    return _log_sigmoid(up + bgk_ref[...]) * (1.0 / C_GATE_NORM)


def _swa_stages(i, qa, kvb, env, sink_ref, *, ct, sb, pos_shift):
    r = lax.broadcasted_iota(jnp.int32, (sb, WINDOW + sb), 0)
    c = lax.broadcasted_iota(jnp.int32, (sb, WINDOW + sb), 1)
    diff = r - c + WINDOW
    in_window = (diff >= 0) & (diff < WINDOW)
    nsb = ct // sb
    items = [(s, j) for s in range(nsb) for j in range(A_KV_HEADS)]
    masks = [jnp.concatenate([in_window & (i * ct + s * sb + c - WINDOW + pos_shift >= 0)] * A_GROUP, axis=0)
             for s in range(nsb)]
    sinks = [jnp.concatenate([jnp.broadcast_to(sink_ref[0:1, j * A_GROUP + g:j * A_GROUP + g + 1], (sb, 1))
                              for g in range(A_GROUP)], axis=0) for j in range(A_KV_HEADS)]
    sc = []
    for s, j in items:
        qs = (jnp.concatenate([qa[s * sb:(s + 1) * sb, (j * A_GROUP + g) * A_HD:(j * A_GROUP + g + 1) * A_HD]
                               for g in range(A_GROUP)], axis=0) * (A_HD ** -0.5)).astype(BF16)
        sc.append(_dotg(qs, kvb[s * sb:s * sb + WINDOW + sb, j * A_HD:(j + 1) * A_HD], NT_DIMS))
        yield
    p, esk = [], []
    for n, (s, j) in enumerate(items):
        scm = jnp.where(masks[s], sc[n], NEG)
        mx = jnp.maximum(jnp.max(scm, axis=-1, keepdims=True), sinks[j])
        p.append(jnp.exp(scm - mx).astype(BF16))
        esk.append(jnp.exp(sinks[j] - mx))
        yield
    pv = []
    ones = jnp.ones((WINDOW + sb, A_HD), BF16)
    for n, (s, j) in enumerate(items):
        vx = jnp.concatenate([kvb[s * sb:s * sb + WINDOW + sb,
                                  A_KV_WIDTH + j * A_HD:A_KV_WIDTH + (j + 1) * A_HD], ones], axis=-1)
        pvx = _dot(p[n], vx)
        pv.append(pvx[:, :A_HD] / (pvx[:, A_HD:] + esk[n]))
        yield
    env['pv'] = pv


def _swa_store(env, mix_scr, *, ct, sb):
    pv = env['pv']
    for s in range(ct // sb):
        o = jnp.concatenate([pv[s * A_KV_HEADS + j][g * sb:(g + 1) * sb]
                             for j in range(A_KV_HEADS) for g in range(A_GROUP)], axis=-1)
        mix_scr[s * sb:(s + 1) * sb, 0:A_WIDTH] = (o * _silu(env['za'][s * sb:(s + 1) * sb])).astype(BF16)


def _gdn_stages(i, cvb, gt, gtt, env, gprm_ref, gprmt_ref, onorm_ref, s_scr, mix_scr, *, ct, n_lead):
    nc = ct // CHUNK
    items = [(c, h) for c in range(nc) for h in range(B_HEADS)]
    n_items = len(items)
    rs = lambda c: slice(c * CHUNK, (c + 1) * CHUNK)
    qn, kn = [], []
    for h in range(B_HEADS):
        q = cvb[h]
        k = cvb[B_HEADS + h]
        qn.append(q * lax.rsqrt(jnp.sum(q * q, axis=-1, keepdims=True) + EPS) * (B_DK ** -0.5))
        kn.append(k * lax.rsqrt(jnp.sum(k * k, axis=-1, keepdims=True) + EPS))
        yield
    beta = _sigmoid(gt)
    rows = i * ct + lax.broadcasted_iota(jnp.int32, gt.shape, 0)
    gcol = jnp.where(rows >= n_lead, -gprm_ref[0:1, :] * _softplus(gt + gprm_ref[1:2, :]), 0.0)
    lanes = i * ct + lax.broadcasted_iota(jnp.int32, gtt.shape, 1)
    grow = jnp.where(lanes >= n_lead, -gprmt_ref[:, 0:1] * _softplus(gtt + gprmt_ref[:, 1:2]), 0.0)
    ii = lax.broadcasted_iota(jnp.int32, (CHUNK, CHUNK), 0)
    jj = lax.broadcasted_iota(jnp.int32, (CHUNK, CHUNK), 1)
    lower = ii >= jj
    strict = ii > jj
    ltri = lower.astype(BF16)
    utri = (ii <= jj).astype(BF16)
    gcum_c = [sum(_dot(ltri, p) for p in _split3(gcol[rs(c)])) for c in range(nc)]
    gcum_r = [sum(_dot(p, utri) for p in _split3(grow[:, rs(c)])) for c in range(nc)]
    yield
    q_ = [qn[h][rs(c)] for c, h in items]
    k_ = [kn[h][rs(c)] for c, h in items]
    v_ = [cvb[2 * B_HEADS + h][rs(c)] for c, h in items]
    bh = [beta[rs(c), h:h + 1] for c, h in items]
    gc = [gcum_c[c][:, B_HEADS + h:B_HEADS + h + 1] for c, h in items]
    gr = [gcum_r[c][B_HEADS + h:B_HEADS + h + 1, :] for c, h in items]
    kb, a, qk = [], [], []
    for n in range(n_items):
        decay = jnp.exp(jnp.where(lower, gc[n] - gr[n], NEG))
        kb.append(k_[n] * bh[n])
        kq = _dotg(jnp.concatenate([kb[n], q_[n]], axis=0).astype(BF16), k_[n].astype(BF16), NT_DIMS)
        a.append(jnp.where(strict, kq[:CHUNK] * decay, 0.0))
        qk.append(jnp.where(lower, kq[CHUNK:] * decay, 0.0).astype(BF16))
        if n % B_HEADS == B_HEADS - 1:
            yield
    doff = [-jnp.where((ii >> 1) == (jj >> 1), a[n], 0.0) for n in range(n_items)]
    for lg in range(1, 6):
        m = ((ii >> (lg + 1)) == (jj >> (lg + 1))) & ((ii >> lg) != (jj >> lg))
        bm = [jnp.where(m, a[n], 0.0) for n in range(n_items)]
        db = [doff[n].astype(BF16) for n in range(n_items)]
        y = [bm[n] + _dot(db[n], bm[n].astype(BF16)) for n in range(n_items)]
        yield
        x = [y[n] + _dot(y[n].astype(BF16), db[n]) for n in range(n_items)]
        doff = [doff[n] - x[n] for n in range(n_items)]
        yield
    egc = [jnp.exp(gc[n]) for n in range(n_items)]
    g_last = [gc[n][CHUNK - 1:CHUNK, :] for n in range(n_items)]
    eg_last = [jnp.exp(g_last[n]) for n in range(n_items)]
    sol, kd, wq = [], [], []
    for n in range(n_items):
        rhs = jnp.concatenate([v_[n] * bh[n], kb[n] * egc[n]], axis=-1)
        sol.append(rhs + _dot(doff[n].astype(BF16), rhs.astype(BF16)))
        kd.append((k_[n] * jnp.exp(g_last[n] - gc[n])).astype(BF16))
        wq.append(jnp.concatenate([sol[n][:, B_DV:], q_[n] * egc[n]], axis=0).astype(BF16))
        if n % B_HEADS == B_HEADS - 1:
            yield

    def gate_store(c, o):
        for h in range(B_HEADS):
            z = env['zb'][rs(c), h * B_DV:(h + 1) * B_DV]
            mix_scr[rs(c), A_WIDTH + h * B_DV:A_WIDTH + (h + 1) * B_DV] = (
                _rms(o[h], onorm_ref[...]) * _silu(z)).astype(BF16)

    s_cur = [s_scr[h] for h in range(B_HEADS)]
    o_prev = None
    for c in range(nc):
        idx = [c * B_HEADS + h for h in range(B_HEADS)]
        ws = [_dot(wq[n], s_cur[h].astype(BF16)) for h, n in enumerate(idx)]
        if o_prev is not None:
            gate_store(c - 1, o_prev)
        yield
        v_new = [sol[n][:, :B_DV] - ws[h][:CHUNK] for h, n in enumerate(idx)]
        vb = [v_new[h].astype(BF16) for h in range(B_HEADS)]
        o_prev = [ws[h][CHUNK:] + _dot(qk[n], vb[h]) for h, n in enumerate(idx)]
        s_cur = [s_cur[h] * eg_last[n] + _dotg(kd[n], vb[h], TN_DIMS) for h, n in enumerate(idx)]
        yield
    for h in range(B_HEADS):
        s_scr[h] = s_cur[h]
    gate_store(nc - 1, o_prev)
    yield


def _ab_layer_kernel(x_ref, g_ref, w_ref, wg_ref, wgt_ref, sink_ref, hist_ref, convw_ref, gprm_ref,
                     gprmt_ref, onorm_ref, wout_ref, convin_ref, sin_ref,
                     y_ref, kvtail_ref, xbtail_ref, sout_ref,
                     s_scr, xc_scr, kvprev_scr, mix_scr, *, ct, n_lead, pos_shift):
    i = pl.program_id(1)
    nsteps = pl.num_programs(1)
    ntail = min(WINDOW, ct)

    @pl.when(i == 0)
    def _():
        s_scr[...] = sin_ref[0]
        xc_scr[0:SUBLANES, :] = convin_ref[0]
        kvprev_scr[1] = hist_ref[...]

    x = x_ref[0]
    hb = _rms(x, g_ref[...]).astype(BF16)
    proj = lambda a, b: _dot(hb, w_ref[:, a:b])
    env = {}

    def step(gen, n=1):
        for _ in range(n):
            next(gen, None)

    nblk = B_CONV_CH // LANES
    w = convw_ref[...]
    cvb = []

    xblk = []
    row8 = lax.broadcasted_iota(jnp.int32, (SUBLANES, LANES), 0)

    def conv_block(j):
        cs = slice(j * LANES, (j + 1) * LANES)
        xj = xblk[j]
        h8 = xc_scr[0:SUBLANES, cs]
        yc = xj * w[CONV_W - 1:CONV_W, cs]
        for k in range(1, CONV_W):
            sh = pltpu.roll(xj, k, 0)
            top = jnp.where(row8 < k, pltpu.roll(h8, k, 0), sh[0:SUBLANES])
            yc = yc + jnp.concatenate([top, sh[SUBLANES:]], axis=0) * w[CONV_W - 1 - k:CONV_W - k, cs]
        cvb.append(_silu(yc))
        xc_scr[0:SUBLANES, cs] = xj[ct - SUBLANES:]

    pw = MXU_WIDTH
    for j in range(nblk // 2):
        xbj = proj(O_XB + j * pw, O_XB + (j + 1) * pw)
        xblk += [xbj[:, 0:LANES], xbj[:, LANES:]]
        xbtail_ref[0, :, j * pw:(j + 1) * pw] = xbj[ct - SUBLANES:]
        if j > 0:
            conv_block(2 * j - 2)
            conv_block(2 * j - 1)
    gt = _dot(hb, wg_ref[...])
    gtt = _dotg(wgt_ref[...], hb, NT_DIMS)
    conv_block(nblk - 2)
    conv_block(nblk - 1)

    gdn = _gdn_stages(i, cvb, gt, gtt, env, gprm_ref, gprmt_ref, onorm_ref, s_scr, mix_scr, ct=ct, n_lead=n_lead)
    qa_p = []
    for j in range(A_WIDTH // pw):
        qa_p.append(proj(O_QA + j * pw, O_QA + (j + 1) * pw))
        step(gdn, 2)
    qa = jnp.concatenate(qa_p, axis=-1)
    kv = proj(O_KV, O_ZA)
    kvtail_ref[0] = kv[ct - ntail:]
    slot = lax.rem(i, 2)
    kvb = jnp.concatenate([kvprev_scr[1 - slot], kv], axis=0).astype(BF16)
    if ct >= WINDOW:
        kvprev_scr[slot] = kv[ct - WINDOW:]
    swa = _swa_stages(i, qa, kvb, env, sink_ref, ct=ct, sb=ntail, pos_shift=pos_shift)
    n_sw = (ct // ntail) * A_KV_HEADS
    step(gdn)
    for _ in range(ct // CHUNK):
        step(gdn)
        step(swa)
    step(swa, max(n_sw - ct // CHUNK, 0))
    za_p, zb_p = [], []
    fill = ([lambda j=j: za_p.append(proj(O_ZA + j * pw, O_ZA + (j + 1) * pw)) for j in range(A_WIDTH // pw)]
            + [lambda j=j: zb_p.append(proj(O_ZB + j * pw, O_ZB + (j + 1) * pw)) for j in range(B_VAL // pw)])
    for lv in range(10):
        step(gdn)
        if lv % 2 == 1 and fill:
            fill.pop(0)()
    for f in fill:
        f()
    env['za'] = jnp.concatenate(za_p, axis=-1)
    env['zb'] = jnp.concatenate(zb_p, axis=-1)
    step(gdn, ct // CHUNK)
    for _ in range(2 * (ct // CHUNK)):
        step(gdn)
        step(swa, -(-2 * n_sw // (2 * (ct // CHUNK))))
    for _ in swa:
        pass
    _swa_store(env, mix_scr, ct=ct, sb=ntail)
    ya = x + _dot(mix_scr[:, 0:A_WIDTH], wout_ref[0:A_WIDTH, :])
    for _ in gdn:
        pass
    y_ref[0] = ya + _dot(mix_scr[:, A_WIDTH:], wout_ref[A_WIDTH:, :])

    @pl.when(i == nsteps - 1)
    def _():
        sout_ref[0] = s_scr[...]


def _ab_layer_prompt(x3, norm_g, w_main, w_gate, w_gate_t, sink, hist_kv, conv_w8, gprm, gprm_t, onorm,
                     w_out, conv_in, s_in, ct, n_lead, pos_shift):
    B, L, _ = x3.shape
    ntail = min(WINDOW, ct)
    kern = functools.partial(_ab_layer_kernel, ct=ct, n_lead=n_lead, pos_shift=pos_shift)
    full = lambda a: pl.BlockSpec(a.shape, lambda b, i: (0,) * a.ndim)
    blk = pl.BlockSpec((1, ct, D_MODEL), lambda b, i: (b, i, 0))
    per_b = lambda *s: pl.BlockSpec((1,) + s, lambda b, i: (b,) + (0,) * len(s))
    return pl.pallas_call(
        kern,
        grid=(B, L // ct),
        in_specs=[blk, full(norm_g), full(w_main), full(w_gate), full(w_gate_t), full(sink), full(hist_kv),
                  full(conv_w8), full(gprm), full(gprm_t), full(onorm), full(w_out),
                  full(conv_in), full(s_in)],
        out_specs=[blk, per_b(ntail, 2 * A_KV_WIDTH), per_b(SUBLANES, B_CONV_CH), per_b(B_HEADS, B_DK, B_DV)],
        out_shape=[jax.ShapeDtypeStruct((B, L, D_MODEL), F32),
                   jax.ShapeDtypeStruct((B, ntail, 2 * A_KV_WIDTH), F32),
                   jax.ShapeDtypeStruct((B, SUBLANES, B_CONV_CH), F32),
                   jax.ShapeDtypeStruct((B, B_HEADS, B_DK, B_DV), F32)],
        scratch_shapes=[pltpu.VMEM((B_HEADS, B_DK, B_DV), F32),
                        pltpu.VMEM((SUBLANES, B_CONV_CH), F32),
                        pltpu.VMEM((2, WINDOW, 2 * A_KV_WIDTH), F32),
                        pltpu.VMEM((ct, AB_MIX), BF16)],
        compiler_params=_cparams(("arbitrary", "arbitrary")),
        name="ab_layer_prompt",
    )(x3, norm_g, w_main, w_gate, w_gate_t, sink, hist_kv, conv_w8, gprm, gprm_t, onorm, w_out, conv_in, s_in)


def _c_layer_kernel(x_ref, g_ref, w_ref, wgk_ref, wup_ref, bgk_ref, onorm_ref, wout_ref, gfin_ref,
                    sin_ref, y_ref, sout_ref, st_scr, og_scr, *, ct, n_lead):
    i = pl.program_id(1)
    nsteps = pl.num_programs(1)

    @pl.when(i == 0)
    def _():
        st_scr[...] = sin_ref[0]

    x = x_ref[0]
    hb = _rms(x, g_ref[...]).astype(BF16)
    low = _dot(hb, wgk_ref[...])
    q_all = _dot(hb, w_ref[:, 0:C_KEY])
    la_all = _gla_log_decay(low, wup_ref, bgk_ref)
    k_all = _dot(hb, w_ref[:, C_KEY:2 * C_KEY])
    v_h, z_h = [], []
    later = ([lambda h=h: v_h.append(_dot(hb, w_ref[:, 2 * C_KEY + h * C_DV:2 * C_KEY + (h + 1) * C_DV]).astype(BF16))
              for h in range(C_HEADS)]
             + [lambda h=h: z_h.append(_dot(hb, w_ref[:, 2 * C_KEY + C_VAL + h * C_DV:
                                                       2 * C_KEY + C_VAL + (h + 1) * C_DV]))
                for h in range(C_HEADS)])

    def run_later(n):
        for _ in range(n):
            if later:
                later.pop(0)()
    rows = i * ct + lax.broadcasted_iota(jnp.int32, la_all.shape, 0)
    la_all = jnp.where(rows >= n_lead, la_all, 0.0)

    ii = lax.broadcasted_iota(jnp.int32, (CHUNK, CHUNK), 0)
    jj = lax.broadcasted_iota(jnp.int32, (CHUNK, CHUNK), 1)
    lower = ii >= jj
    ltri = lower.astype(BF16)
    nc = ct // CHUNK
    rs = lambda c: slice(c * CHUNK, (c + 1) * CHUNK)
    ks = lambda h: slice(h * C_DK, (h + 1) * C_DK)
    vs = lambda h: slice(h * C_DV, (h + 1) * C_DV)
    items = [(c, h) for c in range(nc) for h in range(C_HEADS)]
    bc_all = [sum(_dot(ltri, p) for p in _split2(la_all[rs(c)])) for c in range(nc)]
    run_later(2)
    every = max(nc // C_HEADS, 1)
    qd, kinv, kd, gl, qk = [], [], [], [], []
    for n, (c, h) in enumerate(items):
        bc = bc_all[c][:, ks(h)]
        k = k_all[rs(c), ks(h)]
        bl = bc[CHUNK - 1:CHUNK, :]
        qd.append((q_all[rs(c), ks(h)] * (C_DK ** -0.5) * jnp.exp(bc)).astype(BF16))
        kinv.append((k * jnp.exp(-bc)).astype(BF16))
        kd.append((k * jnp.exp(bl - bc)).astype(BF16))
        gl.append(jnp.exp(bl))
        if h == C_HEADS - 1:
            qk += [jnp.where(lower, _dotg(qd[m], kinv[m], NT_DIMS), 0.0).astype(BF16)
                   for m in range(n - C_HEADS + 1, n + 1)]
            if c % every == every - 1:
                run_later(1)
    run_later(C_HEADS - len(v_h))
    v_ = [v_h[h][rs(c)] for c, h in items]
    o_intra = []
    for c in range(nc):
        o_intra += [_dot(qk[c * C_HEADS + h], v_[c * C_HEADS + h]) for h in range(C_HEADS)]
        if c % every == every - 1:
            run_later(1)
    run_later(len(later))

    rb = min(ct, 4 * CHUNK)

    def gate_store(c, o):
        for h in range(C_HEADS):
            og_scr[rs(c), vs(h)] = (_rms(o[h], onorm_ref[...]) * _silu(z_h[h][rs(c)])).astype(BF16)
        if ((c + 1) * CHUNK) % rb == 0:
            r = slice((c + 1) * CHUNK - rb, (c + 1) * CHUNK)
            y_ref[0, r, :] = _rms(x[r] + _dot(og_scr[r, :], wout_ref[...]), gfin_ref[...])

    st = [st_scr[h] for h in range(C_HEADS)]
    o_prev = None
    pad_rows = jnp.zeros((SUBLANES - C_HEADS, C_DK), F32)
    for c in range(nc):
        idx = [c * C_HEADS + h for h in range(C_HEADS)]
        glt = jnp.concatenate([gl[n] for n in idx] + [pad_rows], axis=0).T
        o = [o_intra[n] + _dot(qd[n], st[h].astype(BF16)) for h, n in enumerate(idx)]
        st = [st[h] * glt[:, h:h + 1] + _dotg(kd[n], v_[n], TN_DIMS) for h, n in enumerate(idx)]
        if o_prev is not None:
            gate_store(c - 1, o_prev)
        o_prev = o
    gate_store(nc - 1, o_prev)
    for h in range(C_HEADS):
        st_scr[h] = st[h]

    @pl.when(i == nsteps - 1)
    def _():
        sout_ref[0] = st_scr[...]


def _c_layer_prompt(x3, norm_g, w_main, w_gk, w_up, b_gk, onorm, w_out, final_g, st_in, ct, n_lead):
    B, L, _ = x3.shape
    kern = functools.partial(_c_layer_kernel, ct=ct, n_lead=n_lead)
    full = lambda a: pl.BlockSpec(a.shape, lambda b, i: (0,) * a.ndim)
    blk = pl.BlockSpec((1, ct, D_MODEL), lambda b, i: (b, i, 0))
    st = pl.BlockSpec((1, C_HEADS, C_DK, C_DV), lambda b, i: (b, 0, 0, 0))
    return pl.pallas_call(
        kern,
        grid=(B, L // ct),
        in_specs=[blk, full(norm_g), full(w_main), full(w_gk), full(w_up), full(b_gk), full(onorm),
                  full(w_out), full(final_g), full(st_in)],
        out_specs=[blk, st],
        out_shape=[jax.ShapeDtypeStruct((B, L, D_MODEL), F32),
                   jax.ShapeDtypeStruct((B, C_HEADS, C_DK, C_DV), F32)],
        scratch_shapes=[pltpu.VMEM((C_HEADS, C_DK, C_DV), F32), pltpu.VMEM((ct, C_VAL), BF16)],
        compiler_params=_cparams(("arbitrary", "arbitrary")),
        name="c_layer_prompt",
    )(x3, norm_g, w_main, w_gk, w_up, b_gk, onorm, w_out, final_g, st_in)


def _ab_proj_kernel(x_ref, g_ref, w_ref, wg_ref, qa_ref, kv_ref, za_ref, xb_ref, zb_ref, gates_ref):
    hb = _rms(x_ref[...], g_ref[...]).astype(BF16)
    qa_ref[...] = _dot(hb, w_ref[:, O_QA:O_KV])
    kv_ref[...] = _dot(hb, w_ref[:, O_KV:O_ZA])
    za_ref[...] = _dot(hb, w_ref[:, O_ZA:O_XB])
    xb_ref[...] = _dot(hb, w_ref[:, O_XB:O_ZB])
    zb_ref[...] = _dot(hb, w_ref[:, O_ZB:AB_MAIN])
    gates_ref[...] = _dot(hb, wg_ref[...])


def _ab_proj(x2d, norm_g, w_main, w_gate):
    n = x2d.shape[0]
    full = lambda a: pl.BlockSpec(a.shape, lambda: (0,) * a.ndim)
    widths = (A_WIDTH, 2 * A_KV_WIDTH, A_WIDTH, B_CONV_CH, B_VAL, LANES)
    args = (x2d, norm_g, w_main, w_gate)
    return pl.pallas_call(
        _ab_proj_kernel,
        in_specs=[full(a) for a in args],
        out_specs=[pl.BlockSpec((n, w), lambda: (0, 0)) for w in widths],
        out_shape=[jax.ShapeDtypeStruct((n, w), F32) for w in widths],
        compiler_params=pltpu.CompilerParams(vmem_limit_bytes=VMEM_LIMIT),
        name="ab_in_proj_sample",
    )(*args)


def _swa_sample_kernel(q_ref, kvn_ref, za_ref, ck_ref, cv_ref, sink_ref, o_ref, nk_ref, nv_ref, *, bt):
    row = lax.broadcasted_iota(jnp.int32, (WINDOW, A_KV_WIDTH), 0)
    hrow = lax.broadcasted_iota(jnp.int32, (A_HEADS, A_HD), 0)
    sk = sink_ref[...]
    nkb, nvb = [], []
    for b in range(bt):
        nk = jnp.where(row == WINDOW - 1, kvn_ref[b:b + 1, 0:A_KV_WIDTH], pltpu.roll(ck_ref[b], WINDOW - 1, 0))
        nv = jnp.where(row == WINDOW - 1, kvn_ref[b:b + 1, A_KV_WIDTH:], pltpu.roll(cv_ref[b], WINDOW - 1, 0))
        nk_ref[b] = nk
        nv_ref[b] = nv
        nkb.append(nk.astype(BF16))
        nvb.append(nv.astype(BF16))
    items = [(b, j) for b in range(bt) for j in range(A_KV_HEADS)]
    q8 = [q_ref[b].astype(BF16) for b in range(bt)]
    sc = [_dotg(q8[b], nkb[b][:, j * A_HD:(j + 1) * A_HD], NT_DIMS) * (A_HD ** -0.5) for b, j in items]
    m = [jnp.maximum(jnp.max(s, axis=-1, keepdims=True), sk) for s in sc]
    p = [jnp.exp(sc[n] - m[n]) for n in range(len(items))]
    den = [jnp.sum(p[n], axis=-1, keepdims=True) + jnp.exp(sk - m[n]) for n in range(len(items))]
    oj = [_dot(p[n].astype(BF16), nvb[b][:, j * A_HD:(j + 1) * A_HD]) / den[n] for n, (b, j) in enumerate(items)]
    for b in range(bt):
        o8 = jnp.where(hrow >= A_GROUP, oj[b * A_KV_HEADS + 1], oj[b * A_KV_HEADS])
        o_ref[b] = o8 * _silu(za_ref[b])


def _swa_sample(q3, kv_new, za3, cache_k, cache_v, sink_col, bt):
    nb = q3.shape[0]
    hd = pl.BlockSpec((bt, A_HEADS, A_HD), lambda i: (i, 0, 0))
    cache = pl.BlockSpec((bt, WINDOW, A_KV_WIDTH), lambda i: (i, 0, 0))
    return pl.pallas_call(
        functools.partial(_swa_sample_kernel, bt=bt),
        grid=(nb // bt,),
        in_specs=[hd, pl.BlockSpec((bt, 2 * A_KV_WIDTH), lambda i: (i, 0)), hd, cache, cache,
                  pl.BlockSpec(sink_col.shape, lambda i: (0, 0))],
        out_specs=[hd, cache, cache],
        out_shape=[jax.ShapeDtypeStruct(q3.shape, F32),
                   jax.ShapeDtypeStruct(cache_k.shape, F32),
                   jax.ShapeDtypeStruct(cache_v.shape, F32)],
        compiler_params=_cparams(("arbitrary",)),
        name="swa_sample",
    )(q3, kv_new, za3, cache_k, cache_v, sink_col)


def _gdn_sample_prep_kernel(xb_ref, h0_ref, h1_ref, h2_ref, gates_ref, convw_ref, gprm_ref,
                            q_ref, k_ref, v_ref, bg_ref):
    w = convw_ref[...]
    y = (h0_ref[...] * w[0:1, :] + h1_ref[...] * w[1:2, :] + h2_ref[...] * w[2:3, :]
         + xb_ref[...] * w[3:4, :])
    cv = _silu(y)
    for h in range(B_HEADS):
        q = cv[:, h * B_DK:(h + 1) * B_DK]
        k = cv[:, B_KEY + h * B_DK:B_KEY + (h + 1) * B_DK]
        q_ref[:, h * B_DK:(h + 1) * B_DK] = (
            q * lax.rsqrt(jnp.sum(q * q, axis=-1, keepdims=True) + EPS) * (B_DK ** -0.5))
        k_ref[:, h * B_DK:(h + 1) * B_DK] = k * lax.rsqrt(jnp.sum(k * k, axis=-1, keepdims=True) + EPS)
    v_ref[...] = cv[:, 2 * B_KEY:]
    gt = gates_ref[...]
    col = lax.broadcasted_iota(jnp.int32, gt.shape, 1)
    g = -gprm_ref[0:1, :] * _softplus(gt + gprm_ref[1:2, :])
    bg_ref[...] = jnp.where(col < B_HEADS, _sigmoid(gt), jnp.exp(g))


def _gdn_sample_prep(xb, h0, h1, h2, gates, conv_w8, gprm):
    n = xb.shape[0]
    full = lambda a: pl.BlockSpec(a.shape, lambda: (0,) * a.ndim)
    args = (xb, h0, h1, h2, gates, conv_w8, gprm)
    return pl.pallas_call(
        _gdn_sample_prep_kernel,
        in_specs=[full(a) for a in args],
        out_specs=[pl.BlockSpec((n, B_KEY), lambda: (0, 0))] * 3 + [pl.BlockSpec((n, LANES), lambda: (0, 0))],
        out_shape=[jax.ShapeDtypeStruct((n, B_KEY), F32)] * 3 + [jax.ShapeDtypeStruct((n, LANES), F32)],
        compiler_params=pltpu.CompilerParams(vmem_limit_bytes=VMEM_LIMIT),
        name="gdn_sample_prep",
    )(*args)


def _gdn_sample_kernel(s_ref, qt_ref, kt_ref, v_ref, bg_ref, zb_ref, onorm_ref, o_ref, sout_ref, *, bt):
    items = [(b, h) for b in range(bt) for h in range(B_HEADS)]
    hs = lambda h: slice(h * B_DV, (h + 1) * B_DV)
    kt = [kt_ref[:, hs(h)].T for h in range(B_HEADS)]
    qt = [qt_ref[:, hs(h)].T for h in range(B_HEADS)]
    kc = [kt[h][:, b:b + 1] for b, h in items]
    eg = [bg_ref[b:b + 1, B_HEADS + h:B_HEADS + h + 1] for b, h in items]
    ks = [jnp.sum(kc[n] * s_ref[b, h], axis=0, keepdims=True) for n, (b, h) in enumerate(items)]
    v_new = [bg_ref[b:b + 1, h:h + 1] * (v_ref[b:b + 1, hs(h)] - eg[n] * ks[n]) for n, (b, h) in enumerate(items)]
    o = []
    for n, (b, h) in enumerate(items):
        s_new = eg[n] * s_ref[b, h] + kc[n] * v_new[n]
        sout_ref[b, h] = s_new
        o.append(jnp.sum(qt[h][:, b:b + 1] * s_new, axis=0, keepdims=True))
    o_blk = jnp.concatenate([jnp.concatenate([o[b * B_HEADS + h] for h in range(B_HEADS)], axis=-1)
                             for b in range(bt)], axis=0)
    for h in range(B_HEADS):
        o_ref[:, hs(h)] = _rms(o_blk[:, hs(h)], onorm_ref[...]) * _silu(zb_ref[:, hs(h)])


def _gdn_sample(state, qt, kt, v, bg, zb, onorm, bt):
    nb = state.shape[0]
    st = pl.BlockSpec((bt, B_HEADS, B_DK, B_DV), lambda i: (i, 0, 0, 0))
    row = lambda n: pl.BlockSpec((bt, n), lambda i: (i, 0))
    return pl.pallas_call(
        functools.partial(_gdn_sample_kernel, bt=bt),
        grid=(nb // bt,),
        in_specs=[st, row(B_KEY), row(B_KEY), row(B_VAL), row(LANES), row(B_VAL),
                  pl.BlockSpec(onorm.shape, lambda i: (0, 0))],
        out_specs=[row(B_VAL), st],
        out_shape=[jax.ShapeDtypeStruct((nb, B_VAL), F32), jax.ShapeDtypeStruct(state.shape, F32)],
        compiler_params=_cparams(("arbitrary",)),
        name="gdn_sample",
    )(state, qt, kt, v, bg, zb, onorm)


def _ab_out_c_proj_kernel(x_ref, oa_ref, ob_ref, wout_ref, g_ref, w_ref, wgk_ref, wup_ref, bgk_ref,
                          y_ref, q_ref, k_ref, v_ref, z_ref, la_ref):
    y = (x_ref[...] + _dot(oa_ref[...].astype(BF16), wout_ref[0:A_WIDTH, :])
         + _dot(ob_ref[...].astype(BF16), wout_ref[A_WIDTH:, :]))
    y_ref[...] = y
    hb = _rms(y, g_ref[...]).astype(BF16)
    q_ref[...] = _dot(hb, w_ref[:, 0:C_KEY])
    k_ref[...] = _dot(hb, w_ref[:, C_KEY:2 * C_KEY])
    v_ref[...] = _dot(hb, w_ref[:, 2 * C_KEY:2 * C_KEY + C_VAL])
    z_ref[...] = _dot(hb, w_ref[:, 2 * C_KEY + C_VAL:C_MAIN])
    la_ref[...] = _gla_log_decay(_dot(hb, wgk_ref[...]), wup_ref, bgk_ref)


def _ab_out_c_proj(x2d, oa, ob, w_out, norm_g, w_main, w_gk, w_up, b_gk):
    n = x2d.shape[0]
    full = lambda a: pl.BlockSpec(a.shape, lambda: (0,) * a.ndim)
    widths = (D_MODEL, C_KEY, C_KEY, C_VAL, C_VAL, C_KEY)
    args = (x2d, oa, ob, w_out, norm_g, w_main, w_gk, w_up, b_gk)
    return pl.pallas_call(
        _ab_out_c_proj_kernel,
        in_specs=[full(a) for a in args],
        out_specs=[pl.BlockSpec((n, w), lambda: (0, 0)) for w in widths],
        out_shape=[jax.ShapeDtypeStruct((n, w), F32) for w in widths],
        compiler_params=pltpu.CompilerParams(vmem_limit_bytes=VMEM_LIMIT),
        name="ab_out_c_in_proj_sample",
    )(*args)


def _gla_sample_kernel(s_ref, qt_ref, kt_ref, lat_ref, v_ref, z_ref, onorm_ref, o_ref, sout_ref, *, bt):
    items = [(b, h) for b in range(bt) for h in range(C_HEADS)]
    hs = lambda h: slice(h * C_DV, (h + 1) * C_DV)
    ks = lambda h: slice(h * C_DK, (h + 1) * C_DK)
    at = [jnp.exp(lat_ref[:, ks(h)]).T for h in range(C_HEADS)]
    kt = [kt_ref[:, ks(h)].T for h in range(C_HEADS)]
    qt = [(qt_ref[:, ks(h)] * (C_DK ** -0.5)).T for h in range(C_HEADS)]
    o = []
    for n, (b, h) in enumerate(items):
        s_new = at[h][:, b:b + 1] * s_ref[b, h] + kt[h][:, b:b + 1] * v_ref[b:b + 1, hs(h)]
        sout_ref[b, h] = s_new
        o.append(jnp.sum(qt[h][:, b:b + 1] * s_new, axis=0, keepdims=True))
    o_blk = jnp.concatenate([jnp.concatenate([o[b * C_HEADS + h] for h in range(C_HEADS)], axis=-1)
                             for b in range(bt)], axis=0)
    for h in range(C_HEADS):
        o_ref[:, hs(h)] = _rms(o_blk[:, hs(h)], onorm_ref[...]) * _silu(z_ref[:, hs(h)])


def _gla_sample(state, qt, kt, lat, v, z, onorm, bt):
    nb = state.shape[0]
    st = pl.BlockSpec((bt, C_HEADS, C_DK, C_DV), lambda i: (i, 0, 0, 0))
    row = lambda n: pl.BlockSpec((bt, n), lambda i: (i, 0))
    return pl.pallas_call(
        functools.partial(_gla_sample_kernel, bt=bt),
        grid=(nb // bt,),
        in_specs=[st, row(C_KEY), row(C_KEY), row(C_KEY), row(C_VAL), row(C_VAL),
                  pl.BlockSpec(onorm.shape, lambda i: (0, 0))],
        out_specs=[row(C_VAL), st],
        out_shape=[jax.ShapeDtypeStruct((nb, C_VAL), F32), jax.ShapeDtypeStruct(state.shape, F32)],
        compiler_params=_cparams(("arbitrary",)),
        name="gla_sample",
    )(state, qt, kt, lat, v, z, onorm)


def _c_out_kernel(x_ref, o_ref, w_ref, g_ref, y_ref):
    y_ref[...] = _rms(x_ref[...] + _dot(o_ref[...].astype(BF16), w_ref[...]), g_ref[...])


def _c_out(x2d, oc, w_out, final_g):
    n = x2d.shape[0]
    full = lambda a: pl.BlockSpec(a.shape, lambda: (0,) * a.ndim)
    args = (x2d, oc, w_out, final_g)
    return pl.pallas_call(
        _c_out_kernel,
        in_specs=[full(a) for a in args],
        out_specs=pl.BlockSpec((n, D_MODEL), lambda: (0, 0)),
        out_shape=jax.ShapeDtypeStruct((n, D_MODEL), F32),
        compiler_params=pltpu.CompilerParams(vmem_limit_bytes=VMEM_LIMIT),
        name="c_out_proj_norm_sample",
    )(*args)


def kernel(x_prompt, x_sample, cache_swa_k, cache_swa_v, state_dn_conv, state_dn, state_gla,
           meta_tokens, norm_ab, w_in_ab, sink_a, conv_b, a_log_b, dt_bias_b, onorm_b, w_out_ab,
           norm_c, w_in_c, w_gk_up, b_gk, onorm_c, w_out_c, final_norm):
    Bp, L, _ = x_prompt.shape
    Bs = x_sample.shape[0]
    assert L % PROMPT_TILE == 0 and Bs % SAMPLE_BLOCK == 0

    w_ab = w_in_ab[0]
    w_ab_main = w_ab.astype(BF16)
    w_ab_gate = jnp.pad(w_ab[:, AB_MAIN:], ((0, 0), (0, LANES - 2 * B_HEADS))).astype(BF16)
    w_ab_gate_t = jnp.pad(w_ab[:, AB_MAIN:].T, ((0, BF16_ROWS - 2 * B_HEADS), (0, 0))).astype(BF16)
    g_ab = norm_ab[0][None, :]
    sink_row = sink_a[0][None, :]
    sink_col = sink_a[0][:, None]
    conv_w8 = jnp.pad(conv_b[0], ((0, SUBLANES - CONV_W), (0, 0)))
    gp = jnp.stack([jnp.exp(a_log_b[0]), dt_bias_b[0]])
    gprm = jnp.pad(gp, ((0, SUBLANES - 2), (B_HEADS, LANES - 2 * B_HEADS)))
    gprm_t = jnp.pad(gp.T, ((B_HEADS, BF16_ROWS - 2 * B_HEADS), (0, LANES - 2)))
    onb = onorm_b[0][None, :]
    w_ab_out = w_out_ab[0].astype(BF16)
    w_c = w_in_c[0]
    w_c_main = w_c.astype(BF16)
    w_c_gk = jnp.pad(w_c[:, C_MAIN:], ((0, 0), (0, LANES - C_RANK))).astype(BF16)
    w_up = jnp.pad(w_gk_up[0], ((0, LANES - C_RANK), (0, 0)))
    bgk = b_gk[0][None, :]
    g_c = norm_c[0][None, :]
    onc = onorm_c[0][None, :]
    w_c_out = w_out_c[0].astype(BF16)
    g_fin = final_norm[None, :]

    def ab_layer(x3, ct, hist_kv, pos_shift, conv_in, s_in, n_lead):
        return _ab_layer_prompt(x3, g_ab, w_ab_main, w_ab_gate, w_ab_gate_t, sink_row, hist_kv, conv_w8,
                                gprm, gprm_t, onb, w_ab_out, conv_in, s_in, ct, n_lead, pos_shift)

    def c_layer(x3, ct, st_in, n_lead):
        return _c_layer_prompt(x3, g_c, w_c_main, w_c_gk, w_up, bgk, onc, w_c_out, g_fin, st_in, ct, n_lead)

    xpre = jnp.concatenate([jnp.zeros((LEAD, D_MODEL), F32), meta_tokens], axis=0)[None]
    y_pre, kv_pre, xbtail_pre, sdn_pre = ab_layer(
        xpre, CHUNK, jnp.zeros((WINDOW, 2 * A_KV_WIDTH), F32), -LEAD,
        jnp.zeros((1, SUBLANES, B_CONV_CH), F32), jnp.zeros((1, B_HEADS, B_DK, B_DV), F32), LEAD)
    _, sgla_pre = c_layer(y_pre, CHUNK, jnp.zeros((1, C_HEADS, C_DK, C_DV), F32), LEAD)

    hist_kv = jnp.concatenate([jnp.zeros((WINDOW - CHUNK, 2 * A_KV_WIDTH), F32), kv_pre[0]], axis=0)
    y1, kv_tail, xb_tail, sdn_p = ab_layer(
        x_prompt, PROMPT_TILE, hist_kv, N_META, xbtail_pre, sdn_pre, 0)
    y_prompt, sgla_p = c_layer(y1, PROMPT_TILE, sgla_pre, 0)

    swa_k_p = kv_tail[:, :, :A_KV_WIDTH].reshape(1, Bp, WINDOW, A_KV_HEADS, A_HD)
    swa_v_p = kv_tail[:, :, A_KV_WIDTH:].reshape(1, Bp, WINDOW, A_KV_HEADS, A_HD)
    dn_conv_p = xb_tail[:, SUBLANES - (CONV_W - 1):, :][None]

    xs = x_sample.reshape(Bs, D_MODEL)
    qa, kvn, za, xb, zb, gates = _ab_proj(xs, g_ab, w_ab_main, w_ab_gate)
    oa3, nk, nv = _swa_sample(qa.reshape(Bs, A_HEADS, A_HD), kvn, za.reshape(Bs, A_HEADS, A_HD),
                              cache_swa_k[0].reshape(Bs, WINDOW, A_KV_WIDTH),
                              cache_swa_v[0].reshape(Bs, WINDOW, A_KV_WIDTH), sink_col, 2 * SAMPLE_BLOCK)
    hist = state_dn_conv[0]
    qn, kn, vn, bg = _gdn_sample_prep(xb, hist[:, 0], hist[:, 1], hist[:, 2], gates, conv_w8, gprm)
    ob, sdn_s = _gdn_sample(state_dn[0], qn, kn, vn, bg, zb, onb, 2 * SAMPLE_BLOCK)
    ys, qc, kc, vc, zc, la = _ab_out_c_proj(xs, oa3.reshape(Bs, A_WIDTH), ob, w_ab_out,
                                            g_c, w_c_main, w_c_gk, w_up, bgk)
    ocs, sgla_s = _gla_sample(state_gla[0], qc, kc, la, vc, zc, onc, SAMPLE_BLOCK)
    y_sample = _c_out(ys, ocs, w_c_out, g_fin).reshape(Bs, 1, D_MODEL)
    dn_conv_s = jnp.concatenate([hist[:, 1:], xb[:, None, :]], axis=1)[None]

    return (y_prompt, y_sample, swa_k_p, swa_v_p, dn_conv_p, sdn_p[None], sgla_p[None],
            nk.reshape(1, Bs, WINDOW, A_KV_HEADS, A_HD), nv.reshape(1, Bs, WINDOW, A_KV_HEADS, A_HD),
            dn_conv_s, sdn_s[None], sgla_s[None])
```

```python
import functools

import jax
import jax.numpy as jnp
from jax import lax
from jax.experimental import pallas as pl
from jax.experimental.pallas import tpu as pltpu

F32 = jnp.float32
BF16 = jnp.bfloat16
EPS = 1e-6

D_MODEL = 1024
N_META = 16
CHUNK = 64
LEAD = (-N_META) % CHUNK
A_HEADS, A_KV_HEADS, A_HD = 8, 2, 64
A_GROUP = A_HEADS // A_KV_HEADS
A_WIDTH = A_HEADS * A_HD
A_KV_WIDTH = A_KV_HEADS * A_HD
WINDOW = 128
B_HEADS, B_DK, B_DV = 4, 128, 128
B_KEY = B_HEADS * B_DK
B_VAL = B_HEADS * B_DV
CONV_W = 4
B_CONV_CH = 2 * B_KEY + B_VAL
C_HEADS, C_DK, C_DV = 4, 128, 256
C_KEY = C_HEADS * C_DK
C_VAL = C_HEADS * C_DV
C_RANK = 16
C_GATE_NORM = 16.0
AB_MAIN = 2 * A_WIDTH + 2 * A_KV_WIDTH + B_CONV_CH + B_VAL
AB_MIX = A_WIDTH + B_VAL
C_MAIN = 2 * C_KEY + 2 * C_VAL
O_QA, O_KV, O_ZA, O_XB, O_ZB = 0, 512, 768, 1280, 2816

LANES = 128
SUBLANES = 8
BF16_ROWS = 16
MXU_WIDTH = 2 * LANES
VMEM_LIMIT = 56 * 1024 * 1024

PROMPT_TILE = 512
SAMPLE_BLOCK = 16
NEG = -1e30

NT_DIMS = (((1,), (1,)), ((), ()))
TN_DIMS = (((0,), (0,)), ((), ()))


def _cparams(sem):
    return pltpu.CompilerParams(dimension_semantics=sem, vmem_limit_bytes=VMEM_LIMIT)


def _dot(a, b):
    return jnp.dot(a, b, preferred_element_type=F32)


def _dotg(a, b, dims):
    return lax.dot_general(a, b, dims, preferred_element_type=F32)


def _split3(a):
    hi = a.astype(BF16)
    r = a - hi.astype(F32)
    mid = r.astype(BF16)
    lo = (r - mid.astype(F32)).astype(BF16)
    return hi, mid, lo


def _split2(a):
    hi = a.astype(BF16)
    return hi, (a - hi.astype(F32)).astype(BF16)


def _sigmoid(x):
    return 1.0 / (1.0 + jnp.exp(-x))


def _silu(x):
    return x * _sigmoid(x)


def _softplus(x):
    return jnp.maximum(x, 0.0) + jnp.log1p(jnp.exp(-jnp.abs(x)))


def _log_sigmoid(x):
    return jnp.minimum(x, 0.0) - jnp.log1p(jnp.exp(-jnp.abs(x)))


def _rms(x, g):
    return x * lax.rsqrt(jnp.mean(x * x, axis=-1, keepdims=True) + EPS) * g


def _gla_log_decay(low, wup_ref, bgk_ref):
    lh, lm, _ = _split3(low)
    wh, wm, _ = _split3(wup_ref[...])
    up = _dot(lh, wh) + _dot(lh, wm) + _dot(lm, wh)
    return _log_sigmoid(up + bgk_ref[...]) * (1.0 / C_GATE_NORM)


def _swa_stages(i, qa, kvb, env, sink_ref, *, ct, sb, pos_shift):
    r = lax.broadcasted_iota(jnp.int32, (sb, WINDOW + sb), 0)
    c = lax.broadcasted_iota(jnp.int32, (sb, WINDOW + sb), 1)
    diff = r - c + WINDOW
    in_window = (diff >= 0) & (diff < WINDOW)
    nsb = ct // sb
    items = [(s, j) for s in range(nsb) for j in range(A_KV_HEADS)]
    masks = [jnp.concatenate([in_window & (i * ct + s * sb + c - WINDOW + pos_shift >= 0)] * A_GROUP, axis=0)
             for s in range(nsb)]
    sinks = [jnp.concatenate([jnp.broadcast_to(sink_ref[0:1, j * A_GROUP + g:j * A_GROUP + g + 1], (sb, 1))
                              for g in range(A_GROUP)], axis=0) for j in range(A_KV_HEADS)]
    sc = []
    for s, j in items:
        qs = (jnp.concatenate([qa[s * sb:(s + 1) * sb, (j * A_GROUP + g) * A_HD:(j * A_GROUP + g + 1) * A_HD]
                               for g in range(A_GROUP)], axis=0) * (A_HD ** -0.5)).astype(BF16)
        sc.append(_dotg(qs, kvb[s * sb:s * sb + WINDOW + sb, j * A_HD:(j + 1) * A_HD], NT_DIMS))
        yield
    p, esk = [], []
    for n, (s, j) in enumerate(items):
        scm = jnp.where(masks[s], sc[n], NEG)
        mx = jnp.maximum(jnp.max(scm, axis=-1, keepdims=True), sinks[j])
        p.append(jnp.exp(scm - mx).astype(BF16))
        esk.append(jnp.exp(sinks[j] - mx))
        yield
    pv = []
    ones = jnp.ones((WINDOW + sb, A_HD), BF16)
    for n, (s, j) in enumerate(items):
        vx = jnp.concatenate([kvb[s * sb:s * sb + WINDOW + sb,
                                  A_KV_WIDTH + j * A_HD:A_KV_WIDTH + (j + 1) * A_HD], ones], axis=-1)
        pvx = _dot(p[n], vx)
        pv.append(pvx[:, :A_HD] / (pvx[:, A_HD:] + esk[n]))
        yield
    env['pv'] = pv


def _swa_store(env, mix_scr, *, ct, sb):
    pv = env['pv']
    for s in range(ct // sb):
        o = jnp.concatenate([pv[s * A_KV_HEADS + j][g * sb:(g + 1) * sb]
                             for j in range(A_KV_HEADS) for g in range(A_GROUP)], axis=-1)
        mix_scr[s * sb:(s + 1) * sb, 0:A_WIDTH] = (o * _silu(env['za'][s * sb:(s + 1) * sb])).astype(BF16)


def _gdn_stages(i, cvb, gt, gtt, env, gprm_ref, gprmt_ref, onorm_ref, s_scr, mix_scr, *, ct, n_lead):
    nc = ct // CHUNK
    items = [(c, h) for c in range(nc) for h in range(B_HEADS)]
    n_items = len(items)
    rs = lambda c: slice(c * CHUNK, (c + 1) * CHUNK)
    qn, kn = [], []
    for h in range(B_HEADS):
        q = cvb[h]
        k = cvb[B_HEADS + h]
        qn.append(q * lax.rsqrt(jnp.sum(q * q, axis=-1, keepdims=True) + EPS) * (B_DK ** -0.5))
        kn.append(k * lax.rsqrt(jnp.sum(k * k, axis=-1, keepdims=True) + EPS))
        yield
    beta = _sigmoid(gt)
    rows = i * ct + lax.broadcasted_iota(jnp.int32, gt.shape, 0)
    gcol = jnp.where(rows >= n_lead, -gprm_ref[0:1, :] * _softplus(gt + gprm_ref[1:2, :]), 0.0)
    lanes = i * ct + lax.broadcasted_iota(jnp.int32, gtt.shape, 1)
    grow = jnp.where(lanes >= n_lead, -gprmt_ref[:, 0:1] * _softplus(gtt + gprmt_ref[:, 1:2]), 0.0)
    ii = lax.broadcasted_iota(jnp.int32, (CHUNK, CHUNK), 0)
    jj = lax.broadcasted_iota(jnp.int32, (CHUNK, CHUNK), 1)
    lower = ii >= jj
    strict = ii > jj
    ltri = lower.astype(BF16)
    utri = (ii <= jj).astype(BF16)
    gcum_c = [sum(_dot(ltri, p) for p in _split3(gcol[rs(c)])) for c in range(nc)]
    gcum_r = [sum(_dot(p, utri) for p in _split3(grow[:, rs(c)])) for c in range(nc)]
    yield
    q_ = [qn[h][rs(c)] for c, h in items]
    k_ = [kn[h][rs(c)] for c, h in items]
    v_ = [cvb[2 * B_HEADS + h][rs(c)] for c, h in items]
    bh = [beta[rs(c), h:h + 1] for c, h in items]
    gc = [gcum_c[c][:, B_HEADS + h:B_HEADS + h + 1] for c, h in items]
    gr = [gcum_r[c][B_HEADS + h:B_HEADS + h + 1, :] for c, h in items]
    kb, a, qk = [], [], []
    for n in range(n_items):
        decay = jnp.exp(jnp.where(lower, gc[n] - gr[n], NEG))
        kb.append(k_[n] * bh[n])
        kq = _dotg(jnp.concatenate([kb[n], q_[n]], axis=0).astype(BF16), k_[n].astype(BF16), NT_DIMS)
        a.append(jnp.where(strict, kq[:CHUNK] * decay, 0.0))
        qk.append(jnp.where(lower, kq[CHUNK:] * decay, 0.0).astype(BF16))
        if n % B_HEADS == B_HEADS - 1:
            yield
    doff = [-jnp.where((ii >> 1) == (jj >> 1), a[n], 0.0) for n in range(n_items)]
    for lg in range(1, 6):
        m = ((ii >> (lg + 1)) == (jj >> (lg + 1))) & ((ii >> lg) != (jj >> lg))
        bm = [jnp.where(m, a[n], 0.0) for n in range(n_items)]
        db = [doff[n].astype(BF16) for n in range(n_items)]
        y = [bm[n] + _dot(db[n], bm[n].astype(BF16)) for n in range(n_items)]
        yield
        x = [y[n] + _dot(y[n].astype(BF16), db[n]) for n in range(n_items)]
        doff = [doff[n] - x[n] for n in range(n_items)]
        yield
    egc = [jnp.exp(gc[n]) for n in range(n_items)]
    g_last = [gc[n][CHUNK - 1:CHUNK, :] for n in range(n_items)]
    eg_last = [jnp.exp(g_last[n]) for n in range(n_items)]
    sol, kd, wq = [], [], []
    for n in range(n_items):
        rhs = jnp.concatenate([v_[n] * bh[n], kb[n] * egc[n]], axis=-1)
        sol.append(rhs + _dot(doff[n].astype(BF16), rhs.astype(BF16)))
        kd.append((k_[n] * jnp.exp(g_last[n] - gc[n])).astype(BF16))
        wq.append(jnp.concatenate([sol[n][:, B_DV:], q_[n] * egc[n]], axis=0).astype(BF16))
        if n % B_HEADS == B_HEADS - 1:
            yield

    def gate_store(c, o):
        for h in range(B_HEADS):
            z = env['zb'][rs(c), h * B_DV:(h + 1) * B_DV]
            mix_scr[rs(c), A_WIDTH + h * B_DV:A_WIDTH + (h + 1) * B_DV] = (
                _rms(o[h], onorm_ref[...]) * _silu(z)).astype(BF16)

    s_cur = [s_scr[h] for h in range(B_HEADS)]
    o_prev = None
    for c in range(nc):
        idx = [c * B_HEADS + h for h in range(B_HEADS)]
        ws = [_dot(wq[n], s_cur[h].astype(BF16)) for h, n in enumerate(idx)]
        if o_prev is not None:
            gate_store(c - 1, o_prev)
        yield
        v_new = [sol[n][:, :B_DV] - ws[h][:CHUNK] for h, n in enumerate(idx)]
        vb = [v_new[h].astype(BF16) for h in range(B_HEADS)]
        o_prev = [ws[h][CHUNK:] + _dot(qk[n], vb[h]) for h, n in enumerate(idx)]
        s_cur = [s_cur[h] * eg_last[n] + _dotg(kd[n], vb[h], TN_DIMS) for h, n in enumerate(idx)]
        yield
    for h in range(B_HEADS):
        s_scr[h] = s_cur[h]
    gate_store(nc - 1, o_prev)
    yield


def _ab_layer_kernel(x_ref, g_ref, w_ref, wg_ref, wgt_ref, sink_ref, hist_ref, convw_ref, gprm_ref,
                     gprmt_ref, onorm_ref, wout_ref, convin_ref, sin_ref,
                     y_ref, kvtail_ref, xbtail_ref, sout_ref,
                     s_scr, xc_scr, kvprev_scr, mix_scr, *, ct, n_lead, pos_shift):
    i = pl.program_id(1)
    nsteps = pl.num_programs(1)
    ntail = min(WINDOW, ct)

    @pl.when(i == 0)
    def _():
        s_scr[...] = sin_ref[0]
        xc_scr[0:SUBLANES, :] = convin_ref[0]
        kvprev_scr[1] = hist_ref[...]

    x = x_ref[0]
    hb = _rms(x, g_ref[...]).astype(BF16)
    proj = lambda a, b: _dot(hb, w_ref[:, a:b])
    env = {}

    def step(gen, n=1):
        for _ in range(n):
            next(gen, None)

    nblk = B_CONV_CH // LANES
    w = convw_ref[...]
    cvb = []

    xblk = []
    row8 = lax.broadcasted_iota(jnp.int32, (SUBLANES, LANES), 0)

    def conv_block(j):
        cs = slice(j * LANES, (j + 1) * LANES)
        xj = xblk[j]
        h8 = xc_scr[0:SUBLANES, cs]
        yc = xj * w[CONV_W - 1:CONV_W, cs]
        for k in range(1, CONV_W):
            sh = pltpu.roll(xj, k, 0)
            top = jnp.where(row8 < k, pltpu.roll(h8, k, 0), sh[0:SUBLANES])
            yc = yc + jnp.concatenate([top, sh[SUBLANES:]], axis=0) * w[CONV_W - 1 - k:CONV_W - k, cs]
        cvb.append(_silu(yc))
        xc_scr[0:SUBLANES, cs] = xj[ct - SUBLANES:]

    pw = MXU_WIDTH
    for j in range(nblk // 2):
        xbj = proj(O_XB + j * pw, O_XB + (j + 1) * pw)
        xblk += [xbj[:, 0:LANES], xbj[:, LANES:]]
        xbtail_ref[0, :, j * pw:(j + 1) * pw] = xbj[ct - SUBLANES:]
        if j > 0:
            conv_block(2 * j - 2)
            conv_block(2 * j - 1)
    gt = _dot(hb, wg_ref[...])
    gtt = _dotg(wgt_ref[...], hb, NT_DIMS)
    conv_block(nblk - 2)
    conv_block(nblk - 1)

    gdn = _gdn_stages(i, cvb, gt, gtt, env, gprm_ref, gprmt_ref, onorm_ref, s_scr, mix_scr, ct=ct, n_lead=n_lead)
    qa_p = []
    for j in range(A_WIDTH // pw):
        qa_p.append(proj(O_QA + j * pw, O_QA + (j + 1) * pw))
        step(gdn, 2)
    qa = jnp.concatenate(qa_p, axis=-1)
    kv = proj(O_KV, O_ZA)
    kvtail_ref[0] = kv[ct - ntail:]
    slot = lax.rem(i, 2)
    kvb = jnp.concatenate([kvprev_scr[1 - slot], kv], axis=0).astype(BF16)
    if ct >= WINDOW:
        kvprev_scr[slot] = kv[ct - WINDOW:]
    swa = _swa_stages(i, qa, kvb, env, sink_ref, ct=ct, sb=ntail, pos_shift=pos_shift)
    n_sw = (ct // ntail) * A_KV_HEADS
    step(gdn)
    for _ in range(ct // CHUNK):
        step(gdn)
        step(swa)
    step(swa, max(n_sw - ct // CHUNK, 0))
    za_p, zb_p = [], []
    fill = ([lambda j=j: za_p.append(proj(O_ZA + j * pw, O_ZA + (j + 1) * pw)) for j in range(A_WIDTH // pw)]
            + [lambda j=j: zb_p.append(proj(O_ZB + j * pw, O_ZB + (j + 1) * pw)) for j in range(B_VAL // pw)])
    for lv in range(10):
        step(gdn)
        if lv % 2 == 1 and fill:
            fill.pop(0)()
    for f in fill:
        f()
    env['za'] = jnp.concatenate(za_p, axis=-1)
    env['zb'] = jnp.concatenate(zb_p, axis=-1)
    step(gdn, ct // CHUNK)
    for _ in range(2 * (ct // CHUNK)):
        step(gdn)
        step(swa, -(-2 * n_sw // (2 * (ct // CHUNK))))
    for _ in swa:
        pass
    _swa_store(env, mix_scr, ct=ct, sb=ntail)
    ya = x + _dot(mix_scr[:, 0:A_WIDTH], wout_ref[0:A_WIDTH, :])
    for _ in gdn:
        pass
    y_ref[0] = ya + _dot(mix_scr[:, A_WIDTH:], wout_ref[A_WIDTH:, :])

    @pl.when(i == nsteps - 1)
    def _():
        sout_ref[0] = s_scr[...]


def _ab_layer_prompt(x3, norm_g, w_main, w_gate, w_gate_t, sink, hist_kv, conv_w8, gprm, gprm_t, onorm,
                     w_out, conv_in, s_in, ct, n_lead, pos_shift):
    B, L, _ = x3.shape
    ntail = min(WINDOW, ct)
    kern = functools.partial(_ab_layer_kernel, ct=ct, n_lead=n_lead, pos_shift=pos_shift)
    full = lambda a: pl.BlockSpec(a.shape, lambda b, i: (0,) * a.ndim)
    blk = pl.BlockSpec((1, ct, D_MODEL), lambda b, i: (b, i, 0))
    per_b = lambda *s: pl.BlockSpec((1,) + s, lambda b, i: (b,) + (0,) * len(s))
    return pl.pallas_call(
        kern,
        grid=(B, L // ct),
        in_specs=[blk, full(norm_g), full(w_main), full(w_gate), full(w_gate_t), full(sink), full(hist_kv),
                  full(conv_w8), full(gprm), full(gprm_t), full(onorm), full(w_out),
                  full(conv_in), full(s_in)],
        out_specs=[blk, per_b(ntail, 2 * A_KV_WIDTH), per_b(SUBLANES, B_CONV_CH), per_b(B_HEADS, B_DK, B_DV)],
        out_shape=[jax.ShapeDtypeStruct((B, L, D_MODEL), F32),
                   jax.ShapeDtypeStruct((B, ntail, 2 * A_KV_WIDTH), F32),
                   jax.ShapeDtypeStruct((B, SUBLANES, B_CONV_CH), F32),
                   jax.ShapeDtypeStruct((B, B_HEADS, B_DK, B_DV), F32)],
        scratch_shapes=[pltpu.VMEM((B_HEADS, B_DK, B_DV), F32),
                        pltpu.VMEM((SUBLANES, B_CONV_CH), F32),
                        pltpu.VMEM((2, WINDOW, 2 * A_KV_WIDTH), F32),
                        pltpu.VMEM((ct, AB_MIX), BF16)],
        compiler_params=_cparams(("arbitrary", "arbitrary")),
        name="ab_layer_prompt",
    )(x3, norm_g, w_main, w_gate, w_gate_t, sink, hist_kv, conv_w8, gprm, gprm_t, onorm, w_out, conv_in, s_in)


def _c_layer_kernel(x_ref, g_ref, w_ref, wgk_ref, wup_ref, bgk_ref, onorm_ref, wout_ref, gfin_ref,
                    sin_ref, y_ref, sout_ref, st_scr, og_scr, *, ct, n_lead):
    i = pl.program_id(1)
    nsteps = pl.num_programs(1)

    @pl.when(i == 0)
    def _():
        st_scr[...] = sin_ref[0]

    x = x_ref[0]
    hb = _rms(x, g_ref[...]).astype(BF16)
    low = _dot(hb, wgk_ref[...])
    q_all = _dot(hb, w_ref[:, 0:C_KEY])
    la_all = _gla_log_decay(low, wup_ref, bgk_ref)
    k_all = _dot(hb, w_ref[:, C_KEY:2 * C_KEY])
    v_h, z_h = [], []
    later = ([lambda h=h: v_h.append(_dot(hb, w_ref[:, 2 * C_KEY + h * C_DV:2 * C_KEY + (h + 1) * C_DV]).astype(BF16))
              for h in range(C_HEADS)]
             + [lambda h=h: z_h.append(_dot(hb, w_ref[:, 2 * C_KEY + C_VAL + h * C_DV:
                                                       2 * C_KEY + C_VAL + (h + 1) * C_DV]))
                for h in range(C_HEADS)])

    def run_later(n):
        for _ in range(n):
            if later:
                later.pop(0)()
    rows = i * ct + lax.broadcasted_iota(jnp.int32, la_all.shape, 0)
    la_all = jnp.where(rows >= n_lead, la_all, 0.0)

    ii = lax.broadcasted_iota(jnp.int32, (CHUNK, CHUNK), 0)
    jj = lax.broadcasted_iota(jnp.int32, (CHUNK, CHUNK), 1)
    lower = ii >= jj
    ltri = lower.astype(BF16)
    nc = ct // CHUNK
    rs = lambda c: slice(c * CHUNK, (c + 1) * CHUNK)
    ks = lambda h: slice(h * C_DK, (h + 1) * C_DK)
    vs = lambda h: slice(h * C_DV, (h + 1) * C_DV)
    items = [(c, h) for c in range(nc) for h in range(C_HEADS)]
    bc_all = [sum(_dot(ltri, p) for p in _split2(la_all[rs(c)])) for c in range(nc)]
    run_later(2)
    every = max(nc // C_HEADS, 1)
    qd, kinv, kd, gl, qk = [], [], [], [], []
    for n, (c, h) in enumerate(items):
        bc = bc_all[c][:, ks(h)]
        k = k_all[rs(c), ks(h)]
        bl = bc[CHUNK - 1:CHUNK, :]
        qd.append((q_all[rs(c), ks(h)] * (C_DK ** -0.5) * jnp.exp(bc)).astype(BF16))
        kinv.append((k * jnp.exp(-bc)).astype(BF16))
        kd.append((k * jnp.exp(bl - bc)).astype(BF16))
        gl.append(jnp.exp(bl))
        if h == C_HEADS - 1:
            qk += [jnp.where(lower, _dotg(qd[m], kinv[m], NT_DIMS), 0.0).astype(BF16)
                   for m in range(n - C_HEADS + 1, n + 1)]
            if c % every == every - 1:
                run_later(1)
    run_later(C_HEADS - len(v_h))
    v_ = [v_h[h][rs(c)] for c, h in items]
    o_intra = []
    for c in range(nc):
        o_intra += [_dot(qk[c * C_HEADS + h], v_[c * C_HEADS + h]) for h in range(C_HEADS)]
        if c % every == every - 1:
            run_later(1)
    run_later(len(later))

    rb = min(ct, 4 * CHUNK)

    def gate_store(c, o):
        for h in range(C_HEADS):
            og_scr[rs(c), vs(h)] = (_rms(o[h], onorm_ref[...]) * _silu(z_h[h][rs(c)])).astype(BF16)
        if ((c + 1) * CHUNK) % rb == 0:
            r = slice((c + 1) * CHUNK - rb, (c + 1) * CHUNK)
            y_ref[0, r, :] = _rms(x[r] + _dot(og_scr[r, :], wout_ref[...]), gfin_ref[...])

    st = [st_scr[h] for h in range(C_HEADS)]
    o_prev = None
    pad_rows = jnp.zeros((SUBLANES - C_HEADS, C_DK), F32)
    for c in range(nc):
        idx = [c * C_HEADS + h for h in range(C_HEADS)]
        glt = jnp.concatenate([gl[n] for n in idx] + [pad_rows], axis=0).T
        o = [o_intra[n] + _dot(qd[n], st[h].astype(BF16)) for h, n in enumerate(idx)]
        st = [st[h] * glt[:, h:h + 1] + _dotg(kd[n], v_[n], TN_DIMS) for h, n in enumerate(idx)]
        if o_prev is not None:
            gate_store(c - 1, o_prev)
        o_prev = o
    gate_store(nc - 1, o_prev)
    for h in range(C_HEADS):
        st_scr[h] = st[h]

    @pl.when(i == nsteps - 1)
    def _():
        sout_ref[0] = st_scr[...]


def _c_layer_prompt(x3, norm_g, w_main, w_gk, w_up, b_gk, onorm, w_out, final_g, st_in, ct, n_lead):
    B, L, _ = x3.shape
    kern = functools.partial(_c_layer_kernel, ct=ct, n_lead=n_lead)
    full = lambda a: pl.BlockSpec(a.shape, lambda b, i: (0,) * a.ndim)
    blk = pl.BlockSpec((1, ct, D_MODEL), lambda b, i: (b, i, 0))
    st = pl.BlockSpec((1, C_HEADS, C_DK, C_DV), lambda b, i: (b, 0, 0, 0))
    return pl.pallas_call(
        kern,
        grid=(B, L // ct),
        in_specs=[blk, full(norm_g), full(w_main), full(w_gk), full(w_up), full(b_gk), full(onorm),
                  full(w_out), full(final_g), full(st_in)],
        out_specs=[blk, st],
        out_shape=[jax.ShapeDtypeStruct((B, L, D_MODEL), F32),
                   jax.ShapeDtypeStruct((B, C_HEADS, C_DK, C_DV), F32)],
        scratch_shapes=[pltpu.VMEM((C_HEADS, C_DK, C_DV), F32), pltpu.VMEM((ct, C_VAL), BF16)],
        compiler_params=_cparams(("arbitrary", "arbitrary")),
        name="c_layer_prompt",
    )(x3, norm_g, w_main, w_gk, w_up, b_gk, onorm, w_out, final_g, st_in)


def _ab_proj_kernel(x_ref, g_ref, w_ref, wg_ref, qa_ref, kv_ref, za_ref, xb_ref, zb_ref, gates_ref):
    hb = _rms(x_ref[...], g_ref[...]).astype(BF16)
    qa_ref[...] = _dot(hb, w_ref[:, O_QA:O_KV])
    kv_ref[...] = _dot(hb, w_ref[:, O_KV:O_ZA])
    za_ref[...] = _dot(hb, w_ref[:, O_ZA:O_XB])
    xb_ref[...] = _dot(hb, w_ref[:, O_XB:O_ZB])
    zb_ref[...] = _dot(hb, w_ref[:, O_ZB:AB_MAIN])
    gates_ref[...] = _dot(hb, wg_ref[...])


def _ab_proj(x2d, norm_g, w_main, w_gate):
    n = x2d.shape[0]
    full = lambda a: pl.BlockSpec(a.shape, lambda: (0,) * a.ndim)
    widths = (A_WIDTH, 2 * A_KV_WIDTH, A_WIDTH, B_CONV_CH, B_VAL, LANES)
    args = (x2d, norm_g, w_main, w_gate)
    return pl.pallas_call(
        _ab_proj_kernel,
        in_specs=[full(a) for a in args],
        out_specs=[pl.BlockSpec((n, w), lambda: (0, 0)) for w in widths],
        out_shape=[jax.ShapeDtypeStruct((n, w), F32) for w in widths],
        compiler_params=pltpu.CompilerParams(vmem_limit_bytes=VMEM_LIMIT),
        name="ab_in_proj_sample",
    )(*args)


def _swa_sample_kernel(q_ref, kvn_ref, za_ref, ck_ref, cv_ref, sink_ref, o_ref, nk_ref, nv_ref, *, bt):
    row = lax.broadcasted_iota(jnp.int32, (WINDOW, A_KV_WIDTH), 0)
    hrow = lax.broadcasted_iota(jnp.int32, (A_HEADS, A_HD), 0)
    sk = sink_ref[...]
    nkb, nvb = [], []
    for b in range(bt):
        nk = jnp.where(row == WINDOW - 1, kvn_ref[b:b + 1, 0:A_KV_WIDTH], pltpu.roll(ck_ref[b], WINDOW - 1, 0))
        nv = jnp.where(row == WINDOW - 1, kvn_ref[b:b + 1, A_KV_WIDTH:], pltpu.roll(cv_ref[b], WINDOW - 1, 0))
        nk_ref[b] = nk
        nv_ref[b] = nv
        nkb.append(nk.astype(BF16))
        nvb.append(nv.astype(BF16))
    items = [(b, j) for b in range(bt) for j in range(A_KV_HEADS)]
    q8 = [q_ref[b].astype(BF16) for b in range(bt)]
    sc = [_dotg(q8[b], nkb[b][:, j * A_HD:(j + 1) * A_HD], NT_DIMS) * (A_HD ** -0.5) for b, j in items]
    m = [jnp.maximum(jnp.max(s, axis=-1, keepdims=True), sk) for s in sc]
    p = [jnp.exp(sc[n] - m[n]) for n in range(len(items))]
    den = [jnp.sum(p[n], axis=-1, keepdims=True) + jnp.exp(sk - m[n]) for n in range(len(items))]
    oj = [_dot(p[n].astype(BF16), nvb[b][:, j * A_HD:(j + 1) * A_HD]) / den[n] for n, (b, j) in enumerate(items)]
    for b in range(bt):
        o8 = jnp.where(hrow >= A_GROUP, oj[b * A_KV_HEADS + 1], oj[b * A_KV_HEADS])
        o_ref[b] = o8 * _silu(za_ref[b])


def _swa_sample(q3, kv_new, za3, cache_k, cache_v, sink_col, bt):
    nb = q3.shape[0]
    hd = pl.BlockSpec((bt, A_HEADS, A_HD), lambda i: (i, 0, 0))
    cache = pl.BlockSpec((bt, WINDOW, A_KV_WIDTH), lambda i: (i, 0, 0))
    return pl.pallas_call(
        functools.partial(_swa_sample_kernel, bt=bt),
        grid=(nb // bt,),
        in_specs=[hd, pl.BlockSpec((bt, 2 * A_KV_WIDTH), lambda i: (i, 0)), hd, cache, cache,
                  pl.BlockSpec(sink_col.shape, lambda i: (0, 0))],
        out_specs=[hd, cache, cache],
        out_shape=[jax.ShapeDtypeStruct(q3.shape, F32),
                   jax.ShapeDtypeStruct(cache_k.shape, F32),
                   jax.ShapeDtypeStruct(cache_v.shape, F32)],
        compiler_params=_cparams(("arbitrary",)),
        name="swa_sample",
    )(q3, kv_new, za3, cache_k, cache_v, sink_col)


def _gdn_sample_prep_kernel(xb_ref, h0_ref, h1_ref, h2_ref, gates_ref, convw_ref, gprm_ref,
                            q_ref, k_ref, v_ref, bg_ref):
    w = convw_ref[...]
    y = (h0_ref[...] * w[0:1, :] + h1_ref[...] * w[1:2, :] + h2_ref[...] * w[2:3, :]
         + xb_ref[...] * w[3:4, :])
    cv = _silu(y)
    for h in range(B_HEADS):
        q = cv[:, h * B_DK:(h + 1) * B_DK]
        k = cv[:, B_KEY + h * B_DK:B_KEY + (h + 1) * B_DK]
        q_ref[:, h * B_DK:(h + 1) * B_DK] = (
            q * lax.rsqrt(jnp.sum(q * q, axis=-1, keepdims=True) + EPS) * (B_DK ** -0.5))
        k_ref[:, h * B_DK:(h + 1) * B_DK] = k * lax.rsqrt(jnp.sum(k * k, axis=-1, keepdims=True) + EPS)
    v_ref[...] = cv[:, 2 * B_KEY:]
    gt = gates_ref[...]
    col = lax.broadcasted_iota(jnp.int32, gt.shape, 1)
    g = -gprm_ref[0:1, :] * _softplus(gt + gprm_ref[1:2, :])
    bg_ref[...] = jnp.where(col < B_HEADS, _sigmoid(gt), jnp.exp(g))


def _gdn_sample_prep(xb, h0, h1, h2, gates, conv_w8, gprm):
    n = xb.shape[0]
    full = lambda a: pl.BlockSpec(a.shape, lambda: (0,) * a.ndim)
    args = (xb, h0, h1, h2, gates, conv_w8, gprm)
    return pl.pallas_call(
        _gdn_sample_prep_kernel,
        in_specs=[full(a) for a in args],
        out_specs=[pl.BlockSpec((n, B_KEY), lambda: (0, 0))] * 3 + [pl.BlockSpec((n, LANES), lambda: (0, 0))],
        out_shape=[jax.ShapeDtypeStruct((n, B_KEY), F32)] * 3 + [jax.ShapeDtypeStruct((n, LANES), F32)],
        compiler_params=pltpu.CompilerParams(vmem_limit_bytes=VMEM_LIMIT),
        name="gdn_sample_prep",
    )(*args)


def _gdn_sample_kernel(s_ref, qt_ref, kt_ref, v_ref, bg_ref, zb_ref, onorm_ref, o_ref, sout_ref, *, bt):
    items = [(b, h) for b in range(bt) for h in range(B_HEADS)]
    hs = lambda h: slice(h * B_DV, (h + 1) * B_DV)
    kt = [kt_ref[:, hs(h)].T for h in range(B_HEADS)]
    qt = [qt_ref[:, hs(h)].T for h in range(B_HEADS)]
    kc = [kt[h][:, b:b + 1] for b, h in items]
    eg = [bg_ref[b:b + 1, B_HEADS + h:B_HEADS + h + 1] for b, h in items]
    ks = [jnp.sum(kc[n] * s_ref[b, h], axis=0, keepdims=True) for n, (b, h) in enumerate(items)]
    v_new = [bg_ref[b:b + 1, h:h + 1] * (v_ref[b:b + 1, hs(h)] - eg[n] * ks[n]) for n, (b, h) in enumerate(items)]
    o = []
    for n, (b, h) in enumerate(items):
        s_new = eg[n] * s_ref[b, h] + kc[n] * v_new[n]
        sout_ref[b, h] = s_new
        o.append(jnp.sum(qt[h][:, b:b + 1] * s_new, axis=0, keepdims=True))
    o_blk = jnp.concatenate([jnp.concatenate([o[b * B_HEADS + h] for h in range(B_HEADS)], axis=-1)
                             for b in range(bt)], axis=0)
    for h in range(B_HEADS):
        o_ref[:, hs(h)] = _rms(o_blk[:, hs(h)], onorm_ref[...]) * _silu(zb_ref[:, hs(h)])


def _gdn_sample(state, qt, kt, v, bg, zb, onorm, bt):
    nb = state.shape[0]
    st = pl.BlockSpec((bt, B_HEADS, B_DK, B_DV), lambda i: (i, 0, 0, 0))
    row = lambda n: pl.BlockSpec((bt, n), lambda i: (i, 0))
    return pl.pallas_call(
        functools.partial(_gdn_sample_kernel, bt=bt),
        grid=(nb // bt,),
        in_specs=[st, row(B_KEY), row(B_KEY), row(B_VAL), row(LANES), row(B_VAL),
                  pl.BlockSpec(onorm.shape, lambda i: (0, 0))],
        out_specs=[row(B_VAL), st],
        out_shape=[jax.ShapeDtypeStruct((nb, B_VAL), F32), jax.ShapeDtypeStruct(state.shape, F32)],
        compiler_params=_cparams(("arbitrary",)),
        name="gdn_sample",
    )(state, qt, kt, v, bg, zb, onorm)


def _ab_out_c_proj_kernel(x_ref, oa_ref, ob_ref, wout_ref, g_ref, w_ref, wgk_ref, wup_ref, bgk_ref,
                          y_ref, q_ref, k_ref, v_ref, z_ref, la_ref):
    y = (x_ref[...] + _dot(oa_ref[...].astype(BF16), wout_ref[0:A_WIDTH, :])
         + _dot(ob_ref[...].astype(BF16), wout_ref[A_WIDTH:, :]))
    y_ref[...] = y
    hb = _rms(y, g_ref[...]).astype(BF16)
    q_ref[...] = _dot(hb, w_ref[:, 0:C_KEY])
    k_ref[...] = _dot(hb, w_ref[:, C_KEY:2 * C_KEY])
    v_ref[...] = _dot(hb, w_ref[:, 2 * C_KEY:2 * C_KEY + C_VAL])
    z_ref[...] = _dot(hb, w_ref[:, 2 * C_KEY + C_VAL:C_MAIN])
    la_ref[...] = _gla_log_decay(_dot(hb, wgk_ref[...]), wup_ref, bgk_ref)


def _ab_out_c_proj(x2d, oa, ob, w_out, norm_g, w_main, w_gk, w_up, b_gk):
    n = x2d.shape[0]
    full = lambda a: pl.BlockSpec(a.shape, lambda: (0,) * a.ndim)
    widths = (D_MODEL, C_KEY, C_KEY, C_VAL, C_VAL, C_KEY)
    args = (x2d, oa, ob, w_out, norm_g, w_main, w_gk, w_up, b_gk)
    return pl.pallas_call(
        _ab_out_c_proj_kernel,
        in_specs=[full(a) for a in args],
        out_specs=[pl.BlockSpec((n, w), lambda: (0, 0)) for w in widths],
        out_shape=[jax.ShapeDtypeStruct((n, w), F32) for w in widths],
        compiler_params=pltpu.CompilerParams(vmem_limit_bytes=VMEM_LIMIT),
        name="ab_out_c_in_proj_sample",
    )(*args)


def _gla_sample_kernel(s_ref, qt_ref, kt_ref, lat_ref, v_ref, z_ref, onorm_ref, o_ref, sout_ref, *, bt):
    items = [(b, h) for b in range(bt) for h in range(C_HEADS)]
    hs = lambda h: slice(h * C_DV, (h + 1) * C_DV)
    ks = lambda h: slice(h * C_DK, (h + 1) * C_DK)
    at = [jnp.exp(lat_ref[:, ks(h)]).T for h in range(C_HEADS)]
    kt = [kt_ref[:, ks(h)].T for h in range(C_HEADS)]
    qt = [(qt_ref[:, ks(h)] * (C_DK ** -0.5)).T for h in range(C_HEADS)]
    o = []
    for n, (b, h) in enumerate(items):
        s_new = at[h][:, b:b + 1] * s_ref[b, h] + kt[h][:, b:b + 1] * v_ref[b:b + 1, hs(h)]
        sout_ref[b, h] = s_new
        o.append(jnp.sum(qt[h][:, b:b + 1] * s_new, axis=0, keepdims=True))
    o_blk = jnp.concatenate([jnp.concatenate([o[b * C_HEADS + h] for h in range(C_HEADS)], axis=-1)
                             for b in range(bt)], axis=0)
    for h in range(C_HEADS):
        o_ref[:, hs(h)] = _rms(o_blk[:, hs(h)], onorm_ref[...]) * _silu(z_ref[:, hs(h)])


def _gla_sample(state, qt, kt, lat, v, z, onorm, bt):
    nb = state.shape[0]
    st = pl.BlockSpec((bt, C_HEADS, C_DK, C_DV), lambda i: (i, 0, 0, 0))
    row = lambda n: pl.BlockSpec((bt, n), lambda i: (i, 0))
    return pl.pallas_call(
        functools.partial(_gla_sample_kernel, bt=bt),
        grid=(nb // bt,),
        in_specs=[st, row(C_KEY), row(C_KEY), row(C_KEY), row(C_VAL), row(C_VAL),
                  pl.BlockSpec(onorm.shape, lambda i: (0, 0))],
        out_specs=[row(C_VAL), st],
        out_shape=[jax.ShapeDtypeStruct((nb, C_VAL), F32), jax.ShapeDtypeStruct(state.shape, F32)],
        compiler_params=_cparams(("arbitrary",)),
        name="gla_sample",
    )(state, qt, kt, lat, v, z, onorm)


def _c_out_kernel(x_ref, o_ref, w_ref, g_ref, y_ref):
    y_ref[...] = _rms(x_ref[...] + _dot(o_ref[...].astype(BF16), w_ref[...]), g_ref[...])


def _c_out(x2d, oc, w_out, final_g):
    n = x2d.shape[0]
    full = lambda a: pl.BlockSpec(a.shape, lambda: (0,) * a.ndim)
    args = (x2d, oc, w_out, final_g)
    return pl.pallas_call(
        _c_out_kernel,
        in_specs=[full(a) for a in args],
        out_specs=pl.BlockSpec((n, D_MODEL), lambda: (0, 0)),
        out_shape=jax.ShapeDtypeStruct((n, D_MODEL), F32),
        compiler_params=pltpu.CompilerParams(vmem_limit_bytes=VMEM_LIMIT),
        name="c_out_proj_norm_sample",
    )(*args)


def kernel(x_prompt, x_sample, cache_swa_k, cache_swa_v, state_dn_conv, state_dn, state_gla,
           meta_tokens, norm_ab, w_in_ab, sink_a, conv_b, a_log_b, dt_bias_b, onorm_b, w_out_ab,
           norm_c, w_in_c, w_gk_up, b_gk, onorm_c, w_out_c, final_norm):
    Bp, L, _ = x_prompt.shape
    Bs = x_sample.shape[0]
    assert L % PROMPT_TILE == 0 and Bs % SAMPLE_BLOCK == 0

    w_ab = w_in_ab[0]
    w_ab_main = w_ab.astype(BF16)
    w_ab_gate = jnp.pad(w_ab[:, AB_MAIN:], ((0, 0), (0, LANES - 2 * B_HEADS))).astype(BF16)
    w_ab_gate_t = jnp.pad(w_ab[:, AB_MAIN:].T, ((0, BF16_ROWS - 2 * B_HEADS), (0, 0))).astype(BF16)
    g_ab = norm_ab[0][None, :]
    sink_row = sink_a[0][None, :]
    sink_col = sink_a[0][:, None]
    conv_w8 = jnp.pad(conv_b[0], ((0, SUBLANES - CONV_W), (0, 0)))
    gp = jnp.stack([jnp.exp(a_log_b[0]), dt_bias_b[0]])
    gprm = jnp.pad(gp, ((0, SUBLANES - 2), (B_HEADS, LANES - 2 * B_HEADS)))
    gprm_t = jnp.pad(gp.T, ((B_HEADS, BF16_ROWS - 2 * B_HEADS), (0, LANES - 2)))
    onb = onorm_b[0][None, :]
    w_ab_out = w_out_ab[0].astype(BF16)
    w_c = w_in_c[0]
    w_c_main = w_c.astype(BF16)
    w_c_gk = jnp.pad(w_c[:, C_MAIN:], ((0, 0), (0, LANES - C_RANK))).astype(BF16)
    w_up = jnp.pad(w_gk_up[0], ((0, LANES - C_RANK), (0, 0)))
    bgk = b_gk[0][None, :]
    g_c = norm_c[0][None, :]
    onc = onorm_c[0][None, :]
    w_c_out = w_out_c[0].astype(BF16)
    g_fin = final_norm[None, :]

    def ab_layer(x3, ct, hist_kv, pos_shift, conv_in, s_in, n_lead):
        return _ab_layer_prompt(x3, g_ab, w_ab_main, w_ab_gate, w_ab_gate_t, sink_row, hist_kv, conv_w8,
                                gprm, gprm_t, onb, w_ab_out, conv_in, s_in, ct, n_lead, pos_shift)

    def c_layer(x3, ct, st_in, n_lead):
        return _c_layer_prompt(x3, g_c, w_c_main, w_c_gk, w_up, bgk, onc, w_c_out, g_fin, st_in, ct, n_lead)

    xpre = jnp.concatenate([jnp.zeros((LEAD, D_MODEL), F32), meta_tokens], axis=0)[None]
    y_pre, kv_pre, xbtail_pre, sdn_pre = ab_layer(
        xpre, CHUNK, jnp.zeros((WINDOW, 2 * A_KV_WIDTH), F32), -LEAD,
        jnp.zeros((1, SUBLANES, B_CONV_CH), F32), jnp.zeros((1, B_HEADS, B_DK, B_DV), F32), LEAD)
    _, sgla_pre = c_layer(y_pre, CHUNK, jnp.zeros((1, C_HEADS, C_DK, C_DV), F32), LEAD)

    hist_kv = jnp.concatenate([jnp.zeros((WINDOW - CHUNK, 2 * A_KV_WIDTH), F32), kv_pre[0]], axis=0)
    y1, kv_tail, xb_tail, sdn_p = ab_layer(
        x_prompt, PROMPT_TILE, hist_kv, N_META, xbtail_pre, sdn_pre, 0)
    y_prompt, sgla_p = c_layer(y1, 2 * PROMPT_TILE, sgla_pre, 0)

    swa_k_p = kv_tail[:, :, :A_KV_WIDTH].reshape(1, Bp, WINDOW, A_KV_HEADS, A_HD)
    swa_v_p = kv_tail[:, :, A_KV_WIDTH:].reshape(1, Bp, WINDOW, A_KV_HEADS, A_HD)
    dn_conv_p = xb_tail[:, SUBLANES - (CONV_W - 1):, :][None]

    xs = x_sample.reshape(Bs, D_MODEL)
    qa, kvn, za, xb, zb, gates = _ab_proj(xs, g_ab, w_ab_main, w_ab_gate)
    oa3, nk, nv = _swa_sample(qa.reshape(Bs, A_HEADS, A_HD), kvn, za.reshape(Bs, A_HEADS, A_HD),
                              cache_swa_k[0].reshape(Bs, WINDOW, A_KV_WIDTH),
                              cache_swa_v[0].reshape(Bs, WINDOW, A_KV_WIDTH), sink_col, SAMPLE_BLOCK)
    hist = state_dn_conv[0]
    qn, kn, vn, bg = _gdn_sample_prep(xb, hist[:, 0], hist[:, 1], hist[:, 2], gates, conv_w8, gprm)
    ob, sdn_s = _gdn_sample(state_dn[0], qn, kn, vn, bg, zb, onb, SAMPLE_BLOCK)
    ys, qc, kc, vc, zc, la = _ab_out_c_proj(xs, oa3.reshape(Bs, A_WIDTH), ob, w_ab_out,
                                            g_c, w_c_main, w_c_gk, w_up, bgk)
    ocs, sgla_s = _gla_sample(state_gla[0], qc, kc, la, vc, zc, onc, SAMPLE_BLOCK)
    y_sample = _c_out(ys, ocs, w_c_out, g_fin).reshape(Bs, 1, D_MODEL)
    dn_conv_s = jnp.concatenate([hist[:, 1:], xb[:, None, :]], axis=1)[None]

    return (y_prompt, y_sample, swa_k_p, swa_v_p, dn_conv_p, sdn_p[None], sgla_p[None],
            nk.reshape(1, Bs, WINDOW, A_KV_HEADS, A_HD), nv.reshape(1, Bs, WINDOW, A_KV_HEADS, A_HD),
            dn_conv_s, sdn_s[None], sgla_s[None])
```

```python
import functools

import jax
import jax.numpy as jnp
from jax import lax
from jax.experimental import pallas as pl
from jax.experimental.pallas import tpu as pltpu

F32 = jnp.float32
BF16 = jnp.bfloat16
EPS = 1e-6

D_MODEL = 1024
N_META = 16
CHUNK = 64
LEAD = (-N_META) % CHUNK
A_HEADS, A_KV_HEADS, A_HD = 8, 2, 64
A_GROUP = A_HEADS // A_KV_HEADS
A_WIDTH = A_HEADS * A_HD
A_KV_WIDTH = A_KV_HEADS * A_HD
WINDOW = 128
B_HEADS, B_DK, B_DV = 4, 128, 128
B_KEY = B_HEADS * B_DK
B_VAL = B_HEADS * B_DV
CONV_W = 4
B_CONV_CH = 2 * B_KEY + B_VAL
C_HEADS, C_DK, C_DV = 4, 128, 256
C_KEY = C_HEADS * C_DK
C_VAL = C_HEADS * C_DV
C_RANK = 16
C_GATE_NORM = 16.0
AB_MAIN = 2 * A_WIDTH + 2 * A_KV_WIDTH + B_CONV_CH + B_VAL
AB_MIX = A_WIDTH + B_VAL
C_MAIN = 2 * C_KEY + 2 * C_VAL
O_QA, O_KV, O_ZA, O_XB, O_ZB = 0, 512, 768, 1280, 2816

LANES = 128
SUBLANES = 8
BF16_ROWS = 16
MXU_WIDTH = 2 * LANES
VMEM_LIMIT = 56 * 1024 * 1024

PROMPT_TILE = 512
SAMPLE_BLOCK = 16
NEG = -1e30

NT_DIMS = (((1,), (1,)), ((), ()))
TN_DIMS = (((0,), (0,)), ((), ()))


def _cparams(sem):
    return pltpu.CompilerParams(dimension_semantics=sem, vmem_limit_bytes=VMEM_LIMIT)


def _dot(a, b):
    return jnp.dot(a, b, preferred_element_type=F32)


def _dotg(a, b, dims):
    return lax.dot_general(a, b, dims, preferred_element_type=F32)


def _split3(a):
    hi = a.astype(BF16)
    r = a - hi.astype(F32)
    mid = r.astype(BF16)
    lo = (r - mid.astype(F32)).astype(BF16)
    return hi, mid, lo


def _split2(a):
    hi = a.astype(BF16)
    return hi, (a - hi.astype(F32)).astype(BF16)


def _sigmoid(x):
    return 1.0 / (1.0 + jnp.exp(-x))


def _silu(x):
    return x * _sigmoid(x)


def _softplus(x):
    return jnp.maximum(x, 0.0) + jnp.log1p(jnp.exp(-jnp.abs(x)))


def _log_sigmoid(x):
    return jnp.minimum(x, 0.0) - jnp.log1p(jnp.exp(-jnp.abs(x)))


def _rms(x, g):
    return x * lax.rsqrt(jnp.mean(x * x, axis=-1, keepdims=True) + EPS) * g


def _gla_log_decay(low, wup_ref, bgk_ref):
    lh, lm, _ = _split3(low)
    wh, wm, _ = _split3(wup_ref[...])
    up = _dot(lh, wh) + _dot(lh, wm) + _dot(lm, wh)
    return _log_sigmoid(up + bgk_ref[...]) * (1.0 / C_GATE_NORM)


def _swa_stages(i, qa, kvb, env, sink_ref, *, ct, sb, pos_shift):
    r = lax.broadcasted_iota(jnp.int32, (sb, WINDOW + sb), 0)
    c = lax.broadcasted_iota(jnp.int32, (sb, WINDOW + sb), 1)
    diff = r - c + WINDOW
    in_window = (diff >= 0) & (diff < WINDOW)
    nsb = ct // sb
    items = [(s, j) for s in range(nsb) for j in range(A_KV_HEADS)]
    masks = [jnp.concatenate([in_window & (i * ct + s * sb + c - WINDOW + pos_shift >= 0)] * A_GROUP, axis=0)
             for s in range(nsb)]
    sinks = [jnp.concatenate([jnp.broadcast_to(sink_ref[0:1, j * A_GROUP + g:j * A_GROUP + g + 1], (sb, 1))
                              for g in range(A_GROUP)], axis=0) for j in range(A_KV_HEADS)]
    sc = []
    for s, j in items:
        qs = (jnp.concatenate([qa[s * sb:(s + 1) * sb, (j * A_GROUP + g) * A_HD:(j * A_GROUP + g + 1) * A_HD]
                               for g in range(A_GROUP)], axis=0) * (A_HD ** -0.5)).astype(BF16)
        sc.append(_dotg(qs, kvb[s * sb:s * sb + WINDOW + sb, j * A_HD:(j + 1) * A_HD], NT_DIMS))
        yield
    p, esk = [], []
    for n, (s, j) in enumerate(items):
        scm = jnp.where(masks[s], sc[n], NEG)
        mx = jnp.maximum(jnp.max(scm, axis=-1, keepdims=True), sinks[j])
        p.append(jnp.exp(scm - mx).astype(BF16))
        esk.append(jnp.exp(sinks[j] - mx))
        yield
    pv = []
    ones = jnp.ones((WINDOW + sb, A_HD), BF16)
    for n, (s, j) in enumerate(items):
        vx = jnp.concatenate([kvb[s * sb:s * sb + WINDOW + sb,
                                  A_KV_WIDTH + j * A_HD:A_KV_WIDTH + (j + 1) * A_HD], ones], axis=-1)
        pvx = _dot(p[n], vx)
        pv.append(pvx[:, :A_HD] / (pvx[:, A_HD:] + esk[n]))
        yield
    env['pv'] = pv


def _swa_store(env, mix_scr, *, ct, sb):
    pv = env['pv']
    for s in range(ct // sb):
        o = jnp.concatenate([pv[s * A_KV_HEADS + j][g * sb:(g + 1) * sb]
                             for j in range(A_KV_HEADS) for g in range(A_GROUP)], axis=-1)
        mix_scr[s * sb:(s + 1) * sb, 0:A_WIDTH] = (o * _silu(env['za'][s * sb:(s + 1) * sb])).astype(BF16)


def _gdn_stages(i, cvb, gt, gtt, env, gprm_ref, gprmt_ref, onorm_ref, s_scr, mix_scr, *, ct, n_lead):
    nc = ct // CHUNK
    items = [(c, h) for c in range(nc) for h in range(B_HEADS)]
    n_items = len(items)
    rs = lambda c: slice(c * CHUNK, (c + 1) * CHUNK)
    qn, kn = [], []
    for h in range(B_HEADS):
        q = cvb[h]
        k = cvb[B_HEADS + h]
        qn.append(q * lax.rsqrt(jnp.sum(q * q, axis=-1, keepdims=True) + EPS) * (B_DK ** -0.5))
        kn.append(k * lax.rsqrt(jnp.sum(k * k, axis=-1, keepdims=True) + EPS))
        yield
    beta = _sigmoid(gt)
    rows = i * ct + lax.broadcasted_iota(jnp.int32, gt.shape, 0)
    gcol = jnp.where(rows >= n_lead, -gprm_ref[0:1, :] * _softplus(gt + gprm_ref[1:2, :]), 0.0)
    lanes = i * ct + lax.broadcasted_iota(jnp.int32, gtt.shape, 1)
    grow = jnp.where(lanes >= n_lead, -gprmt_ref[:, 0:1] * _softplus(gtt + gprmt_ref[:, 1:2]), 0.0)
    ii = lax.broadcasted_iota(jnp.int32, (CHUNK, CHUNK), 0)
    jj = lax.broadcasted_iota(jnp.int32, (CHUNK, CHUNK), 1)
    lower = ii >= jj
    strict = ii > jj
    ltri = lower.astype(BF16)
    utri = (ii <= jj).astype(BF16)
    gcum_c = [sum(_dot(ltri, p) for p in _split3(gcol[rs(c)])) for c in range(nc)]
    gcum_r = [sum(_dot(p, utri) for p in _split3(grow[:, rs(c)])) for c in range(nc)]
    yield
    q_ = [qn[h][rs(c)] for c, h in items]
    k_ = [kn[h][rs(c)] for c, h in items]
    v_ = [cvb[2 * B_HEADS + h][rs(c)] for c, h in items]
    bh = [beta[rs(c), h:h + 1] for c, h in items]
    gc = [gcum_c[c][:, B_HEADS + h:B_HEADS + h + 1] for c, h in items]
    gr = [gcum_r[c][B_HEADS + h:B_HEADS + h + 1, :] for c, h in items]
    kb, a, qk = [], [], []
    for n in range(n_items):
        decay = jnp.exp(jnp.where(lower, gc[n] - gr[n], NEG))
        kb.append(k_[n] * bh[n])
        kq = _dotg(jnp.concatenate([kb[n], q_[n]], axis=0).astype(BF16), k_[n].astype(BF16), NT_DIMS)
        a.append(jnp.where(strict, kq[:CHUNK] * decay, 0.0))
        qk.append(jnp.where(lower, kq[CHUNK:] * decay, 0.0).astype(BF16))
        if n % B_HEADS == B_HEADS - 1:
            yield
    doff = [-jnp.where((ii >> 1) == (jj >> 1), a[n], 0.0) for n in range(n_items)]
    for lg in range(1, 6):
        m = ((ii >> (lg + 1)) == (jj >> (lg + 1))) & ((ii >> lg) != (jj >> lg))
        bm = [jnp.where(m, a[n], 0.0) for n in range(n_items)]
        db = [doff[n].astype(BF16) for n in range(n_items)]
        y = [bm[n] + _dot(db[n], bm[n].astype(BF16)) for n in range(n_items)]
        yield
        x = [y[n] + _dot(y[n].astype(BF16), db[n]) for n in range(n_items)]
        doff = [doff[n] - x[n] for n in range(n_items)]
        yield
    egc = [jnp.exp(gc[n]) for n in range(n_items)]
    g_last = [gc[n][CHUNK - 1:CHUNK, :] for n in range(n_items)]
    eg_last = [jnp.exp(g_last[n]) for n in range(n_items)]
    sol, kd, wq = [], [], []
    for n in range(n_items):
        rhs = jnp.concatenate([v_[n] * bh[n], kb[n] * egc[n]], axis=-1)
        sol.append(rhs + _dot(doff[n].astype(BF16), rhs.astype(BF16)))
        kd.append((k_[n] * jnp.exp(g_last[n] - gc[n])).astype(BF16))
        wq.append(jnp.concatenate([sol[n][:, B_DV:], q_[n] * egc[n]], axis=0).astype(BF16))
        if n % B_HEADS == B_HEADS - 1:
            yield

    def gate_store(c, o):
        for h in range(B_HEADS):
            z = env['zb'][rs(c), h * B_DV:(h + 1) * B_DV]
            mix_scr[rs(c), A_WIDTH + h * B_DV:A_WIDTH + (h + 1) * B_DV] = (
                _rms(o[h], onorm_ref[...]) * _silu(z)).astype(BF16)

    s_cur = [s_scr[h] for h in range(B_HEADS)]
    o_prev = None
    for c in range(nc):
        idx = [c * B_HEADS + h for h in range(B_HEADS)]
        ws = [_dot(wq[n], s_cur[h].astype(BF16)) for h, n in enumerate(idx)]
        if o_prev is not None:
            gate_store(c - 1, o_prev)
        yield
        v_new = [sol[n][:, :B_DV] - ws[h][:CHUNK] for h, n in enumerate(idx)]
        vb = [v_new[h].astype(BF16) for h in range(B_HEADS)]
        o_prev = [ws[h][CHUNK:] + _dot(qk[n], vb[h]) for h, n in enumerate(idx)]
        s_cur = [s_cur[h] * eg_last[n] + _dotg(kd[n], vb[h], TN_DIMS) for h, n in enumerate(idx)]
        yield
    for h in range(B_HEADS):
        s_scr[h] = s_cur[h]
    gate_store(nc - 1, o_prev)
    yield


def _ab_layer_kernel(x_ref, g_ref, w_ref, wg_ref, wgt_ref, sink_ref, hist_ref, convw_ref, gprm_ref,
                     gprmt_ref, onorm_ref, wout_ref, convin_ref, sin_ref,
                     y_ref, kvtail_ref, xbtail_ref, sout_ref,
                     s_scr, xc_scr, kvprev_scr, mix_scr, *, ct, n_lead, pos_shift):
    i = pl.program_id(1)
    nsteps = pl.num_programs(1)
    ntail = min(WINDOW, ct)

    @pl.when(i == 0)
    def _():
        s_scr[...] = sin_ref[0]
        xc_scr[0:SUBLANES, :] = convin_ref[0]
        kvprev_scr[1] = hist_ref[...]

    x = x_ref[0]
    hb = _rms(x, g_ref[...]).astype(BF16)
    proj = lambda a, b: _dot(hb, w_ref[:, a:b])
    env = {}

    def step(gen, n=1):
        for _ in range(n):
            next(gen, None)

    nblk = B_CONV_CH // LANES
    w = convw_ref[...]
    cvb = []

    xblk = []
    row8 = lax.broadcasted_iota(jnp.int32, (SUBLANES, LANES), 0)

    def conv_block(j):
        cs = slice(j * LANES, (j + 1) * LANES)
        xj = xblk[j]
        h8 = xc_scr[0:SUBLANES, cs]
        yc = xj * w[CONV_W - 1:CONV_W, cs]
        for k in range(1, CONV_W):
            sh = pltpu.roll(xj, k, 0)
            top = jnp.where(row8 < k, pltpu.roll(h8, k, 0), sh[0:SUBLANES])
            yc = yc + jnp.concatenate([top, sh[SUBLANES:]], axis=0) * w[CONV_W - 1 - k:CONV_W - k, cs]
        cvb.append(_silu(yc))
        xc_scr[0:SUBLANES, cs] = xj[ct - SUBLANES:]

    pw = MXU_WIDTH
    for j in range(nblk // 2):
        xbj = proj(O_XB + j * pw, O_XB + (j + 1) * pw)
        xblk += [xbj[:, 0:LANES], xbj[:, LANES:]]
        xbtail_ref[0, :, j * pw:(j + 1) * pw] = xbj[ct - SUBLANES:]
        if j > 0:
            conv_block(2 * j - 2)
            conv_block(2 * j - 1)
    gt = _dot(hb, wg_ref[...])
    gtt = _dotg(wgt_ref[...], hb, NT_DIMS)
    conv_block(nblk - 2)
    conv_block(nblk - 1)

    gdn = _gdn_stages(i, cvb, gt, gtt, env, gprm_ref, gprmt_ref, onorm_ref, s_scr, mix_scr, ct=ct, n_lead=n_lead)
    qa_p = []
    for j in range(A_WIDTH // pw):
        qa_p.append(proj(O_QA + j * pw, O_QA + (j + 1) * pw))
        step(gdn, 2)
    qa = jnp.concatenate(qa_p, axis=-1)
    kv = proj(O_KV, O_ZA)
    kvtail_ref[0] = kv[ct - ntail:]
    slot = lax.rem(i, 2)
    kvb = jnp.concatenate([kvprev_scr[1 - slot], kv], axis=0).astype(BF16)
    if ct >= WINDOW:
        kvprev_scr[slot] = kv[ct - WINDOW:]
    swa = _swa_stages(i, qa, kvb, env, sink_ref, ct=ct, sb=ntail, pos_shift=pos_shift)
    n_sw = (ct // ntail) * A_KV_HEADS
    step(gdn)
    for _ in range(ct // CHUNK):
        step(gdn)
        step(swa)
    step(swa, max(n_sw - ct // CHUNK, 0))
    za_p, zb_p = [], []
    fill = ([lambda j=j: za_p.append(proj(O_ZA + j * pw, O_ZA + (j + 1) * pw)) for j in range(A_WIDTH // pw)]
            + [lambda j=j: zb_p.append(proj(O_ZB + j * pw, O_ZB + (j + 1) * pw)) for j in range(B_VAL // pw)])
    for lv in range(10):
        step(gdn)
        if lv % 2 == 1 and fill:
            fill.pop(0)()
    for f in fill:
        f()
    env['za'] = jnp.concatenate(za_p, axis=-1)
    env['zb'] = jnp.concatenate(zb_p, axis=-1)
    step(gdn, ct // CHUNK)
    for _ in range(2 * (ct // CHUNK)):
        step(gdn)
        step(swa, -(-2 * n_sw // (2 * (ct // CHUNK))))
    for _ in swa:
        pass
    _swa_store(env, mix_scr, ct=ct, sb=ntail)
    ya = x + _dot(mix_scr[:, 0:A_WIDTH], wout_ref[0:A_WIDTH, :])
    for _ in gdn:
        pass
    y_ref[0] = ya + _dot(mix_scr[:, A_WIDTH:], wout_ref[A_WIDTH:, :])

    @pl.when(i == nsteps - 1)
    def _():
        sout_ref[0] = s_scr[...]


def _ab_layer_prompt(x3, norm_g, w_main, w_gate, w_gate_t, sink, hist_kv, conv_w8, gprm, gprm_t, onorm,
                     w_out, conv_in, s_in, ct, n_lead, pos_shift):
    B, L, _ = x3.shape
    ntail = min(WINDOW, ct)
    kern = functools.partial(_ab_layer_kernel, ct=ct, n_lead=n_lead, pos_shift=pos_shift)
    full = lambda a: pl.BlockSpec(a.shape, lambda b, i: (0,) * a.ndim)
    blk = pl.BlockSpec((1, ct, D_MODEL), lambda b, i: (b, i, 0))
    per_b = lambda *s: pl.BlockSpec((1,) + s, lambda b, i: (b,) + (0,) * len(s))
    return pl.pallas_call(
        kern,
        grid=(B, L // ct),
        in_specs=[blk, full(norm_g), full(w_main), full(w_gate), full(w_gate_t), full(sink), full(hist_kv),
                  full(conv_w8), full(gprm), full(gprm_t), full(onorm), full(w_out),
                  full(conv_in), full(s_in)],
        out_specs=[blk, per_b(ntail, 2 * A_KV_WIDTH), per_b(SUBLANES, B_CONV_CH), per_b(B_HEADS, B_DK, B_DV)],
        out_shape=[jax.ShapeDtypeStruct((B, L, D_MODEL), F32),
                   jax.ShapeDtypeStruct((B, ntail, 2 * A_KV_WIDTH), F32),
                   jax.ShapeDtypeStruct((B, SUBLANES, B_CONV_CH), F32),
                   jax.ShapeDtypeStruct((B, B_HEADS, B_DK, B_DV), F32)],
        scratch_shapes=[pltpu.VMEM((B_HEADS, B_DK, B_DV), F32),
                        pltpu.VMEM((SUBLANES, B_CONV_CH), F32),
                        pltpu.VMEM((2, WINDOW, 2 * A_KV_WIDTH), F32),
                        pltpu.VMEM((ct, AB_MIX), BF16)],
        compiler_params=_cparams(("arbitrary", "arbitrary")),
        name="ab_layer_prompt",
    )(x3, norm_g, w_main, w_gate, w_gate_t, sink, hist_kv, conv_w8, gprm, gprm_t, onorm, w_out, conv_in, s_in)


def _c_layer_kernel(x_ref, g_ref, w_ref, wgk_ref, wup_ref, bgk_ref, onorm_ref, wout_ref, gfin_ref,
                    sin_ref, y_ref, sout_ref, st_scr, og_scr, *, ct, n_lead):
    i = pl.program_id(1)
    nsteps = pl.num_programs(1)

    @pl.when(i == 0)
    def _():
        st_scr[...] = sin_ref[0]

    x = x_ref[0]
    hb = _rms(x, g_ref[...]).astype(BF16)
    low = _dot(hb, wgk_ref[...])
    q_all = _dot(hb, w_ref[:, 0:C_KEY])
    la_all = _gla_log_decay(low, wup_ref, bgk_ref)
    k_all = _dot(hb, w_ref[:, C_KEY:2 * C_KEY])
    v_h, z_h = [], []
    later = ([lambda h=h: v_h.append(_dot(hb, w_ref[:, 2 * C_KEY + h * C_DV:2 * C_KEY + (h + 1) * C_DV]).astype(BF16))
              for h in range(C_HEADS)]
             + [lambda h=h: z_h.append(_dot(hb, w_ref[:, 2 * C_KEY + C_VAL + h * C_DV:
                                                       2 * C_KEY + C_VAL + (h + 1) * C_DV]))
                for h in range(C_HEADS)])

    def run_later(n):
        for _ in range(n):
            if later:
                later.pop(0)()
    rows = i * ct + lax.broadcasted_iota(jnp.int32, la_all.shape, 0)
    la_all = jnp.where(rows >= n_lead, la_all, 0.0)

    ii = lax.broadcasted_iota(jnp.int32, (CHUNK, CHUNK), 0)
    jj = lax.broadcasted_iota(jnp.int32, (CHUNK, CHUNK), 1)
    lower = ii >= jj
    ltri = lower.astype(BF16)
    nc = ct // CHUNK
    rs = lambda c: slice(c * CHUNK, (c + 1) * CHUNK)
    ks = lambda h: slice(h * C_DK, (h + 1) * C_DK)
    vs = lambda h: slice(h * C_DV, (h + 1) * C_DV)
    items = [(c, h) for c in range(nc) for h in range(C_HEADS)]
    bc_all = [sum(_dot(ltri, p) for p in _split2(la_all[rs(c)])) for c in range(nc)]
    run_later(2)
    every = max(nc // C_HEADS, 1)
    qd, kinv, kd, gl, qk = [], [], [], [], []
    for n, (c, h) in enumerate(items):
        bc = bc_all[c][:, ks(h)]
        k = k_all[rs(c), ks(h)]
        bl = bc[CHUNK - 1:CHUNK, :]
        qd.append((q_all[rs(c), ks(h)] * (C_DK ** -0.5) * jnp.exp(bc)).astype(BF16))
        kinv.append((k * jnp.exp(-bc)).astype(BF16))
        kd.append((k * jnp.exp(bl - bc)).astype(BF16))
        gl.append(jnp.exp(bl))
        if h == C_HEADS - 1:
            qk += [jnp.where(lower, _dotg(qd[m], kinv[m], NT_DIMS), 0.0).astype(BF16)
                   for m in range(n - C_HEADS + 1, n + 1)]
            if c % every == every - 1:
                run_later(1)
    run_later(C_HEADS - len(v_h))
    v_ = [v_h[h][rs(c)] for c, h in items]
    o_intra = []
    for c in range(nc):
        o_intra += [_dot(qk[c * C_HEADS + h], v_[c * C_HEADS + h]) for h in range(C_HEADS)]
        if c % every == every - 1:
            run_later(1)
    run_later(len(later))

    rb = min(ct, 4 * CHUNK)

    def gate_store(c, o):
        for h in range(C_HEADS):
            og_scr[rs(c), vs(h)] = (_rms(o[h], onorm_ref[...]) * _silu(z_h[h][rs(c)])).astype(BF16)
        if ((c + 1) * CHUNK) % rb == 0:
            r = slice((c + 1) * CHUNK - rb, (c + 1) * CHUNK)
            y_ref[0, r, :] = _rms(x[r] + _dot(og_scr[r, :], wout_ref[...]), gfin_ref[...])

    st = [st_scr[h] for h in range(C_HEADS)]
    o_prev = None
    pad_rows = jnp.zeros((SUBLANES - C_HEADS, C_DK), F32)
    for c in range(nc):
        idx = [c * C_HEADS + h for h in range(C_HEADS)]
        glt = jnp.concatenate([gl[n] for n in idx] + [pad_rows], axis=0).T
        o = [o_intra[n] + _dot(qd[n], st[h].astype(BF16)) for h, n in enumerate(idx)]
        st = [st[h] * glt[:, h:h + 1] + _dotg(kd[n], v_[n], TN_DIMS) for h, n in enumerate(idx)]
        if o_prev is not None:
            gate_store(c - 1, o_prev)
        o_prev = o
    gate_store(nc - 1, o_prev)
    for h in range(C_HEADS):
        st_scr[h] = st[h]

    @pl.when(i == nsteps - 1)
    def _():
        sout_ref[0] = st_scr[...]


def _c_layer_prompt(x3, norm_g, w_main, w_gk, w_up, b_gk, onorm, w_out, final_g, st_in, ct, n_lead):
    B, L, _ = x3.shape
    kern = functools.partial(_c_layer_kernel, ct=ct, n_lead=n_lead)
    full = lambda a: pl.BlockSpec(a.shape, lambda b, i: (0,) * a.ndim)
    blk = pl.BlockSpec((1, ct, D_MODEL), lambda b, i: (b, i, 0))
    st = pl.BlockSpec((1, C_HEADS, C_DK, C_DV), lambda b, i: (b, 0, 0, 0))
    return pl.pallas_call(
        kern,
        grid=(B, L // ct),
        in_specs=[blk, full(norm_g), full(w_main), full(w_gk), full(w_up), full(b_gk), full(onorm),
                  full(w_out), full(final_g), full(st_in)],
        out_specs=[blk, st],
        out_shape=[jax.ShapeDtypeStruct((B, L, D_MODEL), F32),
                   jax.ShapeDtypeStruct((B, C_HEADS, C_DK, C_DV), F32)],
        scratch_shapes=[pltpu.VMEM((C_HEADS, C_DK, C_DV), F32), pltpu.VMEM((ct, C_VAL), BF16)],
        compiler_params=_cparams(("arbitrary", "arbitrary")),
        name="c_layer_prompt",
    )(x3, norm_g, w_main, w_gk, w_up, b_gk, onorm, w_out, final_g, st_in)


def _ab_proj_kernel(x_ref, g_ref, w_ref, wg_ref, hist_ref, convw_ref, gprm_ref,
                    qa_ref, kv_ref, za_ref, xb_ref, zb_ref, q_ref, k_ref, v_ref, bg_ref):
    hb = _rms(x_ref[...], g_ref[...]).astype(BF16)
    qa_ref[...] = _dot(hb, w_ref[:, O_QA:O_KV])
    kv_ref[...] = _dot(hb, w_ref[:, O_KV:O_ZA])
    za_ref[...] = _dot(hb, w_ref[:, O_ZA:O_XB])
    xb = _dot(hb, w_ref[:, O_XB:O_ZB])
    xb_ref[...] = xb
    zb_ref[...] = _dot(hb, w_ref[:, O_ZB:AB_MAIN])
    gt = _dot(hb, wg_ref[...])
    w = convw_ref[...]
    y = xb * w[CONV_W - 1:CONV_W, :]
    for t in range(CONV_W - 1):
        y = y + hist_ref[:, t * B_CONV_CH:(t + 1) * B_CONV_CH] * w[t:t + 1, :]
    cv = _silu(y)
    for h in range(B_HEADS):
        q = cv[:, h * B_DK:(h + 1) * B_DK]
        k = cv[:, B_KEY + h * B_DK:B_KEY + (h + 1) * B_DK]
        q_ref[:, h * B_DK:(h + 1) * B_DK] = (
            q * lax.rsqrt(jnp.sum(q * q, axis=-1, keepdims=True) + EPS) * (B_DK ** -0.5))
        k_ref[:, h * B_DK:(h + 1) * B_DK] = k * lax.rsqrt(jnp.sum(k * k, axis=-1, keepdims=True) + EPS)
    v_ref[...] = cv[:, 2 * B_KEY:]
    col = lax.broadcasted_iota(jnp.int32, gt.shape, 1)
    g = -gprm_ref[0:1, :] * _softplus(gt + gprm_ref[1:2, :])
    bg_ref[...] = jnp.where(col < B_HEADS, _sigmoid(gt), jnp.exp(g))


def _ab_proj(x2d, norm_g, w_main, w_gate, hist2d, conv_w8, gprm):
    n = x2d.shape[0]
    full = lambda a: pl.BlockSpec(a.shape, lambda: (0,) * a.ndim)
    widths = (A_WIDTH, 2 * A_KV_WIDTH, A_WIDTH, B_CONV_CH, B_VAL, B_KEY, B_KEY, B_VAL, LANES)
    args = (x2d, norm_g, w_main, w_gate, hist2d, conv_w8, gprm)
    return pl.pallas_call(
        _ab_proj_kernel,
        in_specs=[full(a) for a in args],
        out_specs=[pl.BlockSpec((n, w), lambda: (0, 0)) for w in widths],
        out_shape=[jax.ShapeDtypeStruct((n, w), F32) for w in widths],
        compiler_params=pltpu.CompilerParams(vmem_limit_bytes=VMEM_LIMIT),
        name="ab_in_proj_sample",
    )(*args)


def _swa_sample_kernel(q_ref, kvn_ref, za_ref, ck_ref, cv_ref, sink_ref, o_ref, nk_ref, nv_ref, *, bt):
    row = lax.broadcasted_iota(jnp.int32, (WINDOW, A_KV_WIDTH), 0)
    hrow = lax.broadcasted_iota(jnp.int32, (A_HEADS, A_HD), 0)
    sk = sink_ref[...]
    nkb, nvb = [], []
    for b in range(bt):
        nk = jnp.where(row == WINDOW - 1, kvn_ref[b:b + 1, 0:A_KV_WIDTH], pltpu.roll(ck_ref[b], WINDOW - 1, 0))
        nv = jnp.where(row == WINDOW - 1, kvn_ref[b:b + 1, A_KV_WIDTH:], pltpu.roll(cv_ref[b], WINDOW - 1, 0))
        nk_ref[b] = nk
        nv_ref[b] = nv
        nkb.append(nk.astype(BF16))
        nvb.append(nv.astype(BF16))
    items = [(b, j) for b in range(bt) for j in range(A_KV_HEADS)]
    q8 = [q_ref[b].astype(BF16) for b in range(bt)]
    sc = [_dotg(q8[b], nkb[b][:, j * A_HD:(j + 1) * A_HD], NT_DIMS) * (A_HD ** -0.5) for b, j in items]
    m = [jnp.maximum(jnp.max(s, axis=-1, keepdims=True), sk) for s in sc]
    p = [jnp.exp(sc[n] - m[n]) for n in range(len(items))]
    den = [jnp.sum(p[n], axis=-1, keepdims=True) + jnp.exp(sk - m[n]) for n in range(len(items))]
    oj = [_dot(p[n].astype(BF16), nvb[b][:, j * A_HD:(j + 1) * A_HD]) / den[n] for n, (b, j) in enumerate(items)]
    for b in range(bt):
        o8 = jnp.where(hrow >= A_GROUP, oj[b * A_KV_HEADS + 1], oj[b * A_KV_HEADS])
        o_ref[b] = o8 * _silu(za_ref[b])


def _swa_sample(q3, kv_new, za3, cache_k, cache_v, sink_col, bt):
    nb = q3.shape[0]
    hd = pl.BlockSpec((bt, A_HEADS, A_HD), lambda i: (i, 0, 0))
    cache = pl.BlockSpec((bt, WINDOW, A_KV_WIDTH), lambda i: (i, 0, 0))
    return pl.pallas_call(
        functools.partial(_swa_sample_kernel, bt=bt),
        grid=(nb // bt,),
        in_specs=[hd, pl.BlockSpec((bt, 2 * A_KV_WIDTH), lambda i: (i, 0)), hd, cache, cache,
                  pl.BlockSpec(sink_col.shape, lambda i: (0, 0))],
        out_specs=[hd, cache, cache],
        out_shape=[jax.ShapeDtypeStruct(q3.shape, F32),
                   jax.ShapeDtypeStruct(cache_k.shape, F32),
                   jax.ShapeDtypeStruct(cache_v.shape, F32)],
        compiler_params=_cparams(("arbitrary",)),
        name="swa_sample",
    )(q3, kv_new, za3, cache_k, cache_v, sink_col)


def _gdn_sample_kernel(s_ref, qt_ref, kt_ref, v_ref, bg_ref, zb_ref, onorm_ref, o_ref, sout_ref, *, bt):
    items = [(b, h) for b in range(bt) for h in range(B_HEADS)]
    hs = lambda h: slice(h * B_DV, (h + 1) * B_DV)
    kt = [kt_ref[:, hs(h)].T for h in range(B_HEADS)]
    qt = [qt_ref[:, hs(h)].T for h in range(B_HEADS)]
    kc = [kt[h][:, b:b + 1] for b, h in items]
    eg = [bg_ref[b:b + 1, B_HEADS + h:B_HEADS + h + 1] for b, h in items]
    ks = [jnp.sum(kc[n] * s_ref[b, h], axis=0, keepdims=True) for n, (b, h) in enumerate(items)]
    v_new = [bg_ref[b:b + 1, h:h + 1] * (v_ref[b:b + 1, hs(h)] - eg[n] * ks[n]) for n, (b, h) in enumerate(items)]
    o = []
    for n, (b, h) in enumerate(items):
        s_new = eg[n] * s_ref[b, h] + kc[n] * v_new[n]
        sout_ref[b, h] = s_new
        o.append(jnp.sum(qt[h][:, b:b + 1] * s_new, axis=0, keepdims=True))
    o_blk = jnp.concatenate([jnp.concatenate([o[b * B_HEADS + h] for h in range(B_HEADS)], axis=-1)
                             for b in range(bt)], axis=0)
    for h in range(B_HEADS):
        o_ref[:, hs(h)] = _rms(o_blk[:, hs(h)], onorm_ref[...]) * _silu(zb_ref[:, hs(h)])


def _gdn_sample(state, qt, kt, v, bg, zb, onorm, bt):
    nb = state.shape[0]
    st = pl.BlockSpec((bt, B_HEADS, B_DK, B_DV), lambda i: (i, 0, 0, 0))
    row = lambda n: pl.BlockSpec((bt, n), lambda i: (i, 0))
    return pl.pallas_call(
        functools.partial(_gdn_sample_kernel, bt=bt),
        grid=(nb // bt,),
        in_specs=[st, row(B_KEY), row(B_KEY), row(B_VAL), row(LANES), row(B_VAL),
                  pl.BlockSpec(onorm.shape, lambda i: (0, 0))],
        out_specs=[row(B_VAL), st],
        out_shape=[jax.ShapeDtypeStruct((nb, B_VAL), F32), jax.ShapeDtypeStruct(state.shape, F32)],
        compiler_params=_cparams(("arbitrary",)),
        name="gdn_sample",
    )(state, qt, kt, v, bg, zb, onorm)


def _ab_out_c_proj_kernel(x_ref, oa_ref, ob_ref, wout_ref, g_ref, w_ref, wgk_ref, wup_ref, bgk_ref,
                          y_ref, q_ref, k_ref, v_ref, z_ref, la_ref):
    y = (x_ref[...] + _dot(oa_ref[...].astype(BF16), wout_ref[0:A_WIDTH, :])
         + _dot(ob_ref[...].astype(BF16), wout_ref[A_WIDTH:, :]))
    y_ref[...] = y
    hb = _rms(y, g_ref[...]).astype(BF16)
    q_ref[...] = _dot(hb, w_ref[:, 0:C_KEY])
    k_ref[...] = _dot(hb, w_ref[:, C_KEY:2 * C_KEY])
    v_ref[...] = _dot(hb, w_ref[:, 2 * C_KEY:2 * C_KEY + C_VAL])
    z_ref[...] = _dot(hb, w_ref[:, 2 * C_KEY + C_VAL:C_MAIN])
    la_ref[...] = _gla_log_decay(_dot(hb, wgk_ref[...]), wup_ref, bgk_ref)


def _ab_out_c_proj(x2d, oa, ob, w_out, norm_g, w_main, w_gk, w_up, b_gk):
    n = x2d.shape[0]
    full = lambda a: pl.BlockSpec(a.shape, lambda: (0,) * a.ndim)
    widths = (D_MODEL, C_KEY, C_KEY, C_VAL, C_VAL, C_KEY)
    args = (x2d, oa, ob, w_out, norm_g, w_main, w_gk, w_up, b_gk)
    return pl.pallas_call(
        _ab_out_c_proj_kernel,
        in_specs=[full(a) for a in args],
        out_specs=[pl.BlockSpec((n, w), lambda: (0, 0)) for w in widths],
        out_shape=[jax.ShapeDtypeStruct((n, w), F32) for w in widths],
        compiler_params=pltpu.CompilerParams(vmem_limit_bytes=VMEM_LIMIT),
        name="ab_out_c_in_proj_sample",
    )(*args)


def _gla_sample_kernel(s_ref, qt_ref, kt_ref, lat_ref, v_ref, z_ref, onorm_ref, o_ref, sout_ref, *, bt):
    items = [(b, h) for b in range(bt) for h in range(C_HEADS)]
    hs = lambda h: slice(h * C_DV, (h + 1) * C_DV)
    ks = lambda h: slice(h * C_DK, (h + 1) * C_DK)
    at = [jnp.exp(lat_ref[:, ks(h)]).T for h in range(C_HEADS)]
    kt = [kt_ref[:, ks(h)].T for h in range(C_HEADS)]
    qt = [(qt_ref[:, ks(h)] * (C_DK ** -0.5)).T for h in range(C_HEADS)]
    o = []
    for n, (b, h) in enumerate(items):
        s_new = at[h][:, b:b + 1] * s_ref[b, h] + kt[h][:, b:b + 1] * v_ref[b:b + 1, hs(h)]
        sout_ref[b, h] = s_new
        o.append(jnp.sum(qt[h][:, b:b + 1] * s_new, axis=0, keepdims=True))
    o_blk = jnp.concatenate([jnp.concatenate([o[b * C_HEADS + h] for h in range(C_HEADS)], axis=-1)
                             for b in range(bt)], axis=0)
    for h in range(C_HEADS):
        o_ref[:, hs(h)] = _rms(o_blk[:, hs(h)], onorm_ref[...]) * _silu(z_ref[:, hs(h)])


def _gla_sample(state, qt, kt, lat, v, z, onorm, bt):
    nb = state.shape[0]
    st = pl.BlockSpec((bt, C_HEADS, C_DK, C_DV), lambda i: (i, 0, 0, 0))
    row = lambda n: pl.BlockSpec((bt, n), lambda i: (i, 0))
    return pl.pallas_call(
        functools.partial(_gla_sample_kernel, bt=bt),
        grid=(nb // bt,),
        in_specs=[st, row(C_KEY), row(C_KEY), row(C_KEY), row(C_VAL), row(C_VAL),
                  pl.BlockSpec(onorm.shape, lambda i: (0, 0))],
        out_specs=[row(C_VAL), st],
        out_shape=[jax.ShapeDtypeStruct((nb, C_VAL), F32), jax.ShapeDtypeStruct(state.shape, F32)],
        compiler_params=_cparams(("arbitrary",)),
        name="gla_sample",
    )(state, qt, kt, lat, v, z, onorm)


def _c_out_kernel(x_ref, o_ref, w_ref, g_ref, y_ref):
    y_ref[...] = _rms(x_ref[...] + _dot(o_ref[...].astype(BF16), w_ref[...]), g_ref[...])


def _c_out(x2d, oc, w_out, final_g):
    n = x2d.shape[0]
    full = lambda a: pl.BlockSpec(a.shape, lambda: (0,) * a.ndim)
    args = (x2d, oc, w_out, final_g)
    return pl.pallas_call(
        _c_out_kernel,
        in_specs=[full(a) for a in args],
        out_specs=pl.BlockSpec((n, D_MODEL), lambda: (0, 0)),
        out_shape=jax.ShapeDtypeStruct((n, D_MODEL), F32),
        compiler_params=pltpu.CompilerParams(vmem_limit_bytes=VMEM_LIMIT),
        name="c_out_proj_norm_sample",
    )(*args)


def kernel(x_prompt, x_sample, cache_swa_k, cache_swa_v, state_dn_conv, state_dn, state_gla,
           meta_tokens, norm_ab, w_in_ab, sink_a, conv_b, a_log_b, dt_bias_b, onorm_b, w_out_ab,
           norm_c, w_in_c, w_gk_up, b_gk, onorm_c, w_out_c, final_norm):
    Bp, L, _ = x_prompt.shape
    Bs = x_sample.shape[0]
    assert L % PROMPT_TILE == 0 and Bs % SAMPLE_BLOCK == 0

    w_ab = w_in_ab[0]
    w_ab_main = w_ab.astype(BF16)
    w_ab_gate = jnp.pad(w_ab[:, AB_MAIN:], ((0, 0), (0, LANES - 2 * B_HEADS))).astype(BF16)
    w_ab_gate_t = jnp.pad(w_ab[:, AB_MAIN:].T, ((0, BF16_ROWS - 2 * B_HEADS), (0, 0))).astype(BF16)
    g_ab = norm_ab[0][None, :]
    sink_row = sink_a[0][None, :]
    sink_col = sink_a[0][:, None]
    conv_w8 = jnp.pad(conv_b[0], ((0, SUBLANES - CONV_W), (0, 0)))
    gp = jnp.stack([jnp.exp(a_log_b[0]), dt_bias_b[0]])
    gprm = jnp.pad(gp, ((0, SUBLANES - 2), (B_HEADS, LANES - 2 * B_HEADS)))
    gprm_t = jnp.pad(gp.T, ((B_HEADS, BF16_ROWS - 2 * B_HEADS), (0, LANES - 2)))
    onb = onorm_b[0][None, :]
    w_ab_out = w_out_ab[0].astype(BF16)
    w_c = w_in_c[0]
    w_c_main = w_c.astype(BF16)
    w_c_gk = jnp.pad(w_c[:, C_MAIN:], ((0, 0), (0, LANES - C_RANK))).astype(BF16)
    w_up = jnp.pad(w_gk_up[0], ((0, LANES - C_RANK), (0, 0)))
    bgk = b_gk[0][None, :]
    g_c = norm_c[0][None, :]
    onc = onorm_c[0][None, :]
    w_c_out = w_out_c[0].astype(BF16)
    g_fin = final_norm[None, :]

    def ab_layer(x3, ct, hist_kv, pos_shift, conv_in, s_in, n_lead):
        return _ab_layer_prompt(x3, g_ab, w_ab_main, w_ab_gate, w_ab_gate_t, sink_row, hist_kv, conv_w8,
                                gprm, gprm_t, onb, w_ab_out, conv_in, s_in, ct, n_lead, pos_shift)

    def c_layer(x3, ct, st_in, n_lead):
        return _c_layer_prompt(x3, g_c, w_c_main, w_c_gk, w_up, bgk, onc, w_c_out, g_fin, st_in, ct, n_lead)

    xpre = jnp.concatenate([jnp.zeros((LEAD, D_MODEL), F32), meta_tokens], axis=0)[None]
    y_pre, kv_pre, xbtail_pre, sdn_pre = ab_layer(
        xpre, CHUNK, jnp.zeros((WINDOW, 2 * A_KV_WIDTH), F32), -LEAD,
        jnp.zeros((1, SUBLANES, B_CONV_CH), F32), jnp.zeros((1, B_HEADS, B_DK, B_DV), F32), LEAD)
    _, sgla_pre = c_layer(y_pre, CHUNK, jnp.zeros((1, C_HEADS, C_DK, C_DV), F32), LEAD)

    hist_kv = jnp.concatenate([jnp.zeros((WINDOW - CHUNK, 2 * A_KV_WIDTH), F32), kv_pre[0]], axis=0)
    y1, kv_tail, xb_tail, sdn_p = ab_layer(
        x_prompt, PROMPT_TILE, hist_kv, N_META, xbtail_pre, sdn_pre, 0)
    y_prompt, sgla_p = c_layer(y1, 2 * PROMPT_TILE, sgla_pre, 0)

    swa_k_p = kv_tail[:, :, :A_KV_WIDTH].reshape(1, Bp, WINDOW, A_KV_HEADS, A_HD)
    swa_v_p = kv_tail[:, :, A_KV_WIDTH:].reshape(1, Bp, WINDOW, A_KV_HEADS, A_HD)
    dn_conv_p = xb_tail[:, SUBLANES - (CONV_W - 1):, :][None]

    xs = x_sample.reshape(Bs, D_MODEL)
    hist = state_dn_conv[0]
    qa, kvn, za, xb, zb, qn, kn, vn, bg = _ab_proj(xs, g_ab, w_ab_main, w_ab_gate,
                                                   hist.reshape(Bs, (CONV_W - 1) * B_CONV_CH), conv_w8, gprm)
    oa3, nk, nv = _swa_sample(qa.reshape(Bs, A_HEADS, A_HD), kvn, za.reshape(Bs, A_HEADS, A_HD),
                              cache_swa_k[0].reshape(Bs, WINDOW, A_KV_WIDTH),
                              cache_swa_v[0].reshape(Bs, WINDOW, A_KV_WIDTH), sink_col, SAMPLE_BLOCK)
    ob, sdn_s = _gdn_sample(state_dn[0], qn, kn, vn, bg, zb, onb, SAMPLE_BLOCK)
    ys, qc, kc, vc, zc, la = _ab_out_c_proj(xs, oa3.reshape(Bs, A_WIDTH), ob, w_ab_out,
                                            g_c, w_c_main, w_c_gk, w_up, bgk)
    ocs, sgla_s = _gla_sample(state_gla[0], qc, kc, la, vc, zc, onc, SAMPLE_BLOCK)
    y_sample = _c_out(ys, ocs, w_c_out, g_fin).reshape(Bs, 1, D_MODEL)
    dn_conv_s = jnp.concatenate([hist[:, 1:], xb[:, None, :]], axis=1)[None]

    return (y_prompt, y_sample, swa_k_p, swa_v_p, dn_conv_p, sdn_p[None], sgla_p[None],
            nk.reshape(1, Bs, WINDOW, A_KV_HEADS, A_HD), nv.reshape(1, Bs, WINDOW, A_KV_HEADS, A_HD),
            dn_conv_s, sdn_s[None], sgla_s[None])
```

```python
import functools

import jax
import jax.numpy as jnp
from jax import lax
from jax.experimental import pallas as pl
from jax.experimental.pallas import tpu as pltpu

F32 = jnp.float32
BF16 = jnp.bfloat16
EPS = 1e-6

D_MODEL = 1024
N_META = 16
CHUNK = 64
LEAD = (-N_META) % CHUNK
A_HEADS, A_KV_HEADS, A_HD = 8, 2, 64
A_GROUP = A_HEADS // A_KV_HEADS
A_WIDTH = A_HEADS * A_HD
A_KV_WIDTH = A_KV_HEADS * A_HD
WINDOW = 128
B_HEADS, B_DK, B_DV = 4, 128, 128
B_KEY = B_HEADS * B_DK
B_VAL = B_HEADS * B_DV
CONV_W = 4
B_CONV_CH = 2 * B_KEY + B_VAL
C_HEADS, C_DK, C_DV = 4, 128, 256
C_KEY = C_HEADS * C_DK
C_VAL = C_HEADS * C_DV
C_RANK = 16
C_GATE_NORM = 16.0
AB_MAIN = 2 * A_WIDTH + 2 * A_KV_WIDTH + B_CONV_CH + B_VAL
AB_MIX = A_WIDTH + B_VAL
C_MAIN = 2 * C_KEY + 2 * C_VAL
O_QA, O_KV, O_ZA, O_XB, O_ZB = 0, 512, 768, 1280, 2816

LANES = 128
SUBLANES = 8
BF16_ROWS = 16
MXU_WIDTH = 2 * LANES
VMEM_LIMIT = 56 * 1024 * 1024

PROMPT_TILE = 512
SAMPLE_BLOCK = 16
NEG = -1e30

NT_DIMS = (((1,), (1,)), ((), ()))
TN_DIMS = (((0,), (0,)), ((), ()))


def _cparams(sem):
    return pltpu.CompilerParams(dimension_semantics=sem, vmem_limit_bytes=VMEM_LIMIT)


def _dot(a, b):
    return jnp.dot(a, b, preferred_element_type=F32)


def _dotg(a, b, dims):
    return lax.dot_general(a, b, dims, preferred_element_type=F32)


def _split3(a):
    hi = a.astype(BF16)
    r = a - hi.astype(F32)
    mid = r.astype(BF16)
    lo = (r - mid.astype(F32)).astype(BF16)
    return hi, mid, lo


def _split2(a):
    hi = a.astype(BF16)
    return hi, (a - hi.astype(F32)).astype(BF16)


def _sigmoid(x):
    return 1.0 / (1.0 + jnp.exp(-x))


def _silu(x):
    return x * _sigmoid(x)


def _softplus(x):
    return jnp.maximum(x, 0.0) + jnp.log1p(jnp.exp(-jnp.abs(x)))


def _log_sigmoid(x):
    return jnp.minimum(x, 0.0) - jnp.log1p(jnp.exp(-jnp.abs(x)))


def _rms(x, g):
    return x * lax.rsqrt(jnp.mean(x * x, axis=-1, keepdims=True) + EPS) * g


def _gla_log_decay(low, wup_ref, bgk_ref):
    lh, lm, _ = _split3(low)
    wh, wm, _ = _split3(wup_ref[...])
    up = _dot(lh, wh) + _dot(lh, wm) + _dot(lm, wh)
    return _log_sigmoid(up + bgk_ref[...]) * (1.0 / C_GATE_NORM)


def _swa_stages(i, qa, kvb, env, sink_ref, *, ct, sb, pos_shift):
    r = lax.broadcasted_iota(jnp.int32, (sb, WINDOW + sb), 0)
    c = lax.broadcasted_iota(jnp.int32, (sb, WINDOW + sb), 1)
    diff = r - c + WINDOW
    in_window = (diff >= 0) & (diff < WINDOW)
    nsb = ct // sb
    items = [(s, j) for s in range(nsb) for j in range(A_KV_HEADS)]
    masks = [jnp.concatenate([in_window & (i * ct + s * sb + c - WINDOW + pos_shift >= 0)] * A_GROUP, axis=0)
             for s in range(nsb)]
    sinks = [jnp.concatenate([jnp.broadcast_to(sink_ref[0:1, j * A_GROUP + g:j * A_GROUP + g + 1], (sb, 1))
                              for g in range(A_GROUP)], axis=0) for j in range(A_KV_HEADS)]
    sc = []
    for s, j in items:
        qs = (jnp.concatenate([qa[s * sb:(s + 1) * sb, (j * A_GROUP + g) * A_HD:(j * A_GROUP + g + 1) * A_HD]
                               for g in range(A_GROUP)], axis=0) * (A_HD ** -0.5)).astype(BF16)
        sc.append(_dotg(qs, kvb[s * sb:s * sb + WINDOW + sb, j * A_HD:(j + 1) * A_HD], NT_DIMS))
        yield
    p, esk = [], []
    for n, (s, j) in enumerate(items):
        scm = jnp.where(masks[s], sc[n], NEG)
        mx = jnp.maximum(jnp.max(scm, axis=-1, keepdims=True), sinks[j])
        p.append(jnp.exp(scm - mx).astype(BF16))
        esk.append(jnp.exp(sinks[j] - mx))
        yield
    pv = []
    ones = jnp.ones((WINDOW + sb, A_HD), BF16)
    for n, (s, j) in enumerate(items):
        vx = jnp.concatenate([kvb[s * sb:s * sb + WINDOW + sb,
                                  A_KV_WIDTH + j * A_HD:A_KV_WIDTH + (j + 1) * A_HD], ones], axis=-1)
        pvx = _dot(p[n], vx)
        pv.append(pvx[:, :A_HD] / (pvx[:, A_HD:] + esk[n]))
        yield
    env['pv'] = pv


def _swa_store(env, mix_scr, *, ct, sb):
    pv = env['pv']
    for s in range(ct // sb):
        o = jnp.concatenate([pv[s * A_KV_HEADS + j][g * sb:(g + 1) * sb]
                             for j in range(A_KV_HEADS) for g in range(A_GROUP)], axis=-1)
        mix_scr[s * sb:(s + 1) * sb, 0:A_WIDTH] = (o * _silu(env['za'][s * sb:(s + 1) * sb])).astype(BF16)


def _gdn_stages(i, cvb, gt, gtt, env, gprm_ref, gprmt_ref, onorm_ref, s_scr, mix_scr, *, ct, n_lead):
    nc = ct // CHUNK
    items = [(c, h) for c in range(nc) for h in range(B_HEADS)]
    n_items = len(items)
    rs = lambda c: slice(c * CHUNK, (c + 1) * CHUNK)
    qn, kn = [], []
    for h in range(B_HEADS):
        q = cvb[h]
        k = cvb[B_HEADS + h]
        qn.append(q * lax.rsqrt(jnp.sum(q * q, axis=-1, keepdims=True) + EPS) * (B_DK ** -0.5))
        kn.append(k * lax.rsqrt(jnp.sum(k * k, axis=-1, keepdims=True) + EPS))
        yield
    beta = _sigmoid(gt)
    rows = i * ct + lax.broadcasted_iota(jnp.int32, gt.shape, 0)
    gcol = jnp.where(rows >= n_lead, -gprm_ref[0:1, :] * _softplus(gt + gprm_ref[1:2, :]), 0.0)
    lanes = i * ct + lax.broadcasted_iota(jnp.int32, gtt.shape, 1)
    grow = jnp.where(lanes >= n_lead, -gprmt_ref[:, 0:1] * _softplus(gtt + gprmt_ref[:, 1:2]), 0.0)
    ii = lax.broadcasted_iota(jnp.int32, (CHUNK, CHUNK), 0)
    jj = lax.broadcasted_iota(jnp.int32, (CHUNK, CHUNK), 1)
    lower = ii >= jj
    strict = ii > jj
    ltri = lower.astype(BF16)
    utri = (ii <= jj).astype(BF16)
    gcum_c = [sum(_dot(ltri, p) for p in _split3(gcol[rs(c)])) for c in range(nc)]
    gcum_r = [sum(_dot(p, utri) for p in _split3(grow[:, rs(c)])) for c in range(nc)]
    yield
    q_ = [qn[h][rs(c)] for c, h in items]
    k_ = [kn[h][rs(c)] for c, h in items]
    v_ = [cvb[2 * B_HEADS + h][rs(c)] for c, h in items]
    bh = [beta[rs(c), h:h + 1] for c, h in items]
    gc = [gcum_c[c][:, B_HEADS + h:B_HEADS + h + 1] for c, h in items]
    gr = [gcum_r[c][B_HEADS + h:B_HEADS + h + 1, :] for c, h in items]
    kb, a, qk = [], [], []
    for n in range(n_items):
        decay = jnp.exp(jnp.where(lower, gc[n] - gr[n], NEG))
        kb.append(k_[n] * bh[n])
        kq = _dotg(jnp.concatenate([kb[n], q_[n]], axis=0).astype(BF16), k_[n].astype(BF16), NT_DIMS)
        a.append(jnp.where(strict, kq[:CHUNK] * decay, 0.0))
        qk.append(jnp.where(lower, kq[CHUNK:] * decay, 0.0).astype(BF16))
        if n % B_HEADS == B_HEADS - 1:
            yield
    doff = [-jnp.where((ii >> 1) == (jj >> 1), a[n], 0.0) for n in range(n_items)]
    for lg in range(1, 6):
        m = ((ii >> (lg + 1)) == (jj >> (lg + 1))) & ((ii >> lg) != (jj >> lg))
        bm = [jnp.where(m, a[n], 0.0) for n in range(n_items)]
        db = [doff[n].astype(BF16) for n in range(n_items)]
        y = [bm[n] + _dot(db[n], bm[n].astype(BF16)) for n in range(n_items)]
        yield
        x = [y[n] + _dot(y[n].astype(BF16), db[n]) for n in range(n_items)]
        doff = [doff[n] - x[n] for n in range(n_items)]
        yield
    egc = [jnp.exp(gc[n]) for n in range(n_items)]
    g_last = [gc[n][CHUNK - 1:CHUNK, :] for n in range(n_items)]
    eg_last = [jnp.exp(g_last[n]) for n in range(n_items)]
    sol, kd, wq = [], [], []
    for n in range(n_items):
        rhs = jnp.concatenate([v_[n] * bh[n], kb[n] * egc[n]], axis=-1)
        sol.append(rhs + _dot(doff[n].astype(BF16), rhs.astype(BF16)))
        kd.append((k_[n] * jnp.exp(g_last[n] - gc[n])).astype(BF16))
        wq.append(jnp.concatenate([sol[n][:, B_DV:], q_[n] * egc[n]], axis=0).astype(BF16))
        if n % B_HEADS == B_HEADS - 1:
            yield

    def gate_store(c, o):
        for h in range(B_HEADS):
            z = env['zb'][rs(c), h * B_DV:(h + 1) * B_DV]
            mix_scr[rs(c), A_WIDTH + h * B_DV:A_WIDTH + (h + 1) * B_DV] = (
                _rms(o[h], onorm_ref[...]) * _silu(z)).astype(BF16)

    s_cur = [s_scr[h] for h in range(B_HEADS)]
    o_prev = None
    for c in range(nc):
        idx = [c * B_HEADS + h for h in range(B_HEADS)]
        ws = [_dot(wq[n], s_cur[h].astype(BF16)) for h, n in enumerate(idx)]
        if o_prev is not None:
            gate_store(c - 1, o_prev)
        yield
        v_new = [sol[n][:, :B_DV] - ws[h][:CHUNK] for h, n in enumerate(idx)]
        vb = [v_new[h].astype(BF16) for h in range(B_HEADS)]
        o_prev = [ws[h][CHUNK:] + _dot(qk[n], vb[h]) for h, n in enumerate(idx)]
        s_cur = [s_cur[h] * eg_last[n] + _dotg(kd[n], vb[h], TN_DIMS) for h, n in enumerate(idx)]
        yield
    for h in range(B_HEADS):
        s_scr[h] = s_cur[h]
    gate_store(nc - 1, o_prev)
    yield


def _ab_layer_kernel(x_ref, g_ref, w_ref, wg_ref, wgt_ref, sink_ref, hist_ref, convw_ref, gprm_ref,
                     gprmt_ref, onorm_ref, wout_ref, convin_ref, sin_ref,
                     y_ref, kvtail_ref, xbtail_ref, sout_ref,
                     s_scr, xc_scr, kvprev_scr, mix_scr, *, ct, n_lead, pos_shift):
    i = pl.program_id(1)
    nsteps = pl.num_programs(1)
    ntail = min(WINDOW, ct)

    @pl.when(i == 0)
    def _():
        s_scr[...] = sin_ref[0]
        xc_scr[0:SUBLANES, :] = convin_ref[0]
        kvprev_scr[1] = hist_ref[...]

    x = x_ref[0]
    hb = _rms(x, g_ref[...]).astype(BF16)
    proj = lambda a, b: _dot(hb, w_ref[:, a:b])
    env = {}

    def step(gen, n=1):
        for _ in range(n):
            next(gen, None)

    nblk = B_CONV_CH // LANES
    w = convw_ref[...]
    cvb = []

    xblk = []
    row8 = lax.broadcasted_iota(jnp.int32, (SUBLANES, LANES), 0)

    def conv_block(j):
        cs = slice(j * LANES, (j + 1) * LANES)
        xj = xblk[j]
        h8 = xc_scr[0:SUBLANES, cs]
        yc = xj * w[CONV_W - 1:CONV_W, cs]
        for k in range(1, CONV_W):
            sh = pltpu.roll(xj, k, 0)
            top = jnp.where(row8 < k, pltpu.roll(h8, k, 0), sh[0:SUBLANES])
            yc = yc + jnp.concatenate([top, sh[SUBLANES:]], axis=0) * w[CONV_W - 1 - k:CONV_W - k, cs]
        cvb.append(_silu(yc))
        xc_scr[0:SUBLANES, cs] = xj[ct - SUBLANES:]

    pw = MXU_WIDTH
    for j in range(nblk // 2):
        xbj = proj(O_XB + j * pw, O_XB + (j + 1) * pw)
        xblk += [xbj[:, 0:LANES], xbj[:, LANES:]]
        xbtail_ref[0, :, j * pw:(j + 1) * pw] = xbj[ct - SUBLANES:]
        if j > 0:
            conv_block(2 * j - 2)
            conv_block(2 * j - 1)
    gt = _dot(hb, wg_ref[...])
    gtt = _dotg(wgt_ref[...], hb, NT_DIMS)
    conv_block(nblk - 2)
    conv_block(nblk - 1)

    gdn = _gdn_stages(i, cvb, gt, gtt, env, gprm_ref, gprmt_ref, onorm_ref, s_scr, mix_scr, ct=ct, n_lead=n_lead)
    qa_p = []
    for j in range(A_WIDTH // pw):
        qa_p.append(proj(O_QA + j * pw, O_QA + (j + 1) * pw))
        step(gdn, 2)
    qa = jnp.concatenate(qa_p, axis=-1)
    kv = proj(O_KV, O_ZA)
    kvtail_ref[0] = kv[ct - ntail:]
    slot = lax.rem(i, 2)
    kvb = jnp.concatenate([kvprev_scr[1 - slot], kv], axis=0).astype(BF16)
    if ct >= WINDOW:
        kvprev_scr[slot] = kv[ct - WINDOW:]
    swa = _swa_stages(i, qa, kvb, env, sink_ref, ct=ct, sb=ntail, pos_shift=pos_shift)
    n_sw = (ct // ntail) * A_KV_HEADS
    step(gdn)
    for _ in range(ct // CHUNK):
        step(gdn)
        step(swa)
    step(swa, max(n_sw - ct // CHUNK, 0))
    za_p, zb_p = [], []
    fill = ([lambda j=j: za_p.append(proj(O_ZA + j * pw, O_ZA + (j + 1) * pw)) for j in range(A_WIDTH // pw)]
            + [lambda j=j: zb_p.append(proj(O_ZB + j * pw, O_ZB + (j + 1) * pw)) for j in range(B_VAL // pw)])
    for lv in range(10):
        step(gdn)
        if lv % 2 == 1 and fill:
            fill.pop(0)()
    for f in fill:
        f()
    env['za'] = jnp.concatenate(za_p, axis=-1)
    env['zb'] = jnp.concatenate(zb_p, axis=-1)
    step(gdn, ct // CHUNK)
    for _ in range(2 * (ct // CHUNK)):
        step(gdn)
        step(swa, -(-2 * n_sw // (2 * (ct // CHUNK))))
    for _ in swa:
        pass
    _swa_store(env, mix_scr, ct=ct, sb=ntail)
    ya = x + _dot(mix_scr[:, 0:A_WIDTH], wout_ref[0:A_WIDTH, :])
    for _ in gdn:
        pass
    y_ref[0] = ya + _dot(mix_scr[:, A_WIDTH:], wout_ref[A_WIDTH:, :])

    @pl.when(i == nsteps - 1)
    def _():
        sout_ref[0] = s_scr[...]


def _ab_layer_prompt(x3, norm_g, w_main, w_gate, w_gate_t, sink, hist_kv, conv_w8, gprm, gprm_t, onorm,
                     w_out, conv_in, s_in, ct, n_lead, pos_shift):
    B, L, _ = x3.shape
    ntail = min(WINDOW, ct)
    kern = functools.partial(_ab_layer_kernel, ct=ct, n_lead=n_lead, pos_shift=pos_shift)
    full = lambda a: pl.BlockSpec(a.shape, lambda b, i: (0,) * a.ndim)
    blk = pl.BlockSpec((1, ct, D_MODEL), lambda b, i: (b, i, 0))
    per_b = lambda *s: pl.BlockSpec((1,) + s, lambda b, i: (b,) + (0,) * len(s))
    return pl.pallas_call(
        kern,
        grid=(B, L // ct),
        in_specs=[blk, full(norm_g), full(w_main), full(w_gate), full(w_gate_t), full(sink), full(hist_kv),
                  full(conv_w8), full(gprm), full(gprm_t), full(onorm), full(w_out),
                  full(conv_in), full(s_in)],
        out_specs=[blk, per_b(ntail, 2 * A_KV_WIDTH), per_b(SUBLANES, B_CONV_CH), per_b(B_HEADS, B_DK, B_DV)],
        out_shape=[jax.ShapeDtypeStruct((B, L, D_MODEL), F32),
                   jax.ShapeDtypeStruct((B, ntail, 2 * A_KV_WIDTH), F32),
                   jax.ShapeDtypeStruct((B, SUBLANES, B_CONV_CH), F32),
                   jax.ShapeDtypeStruct((B, B_HEADS, B_DK, B_DV), F32)],
        scratch_shapes=[pltpu.VMEM((B_HEADS, B_DK, B_DV), F32),
                        pltpu.VMEM((SUBLANES, B_CONV_CH), F32),
                        pltpu.VMEM((2, WINDOW, 2 * A_KV_WIDTH), F32),
                        pltpu.VMEM((ct, AB_MIX), BF16)],
        compiler_params=_cparams(("arbitrary", "arbitrary")),
        name="ab_layer_prompt",
    )(x3, norm_g, w_main, w_gate, w_gate_t, sink, hist_kv, conv_w8, gprm, gprm_t, onorm, w_out, conv_in, s_in)


def _c_layer_kernel(x_ref, g_ref, w_ref, wgk_ref, wup_ref, bgk_ref, onorm_ref, wout_ref, gfin_ref,
                    sin_ref, y_ref, sout_ref, st_scr, og_scr, *, ct, n_lead):
    i = pl.program_id(1)
    nsteps = pl.num_programs(1)

    @pl.when(i == 0)
    def _():
        st_scr[...] = sin_ref[0]

    x = x_ref[0]
    hb = _rms(x, g_ref[...]).astype(BF16)
    low = _dot(hb, wgk_ref[...])
    q_all = _dot(hb, w_ref[:, 0:C_KEY])
    la_all = _gla_log_decay(low, wup_ref, bgk_ref)
    k_all = _dot(hb, w_ref[:, C_KEY:2 * C_KEY])
    v_h, z_h = [], []
    later = ([lambda h=h: v_h.append(_dot(hb, w_ref[:, 2 * C_KEY + h * C_DV:2 * C_KEY + (h + 1) * C_DV]).astype(BF16))
              for h in range(C_HEADS)]
             + [lambda h=h: z_h.append(_dot(hb, w_ref[:, 2 * C_KEY + C_VAL + h * C_DV:
                                                       2 * C_KEY + C_VAL + (h + 1) * C_DV]))
                for h in range(C_HEADS)])

    def run_later(n):
        for _ in range(n):
            if later:
                later.pop(0)()
    rows = i * ct + lax.broadcasted_iota(jnp.int32, la_all.shape, 0)
    la_all = jnp.where(rows >= n_lead, la_all, 0.0)

    ii = lax.broadcasted_iota(jnp.int32, (CHUNK, CHUNK), 0)
    jj = lax.broadcasted_iota(jnp.int32, (CHUNK, CHUNK), 1)
    lower = ii >= jj
    ltri = lower.astype(BF16)
    nc = ct // CHUNK
    rs = lambda c: slice(c * CHUNK, (c + 1) * CHUNK)
    ks = lambda h: slice(h * C_DK, (h + 1) * C_DK)
    vs = lambda h: slice(h * C_DV, (h + 1) * C_DV)
    items = [(c, h) for c in range(nc) for h in range(C_HEADS)]
    bc_all = [sum(_dot(ltri, p) for p in _split2(la_all[rs(c)])) for c in range(nc)]
    run_later(2)
    every = max(nc // C_HEADS, 1)
    qd, kinv, kd, gl, qk = [], [], [], [], []
    for n, (c, h) in enumerate(items):
        bc = bc_all[c][:, ks(h)]
        k = k_all[rs(c), ks(h)]
        bl = bc[CHUNK - 1:CHUNK, :]
        qd.append((q_all[rs(c), ks(h)] * (C_DK ** -0.5) * jnp.exp(bc)).astype(BF16))
        kinv.append((k * jnp.exp(-bc)).astype(BF16))
        kd.append((k * jnp.exp(bl - bc)).astype(BF16))
        gl.append(jnp.exp(bl))
        if h == C_HEADS - 1:
            qk += [jnp.where(lower, _dotg(qd[m], kinv[m], NT_DIMS), 0.0).astype(BF16)
                   for m in range(n - C_HEADS + 1, n + 1)]
            if c % every == every - 1:
                run_later(1)
    run_later(C_HEADS - len(v_h))
    v_ = [v_h[h][rs(c)] for c, h in items]
    o_intra = []
    for c in range(nc):
        o_intra += [_dot(qk[c * C_HEADS + h], v_[c * C_HEADS + h]) for h in range(C_HEADS)]
        if c % every == every - 1:
            run_later(1)
    run_later(len(later))

    rb = min(ct, 4 * CHUNK)

    def gate_store(c, o):
        for h in range(C_HEADS):
            og_scr[rs(c), vs(h)] = (_rms(o[h], onorm_ref[...]) * _silu(z_h[h][rs(c)])).astype(BF16)
        if ((c + 1) * CHUNK) % rb == 0:
            r = slice((c + 1) * CHUNK - rb, (c + 1) * CHUNK)
            y_ref[0, r, :] = _rms(x[r] + _dot(og_scr[r, :], wout_ref[...]), gfin_ref[...])

    st = [st_scr[h] for h in range(C_HEADS)]
    o_prev = None
    pad_rows = jnp.zeros((SUBLANES - C_HEADS, C_DK), F32)
    for c in range(nc):
        idx = [c * C_HEADS + h for h in range(C_HEADS)]
        glt = jnp.concatenate([gl[n] for n in idx] + [pad_rows], axis=0).T
        o = [o_intra[n] + _dot(qd[n], st[h].astype(BF16)) for h, n in enumerate(idx)]
        st = [st[h] * glt[:, h:h + 1] + _dotg(kd[n], v_[n], TN_DIMS) for h, n in enumerate(idx)]
        if o_prev is not None:
            gate_store(c - 1, o_prev)
        o_prev = o
    gate_store(nc - 1, o_prev)
    for h in range(C_HEADS):
        st_scr[h] = st[h]

    @pl.when(i == nsteps - 1)
    def _():
        sout_ref[0] = st_scr[...]


def _c_layer_prompt(x3, norm_g, w_main, w_gk, w_up, b_gk, onorm, w_out, final_g, st_in, ct, n_lead):
    B, L, _ = x3.shape
    kern = functools.partial(_c_layer_kernel, ct=ct, n_lead=n_lead)
    full = lambda a: pl.BlockSpec(a.shape, lambda b, i: (0,) * a.ndim)
    blk = pl.BlockSpec((1, ct, D_MODEL), lambda b, i: (b, i, 0))
    st = pl.BlockSpec((1, C_HEADS, C_DK, C_DV), lambda b, i: (b, 0, 0, 0))
    return pl.pallas_call(
        kern,
        grid=(B, L // ct),
        in_specs=[blk, full(norm_g), full(w_main), full(w_gk), full(w_up), full(b_gk), full(onorm),
                  full(w_out), full(final_g), full(st_in)],
        out_specs=[blk, st],
        out_shape=[jax.ShapeDtypeStruct((B, L, D_MODEL), F32),
                   jax.ShapeDtypeStruct((B, C_HEADS, C_DK, C_DV), F32)],
        scratch_shapes=[pltpu.VMEM((C_HEADS, C_DK, C_DV), F32), pltpu.VMEM((ct, C_VAL), BF16)],
        compiler_params=_cparams(("arbitrary", "arbitrary")),
        name="c_layer_prompt",
    )(x3, norm_g, w_main, w_gk, w_up, b_gk, onorm, w_out, final_g, st_in)


def _ab_proj_kernel(x_ref, g_ref, w_ref, wg_ref, qa_ref, kv_ref, za_ref, xb_ref, zb_ref, gates_ref):
    hb = _rms(x_ref[...], g_ref[...]).astype(BF16)
    qa_ref[...] = _dot(hb, w_ref[:, O_QA:O_KV])
    kv_ref[...] = _dot(hb, w_ref[:, O_KV:O_ZA])
    za_ref[...] = _dot(hb, w_ref[:, O_ZA:O_XB])
    xb_ref[...] = _dot(hb, w_ref[:, O_XB:O_ZB])
    zb_ref[...] = _dot(hb, w_ref[:, O_ZB:AB_MAIN])
    gates_ref[...] = _dot(hb, wg_ref[...])


def _ab_proj(x2d, norm_g, w_main, w_gate):
    n = x2d.shape[0]
    full = lambda a: pl.BlockSpec(a.shape, lambda: (0,) * a.ndim)
    widths = (A_WIDTH, 2 * A_KV_WIDTH, A_WIDTH, B_CONV_CH, B_VAL, LANES)
    args = (x2d, norm_g, w_main, w_gate)
    return pl.pallas_call(
        _ab_proj_kernel,
        in_specs=[full(a) for a in args],
        out_specs=[pl.BlockSpec((n, w), lambda: (0, 0)) for w in widths],
        out_shape=[jax.ShapeDtypeStruct((n, w), F32) for w in widths],
        compiler_params=pltpu.CompilerParams(vmem_limit_bytes=VMEM_LIMIT),
        name="ab_in_proj_sample",
    )(*args)


def _swa_sample_kernel(q_ref, kvn_ref, za_ref, ck_ref, cv_ref, sink_ref, o_ref, nk_ref, nv_ref, *, bt):
    row = lax.broadcasted_iota(jnp.int32, (WINDOW, A_KV_WIDTH), 0)
    hrow = lax.broadcasted_iota(jnp.int32, (A_HEADS, A_HD), 0)
    sk = sink_ref[...]
    nkb, nvb = [], []
    for b in range(bt):
        nk = jnp.where(row == WINDOW - 1, kvn_ref[b:b + 1, 0:A_KV_WIDTH], pltpu.roll(ck_ref[b], WINDOW - 1, 0))
        nv = jnp.where(row == WINDOW - 1, kvn_ref[b:b + 1, A_KV_WIDTH:], pltpu.roll(cv_ref[b], WINDOW - 1, 0))
        nk_ref[b] = nk
        nv_ref[b] = nv
        nkb.append(nk.astype(BF16))
        nvb.append(nv.astype(BF16))
    items = [(b, j) for b in range(bt) for j in range(A_KV_HEADS)]
    q8 = [q_ref[b].astype(BF16) for b in range(bt)]
    sc = [_dotg(q8[b], nkb[b][:, j * A_HD:(j + 1) * A_HD], NT_DIMS) * (A_HD ** -0.5) for b, j in items]
    m = [jnp.maximum(jnp.max(s, axis=-1, keepdims=True), sk) for s in sc]
    p = [jnp.exp(sc[n] - m[n]) for n in range(len(items))]
    den = [jnp.sum(p[n], axis=-1, keepdims=True) + jnp.exp(sk - m[n]) for n in range(len(items))]
    oj = [_dot(p[n].astype(BF16), nvb[b][:, j * A_HD:(j + 1) * A_HD]) / den[n] for n, (b, j) in enumerate(items)]
    for b in range(bt):
        o8 = jnp.where(hrow >= A_GROUP, oj[b * A_KV_HEADS + 1], oj[b * A_KV_HEADS])
        o_ref[b] = o8 * _silu(za_ref[b])


def _swa_sample(q3, kv_new, za3, cache_k, cache_v, sink_col, bt):
    nb = q3.shape[0]
    hd = pl.BlockSpec((bt, A_HEADS, A_HD), lambda i: (i, 0, 0))
    cache = pl.BlockSpec((bt, WINDOW, A_KV_WIDTH), lambda i: (i, 0, 0))
    return pl.pallas_call(
        functools.partial(_swa_sample_kernel, bt=bt),
        grid=(nb // bt,),
        in_specs=[hd, pl.BlockSpec((bt, 2 * A_KV_WIDTH), lambda i: (i, 0)), hd, cache, cache,
                  pl.BlockSpec(sink_col.shape, lambda i: (0, 0))],
        out_specs=[hd, cache, cache],
        out_shape=[jax.ShapeDtypeStruct(q3.shape, F32),
                   jax.ShapeDtypeStruct(cache_k.shape, F32),
                   jax.ShapeDtypeStruct(cache_v.shape, F32)],
        compiler_params=_cparams(("arbitrary",)),
        name="swa_sample",
    )(q3, kv_new, za3, cache_k, cache_v, sink_col)


def _gdn_sample_prep_kernel(xb_ref, h0_ref, h1_ref, h2_ref, gates_ref, convw_ref, gprm_ref,
                            q_ref, k_ref, v_ref, bg_ref):
    w = convw_ref[...]
    y = (h0_ref[...] * w[0:1, :] + h1_ref[...] * w[1:2, :] + h2_ref[...] * w[2:3, :]
         + xb_ref[...] * w[3:4, :])
    cv = _silu(y)
    for h in range(B_HEADS):
        q = cv[:, h * B_DK:(h + 1) * B_DK]
        k = cv[:, B_KEY + h * B_DK:B_KEY + (h + 1) * B_DK]
        q_ref[:, h * B_DK:(h + 1) * B_DK] = (
            q * lax.rsqrt(jnp.sum(q * q, axis=-1, keepdims=True) + EPS) * (B_DK ** -0.5))
        k_ref[:, h * B_DK:(h + 1) * B_DK] = k * lax.rsqrt(jnp.sum(k * k, axis=-1, keepdims=True) + EPS)
    v_ref[...] = cv[:, 2 * B_KEY:]
    gt = gates_ref[...]
    col = lax.broadcasted_iota(jnp.int32, gt.shape, 1)
    g = -gprm_ref[0:1, :] * _softplus(gt + gprm_ref[1:2, :])
    bg_ref[...] = jnp.where(col < B_HEADS, _sigmoid(gt), jnp.exp(g))


def _gdn_sample_prep(xb, h0, h1, h2, gates, conv_w8, gprm):
    n = xb.shape[0]
    full = lambda a: pl.BlockSpec(a.shape, lambda: (0,) * a.ndim)
    args = (xb, h0, h1, h2, gates, conv_w8, gprm)
    return pl.pallas_call(
        _gdn_sample_prep_kernel,
        in_specs=[full(a) for a in args],
        out_specs=[pl.BlockSpec((n, B_KEY), lambda: (0, 0))] * 3 + [pl.BlockSpec((n, LANES), lambda: (0, 0))],
        out_shape=[jax.ShapeDtypeStruct((n, B_KEY), F32)] * 3 + [jax.ShapeDtypeStruct((n, LANES), F32)],
        compiler_params=pltpu.CompilerParams(vmem_limit_bytes=VMEM_LIMIT),
        name="gdn_sample_prep",
    )(*args)


def _gdn_sample_kernel(s_ref, qt_ref, kt_ref, v_ref, bg_ref, zb_ref, onorm_ref, o_ref, sout_ref, *, bt):
    items = [(b, h) for b in range(bt) for h in range(B_HEADS)]
    hs = lambda h: slice(h * B_DV, (h + 1) * B_DV)
    kt = [kt_ref[:, hs(h)].T for h in range(B_HEADS)]
    qg = [[qt_ref[g * SUBLANES:(g + 1) * SUBLANES, hs(h)].astype(BF16) for g in range(bt // SUBLANES)]
          for h in range(B_HEADS)]
    kc = [kt[h][:, b:b + 1] for b, h in items]
    eg = [bg_ref[b:b + 1, B_HEADS + h:B_HEADS + h + 1] for b, h in items]
    ks = [jnp.sum(kc[n] * s_ref[b, h], axis=0, keepdims=True) for n, (b, h) in enumerate(items)]
    v_new = [bg_ref[b:b + 1, h:h + 1] * (v_ref[b:b + 1, hs(h)] - eg[n] * ks[n]) for n, (b, h) in enumerate(items)]
    o = []
    for n, (b, h) in enumerate(items):
        s_new = eg[n] * s_ref[b, h] + kc[n] * v_new[n]
        sout_ref[b, h] = s_new
        o8 = _dot(qg[h][b // SUBLANES], s_new.astype(BF16))
        o.append(o8[b % SUBLANES:b % SUBLANES + 1])
    o_blk = jnp.concatenate([jnp.concatenate([o[b * B_HEADS + h] for h in range(B_HEADS)], axis=-1)
                             for b in range(bt)], axis=0)
    for h in range(B_HEADS):
        o_ref[:, hs(h)] = _rms(o_blk[:, hs(h)], onorm_ref[...]) * _silu(zb_ref[:, hs(h)])


def _gdn_sample(state, qt, kt, v, bg, zb, onorm, bt):
    nb = state.shape[0]
    st = pl.BlockSpec((bt, B_HEADS, B_DK, B_DV), lambda i: (i, 0, 0, 0))
    row = lambda n: pl.BlockSpec((bt, n), lambda i: (i, 0))
    return pl.pallas_call(
        functools.partial(_gdn_sample_kernel, bt=bt),
        grid=(nb // bt,),
        in_specs=[st, row(B_KEY), row(B_KEY), row(B_VAL), row(LANES), row(B_VAL),
                  pl.BlockSpec(onorm.shape, lambda i: (0, 0))],
        out_specs=[row(B_VAL), st],
        out_shape=[jax.ShapeDtypeStruct((nb, B_VAL), F32), jax.ShapeDtypeStruct(state.shape, F32)],
        compiler_params=_cparams(("arbitrary",)),
        name="gdn_sample",
    )(state, qt, kt, v, bg, zb, onorm)


def _ab_out_c_proj_kernel(x_ref, oa_ref, ob_ref, wout_ref, g_ref, w_ref, wgk_ref, wup_ref, bgk_ref,
                          y_ref, q_ref, k_ref, v_ref, z_ref, la_ref):
    y = (x_ref[...] + _dot(oa_ref[...].astype(BF16), wout_ref[0:A_WIDTH, :])
         + _dot(ob_ref[...].astype(BF16), wout_ref[A_WIDTH:, :]))
    y_ref[...] = y
    hb = _rms(y, g_ref[...]).astype(BF16)
    q_ref[...] = _dot(hb, w_ref[:, 0:C_KEY])
    k_ref[...] = _dot(hb, w_ref[:, C_KEY:2 * C_KEY])
    v_ref[...] = _dot(hb, w_ref[:, 2 * C_KEY:2 * C_KEY + C_VAL])
    z_ref[...] = _dot(hb, w_ref[:, 2 * C_KEY + C_VAL:C_MAIN])
    la_ref[...] = _gla_log_decay(_dot(hb, wgk_ref[...]), wup_ref, bgk_ref)


def _ab_out_c_proj(x2d, oa, ob, w_out, norm_g, w_main, w_gk, w_up, b_gk):
    n = x2d.shape[0]
    full = lambda a: pl.BlockSpec(a.shape, lambda: (0,) * a.ndim)
    widths = (D_MODEL, C_KEY, C_KEY, C_VAL, C_VAL, C_KEY)
    args = (x2d, oa, ob, w_out, norm_g, w_main, w_gk, w_up, b_gk)
    return pl.pallas_call(
        _ab_out_c_proj_kernel,
        in_specs=[full(a) for a in args],
        out_specs=[pl.BlockSpec((n, w), lambda: (0, 0)) for w in widths],
        out_shape=[jax.ShapeDtypeStruct((n, w), F32) for w in widths],
        compiler_params=pltpu.CompilerParams(vmem_limit_bytes=VMEM_LIMIT),
        name="ab_out_c_in_proj_sample",
    )(*args)


def _gla_sample_kernel(s_ref, qt_ref, kt_ref, lat_ref, v_ref, z_ref, onorm_ref, o_ref, sout_ref, *, bt):
    items = [(b, h) for b in range(bt) for h in range(C_HEADS)]
    hs = lambda h: slice(h * C_DV, (h + 1) * C_DV)
    ks = lambda h: slice(h * C_DK, (h + 1) * C_DK)
    at = [jnp.exp(lat_ref[:, ks(h)]).T for h in range(C_HEADS)]
    kt = [kt_ref[:, ks(h)].T for h in range(C_HEADS)]
    qt = [(qt_ref[:, ks(h)] * (C_DK ** -0.5)).T for h in range(C_HEADS)]
    o = []
    for n, (b, h) in enumerate(items):
        s_new = at[h][:, b:b + 1] * s_ref[b, h] + kt[h][:, b:b + 1] * v_ref[b:b + 1, hs(h)]
        sout_ref[b, h] = s_new
        o.append(jnp.sum(qt[h][:, b:b + 1] * s_new, axis=0, keepdims=True))
    o_blk = jnp.concatenate([jnp.concatenate([o[b * C_HEADS + h] for h in range(C_HEADS)], axis=-1)
                             for b in range(bt)], axis=0)
    for h in range(C_HEADS):
        o_ref[:, hs(h)] = _rms(o_blk[:, hs(h)], onorm_ref[...]) * _silu(z_ref[:, hs(h)])


def _gla_sample(state, qt, kt, lat, v, z, onorm, bt):
    nb = state.shape[0]
    st = pl.BlockSpec((bt, C_HEADS, C_DK, C_DV), lambda i: (i, 0, 0, 0))
    row = lambda n: pl.BlockSpec((bt, n), lambda i: (i, 0))
    return pl.pallas_call(
        functools.partial(_gla_sample_kernel, bt=bt),
        grid=(nb // bt,),
        in_specs=[st, row(C_KEY), row(C_KEY), row(C_KEY), row(C_VAL), row(C_VAL),
                  pl.BlockSpec(onorm.shape, lambda i: (0, 0))],
        out_specs=[row(C_VAL), st],
        out_shape=[jax.ShapeDtypeStruct((nb, C_VAL), F32), jax.ShapeDtypeStruct(state.shape, F32)],
        compiler_params=_cparams(("arbitrary",)),
        name="gla_sample",
    )(state, qt, kt, lat, v, z, onorm)


def _c_out_kernel(x_ref, o_ref, w_ref, g_ref, y_ref):
    y_ref[...] = _rms(x_ref[...] + _dot(o_ref[...].astype(BF16), w_ref[...]), g_ref[...])


def _c_out(x2d, oc, w_out, final_g):
    n = x2d.shape[0]
    full = lambda a: pl.BlockSpec(a.shape, lambda: (0,) * a.ndim)
    args = (x2d, oc, w_out, final_g)
    return pl.pallas_call(
        _c_out_kernel,
        in_specs=[full(a) for a in args],
        out_specs=pl.BlockSpec((n, D_MODEL), lambda: (0, 0)),
        out_shape=jax.ShapeDtypeStruct((n, D_MODEL), F32),
        compiler_params=pltpu.CompilerParams(vmem_limit_bytes=VMEM_LIMIT),
        name="c_out_proj_norm_sample",
    )(*args)


def kernel(x_prompt, x_sample, cache_swa_k, cache_swa_v, state_dn_conv, state_dn, state_gla,
           meta_tokens, norm_ab, w_in_ab, sink_a, conv_b, a_log_b, dt_bias_b, onorm_b, w_out_ab,
           norm_c, w_in_c, w_gk_up, b_gk, onorm_c, w_out_c, final_norm):
    Bp, L, _ = x_prompt.shape
    Bs = x_sample.shape[0]
    assert L % PROMPT_TILE == 0 and Bs % SAMPLE_BLOCK == 0

    w_ab = w_in_ab[0]
    w_ab_main = w_ab.astype(BF16)
    w_ab_gate = jnp.pad(w_ab[:, AB_MAIN:], ((0, 0), (0, LANES - 2 * B_HEADS))).astype(BF16)
    w_ab_gate_t = jnp.pad(w_ab[:, AB_MAIN:].T, ((0, BF16_ROWS - 2 * B_HEADS), (0, 0))).astype(BF16)
    g_ab = norm_ab[0][None, :]
    sink_row = sink_a[0][None, :]
    sink_col = sink_a[0][:, None]
    conv_w8 = jnp.pad(conv_b[0], ((0, SUBLANES - CONV_W), (0, 0)))
    gp = jnp.stack([jnp.exp(a_log_b[0]), dt_bias_b[0]])
    gprm = jnp.pad(gp, ((0, SUBLANES - 2), (B_HEADS, LANES - 2 * B_HEADS)))
    gprm_t = jnp.pad(gp.T, ((B_HEADS, BF16_ROWS - 2 * B_HEADS), (0, LANES - 2)))
    onb = onorm_b[0][None, :]
    w_ab_out = w_out_ab[0].astype(BF16)
    w_c = w_in_c[0]
    w_c_main = w_c.astype(BF16)
    w_c_gk = jnp.pad(w_c[:, C_MAIN:], ((0, 0), (0, LANES - C_RANK))).astype(BF16)
    w_up = jnp.pad(w_gk_up[0], ((0, LANES - C_RANK), (0, 0)))
    bgk = b_gk[0][None, :]
    g_c = norm_c[0][None, :]
    onc = onorm_c[0][None, :]
    w_c_out = w_out_c[0].astype(BF16)
    g_fin = final_norm[None, :]

    def ab_layer(x3, ct, hist_kv, pos_shift, conv_in, s_in, n_lead):
        return _ab_layer_prompt(x3, g_ab, w_ab_main, w_ab_gate, w_ab_gate_t, sink_row, hist_kv, conv_w8,
                                gprm, gprm_t, onb, w_ab_out, conv_in, s_in, ct, n_lead, pos_shift)

    def c_layer(x3, ct, st_in, n_lead):
        return _c_layer_prompt(x3, g_c, w_c_main, w_c_gk, w_up, bgk, onc, w_c_out, g_fin, st_in, ct, n_lead)

    xpre = jnp.concatenate([jnp.zeros((LEAD, D_MODEL), F32), meta_tokens], axis=0)[None]
    y_pre, kv_pre, xbtail_pre, sdn_pre = ab_layer(
        xpre, CHUNK, jnp.zeros((WINDOW, 2 * A_KV_WIDTH), F32), -LEAD,
        jnp.zeros((1, SUBLANES, B_CONV_CH), F32), jnp.zeros((1, B_HEADS, B_DK, B_DV), F32), LEAD)
    _, sgla_pre = c_layer(y_pre, CHUNK, jnp.zeros((1, C_HEADS, C_DK, C_DV), F32), LEAD)

    hist_kv = jnp.concatenate([jnp.zeros((WINDOW - CHUNK, 2 * A_KV_WIDTH), F32), kv_pre[0]], axis=0)
    y1, kv_tail, xb_tail, sdn_p = ab_layer(
        x_prompt, PROMPT_TILE, hist_kv, N_META, xbtail_pre, sdn_pre, 0)
    y_prompt, sgla_p = c_layer(y1, 2 * PROMPT_TILE, sgla_pre, 0)

    swa_k_p = kv_tail[:, :, :A_KV_WIDTH].reshape(1, Bp, WINDOW, A_KV_HEADS, A_HD)
    swa_v_p = kv_tail[:, :, A_KV_WIDTH:].reshape(1, Bp, WINDOW, A_KV_HEADS, A_HD)
    dn_conv_p = xb_tail[:, SUBLANES - (CONV_W - 1):, :][None]

    xs = x_sample.reshape(Bs, D_MODEL)
    qa, kvn, za, xb, zb, gates = _ab_proj(xs, g_ab, w_ab_main, w_ab_gate)
    oa3, nk, nv = _swa_sample(qa.reshape(Bs, A_HEADS, A_HD), kvn, za.reshape(Bs, A_HEADS, A_HD),
                              cache_swa_k[0].reshape(Bs, WINDOW, A_KV_WIDTH),
                              cache_swa_v[0].reshape(Bs, WINDOW, A_KV_WIDTH), sink_col, SAMPLE_BLOCK)
    hist = state_dn_conv[0]
    qn, kn, vn, bg = _gdn_sample_prep(xb, hist[:, 0], hist[:, 1], hist[:, 2], gates, conv_w8, gprm)
    ob, sdn_s = _gdn_sample(state_dn[0], qn, kn, vn, bg, zb, onb, SAMPLE_BLOCK)
    ys, qc, kc, vc, zc, la = _ab_out_c_proj(xs, oa3.reshape(Bs, A_WIDTH), ob, w_ab_out,
                                            g_c, w_c_main, w_c_gk, w_up, bgk)
    ocs, sgla_s = _gla_sample(state_gla[0], qc, kc, la, vc, zc, onc, SAMPLE_BLOCK)
    y_sample = _c_out(ys, ocs, w_c_out, g_fin).reshape(Bs, 1, D_MODEL)
    dn_conv_s = jnp.concatenate([hist[:, 1:], xb[:, None, :]], axis=1)[None]

    return (y_prompt, y_sample, swa_k_p, swa_v_p, dn_conv_p, sdn_p[None], sgla_p[None],
            nk.reshape(1, Bs, WINDOW, A_KV_HEADS, A_HD), nv.reshape(1, Bs, WINDOW, A_KV_HEADS, A_HD),
            dn_conv_s, sdn_s[None], sgla_s[None])
```

```python
import functools

import jax
import jax.numpy as jnp
from jax import lax
from jax.experimental import pallas as pl
from jax.experimental.pallas import tpu as pltpu

F32 = jnp.float32
BF16 = jnp.bfloat16
EPS = 1e-6

D_MODEL = 1024
N_META = 16
CHUNK = 64
LEAD = (-N_META) % CHUNK
A_HEADS, A_KV_HEADS, A_HD = 8, 2, 64
A_GROUP = A_HEADS // A_KV_HEADS
A_WIDTH = A_HEADS * A_HD
A_KV_WIDTH = A_KV_HEADS * A_HD
WINDOW = 128
B_HEADS, B_DK, B_DV = 4, 128, 128
B_KEY = B_HEADS * B_DK
B_VAL = B_HEADS * B_DV
CONV_W = 4
B_CONV_CH = 2 * B_KEY + B_VAL
C_HEADS, C_DK, C_DV = 4, 128, 256
C_KEY = C_HEADS * C_DK
C_VAL = C_HEADS * C_DV
C_RANK = 16
C_GATE_NORM = 16.0
AB_MAIN = 2 * A_WIDTH + 2 * A_KV_WIDTH + B_CONV_CH + B_VAL
AB_MIX = A_WIDTH + B_VAL
C_MAIN = 2 * C_KEY + 2 * C_VAL
O_QA, O_KV, O_ZA, O_XB, O_ZB = 0, 512, 768, 1280, 2816

LANES = 128
SUBLANES = 8
BF16_ROWS = 16
MXU_WIDTH = 2 * LANES
VMEM_LIMIT = 56 * 1024 * 1024

PROMPT_TILE = 512
SAMPLE_BLOCK = 16
NEG = -1e30

NT_DIMS = (((1,), (1,)), ((), ()))
TN_DIMS = (((0,), (0,)), ((), ()))


def _cparams(sem):
    return pltpu.CompilerParams(dimension_semantics=sem, vmem_limit_bytes=VMEM_LIMIT)


def _dot(a, b):
    return jnp.dot(a, b, preferred_element_type=F32)


def _dotg(a, b, dims):
    return lax.dot_general(a, b, dims, preferred_element_type=F32)


def _split3(a):
    hi = a.astype(BF16)
    r = a - hi.astype(F32)
    mid = r.astype(BF16)
    lo = (r - mid.astype(F32)).astype(BF16)
    return hi, mid, lo


def _split2(a):
    hi = a.astype(BF16)
    return hi, (a - hi.astype(F32)).astype(BF16)


def _sigmoid(x):
    return 1.0 / (1.0 + jnp.exp(-x))


def _silu(x):
    return x * _sigmoid(x)


def _softplus(x):
    return jnp.maximum(x, 0.0) + jnp.log1p(jnp.exp(-jnp.abs(x)))


def _log_sigmoid(x):
    return jnp.minimum(x, 0.0) - jnp.log1p(jnp.exp(-jnp.abs(x)))


def _rms(x, g):
    return x * lax.rsqrt(jnp.mean(x * x, axis=-1, keepdims=True) + EPS) * g


def _gla_log_decay(low, wup_ref, bgk_ref):
    lh, lm, _ = _split3(low)
    wh, wm, _ = _split3(wup_ref[...])
    up = _dot(lh, wh) + _dot(lh, wm) + _dot(lm, wh)
    return _log_sigmoid(up + bgk_ref[...]) * (1.0 / C_GATE_NORM)


def _swa_stages(i, qa, kvb, env, sink_ref, *, ct, sb, pos_shift):
    r = lax.broadcasted_iota(jnp.int32, (sb, WINDOW + sb), 0)
    c = lax.broadcasted_iota(jnp.int32, (sb, WINDOW + sb), 1)
    diff = r - c + WINDOW
    in_window = (diff >= 0) & (diff < WINDOW)
    nsb = ct // sb
    items = [(s, j) for s in range(nsb) for j in range(A_KV_HEADS)]
    masks = [jnp.concatenate([in_window & (i * ct + s * sb + c - WINDOW + pos_shift >= 0)] * A_GROUP, axis=0)
             for s in range(nsb)]
    sinks = [jnp.concatenate([jnp.broadcast_to(sink_ref[0:1, j * A_GROUP + g:j * A_GROUP + g + 1], (sb, 1))
                              for g in range(A_GROUP)], axis=0) for j in range(A_KV_HEADS)]
    sc = []
    for s, j in items:
        qs = (jnp.concatenate([qa[s * sb:(s + 1) * sb, (j * A_GROUP + g) * A_HD:(j * A_GROUP + g + 1) * A_HD]
                               for g in range(A_GROUP)], axis=0) * (A_HD ** -0.5)).astype(BF16)
        sc.append(_dotg(qs, kvb[s * sb:s * sb + WINDOW + sb, j * A_HD:(j + 1) * A_HD], NT_DIMS))
        yield
    p, esk = [], []
    for n, (s, j) in enumerate(items):
        scm = jnp.where(masks[s], sc[n], NEG)
        mx = jnp.maximum(jnp.max(scm, axis=-1, keepdims=True), sinks[j])
        p.append(jnp.exp(scm - mx).astype(BF16))
        esk.append(jnp.exp(sinks[j] - mx))
        yield
    pv = []
    ones = jnp.ones((WINDOW + sb, A_HD), BF16)
    for n, (s, j) in enumerate(items):
        vx = jnp.concatenate([kvb[s * sb:s * sb + WINDOW + sb,
                                  A_KV_WIDTH + j * A_HD:A_KV_WIDTH + (j + 1) * A_HD], ones], axis=-1)
        pvx = _dot(p[n], vx)
        pv.append(pvx[:, :A_HD] / (pvx[:, A_HD:] + esk[n]))
        yield
    env['pv'] = pv


def _swa_store(env, mix_scr, *, ct, sb):
    pv = env['pv']
    for s in range(ct // sb):
        o = jnp.concatenate([pv[s * A_KV_HEADS + j][g * sb:(g + 1) * sb]
                             for j in range(A_KV_HEADS) for g in range(A_GROUP)], axis=-1)
        mix_scr[s * sb:(s + 1) * sb, 0:A_WIDTH] = (o * _silu(env['za'][s * sb:(s + 1) * sb])).astype(BF16)


def _gdn_stages(i, cvb, gt, gtt, env, gprm_ref, gprmt_ref, onorm_ref, s_scr, mix_scr, *, ct, n_lead):
    nc = ct // CHUNK
    items = [(c, h) for c in range(nc) for h in range(B_HEADS)]
    n_items = len(items)
    rs = lambda c: slice(c * CHUNK, (c + 1) * CHUNK)
    qn, kn = [], []
    for h in range(B_HEADS):
        q = cvb[h]
        k = cvb[B_HEADS + h]
        qn.append(q * lax.rsqrt(jnp.sum(q * q, axis=-1, keepdims=True) + EPS) * (B_DK ** -0.5))
        kn.append(k * lax.rsqrt(jnp.sum(k * k, axis=-1, keepdims=True) + EPS))
        yield
    beta = _sigmoid(gt)
    rows = i * ct + lax.broadcasted_iota(jnp.int32, gt.shape, 0)
    gcol = jnp.where(rows >= n_lead, -gprm_ref[0:1, :] * _softplus(gt + gprm_ref[1:2, :]), 0.0)
    lanes = i * ct + lax.broadcasted_iota(jnp.int32, gtt.shape, 1)
    grow = jnp.where(lanes >= n_lead, -gprmt_ref[:, 0:1] * _softplus(gtt + gprmt_ref[:, 1:2]), 0.0)
    ii = lax.broadcasted_iota(jnp.int32, (CHUNK, CHUNK), 0)
    jj = lax.broadcasted_iota(jnp.int32, (CHUNK, CHUNK), 1)
    lower = ii >= jj
    strict = ii > jj
    ltri = lower.astype(BF16)
    utri = (ii <= jj).astype(BF16)
    gcum_c = [sum(_dot(ltri, p) for p in _split3(gcol[rs(c)])) for c in range(nc)]
    gcum_r = [sum(_dot(p, utri) for p in _split3(grow[:, rs(c)])) for c in range(nc)]
    yield
    q_ = [qn[h][rs(c)] for c, h in items]
    k_ = [kn[h][rs(c)] for c, h in items]
    v_ = [cvb[2 * B_HEADS + h][rs(c)] for c, h in items]
    bh = [beta[rs(c), h:h + 1] for c, h in items]
    gc = [gcum_c[c][:, B_HEADS + h:B_HEADS + h + 1] for c, h in items]
    gr = [gcum_r[c][B_HEADS + h:B_HEADS + h + 1, :] for c, h in items]
    kb, a, qk = [], [], []
    for n in range(n_items):
        decay = jnp.exp(jnp.where(lower, gc[n] - gr[n], NEG))
        kb.append(k_[n] * bh[n])
        kq = _dotg(jnp.concatenate([kb[n], q_[n]], axis=0).astype(BF16), k_[n].astype(BF16), NT_DIMS)
        a.append(jnp.where(strict, kq[:CHUNK] * decay, 0.0))
        qk.append(jnp.where(lower, kq[CHUNK:] * decay, 0.0).astype(BF16))
        if n % B_HEADS == B_HEADS - 1:
            yield
    doff = [-jnp.where((ii >> 1) == (jj >> 1), a[n], 0.0) for n in range(n_items)]
    for lg in range(1, 6):
        m = ((ii >> (lg + 1)) == (jj >> (lg + 1))) & ((ii >> lg) != (jj >> lg))
        bm = [jnp.where(m, a[n], 0.0) for n in range(n_items)]
        db = [doff[n].astype(BF16) for n in range(n_items)]
        y = [bm[n] + _dot(db[n], bm[n].astype(BF16)) for n in range(n_items)]
        yield
        x = [y[n] + _dot(y[n].astype(BF16), db[n]) for n in range(n_items)]
        doff = [doff[n] - x[n] for n in range(n_items)]
        yield
    egc = [jnp.exp(gc[n]) for n in range(n_items)]
    g_last = [gc[n][CHUNK - 1:CHUNK, :] for n in range(n_items)]
    eg_last = [jnp.exp(g_last[n]) for n in range(n_items)]
    sol, kd, wq = [], [], []
    for n in range(n_items):
        rhs = jnp.concatenate([v_[n] * bh[n], kb[n] * egc[n]], axis=-1)
        sol.append(rhs + _dot(doff[n].astype(BF16), rhs.astype(BF16)))
        kd.append((k_[n] * jnp.exp(g_last[n] - gc[n])).astype(BF16))
        wq.append(jnp.concatenate([sol[n][:, B_DV:], q_[n] * egc[n]], axis=0).astype(BF16))
        if n % B_HEADS == B_HEADS - 1:
            yield

    def gate_store(c, o):
        for h in range(B_HEADS):
            z = env['zb'][rs(c), h * B_DV:(h + 1) * B_DV]
            mix_scr[rs(c), A_WIDTH + h * B_DV:A_WIDTH + (h + 1) * B_DV] = (
                _rms(o[h], onorm_ref[...]) * _silu(z)).astype(BF16)

    s_cur = [s_scr[h] for h in range(B_HEADS)]
    o_prev = None
    for c in range(nc):
        idx = [c * B_HEADS + h for h in range(B_HEADS)]
        ws = [_dot(wq[n], s_cur[h].astype(BF16)) for h, n in enumerate(idx)]
        if o_prev is not None:
            gate_store(c - 1, o_prev)
        yield
        v_new = [sol[n][:, :B_DV] - ws[h][:CHUNK] for h, n in enumerate(idx)]
        vb = [v_new[h].astype(BF16) for h in range(B_HEADS)]
        o_prev = [ws[h][CHUNK:] + _dot(qk[n], vb[h]) for h, n in enumerate(idx)]
        s_cur = [s_cur[h] * eg_last[n] + _dotg(kd[n], vb[h], TN_DIMS) for h, n in enumerate(idx)]
        yield
    for h in range(B_HEADS):
        s_scr[h] = s_cur[h]
    gate_store(nc - 1, o_prev)
    yield


def _ab_layer_kernel(x_ref, g_ref, w_ref, wg_ref, wgt_ref, sink_ref, hist_ref, convw_ref, gprm_ref,
                     gprmt_ref, onorm_ref, wout_ref, convin_ref, sin_ref,
                     y_ref, kvtail_ref, xbtail_ref, sout_ref,
                     s_scr, xc_scr, kvprev_scr, mix_scr, *, ct, n_lead, pos_shift):
    i = pl.program_id(1)
    nsteps = pl.num_programs(1)
    ntail = min(WINDOW, ct)

    @pl.when(i == 0)
    def _():
        s_scr[...] = sin_ref[0]
        xc_scr[0:SUBLANES, :] = convin_ref[0]
        kvprev_scr[1] = hist_ref[...]

    x = x_ref[0]
    hb = _rms(x, g_ref[...]).astype(BF16)
    proj = lambda a, b: _dot(hb, w_ref[:, a:b])
    env = {}

    def step(gen, n=1):
        for _ in range(n):
            next(gen, None)

    nblk = B_CONV_CH // LANES
    w = convw_ref[...]
    cvb = []

    xblk = []
    row8 = lax.broadcasted_iota(jnp.int32, (SUBLANES, LANES), 0)

    def conv_block(j):
        cs = slice(j * LANES, (j + 1) * LANES)
        xj = xblk[j]
        h8 = xc_scr[0:SUBLANES, cs]
        yc = xj * w[CONV_W - 1:CONV_W, cs]
        for k in range(1, CONV_W):
            sh = pltpu.roll(xj, k, 0)
            top = jnp.where(row8 < k, pltpu.roll(h8, k, 0), sh[0:SUBLANES])
            yc = yc + jnp.concatenate([top, sh[SUBLANES:]], axis=0) * w[CONV_W - 1 - k:CONV_W - k, cs]
        cvb.append(_silu(yc))
        xc_scr[0:SUBLANES, cs] = xj[ct - SUBLANES:]

    pw = MXU_WIDTH
    for j in range(nblk // 2):
        xbj = proj(O_XB + j * pw, O_XB + (j + 1) * pw)
        xblk += [xbj[:, 0:LANES], xbj[:, LANES:]]
        xbtail_ref[0, :, j * pw:(j + 1) * pw] = xbj[ct - SUBLANES:]
        if j > 0:
            conv_block(2 * j - 2)
            conv_block(2 * j - 1)
    gt = _dot(hb, wg_ref[...])
    gtt = _dotg(wgt_ref[...], hb, NT_DIMS)
    conv_block(nblk - 2)
    conv_block(nblk - 1)

    gdn = _gdn_stages(i, cvb, gt, gtt, env, gprm_ref, gprmt_ref, onorm_ref, s_scr, mix_scr, ct=ct, n_lead=n_lead)
    qa_p = []
    for j in range(A_WIDTH // pw):
        qa_p.append(proj(O_QA + j * pw, O_QA + (j + 1) * pw))
        step(gdn, 2)
    qa = jnp.concatenate(qa_p, axis=-1)
    kv = proj(O_KV, O_ZA)
    kvtail_ref[0] = kv[ct - ntail:]
    slot = lax.rem(i, 2)
    kvb = jnp.concatenate([kvprev_scr[1 - slot], kv], axis=0).astype(BF16)
    if ct >= WINDOW:
        kvprev_scr[slot] = kv[ct - WINDOW:]
    swa = _swa_stages(i, qa, kvb, env, sink_ref, ct=ct, sb=ntail, pos_shift=pos_shift)
    n_sw = (ct // ntail) * A_KV_HEADS
    step(gdn)
    for _ in range(ct // CHUNK):
        step(gdn)
        step(swa)
    step(swa, max(n_sw - ct // CHUNK, 0))
    za_p, zb_p = [], []
    fill = ([lambda j=j: za_p.append(proj(O_ZA + j * pw, O_ZA + (j + 1) * pw)) for j in range(A_WIDTH // pw)]
            + [lambda j=j: zb_p.append(proj(O_ZB + j * pw, O_ZB + (j + 1) * pw)) for j in range(B_VAL // pw)])
    for lv in range(10):
        step(gdn)
        if lv % 2 == 1 and fill:
            fill.pop(0)()
    for f in fill:
        f()
    env['za'] = jnp.concatenate(za_p, axis=-1)
    env['zb'] = jnp.concatenate(zb_p, axis=-1)
    step(gdn, ct // CHUNK)
    for _ in range(2 * (ct // CHUNK)):
        step(gdn)
        step(swa, -(-2 * n_sw // (2 * (ct // CHUNK))))
    for _ in swa:
        pass
    _swa_store(env, mix_scr, ct=ct, sb=ntail)
    ya = x + _dot(mix_scr[:, 0:A_WIDTH], wout_ref[0:A_WIDTH, :])
    for _ in gdn:
        pass
    y_ref[0] = ya + _dot(mix_scr[:, A_WIDTH:], wout_ref[A_WIDTH:, :])

    @pl.when(i == nsteps - 1)
    def _():
        sout_ref[0] = s_scr[...]


def _ab_layer_prompt(x3, norm_g, w_main, w_gate, w_gate_t, sink, hist_kv, conv_w8, gprm, gprm_t, onorm,
                     w_out, conv_in, s_in, ct, n_lead, pos_shift):
    B, L, _ = x3.shape
    ntail = min(WINDOW, ct)
    kern = functools.partial(_ab_layer_kernel, ct=ct, n_lead=n_lead, pos_shift=pos_shift)
    full = lambda a: pl.BlockSpec(a.shape, lambda b, i: (0,) * a.ndim)
    blk = pl.BlockSpec((1, ct, D_MODEL), lambda b, i: (b, i, 0))
    per_b = lambda *s: pl.BlockSpec((1,) + s, lambda b, i: (b,) + (0,) * len(s))
    return pl.pallas_call(
        kern,
        grid=(B, L // ct),
        in_specs=[blk, full(norm_g), full(w_main), full(w_gate), full(w_gate_t), full(sink), full(hist_kv),
                  full(conv_w8), full(gprm), full(gprm_t), full(onorm), full(w_out),
                  full(conv_in), full(s_in)],
        out_specs=[blk, per_b(ntail, 2 * A_KV_WIDTH), per_b(SUBLANES, B_CONV_CH), per_b(B_HEADS, B_DK, B_DV)],
        out_shape=[jax.ShapeDtypeStruct((B, L, D_MODEL), F32),
                   jax.ShapeDtypeStruct((B, ntail, 2 * A_KV_WIDTH), F32),
                   jax.ShapeDtypeStruct((B, SUBLANES, B_CONV_CH), F32),
                   jax.ShapeDtypeStruct((B, B_HEADS, B_DK, B_DV), F32)],
        scratch_shapes=[pltpu.VMEM((B_HEADS, B_DK, B_DV), F32),
                        pltpu.VMEM((SUBLANES, B_CONV_CH), F32),
                        pltpu.VMEM((2, WINDOW, 2 * A_KV_WIDTH), F32),
                        pltpu.VMEM((ct, AB_MIX), BF16)],
        compiler_params=_cparams(("arbitrary", "arbitrary")),
        name="ab_layer_prompt",
    )(x3, norm_g, w_main, w_gate, w_gate_t, sink, hist_kv, conv_w8, gprm, gprm_t, onorm, w_out, conv_in, s_in)


def _c_layer_kernel(x_ref, g_ref, w_ref, wgk_ref, wup_ref, bgk_ref, onorm_ref, wout_ref, gfin_ref,
                    sin_ref, y_ref, sout_ref, st_scr, og_scr, *, ct, n_lead):
    i = pl.program_id(1)
    nsteps = pl.num_programs(1)

    @pl.when(i == 0)
    def _():
        st_scr[...] = sin_ref[0]

    x = x_ref[0]
    hb = _rms(x, g_ref[...]).astype(BF16)
    low = _dot(hb, wgk_ref[...])
    q_all = _dot(hb, w_ref[:, 0:C_KEY])
    la_all = _gla_log_decay(low, wup_ref, bgk_ref)
    k_all = _dot(hb, w_ref[:, C_KEY:2 * C_KEY])
    v_h, z_h = [], []
    later = ([lambda h=h: v_h.append(_dot(hb, w_ref[:, 2 * C_KEY + h * C_DV:2 * C_KEY + (h + 1) * C_DV]).astype(BF16))
              for h in range(C_HEADS)]
             + [lambda h=h: z_h.append(_dot(hb, w_ref[:, 2 * C_KEY + C_VAL + h * C_DV:
                                                       2 * C_KEY + C_VAL + (h + 1) * C_DV]))
                for h in range(C_HEADS)])

    def run_later(n):
        for _ in range(n):
            if later:
                later.pop(0)()
    rows = i * ct + lax.broadcasted_iota(jnp.int32, la_all.shape, 0)
    la_all = jnp.where(rows >= n_lead, la_all, 0.0)

    ii = lax.broadcasted_iota(jnp.int32, (CHUNK, CHUNK), 0)
    jj = lax.broadcasted_iota(jnp.int32, (CHUNK, CHUNK), 1)
    lower = ii >= jj
    ltri = lower.astype(BF16)
    nc = ct // CHUNK
    rs = lambda c: slice(c * CHUNK, (c + 1) * CHUNK)
    ks = lambda h: slice(h * C_DK, (h + 1) * C_DK)
    vs = lambda h: slice(h * C_DV, (h + 1) * C_DV)
    items = [(c, h) for c in range(nc) for h in range(C_HEADS)]
    bc_all = [sum(_dot(ltri, p) for p in _split2(la_all[rs(c)])) for c in range(nc)]
    run_later(2)
    every = max(nc // C_HEADS, 1)
    qd, kinv, kd, gl, qk = [], [], [], [], []
    for n, (c, h) in enumerate(items):
        bc = bc_all[c][:, ks(h)]
        k = k_all[rs(c), ks(h)]
        bl = bc[CHUNK - 1:CHUNK, :]
        qd.append((q_all[rs(c), ks(h)] * (C_DK ** -0.5) * jnp.exp(bc)).astype(BF16))
        kinv.append((k * jnp.exp(-bc)).astype(BF16))
        kd.append((k * jnp.exp(bl - bc)).astype(BF16))
        gl.append(jnp.exp(bl))
        if h == C_HEADS - 1:
            qk += [jnp.where(lower, _dotg(qd[m], kinv[m], NT_DIMS), 0.0).astype(BF16)
                   for m in range(n - C_HEADS + 1, n + 1)]
            if c % every == every - 1:
                run_later(1)
    run_later(C_HEADS - len(v_h))
    v_ = [v_h[h][rs(c)] for c, h in items]
    o_intra = []
    for c in range(nc):
        o_intra += [_dot(qk[c * C_HEADS + h], v_[c * C_HEADS + h]) for h in range(C_HEADS)]
        if c % every == every - 1:
            run_later(1)
    run_later(len(later))

    rb = min(ct, 4 * CHUNK)

    def gate_store(c, o):
        for h in range(C_HEADS):
            og_scr[rs(c), vs(h)] = (_rms(o[h], onorm_ref[...]) * _silu(z_h[h][rs(c)])).astype(BF16)
        if ((c + 1) * CHUNK) % rb == 0:
            r = slice((c + 1) * CHUNK - rb, (c + 1) * CHUNK)
            y_ref[0, r, :] = _rms(x[r] + _dot(og_scr[r, :], wout_ref[...]), gfin_ref[...])

    st = [st_scr[h] for h in range(C_HEADS)]
    o_prev = None
    pad_rows = jnp.zeros((SUBLANES - C_HEADS, C_DK), F32)
    for c in range(nc):
        idx = [c * C_HEADS + h for h in range(C_HEADS)]
        glt = jnp.concatenate([gl[n] for n in idx] + [pad_rows], axis=0).T
        o = [o_intra[n] + _dot(qd[n], st[h].astype(BF16)) for h, n in enumerate(idx)]
        st = [st[h] * glt[:, h:h + 1] + _dotg(kd[n], v_[n], TN_DIMS) for h, n in enumerate(idx)]
        if o_prev is not None:
            gate_store(c - 1, o_prev)
        o_prev = o
    gate_store(nc - 1, o_prev)
    for h in range(C_HEADS):
        st_scr[h] = st[h]

    @pl.when(i == nsteps - 1)
    def _():
        sout_ref[0] = st_scr[...]


def _c_layer_prompt(x3, norm_g, w_main, w_gk, w_up, b_gk, onorm, w_out, final_g, st_in, ct, n_lead):
    B, L, _ = x3.shape
    kern = functools.partial(_c_layer_kernel, ct=ct, n_lead=n_lead)
    full = lambda a: pl.BlockSpec(a.shape, lambda b, i: (0,) * a.ndim)
    blk = pl.BlockSpec((1, ct, D_MODEL), lambda b, i: (b, i, 0))
    st = pl.BlockSpec((1, C_HEADS, C_DK, C_DV), lambda b, i: (b, 0, 0, 0))
    return pl.pallas_call(
        kern,
        grid=(B, L // ct),
        in_specs=[blk, full(norm_g), full(w_main), full(w_gk), full(w_up), full(b_gk), full(onorm),
                  full(w_out), full(final_g), full(st_in)],
        out_specs=[blk, st],
        out_shape=[jax.ShapeDtypeStruct((B, L, D_MODEL), F32),
                   jax.ShapeDtypeStruct((B, C_HEADS, C_DK, C_DV), F32)],
        scratch_shapes=[pltpu.VMEM((C_HEADS, C_DK, C_DV), F32), pltpu.VMEM((ct, C_VAL), BF16)],
        compiler_params=_cparams(("arbitrary", "arbitrary")),
        name="c_layer_prompt",
    )(x3, norm_g, w_main, w_gk, w_up, b_gk, onorm, w_out, final_g, st_in)


def _ab_proj_kernel(x_ref, g_ref, w_ref, wg_ref, qa_ref, kv_ref, za_ref, xb_ref, zb_ref, gates_ref):
    hb = _rms(x_ref[...], g_ref[...]).astype(BF16)
    qa_ref[...] = _dot(hb, w_ref[:, O_QA:O_KV])
    kv_ref[...] = _dot(hb, w_ref[:, O_KV:O_ZA])
    za_ref[...] = _dot(hb, w_ref[:, O_ZA:O_XB])
    xb_ref[...] = _dot(hb, w_ref[:, O_XB:O_ZB])
    zb_ref[...] = _dot(hb, w_ref[:, O_ZB:AB_MAIN])
    gates_ref[...] = _dot(hb, wg_ref[...])


def _ab_proj(x2d, norm_g, w_main, w_gate):
    n = x2d.shape[0]
    full = lambda a: pl.BlockSpec(a.shape, lambda: (0,) * a.ndim)
    widths = (A_WIDTH, 2 * A_KV_WIDTH, A_WIDTH, B_CONV_CH, B_VAL, LANES)
    args = (x2d, norm_g, w_main, w_gate)
    return pl.pallas_call(
        _ab_proj_kernel,
        in_specs=[full(a) for a in args],
        out_specs=[pl.BlockSpec((n, w), lambda: (0, 0)) for w in widths],
        out_shape=[jax.ShapeDtypeStruct((n, w), F32) for w in widths],
        compiler_params=pltpu.CompilerParams(vmem_limit_bytes=VMEM_LIMIT),
        name="ab_in_proj_sample",
    )(*args)


def _swa_sample_kernel(q_ref, kvn_ref, za_ref, ck_ref, cv_ref, sink_ref, o_ref, nk_ref, nv_ref, *, bt):
    row = lax.broadcasted_iota(jnp.int32, (WINDOW, A_KV_WIDTH), 0)
    hrow = lax.broadcasted_iota(jnp.int32, (A_HEADS, A_HD), 0)
    sk = sink_ref[...]
    nkb, nvb = [], []
    for b in range(bt):
        nk = jnp.where(row == WINDOW - 1, kvn_ref[b:b + 1, 0:A_KV_WIDTH], pltpu.roll(ck_ref[b], WINDOW - 1, 0))
        nv = jnp.where(row == WINDOW - 1, kvn_ref[b:b + 1, A_KV_WIDTH:], pltpu.roll(cv_ref[b], WINDOW - 1, 0))
        nk_ref[b] = nk
        nv_ref[b] = nv
        nkb.append(nk.astype(BF16))
        nvb.append(nv.astype(BF16))
    items = [(b, j) for b in range(bt) for j in range(A_KV_HEADS)]
    q8 = [q_ref[b].astype(BF16) for b in range(bt)]
    sc = [_dotg(q8[b], nkb[b][:, j * A_HD:(j + 1) * A_HD], NT_DIMS) * (A_HD ** -0.5) for b, j in items]
    m = [jnp.maximum(jnp.max(s, axis=-1, keepdims=True), sk) for s in sc]
    p = [jnp.exp(sc[n] - m[n]) for n in range(len(items))]
    den = [jnp.sum(p[n], axis=-1, keepdims=True) + jnp.exp(sk - m[n]) for n in range(len(items))]
    oj = [_dot(p[n].astype(BF16), nvb[b][:, j * A_HD:(j + 1) * A_HD]) / den[n] for n, (b, j) in enumerate(items)]
    for b in range(bt):
        o8 = jnp.where(hrow >= A_GROUP, oj[b * A_KV_HEADS + 1], oj[b * A_KV_HEADS])
        o_ref[b] = o8 * _silu(za_ref[b])


def _swa_sample(q3, kv_new, za3, cache_k, cache_v, sink_col, bt):
    nb = q3.shape[0]
    hd = pl.BlockSpec((bt, A_HEADS, A_HD), lambda i: (i, 0, 0))
    cache = pl.BlockSpec((bt, WINDOW, A_KV_WIDTH), lambda i: (i, 0, 0))
    return pl.pallas_call(
        functools.partial(_swa_sample_kernel, bt=bt),
        grid=(nb // bt,),
        in_specs=[hd, pl.BlockSpec((bt, 2 * A_KV_WIDTH), lambda i: (i, 0)), hd, cache, cache,
                  pl.BlockSpec(sink_col.shape, lambda i: (0, 0))],
        out_specs=[hd, cache, cache],
        out_shape=[jax.ShapeDtypeStruct(q3.shape, F32),
                   jax.ShapeDtypeStruct(cache_k.shape, F32),
                   jax.ShapeDtypeStruct(cache_v.shape, F32)],
        compiler_params=_cparams(("arbitrary",)),
        name="swa_sample",
    )(q3, kv_new, za3, cache_k, cache_v, sink_col)


def _gdn_sample_prep_kernel(xb_ref, h0_ref, h1_ref, h2_ref, gates_ref, convw_ref, gprm_ref,
                            q_ref, k_ref, v_ref, bg_ref):
    w = convw_ref[...]
    y = (h0_ref[...] * w[0:1, :] + h1_ref[...] * w[1:2, :] + h2_ref[...] * w[2:3, :]
         + xb_ref[...] * w[3:4, :])
    cv = _silu(y)
    for h in range(B_HEADS):
        q = cv[:, h * B_DK:(h + 1) * B_DK]
        k = cv[:, B_KEY + h * B_DK:B_KEY + (h + 1) * B_DK]
        q_ref[:, h * B_DK:(h + 1) * B_DK] = (
            q * lax.rsqrt(jnp.sum(q * q, axis=-1, keepdims=True) + EPS) * (B_DK ** -0.5))
        k_ref[:, h * B_DK:(h + 1) * B_DK] = k * lax.rsqrt(jnp.sum(k * k, axis=-1, keepdims=True) + EPS)
    v_ref[...] = cv[:, 2 * B_KEY:]
    gt = gates_ref[...]
    col = lax.broadcasted_iota(jnp.int32, gt.shape, 1)
    g = -gprm_ref[0:1, :] * _softplus(gt + gprm_ref[1:2, :])
    bg_ref[...] = jnp.where(col < B_HEADS, _sigmoid(gt), jnp.exp(g))


def _gdn_sample_prep(xb, h0, h1, h2, gates, conv_w8, gprm):
    n = xb.shape[0]
    full = lambda a: pl.BlockSpec(a.shape, lambda: (0,) * a.ndim)
    args = (xb, h0, h1, h2, gates, conv_w8, gprm)
    return pl.pallas_call(
        _gdn_sample_prep_kernel,
        in_specs=[full(a) for a in args],
        out_specs=[pl.BlockSpec((n, B_KEY), lambda: (0, 0))] * 3 + [pl.BlockSpec((n, LANES), lambda: (0, 0))],
        out_shape=[jax.ShapeDtypeStruct((n, B_KEY), F32)] * 3 + [jax.ShapeDtypeStruct((n, LANES), F32)],
        compiler_params=pltpu.CompilerParams(vmem_limit_bytes=VMEM_LIMIT),
        name="gdn_sample_prep",
    )(*args)


def _gdn_sample_kernel(s_ref, qt_ref, kt_ref, v_ref, bg_ref, zb_ref, onorm_ref, o_ref, sout_ref, *, bt):
    items = [(b, h) for b in range(bt) for h in range(B_HEADS)]
    hs = lambda h: slice(h * B_DV, (h + 1) * B_DV)
    kt = [kt_ref[:, hs(h)].T for h in range(B_HEADS)]
    qg = [[qt_ref[g * SUBLANES:(g + 1) * SUBLANES, hs(h)].astype(BF16) for g in range(bt // SUBLANES)]
          for h in range(B_HEADS)]
    kc = [kt[h][:, b:b + 1] for b, h in items]
    eg = [bg_ref[b:b + 1, B_HEADS + h:B_HEADS + h + 1] for b, h in items]
    ks = [jnp.sum(kc[n] * s_ref[b, h], axis=0, keepdims=True) for n, (b, h) in enumerate(items)]
    v_new = [bg_ref[b:b + 1, h:h + 1] * (v_ref[b:b + 1, hs(h)] - eg[n] * ks[n]) for n, (b, h) in enumerate(items)]
    o = []
    for n, (b, h) in enumerate(items):
        s_new = eg[n] * s_ref[b, h] + kc[n] * v_new[n]
        sout_ref[b, h] = s_new
        o8 = _dot(qg[h][b // SUBLANES], s_new.astype(BF16))
        o.append(o8[b % SUBLANES:b % SUBLANES + 1])
    o_blk = jnp.concatenate([jnp.concatenate([o[b * B_HEADS + h] for h in range(B_HEADS)], axis=-1)
                             for b in range(bt)], axis=0)
    for h in range(B_HEADS):
        o_ref[:, hs(h)] = _rms(o_blk[:, hs(h)], onorm_ref[...]) * _silu(zb_ref[:, hs(h)])


def _gdn_sample(state, qt, kt, v, bg, zb, onorm, bt):
    nb = state.shape[0]
    st = pl.BlockSpec((bt, B_HEADS, B_DK, B_DV), lambda i: (i, 0, 0, 0))
    row = lambda n: pl.BlockSpec((bt, n), lambda i: (i, 0))
    return pl.pallas_call(
        functools.partial(_gdn_sample_kernel, bt=bt),
        grid=(nb // bt,),
        in_specs=[st, row(B_KEY), row(B_KEY), row(B_VAL), row(LANES), row(B_VAL),
                  pl.BlockSpec(onorm.shape, lambda i: (0, 0))],
        out_specs=[row(B_VAL), st],
        out_shape=[jax.ShapeDtypeStruct((nb, B_VAL), F32), jax.ShapeDtypeStruct(state.shape, F32)],
        compiler_params=_cparams(("arbitrary",)),
        name="gdn_sample",
    )(state, qt, kt, v, bg, zb, onorm)


def _ab_out_c_proj_kernel(x_ref, oa_ref, ob_ref, wout_ref, g_ref, w_ref, wgk_ref, wup_ref, bgk_ref,
                          y_ref, q_ref, k_ref, v_ref, z_ref, la_ref):
    y = (x_ref[...] + _dot(oa_ref[...].astype(BF16), wout_ref[0:A_WIDTH, :])
         + _dot(ob_ref[...].astype(BF16), wout_ref[A_WIDTH:, :]))
    y_ref[...] = y
    hb = _rms(y, g_ref[...]).astype(BF16)
    q_ref[...] = _dot(hb, w_ref[:, 0:C_KEY])
    k_ref[...] = _dot(hb, w_ref[:, C_KEY:2 * C_KEY])
    v_ref[...] = _dot(hb, w_ref[:, 2 * C_KEY:2 * C_KEY + C_VAL])
    z_ref[...] = _dot(hb, w_ref[:, 2 * C_KEY + C_VAL:C_MAIN])
    la_ref[...] = _gla_log_decay(_dot(hb, wgk_ref[...]), wup_ref, bgk_ref)


def _ab_out_c_proj(x2d, oa, ob, w_out, norm_g, w_main, w_gk, w_up, b_gk):
    n = x2d.shape[0]
    full = lambda a: pl.BlockSpec(a.shape, lambda: (0,) * a.ndim)
    widths = (D_MODEL, C_KEY, C_KEY, C_VAL, C_VAL, C_KEY)
    args = (x2d, oa, ob, w_out, norm_g, w_main, w_gk, w_up, b_gk)
    return pl.pallas_call(
        _ab_out_c_proj_kernel,
        in_specs=[full(a) for a in args],
        out_specs=[pl.BlockSpec((n, w), lambda: (0, 0)) for w in widths],
        out_shape=[jax.ShapeDtypeStruct((n, w), F32) for w in widths],
        compiler_params=pltpu.CompilerParams(vmem_limit_bytes=VMEM_LIMIT),
        name="ab_out_c_in_proj_sample",
    )(*args)


def _gla_sample_kernel(s_ref, qt_ref, kt_ref, lat_ref, v_ref, z_ref, onorm_ref, o_ref, sout_ref, *, bt):
    items = [(b, h) for b in range(bt) for h in range(C_HEADS)]
    hs = lambda h: slice(h * C_DV, (h + 1) * C_DV)
    ks = lambda h: slice(h * C_DK, (h + 1) * C_DK)
    at = [jnp.exp(lat_ref[:, ks(h)]).T for h in range(C_HEADS)]
    kt = [kt_ref[:, ks(h)].T for h in range(C_HEADS)]
    qt = [(qt_ref[:, ks(h)] * (C_DK ** -0.5)).T for h in range(C_HEADS)]
    o = []
    for n, (b, h) in enumerate(items):
        s_new = at[h][:, b:b + 1] * s_ref[b, h] + kt[h][:, b:b + 1] * v_ref[b:b + 1, hs(h)]
        sout_ref[b, h] = s_new
        o.append(jnp.sum(qt[h][:, b:b + 1] * s_new, axis=0, keepdims=True))
    o_blk = jnp.concatenate([jnp.concatenate([o[b * C_HEADS + h] for h in range(C_HEADS)], axis=-1)
                             for b in range(bt)], axis=0)
    for h in range(C_HEADS):
        o_ref[:, hs(h)] = _rms(o_blk[:, hs(h)], onorm_ref[...]) * _silu(z_ref[:, hs(h)])


def _gla_sample(state, qt, kt, lat, v, z, onorm, bt):
    nb = state.shape[0]
    st = pl.BlockSpec((bt, C_HEADS, C_DK, C_DV), lambda i: (i, 0, 0, 0))
    row = lambda n: pl.BlockSpec((bt, n), lambda i: (i, 0))
    return pl.pallas_call(
        functools.partial(_gla_sample_kernel, bt=bt),
        grid=(nb // bt,),
        in_specs=[st, row(C_KEY), row(C_KEY), row(C_KEY), row(C_VAL), row(C_VAL),
                  pl.BlockSpec(onorm.shape, lambda i: (0, 0))],
        out_specs=[row(C_VAL), st],
        out_shape=[jax.ShapeDtypeStruct((nb, C_VAL), F32), jax.ShapeDtypeStruct(state.shape, F32)],
        compiler_params=_cparams(("arbitrary",)),
        name="gla_sample",
    )(state, qt, kt, lat, v, z, onorm)


def _c_out_kernel(x_ref, o_ref, w_ref, g_ref, y_ref):
    y_ref[...] = _rms(x_ref[...] + _dot(o_ref[...].astype(BF16), w_ref[...]), g_ref[...])


def _c_out(x2d, oc, w_out, final_g):
    n = x2d.shape[0]
    full = lambda a: pl.BlockSpec(a.shape, lambda: (0,) * a.ndim)
    args = (x2d, oc, w_out, final_g)
    return pl.pallas_call(
        _c_out_kernel,
        in_specs=[full(a) for a in args],
        out_specs=pl.BlockSpec((n, D_MODEL), lambda: (0, 0)),
        out_shape=jax.ShapeDtypeStruct((n, D_MODEL), F32),
        compiler_params=pltpu.CompilerParams(vmem_limit_bytes=VMEM_LIMIT),
        name="c_out_proj_norm_sample",
    )(*args)


def kernel(x_prompt, x_sample, cache_swa_k, cache_swa_v, state_dn_conv, state_dn, state_gla,
           meta_tokens, norm_ab, w_in_ab, sink_a, conv_b, a_log_b, dt_bias_b, onorm_b, w_out_ab,
           norm_c, w_in_c, w_gk_up, b_gk, onorm_c, w_out_c, final_norm):
    Bp, L, _ = x_prompt.shape
    Bs = x_sample.shape[0]
    assert L % PROMPT_TILE == 0 and Bs % SAMPLE_BLOCK == 0

    w_ab = w_in_ab[0]
    w_ab_main = w_ab.astype(BF16)
    w_ab_gate = jnp.pad(w_ab[:, AB_MAIN:], ((0, 0), (0, LANES - 2 * B_HEADS))).astype(BF16)
    w_ab_gate_t = jnp.pad(w_ab[:, AB_MAIN:].T, ((0, BF16_ROWS - 2 * B_HEADS), (0, 0))).astype(BF16)
    g_ab = norm_ab[0][None, :]
    sink_row = sink_a[0][None, :]
    sink_col = sink_a[0][:, None]
    conv_w8 = jnp.pad(conv_b[0], ((0, SUBLANES - CONV_W), (0, 0)))
    gp = jnp.stack([jnp.exp(a_log_b[0]), dt_bias_b[0]])
    gprm = jnp.pad(gp, ((0, SUBLANES - 2), (B_HEADS, LANES - 2 * B_HEADS)))
    gprm_t = jnp.pad(gp.T, ((B_HEADS, BF16_ROWS - 2 * B_HEADS), (0, LANES - 2)))
    onb = onorm_b[0][None, :]
    w_ab_out = w_out_ab[0].astype(BF16)
    w_c = w_in_c[0]
    w_c_main = w_c.astype(BF16)
    w_c_gk = jnp.pad(w_c[:, C_MAIN:], ((0, 0), (0, LANES - C_RANK))).astype(BF16)
    w_up = jnp.pad(w_gk_up[0], ((0, LANES - C_RANK), (0, 0)))
    bgk = b_gk[0][None, :]
    g_c = norm_c[0][None, :]
    onc = onorm_c[0][None, :]
    w_c_out = w_out_c[0].astype(BF16)
    g_fin = final_norm[None, :]

    def ab_layer(x3, ct, hist_kv, pos_shift, conv_in, s_in, n_lead):
        return _ab_layer_prompt(x3, g_ab, w_ab_main, w_ab_gate, w_ab_gate_t, sink_row, hist_kv, conv_w8,
                                gprm, gprm_t, onb, w_ab_out, conv_in, s_in, ct, n_lead, pos_shift)

    def c_layer(x3, ct, st_in, n_lead):
        return _c_layer_prompt(x3, g_c, w_c_main, w_c_gk, w_up, bgk, onc, w_c_out, g_fin, st_in, ct, n_lead)

    xpre = jnp.concatenate([jnp.zeros((LEAD, D_MODEL), F32), meta_tokens], axis=0)[None]
    y_pre, kv_pre, xbtail_pre, sdn_pre = ab_layer(
        xpre, CHUNK, jnp.zeros((WINDOW, 2 * A_KV_WIDTH), F32), -LEAD,
        jnp.zeros((1, SUBLANES, B_CONV_CH), F32), jnp.zeros((1, B_HEADS, B_DK, B_DV), F32), LEAD)
    _, sgla_pre = c_layer(y_pre, CHUNK, jnp.zeros((1, C_HEADS, C_DK, C_DV), F32), LEAD)

    hist_kv = jnp.concatenate([jnp.zeros((WINDOW - CHUNK, 2 * A_KV_WIDTH), F32), kv_pre[0]], axis=0)
    y1, kv_tail, xb_tail, sdn_p = ab_layer(
        x_prompt, PROMPT_TILE, hist_kv, N_META, xbtail_pre, sdn_pre, 0)
    y_prompt, sgla_p = c_layer(y1, 2 * PROMPT_TILE, sgla_pre, 0)

    swa_k_p = kv_tail[:, :, :A_KV_WIDTH].reshape(1, Bp, WINDOW, A_KV_HEADS, A_HD)
    swa_v_p = kv_tail[:, :, A_KV_WIDTH:].reshape(1, Bp, WINDOW, A_KV_HEADS, A_HD)
    dn_conv_p = xb_tail[:, SUBLANES - (CONV_W - 1):, :][None]

    xs = x_sample.reshape(Bs, D_MODEL)
    qa, kvn, za, xb, zb, gates = _ab_proj(xs, g_ab, w_ab_main, w_ab_gate)
    oa3, nk, nv = _swa_sample(qa.reshape(Bs, A_HEADS, A_HD), kvn, za.reshape(Bs, A_HEADS, A_HD),
                              cache_swa_k[0].reshape(Bs, WINDOW, A_KV_WIDTH),
                              cache_swa_v[0].reshape(Bs, WINDOW, A_KV_WIDTH), sink_col, 2 * SAMPLE_BLOCK)
    hist = state_dn_conv[0]
    qn, kn, vn, bg = _gdn_sample_prep(xb, hist[:, 0], hist[:, 1], hist[:, 2], gates, conv_w8, gprm)
    ob, sdn_s = _gdn_sample(state_dn[0], qn, kn, vn, bg, zb, onb, 2 * SAMPLE_BLOCK)
    ys, qc, kc, vc, zc, la = _ab_out_c_proj(xs, oa3.reshape(Bs, A_WIDTH), ob, w_ab_out,
                                            g_c, w_c_main, w_c_gk, w_up, bgk)
    ocs, sgla_s = _gla_sample(state_gla[0], qc, kc, la, vc, zc, onc, SAMPLE_BLOCK)
    y_sample = _c_out(ys, ocs, w_c_out, g_fin).reshape(Bs, 1, D_MODEL)
    dn_conv_s = jnp.concatenate([hist[:, 1:], xb[:, None, :]], axis=1)[None]

    return (y_prompt, y_sample, swa_k_p, swa_v_p, dn_conv_p, sdn_p[None], sgla_p[None],
            nk.reshape(1, Bs, WINDOW, A_KV_HEADS, A_HD), nv.reshape(1, Bs, WINDOW, A_KV_HEADS, A_HD),
            dn_conv_s, sdn_s[None], sgla_s[None])
```

```python
import functools

import jax
import jax.numpy as jnp
from jax import lax
from jax.experimental import pallas as pl
from jax.experimental.pallas import tpu as pltpu

F32 = jnp.float32
BF16 = jnp.bfloat16
EPS = 1e-6

D_MODEL = 1024
N_META = 16
CHUNK = 64
LEAD = (-N_META) % CHUNK
A_HEADS, A_KV_HEADS, A_HD = 8, 2, 64
A_GROUP = A_HEADS // A_KV_HEADS
A_WIDTH = A_HEADS * A_HD
A_KV_WIDTH = A_KV_HEADS * A_HD
WINDOW = 128
B_HEADS, B_DK, B_DV = 4, 128, 128
B_KEY = B_HEADS * B_DK
B_VAL = B_HEADS * B_DV
CONV_W = 4
B_CONV_CH = 2 * B_KEY + B_VAL
C_HEADS, C_DK, C_DV = 4, 128, 256
C_KEY = C_HEADS * C_DK
C_VAL = C_HEADS * C_DV
C_RANK = 16
C_GATE_NORM = 16.0
AB_MAIN = 2 * A_WIDTH + 2 * A_KV_WIDTH + B_CONV_CH + B_VAL
AB_MIX = A_WIDTH + B_VAL
C_MAIN = 2 * C_KEY + 2 * C_VAL
O_QA, O_KV, O_ZA, O_XB, O_ZB = 0, 512, 768, 1280, 2816

LANES = 128
SUBLANES = 8
BF16_ROWS = 16
MXU_WIDTH = 2 * LANES
VMEM_LIMIT = 56 * 1024 * 1024

PROMPT_TILE = 512
SAMPLE_BLOCK = 16
NEG = -1e30

NT_DIMS = (((1,), (1,)), ((), ()))
TN_DIMS = (((0,), (0,)), ((), ()))


def _cparams(sem):
    return pltpu.CompilerParams(dimension_semantics=sem, vmem_limit_bytes=VMEM_LIMIT)


def _dot(a, b):
    return jnp.dot(a, b, preferred_element_type=F32)


def _dotg(a, b, dims):
    return lax.dot_general(a, b, dims, preferred_element_type=F32)


def _split3(a):
    hi = a.astype(BF16)
    r = a - hi.astype(F32)
    mid = r.astype(BF16)
    lo = (r - mid.astype(F32)).astype(BF16)
    return hi, mid, lo


def _split2(a):
    hi = a.astype(BF16)
    return hi, (a - hi.astype(F32)).astype(BF16)


def _sigmoid(x):
    return 1.0 / (1.0 + jnp.exp(-x))


def _silu(x):
    return x * _sigmoid(x)


def _softplus(x):
    return jnp.maximum(x, 0.0) + jnp.log1p(jnp.exp(-jnp.abs(x)))


def _log_sigmoid(x):
    return jnp.minimum(x, 0.0) - jnp.log1p(jnp.exp(-jnp.abs(x)))


def _rms(x, g):
    return x * lax.rsqrt(jnp.mean(x * x, axis=-1, keepdims=True) + EPS) * g


def _gla_log_decay(low, wup_ref, bgk_ref):
    lh, lm, _ = _split3(low)
    wh, wm, _ = _split3(wup_ref[...])
    up = _dot(lh, wh) + _dot(lh, wm) + _dot(lm, wh)
    return _log_sigmoid(up + bgk_ref[...]) * (1.0 / C_GATE_NORM)


def _swa_stages(i, qa, kvb, env, sink_ref, *, ct, sb, pos_shift):
    r = lax.broadcasted_iota(jnp.int32, (sb, WINDOW + sb), 0)
    c = lax.broadcasted_iota(jnp.int32, (sb, WINDOW + sb), 1)
    diff = r - c + WINDOW
    in_window = (diff >= 0) & (diff < WINDOW)
    nsb = ct // sb
    items = [(s, j) for s in range(nsb) for j in range(A_KV_HEADS)]
    masks = [jnp.concatenate([in_window & (i * ct + s * sb + c - WINDOW + pos_shift >= 0)] * A_GROUP, axis=0)
             for s in range(nsb)]
    sinks = [jnp.concatenate([jnp.broadcast_to(sink_ref[0:1, j * A_GROUP + g:j * A_GROUP + g + 1], (sb, 1))
                              for g in range(A_GROUP)], axis=0) for j in range(A_KV_HEADS)]
    sc = []
    for s, j in items:
        qs = (jnp.concatenate([qa[s * sb:(s + 1) * sb, (j * A_GROUP + g) * A_HD:(j * A_GROUP + g + 1) * A_HD]
                               for g in range(A_GROUP)], axis=0) * (A_HD ** -0.5)).astype(BF16)
        sc.append(_dotg(qs, kvb[s * sb:s * sb + WINDOW + sb, j * A_HD:(j + 1) * A_HD], NT_DIMS))
        yield
    p, esk = [], []
    for n, (s, j) in enumerate(items):
        scm = jnp.where(masks[s], sc[n], NEG)
        mx = jnp.maximum(jnp.max(scm, axis=-1, keepdims=True), sinks[j])
        p.append(jnp.exp(scm - mx).astype(BF16))
        esk.append(jnp.exp(sinks[j] - mx))
        yield
    pv = []
    ones = jnp.ones((WINDOW + sb, A_HD), BF16)
    for n, (s, j) in enumerate(items):
        vx = jnp.concatenate([kvb[s * sb:s * sb + WINDOW + sb,
                                  A_KV_WIDTH + j * A_HD:A_KV_WIDTH + (j + 1) * A_HD], ones], axis=-1)
        pvx = _dot(p[n], vx)
        pv.append(pvx[:, :A_HD] / (pvx[:, A_HD:] + esk[n]))
        yield
    env['pv'] = pv


def _swa_store(env, mix_scr, *, ct, sb):
    pv = env['pv']
    for s in range(ct // sb):
        o = jnp.concatenate([pv[s * A_KV_HEADS + j][g * sb:(g + 1) * sb]
                             for j in range(A_KV_HEADS) for g in range(A_GROUP)], axis=-1)
        mix_scr[s * sb:(s + 1) * sb, 0:A_WIDTH] = (o * _silu(env['za'][s * sb:(s + 1) * sb])).astype(BF16)


def _gdn_stages(i, cvb, gt, gtt, env, gprm_ref, gprmt_ref, onorm_ref, s_scr, mix_scr, *, ct, n_lead):
    nc = ct // CHUNK
    items = [(c, h) for c in range(nc) for h in range(B_HEADS)]
    n_items = len(items)
    rs = lambda c: slice(c * CHUNK, (c + 1) * CHUNK)
    qn, kn = [], []
    for h in range(B_HEADS):
        q = cvb[h]
        k = cvb[B_HEADS + h]
        qn.append(q * lax.rsqrt(jnp.sum(q * q, axis=-1, keepdims=True) + EPS) * (B_DK ** -0.5))
        kn.append(k * lax.rsqrt(jnp.sum(k * k, axis=-1, keepdims=True) + EPS))
        yield
    beta = _sigmoid(gt)
    rows = i * ct + lax.broadcasted_iota(jnp.int32, gt.shape, 0)
    gcol = jnp.where(rows >= n_lead, -gprm_ref[0:1, :] * _softplus(gt + gprm_ref[1:2, :]), 0.0)
    lanes = i * ct + lax.broadcasted_iota(jnp.int32, gtt.shape, 1)
    grow = jnp.where(lanes >= n_lead, -gprmt_ref[:, 0:1] * _softplus(gtt + gprmt_ref[:, 1:2]), 0.0)
    ii = lax.broadcasted_iota(jnp.int32, (CHUNK, CHUNK), 0)
    jj = lax.broadcasted_iota(jnp.int32, (CHUNK, CHUNK), 1)
    lower = ii >= jj
    strict = ii > jj
    ltri = lower.astype(BF16)
    utri = (ii <= jj).astype(BF16)
    gcum_c = [sum(_dot(ltri, p) for p in _split3(gcol[rs(c)])) for c in range(nc)]
    gcum_r = [sum(_dot(p, utri) for p in _split3(grow[:, rs(c)])) for c in range(nc)]
    yield
    q_ = [qn[h][rs(c)] for c, h in items]
    k_ = [kn[h][rs(c)] for c, h in items]
    v_ = [cvb[2 * B_HEADS + h][rs(c)] for c, h in items]
    bh = [beta[rs(c), h:h + 1] for c, h in items]
    gc = [gcum_c[c][:, B_HEADS + h:B_HEADS + h + 1] for c, h in items]
    gr = [gcum_r[c][B_HEADS + h:B_HEADS + h + 1, :] for c, h in items]
    kb, a, qk = [], [], []
    for n in range(n_items):
        decay = jnp.exp(jnp.where(lower, gc[n] - gr[n], NEG))
        kb.append(k_[n] * bh[n])
        kq = _dotg(jnp.concatenate([kb[n], q_[n]], axis=0).astype(BF16), k_[n].astype(BF16), NT_DIMS)
        a.append(jnp.where(strict, kq[:CHUNK] * decay, 0.0))
        qk.append(jnp.where(lower, kq[CHUNK:] * decay, 0.0).astype(BF16))
        if n % B_HEADS == B_HEADS - 1:
            yield
    doff = [-jnp.where((ii >> 1) == (jj >> 1), a[n], 0.0) for n in range(n_items)]
    for lg in range(1, 6):
        m = ((ii >> (lg + 1)) == (jj >> (lg + 1))) & ((ii >> lg) != (jj >> lg))
        bm = [jnp.where(m, a[n], 0.0) for n in range(n_items)]
        db = [doff[n].astype(BF16) for n in range(n_items)]
        y = [bm[n] + _dot(db[n], bm[n].astype(BF16)) for n in range(n_items)]
        yield
        x = [y[n] + _dot(y[n].astype(BF16), db[n]) for n in range(n_items)]
        doff = [doff[n] - x[n] for n in range(n_items)]
        yield
    egc = [jnp.exp(gc[n]) for n in range(n_items)]
    g_last = [gc[n][CHUNK - 1:CHUNK, :] for n in range(n_items)]
    eg_last = [jnp.exp(g_last[n]) for n in range(n_items)]
    sol, kd, wq = [], [], []
    for n in range(n_items):
        rhs = jnp.concatenate([v_[n] * bh[n], kb[n] * egc[n]], axis=-1)
        sol.append(rhs + _dot(doff[n].astype(BF16), rhs.astype(BF16)))
        kd.append((k_[n] * jnp.exp(g_last[n] - gc[n])).astype(BF16))
        wq.append(jnp.concatenate([sol[n][:, B_DV:], q_[n] * egc[n]], axis=0).astype(BF16))
        if n % B_HEADS == B_HEADS - 1:
            yield

    def gate_store(c, o):
        for h in range(B_HEADS):
            z = env['zb'][rs(c), h * B_DV:(h + 1) * B_DV]
            mix_scr[rs(c), A_WIDTH + h * B_DV:A_WIDTH + (h + 1) * B_DV] = (
                _rms(o[h], onorm_ref[...]) * _silu(z)).astype(BF16)

    s_cur = [s_scr[h] for h in range(B_HEADS)]
    o_prev = None
    for c in range(nc):
        idx = [c * B_HEADS + h for h in range(B_HEADS)]
        ws = [_dot(wq[n], s_cur[h].astype(BF16)) for h, n in enumerate(idx)]
        if o_prev is not None:
            gate_store(c - 1, o_prev)
        yield
        v_new = [sol[n][:, :B_DV] - ws[h][:CHUNK] for h, n in enumerate(idx)]
        vb = [v_new[h].astype(BF16) for h in range(B_HEADS)]
        o_prev = [ws[h][CHUNK:] + _dot(qk[n], vb[h]) for h, n in enumerate(idx)]
        s_cur = [s_cur[h] * eg_last[n] + _dotg(kd[n], vb[h], TN_DIMS) for h, n in enumerate(idx)]
        yield
    for h in range(B_HEADS):
        s_scr[h] = s_cur[h]
    gate_store(nc - 1, o_prev)
    yield


def _ab_layer_kernel(x_ref, g_ref, w_ref, wg_ref, wgt_ref, sink_ref, hist_ref, convw_ref, gprm_ref,
                     gprmt_ref, onorm_ref, wout_ref, convin_ref, sin_ref,
                     y_ref, kvtail_ref, xbtail_ref, sout_ref,
                     s_scr, xc_scr, kvprev_scr, mix_scr, *, ct, n_lead, pos_shift):
    i = pl.program_id(1)
    nsteps = pl.num_programs(1)
    ntail = min(WINDOW, ct)

    @pl.when(i == 0)
    def _():
        s_scr[...] = sin_ref[0]
        xc_scr[0:SUBLANES, :] = convin_ref[0]
        kvprev_scr[1] = hist_ref[...]

    x = x_ref[0]
    hb = _rms(x, g_ref[...]).astype(BF16)
    proj = lambda a, b: _dot(hb, w_ref[:, a:b])
    env = {}

    def step(gen, n=1):
        for _ in range(n):
            next(gen, None)

    nblk = B_CONV_CH // LANES
    w = convw_ref[...]
    cvb = []

    xblk = []
    row8 = lax.broadcasted_iota(jnp.int32, (SUBLANES, LANES), 0)

    def conv_block(j):
        cs = slice(j * LANES, (j + 1) * LANES)
        xj = xblk[j]
        h8 = xc_scr[0:SUBLANES, cs]
        yc = xj * w[CONV_W - 1:CONV_W, cs]
        for k in range(1, CONV_W):
            sh = pltpu.roll(xj, k, 0)
            top = jnp.where(row8 < k, pltpu.roll(h8, k, 0), sh[0:SUBLANES])
            yc = yc + jnp.concatenate([top, sh[SUBLANES:]], axis=0) * w[CONV_W - 1 - k:CONV_W - k, cs]
        cvb.append(_silu(yc))
        xc_scr[0:SUBLANES, cs] = xj[ct - SUBLANES:]

    pw = MXU_WIDTH
    for j in range(nblk // 2):
        xbj = proj(O_XB + j * pw, O_XB + (j + 1) * pw)
        xblk += [xbj[:, 0:LANES], xbj[:, LANES:]]
        xbtail_ref[0, :, j * pw:(j + 1) * pw] = xbj[ct - SUBLANES:]
        if j > 0:
            conv_block(2 * j - 2)
            conv_block(2 * j - 1)
    gt = _dot(hb, wg_ref[...])
    gtt = _dotg(wgt_ref[...], hb, NT_DIMS)
    conv_block(nblk - 2)
    conv_block(nblk - 1)

    gdn = _gdn_stages(i, cvb, gt, gtt, env, gprm_ref, gprmt_ref, onorm_ref, s_scr, mix_scr, ct=ct, n_lead=n_lead)
    qa_p = []
    for j in range(A_WIDTH // pw):
        qa_p.append(proj(O_QA + j * pw, O_QA + (j + 1) * pw))
        step(gdn, 2)
    qa = jnp.concatenate(qa_p, axis=-1)
    kv = proj(O_KV, O_ZA)
    kvtail_ref[0] = kv[ct - ntail:]
    slot = lax.rem(i, 2)
    kvb = jnp.concatenate([kvprev_scr[1 - slot], kv], axis=0).astype(BF16)
    if ct >= WINDOW:
        kvprev_scr[slot] = kv[ct - WINDOW:]
    swa = _swa_stages(i, qa, kvb, env, sink_ref, ct=ct, sb=ntail, pos_shift=pos_shift)
    n_sw = (ct // ntail) * A_KV_HEADS
    step(gdn)
    for _ in range(ct // CHUNK):
        step(gdn)
        step(swa)
    step(swa, max(n_sw - ct // CHUNK, 0))
    za_p, zb_p = [], []
    fill = ([lambda j=j: za_p.append(proj(O_ZA + j * pw, O_ZA + (j + 1) * pw)) for j in range(A_WIDTH // pw)]
            + [lambda j=j: zb_p.append(proj(O_ZB + j * pw, O_ZB + (j + 1) * pw)) for j in range(B_VAL // pw)])
    for lv in range(10):
        step(gdn)
        if lv % 2 == 1 and fill:
            fill.pop(0)()
    for f in fill:
        f()
    env['za'] = jnp.concatenate(za_p, axis=-1)
    env['zb'] = jnp.concatenate(zb_p, axis=-1)
    step(gdn, ct // CHUNK)
    for _ in range(2 * (ct // CHUNK)):
        step(gdn)
        step(swa, -(-2 * n_sw // (2 * (ct // CHUNK))))
    for _ in swa:
        pass
    _swa_store(env, mix_scr, ct=ct, sb=ntail)
    ya = x + _dot(mix_scr[:, 0:A_WIDTH], wout_ref[0:A_WIDTH, :])
    for _ in gdn:
        pass
    y_ref[0] = ya + _dot(mix_scr[:, A_WIDTH:], wout_ref[A_WIDTH:, :])

    @pl.when(i == nsteps - 1)
    def _():
        sout_ref[0] = s_scr[...]


def _ab_layer_prompt(x3, norm_g, w_main, w_gate, w_gate_t, sink, hist_kv, conv_w8, gprm, gprm_t, onorm,
                     w_out, conv_in, s_in, ct, n_lead, pos_shift):
    B, L, _ = x3.shape
    ntail = min(WINDOW, ct)
    kern = functools.partial(_ab_layer_kernel, ct=ct, n_lead=n_lead, pos_shift=pos_shift)
    full = lambda a: pl.BlockSpec(a.shape, lambda b, i: (0,) * a.ndim)
    blk = pl.BlockSpec((1, ct, D_MODEL), lambda b, i: (b, i, 0))
    per_b = lambda *s: pl.BlockSpec((1,) + s, lambda b, i: (b,) + (0,) * len(s))
    return pl.pallas_call(
        kern,
        grid=(B, L // ct),
        in_specs=[blk, full(norm_g), full(w_main), full(w_gate), full(w_gate_t), full(sink), full(hist_kv),
                  full(conv_w8), full(gprm), full(gprm_t), full(onorm), full(w_out),
                  full(conv_in), full(s_in)],
        out_specs=[blk, per_b(ntail, 2 * A_KV_WIDTH), per_b(SUBLANES, B_CONV_CH), per_b(B_HEADS, B_DK, B_DV)],
        out_shape=[jax.ShapeDtypeStruct((B, L, D_MODEL), F32),
                   jax.ShapeDtypeStruct((B, ntail, 2 * A_KV_WIDTH), F32),
                   jax.ShapeDtypeStruct((B, SUBLANES, B_CONV_CH), F32),
                   jax.ShapeDtypeStruct((B, B_HEADS, B_DK, B_DV), F32)],
        scratch_shapes=[pltpu.VMEM((B_HEADS, B_DK, B_DV), F32),
                        pltpu.VMEM((SUBLANES, B_CONV_CH), F32),
                        pltpu.VMEM((2, WINDOW, 2 * A_KV_WIDTH), F32),
                        pltpu.VMEM((ct, AB_MIX), BF16)],
        compiler_params=_cparams(("arbitrary", "arbitrary")),
        name="ab_layer_prompt",
    )(x3, norm_g, w_main, w_gate, w_gate_t, sink, hist_kv, conv_w8, gprm, gprm_t, onorm, w_out, conv_in, s_in)


def _c_layer_kernel(x_ref, g_ref, w_ref, wgk_ref, wup_ref, bgk_ref, onorm_ref, wout_ref, gfin_ref,
                    sin_ref, y_ref, sout_ref, st_scr, og_scr, *, ct, n_lead):
    i = pl.program_id(1)
    nsteps = pl.num_programs(1)

    @pl.when(i == 0)
    def _():
        st_scr[...] = sin_ref[0]

    x = x_ref[0]
    hb = _rms(x, g_ref[...]).astype(BF16)
    low = _dot(hb, wgk_ref[...])
    q_all = _dot(hb, w_ref[:, 0:C_KEY])
    la_all = _gla_log_decay(low, wup_ref, bgk_ref)
    k_all = _dot(hb, w_ref[:, C_KEY:2 * C_KEY])
    v_h, z_h = [], []
    later = ([lambda h=h: v_h.append(_dot(hb, w_ref[:, 2 * C_KEY + h * C_DV:2 * C_KEY + (h + 1) * C_DV]).astype(BF16))
              for h in range(C_HEADS)]
             + [lambda h=h: z_h.append(_dot(hb, w_ref[:, 2 * C_KEY + C_VAL + h * C_DV:
                                                       2 * C_KEY + C_VAL + (h + 1) * C_DV]))
                for h in range(C_HEADS)])

    def run_later(n):
        for _ in range(n):
            if later:
                later.pop(0)()
    rows = i * ct + lax.broadcasted_iota(jnp.int32, la_all.shape, 0)
    la_all = jnp.where(rows >= n_lead, la_all, 0.0)

    ii = lax.broadcasted_iota(jnp.int32, (CHUNK, CHUNK), 0)
    jj = lax.broadcasted_iota(jnp.int32, (CHUNK, CHUNK), 1)
    lower = ii >= jj
    ltri = lower.astype(BF16)
    nc = ct // CHUNK
    rs = lambda c: slice(c * CHUNK, (c + 1) * CHUNK)
    ks = lambda h: slice(h * C_DK, (h + 1) * C_DK)
    vs = lambda h: slice(h * C_DV, (h + 1) * C_DV)
    items = [(c, h) for c in range(nc) for h in range(C_HEADS)]
    bc_all = [sum(_dot(ltri, p) for p in _split2(la_all[rs(c)])) for c in range(nc)]
    run_later(2)
    every = max(nc // C_HEADS, 1)
    qd, kinv, kd, gl, qk = [], [], [], [], []
    for n, (c, h) in enumerate(items):
        bc = bc_all[c][:, ks(h)]
        k = k_all[rs(c), ks(h)]
        bl = bc[CHUNK - 1:CHUNK, :]
        qd.append((q_all[rs(c), ks(h)] * (C_DK ** -0.5) * jnp.exp(bc)).astype(BF16))
        kinv.append((k * jnp.exp(-bc)).astype(BF16))
        kd.append((k * jnp.exp(bl - bc)).astype(BF16))
        gl.append(jnp.exp(bl))
        if h == C_HEADS - 1:
            qk += [jnp.where(lower, _dotg(qd[m], kinv[m], NT_DIMS), 0.0).astype(BF16)
                   for m in range(n - C_HEADS + 1, n + 1)]
            if c % every == every - 1:
                run_later(1)
    run_later(C_HEADS - len(v_h))
    v_ = [v_h[h][rs(c)] for c, h in items]
    o_intra = []
    for c in range(nc):
        o_intra += [_dot(qk[c * C_HEADS + h], v_[c * C_HEADS + h]) for h in range(C_HEADS)]
        if c % every == every - 1:
            run_later(1)
    run_later(len(later))

    rb = min(ct, 4 * CHUNK)

    def gate_store(c, o):
        for h in range(C_HEADS):
            og_scr[rs(c), vs(h)] = (_rms(o[h], onorm_ref[...]) * _silu(z_h[h][rs(c)])).astype(BF16)
        if ((c + 1) * CHUNK) % rb == 0:
            r = slice((c + 1) * CHUNK - rb, (c + 1) * CHUNK)
            y_ref[0, r, :] = _rms(x[r] + _dot(og_scr[r, :], wout_ref[...]), gfin_ref[...])

    st = [st_scr[h] for h in range(C_HEADS)]
    o_prev = None
    pad_rows = jnp.zeros((SUBLANES - C_HEADS, C_DK), F32)
    for c in range(nc):
        idx = [c * C_HEADS + h for h in range(C_HEADS)]
        glt = jnp.concatenate([gl[n] for n in idx] + [pad_rows], axis=0).T
        o = [o_intra[n] + _dot(qd[n], st[h].astype(BF16)) for h, n in enumerate(idx)]
        st = [st[h] * glt[:, h:h + 1] + _dotg(kd[n], v_[n], TN_DIMS) for h, n in enumerate(idx)]
        if o_prev is not None:
            gate_store(c - 1, o_prev)
        o_prev = o
    gate_store(nc - 1, o_prev)
    for h in range(C_HEADS):
        st_scr[h] = st[h]

    @pl.when(i == nsteps - 1)
    def _():
        sout_ref[0] = st_scr[...]


def _c_layer_prompt(x3, norm_g, w_main, w_gk, w_up, b_gk, onorm, w_out, final_g, st_in, ct, n_lead):
    B, L, _ = x3.shape
    kern = functools.partial(_c_layer_kernel, ct=ct, n_lead=n_lead)
    full = lambda a: pl.BlockSpec(a.shape, lambda b, i: (0,) * a.ndim)
    blk = pl.BlockSpec((1, ct, D_MODEL), lambda b, i: (b, i, 0))
    st = pl.BlockSpec((1, C_HEADS, C_DK, C_DV), lambda b, i: (b, 0, 0, 0))
    return pl.pallas_call(
        kern,
        grid=(B, L // ct),
        in_specs=[blk, full(norm_g), full(w_main), full(w_gk), full(w_up), full(b_gk), full(onorm),
                  full(w_out), full(final_g), full(st_in)],
        out_specs=[blk, st],
        out_shape=[jax.ShapeDtypeStruct((B, L, D_MODEL), F32),
                   jax.ShapeDtypeStruct((B, C_HEADS, C_DK, C_DV), F32)],
        scratch_shapes=[pltpu.VMEM((C_HEADS, C_DK, C_DV), F32), pltpu.VMEM((ct, C_VAL), BF16)],
        compiler_params=_cparams(("arbitrary", "arbitrary")),
        name="c_layer_prompt",
    )(x3, norm_g, w_main, w_gk, w_up, b_gk, onorm, w_out, final_g, st_in)


def _ab_proj_kernel(x_ref, g_ref, w_ref, wg_ref, qa_ref, kv_ref, za_ref, xb_ref, zb_ref, gates_ref):
    hb = _rms(x_ref[...], g_ref[...]).astype(BF16)
    qa_ref[...] = _dot(hb, w_ref[:, O_QA:O_KV])
    kv_ref[...] = _dot(hb, w_ref[:, O_KV:O_ZA])
    za_ref[...] = _dot(hb, w_ref[:, O_ZA:O_XB])
    xb_ref[...] = _dot(hb, w_ref[:, O_XB:O_ZB])
    zb_ref[...] = _dot(hb, w_ref[:, O_ZB:AB_MAIN])
    gates_ref[...] = _dot(hb, wg_ref[...])


def _ab_proj(x2d, norm_g, w_main, w_gate):
    n = x2d.shape[0]
    full = lambda a: pl.BlockSpec(a.shape, lambda: (0,) * a.ndim)
    widths = (A_WIDTH, 2 * A_KV_WIDTH, A_WIDTH, B_CONV_CH, B_VAL, LANES)
    args = (x2d, norm_g, w_main, w_gate)
    return pl.pallas_call(
        _ab_proj_kernel,
        in_specs=[full(a) for a in args],
        out_specs=[pl.BlockSpec((n, w), lambda: (0, 0)) for w in widths],
        out_shape=[jax.ShapeDtypeStruct((n, w), F32) for w in widths],
        compiler_params=pltpu.CompilerParams(vmem_limit_bytes=VMEM_LIMIT),
        name="ab_in_proj_sample",
    )(*args)


def _swa_sample_kernel(q_ref, kvn_ref, za_ref, ck_ref, cv_ref, sink_ref, o_ref, nk_ref, nv_ref, *, bt):
    row = lax.broadcasted_iota(jnp.int32, (WINDOW, A_KV_WIDTH), 0)
    hrow = lax.broadcasted_iota(jnp.int32, (A_HEADS, A_HD), 0)
    sk = sink_ref[...]
    nkb, nvb = [], []
    for b in range(bt):
        nk = jnp.where(row == WINDOW - 1, kvn_ref[b:b + 1, 0:A_KV_WIDTH], pltpu.roll(ck_ref[b], WINDOW - 1, 0))
        nv = jnp.where(row == WINDOW - 1, kvn_ref[b:b + 1, A_KV_WIDTH:], pltpu.roll(cv_ref[b], WINDOW - 1, 0))
        nk_ref[b] = nk
        nv_ref[b] = nv
        nkb.append(nk.astype(BF16))
        nvb.append(nv.astype(BF16))
    items = [(b, j) for b in range(bt) for j in range(A_KV_HEADS)]
    q8 = [q_ref[b].astype(BF16) for b in range(bt)]
    sc = [_dotg(q8[b], nkb[b][:, j * A_HD:(j + 1) * A_HD], NT_DIMS) * (A_HD ** -0.5) for b, j in items]
    m = [jnp.maximum(jnp.max(s, axis=-1, keepdims=True), sk) for s in sc]
    p = [jnp.exp(sc[n] - m[n]) for n in range(len(items))]
    den = [jnp.sum(p[n], axis=-1, keepdims=True) + jnp.exp(sk - m[n]) for n in range(len(items))]
    oj = [_dot(p[n].astype(BF16), nvb[b][:, j * A_HD:(j + 1) * A_HD]) / den[n] for n, (b, j) in enumerate(items)]
    for b in range(bt):
        o8 = jnp.where(hrow >= A_GROUP, oj[b * A_KV_HEADS + 1], oj[b * A_KV_HEADS])
        o_ref[b] = o8 * _silu(za_ref[b])


def _swa_sample(q3, kv_new, za3, cache_k, cache_v, sink_col, bt):
    nb = q3.shape[0]
    hd = pl.BlockSpec((bt, A_HEADS, A_HD), lambda i: (i, 0, 0))
    cache = pl.BlockSpec((bt, WINDOW, A_KV_WIDTH), lambda i: (i, 0, 0))
    return pl.pallas_call(
        functools.partial(_swa_sample_kernel, bt=bt),
        grid=(nb // bt,),
        in_specs=[hd, pl.BlockSpec((bt, 2 * A_KV_WIDTH), lambda i: (i, 0)), hd, cache, cache,
                  pl.BlockSpec(sink_col.shape, lambda i: (0, 0))],
        out_specs=[hd, cache, cache],
        out_shape=[jax.ShapeDtypeStruct(q3.shape, F32),
                   jax.ShapeDtypeStruct(cache_k.shape, F32),
                   jax.ShapeDtypeStruct(cache_v.shape, F32)],
        compiler_params=_cparams(("arbitrary",)),
        name="swa_sample",
    )(q3, kv_new, za3, cache_k, cache_v, sink_col)


def _gdn_sample_prep_kernel(xb_ref, h0_ref, h1_ref, h2_ref, gates_ref, convw_ref, gprm_ref,
                            q_ref, k_ref, v_ref, bg_ref):
    w = convw_ref[...]
    y = (h0_ref[...] * w[0:1, :] + h1_ref[...] * w[1:2, :] + h2_ref[...] * w[2:3, :]
         + xb_ref[...] * w[3:4, :])
    cv = _silu(y)
    for h in range(B_HEADS):
        q = cv[:, h * B_DK:(h + 1) * B_DK]
        k = cv[:, B_KEY + h * B_DK:B_KEY + (h + 1) * B_DK]
        q_ref[:, h * B_DK:(h + 1) * B_DK] = (
            q * lax.rsqrt(jnp.sum(q * q, axis=-1, keepdims=True) + EPS) * (B_DK ** -0.5))
        k_ref[:, h * B_DK:(h + 1) * B_DK] = k * lax.rsqrt(jnp.sum(k * k, axis=-1, keepdims=True) + EPS)
    v_ref[...] = cv[:, 2 * B_KEY:]
    gt = gates_ref[...]
    col = lax.broadcasted_iota(jnp.int32, gt.shape, 1)
    g = -gprm_ref[0:1, :] * _softplus(gt + gprm_ref[1:2, :])
    bg_ref[...] = jnp.where(col < B_HEADS, _sigmoid(gt), jnp.exp(g))


def _gdn_sample_prep(xb, h0, h1, h2, gates, conv_w8, gprm):
    n = xb.shape[0]
    full = lambda a: pl.BlockSpec(a.shape, lambda: (0,) * a.ndim)
    args = (xb, h0, h1, h2, gates, conv_w8, gprm)
    return pl.pallas_call(
        _gdn_sample_prep_kernel,
        in_specs=[full(a) for a in args],
        out_specs=[pl.BlockSpec((n, B_KEY), lambda: (0, 0))] * 3 + [pl.BlockSpec((n, LANES), lambda: (0, 0))],
        out_shape=[jax.ShapeDtypeStruct((n, B_KEY), F32)] * 3 + [jax.ShapeDtypeStruct((n, LANES), F32)],
        compiler_params=pltpu.CompilerParams(vmem_limit_bytes=VMEM_LIMIT),
        name="gdn_sample_prep",
    )(*args)


def _gdn_sample_kernel(s_ref, qt_ref, kt_ref, v_ref, bg_ref, zb_ref, onorm_ref, o_ref, sout_ref, *, bt):
    items = [(b, h) for b in range(bt) for h in range(B_HEADS)]
    hs = lambda h: slice(h * B_DV, (h + 1) * B_DV)
    kt = [kt_ref[:, hs(h)].T for h in range(B_HEADS)]
    qg = [[qt_ref[g * SUBLANES:(g + 1) * SUBLANES, hs(h)].astype(BF16) for g in range(bt // SUBLANES)]
          for h in range(B_HEADS)]
    kc = [kt[h][:, b:b + 1] for b, h in items]
    eg = [bg_ref[b:b + 1, B_HEADS + h:B_HEADS + h + 1] for b, h in items]
    ks = [jnp.sum(kc[n] * s_ref[b, h], axis=0, keepdims=True) for n, (b, h) in enumerate(items)]
    v_new = [bg_ref[b:b + 1, h:h + 1] * (v_ref[b:b + 1, hs(h)] - eg[n] * ks[n]) for n, (b, h) in enumerate(items)]
    o = []
    for n, (b, h) in enumerate(items):
        s_new = eg[n] * s_ref[b, h] + kc[n] * v_new[n]
        sout_ref[b, h] = s_new
        o8 = _dot(qg[h][b // SUBLANES], s_new.astype(BF16))
        o.append(o8[b % SUBLANES:b % SUBLANES + 1])
    o_blk = jnp.concatenate([jnp.concatenate([o[b * B_HEADS + h] for h in range(B_HEADS)], axis=-1)
                             for b in range(bt)], axis=0)
    for h in range(B_HEADS):
        o_ref[:, hs(h)] = _rms(o_blk[:, hs(h)], onorm_ref[...]) * _silu(zb_ref[:, hs(h)])


def _gdn_sample(state, qt, kt, v, bg, zb, onorm, bt):
    nb = state.shape[0]
    st = pl.BlockSpec((bt, B_HEADS, B_DK, B_DV), lambda i: (i, 0, 0, 0))
    row = lambda n: pl.BlockSpec((bt, n), lambda i: (i, 0))
    return pl.pallas_call(
        functools.partial(_gdn_sample_kernel, bt=bt),
        grid=(nb // bt,),
        in_specs=[st, row(B_KEY), row(B_KEY), row(B_VAL), row(LANES), row(B_VAL),
                  pl.BlockSpec(onorm.shape, lambda i: (0, 0))],
        out_specs=[row(B_VAL), st],
        out_shape=[jax.ShapeDtypeStruct((nb, B_VAL), F32), jax.ShapeDtypeStruct(state.shape, F32)],
        compiler_params=_cparams(("arbitrary",)),
        name="gdn_sample",
    )(state, qt, kt, v, bg, zb, onorm)


def _ab_out_c_proj_kernel(x_ref, oa_ref, ob_ref, wout_ref, g_ref, w_ref, wgk_ref, wup_ref, bgk_ref,
                          y_ref, q_ref, k_ref, v_ref, z_ref, la_ref):
    y = (x_ref[...] + _dot(oa_ref[...].astype(BF16), wout_ref[0:A_WIDTH, :])
         + _dot(ob_ref[...].astype(BF16), wout_ref[A_WIDTH:, :]))
    y_ref[...] = y
    hb = _rms(y, g_ref[...]).astype(BF16)
    q_ref[...] = _dot(hb, w_ref[:, 0:C_KEY])
    k_ref[...] = _dot(hb, w_ref[:, C_KEY:2 * C_KEY])
    v_ref[...] = _dot(hb, w_ref[:, 2 * C_KEY:2 * C_KEY + C_VAL])
    z_ref[...] = _dot(hb, w_ref[:, 2 * C_KEY + C_VAL:C_MAIN])
    la_ref[...] = _gla_log_decay(_dot(hb, wgk_ref[...]), wup_ref, bgk_ref)


def _ab_out_c_proj(x2d, oa, ob, w_out, norm_g, w_main, w_gk, w_up, b_gk):
    n = x2d.shape[0]
    full = lambda a: pl.BlockSpec(a.shape, lambda: (0,) * a.ndim)
    widths = (D_MODEL, C_KEY, C_KEY, C_VAL, C_VAL, C_KEY)
    args = (x2d, oa, ob, w_out, norm_g, w_main, w_gk, w_up, b_gk)
    return pl.pallas_call(
        _ab_out_c_proj_kernel,
        in_specs=[full(a) for a in args],
        out_specs=[pl.BlockSpec((n, w), lambda: (0, 0)) for w in widths],
        out_shape=[jax.ShapeDtypeStruct((n, w), F32) for w in widths],
        compiler_params=pltpu.CompilerParams(vmem_limit_bytes=VMEM_LIMIT),
        name="ab_out_c_in_proj_sample",
    )(*args)


def _gla_sample_kernel(s_ref, qt_ref, kt_ref, lat_ref, v_ref, z_ref, onorm_ref, o_ref, sout_ref, *, bt):
    items = [(b, h) for b in range(bt) for h in range(C_HEADS)]
    hs = lambda h: slice(h * C_DV, (h + 1) * C_DV)
    ks = lambda h: slice(h * C_DK, (h + 1) * C_DK)
    at = [jnp.exp(lat_ref[:, ks(h)]).T for h in range(C_HEADS)]
    kt = [kt_ref[:, ks(h)].T for h in range(C_HEADS)]
    qg = [[(qt_ref[g * SUBLANES:(g + 1) * SUBLANES, ks(h)] * (C_DK ** -0.5)).astype(BF16)
           for g in range(bt // SUBLANES)] for h in range(C_HEADS)]
    o = []
    for n, (b, h) in enumerate(items):
        s_new = at[h][:, b:b + 1] * s_ref[b, h] + kt[h][:, b:b + 1] * v_ref[b:b + 1, hs(h)]
        sout_ref[b, h] = s_new
        o8 = _dot(qg[h][b // SUBLANES], s_new.astype(BF16))
        o.append(o8[b % SUBLANES:b % SUBLANES + 1])
    o_blk = jnp.concatenate([jnp.concatenate([o[b * C_HEADS + h] for h in range(C_HEADS)], axis=-1)
                             for b in range(bt)], axis=0)
    for h in range(C_HEADS):
        o_ref[:, hs(h)] = _rms(o_blk[:, hs(h)], onorm_ref[...]) * _silu(z_ref[:, hs(h)])


def _gla_sample(state, qt, kt, lat, v, z, onorm, bt):
    nb = state.shape[0]
    st = pl.BlockSpec((bt, C_HEADS, C_DK, C_DV), lambda i: (i, 0, 0, 0))
    row = lambda n: pl.BlockSpec((bt, n), lambda i: (i, 0))
    return pl.pallas_call(
        functools.partial(_gla_sample_kernel, bt=bt),
        grid=(nb // bt,),
        in_specs=[st, row(C_KEY), row(C_KEY), row(C_KEY), row(C_VAL), row(C_VAL),
                  pl.BlockSpec(onorm.shape, lambda i: (0, 0))],
        out_specs=[row(C_VAL), st],
        out_shape=[jax.ShapeDtypeStruct((nb, C_VAL), F32), jax.ShapeDtypeStruct(state.shape, F32)],
        compiler_params=_cparams(("arbitrary",)),
        name="gla_sample",
    )(state, qt, kt, lat, v, z, onorm)


def _c_out_kernel(x_ref, o_ref, w_ref, g_ref, y_ref):
    y_ref[...] = _rms(x_ref[...] + _dot(o_ref[...].astype(BF16), w_ref[...]), g_ref[...])


def _c_out(x2d, oc, w_out, final_g):
    n = x2d.shape[0]
    full = lambda a: pl.BlockSpec(a.shape, lambda: (0,) * a.ndim)
    args = (x2d, oc, w_out, final_g)
    return pl.pallas_call(
        _c_out_kernel,
        in_specs=[full(a) for a in args],
        out_specs=pl.BlockSpec((n, D_MODEL), lambda: (0, 0)),
        out_shape=jax.ShapeDtypeStruct((n, D_MODEL), F32),
        compiler_params=pltpu.CompilerParams(vmem_limit_bytes=VMEM_LIMIT),
        name="c_out_proj_norm_sample",
    )(*args)


def kernel(x_prompt, x_sample, cache_swa_k, cache_swa_v, state_dn_conv, state_dn, state_gla,
           meta_tokens, norm_ab, w_in_ab, sink_a, conv_b, a_log_b, dt_bias_b, onorm_b, w_out_ab,
           norm_c, w_in_c, w_gk_up, b_gk, onorm_c, w_out_c, final_norm):
    Bp, L, _ = x_prompt.shape
    Bs = x_sample.shape[0]
    assert L % PROMPT_TILE == 0 and Bs % SAMPLE_BLOCK == 0

    w_ab = w_in_ab[0]
    w_ab_main = w_ab.astype(BF16)
    w_ab_gate = jnp.pad(w_ab[:, AB_MAIN:], ((0, 0), (0, LANES - 2 * B_HEADS))).astype(BF16)
    w_ab_gate_t = jnp.pad(w_ab[:, AB_MAIN:].T, ((0, BF16_ROWS - 2 * B_HEADS), (0, 0))).astype(BF16)
    g_ab = norm_ab[0][None, :]
    sink_row = sink_a[0][None, :]
    sink_col = sink_a[0][:, None]
    conv_w8 = jnp.pad(conv_b[0], ((0, SUBLANES - CONV_W), (0, 0)))
    gp = jnp.stack([jnp.exp(a_log_b[0]), dt_bias_b[0]])
    gprm = jnp.pad(gp, ((0, SUBLANES - 2), (B_HEADS, LANES - 2 * B_HEADS)))
    gprm_t = jnp.pad(gp.T, ((B_HEADS, BF16_ROWS - 2 * B_HEADS), (0, LANES - 2)))
    onb = onorm_b[0][None, :]
    w_ab_out = w_out_ab[0].astype(BF16)
    w_c = w_in_c[0]
    w_c_main = w_c.astype(BF16)
    w_c_gk = jnp.pad(w_c[:, C_MAIN:], ((0, 0), (0, LANES - C_RANK))).astype(BF16)
    w_up = jnp.pad(w_gk_up[0], ((0, LANES - C_RANK), (0, 0)))
    bgk = b_gk[0][None, :]
    g_c = norm_c[0][None, :]
    onc = onorm_c[0][None, :]
    w_c_out = w_out_c[0].astype(BF16)
    g_fin = final_norm[None, :]

    def ab_layer(x3, ct, hist_kv, pos_shift, conv_in, s_in, n_lead):
        return _ab_layer_prompt(x3, g_ab, w_ab_main, w_ab_gate, w_ab_gate_t, sink_row, hist_kv, conv_w8,
                                gprm, gprm_t, onb, w_ab_out, conv_in, s_in, ct, n_lead, pos_shift)

    def c_layer(x3, ct, st_in, n_lead):
        return _c_layer_prompt(x3, g_c, w_c_main, w_c_gk, w_up, bgk, onc, w_c_out, g_fin, st_in, ct, n_lead)

    xpre = jnp.concatenate([jnp.zeros((LEAD, D_MODEL), F32), meta_tokens], axis=0)[None]
    y_pre, kv_pre, xbtail_pre, sdn_pre = ab_layer(
        xpre, CHUNK, jnp.zeros((WINDOW, 2 * A_KV_WIDTH), F32), -LEAD,
        jnp.zeros((1, SUBLANES, B_CONV_CH), F32), jnp.zeros((1, B_HEADS, B_DK, B_DV), F32), LEAD)
    _, sgla_pre = c_layer(y_pre, CHUNK, jnp.zeros((1, C_HEADS, C_DK, C_DV), F32), LEAD)

    hist_kv = jnp.concatenate([jnp.zeros((WINDOW - CHUNK, 2 * A_KV_WIDTH), F32), kv_pre[0]], axis=0)
    y1, kv_tail, xb_tail, sdn_p = ab_layer(
        x_prompt, PROMPT_TILE, hist_kv, N_META, xbtail_pre, sdn_pre, 0)
    y_prompt, sgla_p = c_layer(y1, 2 * PROMPT_TILE, sgla_pre, 0)

    swa_k_p = kv_tail[:, :, :A_KV_WIDTH].reshape(1, Bp, WINDOW, A_KV_HEADS, A_HD)
    swa_v_p = kv_tail[:, :, A_KV_WIDTH:].reshape(1, Bp, WINDOW, A_KV_HEADS, A_HD)
    dn_conv_p = xb_tail[:, SUBLANES - (CONV_W - 1):, :][None]

    xs = x_sample.reshape(Bs, D_MODEL)
    qa, kvn, za, xb, zb, gates = _ab_proj(xs, g_ab, w_ab_main, w_ab_gate)
    oa3, nk, nv = _swa_sample(qa.reshape(Bs, A_HEADS, A_HD), kvn, za.reshape(Bs, A_HEADS, A_HD),
                              cache_swa_k[0].reshape(Bs, WINDOW, A_KV_WIDTH),
                              cache_swa_v[0].reshape(Bs, WINDOW, A_KV_WIDTH), sink_col, 2 * SAMPLE_BLOCK)
    hist = state_dn_conv[0]
    qn, kn, vn, bg = _gdn_sample_prep(xb, hist[:, 0], hist[:, 1], hist[:, 2], gates, conv_w8, gprm)
    ob, sdn_s = _gdn_sample(state_dn[0], qn, kn, vn, bg, zb, onb, 2 * SAMPLE_BLOCK)
    ys, qc, kc, vc, zc, la = _ab_out_c_proj(xs, oa3.reshape(Bs, A_WIDTH), ob, w_ab_out,
                                            g_c, w_c_main, w_c_gk, w_up, bgk)
    ocs, sgla_s = _gla_sample(state_gla[0], qc, kc, la, vc, zc, onc, SAMPLE_BLOCK)
    y_sample = _c_out(ys, ocs, w_c_out, g_fin).reshape(Bs, 1, D_MODEL)
    dn_conv_s = jnp.concatenate([hist[:, 1:], xb[:, None, :]], axis=1)[None]

    return (y_prompt, y_sample, swa_k_p, swa_v_p, dn_conv_p, sdn_p[None], sgla_p[None],
            nk.reshape(1, Bs, WINDOW, A_KV_HEADS, A_HD), nv.reshape(1, Bs, WINDOW, A_KV_HEADS, A_HD),
            dn_conv_s, sdn_s[None], sgla_s[None])
```

```python
import functools

import jax
import jax.numpy as jnp
from jax import lax
from jax.experimental import pallas as pl
from jax.experimental.pallas import tpu as pltpu

F32 = jnp.float32
BF16 = jnp.bfloat16
EPS = 1e-6

D_MODEL = 1024
N_META = 16
CHUNK = 64
LEAD = (-N_META) % CHUNK
A_HEADS, A_KV_HEADS, A_HD = 8, 2, 64
A_GROUP = A_HEADS // A_KV_HEADS
A_WIDTH = A_HEADS * A_HD
A_KV_WIDTH = A_KV_HEADS * A_HD
WINDOW = 128
B_HEADS, B_DK, B_DV = 4, 128, 128
B_KEY = B_HEADS * B_DK
B_VAL = B_HEADS * B_DV
CONV_W = 4
B_CONV_CH = 2 * B_KEY + B_VAL
C_HEADS, C_DK, C_DV = 4, 128, 256
C_KEY = C_HEADS * C_DK
C_VAL = C_HEADS * C_DV
C_RANK = 16
C_GATE_NORM = 16.0
AB_MAIN = 2 * A_WIDTH + 2 * A_KV_WIDTH + B_CONV_CH + B_VAL
AB_MIX = A_WIDTH + B_VAL
C_MAIN = 2 * C_KEY + 2 * C_VAL
O_QA, O_KV, O_ZA, O_XB, O_ZB = 0, 512, 768, 1280, 2816

LANES = 128
SUBLANES = 8
BF16_ROWS = 16
MXU_WIDTH = 2 * LANES
VMEM_LIMIT = 56 * 1024 * 1024

PROMPT_TILE = 512
SAMPLE_BLOCK = 16
NEG = -1e30

NT_DIMS = (((1,), (1,)), ((), ()))
TN_DIMS = (((0,), (0,)), ((), ()))


def _cparams(sem):
    return pltpu.CompilerParams(dimension_semantics=sem, vmem_limit_bytes=VMEM_LIMIT)


def _dot(a, b):
    return jnp.dot(a, b, preferred_element_type=F32)


def _dotg(a, b, dims):
    return lax.dot_general(a, b, dims, preferred_element_type=F32)


def _split3(a):
    hi = a.astype(BF16)
    r = a - hi.astype(F32)
    mid = r.astype(BF16)
    lo = (r - mid.astype(F32)).astype(BF16)
    return hi, mid, lo


def _split2(a):
    hi = a.astype(BF16)
    return hi, (a - hi.astype(F32)).astype(BF16)


def _sigmoid(x):
    return 1.0 / (1.0 + jnp.exp(-x))


def _silu(x):
    return x * _sigmoid(x)


def _softplus(x):
    return jnp.maximum(x, 0.0) + jnp.log1p(jnp.exp(-jnp.abs(x)))


def _log_sigmoid(x):
    return jnp.minimum(x, 0.0) - jnp.log1p(jnp.exp(-jnp.abs(x)))


def _rms(x, g):
    return x * lax.rsqrt(jnp.mean(x * x, axis=-1, keepdims=True) + EPS) * g


def _gla_log_decay(low, wup_ref, bgk_ref):
    lh, lm, _ = _split3(low)
    wh, wm, _ = _split3(wup_ref[...])
    up = _dot(lh, wh) + _dot(lh, wm) + _dot(lm, wh)
    return _log_sigmoid(up + bgk_ref[...]) * (1.0 / C_GATE_NORM)


def _swa_stages(i, qa, kvb, env, sink_ref, *, ct, sb, pos_shift):
    r = lax.broadcasted_iota(jnp.int32, (sb, WINDOW + sb), 0)
    c = lax.broadcasted_iota(jnp.int32, (sb, WINDOW + sb), 1)
    diff = r - c + WINDOW
    in_window = (diff >= 0) & (diff < WINDOW)
    nsb = ct // sb
    items = [(s, j) for s in range(nsb) for j in range(A_KV_HEADS)]
    masks = [jnp.concatenate([in_window & (i * ct + s * sb + c - WINDOW + pos_shift >= 0)] * A_GROUP, axis=0)
             for s in range(nsb)]
    sinks = [jnp.concatenate([jnp.broadcast_to(sink_ref[0:1, j * A_GROUP + g:j * A_GROUP + g + 1], (sb, 1))
                              for g in range(A_GROUP)], axis=0) for j in range(A_KV_HEADS)]
    sc = []
    for s, j in items:
        qs = (jnp.concatenate([qa[s * sb:(s + 1) * sb, (j * A_GROUP + g) * A_HD:(j * A_GROUP + g + 1) * A_HD]
                               for g in range(A_GROUP)], axis=0) * (A_HD ** -0.5)).astype(BF16)
        sc.append(_dotg(qs, kvb[s * sb:s * sb + WINDOW + sb, j * A_HD:(j + 1) * A_HD], NT_DIMS))
        yield
    p, esk = [], []
    for n, (s, j) in enumerate(items):
        scm = jnp.where(masks[s], sc[n], NEG)
        mx = jnp.maximum(jnp.max(scm, axis=-1, keepdims=True), sinks[j])
        p.append(jnp.exp(scm - mx).astype(BF16))
        esk.append(jnp.exp(sinks[j] - mx))
        yield
    pv = []
    ones = jnp.ones((WINDOW + sb, A_HD), BF16)
    for n, (s, j) in enumerate(items):
        vx = jnp.concatenate([kvb[s * sb:s * sb + WINDOW + sb,
                                  A_KV_WIDTH + j * A_HD:A_KV_WIDTH + (j + 1) * A_HD], ones], axis=-1)
        pvx = _dot(p[n], vx)
        pv.append(pvx[:, :A_HD] / (pvx[:, A_HD:] + esk[n]))
        yield
    env['pv'] = pv


def _swa_store(env, mix_scr, *, ct, sb):
    pv = env['pv']
    for s in range(ct // sb):
        o = jnp.concatenate([pv[s * A_KV_HEADS + j][g * sb:(g + 1) * sb]
                             for j in range(A_KV_HEADS) for g in range(A_GROUP)], axis=-1)
        mix_scr[s * sb:(s + 1) * sb, 0:A_WIDTH] = (o * _silu(env['za'][s * sb:(s + 1) * sb])).astype(BF16)


def _gdn_stages(i, cvb, gt, gtt, env, gprm_ref, gprmt_ref, onorm_ref, s_scr, mix_scr, *, ct, n_lead):
    nc = ct // CHUNK
    items = [(c, h) for c in range(nc) for h in range(B_HEADS)]
    n_items = len(items)
    rs = lambda c: slice(c * CHUNK, (c + 1) * CHUNK)
    qn, kn = [], []
    for h in range(B_HEADS):
        q = cvb[h]
        k = cvb[B_HEADS + h]
        qn.append(q * lax.rsqrt(jnp.sum(q * q, axis=-1, keepdims=True) + EPS) * (B_DK ** -0.5))
        kn.append(k * lax.rsqrt(jnp.sum(k * k, axis=-1, keepdims=True) + EPS))
        yield
    beta = _sigmoid(gt)
    rows = i * ct + lax.broadcasted_iota(jnp.int32, gt.shape, 0)
    gcol = jnp.where(rows >= n_lead, -gprm_ref[0:1, :] * _softplus(gt + gprm_ref[1:2, :]), 0.0)
    lanes = i * ct + lax.broadcasted_iota(jnp.int32, gtt.shape, 1)
    grow = jnp.where(lanes >= n_lead, -gprmt_ref[:, 0:1] * _softplus(gtt + gprmt_ref[:, 1:2]), 0.0)
    ii = lax.broadcasted_iota(jnp.int32, (CHUNK, CHUNK), 0)
    jj = lax.broadcasted_iota(jnp.int32, (CHUNK, CHUNK), 1)
    lower = ii >= jj
    strict = ii > jj
    ltri = lower.astype(BF16)
    utri = (ii <= jj).astype(BF16)
    gcum_c = [sum(_dot(ltri, p) for p in _split3(gcol[rs(c)])) for c in range(nc)]
    gcum_r = [sum(_dot(p, utri) for p in _split3(grow[:, rs(c)])) for c in range(nc)]
    yield
    q_ = [qn[h][rs(c)] for c, h in items]
    k_ = [kn[h][rs(c)] for c, h in items]
    v_ = [cvb[2 * B_HEADS + h][rs(c)] for c, h in items]
    bh = [beta[rs(c), h:h + 1] for c, h in items]
    gc = [gcum_c[c][:, B_HEADS + h:B_HEADS + h + 1] for c, h in items]
    gr = [gcum_r[c][B_HEADS + h:B_HEADS + h + 1, :] for c, h in items]
    kb, a, qk = [], [], []
    for n in range(n_items):
        decay = jnp.exp(jnp.where(lower, gc[n] - gr[n], NEG))
        kb.append(k_[n] * bh[n])
        kq = _dotg(jnp.concatenate([kb[n], q_[n]], axis=0).astype(BF16), k_[n].astype(BF16), NT_DIMS)
        a.append(jnp.where(strict, kq[:CHUNK] * decay, 0.0))
        qk.append(jnp.where(lower, kq[CHUNK:] * decay, 0.0).astype(BF16))
        if n % B_HEADS == B_HEADS - 1:
            yield
    doff = [-jnp.where((ii >> 1) == (jj >> 1), a[n], 0.0) for n in range(n_items)]
    for lg in range(1, 6):
        m = ((ii >> (lg + 1)) == (jj >> (lg + 1))) & ((ii >> lg) != (jj >> lg))
        bm = [jnp.where(m, a[n], 0.0) for n in range(n_items)]
        db = [doff[n].astype(BF16) for n in range(n_items)]
        y = [bm[n] + _dot(db[n], bm[n].astype(BF16)) for n in range(n_items)]
        yield
        x = [y[n] + _dot(y[n].astype(BF16), db[n]) for n in range(n_items)]
        doff = [doff[n] - x[n] for n in range(n_items)]
        yield
    egc = [jnp.exp(gc[n]) for n in range(n_items)]
    g_last = [gc[n][CHUNK - 1:CHUNK, :] for n in range(n_items)]
    eg_last = [jnp.exp(g_last[n]) for n in range(n_items)]
    sol, kd, wq = [], [], []
    for n in range(n_items):
        rhs = jnp.concatenate([v_[n] * bh[n], kb[n] * egc[n]], axis=-1)
        sol.append(rhs + _dot(doff[n].astype(BF16), rhs.astype(BF16)))
        kd.append((k_[n] * jnp.exp(g_last[n] - gc[n])).astype(BF16))
        wq.append(jnp.concatenate([sol[n][:, B_DV:], q_[n] * egc[n]], axis=0).astype(BF16))
        if n % B_HEADS == B_HEADS - 1:
            yield

    def gate_store(c, o):
        for h in range(B_HEADS):
            z = env['zb'][rs(c), h * B_DV:(h + 1) * B_DV]
            mix_scr[rs(c), A_WIDTH + h * B_DV:A_WIDTH + (h + 1) * B_DV] = (
                _rms(o[h], onorm_ref[...]) * _silu(z)).astype(BF16)

    s_cur = [s_scr[h] for h in range(B_HEADS)]
    o_prev = None
    for c in range(nc):
        idx = [c * B_HEADS + h for h in range(B_HEADS)]
        ws = [_dot(wq[n], s_cur[h].astype(BF16)) for h, n in enumerate(idx)]
        if o_prev is not None:
            gate_store(c - 1, o_prev)
        yield
        v_new = [sol[n][:, :B_DV] - ws[h][:CHUNK] for h, n in enumerate(idx)]
        vb = [v_new[h].astype(BF16) for h in range(B_HEADS)]
        o_prev = [ws[h][CHUNK:] + _dot(qk[n], vb[h]) for h, n in enumerate(idx)]
        s_cur = [s_cur[h] * eg_last[n] + _dotg(kd[n], vb[h], TN_DIMS) for h, n in enumerate(idx)]
        yield
    for h in range(B_HEADS):
        s_scr[h] = s_cur[h]
    gate_store(nc - 1, o_prev)
    yield


def _ab_layer_kernel(x_ref, g_ref, w_ref, wg_ref, wgt_ref, sink_ref, hist_ref, convw_ref, gprm_ref,
                     gprmt_ref, onorm_ref, wout_ref, convin_ref, sin_ref,
                     y_ref, kvtail_ref, xbtail_ref, sout_ref,
                     s_scr, xc_scr, kvprev_scr, mix_scr, *, ct, n_lead, pos_shift):
    i = pl.program_id(1)
    nsteps = pl.num_programs(1)
    ntail = min(WINDOW, ct)

    @pl.when(i == 0)
    def _():
        s_scr[...] = sin_ref[0]
        xc_scr[0:SUBLANES, :] = convin_ref[0]
        kvprev_scr[1] = hist_ref[...]

    x = x_ref[0]
    hb = _rms(x, g_ref[...]).astype(BF16)
    proj = lambda a, b: _dot(hb, w_ref[:, a:b])
    env = {}

    def step(gen, n=1):
        for _ in range(n):
            next(gen, None)

    nblk = B_CONV_CH // LANES
    w = convw_ref[...]
    cvb = []

    xblk = []
    row8 = lax.broadcasted_iota(jnp.int32, (SUBLANES, LANES), 0)

    def conv_block(j):
        cs = slice(j * LANES, (j + 1) * LANES)
        xj = xblk[j]
        h8 = xc_scr[0:SUBLANES, cs]
        yc = xj * w[CONV_W - 1:CONV_W, cs]
        for k in range(1, CONV_W):
            sh = pltpu.roll(xj, k, 0)
            top = jnp.where(row8 < k, pltpu.roll(h8, k, 0), sh[0:SUBLANES])
            yc = yc + jnp.concatenate([top, sh[SUBLANES:]], axis=0) * w[CONV_W - 1 - k:CONV_W - k, cs]
        cvb.append(_silu(yc))
        xc_scr[0:SUBLANES, cs] = xj[ct - SUBLANES:]

    pw = MXU_WIDTH
    for j in range(nblk // 2):
        xbj = proj(O_XB + j * pw, O_XB + (j + 1) * pw)
        xblk += [xbj[:, 0:LANES], xbj[:, LANES:]]
        xbtail_ref[0, :, j * pw:(j + 1) * pw] = xbj[ct - SUBLANES:]
        if j > 0:
            conv_block(2 * j - 2)
            conv_block(2 * j - 1)
    gt = _dot(hb, wg_ref[...])
    gtt = _dotg(wgt_ref[...], hb, NT_DIMS)
    conv_block(nblk - 2)
    conv_block(nblk - 1)

    gdn = _gdn_stages(i, cvb, gt, gtt, env, gprm_ref, gprmt_ref, onorm_ref, s_scr, mix_scr, ct=ct, n_lead=n_lead)
    qa_p = []
    for j in range(A_WIDTH // pw):
        qa_p.append(proj(O_QA + j * pw, O_QA + (j + 1) * pw))
        step(gdn, 2)
    qa = jnp.concatenate(qa_p, axis=-1)
    kv = proj(O_KV, O_ZA)
    kvtail_ref[0] = kv[ct - ntail:]
    slot = lax.rem(i, 2)
    kvb = jnp.concatenate([kvprev_scr[1 - slot], kv], axis=0).astype(BF16)
    if ct >= WINDOW:
        kvprev_scr[slot] = kv[ct - WINDOW:]
    swa = _swa_stages(i, qa, kvb, env, sink_ref, ct=ct, sb=ntail, pos_shift=pos_shift)
    n_sw = (ct // ntail) * A_KV_HEADS
    step(gdn)
    for _ in range(ct // CHUNK):
        step(gdn)
        step(swa)
    step(swa, max(n_sw - ct // CHUNK, 0))
    za_p, zb_p = [], []
    fill = ([lambda j=j: za_p.append(proj(O_ZA + j * pw, O_ZA + (j + 1) * pw)) for j in range(A_WIDTH // pw)]
            + [lambda j=j: zb_p.append(proj(O_ZB + j * pw, O_ZB + (j + 1) * pw)) for j in range(B_VAL // pw)])
    for lv in range(10):
        step(gdn)
        if lv % 2 == 1 and fill:
            fill.pop(0)()
    for f in fill:
        f()
    env['za'] = jnp.concatenate(za_p, axis=-1)
    env['zb'] = jnp.concatenate(zb_p, axis=-1)
    step(gdn, ct // CHUNK)
    for _ in range(2 * (ct // CHUNK)):
        step(gdn)
        step(swa, -(-2 * n_sw // (2 * (ct // CHUNK))))
    for _ in swa:
        pass
    _swa_store(env, mix_scr, ct=ct, sb=ntail)
    ya = x + _dot(mix_scr[:, 0:A_WIDTH], wout_ref[0:A_WIDTH, :])
    for _ in gdn:
        pass
    y_ref[0] = ya + _dot(mix_scr[:, A_WIDTH:], wout_ref[A_WIDTH:, :])

    @pl.when(i == nsteps - 1)
    def _():
        sout_ref[0] = s_scr[...]


def _ab_layer_prompt(x3, norm_g, w_main, w_gate, w_gate_t, sink, hist_kv, conv_w8, gprm, gprm_t, onorm,
                     w_out, conv_in, s_in, ct, n_lead, pos_shift):
    B, L, _ = x3.shape
    ntail = min(WINDOW, ct)
    kern = functools.partial(_ab_layer_kernel, ct=ct, n_lead=n_lead, pos_shift=pos_shift)
    full = lambda a: pl.BlockSpec(a.shape, lambda b, i: (0,) * a.ndim, pipeline_mode=pl.Buffered(1))
    blk = pl.BlockSpec((1, ct, D_MODEL), lambda b, i: (b, i, 0))
    per_b = lambda *s: pl.BlockSpec((1,) + s, lambda b, i: (b,) + (0,) * len(s))
    return pl.pallas_call(
        kern,
        grid=(B, L // ct),
        in_specs=[blk, full(norm_g), full(w_main), full(w_gate), full(w_gate_t), full(sink), full(hist_kv),
                  full(conv_w8), full(gprm), full(gprm_t), full(onorm), full(w_out),
                  full(conv_in), full(s_in)],
        out_specs=[blk, per_b(ntail, 2 * A_KV_WIDTH), per_b(SUBLANES, B_CONV_CH), per_b(B_HEADS, B_DK, B_DV)],
        out_shape=[jax.ShapeDtypeStruct((B, L, D_MODEL), F32),
                   jax.ShapeDtypeStruct((B, ntail, 2 * A_KV_WIDTH), F32),
                   jax.ShapeDtypeStruct((B, SUBLANES, B_CONV_CH), F32),
                   jax.ShapeDtypeStruct((B, B_HEADS, B_DK, B_DV), F32)],
        scratch_shapes=[pltpu.VMEM((B_HEADS, B_DK, B_DV), F32),
                        pltpu.VMEM((SUBLANES, B_CONV_CH), F32),
                        pltpu.VMEM((2, WINDOW, 2 * A_KV_WIDTH), F32),
                        pltpu.VMEM((ct, AB_MIX), BF16)],
        compiler_params=_cparams(("arbitrary", "arbitrary")),
        name="ab_layer_prompt",
    )(x3, norm_g, w_main, w_gate, w_gate_t, sink, hist_kv, conv_w8, gprm, gprm_t, onorm, w_out, conv_in, s_in)


def _c_layer_kernel(x_ref, g_ref, w_ref, wgk_ref, wup_ref, bgk_ref, onorm_ref, wout_ref, gfin_ref,
                    sin_ref, y_ref, sout_ref, st_scr, og_scr, *, ct, n_lead):
    i = pl.program_id(1)
    nsteps = pl.num_programs(1)

    @pl.when(i == 0)
    def _():
        st_scr[...] = sin_ref[0]

    x = x_ref[0]
    hb = _rms(x, g_ref[...]).astype(BF16)
    low = _dot(hb, wgk_ref[...])
    q_all = _dot(hb, w_ref[:, 0:C_KEY])
    la_all = _gla_log_decay(low, wup_ref, bgk_ref)
    k_all = _dot(hb, w_ref[:, C_KEY:2 * C_KEY])
    v_h, z_h = [], []
    later = ([lambda h=h: v_h.append(_dot(hb, w_ref[:, 2 * C_KEY + h * C_DV:2 * C_KEY + (h + 1) * C_DV]).astype(BF16))
              for h in range(C_HEADS)]
             + [lambda h=h: z_h.append(_dot(hb, w_ref[:, 2 * C_KEY + C_VAL + h * C_DV:
                                                       2 * C_KEY + C_VAL + (h + 1) * C_DV]))
                for h in range(C_HEADS)])

    def run_later(n):
        for _ in range(n):
            if later:
                later.pop(0)()
    rows = i * ct + lax.broadcasted_iota(jnp.int32, la_all.shape, 0)
    la_all = jnp.where(rows >= n_lead, la_all, 0.0)

    ii = lax.broadcasted_iota(jnp.int32, (CHUNK, CHUNK), 0)
    jj = lax.broadcasted_iota(jnp.int32, (CHUNK, CHUNK), 1)
    lower = ii >= jj
    ltri = lower.astype(BF16)
    nc = ct // CHUNK
    rs = lambda c: slice(c * CHUNK, (c + 1) * CHUNK)
    ks = lambda h: slice(h * C_DK, (h + 1) * C_DK)
    vs = lambda h: slice(h * C_DV, (h + 1) * C_DV)
    items = [(c, h) for c in range(nc) for h in range(C_HEADS)]
    bc_all = [sum(_dot(ltri, p) for p in _split2(la_all[rs(c)])) for c in range(nc)]
    run_later(2)
    every = max(nc // C_HEADS, 1)
    qd, kinv, kd, gl, qk = [], [], [], [], []
    for n, (c, h) in enumerate(items):
        bc = bc_all[c][:, ks(h)]
        k = k_all[rs(c), ks(h)]
        bl = bc[CHUNK - 1:CHUNK, :]
        qd.append((q_all[rs(c), ks(h)] * (C_DK ** -0.5) * jnp.exp(bc)).astype(BF16))
        kinv.append((k * jnp.exp(-bc)).astype(BF16))
        kd.append((k * jnp.exp(bl - bc)).astype(BF16))
        gl.append(jnp.exp(bl))
        if h == C_HEADS - 1:
            qk += [jnp.where(lower, _dotg(qd[m], kinv[m], NT_DIMS), 0.0).astype(BF16)
                   for m in range(n - C_HEADS + 1, n + 1)]
            if c % every == every - 1:
                run_later(1)
    run_later(C_HEADS - len(v_h))
    v_ = [v_h[h][rs(c)] for c, h in items]
    o_intra = []
    for c in range(nc):
        o_intra += [_dot(qk[c * C_HEADS + h], v_[c * C_HEADS + h]) for h in range(C_HEADS)]
        if c % every == every - 1:
            run_later(1)
    run_later(len(later))

    rb = min(ct, 4 * CHUNK)

    def gate_store(c, o):
        for h in range(C_HEADS):
            og_scr[rs(c), vs(h)] = (_rms(o[h], onorm_ref[...]) * _silu(z_h[h][rs(c)])).astype(BF16)
        if ((c + 1) * CHUNK) % rb == 0:
            r = slice((c + 1) * CHUNK - rb, (c + 1) * CHUNK)
            y_ref[0, r, :] = _rms(x[r] + _dot(og_scr[r, :], wout_ref[...]), gfin_ref[...])

    st = [st_scr[h] for h in range(C_HEADS)]
    o_prev = None
    pad_rows = jnp.zeros((SUBLANES - C_HEADS, C_DK), F32)
    for c in range(nc):
        idx = [c * C_HEADS + h for h in range(C_HEADS)]
        glt = jnp.concatenate([gl[n] for n in idx] + [pad_rows], axis=0).T
        o = [o_intra[n] + _dot(qd[n], st[h].astype(BF16)) for h, n in enumerate(idx)]
        st = [st[h] * glt[:, h:h + 1] + _dotg(kd[n], v_[n], TN_DIMS) for h, n in enumerate(idx)]
        if o_prev is not None:
            gate_store(c - 1, o_prev)
        o_prev = o
    gate_store(nc - 1, o_prev)
    for h in range(C_HEADS):
        st_scr[h] = st[h]

    @pl.when(i == nsteps - 1)
    def _():
        sout_ref[0] = st_scr[...]


def _c_layer_prompt(x3, norm_g, w_main, w_gk, w_up, b_gk, onorm, w_out, final_g, st_in, ct, n_lead):
    B, L, _ = x3.shape
    kern = functools.partial(_c_layer_kernel, ct=ct, n_lead=n_lead)
    full = lambda a: pl.BlockSpec(a.shape, lambda b, i: (0,) * a.ndim, pipeline_mode=pl.Buffered(1))
    blk = pl.BlockSpec((1, ct, D_MODEL), lambda b, i: (b, i, 0))
    st = pl.BlockSpec((1, C_HEADS, C_DK, C_DV), lambda b, i: (b, 0, 0, 0))
    return pl.pallas_call(
        kern,
        grid=(B, L // ct),
        in_specs=[blk, full(norm_g), full(w_main), full(w_gk), full(w_up), full(b_gk), full(onorm),
                  full(w_out), full(final_g), full(st_in)],
        out_specs=[blk, st],
        out_shape=[jax.ShapeDtypeStruct((B, L, D_MODEL), F32),
                   jax.ShapeDtypeStruct((B, C_HEADS, C_DK, C_DV), F32)],
        scratch_shapes=[pltpu.VMEM((C_HEADS, C_DK, C_DV), F32), pltpu.VMEM((ct, C_VAL), BF16)],
        compiler_params=_cparams(("arbitrary", "arbitrary")),
        name="c_layer_prompt",
    )(x3, norm_g, w_main, w_gk, w_up, b_gk, onorm, w_out, final_g, st_in)


def _ab_proj_kernel(x_ref, g_ref, w_ref, wg_ref, qa_ref, kv_ref, za_ref, xb_ref, zb_ref, gates_ref):
    hb = _rms(x_ref[...], g_ref[...]).astype(BF16)
    qa_ref[...] = _dot(hb, w_ref[:, O_QA:O_KV])
    kv_ref[...] = _dot(hb, w_ref[:, O_KV:O_ZA])
    za_ref[...] = _dot(hb, w_ref[:, O_ZA:O_XB])
    xb_ref[...] = _dot(hb, w_ref[:, O_XB:O_ZB])
    zb_ref[...] = _dot(hb, w_ref[:, O_ZB:AB_MAIN])
    gates_ref[...] = _dot(hb, wg_ref[...])


def _ab_proj(x2d, norm_g, w_main, w_gate):
    n = x2d.shape[0]
    full = lambda a: pl.BlockSpec(a.shape, lambda: (0,) * a.ndim)
    widths = (A_WIDTH, 2 * A_KV_WIDTH, A_WIDTH, B_CONV_CH, B_VAL, LANES)
    args = (x2d, norm_g, w_main, w_gate)
    return pl.pallas_call(
        _ab_proj_kernel,
        in_specs=[full(a) for a in args],
        out_specs=[pl.BlockSpec((n, w), lambda: (0, 0)) for w in widths],
        out_shape=[jax.ShapeDtypeStruct((n, w), F32) for w in widths],
        compiler_params=pltpu.CompilerParams(vmem_limit_bytes=VMEM_LIMIT),
        name="ab_in_proj_sample",
    )(*args)


def _swa_sample_kernel(q_ref, kvn_ref, za_ref, ck_ref, cv_ref, sink_ref, o_ref, nk_ref, nv_ref, *, bt):
    row = lax.broadcasted_iota(jnp.int32, (WINDOW, A_KV_WIDTH), 0)
    hrow = lax.broadcasted_iota(jnp.int32, (A_HEADS, A_HD), 0)
    sk = sink_ref[...]
    nkb, nvb = [], []
    for b in range(bt):
        nk = jnp.where(row == WINDOW - 1, kvn_ref[b:b + 1, 0:A_KV_WIDTH], pltpu.roll(ck_ref[b], WINDOW - 1, 0))
        nv = jnp.where(row == WINDOW - 1, kvn_ref[b:b + 1, A_KV_WIDTH:], pltpu.roll(cv_ref[b], WINDOW - 1, 0))
        nk_ref[b] = nk
        nv_ref[b] = nv
        nkb.append(nk.astype(BF16))
        nvb.append(nv.astype(BF16))
    items = [(b, j) for b in range(bt) for j in range(A_KV_HEADS)]
    q8 = [q_ref[b].astype(BF16) for b in range(bt)]
    sc = [_dotg(q8[b], nkb[b][:, j * A_HD:(j + 1) * A_HD], NT_DIMS) * (A_HD ** -0.5) for b, j in items]
    m = [jnp.maximum(jnp.max(s, axis=-1, keepdims=True), sk) for s in sc]
    p = [jnp.exp(sc[n] - m[n]) for n in range(len(items))]
    den = [jnp.sum(p[n], axis=-1, keepdims=True) + jnp.exp(sk - m[n]) for n in range(len(items))]
    oj = [_dot(p[n].astype(BF16), nvb[b][:, j * A_HD:(j + 1) * A_HD]) / den[n] for n, (b, j) in enumerate(items)]
    for b in range(bt):
        o8 = jnp.where(hrow >= A_GROUP, oj[b * A_KV_HEADS + 1], oj[b * A_KV_HEADS])
        o_ref[b] = o8 * _silu(za_ref[b])


def _swa_sample(q3, kv_new, za3, cache_k, cache_v, sink_col, bt):
    nb = q3.shape[0]
    hd = pl.BlockSpec((bt, A_HEADS, A_HD), lambda i: (i, 0, 0))
    cache = pl.BlockSpec((bt, WINDOW, A_KV_WIDTH), lambda i: (i, 0, 0))
    return pl.pallas_call(
        functools.partial(_swa_sample_kernel, bt=bt),
        grid=(nb // bt,),
        in_specs=[hd, pl.BlockSpec((bt, 2 * A_KV_WIDTH), lambda i: (i, 0)), hd, cache, cache,
                  pl.BlockSpec(sink_col.shape, lambda i: (0, 0))],
        out_specs=[hd, cache, cache],
        out_shape=[jax.ShapeDtypeStruct(q3.shape, F32),
                   jax.ShapeDtypeStruct(cache_k.shape, F32),
                   jax.ShapeDtypeStruct(cache_v.shape, F32)],
        compiler_params=_cparams(("arbitrary",)),
        name="swa_sample",
    )(q3, kv_new, za3, cache_k, cache_v, sink_col)


def _gdn_sample_prep_kernel(xb_ref, h0_ref, h1_ref, h2_ref, gates_ref, convw_ref, gprm_ref,
                            q_ref, k_ref, v_ref, bg_ref):
    w = convw_ref[...]
    y = (h0_ref[...] * w[0:1, :] + h1_ref[...] * w[1:2, :] + h2_ref[...] * w[2:3, :]
         + xb_ref[...] * w[3:4, :])
    cv = _silu(y)
    for h in range(B_HEADS):
        q = cv[:, h * B_DK:(h + 1) * B_DK]
        k = cv[:, B_KEY + h * B_DK:B_KEY + (h + 1) * B_DK]
        q_ref[:, h * B_DK:(h + 1) * B_DK] = (
            q * lax.rsqrt(jnp.sum(q * q, axis=-1, keepdims=True) + EPS) * (B_DK ** -0.5))
        k_ref[:, h * B_DK:(h + 1) * B_DK] = k * lax.rsqrt(jnp.sum(k * k, axis=-1, keepdims=True) + EPS)
    v_ref[...] = cv[:, 2 * B_KEY:]
    gt = gates_ref[...]
    col = lax.broadcasted_iota(jnp.int32, gt.shape, 1)
    g = -gprm_ref[0:1, :] * _softplus(gt + gprm_ref[1:2, :])
    bg_ref[...] = jnp.where(col < B_HEADS, _sigmoid(gt), jnp.exp(g))


def _gdn_sample_prep(xb, h0, h1, h2, gates, conv_w8, gprm):
    n = xb.shape[0]
    full = lambda a: pl.BlockSpec(a.shape, lambda: (0,) * a.ndim)
    args = (xb, h0, h1, h2, gates, conv_w8, gprm)
    return pl.pallas_call(
        _gdn_sample_prep_kernel,
        in_specs=[full(a) for a in args],
        out_specs=[pl.BlockSpec((n, B_KEY), lambda: (0, 0))] * 3 + [pl.BlockSpec((n, LANES), lambda: (0, 0))],
        out_shape=[jax.ShapeDtypeStruct((n, B_KEY), F32)] * 3 + [jax.ShapeDtypeStruct((n, LANES), F32)],
        compiler_params=pltpu.CompilerParams(vmem_limit_bytes=VMEM_LIMIT),
        name="gdn_sample_prep",
    )(*args)


def _gdn_sample_kernel(s_ref, qt_ref, kt_ref, v_ref, bg_ref, zb_ref, onorm_ref, o_ref, sout_ref, *, bt):
    items = [(b, h) for b in range(bt) for h in range(B_HEADS)]
    hs = lambda h: slice(h * B_DV, (h + 1) * B_DV)
    kt = [kt_ref[:, hs(h)].T for h in range(B_HEADS)]
    qg = [[qt_ref[g * SUBLANES:(g + 1) * SUBLANES, hs(h)].astype(BF16) for g in range(bt // SUBLANES)]
          for h in range(B_HEADS)]
    kc = [kt[h][:, b:b + 1] for b, h in items]
    eg = [bg_ref[b:b + 1, B_HEADS + h:B_HEADS + h + 1] for b, h in items]
    ks = [jnp.sum(kc[n] * s_ref[b, h], axis=0, keepdims=True) for n, (b, h) in enumerate(items)]
    v_new = [bg_ref[b:b + 1, h:h + 1] * (v_ref[b:b + 1, hs(h)] - eg[n] * ks[n]) for n, (b, h) in enumerate(items)]
    o = []
    for n, (b, h) in enumerate(items):
        s_new = eg[n] * s_ref[b, h] + kc[n] * v_new[n]
        sout_ref[b, h] = s_new
        o8 = _dot(qg[h][b // SUBLANES], s_new.astype(BF16))
        o.append(o8[b % SUBLANES:b % SUBLANES + 1])
    o_blk = jnp.concatenate([jnp.concatenate([o[b * B_HEADS + h] for h in range(B_HEADS)], axis=-1)
                             for b in range(bt)], axis=0)
    for h in range(B_HEADS):
        o_ref[:, hs(h)] = _rms(o_blk[:, hs(h)], onorm_ref[...]) * _silu(zb_ref[:, hs(h)])


def _gdn_sample(state, qt, kt, v, bg, zb, onorm, bt):
    nb = state.shape[0]
    st = pl.BlockSpec((bt, B_HEADS, B_DK, B_DV), lambda i: (i, 0, 0, 0))
    row = lambda n: pl.BlockSpec((bt, n), lambda i: (i, 0))
    return pl.pallas_call(
        functools.partial(_gdn_sample_kernel, bt=bt),
        grid=(nb // bt,),
        in_specs=[st, row(B_KEY), row(B_KEY), row(B_VAL), row(LANES), row(B_VAL),
                  pl.BlockSpec(onorm.shape, lambda i: (0, 0))],
        out_specs=[row(B_VAL), st],
        out_shape=[jax.ShapeDtypeStruct((nb, B_VAL), F32), jax.ShapeDtypeStruct(state.shape, F32)],
        compiler_params=_cparams(("arbitrary",)),
        name="gdn_sample",
    )(state, qt, kt, v, bg, zb, onorm)


def _ab_out_c_proj_kernel(x_ref, oa_ref, ob_ref, wout_ref, g_ref, w_ref, wgk_ref, wup_ref, bgk_ref,
                          y_ref, q_ref, k_ref, v_ref, z_ref, la_ref):
    y = (x_ref[...] + _dot(oa_ref[...].astype(BF16), wout_ref[0:A_WIDTH, :])
         + _dot(ob_ref[...].astype(BF16), wout_ref[A_WIDTH:, :]))
    y_ref[...] = y
    hb = _rms(y, g_ref[...]).astype(BF16)
    q_ref[...] = _dot(hb, w_ref[:, 0:C_KEY])
    k_ref[...] = _dot(hb, w_ref[:, C_KEY:2 * C_KEY])
    v_ref[...] = _dot(hb, w_ref[:, 2 * C_KEY:2 * C_KEY + C_VAL])
    z_ref[...] = _dot(hb, w_ref[:, 2 * C_KEY + C_VAL:C_MAIN])
    la_ref[...] = _gla_log_decay(_dot(hb, wgk_ref[...]), wup_ref, bgk_ref)


def _ab_out_c_proj(x2d, oa, ob, w_out, norm_g, w_main, w_gk, w_up, b_gk):
    n = x2d.shape[0]
    full = lambda a: pl.BlockSpec(a.shape, lambda: (0,) * a.ndim)
    widths = (D_MODEL, C_KEY, C_KEY, C_VAL, C_VAL, C_KEY)
    args = (x2d, oa, ob, w_out, norm_g, w_main, w_gk, w_up, b_gk)
    return pl.pallas_call(
        _ab_out_c_proj_kernel,
        in_specs=[full(a) for a in args],
        out_specs=[pl.BlockSpec((n, w), lambda: (0, 0)) for w in widths],
        out_shape=[jax.ShapeDtypeStruct((n, w), F32) for w in widths],
        compiler_params=pltpu.CompilerParams(vmem_limit_bytes=VMEM_LIMIT),
        name="ab_out_c_in_proj_sample",
    )(*args)


def _gla_sample_kernel(s_ref, qt_ref, kt_ref, lat_ref, v_ref, z_ref, onorm_ref, o_ref, sout_ref, *, bt):
    items = [(b, h) for b in range(bt) for h in range(C_HEADS)]
    hs = lambda h: slice(h * C_DV, (h + 1) * C_DV)
    ks = lambda h: slice(h * C_DK, (h + 1) * C_DK)
    at = [jnp.exp(lat_ref[:, ks(h)]).T for h in range(C_HEADS)]
    kt = [kt_ref[:, ks(h)].T for h in range(C_HEADS)]
    qg = [[(qt_ref[g * SUBLANES:(g + 1) * SUBLANES, ks(h)] * (C_DK ** -0.5)).astype(BF16)
           for g in range(bt // SUBLANES)] for h in range(C_HEADS)]
    o = []
    for n, (b, h) in enumerate(items):
        s_new = at[h][:, b:b + 1] * s_ref[b, h] + kt[h][:, b:b + 1] * v_ref[b:b + 1, hs(h)]
        sout_ref[b, h] = s_new
        o8 = _dot(qg[h][b // SUBLANES], s_new.astype(BF16))
        o.append(o8[b % SUBLANES:b % SUBLANES + 1])
    o_blk = jnp.concatenate([jnp.concatenate([o[b * C_HEADS + h] for h in range(C_HEADS)], axis=-1)
                             for b in range(bt)], axis=0)
    for h in range(C_HEADS):
        o_ref[:, hs(h)] = _rms(o_blk[:, hs(h)], onorm_ref[...]) * _silu(z_ref[:, hs(h)])


def _gla_sample(state, qt, kt, lat, v, z, onorm, bt):
    nb = state.shape[0]
    st = pl.BlockSpec((bt, C_HEADS, C_DK, C_DV), lambda i: (i, 0, 0, 0))
    row = lambda n: pl.BlockSpec((bt, n), lambda i: (i, 0))
    return pl.pallas_call(
        functools.partial(_gla_sample_kernel, bt=bt),
        grid=(nb // bt,),
        in_specs=[st, row(C_KEY), row(C_KEY), row(C_KEY), row(C_VAL), row(C_VAL),
                  pl.BlockSpec(onorm.shape, lambda i: (0, 0))],
        out_specs=[row(C_VAL), st],
        out_shape=[jax.ShapeDtypeStruct((nb, C_VAL), F32), jax.ShapeDtypeStruct(state.shape, F32)],
        compiler_params=_cparams(("arbitrary",)),
        name="gla_sample",
    )(state, qt, kt, lat, v, z, onorm)


def _c_out_kernel(x_ref, o_ref, w_ref, g_ref, y_ref):
    y_ref[...] = _rms(x_ref[...] + _dot(o_ref[...].astype(BF16), w_ref[...]), g_ref[...])


def _c_out(x2d, oc, w_out, final_g):
    n = x2d.shape[0]
    full = lambda a: pl.BlockSpec(a.shape, lambda: (0,) * a.ndim)
    args = (x2d, oc, w_out, final_g)
    return pl.pallas_call(
        _c_out_kernel,
        in_specs=[full(a) for a in args],
        out_specs=pl.BlockSpec((n, D_MODEL), lambda: (0, 0)),
        out_shape=jax.ShapeDtypeStruct((n, D_MODEL), F32),
        compiler_params=pltpu.CompilerParams(vmem_limit_bytes=VMEM_LIMIT),
        name="c_out_proj_norm_sample",
    )(*args)


def kernel(x_prompt, x_sample, cache_swa_k, cache_swa_v, state_dn_conv, state_dn, state_gla,
           meta_tokens, norm_ab, w_in_ab, sink_a, conv_b, a_log_b, dt_bias_b, onorm_b, w_out_ab,
           norm_c, w_in_c, w_gk_up, b_gk, onorm_c, w_out_c, final_norm):
    Bp, L, _ = x_prompt.shape
    Bs = x_sample.shape[0]
    assert L % PROMPT_TILE == 0 and Bs % SAMPLE_BLOCK == 0

    w_ab = w_in_ab[0]
    w_ab_main = w_ab.astype(BF16)
    w_ab_gate = jnp.pad(w_ab[:, AB_MAIN:], ((0, 0), (0, LANES - 2 * B_HEADS))).astype(BF16)
    w_ab_gate_t = jnp.pad(w_ab[:, AB_MAIN:].T, ((0, BF16_ROWS - 2 * B_HEADS), (0, 0))).astype(BF16)
    g_ab = norm_ab[0][None, :]
    sink_row = sink_a[0][None, :]
    sink_col = sink_a[0][:, None]
    conv_w8 = jnp.pad(conv_b[0], ((0, SUBLANES - CONV_W), (0, 0)))
    gp = jnp.stack([jnp.exp(a_log_b[0]), dt_bias_b[0]])
    gprm = jnp.pad(gp, ((0, SUBLANES - 2), (B_HEADS, LANES - 2 * B_HEADS)))
    gprm_t = jnp.pad(gp.T, ((B_HEADS, BF16_ROWS - 2 * B_HEADS), (0, LANES - 2)))
    onb = onorm_b[0][None, :]
    w_ab_out = w_out_ab[0].astype(BF16)
    w_c = w_in_c[0]
    w_c_main = w_c.astype(BF16)
    w_c_gk = jnp.pad(w_c[:, C_MAIN:], ((0, 0), (0, LANES - C_RANK))).astype(BF16)
    w_up = jnp.pad(w_gk_up[0], ((0, LANES - C_RANK), (0, 0)))
    bgk = b_gk[0][None, :]
    g_c = norm_c[0][None, :]
    onc = onorm_c[0][None, :]
    w_c_out = w_out_c[0].astype(BF16)
    g_fin = final_norm[None, :]

    def ab_layer(x3, ct, hist_kv, pos_shift, conv_in, s_in, n_lead):
        return _ab_layer_prompt(x3, g_ab, w_ab_main, w_ab_gate, w_ab_gate_t, sink_row, hist_kv, conv_w8,
                                gprm, gprm_t, onb, w_ab_out, conv_in, s_in, ct, n_lead, pos_shift)

    def c_layer(x3, ct, st_in, n_lead):
        return _c_layer_prompt(x3, g_c, w_c_main, w_c_gk, w_up, bgk, onc, w_c_out, g_fin, st_in, ct, n_lead)

    xpre = jnp.concatenate([jnp.zeros((LEAD, D_MODEL), F32), meta_tokens], axis=0)[None]
    y_pre, kv_pre, xbtail_pre, sdn_pre = ab_layer(
        xpre, CHUNK, jnp.zeros((WINDOW, 2 * A_KV_WIDTH), F32), -LEAD,
        jnp.zeros((1, SUBLANES, B_CONV_CH), F32), jnp.zeros((1, B_HEADS, B_DK, B_DV), F32), LEAD)
    _, sgla_pre = c_layer(y_pre, CHUNK, jnp.zeros((1, C_HEADS, C_DK, C_DV), F32), LEAD)

    hist_kv = jnp.concatenate([jnp.zeros((WINDOW - CHUNK, 2 * A_KV_WIDTH), F32), kv_pre[0]], axis=0)
    y1, kv_tail, xb_tail, sdn_p = ab_layer(
        x_prompt, PROMPT_TILE, hist_kv, N_META, xbtail_pre, sdn_pre, 0)
    y_prompt, sgla_p = c_layer(y1, 2 * PROMPT_TILE, sgla_pre, 0)

    swa_k_p = kv_tail[:, :, :A_KV_WIDTH].reshape(1, Bp, WINDOW, A_KV_HEADS, A_HD)
    swa_v_p = kv_tail[:, :, A_KV_WIDTH:].reshape(1, Bp, WINDOW, A_KV_HEADS, A_HD)
    dn_conv_p = xb_tail[:, SUBLANES - (CONV_W - 1):, :][None]

    xs = x_sample.reshape(Bs, D_MODEL)
    qa, kvn, za, xb, zb, gates = _ab_proj(xs, g_ab, w_ab_main, w_ab_gate)
    oa3, nk, nv = _swa_sample(qa.reshape(Bs, A_HEADS, A_HD), kvn, za.reshape(Bs, A_HEADS, A_HD),
                              cache_swa_k[0].reshape(Bs, WINDOW, A_KV_WIDTH),
                              cache_swa_v[0].reshape(Bs, WINDOW, A_KV_WIDTH), sink_col, 2 * SAMPLE_BLOCK)
    hist = state_dn_conv[0]
    qn, kn, vn, bg = _gdn_sample_prep(xb, hist[:, 0], hist[:, 1], hist[:, 2], gates, conv_w8, gprm)
    ob, sdn_s = _gdn_sample(state_dn[0], qn, kn, vn, bg, zb, onb, 2 * SAMPLE_BLOCK)
    ys, qc, kc, vc, zc, la = _ab_out_c_proj(xs, oa3.reshape(Bs, A_WIDTH), ob, w_ab_out,
                                            g_c, w_c_main, w_c_gk, w_up, bgk)
    ocs, sgla_s = _gla_sample(state_gla[0], qc, kc, la, vc, zc, onc, SAMPLE_BLOCK)
    y_sample = _c_out(ys, ocs, w_c_out, g_fin).reshape(Bs, 1, D_MODEL)
    dn_conv_s = jnp.concatenate([hist[:, 1:], xb[:, None, :]], axis=1)[None]

    return (y_prompt, y_sample, swa_k_p, swa_v_p, dn_conv_p, sdn_p[None], sgla_p[None],
            nk.reshape(1, Bs, WINDOW, A_KV_HEADS, A_HD), nv.reshape(1, Bs, WINDOW, A_KV_HEADS, A_HD),
            dn_conv_s, sdn_s[None], sgla_s[None])
```
